```python
import math
import jax, jax.numpy as jnp
from jax import lax
import numpy as np

D_MODEL = 2048
BATCH = 8
SEQ = 4096
DEPTH = 1

MIX_WIDTH = D_MODEL
C_CONV = MIX_WIDTH // 2
FOX_HEAD_DIM = 64
FOX_HEADS = (MIX_WIDTH - C_CONV) // FOX_HEAD_DIM
FOX_WIDTH = FOX_HEADS * FOX_HEAD_DIM
CONV_WIDTH = 31
Q_BLOCK = 128
D_FF = 5632
IN_COLS = 2 * C_CONV + 3 * FOX_WIDTH + FOX_HEADS
NORM_EPS = 1e-6
LN_EPS = 1e-5
NEG_INF = -1e30

kernel_name = "hymba_conformer_fox_macaron"


def rms_norm(x, g):
    xf = x.astype(jnp.float32)
    y = xf * lax.rsqrt(jnp.mean(xf * xf, axis=-1, keepdims=True) + NORM_EPS)
    return (y * g.astype(jnp.float32)).astype(x.dtype)


def layer_norm(x, g, b):
    xf = x.astype(jnp.float32)
    mu = jnp.mean(xf, axis=-1, keepdims=True)
    var = jnp.mean(jnp.square(xf - mu), axis=-1, keepdims=True)
    y = (xf - mu) * lax.rsqrt(var + LN_EPS)
    return (y * g.astype(jnp.float32) + b.astype(jnp.float32)).astype(x.dtype)


def swiglu(h, w_gate, w_up, w_down):
    return (jax.nn.silu(h @ w_gate) * (h @ w_up)) @ w_down


def conformer_conv_group(u_in, conv_w, conv_b, ln_g, ln_b):
    a, g = jnp.split(u_in, 2, axis=-1)
    u = a * jax.nn.sigmoid(g)
    y = lax.conv_general_dilated(
        u, conv_w[:, None, :].astype(u.dtype),
        window_strides=(1,),
        padding=[(CONV_WIDTH - 1, 0)],
        dimension_numbers=("NWC", "WIO", "NWC"),
        feature_group_count=C_CONV,
    ) + conv_b
    y = layer_norm(y, ln_g, ln_b)
    return jax.nn.silu(y)


def forgetting_attention(q, k, v, f_logit):
    B, S, _ = q.shape
    H, dh = FOX_HEADS, FOX_HEAD_DIM
    q = q.reshape(B, S, H, dh).transpose(0, 2, 1, 3)
    k = k.reshape(B, S, H, dh).transpose(0, 2, 1, 3)
    v = v.reshape(B, S, H, dh).transpose(0, 2, 1, 3)
    log_f = jax.nn.log_sigmoid(f_logit.astype(jnp.float32)).transpose(0, 2, 1)
    c = jnp.cumsum(log_f, axis=-1)
    scale = 1.0 / math.sqrt(dh)
    nb = S // Q_BLOCK
    q_blocks = q.reshape(B, H, nb, Q_BLOCK, dh).transpose(2, 0, 1, 3, 4)
    c_blocks = c.reshape(B, H, nb, Q_BLOCK).transpose(2, 0, 1, 3)
    pos_blocks = jnp.arange(S, dtype=jnp.int32).reshape(nb, Q_BLOCK)
    k_pos = jnp.arange(S, dtype=jnp.int32)

    def one_block(args):
        qb, cqb, pos_b = args
        s = jnp.einsum("bhqd,bhkd->bhqk", qb, k).astype(jnp.float32) * scale
        s = s + cqb[..., :, None] - c[:, :, None, :]
        causal = pos_b[:, None] >= k_pos[None, :]
        s = jnp.where(causal[None, None], s, NEG_INF)
        p = jax.nn.softmax(s, axis=-1)
        return jnp.einsum("bhqk,bhkd->bhqd", p.astype(v.dtype), v)

    out = lax.map(one_block, (q_blocks, c_blocks, pos_blocks))
    return out.transpose(1, 0, 3, 2, 4).reshape(B, S, H * dh)


def _fwd_setup_inputs(seed: int = 0) -> dict:
    key = jax.random.key(seed)
    ks = jax.random.split(key, 20)
    f32 = jnp.float32
    D, F, C = D_MODEL, D_FF, C_CONV

    def nrm(k, shape, fan_in):
        return jax.random.normal(k, shape, f32) * (fan_in ** -0.5)

    def gain(k, n):
        return 1.0 + 0.01 * jax.random.normal(k, (n,), f32)

    return {
        "x": jax.random.normal(ks[0], (BATCH, SEQ, D), f32),
        "ffn1_norm": gain(ks[1], D),
        "ffn1_w_gate": nrm(ks[2], (D, F), D),
        "ffn1_w_up": nrm(ks[3], (D, F), D),
        "ffn1_w_down": nrm(ks[4], (F, D), F),
        "mix_norm": gain(ks[5], D),
        "w_in": nrm(ks[6], (D, IN_COLS), D),
        "fgate_bias": jax.random.uniform(ks[7], (FOX_HEADS,), f32, 1.0, 3.0),
        "conv_w": nrm(ks[8], (CONV_WIDTH, C), CONV_WIDTH),
        "conv_b": 0.01 * jax.random.normal(ks[9], (C,), f32),
        "conv_ln_g": gain(ks[10], C),
        "conv_ln_b": 0.01 * jax.random.normal(ks[11], (C,), f32),
        "w_out": nrm(ks[12], (MIX_WIDTH, D), MIX_WIDTH),
        "ffn2_norm": gain(ks[13], D),
        "ffn2_w_gate": nrm(ks[14], (D, F), D),
        "ffn2_w_up": nrm(ks[15], (D, F), D),
        "ffn2_w_down": nrm(ks[16], (F, D), F),
        "final_norm": gain(ks[17], D),
    }


def _fwd_reference(x, ffn1_norm, ffn1_w_gate, ffn1_w_up, ffn1_w_down, mix_norm, w_in,
              fgate_bias, conv_w, conv_b, conv_ln_g, conv_ln_b, w_out,
              ffn2_norm, ffn2_w_gate, ffn2_w_up, ffn2_w_down, final_norm):
    for _ in range(DEPTH):
        x = x + 0.5 * swiglu(rms_norm(x, ffn1_norm), ffn1_w_gate, ffn1_w_up, ffn1_w_down)
        h = rms_norm(x, mix_norm)
        proj = h @ w_in
        o1 = 2 * C_CONV
        o2 = o1 + FOX_WIDTH
        o3 = o2 + FOX_WIDTH
        o4 = o3 + FOX_WIDTH
        u_conv = proj[..., :o1]
        q = proj[..., o1:o2]
        k = proj[..., o2:o3]
        v = proj[..., o3:o4]
        f_logit = proj[..., o4:] + fgate_bias
        y_conv = conformer_conv_group(u_conv, conv_w, conv_b, conv_ln_g, conv_ln_b)
        y_fox = forgetting_attention(q, k, v, f_logit)
        x = x + jnp.concatenate([y_conv, y_fox], axis=-1) @ w_out
        x = x + 0.5 * swiglu(rms_norm(x, ffn2_norm), ffn2_w_gate, ffn2_w_up, ffn2_w_down)
    return rms_norm(x, final_norm)


import jax as _jax
import jax.numpy as _jnp

TWIN_FORMAT = 'train_step'
FWD_PARAMS = ['x', 'ffn1_norm', 'ffn1_w_gate', 'ffn1_w_up', 'ffn1_w_down', 'mix_norm', 'w_in', 'fgate_bias', 'conv_w', 'conv_b', 'conv_ln_g', 'conv_ln_b', 'w_out', 'ffn2_norm', 'ffn2_w_gate', 'ffn2_w_up', 'ffn2_w_down', 'final_norm']
TWIN_WEIGHTS = ['ffn1_norm', 'ffn1_w_gate', 'ffn1_w_up', 'ffn1_w_down', 'mix_norm', 'w_in', 'fgate_bias', 'conv_w', 'conv_b', 'conv_ln_g', 'conv_ln_b', 'w_out', 'ffn2_norm', 'ffn2_w_gate', 'ffn2_w_up', 'ffn2_w_down', 'final_norm']
TWIN_DIFF_INPUT = 'x'
TWIN_INPUTS = ['x', 'ffn1_norm', 'ffn1_w_gate', 'ffn1_w_up', 'ffn1_w_down', 'mix_norm', 'w_in', 'fgate_bias', 'conv_w', 'conv_b', 'conv_ln_g', 'conv_ln_b', 'w_out', 'ffn2_norm', 'ffn2_w_gate', 'ffn2_w_up', 'ffn2_w_down', 'final_norm', 'loss_target', 'm_ffn1_norm', 'm_ffn1_w_gate', 'm_ffn1_w_up', 'm_ffn1_w_down', 'm_mix_norm', 'm_w_in', 'm_fgate_bias', 'm_conv_w', 'm_conv_b', 'm_conv_ln_g', 'm_conv_ln_b', 'm_w_out', 'm_ffn2_norm', 'm_ffn2_w_gate', 'm_ffn2_w_up', 'm_ffn2_w_down', 'm_final_norm', 'v_ffn1_norm', 'v_ffn1_w_gate', 'v_ffn1_w_up', 'v_ffn1_w_down', 'v_mix_norm', 'v_w_in', 'v_fgate_bias', 'v_conv_w', 'v_conv_b', 'v_conv_ln_g', 'v_conv_ln_b', 'v_w_out', 'v_ffn2_norm', 'v_ffn2_w_gate', 'v_ffn2_w_up', 'v_ffn2_w_down', 'v_final_norm']
TWIN_OUTPUTS = ['loss', 'grad_x', 'grad_ffn1_norm', 'grad_ffn1_w_gate', 'grad_ffn1_w_up', 'grad_ffn1_w_down', 'grad_mix_norm', 'grad_w_in', 'grad_fgate_bias', 'grad_conv_w', 'grad_conv_b', 'grad_conv_ln_g', 'grad_conv_ln_b', 'grad_w_out', 'grad_ffn2_norm', 'grad_ffn2_w_gate', 'grad_ffn2_w_up', 'grad_ffn2_w_down', 'grad_final_norm', 'delta_ffn1_norm', 'delta_ffn1_w_gate', 'delta_ffn1_w_up', 'delta_ffn1_w_down', 'delta_mix_norm', 'delta_w_in', 'delta_fgate_bias', 'delta_conv_w', 'delta_conv_b', 'delta_conv_ln_g', 'delta_conv_ln_b', 'delta_w_out', 'delta_ffn2_norm', 'delta_ffn2_w_gate', 'delta_ffn2_w_up', 'delta_ffn2_w_down', 'delta_final_norm', 'new_m_ffn1_norm', 'new_m_ffn1_w_gate', 'new_m_ffn1_w_up', 'new_m_ffn1_w_down', 'new_m_mix_norm', 'new_m_w_in', 'new_m_fgate_bias', 'new_m_conv_w', 'new_m_conv_b', 'new_m_conv_ln_g', 'new_m_conv_ln_b', 'new_m_w_out', 'new_m_ffn2_norm', 'new_m_ffn2_w_gate', 'new_m_ffn2_w_up', 'new_m_ffn2_w_down', 'new_m_final_norm', 'new_v_ffn1_norm', 'new_v_ffn1_w_gate', 'new_v_ffn1_w_up', 'new_v_ffn1_w_down', 'new_v_mix_norm', 'new_v_w_in', 'new_v_fgate_bias', 'new_v_conv_w', 'new_v_conv_b', 'new_v_conv_ln_g', 'new_v_conv_ln_b', 'new_v_w_out', 'new_v_ffn2_norm', 'new_v_ffn2_w_gate', 'new_v_ffn2_w_up', 'new_v_ffn2_w_down', 'new_v_final_norm']
TWIN_LEAF_KINDS = {'loss': 'loss', 'grad_x': 'grad_x', 'grad_ffn1_norm': 'grad_w', 'grad_ffn1_w_gate': 'grad_w', 'grad_ffn1_w_up': 'grad_w', 'grad_ffn1_w_down': 'grad_w', 'grad_mix_norm': 'grad_w', 'grad_w_in': 'grad_w', 'grad_fgate_bias': 'grad_w', 'grad_conv_w': 'grad_w', 'grad_conv_b': 'grad_w', 'grad_conv_ln_g': 'grad_w', 'grad_conv_ln_b': 'grad_w', 'grad_w_out': 'grad_w', 'grad_ffn2_norm': 'grad_w', 'grad_ffn2_w_gate': 'grad_w', 'grad_ffn2_w_up': 'grad_w', 'grad_ffn2_w_down': 'grad_w', 'grad_final_norm': 'grad_w', 'delta_ffn1_norm': 'delta_w', 'delta_ffn1_w_gate': 'delta_w', 'delta_ffn1_w_up': 'delta_w', 'delta_ffn1_w_down': 'delta_w', 'delta_mix_norm': 'delta_w', 'delta_w_in': 'delta_w', 'delta_fgate_bias': 'delta_w', 'delta_conv_w': 'delta_w', 'delta_conv_b': 'delta_w', 'delta_conv_ln_g': 'delta_w', 'delta_conv_ln_b': 'delta_w', 'delta_w_out': 'delta_w', 'delta_ffn2_norm': 'delta_w', 'delta_ffn2_w_gate': 'delta_w', 'delta_ffn2_w_up': 'delta_w', 'delta_ffn2_w_down': 'delta_w', 'delta_final_norm': 'delta_w', 'new_m_ffn1_norm': 'new_m', 'new_m_ffn1_w_gate': 'new_m', 'new_m_ffn1_w_up': 'new_m', 'new_m_ffn1_w_down': 'new_m', 'new_m_mix_norm': 'new_m', 'new_m_w_in': 'new_m', 'new_m_fgate_bias': 'new_m', 'new_m_conv_w': 'new_m', 'new_m_conv_b': 'new_m', 'new_m_conv_ln_g': 'new_m', 'new_m_conv_ln_b': 'new_m', 'new_m_w_out': 'new_m', 'new_m_ffn2_norm': 'new_m', 'new_m_ffn2_w_gate': 'new_m', 'new_m_ffn2_w_up': 'new_m', 'new_m_ffn2_w_down': 'new_m', 'new_m_final_norm': 'new_m', 'new_v_ffn1_norm': 'new_v', 'new_v_ffn1_w_gate': 'new_v', 'new_v_ffn1_w_up': 'new_v', 'new_v_ffn1_w_down': 'new_v', 'new_v_mix_norm': 'new_v', 'new_v_w_in': 'new_v', 'new_v_fgate_bias': 'new_v', 'new_v_conv_w': 'new_v', 'new_v_conv_b': 'new_v', 'new_v_conv_ln_g': 'new_v', 'new_v_conv_ln_b': 'new_v', 'new_v_w_out': 'new_v', 'new_v_ffn2_norm': 'new_v', 'new_v_ffn2_w_gate': 'new_v', 'new_v_ffn2_w_up': 'new_v', 'new_v_ffn2_w_down': 'new_v', 'new_v_final_norm': 'new_v'}


def _forward(args):
    return _fwd_reference(*[args[k] for k in FWD_PARAMS])


def _output_shape():
    def fwd():
        inp = _fwd_setup_inputs(0)
        return _fwd_reference(*[inp[k] for k in FWD_PARAMS])
    out = _jax.eval_shape(fwd)
    return out.shape, out.dtype

N_MICROBATCH = 1
ADAM_LR = 0.001
ADAM_B1 = 0.9
ADAM_B2 = 0.999
ADAM_EPS = 1e-08
ADAM_WD = 0.01
ADAM_STEP = 10
PER_EXAMPLE_BATCH_AXIS = {'x': 0, 'loss_target': 0}
SHARED_INPUTS = []
_WEIGHT_DTYPES = {'ffn1_norm': _jnp.float32, 'ffn1_w_gate': _jnp.float32, 'ffn1_w_up': _jnp.float32, 'ffn1_w_down': _jnp.float32, 'mix_norm': _jnp.float32, 'w_in': _jnp.float32, 'fgate_bias': _jnp.float32, 'conv_w': _jnp.float32, 'conv_b': _jnp.float32, 'conv_ln_g': _jnp.float32, 'conv_ln_b': _jnp.float32, 'w_out': _jnp.float32, 'ffn2_norm': _jnp.float32, 'ffn2_w_gate': _jnp.float32, 'ffn2_w_up': _jnp.float32, 'ffn2_w_down': _jnp.float32, 'final_norm': _jnp.float32}
MOMENT_SCALE = {'ffn1_norm': 3.997685e-02, 'ffn1_w_gate': 1.712808e-02, 'ffn1_w_up': 1.657728e-02, 'ffn1_w_down': 2.748755e-02, 'mix_norm': 5.053801e-02, 'w_in': 3.223334e-02, 'fgate_bias': 1.172900e-01, 'conv_w': 4.913564e-02, 'conv_b': 1.005786e-01, 'conv_ln_g': 5.775063e-02, 'conv_ln_b': 4.994277e-02, 'w_out': 4.097848e-02, 'ffn2_norm': 3.207200e-02, 'ffn2_w_gate': 1.367270e-02, 'ffn2_w_up': 1.324481e-02, 'ffn2_w_down': 2.195710e-02, 'final_norm': 1.598679e+01}


def _to_microbatches(a, axis):
    t = _jnp.moveaxis(a, axis, 0)
    t = t.reshape((N_MICROBATCH, t.shape[0] // N_MICROBATCH) + t.shape[1:])
    return _jnp.moveaxis(t, 1, axis + 1)


def setup_inputs(seed: int = 0) -> dict:
    inp = _fwd_setup_inputs(seed)
    key = _jax.random.fold_in(_jax.random.key(seed), 7919)
    shape, _ = _output_shape()
    out = dict(inp)
    out["loss_target"] = _jax.random.normal(_jax.random.fold_in(key, 0), shape, _jnp.float32)
    for i, name in enumerate(TWIN_WEIGHTS):
        w = inp[name].astype(_jnp.float32)
        if MOMENT_SCALE is None:
            s = _jnp.sqrt(_jnp.mean(_jnp.square(w)) + 1e-30)
        else:
            s = MOMENT_SCALE[name]
        km, kv = _jax.random.split(_jax.random.fold_in(key, i + 1))
        out[name] = w
        out["m_" + name] = s * _jax.random.normal(km, w.shape, _jnp.float32)
        out["v_" + name] = (s * s) * _jax.random.uniform(kv, w.shape, _jnp.float32, 0.5, 1.5)
    if N_MICROBATCH > 1:
        for name, axis in PER_EXAMPLE_BATCH_AXIS.items():
            out[name] = _to_microbatches(out[name], axis)
    return {'x': out['x'], 'ffn1_norm': out['ffn1_norm'], 'ffn1_w_gate': out['ffn1_w_gate'], 'ffn1_w_up': out['ffn1_w_up'], 'ffn1_w_down': out['ffn1_w_down'], 'mix_norm': out['mix_norm'], 'w_in': out['w_in'], 'fgate_bias': out['fgate_bias'], 'conv_w': out['conv_w'], 'conv_b': out['conv_b'], 'conv_ln_g': out['conv_ln_g'], 'conv_ln_b': out['conv_ln_b'], 'w_out': out['w_out'], 'ffn2_norm': out['ffn2_norm'], 'ffn2_w_gate': out['ffn2_w_gate'], 'ffn2_w_up': out['ffn2_w_up'], 'ffn2_w_down': out['ffn2_w_down'], 'final_norm': out['final_norm'], 'loss_target': out['loss_target'], 'm_ffn1_norm': out['m_ffn1_norm'], 'm_ffn1_w_gate': out['m_ffn1_w_gate'], 'm_ffn1_w_up': out['m_ffn1_w_up'], 'm_ffn1_w_down': out['m_ffn1_w_down'], 'm_mix_norm': out['m_mix_norm'], 'm_w_in': out['m_w_in'], 'm_fgate_bias': out['m_fgate_bias'], 'm_conv_w': out['m_conv_w'], 'm_conv_b': out['m_conv_b'], 'm_conv_ln_g': out['m_conv_ln_g'], 'm_conv_ln_b': out['m_conv_ln_b'], 'm_w_out': out['m_w_out'], 'm_ffn2_norm': out['m_ffn2_norm'], 'm_ffn2_w_gate': out['m_ffn2_w_gate'], 'm_ffn2_w_up': out['m_ffn2_w_up'], 'm_ffn2_w_down': out['m_ffn2_w_down'], 'm_final_norm': out['m_final_norm'], 'v_ffn1_norm': out['v_ffn1_norm'], 'v_ffn1_w_gate': out['v_ffn1_w_gate'], 'v_ffn1_w_up': out['v_ffn1_w_up'], 'v_ffn1_w_down': out['v_ffn1_w_down'], 'v_mix_norm': out['v_mix_norm'], 'v_w_in': out['v_w_in'], 'v_fgate_bias': out['v_fgate_bias'], 'v_conv_w': out['v_conv_w'], 'v_conv_b': out['v_conv_b'], 'v_conv_ln_g': out['v_conv_ln_g'], 'v_conv_ln_b': out['v_conv_ln_b'], 'v_w_out': out['v_w_out'], 'v_ffn2_norm': out['v_ffn2_norm'], 'v_ffn2_w_gate': out['v_ffn2_w_gate'], 'v_ffn2_w_up': out['v_ffn2_w_up'], 'v_ffn2_w_down': out['v_ffn2_w_down'], 'v_final_norm': out['v_final_norm']}


def _loss(weights, diff, rest, loss_target):
    with _jax.named_scope("forward"):
        args = {**rest, TWIN_DIFF_INPUT: diff, **{k: w.astype(_WEIGHT_DTYPES[k]) for k, w in weights.items()}}
        y = _forward(args)
    with _jax.named_scope("loss_head"):
        err = _jnp.square(y.astype(_jnp.float32) - loss_target)
        return 0.5 * _jnp.sum(_jnp.mean(err, axis=-1)) if err.ndim else 0.5 * err


def _adamw(w, g, m, v):
    m = ADAM_B1 * m + (1.0 - ADAM_B1) * g
    v = ADAM_B2 * v + (1.0 - ADAM_B2) * _jnp.square(g)
    m_hat = m / (1.0 - ADAM_B1 ** ADAM_STEP)
    v_hat = v / (1.0 - ADAM_B2 ** ADAM_STEP)
    delta = -ADAM_LR * (m_hat / (_jnp.sqrt(v_hat) + ADAM_EPS) + ADAM_WD * w)
    return delta, m, v


def reference(x, ffn1_norm, ffn1_w_gate, ffn1_w_up, ffn1_w_down, mix_norm, w_in, fgate_bias, conv_w, conv_b, conv_ln_g, conv_ln_b, w_out, ffn2_norm, ffn2_w_gate, ffn2_w_up, ffn2_w_down, final_norm, loss_target, m_ffn1_norm, m_ffn1_w_gate, m_ffn1_w_up, m_ffn1_w_down, m_mix_norm, m_w_in, m_fgate_bias, m_conv_w, m_conv_b, m_conv_ln_g, m_conv_ln_b, m_w_out, m_ffn2_norm, m_ffn2_w_gate, m_ffn2_w_up, m_ffn2_w_down, m_final_norm, v_ffn1_norm, v_ffn1_w_gate, v_ffn1_w_up, v_ffn1_w_down, v_mix_norm, v_w_in, v_fgate_bias, v_conv_w, v_conv_b, v_conv_ln_g, v_conv_ln_b, v_w_out, v_ffn2_norm, v_ffn2_w_gate, v_ffn2_w_up, v_ffn2_w_down, v_final_norm):
    given = dict(x=x, ffn1_norm=ffn1_norm, ffn1_w_gate=ffn1_w_gate, ffn1_w_up=ffn1_w_up, ffn1_w_down=ffn1_w_down, mix_norm=mix_norm, w_in=w_in, fgate_bias=fgate_bias, conv_w=conv_w, conv_b=conv_b, conv_ln_g=conv_ln_g, conv_ln_b=conv_ln_b, w_out=w_out, ffn2_norm=ffn2_norm, ffn2_w_gate=ffn2_w_gate, ffn2_w_up=ffn2_w_up, ffn2_w_down=ffn2_w_down, final_norm=final_norm, loss_target=loss_target, m_ffn1_norm=m_ffn1_norm, m_ffn1_w_gate=m_ffn1_w_gate, m_ffn1_w_up=m_ffn1_w_up, m_ffn1_w_down=m_ffn1_w_down, m_mix_norm=m_mix_norm, m_w_in=m_w_in, m_fgate_bias=m_fgate_bias, m_conv_w=m_conv_w, m_conv_b=m_conv_b, m_conv_ln_g=m_conv_ln_g, m_conv_ln_b=m_conv_ln_b, m_w_out=m_w_out, m_ffn2_norm=m_ffn2_norm, m_ffn2_w_gate=m_ffn2_w_gate, m_ffn2_w_up=m_ffn2_w_up, m_ffn2_w_down=m_ffn2_w_down, m_final_norm=m_final_norm, v_ffn1_norm=v_ffn1_norm, v_ffn1_w_gate=v_ffn1_w_gate, v_ffn1_w_up=v_ffn1_w_up, v_ffn1_w_down=v_ffn1_w_down, v_mix_norm=v_mix_norm, v_w_in=v_w_in, v_fgate_bias=v_fgate_bias, v_conv_w=v_conv_w, v_conv_b=v_conv_b, v_conv_ln_g=v_conv_ln_g, v_conv_ln_b=v_conv_ln_b, v_w_out=v_w_out, v_ffn2_norm=v_ffn2_norm, v_ffn2_w_gate=v_ffn2_w_gate, v_ffn2_w_up=v_ffn2_w_up, v_ffn2_w_down=v_ffn2_w_down, v_final_norm=v_final_norm)
    weights = {n: given[n] for n in TWIN_WEIGHTS}
    shared = {n: given[n] for n in SHARED_INPUTS}
    per_example = {n: given[n] for n in ['x']}
    grad_fn = _jax.value_and_grad(_loss, argnums=(0, 1))

    def one_microbatch(ex, loss_target):
        ex = dict(ex)
        diff = ex.pop(TWIN_DIFF_INPUT)
        return grad_fn(weights, diff, {**shared, **ex}, loss_target)

    if N_MICROBATCH == 1:
        loss, (grad_w, grad_x) = one_microbatch(per_example, given["loss_target"])
    else:
        def body(carry, xs):
            loss_sum, grad_sum = carry
            l_k, (gw_k, gx_k) = one_microbatch(xs[0], xs[1])
            with _jax.named_scope("update"):
                return (loss_sum + l_k, _jax.tree.map(_jnp.add, grad_sum, gw_k)), gx_k

        init = (_jnp.zeros((), _jnp.float32), _jax.tree.map(_jnp.zeros_like, weights))
        (loss, grad_w), grad_x = _jax.lax.scan(body, init, (per_example, given["loss_target"]))
    with _jax.named_scope("update"):
        delta_w, new_m, new_v = {}, {}, {}
        for n in TWIN_WEIGHTS:
            delta_w[n], new_m[n], new_v[n] = _adamw(weights[n], grad_w[n], given["m_" + n], given["v_" + n])
    return (loss, grad_x, *[grad_w[n] for n in TWIN_WEIGHTS], *[delta_w[n] for n in TWIN_WEIGHTS],
            *[new_m[n] for n in TWIN_WEIGHTS], *[new_v[n] for n in TWIN_WEIGHTS])
```

```python
import functools
import math

import jax
import jax.numpy as jnp
from jax import lax
from jax.experimental import pallas as pl
from jax.experimental.pallas import tpu as pltpu

F32 = jnp.float32
BF16 = jnp.bfloat16
NORM_EPS = 1e-6
LN_EPS = 1e-5
NEG_INF = -1e30
HEAD_DIM = 64
CONV_K = 31
HALO = 32
LANES = 128
N_CHIP = 4
N_DEV = 8
VMEM_LIMIT = 52 * 1024 * 1024
MESH = pl.DeviceIdType.MESH

ADAM_LR = 0.001
ADAM_B1 = 0.9
ADAM_B2 = 0.999
ADAM_EPS = 1e-08
ADAM_WD = 0.01
ADAM_STEP = 10

NN = (((1,), (0,)), ((), ()))
NT = (((1,), (1,)), ((), ()))
TN = (((0,), (0,)), ((), ()))


def _tile(n, pref, unit=128):
    if n <= pref:
        return n
    t = (pref // unit) * unit
    while t > 0:
        if n % t == 0:
            return t
        t -= unit
    raise ValueError(f"no tile for {n} under {pref}")


def _pcall(body, *, name, grid, in_specs, out_specs, out_shape, scratch=()):
    return pl.pallas_call(
        body, name=name, grid=grid, in_specs=in_specs, out_specs=out_specs, out_shape=out_shape,
        scratch_shapes=list(scratch),
        compiler_params=pltpu.CompilerParams(
            dimension_semantics=("arbitrary",) * len(grid), vmem_limit_bytes=VMEM_LIMIT),
    )


def _sigmoid(x):
    return 1.0 / (1.0 + jnp.exp(-x))


def _mm(name, *, grid, pairs, once_pairs=(), extra=(), out_shape, out_specs, acc_shapes, nk, kaxis, epilogue):
    all_pairs = list(pairs) + list(once_pairs)
    n_p, n_o = len(pairs), len(once_pairs)
    n_e, n_out, n_acc = len(extra), len(out_shape), len(acc_shapes)

    def body(*refs):
        ab = refs[: 2 * (n_p + n_o)]
        ex = refs[2 * (n_p + n_o): 2 * (n_p + n_o) + n_e]
        outs = refs[2 * (n_p + n_o) + n_e: 2 * (n_p + n_o) + n_e + n_out]
        accs = refs[2 * (n_p + n_o) + n_e + n_out:]

        def dots(idx_range):
            vals = [None] * n_acc
            for p in idx_range:
                d = lax.dot_general(ab[2 * p][...], ab[2 * p + 1][...], all_pairs[p][4],
                                    preferred_element_type=F32)
                ai = all_pairs[p][5]
                vals[ai] = d if vals[ai] is None else vals[ai] + d
            return vals

        if nk == 1:
            vals = dots(range(n_p + n_o))
            epilogue(vals, ex, outs)
            return

        k = pl.program_id(kaxis)

        @pl.when(k == 0)
        def _():
            vals = dots(range(n_p + n_o))
            for ai in range(n_acc):
                accs[ai][...] = vals[ai]

        @pl.when(k > 0)
        def _():
            vals = dots(range(n_p))
            for ai in range(n_acc):
                if vals[ai] is not None:
                    accs[ai][...] += vals[ai]

        @pl.when(k == nk - 1)
        def _():
            epilogue([a[...] for a in accs], ex, outs)

    operands, in_specs = [], []
    for p in all_pairs:
        operands += [p[0], p[2]]
        in_specs += [p[1], p[3]]
    for arr, spec in extra:
        operands.append(arr)
        in_specs.append(spec)
    scratch = [pltpu.VMEM(s, F32) for s in acc_shapes] if nk > 1 else []
    return _pcall(body, name=name, grid=grid, in_specs=in_specs, out_specs=out_specs, out_shape=out_shape,
                  scratch=scratch)(*operands)


def rms_fwd(x, g):
    T, D = x.shape
    tt = _tile(T, 512, 8)

    def body(x_ref, g_ref, h_ref, r_ref):
        xv = x_ref[...]
        r = lax.rsqrt(jnp.mean(xv * xv, axis=-1, keepdims=True) + NORM_EPS)
        h_ref[...] = (xv * r * g_ref[...]).astype(BF16)
        r_ref[...] = r

    return _pcall(
        body, name="rms_fwd", grid=(T // tt,),
        in_specs=[pl.BlockSpec((tt, D), lambda i: (i, 0)), pl.BlockSpec((1, D), lambda i: (0, 0))],
        out_specs=[pl.BlockSpec((tt, D), lambda i: (i, 0)), pl.BlockSpec((tt, 1), lambda i: (i, 0))],
        out_shape=[jax.ShapeDtypeStruct((T, D), BF16), jax.ShapeDtypeStruct((T, 1), F32)],
    )(x, g)


def rms_bwd(dh, x, r, g, dres, out_scale):
    T, D = x.shape
    tt = _tile(T, 256, 8)

    def body(dh_ref, x_ref, r_ref, g_ref, dres_ref, dx_ref, dxb_ref, dg_ref):
        i = pl.program_id(0)
        xh = x_ref[...] * r_ref[...]
        dhv = dh_ref[...]
        dxh = dhv * g_ref[...]
        dx = dres_ref[...] + r_ref[...] * (dxh - xh * jnp.mean(dxh * xh, axis=-1, keepdims=True))
        dx_ref[...] = dx
        dxb_ref[...] = (out_scale * dx).astype(BF16)
        part = jnp.sum(dhv * xh, axis=0, keepdims=True)

        @pl.when(i == 0)
        def _():
            dg_ref[...] = part

        @pl.when(i > 0)
        def _():
            dg_ref[...] += part

    row = pl.BlockSpec((tt, D), lambda i: (i, 0))
    return _pcall(
        body, name="rms_bwd", grid=(T // tt,),
        in_specs=[row, row, pl.BlockSpec((tt, 1), lambda i: (i, 0)), pl.BlockSpec((1, D), lambda i: (0, 0)), row],
        out_specs=[row, row, pl.BlockSpec((1, D), lambda i: (0, 0))],
        out_shape=[jax.ShapeDtypeStruct((T, D), F32), jax.ShapeDtypeStruct((T, D), BF16),
                   jax.ShapeDtypeStruct((1, D), F32)],
    )(dh, x, r, g, dres)


def final_loss(x, tgt, g):
    T, D = x.shape
    tt = _tile(T, 256, 8)

    def body(x_ref, t_ref, g_ref, dx_ref, dxb_ref, loss_ref, dg_ref):
        i = pl.program_id(0)
        xv = x_ref[...]
        r = lax.rsqrt(jnp.mean(xv * xv, axis=-1, keepdims=True) + NORM_EPS)
        xh = xv * r
        err = xh * g_ref[...] - t_ref[...]
        part_loss = 0.5 * jnp.sum(jnp.mean(err * err, axis=-1, keepdims=True), axis=0, keepdims=True)
        dy = err * (1.0 / D)
        dxh = dy * g_ref[...]
        dx = r * (dxh - xh * jnp.mean(dxh * xh, axis=-1, keepdims=True))
        dx_ref[...] = dx
        dxb_ref[...] = (0.5 * dx).astype(BF16)
        part_g = jnp.sum(dy * xh, axis=0, keepdims=True)
        part_l = jnp.broadcast_to(part_loss, (8, LANES))

        @pl.when(i == 0)
        def _():
            dg_ref[...] = part_g
            loss_ref[...] = part_l

        @pl.when(i > 0)
        def _():
            dg_ref[...] += part_g
            loss_ref[...] += part_l

    row = pl.BlockSpec((tt, D), lambda i: (i, 0))
    return _pcall(
        body, name="final_loss", grid=(T // tt,),
        in_specs=[row, row, pl.BlockSpec((1, D), lambda i: (0, 0))],
        out_specs=[row, row, pl.BlockSpec((8, LANES), lambda i: (0, 0)), pl.BlockSpec((1, D), lambda i: (0, 0))],
        out_shape=[jax.ShapeDtypeStruct((T, D), F32), jax.ShapeDtypeStruct((T, D), BF16),
                   jax.ShapeDtypeStruct((8, LANES), F32), jax.ShapeDtypeStruct((1, D), F32)],
    )(x, tgt, g)


def ffn_up(h, wg3, wu3):
    T, D = h.shape
    nc, _, fs = wg3.shape
    tm = _tile(T, 256, 8)

    def epilogue(vals, ex, outs):
        a, b = vals
        outs[0][...] = a.astype(BF16)
        outs[1][...] = b.astype(BF16)
        outs[2][...] = (a * _sigmoid(a) * b).astype(BF16)

    h_spec = pl.BlockSpec((tm, D), lambda j, i: (i, 0))
    w_spec = pl.BlockSpec((None, D, fs), lambda j, i: (j, 0, 0))
    o_spec = pl.BlockSpec((tm, fs), lambda j, i: (i, j))
    o_shape = jax.ShapeDtypeStruct((T, nc * fs), BF16)
    return _mm("ffn_up", grid=(nc, T // tm),
               pairs=[(h, h_spec, wg3, w_spec, NN, 0), (h, h_spec, wu3, w_spec, NN, 1)],
               out_shape=[o_shape] * 3, out_specs=[o_spec] * 3, acc_shapes=[(tm, fs)] * 2, nk=1, kaxis=None,
               epilogue=epilogue)


def mm_residual(name, a, b3, res, scale):
    T = a.shape[0]
    nk, tk, N = b3.shape
    tm, tn = _tile(T, 512, 8), _tile(N, 1024)

    def epilogue(vals, ex, outs):
        outs[0][...] = ex[0][...] + scale * vals[0]

    return _mm(name, grid=(T // tm, N // tn, nk),
               pairs=[(a, pl.BlockSpec((tm, tk), lambda i, n, k: (i, k)),
                       b3, pl.BlockSpec((None, tk, tn), lambda i, n, k: (k, 0, n)), NN, 0)],
               extra=[(res, pl.BlockSpec((tm, tn), lambda i, n, k: (i, n)))],
               out_shape=[jax.ShapeDtypeStruct((T, N), F32)],
               out_specs=[pl.BlockSpec((tm, tn), lambda i, n, k: (i, n))],
               acc_shapes=[(tm, tn)], nk=nk, kaxis=2, epilogue=epilogue)


def ffn_bwd_mid(dout, wd3, a, b):
    T, D = dout.shape
    nc, fs, _ = wd3.shape
    tm = _tile(T, 256, 8)

    def epilogue(vals, ex, outs):
        dm = vals[0]
        av = ex[0][...].astype(F32)
        bv = ex[1][...].astype(F32)
        s = _sigmoid(av)
        outs[0][...] = (dm * bv * (s * (1.0 + av * (1.0 - s)))).astype(BF16)
        outs[1][...] = (dm * (av * s)).astype(BF16)

    t_spec = pl.BlockSpec((tm, fs), lambda j, i: (i, j))
    o_shape = jax.ShapeDtypeStruct((T, nc * fs), BF16)
    return _mm("ffn_bwd_mid", grid=(nc, T // tm),
               pairs=[(dout, pl.BlockSpec((tm, D), lambda j, i: (i, 0)),
                       wd3, pl.BlockSpec((None, fs, D), lambda j, i: (j, 0, 0)), NT, 0)],
               extra=[(a, t_spec), (b, t_spec)],
               out_shape=[o_shape] * 2, out_specs=[t_spec] * 2, acc_shapes=[(tm, fs)], nk=1, kaxis=None,
               epilogue=epilogue)


def dw_rowshard(name, a, b, nc):
    T, M = a.shape
    N = b.shape[1]
    ms = M // nc
    tn, tk = _tile(N, 1024), _tile(T, 512, 16)

    def epilogue(vals, ex, outs):
        outs[0][...] = vals[0].astype(BF16)

    return _mm(name, grid=(nc, N // tn, T // tk),
               pairs=[(a, pl.BlockSpec((tk, ms), lambda j, n, k: (k, j)),
                       b, pl.BlockSpec((tk, tn), lambda j, n, k: (k, n)), TN, 0)],
               out_shape=[jax.ShapeDtypeStruct((nc, ms, N), BF16)],
               out_specs=[pl.BlockSpec((None, ms, tn), lambda j, n, k: (j, 0, n))],
               acc_shapes=[(ms, tn)], nk=T // tk, kaxis=2, epilogue=epilogue)


def dw_colshard(name, a, bs, nc):
    T, M = a.shape
    ns = bs[0].shape[1] // nc
    tm, tk = _tile(M, 512), _tile(T, 512, 16)

    def epilogue(vals, ex, outs):
        for v, o in zip(vals, outs):
            o[...] = v.astype(BF16)

    a_spec = pl.BlockSpec((tk, tm), lambda j, m, k: (k, m))
    b_spec = pl.BlockSpec((tk, ns), lambda j, m, k: (k, j))
    return _mm(name, grid=(nc, M // tm, T // tk),
               pairs=[(a, a_spec, b, b_spec, TN, p) for p, b in enumerate(bs)],
               out_shape=[jax.ShapeDtypeStruct((nc, M, ns), BF16)] * len(bs),
               out_specs=[pl.BlockSpec((None, tm, ns), lambda j, m, k: (j, m, 0))] * len(bs),
               acc_shapes=[(tm, ns)] * len(bs), nk=T // tk, kaxis=2, epilogue=epilogue)


def dw_plain(name, a, b):
    T, M = a.shape
    N = b.shape[1]
    tm, tn, tk = _tile(M, 1024), _tile(N, 1024), _tile(T, 512, 16)

    def epilogue(vals, ex, outs):
        outs[0][...] = vals[0]

    return _mm(name, grid=(N // tn, M // tm, T // tk),
               pairs=[(a, pl.BlockSpec((tk, tm), lambda n, m, k: (k, m)),
                       b, pl.BlockSpec((tk, tn), lambda n, m, k: (k, n)), TN, 0)],
               out_shape=[jax.ShapeDtypeStruct((M, N), F32)],
               out_specs=[pl.BlockSpec((tm, tn), lambda n, m, k: (m, n))],
               acc_shapes=[(tm, tn)], nk=T // tk, kaxis=2, epilogue=epilogue)[0]


def ffn_dh(da, db, wg3, wu3):
    T = da.shape[0]
    nc, D, fs = wg3.shape
    tm, tn = _tile(T, 512, 8), _tile(D, 1024)

    def epilogue(vals, ex, outs):
        outs[0][...] = vals[0]

    a_spec = pl.BlockSpec((tm, fs), lambda i, n, k: (i, k))
    w_spec = pl.BlockSpec((None, tn, fs), lambda i, n, k: (k, n, 0))
    return _mm("ffn_dh", grid=(T // tm, D // tn, nc),
               pairs=[(da, a_spec, wg3, w_spec, NT, 0), (db, a_spec, wu3, w_spec, NT, 0)],
               out_shape=[jax.ShapeDtypeStruct((T, D), F32)],
               out_specs=[pl.BlockSpec((tm, tn), lambda i, n, k: (i, n))],
               acc_shapes=[(tm, tn)], nk=nc, kaxis=2, epilogue=epilogue)[0]


def proj_main(h, w):
    T, D = h.shape
    P = w.shape[1]
    tm, tn = _tile(T, 512, 8), _tile(P, 1024)

    def epilogue(vals, ex, outs):
        outs[0][...] = vals[0].astype(BF16)

    return _mm("proj_main", grid=(P // tn, T // tm),
               pairs=[(h, pl.BlockSpec((tm, D), lambda j, i: (i, 0)),
                       w, pl.BlockSpec((D, tn), lambda j, i: (0, j)), NN, 0)],
               out_shape=[jax.ShapeDtypeStruct((T, P), BF16)],
               out_specs=[pl.BlockSpec((tm, tn), lambda j, i: (i, j))],
               acc_shapes=[(tm, tn)], nk=1, kaxis=None, epilogue=epilogue)[0]


def mm_nt_bf16(name, a, w):
    T, K = a.shape
    M = w.shape[0]
    tm, tn = _tile(T, 512, 8), _tile(M, 1024)

    def epilogue(vals, ex, outs):
        outs[0][...] = vals[0].astype(BF16)

    return _mm(name, grid=(T // tm, M // tn),
               pairs=[(a, pl.BlockSpec((tm, K), lambda i, n: (i, 0)),
                       w, pl.BlockSpec((tn, K), lambda i, n: (n, 0)), NT, 0)],
               out_shape=[jax.ShapeDtypeStruct((T, M), BF16)],
               out_specs=[pl.BlockSpec((tm, tn), lambda i, n: (i, n))],
               acc_shapes=[(tm, tn)], nk=1, kaxis=None, epilogue=epilogue)[0]


def proj_dh(dproj, w_main, df, w_f):
    T, P = dproj.shape
    D = w_main.shape[0]
    tm, tn, tk = _tile(T, 512, 8), _tile(D, 1024), _tile(P, 1280)

    def epilogue(vals, ex, outs):
        outs[0][...] = vals[0]

    return _mm("proj_dh", grid=(T // tm, D // tn, P // tk),
               pairs=[(dproj, pl.BlockSpec((tm, tk), lambda i, n, k: (i, k)),
                       w_main, pl.BlockSpec((tn, tk), lambda i, n, k: (n, k)), NT, 0)],
               once_pairs=[(df, pl.BlockSpec((tm, LANES), lambda i, n, k: (i, 0)),
                            w_f, pl.BlockSpec((tn, LANES), lambda i, n, k: (n, 0)), NT, 0)],
               out_shape=[jax.ShapeDtypeStruct((T, D), F32)],
               out_specs=[pl.BlockSpec((tm, tn), lambda i, n, k: (i, n))],
               acc_shapes=[(tm, tn)], nk=P // tk, kaxis=2, epilogue=epilogue)[0]


def fgate_fwd(h, w_f, bias, n_heads):
    T, D = h.shape
    tt = _tile(T, 512, 8)

    def body(h_ref, w_ref, b_ref, f_ref, c_ref, carry):
        i = pl.program_id(0)

        @pl.when(i == 0)
        def _():
            carry[...] = jnp.zeros_like(carry)

        f = jnp.dot(h_ref[...], w_ref[...], preferred_element_type=F32) + b_ref[...]
        logf = jnp.minimum(f, 0.0) - jnp.log(1.0 + jnp.exp(-jnp.abs(f)))
        tri = (lax.broadcasted_iota(jnp.int32, (tt, tt), 0) >= lax.broadcasted_iota(jnp.int32, (tt, tt), 1))
        cs = jnp.dot(tri.astype(F32), logf, preferred_element_type=F32, precision=lax.Precision.HIGHEST)
        c = cs + carry[...]
        f_ref[...] = f
        c_ref[...] = c
        carry[...] = c[tt - 1:tt, :]

    row = pl.BlockSpec((tt, LANES), lambda i: (i, 0))
    return _pcall(
        body, name="fgate_fwd", grid=(T // tt,),
        in_specs=[pl.BlockSpec((tt, D), lambda i: (i, 0)), pl.BlockSpec((D, LANES), lambda i: (0, 0)),
                  pl.BlockSpec((1, LANES), lambda i: (0, 0))],
        out_specs=[row, row],
        out_shape=[jax.ShapeDtypeStruct((T, LANES), F32)] * 2,
        scratch=[pltpu.VMEM((1, LANES), F32)],
    )(h, w_f, bias)


def fgate_bwd(dc, f, n_heads):
    T = dc.shape[0]
    tt = _tile(T, 512, 8)
    nt = T // tt

    def body(dc_ref, f_ref, df_ref, db_ref, carry):
        i = pl.program_id(0)

        @pl.when(i == 0)
        def _():
            carry[...] = jnp.zeros_like(carry)

        tri = (lax.broadcasted_iota(jnp.int32, (tt, tt), 1) >= lax.broadcasted_iota(jnp.int32, (tt, tt), 0))
        rs = jnp.dot(tri.astype(F32), dc_ref[...], preferred_element_type=F32,
                     precision=lax.Precision.HIGHEST) + carry[...]
        carry[...] = rs[0:1, :]
        lane = lax.broadcasted_iota(jnp.int32, (tt, LANES), 1)
        df = jnp.where(lane < n_heads, rs * _sigmoid(-f_ref[...]), 0.0)
        df_ref[...] = df.astype(BF16)
        part = jnp.sum(df, axis=0, keepdims=True)

        @pl.when(i == 0)
        def _():
            db_ref[...] = part

        @pl.when(i > 0)
        def _():
            db_ref[...] += part

    rev = pl.BlockSpec((tt, LANES), lambda i: (nt - 1 - i, 0))
    return _pcall(
        body, name="fgate_bwd", grid=(nt,),
        in_specs=[rev, rev],
        out_specs=[rev, pl.BlockSpec((1, LANES), lambda i: (0, 0))],
        out_shape=[jax.ShapeDtypeStruct((T, LANES), BF16), jax.ShapeDtypeStruct((1, LANES), F32)],
        scratch=[pltpu.VMEM((1, LANES), F32)],
    )(dc, f)


def conv_fwd(proj, conv_w, conv_b, ln_g, ln_b):
    T = proj.shape[0]
    C = conv_w.shape[1]
    tt = _tile(T, 256, HALO)
    hb = tt // HALO

    def body(a_ref, g_ref, ah_ref, gh_ref, w_ref, cb_ref, lg_ref, lb_ref, ypre_ref, y_ref, ubuf):
        i = pl.program_id(0)
        u = a_ref[...].astype(F32) * _sigmoid(g_ref[...].astype(F32))
        uh = ah_ref[...].astype(F32) * _sigmoid(gh_ref[...].astype(F32))
        ubuf[0:HALO, :] = jnp.where(i == 0, 0.0, uh)
        ubuf[HALO:HALO + tt, :] = u
        acc = jnp.broadcast_to(cb_ref[...], (tt, C))
        for k in range(CONV_K):
            acc = acc + w_ref[k:k + 1, :] * ubuf[pl.ds(HALO - (CONV_K - 1) + k, tt), :]
        ypre_ref[...] = acc
        mu = jnp.mean(acc, axis=-1, keepdims=True)
        d = acc - mu
        rstd = lax.rsqrt(jnp.mean(d * d, axis=-1, keepdims=True) + LN_EPS)
        z = d * rstd * lg_ref[...] + lb_ref[...]
        y_ref[...] = (z * _sigmoid(z)).astype(BF16)

    vec = pl.BlockSpec((1, C), lambda i: (0, 0))
    return _pcall(
        body, name="conv_fwd", grid=(T // tt,),
        in_specs=[pl.BlockSpec((tt, C), lambda i: (i, 0)), pl.BlockSpec((tt, C), lambda i: (i, 1)),
                  pl.BlockSpec((HALO, C), lambda i: (jnp.maximum(i * hb - 1, 0), 0)),
                  pl.BlockSpec((HALO, C), lambda i: (jnp.maximum(i * hb - 1, 0), 1)),
                  pl.BlockSpec((HALO, C), lambda i: (0, 0)), vec, vec, vec],
        out_specs=[pl.BlockSpec((tt, C), lambda i: (i, 0))] * 2,
        out_shape=[jax.ShapeDtypeStruct((T, C), F32), jax.ShapeDtypeStruct((T, C), BF16)],
        scratch=[pltpu.VMEM((tt + HALO, C), F32)],
    )(proj, proj, proj, proj, conv_w, conv_b, ln_g, ln_b)


def conv_bwd(proj, ypre, dycat, conv_w, ln_g, ln_b):
    T = proj.shape[0]
    C = conv_w.shape[1]
    tt = _tile(T, 256, HALO)
    hb = tt // HALO
    nt = T // tt
    last_h = T // HALO - 1

    def ln_bwd(ypre_v, dout_v, lg, lb):
        mu = jnp.mean(ypre_v, axis=-1, keepdims=True)
        d = ypre_v - mu
        rstd = lax.rsqrt(jnp.mean(d * d, axis=-1, keepdims=True) + LN_EPS)
        yh = d * rstd
        z = yh * lg + lb
        s = _sigmoid(z)
        dz = dout_v * (s * (1.0 + z * (1.0 - s)))
        dyh = dz * lg
        dy = rstd * (dyh - jnp.mean(dyh, axis=-1, keepdims=True)
                     - yh * jnp.mean(dyh * yh, axis=-1, keepdims=True))
        return dy, dz, yh

    def body(a_ref, g_ref, ah_ref, gh_ref, yp_ref, ypn_ref, do_ref, don_ref, w_ref, lg_ref, lb_ref,
             dag_ref, dw_ref, dcb_ref, dlg_ref, dlb_ref, ubuf, dybuf):
        i = pl.program_id(0)
        av = a_ref[...].astype(F32)
        sg = _sigmoid(g_ref[...].astype(F32))
        uh = ah_ref[...].astype(F32) * _sigmoid(gh_ref[...].astype(F32))
        ubuf[0:HALO, :] = jnp.where(i == 0, 0.0, uh)
        ubuf[HALO:HALO + tt, :] = av * sg
        lg, lb = lg_ref[...], lb_ref[...]
        dy, dz, yh = ln_bwd(yp_ref[...], do_ref[...].astype(F32), lg, lb)
        dyn, _, _ = ln_bwd(ypn_ref[...], don_ref[...].astype(F32), lg, lb)
        dybuf[0:tt, :] = dy
        dybuf[tt:tt + HALO, :] = jnp.where(i == nt - 1, 0.0, dyn)

        @pl.when(i == 0)
        def _():
            dw_ref[...] = jnp.zeros_like(dw_ref)
            dcb_ref[...] = jnp.zeros_like(dcb_ref)
            dlg_ref[...] = jnp.zeros_like(dlg_ref)
            dlb_ref[...] = jnp.zeros_like(dlb_ref)

        du = jnp.zeros((tt, C), F32)
        for k in range(CONV_K):
            du = du + w_ref[k:k + 1, :] * dybuf[pl.ds(CONV_K - 1 - k, tt), :]
            dw_ref[k:k + 1, :] += jnp.sum(dy * ubuf[pl.ds(HALO - (CONV_K - 1) + k, tt), :], axis=0, keepdims=True)
        dcb_ref[...] += jnp.sum(dy, axis=0, keepdims=True)
        dlg_ref[...] += jnp.sum(dz * yh, axis=0, keepdims=True)
        dlb_ref[...] += jnp.sum(dz, axis=0, keepdims=True)

        dag_ref[:, 0:C] = (du * sg).astype(BF16)
        dag_ref[:, C:2 * C] = (du * av * sg * (1.0 - sg)).astype(BF16)

    vec = pl.BlockSpec((1, C), lambda i: (0, 0))
    prev_h = lambda col: pl.BlockSpec((HALO, C), lambda i: (jnp.maximum(i * hb - 1, 0), col))
    next_h = pl.BlockSpec((HALO, C), lambda i: (jnp.minimum((i + 1) * hb, last_h), 0))
    return _pcall(
        body, name="conv_bwd", grid=(nt,),
        in_specs=[pl.BlockSpec((tt, C), lambda i: (i, 0)), pl.BlockSpec((tt, C), lambda i: (i, 1)),
                  prev_h(0), prev_h(1),
                  pl.BlockSpec((tt, C), lambda i: (i, 0)), next_h,
                  pl.BlockSpec((tt, C), lambda i: (i, 0)), next_h,
                  pl.BlockSpec((HALO, C), lambda i: (0, 0)), vec, vec],
        out_specs=[pl.BlockSpec((tt, 2 * C), lambda i: (i, 0)), pl.BlockSpec((HALO, C), lambda i: (0, 0)),
                   vec, vec, vec],
        out_shape=[jax.ShapeDtypeStruct((T, 2 * C), BF16), jax.ShapeDtypeStruct((HALO, C), F32),
                   jax.ShapeDtypeStruct((1, C), F32), jax.ShapeDtypeStruct((1, C), F32),
                   jax.ShapeDtypeStruct((1, C), F32)],
        scratch=[pltpu.VMEM((tt + HALO, C), F32), pltpu.VMEM((tt + HALO, C), F32)],
    )(proj, proj, proj, proj, ypre, ypre, dycat, dycat, conv_w, ln_g, ln_b)


def attn_fwd(q, k, v, cq, ck4):
    H, T, dh = q.shape
    nkv, tk = ck4.shape[1], ck4.shape[3]
    tq = tk
    scale = 1.0 / math.sqrt(dh)

    def body(q_ref, k_ref, v_ref, cq_ref, ck_ref, o_ref, lse_ref):
        i = pl.program_id(1)
        qv = q_ref[...]
        cqv = cq_ref[...]
        row = i * tq + lax.broadcasted_iota(jnp.int32, (tq, tk), 0)
        col = lax.broadcasted_iota(jnp.int32, (tq, tk), 1)

        def step(j, carry):
            m, l, acc = carry
            off = pl.multiple_of(j * tk, tk)
            kj = k_ref[pl.ds(off, tk), :]
            vj = v_ref[pl.ds(off, tk), :]
            s = lax.dot_general(qv, kj, NT, preferred_element_type=F32) * scale
            s = s + cqv - ck_ref[j]
            s = jnp.where(row >= j * tk + col, s, NEG_INF)
            m_new = jnp.maximum(m, jnp.max(s, axis=-1, keepdims=True))
            alpha = jnp.exp(m - m_new)
            p = jnp.exp(s - m_new)
            l = alpha * l + jnp.sum(p, axis=-1, keepdims=True)
            acc = alpha * acc + jnp.dot(p.astype(BF16), vj, preferred_element_type=F32)
            return m_new, l, acc

        init = (jnp.full((tq, 1), -jnp.inf, F32), jnp.zeros((tq, 1), F32), jnp.zeros((tq, dh), F32))
        m, l, acc = lax.fori_loop(0, i + 1, step, init)
        o_ref[...] = acc / l
        lse_ref[...] = m + jnp.log(l)

    return _pcall(
        body, name="attn_fwd", grid=(H, T // tq),
        in_specs=[pl.BlockSpec((None, tq, dh), lambda h, i: (h, i, 0)),
                  pl.BlockSpec((None, T, dh), lambda h, i: (h, 0, 0)),
                  pl.BlockSpec((None, T, dh), lambda h, i: (h, 0, 0)),
                  pl.BlockSpec((None, tq, 1), lambda h, i: (h, i, 0)),
                  pl.BlockSpec((None, nkv, 1, tk), lambda h, i: (h, 0, 0, 0))],
        out_specs=[pl.BlockSpec((None, tq, dh), lambda h, i: (h, i, 0)),
                   pl.BlockSpec((None, tq, 1), lambda h, i: (h, i, 0))],
        out_shape=[jax.ShapeDtypeStruct((H, T, dh), F32), jax.ShapeDtypeStruct((H, T, 1), F32)],
    )(q, k, v, cq, ck4)


def attn_bwd(q, k, v, o, do, lse, cq, ck4):
    H, T, dh = q.shape
    nkv, tk = ck4.shape[1], ck4.shape[3]
    tq = tk
    nq = T // tq
    scale = 1.0 / math.sqrt(dh)

    def body(q_ref, k_ref, v_ref, o_ref, do_ref, lse_ref, cq_ref, ck_ref,
             dq_ref, dk_ref, dv_ref, dcq_ref, dck_ref):
        j = pl.program_id(1)
        i = pl.program_id(2)

        @pl.when(i >= j)
        def _():
            qv, kv, vv = q_ref[...], k_ref[...], v_ref[...]
            dov = do_ref[...]
            dob = dov.astype(BF16)
            delta = jnp.sum(dov * o_ref[...], axis=-1, keepdims=True)
            s = lax.dot_general(qv, kv, NT, preferred_element_type=F32) * scale
            s = s + cq_ref[...] - ck_ref[...]
            row = i * tq + lax.broadcasted_iota(jnp.int32, (tq, tk), 0)
            col = j * tk + lax.broadcasted_iota(jnp.int32, (tq, tk), 1)
            s = jnp.where(row >= col, s, NEG_INF)
            p = jnp.exp(s - lse_ref[...])
            dp = lax.dot_general(dob, vv, NT, preferred_element_type=F32)
            ds = p * (dp - delta)
            dsb = ds.astype(BF16)
            dv_part = lax.dot_general(p.astype(BF16), dob, TN, preferred_element_type=F32)
            dk_part = lax.dot_general(dsb, qv, TN, preferred_element_type=F32) * scale
            dq_part = jnp.dot(dsb, kv, preferred_element_type=F32) * scale
            dck_part = -jnp.sum(ds, axis=0, keepdims=True)
            dcq_part = jnp.sum(ds, axis=-1, keepdims=True)
            rows = pl.ds(pl.multiple_of(i * tq, tq), tq)

            @pl.when(j == 0)
            def _():
                dq_ref[rows, :] = dq_part
                dcq_ref[rows, :] = dcq_part

            @pl.when(j > 0)
            def _():
                dq_ref[rows, :] += dq_part
                dcq_ref[rows, :] += dcq_part

            @pl.when(i == j)
            def _():
                dk_ref[...] = dk_part
                dv_ref[...] = dv_part
                dck_ref[...] = dck_part

            @pl.when(i > j)
            def _():
                dk_ref[...] += dk_part
                dv_ref[...] += dv_part
                dck_ref[...] += dck_part

    qi = lambda h, j, i: (h, jnp.maximum(i, j), 0)
    q_spec = pl.BlockSpec((None, tq, dh), qi)
    col_spec = pl.BlockSpec((None, tq, 1), qi)
    kv_spec = pl.BlockSpec((None, tk, dh), lambda h, j, i: (h, j, 0))
    ck_spec = pl.BlockSpec((None, None, 1, tk), lambda h, j, i: (h, j, 0, 0))
    return _pcall(
        body, name="attn_bwd", grid=(H, nkv, nq),
        in_specs=[q_spec, kv_spec, kv_spec, q_spec, q_spec, col_spec, col_spec, ck_spec],
        out_specs=[pl.BlockSpec((None, T, dh), lambda h, j, i: (h, 0, 0)), kv_spec, kv_spec,
                   pl.BlockSpec((None, T, 1), lambda h, j, i: (h, 0, 0)), ck_spec],
        out_shape=[jax.ShapeDtypeStruct((H, T, dh), F32)] * 3
        + [jax.ShapeDtypeStruct((H, T, 1), F32), jax.ShapeDtypeStruct((H, nkv, 1, tk), F32)],
    )(q, k, v, o, do, lse, cq, ck4)


def _rows_tile(rows, cols, bytes_per_row_set):
    target = max(8, (2 * 1024 * 1024) // max(1, bytes_per_row_set))
    if rows <= target:
        return rows
    t = (target // 16) * 16
    while t >= 16:
        if rows % t == 0:
            return t
        t -= 16
    return rows


def sum_chips(recv):
    nc, R, C = recv.shape
    tr = _rows_tile(R, C, C * 4)

    def body(r_ref, o_ref):
        acc = r_ref[0].astype(F32)
        for j in range(1, nc):
            acc = acc + r_ref[j].astype(F32)
        o_ref[...] = acc

    return _pcall(
        body, name="sum_chips", grid=(R // tr,),
        in_specs=[pl.BlockSpec((nc, tr, C), lambda i: (0, i, 0))],
        out_specs=[pl.BlockSpec((tr, C), lambda i: (i, 0))],
        out_shape=[jax.ShapeDtypeStruct((R, C), F32)],
    )(recv)[0]


def adamw(w, m, v, g_parts):
    R, C = w.shape
    tr = _rows_tile(R, C, C * 4 * 4)
    n_g = len(g_parts)
    c1 = 1.0 - ADAM_B1
    c2 = 1.0 - ADAM_B2
    bc1 = 1.0 - ADAM_B1 ** ADAM_STEP
    bc2 = 1.0 - ADAM_B2 ** ADAM_STEP

    def body(*refs):
        w_ref, m_ref, v_ref = refs[:3]
        g_refs = refs[3:3 + n_g]
        g_out, d_out, m_out, v_out = refs[3 + n_g:]
        g = g_refs[0][...]
        for r in g_refs[1:]:
            g = g + r[...]
        m_new = ADAM_B1 * m_ref[...] + c1 * g
        v_new = ADAM_B2 * v_ref[...] + c2 * (g * g)
        m_hat = m_new / bc1
        v_hat = v_new / bc2
        g_out[...] = g
        d_out[...] = -ADAM_LR * (m_hat / (jnp.sqrt(v_hat) + ADAM_EPS) + ADAM_WD * w_ref[...])
        m_out[...] = m_new
        v_out[...] = v_new

    spec = pl.BlockSpec((tr, C), lambda i: (i, 0))
    return _pcall(
        body, name="adamw", grid=(R // tr,),
        in_specs=[spec] * (3 + n_g), out_specs=[spec] * 4,
        out_shape=[jax.ShapeDtypeStruct((R, C), F32)] * 4,
    )(w, m, v, *g_parts)


def _chip_coords():
    x, y, c = lax.axis_index("x"), lax.axis_index("y"), lax.axis_index("c")
    others = [(1 - x, y), (x, 1 - y), (1 - x, 1 - y)]
    return x, y, c, others


def gather_weights(shards):
    n = len(shards)
    any_spec = pl.BlockSpec(memory_space=pl.ANY)

    def body(*refs):
        ins, outs = refs[:n], refs[n:2 * n]
        send_sems, recv_sems, local_sems = refs[2 * n:]
        x, y, c, others = _chip_coords()
        me = 2 * x + y
        sibling = (x, y, 1 - c)
        waits = []
        for a in range(n):
            hr = ins[a].shape[0] // 2
            mine = pltpu.make_async_copy(ins[a], outs[a].at[me], local_sems.at[a])
            mine.start()
            waits.append(mine)
            half = ins[a].at[pl.ds(c * hr, hr)]
            for jj, (ox, oy) in enumerate(others):
                cp = pltpu.make_async_remote_copy(
                    src_ref=half, dst_ref=outs[a].at[me, pl.ds(c * hr, hr)],
                    send_sem=send_sems.at[6 * a + jj], recv_sem=recv_sems.at[6 * a + jj],
                    device_id=(ox, oy, c), device_id_type=MESH)
                cp.start()
                waits.append(cp)
        sends = []
        for a in range(n):
            hr = ins[a].shape[0] // 2
            for jj, (ox, oy) in enumerate(others):
                landed = outs[a].at[2 * ox + oy, pl.ds(c * hr, hr)]
                pltpu.make_async_remote_copy(
                    src_ref=landed, dst_ref=landed, send_sem=send_sems.at[6 * a + jj],
                    recv_sem=recv_sems.at[6 * a + jj], device_id=(ox, oy, c), device_id_type=MESH).wait_recv()
                fwd = pltpu.make_async_remote_copy(
                    src_ref=landed, dst_ref=landed, send_sem=send_sems.at[6 * a + 3 + jj],
                    recv_sem=recv_sems.at[6 * a + 3 + jj], device_id=sibling, device_id_type=MESH)
                fwd.start()
                sends.append(fwd)
        for a in range(n):
            hr = ins[a].shape[0] // 2
            for jj, (ox, oy) in enumerate(others):
                theirs = outs[a].at[2 * ox + oy, pl.ds((1 - c) * hr, hr)]
                pltpu.make_async_remote_copy(
                    src_ref=theirs, dst_ref=theirs, send_sem=send_sems.at[6 * a + 3 + jj],
                    recv_sem=recv_sems.at[6 * a + 3 + jj], device_id=sibling, device_id_type=MESH).wait_recv()
        for cp in waits[1::4] + waits[2::4] + waits[3::4] + sends:
            cp.wait_send()
        for cp in waits[0::4]:
            cp.wait()

    return pl.pallas_call(
        body, name="gather_weights",
        in_specs=[any_spec] * n, out_specs=[any_spec] * n,
        out_shape=[jax.ShapeDtypeStruct((N_CHIP,) + s.shape, s.dtype) for s in shards],
        scratch_shapes=[pltpu.SemaphoreType.DMA((6 * n,)), pltpu.SemaphoreType.DMA((6 * n,)),
                        pltpu.SemaphoreType.DMA((n,))],
    )(*shards)


def scatter_grads(grads):
    n = len(grads)
    any_spec = pl.BlockSpec(memory_space=pl.ANY)

    def body(*refs):
        ins, outs = refs[:n], refs[n:2 * n]
        send_sems, recv_sems, local_sems = refs[2 * n:]
        x, y, c, others = _chip_coords()
        me = 2 * x + y
        copies, locals_ = [], []
        for a in range(n):
            mine = pltpu.make_async_copy(ins[a].at[me], outs[a].at[me], local_sems.at[a])
            mine.start()
            locals_.append(mine)
            for jj, (ox, oy) in enumerate(others):
                cp = pltpu.make_async_remote_copy(
                    src_ref=ins[a].at[2 * ox + oy], dst_ref=outs[a].at[me],
                    send_sem=send_sems.at[3 * a + jj], recv_sem=recv_sems.at[3 * a + jj],
                    device_id=(ox, oy, c), device_id_type=MESH)
                cp.start()
                copies.append(cp)
        for a in range(n):
            for jj, (ox, oy) in enumerate(others):
                slot = outs[a].at[2 * ox + oy]
                pltpu.make_async_remote_copy(
                    src_ref=slot, dst_ref=slot, send_sem=send_sems.at[3 * a + jj],
                    recv_sem=recv_sems.at[3 * a + jj], device_id=(ox, oy, c), device_id_type=MESH).wait_recv()
        for cp in copies:
            cp.wait_send()
        for cp in locals_:
            cp.wait()

    return pl.pallas_call(
        body, name="scatter_grads",
        in_specs=[any_spec] * n, out_specs=[any_spec] * n,
        out_shape=[jax.ShapeDtypeStruct(g.shape, g.dtype) for g in grads],
        scratch_shapes=[pltpu.SemaphoreType.DMA((3 * n,)), pltpu.SemaphoreType.DMA((3 * n,)),
                        pltpu.SemaphoreType.DMA((n,))],
    )(*grads)


def swap_siblings(parts):
    n = len(parts)
    any_spec = pl.BlockSpec(memory_space=pl.ANY)

    def body(*refs):
        ins, outs = refs[:n], refs[n:2 * n]
        send_sems, recv_sems = refs[2 * n:]
        x, y, c, _ = _chip_coords()
        copies = []
        for a in range(n):
            cp = pltpu.make_async_remote_copy(
                src_ref=ins[a], dst_ref=outs[a], send_sem=send_sems.at[a], recv_sem=recv_sems.at[a],
                device_id=(x, y, 1 - c), device_id_type=MESH)
            cp.start()
            copies.append(cp)
        for cp in copies:
            cp.wait()

    return pl.pallas_call(
        body, name="swap_siblings",
        in_specs=[any_spec] * n, out_specs=[any_spec] * n,
        out_shape=[jax.ShapeDtypeStruct(p.shape, p.dtype) for p in parts],
        scratch_shapes=[pltpu.SemaphoreType.DMA((n,)), pltpu.SemaphoreType.DMA((n,))],
    )(*parts)


def allreduce_small(v):
    R = v.shape[0]

    def body(v_ref, sum_ref, all_ref, send_sems, recv_sems):
        x, y, c = lax.axis_index("x"), lax.axis_index("y"), lax.axis_index("c")
        me = 4 * x + 2 * y + c
        all_ref[me] = v_ref[...]
        copies = []
        for k in range(1, N_DEV):
            px = 1 - x if k & 4 else x
            py = 1 - y if k & 2 else y
            pc = 1 - c if k & 1 else c
            cp = pltpu.make_async_remote_copy(
                src_ref=v_ref, dst_ref=all_ref.at[me], send_sem=send_sems.at[k - 1], recv_sem=recv_sems.at[k - 1],
                device_id=(px, py, pc), device_id_type=MESH)
            cp.start()
            copies.append((cp, 4 * px + 2 * py + pc))
        for k, (cp, peer) in enumerate(copies):
            pltpu.make_async_remote_copy(
                src_ref=v_ref, dst_ref=all_ref.at[peer], send_sem=send_sems.at[k], recv_sem=recv_sems.at[k],
                device_id=(x, y, c), device_id_type=MESH).wait_recv()
        for cp, _ in copies:
            cp.wait_send()
        acc = all_ref[0]
        for d in range(1, N_DEV):
            acc = acc + all_ref[d]
        sum_ref[...] = acc

    vm = pl.BlockSpec(memory_space=pltpu.VMEM)
    return pl.pallas_call(
        body, name="allreduce_small",
        in_specs=[vm], out_specs=[vm, vm],
        out_shape=[jax.ShapeDtypeStruct((R, LANES), F32), jax.ShapeDtypeStruct((N_DEV, R, LANES), F32)],
        scratch_shapes=[pltpu.SemaphoreType.DMA((N_DEV - 1,)), pltpu.SemaphoreType.DMA((N_DEV - 1,))],
    )(v)[0]


def _heads_first(t, n_heads):
    T = t.shape[0]
    return t.reshape(T, n_heads, HEAD_DIM).transpose(1, 0, 2)


def _heads_last(t):
    H, T, dh = t.shape
    return t.transpose(1, 0, 2).reshape(T, H * dh)


def ffn_forward(x, gain, wg3, wu3, wd3):
    h, r = rms_fwd(x, gain)
    a, b, mid = ffn_up(h, wg3, wu3)
    x_out = mm_residual("ffn_down", mid, wd3, x, 0.5)[0]
    return x_out, (h, r, a, b, mid)


def ffn_backward(dout_b, dres, x, gain, wg3, wu3, wd3, saved, out_scale):
    h, r, a, b, mid = saved
    nc = wg3.shape[0]
    da, db = ffn_bwd_mid(dout_b, wd3, a, b)
    dwd3 = dw_rowshard("ffn_dwd", mid, dout_b, nc)[0]
    dwg3, dwu3 = dw_colshard("ffn_dwgu", h, [da, db], nc)
    dh = ffn_dh(da, db, wg3, wu3)
    dx, dxb, dgain = rms_bwd(dh, x, r, gain, dres, out_scale)
    return dx, dxb, dgain, dwg3, dwu3, dwd3


def local_step(x, tgt, p):
    T, D = x.shape
    C = p["conv_w"].shape[1]
    H = (p["w_main"].shape[1] - 2 * C) // (3 * HEAD_DIM)
    nc = p["ffn1_wg"].shape[0]
    tk = _tile(T, 512, 128)
    nkv = T // tk

    x1, s1 = ffn_forward(x, p["ffn1_norm"], p["ffn1_wg"], p["ffn1_wu"], p["ffn1_wd"])
    h2, r2 = rms_fwd(x1, p["mix_norm"])
    proj = proj_main(h2, p["w_main"])
    f, cum = fgate_fwd(h2, p["w_f"], p["fgate_bias"], H)
    ypre, yconv = conv_fwd(proj, p["conv_w"], p["conv_b"], p["conv_ln_g"], p["conv_ln_b"])
    q = _heads_first(proj[:, 2 * C:2 * C + H * HEAD_DIM], H)
    k = _heads_first(proj[:, 2 * C + H * HEAD_DIM:2 * C + 2 * H * HEAD_DIM], H)
    v = _heads_first(proj[:, 2 * C + 2 * H * HEAD_DIM:], H)
    c_ht = cum[:, :H].T
    cq = c_ht[:, :, None]
    ck4 = c_ht.reshape(H, nkv, 1, tk)
    o, lse = attn_fwd(q, k, v, cq, ck4)
    ycat = jnp.concatenate([yconv, _heads_last(o).astype(BF16)], axis=1)
    x2 = mm_residual("out_proj", ycat, p["w_out3"], x1, 1.0)[0]
    x3, s2 = ffn_forward(x2, p["ffn2_norm"], p["ffn2_wg"], p["ffn2_wu"], p["ffn2_wd"])

    dx3, dx3b, loss_tile, d_final = final_loss(x3, tgt, p["final_norm"])

    dx2, dx2b, d_ffn2_norm, d2wg, d2wu, d2wd = ffn_backward(
        dx3b, dx3, x2, p["ffn2_norm"], p["ffn2_wg"], p["ffn2_wu"], p["ffn2_wd"], s2, 1.0)

    w_out2 = p["w_out3"].reshape(-1, D)
    dycat = mm_nt_bf16("out_proj_dy", dx2b, w_out2)
    dw_out3 = dw_rowshard("out_proj_dw", ycat, dx2b, nc)[0]
    do = _heads_first(dycat[:, C:], H).astype(F32)
    dq, dk, dv, dcq, dck4 = attn_bwd(q, k, v, o, do, lse, cq, ck4)
    dc = (dcq[:, :, 0] + dck4.reshape(H, T)).T
    dc = jnp.pad(dc, ((0, 0), (0, LANES - H)))
    df, d_bias = fgate_bwd(dc, f, H)
    dag, d_conv_w, d_conv_b, d_ln_g, d_ln_b = conv_bwd(proj, ypre, dycat, p["conv_w"], p["conv_ln_g"],
                                                       p["conv_ln_b"])
    dproj = jnp.concatenate([dag, _heads_last(dq).astype(BF16), _heads_last(dk).astype(BF16),
                             _heads_last(dv).astype(BF16)], axis=1)
    dh2 = proj_dh(dproj, p["w_main"], df, p["w_f"])
    dw_main = dw_plain("proj_dw_main", h2, dproj)
    dw_f = dw_plain("proj_dw_f", h2, df)
    dx1, dx1b, d_mix_norm = rms_bwd(dh2, x1, r2, p["mix_norm"], dx2, 0.5)

    dx0, _, d_ffn1_norm, d1wg, d1wu, d1wd = ffn_backward(
        dx1b, dx1, x, p["ffn1_norm"], p["ffn1_wg"], p["ffn1_wu"], p["ffn1_wd"], s1, 1.0)

    grads = dict(
        ffn1_norm=d_ffn1_norm, ffn1_wg=d1wg, ffn1_wu=d1wu, ffn1_wd=d1wd, mix_norm=d_mix_norm,
        w_main=dw_main, w_f=dw_f, fgate_bias=d_bias, conv_w=d_conv_w, conv_b=d_conv_b,
        conv_ln_g=d_ln_g, conv_ln_b=d_ln_b, w_out3=dw_out3, ffn2_norm=d_ffn2_norm,
        ffn2_wg=d2wg, ffn2_wu=d2wu, ffn2_wd=d2wd, final_norm=d_final)
    return loss_tile, dx0, grads


SMALL_NAMES = ("ffn1_norm", "mix_norm", "ffn2_norm", "final_norm", "conv_b", "conv_ln_g", "conv_ln_b")


def _pack_small(vecs, bias, conv_w_rows, loss_tile):
    rows = [vecs[n].reshape(-1, LANES) for n in SMALL_NAMES]
    rows.append(bias.reshape(1, LANES))
    rows.append(conv_w_rows.reshape(-1, LANES))
    rows.append(loss_tile[0:1, :])
    packed = jnp.concatenate(rows, axis=0)
    pad = (-packed.shape[0]) % 8
    return jnp.pad(packed, ((0, pad), (0, 0)))


def _unpack_small(packed, sizes, n_conv_rows):
    out, r = {}, 0
    for n in SMALL_NAMES:
        k = sizes[n] // LANES
        out[n] = packed[r:r + k].reshape(-1)
        r += k
    out["fgate_bias"] = packed[r]
    r += 1
    out["conv_w"] = packed[r:r + n_conv_rows]
    r += n_conv_rows
    out["loss"] = packed[r, 0]
    return out


def kernel(x, ffn1_norm, ffn1_w_gate, ffn1_w_up, ffn1_w_down, mix_norm, w_in, fgate_bias, conv_w, conv_b, conv_ln_g, conv_ln_b, w_out, ffn2_norm, ffn2_w_gate, ffn2_w_up, ffn2_w_down, final_norm, loss_target, m_ffn1_norm, m_ffn1_w_gate, m_ffn1_w_up, m_ffn1_w_down, m_mix_norm, m_w_in, m_fgate_bias, m_conv_w, m_conv_b, m_conv_ln_g, m_conv_ln_b, m_w_out, m_ffn2_norm, m_ffn2_w_gate, m_ffn2_w_up, m_ffn2_w_down, m_final_norm, v_ffn1_norm, v_ffn1_w_gate, v_ffn1_w_up, v_ffn1_w_down, v_mix_norm, v_w_in, v_fgate_bias, v_conv_w, v_conv_b, v_conv_ln_g, v_conv_ln_b, v_w_out, v_ffn2_norm, v_ffn2_w_gate, v_ffn2_w_up, v_ffn2_w_down, v_final_norm):
    w = dict(ffn1_norm=ffn1_norm, ffn1_w_gate=ffn1_w_gate, ffn1_w_up=ffn1_w_up, ffn1_w_down=ffn1_w_down,
             mix_norm=mix_norm, w_in=w_in, fgate_bias=fgate_bias, conv_w=conv_w, conv_b=conv_b,
             conv_ln_g=conv_ln_g, conv_ln_b=conv_ln_b, w_out=w_out, ffn2_norm=ffn2_norm,
             ffn2_w_gate=ffn2_w_gate, ffn2_w_up=ffn2_w_up, ffn2_w_down=ffn2_w_down, final_norm=final_norm)
    m = dict(ffn1_norm=m_ffn1_norm, ffn1_w_gate=m_ffn1_w_gate, ffn1_w_up=m_ffn1_w_up, ffn1_w_down=m_ffn1_w_down,
             mix_norm=m_mix_norm, w_in=m_w_in, fgate_bias=m_fgate_bias, conv_w=m_conv_w, conv_b=m_conv_b,
             conv_ln_g=m_conv_ln_g, conv_ln_b=m_conv_ln_b, w_out=m_w_out, ffn2_norm=m_ffn2_norm,
             ffn2_w_gate=m_ffn2_w_gate, ffn2_w_up=m_ffn2_w_up, ffn2_w_down=m_ffn2_w_down, final_norm=m_final_norm)
    v = dict(ffn1_norm=v_ffn1_norm, ffn1_w_gate=v_ffn1_w_gate, ffn1_w_up=v_ffn1_w_up, ffn1_w_down=v_ffn1_w_down,
             mix_norm=v_mix_norm, w_in=v_w_in, fgate_bias=v_fgate_bias, conv_w=v_conv_w, conv_b=v_conv_b,
             conv_ln_g=v_conv_ln_g, conv_ln_b=v_conv_ln_b, w_out=v_w_out, ffn2_norm=v_ffn2_norm,
             ffn2_w_gate=v_ffn2_w_gate, ffn2_w_up=v_ffn2_w_up, ffn2_w_down=v_ffn2_w_down, final_norm=v_final_norm)
    names = list(w.keys())
    big = ("ffn1_w_gate", "ffn1_w_up", "ffn1_w_down", "w_in", "w_out", "ffn2_w_gate", "ffn2_w_up", "ffn2_w_down")

    T, D = x.shape[1], x.shape[2]
    C = conv_b.shape[0]
    H = fgate_bias.shape[0]
    cs = conv_w.shape[1]
    in_cols = N_CHIP * w_in.shape[1]
    p_main = in_cols - H

    conv_w_pad = jnp.pad(conv_w, ((0, HALO - CONV_K), (0, 0)))
    gathered = gather_weights([w[n].astype(BF16) for n in big] + [conv_w_pad])
    g = dict(zip(big, gathered[:-1]))
    conv_w_full = gathered[-1].transpose(1, 0, 2).reshape(HALO, C)
    w_in_full = g["w_in"].transpose(1, 0, 2).reshape(D, in_cols)
    row = lambda a: a.reshape(1, -1)
    p = dict(
        ffn1_norm=row(ffn1_norm), ffn1_wg=g["ffn1_w_gate"], ffn1_wu=g["ffn1_w_up"], ffn1_wd=g["ffn1_w_down"],
        mix_norm=row(mix_norm), w_main=w_in_full[:, :p_main],
        w_f=jnp.pad(w_in_full[:, p_main:], ((0, 0), (0, LANES - H))),
        fgate_bias=jnp.pad(row(fgate_bias), ((0, 0), (0, LANES - H))),
        conv_w=conv_w_full, conv_b=row(conv_b), conv_ln_g=row(conv_ln_g), conv_ln_b=row(conv_ln_b),
        w_out3=g["w_out"], ffn2_norm=row(ffn2_norm), ffn2_wg=g["ffn2_w_gate"], ffn2_wu=g["ffn2_w_up"],
        ffn2_wd=g["ffn2_w_down"], final_norm=row(final_norm))

    loss_tile, grad_x, gl = local_step(x[0], loss_target[0], p)

    small_sizes = {n: w[n].shape[0] for n in SMALL_NAMES}
    packed = _pack_small({n: gl[n] for n in SMALL_NAMES}, gl["fgate_bias"], gl["conv_w"], loss_tile)
    red = _unpack_small(allreduce_small(packed), small_sizes, HALO * C // LANES)
    loss = red["loss"]
    my_chip = 2 * lax.axis_index("x") + lax.axis_index("y")
    g_conv_w = lax.dynamic_slice_in_dim(red["conv_w"].reshape(HALO, C)[:CONV_K], my_chip * cs, cs, axis=1)

    dw_in_full = jnp.concatenate([gl["w_main"], gl["w_f"][:, :H]], axis=1)
    dw_in3 = dw_in_full.reshape(D, N_CHIP, in_cols // N_CHIP).transpose(1, 0, 2).astype(BF16)
    pieces = dict(ffn1_w_gate=gl["ffn1_wg"], ffn1_w_up=gl["ffn1_wu"], ffn1_w_down=gl["ffn1_wd"], w_in=dw_in3,
                  w_out=gl["w_out3"], ffn2_w_gate=gl["ffn2_wg"], ffn2_w_up=gl["ffn2_wu"],
                  ffn2_w_down=gl["ffn2_wd"])
    received = scatter_grads([pieces[n] for n in big])
    sums = [sum_chips(r) for r in received]
    theirs = swap_siblings(sums)

    grad, delta, new_m, new_v = {}, {}, {}, {}
    for n, mine, other in zip(big, sums, theirs):
        grad[n], delta[n], new_m[n], new_v[n] = adamw(w[n], m[n], v[n], [mine, other])
    grad["conv_w"], delta["conv_w"], new_m["conv_w"], new_v["conv_w"] = adamw(
        conv_w, m["conv_w"], v["conv_w"], [g_conv_w])
    vec_names = SMALL_NAMES + ("fgate_bias",)
    stack = lambda d: jnp.concatenate(
        [jnp.pad(d[n], (0, (-d[n].shape[0]) % LANES)).reshape(-1, LANES) for n in vec_names], axis=0)
    g_stack = jnp.concatenate([red[n].reshape(-1, LANES) for n in SMALL_NAMES] + [red["fgate_bias"][None, :]],
                              axis=0)
    outs = adamw(stack(w), stack(m), stack(v), [g_stack])
    r = 0
    for n in vec_names:
        size = w[n].shape[0]
        k = -(-size // LANES)
        for dst, src in zip((grad, delta, new_m, new_v), outs):
            dst[n] = src[r:r + k].reshape(-1)[:size]
        r += k

    return (loss, grad_x[None], *[grad[n] for n in names], *[delta[n] for n in names],
            *[new_m[n] for n in names], *[new_v[n] for n in names])
```

```python
import functools
import math

import jax
import jax.numpy as jnp
from jax import lax
from jax.experimental import pallas as pl
from jax.experimental.pallas import tpu as pltpu

F32 = jnp.float32
BF16 = jnp.bfloat16
NORM_EPS = 1e-6
LN_EPS = 1e-5
NEG_INF = -1e30
HEAD_DIM = 64
CONV_K = 31
HALO = 32
LANES = 128
N_CHIP = 4
N_DEV = 8
VMEM_LIMIT = 52 * 1024 * 1024
MESH = pl.DeviceIdType.MESH

ADAM_LR = 0.001
ADAM_B1 = 0.9
ADAM_B2 = 0.999
ADAM_EPS = 1e-08
ADAM_WD = 0.01
ADAM_STEP = 10

NN = (((1,), (0,)), ((), ()))
NT = (((1,), (1,)), ((), ()))
TN = (((0,), (0,)), ((), ()))


def _tile(n, pref, unit=128):
    if n <= pref:
        return n
    t = (pref // unit) * unit
    while t > 0:
        if n % t == 0:
            return t
        t -= unit
    raise ValueError(f"no tile for {n} under {pref}")


class Comm:
    def __init__(self, operands, out_shape, sems, start, finish):
        self.operands, self.out_shape, self.sems = list(operands), list(out_shape), list(sems)
        self.start, self.finish = start, finish
        self.results = None


def _pcall(body, *, name, grid, in_specs, out_specs, out_shape, scratch=(), comm=None):
    params = pltpu.CompilerParams(dimension_semantics=("arbitrary",) * len(grid), vmem_limit_bytes=VMEM_LIMIT)
    scratch = list(scratch)
    if comm is None:
        return pl.pallas_call(body, name=name, grid=grid, in_specs=in_specs, out_specs=out_specs,
                              out_shape=out_shape, scratch_shapes=scratch, compiler_params=params)
    n_in, n_out, n_s = len(in_specs), len(out_shape), len(scratch)
    n_ci, n_co = len(comm.operands), len(comm.out_shape)
    any_spec = pl.BlockSpec(memory_space=pl.ANY)

    def carried(*refs):
        ins, refs = refs[:n_in], refs[n_in:]
        c_ins, refs = refs[:n_ci], refs[n_ci:]
        outs, refs = refs[:n_out], refs[n_out:]
        c_outs, refs = refs[:n_co], refs[n_co:]
        scr, c_sems = refs[:n_s], refs[n_s:]
        first = pl.program_id(0) == 0
        last = pl.program_id(0) == grid[0] - 1
        for d in range(1, len(grid)):
            first = jnp.logical_and(first, pl.program_id(d) == 0)
            last = jnp.logical_and(last, pl.program_id(d) == grid[d] - 1)

        @pl.when(first)
        def _():
            comm.start(c_ins, c_outs, c_sems)

        body(*ins, *outs, *scr)

        @pl.when(last)
        def _():
            comm.finish(c_ins, c_outs, c_sems)

    call = pl.pallas_call(
        carried, name=name, grid=grid, in_specs=list(in_specs) + [any_spec] * n_ci,
        out_specs=list(out_specs) + [any_spec] * n_co, out_shape=list(out_shape) + comm.out_shape,
        scratch_shapes=scratch + comm.sems, compiler_params=params)

    def run(*operands):
        res = call(*operands, *comm.operands)
        comm.results = list(res[n_out:])
        return list(res[:n_out])

    return run


def _run_comm(name, comm):
    n_ci, n_co = len(comm.operands), len(comm.out_shape)
    any_spec = pl.BlockSpec(memory_space=pl.ANY)

    def body(*refs):
        c_ins, c_outs, c_sems = refs[:n_ci], refs[n_ci:n_ci + n_co], refs[n_ci + n_co:]
        comm.start(c_ins, c_outs, c_sems)
        comm.finish(c_ins, c_outs, c_sems)

    return pl.pallas_call(body, name=name, in_specs=[any_spec] * n_ci, out_specs=[any_spec] * n_co,
                          out_shape=comm.out_shape, scratch_shapes=comm.sems)(*comm.operands)


def _sigmoid(x):
    return 1.0 / (1.0 + jnp.exp(-x))


def _mm(name, *, grid, pairs, once_pairs=(), extra=(), out_shape, out_specs, acc_shapes, nk, kaxis, epilogue,
        comm=None):
    all_pairs = list(pairs) + list(once_pairs)
    n_p, n_o = len(pairs), len(once_pairs)
    n_e, n_out, n_acc = len(extra), len(out_shape), len(acc_shapes)

    def body(*refs):
        ab = refs[: 2 * (n_p + n_o)]
        ex = refs[2 * (n_p + n_o): 2 * (n_p + n_o) + n_e]
        outs = refs[2 * (n_p + n_o) + n_e: 2 * (n_p + n_o) + n_e + n_out]
        accs = refs[2 * (n_p + n_o) + n_e + n_out:]

        def dots(idx_range):
            vals = [None] * n_acc
            for p in idx_range:
                d = lax.dot_general(ab[2 * p][...], ab[2 * p + 1][...], all_pairs[p][4],
                                    preferred_element_type=F32)
                ai = all_pairs[p][5]
                vals[ai] = d if vals[ai] is None else vals[ai] + d
            return vals

        if nk == 1:
            vals = dots(range(n_p + n_o))
            epilogue(vals, ex, outs)
            return

        k = pl.program_id(kaxis)

        @pl.when(k == 0)
        def _():
            vals = dots(range(n_p + n_o))
            for ai in range(n_acc):
                accs[ai][...] = vals[ai]

        @pl.when(k > 0)
        def _():
            vals = dots(range(n_p))
            for ai in range(n_acc):
                if vals[ai] is not None:
                    accs[ai][...] += vals[ai]

        @pl.when(k == nk - 1)
        def _():
            epilogue([a[...] for a in accs], ex, outs)

    operands, in_specs = [], []
    for p in all_pairs:
        operands += [p[0], p[2]]
        in_specs += [p[1], p[3]]
    for arr, spec in extra:
        operands.append(arr)
        in_specs.append(spec)
    scratch = [pltpu.VMEM(s, F32) for s in acc_shapes] if nk > 1 else []
    return _pcall(body, name=name, grid=grid, in_specs=in_specs, out_specs=out_specs, out_shape=out_shape,
                  scratch=scratch, comm=comm)(*operands)


def rms_fwd(x, g):
    T, D = x.shape
    tt = _tile(T, 512, 8)

    def body(x_ref, g_ref, h_ref, r_ref):
        xv = x_ref[...]
        r = lax.rsqrt(jnp.mean(xv * xv, axis=-1, keepdims=True) + NORM_EPS)
        h_ref[...] = (xv * r * g_ref[...]).astype(BF16)
        r_ref[...] = r

    return _pcall(
        body, name="rms_fwd", grid=(T // tt,),
        in_specs=[pl.BlockSpec((tt, D), lambda i: (i, 0)), pl.BlockSpec((1, D), lambda i: (0, 0))],
        out_specs=[pl.BlockSpec((tt, D), lambda i: (i, 0)), pl.BlockSpec((tt, 1), lambda i: (i, 0))],
        out_shape=[jax.ShapeDtypeStruct((T, D), BF16), jax.ShapeDtypeStruct((T, 1), F32)],
    )(x, g)


def rms_bwd(dh, x, r, g, dres, out_scale):
    T, D = x.shape
    tt = _tile(T, 256, 8)

    def body(dh_ref, x_ref, r_ref, g_ref, dres_ref, dx_ref, dxb_ref, dg_ref):
        i = pl.program_id(0)
        xh = x_ref[...] * r_ref[...]
        dhv = dh_ref[...]
        dxh = dhv * g_ref[...]
        dx = dres_ref[...] + r_ref[...] * (dxh - xh * jnp.mean(dxh * xh, axis=-1, keepdims=True))
        dx_ref[...] = dx
        dxb_ref[...] = (out_scale * dx).astype(BF16)
        part = jnp.sum(dhv * xh, axis=0, keepdims=True)

        @pl.when(i == 0)
        def _():
            dg_ref[...] = part

        @pl.when(i > 0)
        def _():
            dg_ref[...] += part

    row = pl.BlockSpec((tt, D), lambda i: (i, 0))
    return _pcall(
        body, name="rms_bwd", grid=(T // tt,),
        in_specs=[row, row, pl.BlockSpec((tt, 1), lambda i: (i, 0)), pl.BlockSpec((1, D), lambda i: (0, 0)), row],
        out_specs=[row, row, pl.BlockSpec((1, D), lambda i: (0, 0))],
        out_shape=[jax.ShapeDtypeStruct((T, D), F32), jax.ShapeDtypeStruct((T, D), BF16),
                   jax.ShapeDtypeStruct((1, D), F32)],
    )(dh, x, r, g, dres)


def final_loss(x, tgt, g):
    T, D = x.shape
    tt = _tile(T, 256, 8)

    def body(x_ref, t_ref, g_ref, dx_ref, dxb_ref, loss_ref, dg_ref):
        i = pl.program_id(0)
        xv = x_ref[...]
        r = lax.rsqrt(jnp.mean(xv * xv, axis=-1, keepdims=True) + NORM_EPS)
        xh = xv * r
        err = xh * g_ref[...] - t_ref[...]
        part_loss = 0.5 * jnp.sum(jnp.mean(err * err, axis=-1, keepdims=True), axis=0, keepdims=True)
        dy = err * (1.0 / D)
        dxh = dy * g_ref[...]
        dx = r * (dxh - xh * jnp.mean(dxh * xh, axis=-1, keepdims=True))
        dx_ref[...] = dx
        dxb_ref[...] = (0.5 * dx).astype(BF16)
        part_g = jnp.sum(dy * xh, axis=0, keepdims=True)
        part_l = jnp.broadcast_to(part_loss, (8, LANES))

        @pl.when(i == 0)
        def _():
            dg_ref[...] = part_g
            loss_ref[...] = part_l

        @pl.when(i > 0)
        def _():
            dg_ref[...] += part_g
            loss_ref[...] += part_l

    row = pl.BlockSpec((tt, D), lambda i: (i, 0))
    return _pcall(
        body, name="final_loss", grid=(T // tt,),
        in_specs=[row, row, pl.BlockSpec((1, D), lambda i: (0, 0))],
        out_specs=[row, row, pl.BlockSpec((8, LANES), lambda i: (0, 0)), pl.BlockSpec((1, D), lambda i: (0, 0))],
        out_shape=[jax.ShapeDtypeStruct((T, D), F32), jax.ShapeDtypeStruct((T, D), BF16),
                   jax.ShapeDtypeStruct((8, LANES), F32), jax.ShapeDtypeStruct((1, D), F32)],
    )(x, tgt, g)


def ffn_up(h, wg3, wu3, comm=None):
    T, D = h.shape
    nc, _, fs = wg3.shape
    tm = _tile(T, 256, 8)

    def epilogue(vals, ex, outs):
        a, b = vals
        outs[0][...] = a.astype(BF16)
        outs[1][...] = b.astype(BF16)
        outs[2][...] = (a * _sigmoid(a) * b).astype(BF16)

    h_spec = pl.BlockSpec((tm, D), lambda j, i: (i, 0))
    w_spec = pl.BlockSpec((None, D, fs), lambda j, i: (j, 0, 0))
    o_spec = pl.BlockSpec((tm, fs), lambda j, i: (i, j))
    o_shape = jax.ShapeDtypeStruct((T, nc * fs), BF16)
    return _mm("ffn_up", grid=(nc, T // tm),
               pairs=[(h, h_spec, wg3, w_spec, NN, 0), (h, h_spec, wu3, w_spec, NN, 1)],
               out_shape=[o_shape] * 3, out_specs=[o_spec] * 3, acc_shapes=[(tm, fs)] * 2, nk=1, kaxis=None,
               epilogue=epilogue, comm=comm)


def mm_residual(name, a, b3, res, scale, comm=None):
    T = a.shape[0]
    nk, tk, N = b3.shape
    tm, tn = _tile(T, 512, 8), _tile(N, 1024)

    def epilogue(vals, ex, outs):
        outs[0][...] = ex[0][...] + scale * vals[0]

    return _mm(name, grid=(T // tm, N // tn, nk),
               pairs=[(a, pl.BlockSpec((tm, tk), lambda i, n, k: (i, k)),
                       b3, pl.BlockSpec((None, tk, tn), lambda i, n, k: (k, 0, n)), NN, 0)],
               extra=[(res, pl.BlockSpec((tm, tn), lambda i, n, k: (i, n)))],
               out_shape=[jax.ShapeDtypeStruct((T, N), F32)],
               out_specs=[pl.BlockSpec((tm, tn), lambda i, n, k: (i, n))],
               acc_shapes=[(tm, tn)], nk=nk, kaxis=2, epilogue=epilogue, comm=comm)


def ffn_bwd_mid(dout, wd3, a, b, comm=None):
    T, D = dout.shape
    nc, fs, _ = wd3.shape
    tm = _tile(T, 256, 8)

    def epilogue(vals, ex, outs):
        dm = vals[0]
        av = ex[0][...].astype(F32)
        bv = ex[1][...].astype(F32)
        s = _sigmoid(av)
        outs[0][...] = (dm * bv * (s * (1.0 + av * (1.0 - s)))).astype(BF16)
        outs[1][...] = (dm * (av * s)).astype(BF16)

    t_spec = pl.BlockSpec((tm, fs), lambda j, i: (i, j))
    o_shape = jax.ShapeDtypeStruct((T, nc * fs), BF16)
    return _mm("ffn_bwd_mid", grid=(nc, T // tm),
               pairs=[(dout, pl.BlockSpec((tm, D), lambda j, i: (i, 0)),
                       wd3, pl.BlockSpec((None, fs, D), lambda j, i: (j, 0, 0)), NT, 0)],
               extra=[(a, t_spec), (b, t_spec)],
               out_shape=[o_shape] * 2, out_specs=[t_spec] * 2, acc_shapes=[(tm, fs)], nk=1, kaxis=None,
               epilogue=epilogue, comm=comm)


def dw_rowshard(name, a, b, nc, comm=None):
    T, M = a.shape
    N = b.shape[1]
    ms = M // nc
    tn, tk = _tile(N, 1024), _tile(T, 512, 16)

    def epilogue(vals, ex, outs):
        outs[0][...] = vals[0].astype(BF16)

    return _mm(name, grid=(nc, N // tn, T // tk),
               pairs=[(a, pl.BlockSpec((tk, ms), lambda j, n, k: (k, j)),
                       b, pl.BlockSpec((tk, tn), lambda j, n, k: (k, n)), TN, 0)],
               out_shape=[jax.ShapeDtypeStruct((nc, ms, N), BF16)],
               out_specs=[pl.BlockSpec((None, ms, tn), lambda j, n, k: (j, 0, n))],
               acc_shapes=[(ms, tn)], nk=T // tk, kaxis=2, epilogue=epilogue, comm=comm)


def dw_colshard(name, a, bs, nc, comm=None):
    T, M = a.shape
    ns = bs[0].shape[1] // nc
    tm, tk = _tile(M, 512), _tile(T, 512, 16)

    def epilogue(vals, ex, outs):
        for v, o in zip(vals, outs):
            o[...] = v.astype(BF16)

    a_spec = pl.BlockSpec((tk, tm), lambda j, m, k: (k, m))
    b_spec = pl.BlockSpec((tk, ns), lambda j, m, k: (k, j))
    return _mm(name, grid=(nc, M // tm, T // tk),
               pairs=[(a, a_spec, b, b_spec, TN, p) for p, b in enumerate(bs)],
               out_shape=[jax.ShapeDtypeStruct((nc, M, ns), BF16)] * len(bs),
               out_specs=[pl.BlockSpec((None, tm, ns), lambda j, m, k: (j, m, 0))] * len(bs),
               acc_shapes=[(tm, ns)] * len(bs), nk=T // tk, kaxis=2, epilogue=epilogue, comm=comm)


def dw_plain(name, a, b):
    T, M = a.shape
    N = b.shape[1]
    tm, tn, tk = _tile(M, 1024), _tile(N, 1024), _tile(T, 512, 16)

    def epilogue(vals, ex, outs):
        outs[0][...] = vals[0]

    return _mm(name, grid=(N // tn, M // tm, T // tk),
               pairs=[(a, pl.BlockSpec((tk, tm), lambda n, m, k: (k, m)),
                       b, pl.BlockSpec((tk, tn), lambda n, m, k: (k, n)), TN, 0)],
               out_shape=[jax.ShapeDtypeStruct((M, N), F32)],
               out_specs=[pl.BlockSpec((tm, tn), lambda n, m, k: (m, n))],
               acc_shapes=[(tm, tn)], nk=T // tk, kaxis=2, epilogue=epilogue)[0]


def ffn_dh(da, db, wg3, wu3, comm=None):
    T = da.shape[0]
    nc, D, fs = wg3.shape
    tm, tn = _tile(T, 512, 8), _tile(D, 1024)

    def epilogue(vals, ex, outs):
        outs[0][...] = vals[0]

    a_spec = pl.BlockSpec((tm, fs), lambda i, n, k: (i, k))
    w_spec = pl.BlockSpec((None, tn, fs), lambda i, n, k: (k, n, 0))
    return _mm("ffn_dh", grid=(T // tm, D // tn, nc),
               pairs=[(da, a_spec, wg3, w_spec, NT, 0), (db, a_spec, wu3, w_spec, NT, 0)],
               out_shape=[jax.ShapeDtypeStruct((T, D), F32)],
               out_specs=[pl.BlockSpec((tm, tn), lambda i, n, k: (i, n))],
               acc_shapes=[(tm, tn)], nk=nc, kaxis=2, epilogue=epilogue, comm=comm)[0]


def proj_main(h, w):
    T, D = h.shape
    P = w.shape[1]
    tm, tn = _tile(T, 512, 8), _tile(P, 1024)

    def epilogue(vals, ex, outs):
        outs[0][...] = vals[0].astype(BF16)

    return _mm("proj_main", grid=(P // tn, T // tm),
               pairs=[(h, pl.BlockSpec((tm, D), lambda j, i: (i, 0)),
                       w, pl.BlockSpec((D, tn), lambda j, i: (0, j)), NN, 0)],
               out_shape=[jax.ShapeDtypeStruct((T, P), BF16)],
               out_specs=[pl.BlockSpec((tm, tn), lambda j, i: (i, j))],
               acc_shapes=[(tm, tn)], nk=1, kaxis=None, epilogue=epilogue)[0]


def mm_nt_bf16(name, a, w):
    T, K = a.shape
    M = w.shape[0]
    tm, tn = _tile(T, 512, 8), _tile(M, 1024)

    def epilogue(vals, ex, outs):
        outs[0][...] = vals[0].astype(BF16)

    return _mm(name, grid=(T // tm, M // tn),
               pairs=[(a, pl.BlockSpec((tm, K), lambda i, n: (i, 0)),
                       w, pl.BlockSpec((tn, K), lambda i, n: (n, 0)), NT, 0)],
               out_shape=[jax.ShapeDtypeStruct((T, M), BF16)],
               out_specs=[pl.BlockSpec((tm, tn), lambda i, n: (i, n))],
               acc_shapes=[(tm, tn)], nk=1, kaxis=None, epilogue=epilogue)[0]


def proj_dh(dproj, w_main, df, w_f, comm=None):
    T, P = dproj.shape
    D = w_main.shape[0]
    tm, tn, tk = _tile(T, 512, 8), _tile(D, 1024), _tile(P, 1280)

    def epilogue(vals, ex, outs):
        outs[0][...] = vals[0]

    return _mm("proj_dh", grid=(T // tm, D // tn, P // tk),
               pairs=[(dproj, pl.BlockSpec((tm, tk), lambda i, n, k: (i, k)),
                       w_main, pl.BlockSpec((tn, tk), lambda i, n, k: (n, k)), NT, 0)],
               once_pairs=[(df, pl.BlockSpec((tm, LANES), lambda i, n, k: (i, 0)),
                            w_f, pl.BlockSpec((tn, LANES), lambda i, n, k: (n, 0)), NT, 0)],
               out_shape=[jax.ShapeDtypeStruct((T, D), F32)],
               out_specs=[pl.BlockSpec((tm, tn), lambda i, n, k: (i, n))],
               acc_shapes=[(tm, tn)], nk=P // tk, kaxis=2, epilogue=epilogue, comm=comm)[0]


def fgate_fwd(h, w_f, bias, n_heads):
    T, D = h.shape
    tt = _tile(T, 512, 8)

    def body(h_ref, w_ref, b_ref, f_ref, c_ref, carry):
        i = pl.program_id(0)

        @pl.when(i == 0)
        def _():
            carry[...] = jnp.zeros_like(carry)

        f = jnp.dot(h_ref[...], w_ref[...], preferred_element_type=F32) + b_ref[...]
        logf = jnp.minimum(f, 0.0) - jnp.log(1.0 + jnp.exp(-jnp.abs(f)))
        tri = (lax.broadcasted_iota(jnp.int32, (tt, tt), 0) >= lax.broadcasted_iota(jnp.int32, (tt, tt), 1))
        cs = jnp.dot(tri.astype(F32), logf, preferred_element_type=F32, precision=lax.Precision.HIGHEST)
        c = cs + carry[...]
        f_ref[...] = f
        c_ref[...] = c
        carry[...] = c[tt - 1:tt, :]

    row = pl.BlockSpec((tt, LANES), lambda i: (i, 0))
    return _pcall(
        body, name="fgate_fwd", grid=(T // tt,),
        in_specs=[pl.BlockSpec((tt, D), lambda i: (i, 0)), pl.BlockSpec((D, LANES), lambda i: (0, 0)),
                  pl.BlockSpec((1, LANES), lambda i: (0, 0))],
        out_specs=[row, row],
        out_shape=[jax.ShapeDtypeStruct((T, LANES), F32)] * 2,
        scratch=[pltpu.VMEM((1, LANES), F32)],
    )(h, w_f, bias)


def fgate_bwd(dc, f, n_heads):
    T = dc.shape[0]
    tt = _tile(T, 512, 8)
    nt = T // tt

    def body(dc_ref, f_ref, df_ref, db_ref, carry):
        i = pl.program_id(0)

        @pl.when(i == 0)
        def _():
            carry[...] = jnp.zeros_like(carry)

        tri = (lax.broadcasted_iota(jnp.int32, (tt, tt), 1) >= lax.broadcasted_iota(jnp.int32, (tt, tt), 0))
        rs = jnp.dot(tri.astype(F32), dc_ref[...], preferred_element_type=F32,
                     precision=lax.Precision.HIGHEST) + carry[...]
        carry[...] = rs[0:1, :]
        lane = lax.broadcasted_iota(jnp.int32, (tt, LANES), 1)
        df = jnp.where(lane < n_heads, rs * _sigmoid(-f_ref[...]), 0.0)
        df_ref[...] = df.astype(BF16)
        part = jnp.sum(df, axis=0, keepdims=True)

        @pl.when(i == 0)
        def _():
            db_ref[...] = part

        @pl.when(i > 0)
        def _():
            db_ref[...] += part

    rev = pl.BlockSpec((tt, LANES), lambda i: (nt - 1 - i, 0))
    return _pcall(
        body, name="fgate_bwd", grid=(nt,),
        in_specs=[rev, rev],
        out_specs=[rev, pl.BlockSpec((1, LANES), lambda i: (0, 0))],
        out_shape=[jax.ShapeDtypeStruct((T, LANES), BF16), jax.ShapeDtypeStruct((1, LANES), F32)],
        scratch=[pltpu.VMEM((1, LANES), F32)],
    )(dc, f)


def conv_fwd(proj, conv_w, conv_b, ln_g, ln_b):
    T = proj.shape[0]
    C = conv_w.shape[1]
    tt = _tile(T, 256, HALO)
    hb = tt // HALO

    def body(a_ref, g_ref, ah_ref, gh_ref, w_ref, cb_ref, lg_ref, lb_ref, ypre_ref, y_ref, ubuf):
        i = pl.program_id(0)
        u = a_ref[...].astype(F32) * _sigmoid(g_ref[...].astype(F32))
        uh = ah_ref[...].astype(F32) * _sigmoid(gh_ref[...].astype(F32))
        ubuf[0:HALO, :] = jnp.where(i == 0, 0.0, uh)
        ubuf[HALO:HALO + tt, :] = u
        acc = jnp.broadcast_to(cb_ref[...], (tt, C))
        for k in range(CONV_K):
            acc = acc + w_ref[k:k + 1, :] * ubuf[pl.ds(HALO - (CONV_K - 1) + k, tt), :]
        ypre_ref[...] = acc
        mu = jnp.mean(acc, axis=-1, keepdims=True)
        d = acc - mu
        rstd = lax.rsqrt(jnp.mean(d * d, axis=-1, keepdims=True) + LN_EPS)
        z = d * rstd * lg_ref[...] + lb_ref[...]
        y_ref[...] = (z * _sigmoid(z)).astype(BF16)

    vec = pl.BlockSpec((1, C), lambda i: (0, 0))
    return _pcall(
        body, name="conv_fwd", grid=(T // tt,),
        in_specs=[pl.BlockSpec((tt, C), lambda i: (i, 0)), pl.BlockSpec((tt, C), lambda i: (i, 1)),
                  pl.BlockSpec((HALO, C), lambda i: (jnp.maximum(i * hb - 1, 0), 0)),
                  pl.BlockSpec((HALO, C), lambda i: (jnp.maximum(i * hb - 1, 0), 1)),
                  pl.BlockSpec((HALO, C), lambda i: (0, 0)), vec, vec, vec],
        out_specs=[pl.BlockSpec((tt, C), lambda i: (i, 0))] * 2,
        out_shape=[jax.ShapeDtypeStruct((T, C), F32), jax.ShapeDtypeStruct((T, C), BF16)],
        scratch=[pltpu.VMEM((tt + HALO, C), F32)],
    )(proj, proj, proj, proj, conv_w, conv_b, ln_g, ln_b)


def conv_bwd(proj, ypre, dycat, conv_w, ln_g, ln_b):
    T = proj.shape[0]
    C = conv_w.shape[1]
    tt = _tile(T, 256, HALO)
    hb = tt // HALO
    nt = T // tt
    last_h = T // HALO - 1

    def ln_bwd(ypre_v, dout_v, lg, lb):
        mu = jnp.mean(ypre_v, axis=-1, keepdims=True)
        d = ypre_v - mu
        rstd = lax.rsqrt(jnp.mean(d * d, axis=-1, keepdims=True) + LN_EPS)
        yh = d * rstd
        z = yh * lg + lb
        s = _sigmoid(z)
        dz = dout_v * (s * (1.0 + z * (1.0 - s)))
        dyh = dz * lg
        dy = rstd * (dyh - jnp.mean(dyh, axis=-1, keepdims=True)
                     - yh * jnp.mean(dyh * yh, axis=-1, keepdims=True))
        return dy, dz, yh

    def body(a_ref, g_ref, ah_ref, gh_ref, yp_ref, ypn_ref, do_ref, don_ref, w_ref, lg_ref, lb_ref,
             dag_ref, dw_ref, dcb_ref, dlg_ref, dlb_ref, ubuf, dybuf):
        i = pl.program_id(0)
        av = a_ref[...].astype(F32)
        sg = _sigmoid(g_ref[...].astype(F32))
        uh = ah_ref[...].astype(F32) * _sigmoid(gh_ref[...].astype(F32))
        ubuf[0:HALO, :] = jnp.where(i == 0, 0.0, uh)
        ubuf[HALO:HALO + tt, :] = av * sg
        lg, lb = lg_ref[...], lb_ref[...]
        dy, dz, yh = ln_bwd(yp_ref[...], do_ref[...].astype(F32), lg, lb)
        dyn, _, _ = ln_bwd(ypn_ref[...], don_ref[...].astype(F32), lg, lb)
        dybuf[0:tt, :] = dy
        dybuf[tt:tt + HALO, :] = jnp.where(i == nt - 1, 0.0, dyn)

        @pl.when(i == 0)
        def _():
            dw_ref[...] = jnp.zeros_like(dw_ref)
            dcb_ref[...] = jnp.zeros_like(dcb_ref)
            dlg_ref[...] = jnp.zeros_like(dlg_ref)
            dlb_ref[...] = jnp.zeros_like(dlb_ref)

        du = jnp.zeros((tt, C), F32)
        for k in range(CONV_K):
            du = du + w_ref[k:k + 1, :] * dybuf[pl.ds(CONV_K - 1 - k, tt), :]
            dw_ref[k:k + 1, :] += jnp.sum(dy * ubuf[pl.ds(HALO - (CONV_K - 1) + k, tt), :], axis=0, keepdims=True)
        dcb_ref[...] += jnp.sum(dy, axis=0, keepdims=True)
        dlg_ref[...] += jnp.sum(dz * yh, axis=0, keepdims=True)
        dlb_ref[...] += jnp.sum(dz, axis=0, keepdims=True)

        dag_ref[:, 0:C] = (du * sg).astype(BF16)
        dag_ref[:, C:2 * C] = (du * av * sg * (1.0 - sg)).astype(BF16)

    vec = pl.BlockSpec((1, C), lambda i: (0, 0))
    prev_h = lambda col: pl.BlockSpec((HALO, C), lambda i: (jnp.maximum(i * hb - 1, 0), col))
    next_h = pl.BlockSpec((HALO, C), lambda i: (jnp.minimum((i + 1) * hb, last_h), 0))
    return _pcall(
        body, name="conv_bwd", grid=(nt,),
        in_specs=[pl.BlockSpec((tt, C), lambda i: (i, 0)), pl.BlockSpec((tt, C), lambda i: (i, 1)),
                  prev_h(0), prev_h(1),
                  pl.BlockSpec((tt, C), lambda i: (i, 0)), next_h,
                  pl.BlockSpec((tt, C), lambda i: (i, 0)), next_h,
                  pl.BlockSpec((HALO, C), lambda i: (0, 0)), vec, vec],
        out_specs=[pl.BlockSpec((tt, 2 * C), lambda i: (i, 0)), pl.BlockSpec((HALO, C), lambda i: (0, 0)),
                   vec, vec, vec],
        out_shape=[jax.ShapeDtypeStruct((T, 2 * C), BF16), jax.ShapeDtypeStruct((HALO, C), F32),
                   jax.ShapeDtypeStruct((1, C), F32), jax.ShapeDtypeStruct((1, C), F32),
                   jax.ShapeDtypeStruct((1, C), F32)],
        scratch=[pltpu.VMEM((tt + HALO, C), F32), pltpu.VMEM((tt + HALO, C), F32)],
    )(proj, proj, proj, proj, ypre, ypre, dycat, dycat, conv_w, ln_g, ln_b)


def attn_fwd(q, k, v, cq, ck4, comm=None):
    H, T, dh = q.shape
    nkv, tk = ck4.shape[1], ck4.shape[3]
    tq = tk

    def body(q_ref, k_ref, v_ref, cq_ref, ck_ref, o_ref, lse_ref):
        i = pl.program_id(1)
        qv = q_ref[...]
        cqv = cq_ref[...]

        def step(j, carry, diagonal):
            m, l, acc = carry
            off = pl.multiple_of(j * tk, tk)
            kj = k_ref[pl.ds(off, tk), :]
            vj = v_ref[pl.ds(off, tk), :]
            s = lax.dot_general(qv, kj, NT, preferred_element_type=F32)
            s = s + cqv - ck_ref[j]
            if diagonal:
                keep = (lax.broadcasted_iota(jnp.int32, (tq, tk), 0)
                        >= lax.broadcasted_iota(jnp.int32, (tq, tk), 1))
                s = jnp.where(keep, s, NEG_INF)
            m_new = jnp.maximum(m, jnp.max(s, axis=-1, keepdims=True))
            alpha = jnp.exp(m - m_new)
            p = jnp.exp(s - m_new)
            l = alpha * l + jnp.sum(p, axis=-1, keepdims=True)
            acc = alpha * acc + jnp.dot(p.astype(BF16), vj, preferred_element_type=F32)
            return m_new, l, acc

        init = (jnp.full((tq, 1), -jnp.inf, F32), jnp.zeros((tq, 1), F32), jnp.zeros((tq, dh), F32))
        carry = lax.fori_loop(0, i, functools.partial(step, diagonal=False), init)
        m, l, acc = step(i, carry, True)
        o_ref[...] = acc / l
        lse_ref[...] = m + jnp.log(l)

    return _pcall(
        body, name="attn_fwd", grid=(H, T // tq),
        in_specs=[pl.BlockSpec((None, tq, dh), lambda h, i: (h, i, 0)),
                  pl.BlockSpec((None, T, dh), lambda h, i: (h, 0, 0)),
                  pl.BlockSpec((None, T, dh), lambda h, i: (h, 0, 0)),
                  pl.BlockSpec((None, tq, 1), lambda h, i: (h, i, 0)),
                  pl.BlockSpec((None, nkv, 1, tk), lambda h, i: (h, 0, 0, 0))],
        out_specs=[pl.BlockSpec((None, tq, dh), lambda h, i: (h, i, 0)),
                   pl.BlockSpec((None, tq, 1), lambda h, i: (h, i, 0))],
        out_shape=[jax.ShapeDtypeStruct((H, T, dh), F32), jax.ShapeDtypeStruct((H, T, 1), F32)],
        comm=comm,
    )(q, k, v, cq, ck4)


def attn_bwd(q, k, v, o, do, lse, cq, ck4, comm=None):
    H, T, dh = q.shape
    nkv, tk = ck4.shape[1], ck4.shape[3]
    tq = tk
    nq = T // tq
    scale = 1.0 / math.sqrt(dh)

    def body(q_ref, k_ref, v_ref, o_ref, do_ref, lse_ref, cq_ref, ck_ref,
             dq_ref, dk_ref, dv_ref, dcq_ref, dck_ref):
        j = pl.program_id(1)
        i = pl.program_id(2)

        def block(diagonal):
            qv, kv, vv = q_ref[...], k_ref[...], v_ref[...]
            dov = do_ref[...]
            dob = dov.astype(BF16)
            delta = jnp.sum(dov * o_ref[...], axis=-1, keepdims=True)
            s = lax.dot_general(qv, kv, NT, preferred_element_type=F32)
            s = s + cq_ref[...] - ck_ref[...]
            if diagonal:
                keep = (lax.broadcasted_iota(jnp.int32, (tq, tk), 0)
                        >= lax.broadcasted_iota(jnp.int32, (tq, tk), 1))
                s = jnp.where(keep, s, NEG_INF)
            p = jnp.exp(s - lse_ref[...])
            dp = lax.dot_general(dob, vv, NT, preferred_element_type=F32)
            ds = p * (dp - delta)
            dsb = ds.astype(BF16)
            dv_part = lax.dot_general(p.astype(BF16), dob, TN, preferred_element_type=F32)
            dk_part = lax.dot_general(dsb, qv, TN, preferred_element_type=F32)
            dq_part = jnp.dot(dsb, kv, preferred_element_type=F32) * scale
            dck_part = -jnp.sum(ds, axis=0, keepdims=True)
            dcq_part = jnp.sum(ds, axis=-1, keepdims=True)
            rows = pl.ds(pl.multiple_of(i * tq, tq), tq)

            @pl.when(j == 0)
            def _():
                dq_ref[rows, :] = dq_part
                dcq_ref[rows, :] = dcq_part

            @pl.when(j > 0)
            def _():
                dq_ref[rows, :] += dq_part
                dcq_ref[rows, :] += dcq_part

            if diagonal:
                dk_ref[...] = dk_part
                dv_ref[...] = dv_part
                dck_ref[...] = dck_part
            else:
                dk_ref[...] += dk_part
                dv_ref[...] += dv_part
                dck_ref[...] += dck_part

        @pl.when(i == j)
        def _():
            block(True)

        @pl.when(i > j)
        def _():
            block(False)

    qi = lambda h, j, i: (h, jnp.maximum(i, j), 0)
    q_spec = pl.BlockSpec((None, tq, dh), qi)
    col_spec = pl.BlockSpec((None, tq, 1), qi)
    kv_spec = pl.BlockSpec((None, tk, dh), lambda h, j, i: (h, j, 0))
    ck_spec = pl.BlockSpec((None, None, 1, tk), lambda h, j, i: (h, j, 0, 0))
    return _pcall(
        body, name="attn_bwd", grid=(H, nkv, nq),
        in_specs=[q_spec, kv_spec, kv_spec, q_spec, q_spec, col_spec, col_spec, ck_spec],
        out_specs=[pl.BlockSpec((None, T, dh), lambda h, j, i: (h, 0, 0)), kv_spec, kv_spec,
                   pl.BlockSpec((None, T, 1), lambda h, j, i: (h, 0, 0)), ck_spec],
        out_shape=[jax.ShapeDtypeStruct((H, T, dh), F32)] * 3
        + [jax.ShapeDtypeStruct((H, T, 1), F32), jax.ShapeDtypeStruct((H, nkv, 1, tk), F32)],
        comm=comm,
    )(q, k, v, o, do, lse, cq, ck4)


def _rows_tile(rows, cols, bytes_per_row_set):
    target = max(8, (2 * 1024 * 1024) // max(1, bytes_per_row_set))
    if rows <= target:
        return rows
    t = (target // 16) * 16
    while t >= 16:
        if rows % t == 0:
            return t
        t -= 16
    return rows


def sum_chips(recv):
    nc, R, C = recv.shape
    tr = _rows_tile(R, C, C * 4)

    def body(r_ref, o_ref):
        acc = r_ref[0].astype(F32)
        for j in range(1, nc):
            acc = acc + r_ref[j].astype(F32)
        o_ref[...] = acc

    return _pcall(
        body, name="sum_chips", grid=(R // tr,),
        in_specs=[pl.BlockSpec((nc, tr, C), lambda i: (0, i, 0))],
        out_specs=[pl.BlockSpec((tr, C), lambda i: (i, 0))],
        out_shape=[jax.ShapeDtypeStruct((R, C), F32)],
    )(recv)[0]


def adamw(w, m, v, g_parts, comm=None):
    R, C = w.shape
    tr = _rows_tile(R, C, C * 4 * 4)
    n_g = len(g_parts)
    c1 = 1.0 - ADAM_B1
    c2 = 1.0 - ADAM_B2
    bc1 = 1.0 - ADAM_B1 ** ADAM_STEP
    bc2 = 1.0 - ADAM_B2 ** ADAM_STEP

    def body(*refs):
        w_ref, m_ref, v_ref = refs[:3]
        g_refs = refs[3:3 + n_g]
        g_out, d_out, m_out, v_out = refs[3 + n_g:]
        g = g_refs[0][...]
        for r in g_refs[1:]:
            g = g + r[...]
        m_new = ADAM_B1 * m_ref[...] + c1 * g
        v_new = ADAM_B2 * v_ref[...] + c2 * (g * g)
        m_hat = m_new / bc1
        v_hat = v_new / bc2
        g_out[...] = g
        d_out[...] = -ADAM_LR * (m_hat / (jnp.sqrt(v_hat) + ADAM_EPS) + ADAM_WD * w_ref[...])
        m_out[...] = m_new
        v_out[...] = v_new

    spec = pl.BlockSpec((tr, C), lambda i: (i, 0))
    return _pcall(
        body, name="adamw", grid=(R // tr,),
        in_specs=[spec] * (3 + n_g), out_specs=[spec] * 4,
        out_shape=[jax.ShapeDtypeStruct((R, C), F32)] * 4, comm=comm,
    )(w, m, v, *g_parts)


def _chip_coords():
    x, y, c = lax.axis_index("x"), lax.axis_index("y"), lax.axis_index("c")
    others = [(1 - x, y), (x, 1 - y), (1 - x, 1 - y)]
    return x, y, c, others


def _remote(src, dst, send_sem, recv_sem, device):
    return pltpu.make_async_remote_copy(src_ref=src, dst_ref=dst, send_sem=send_sem, recv_sem=recv_sem,
                                        device_id=device, device_id_type=MESH)


def gather_comm(shards):
    n = len(shards)

    def copies(ins, outs, sems):
        send_sems, recv_sems, local_sems = sems
        x, y, c, others = _chip_coords()
        me = 2 * x + y
        local, ici, landed, fwd, theirs = [], [], [], [], []
        for a in range(n):
            hr = ins[a].shape[0] // 2
            local.append(pltpu.make_async_copy(ins[a], outs[a].at[me], local_sems.at[a]))
            for jj, (ox, oy) in enumerate(others):
                k = 6 * a + jj
                ici.append(_remote(ins[a].at[pl.ds(c * hr, hr)], outs[a].at[me, pl.ds(c * hr, hr)],
                                   send_sems.at[k], recv_sems.at[k], (ox, oy, c)))
                got = outs[a].at[2 * ox + oy, pl.ds(c * hr, hr)]
                landed.append(_remote(got, got, send_sems.at[k], recv_sems.at[k], (ox, oy, c)))
                fwd.append(_remote(got, got, send_sems.at[k + 3], recv_sems.at[k + 3], (x, y, 1 - c)))
                sib = outs[a].at[2 * ox + oy, pl.ds((1 - c) * hr, hr)]
                theirs.append(_remote(sib, sib, send_sems.at[k + 3], recv_sems.at[k + 3], (x, y, 1 - c)))
        return local, ici, landed, fwd, theirs

    def start(ins, outs, sems):
        local, ici, _, _, _ = copies(ins, outs, sems)
        for cp in local + ici:
            cp.start()

    def finish(ins, outs, sems):
        local, ici, landed, fwd, theirs = copies(ins, outs, sems)
        for got, cp in zip(landed, fwd):
            got.wait_recv()
            cp.start()
        for cp in theirs:
            cp.wait_recv()
        for cp in ici + fwd:
            cp.wait_send()
        for cp in local:
            cp.wait()

    return Comm(shards, [jax.ShapeDtypeStruct((N_CHIP,) + s.shape, s.dtype) for s in shards],
                [pltpu.SemaphoreType.DMA((6 * n,)), pltpu.SemaphoreType.DMA((6 * n,)),
                 pltpu.SemaphoreType.DMA((n,))], start, finish)


def scatter_comm(grads):
    n = len(grads)

    def copies(ins, outs, sems):
        send_sems, recv_sems, local_sems = sems
        x, y, c, others = _chip_coords()
        me = 2 * x + y
        local, ici, landed = [], [], []
        for a in range(n):
            local.append(pltpu.make_async_copy(ins[a].at[me], outs[a].at[me], local_sems.at[a]))
            for jj, (ox, oy) in enumerate(others):
                k = 3 * a + jj
                ici.append(_remote(ins[a].at[2 * ox + oy], outs[a].at[me], send_sems.at[k], recv_sems.at[k],
                                   (ox, oy, c)))
                slot = outs[a].at[2 * ox + oy]
                landed.append(_remote(slot, slot, send_sems.at[k], recv_sems.at[k], (ox, oy, c)))
        return local, ici, landed

    def start(ins, outs, sems):
        local, ici, _ = copies(ins, outs, sems)
        for cp in local + ici:
            cp.start()

    def finish(ins, outs, sems):
        local, ici, landed = copies(ins, outs, sems)
        for cp in landed:
            cp.wait_recv()
        for cp in ici:
            cp.wait_send()
        for cp in local:
            cp.wait()

    return Comm(grads, [jax.ShapeDtypeStruct(g.shape, g.dtype) for g in grads],
                [pltpu.SemaphoreType.DMA((3 * n,)), pltpu.SemaphoreType.DMA((3 * n,)),
                 pltpu.SemaphoreType.DMA((n,))], start, finish)


def swap_comm(parts):
    n = len(parts)

    def copies(ins, outs, sems):
        send_sems, recv_sems = sems
        x, y, c, _ = _chip_coords()
        return [_remote(ins[a], outs[a], send_sems.at[a], recv_sems.at[a], (x, y, 1 - c)) for a in range(n)]

    def start(ins, outs, sems):
        for cp in copies(ins, outs, sems):
            cp.start()

    def finish(ins, outs, sems):
        for cp in copies(ins, outs, sems):
            cp.wait()

    return Comm(parts, [jax.ShapeDtypeStruct(p.shape, p.dtype) for p in parts],
                [pltpu.SemaphoreType.DMA((n,)), pltpu.SemaphoreType.DMA((n,))], start, finish)


def allreduce_small(v):
    R = v.shape[0]

    def body(v_ref, sum_ref, all_ref, send_sems, recv_sems):
        x, y, c = lax.axis_index("x"), lax.axis_index("y"), lax.axis_index("c")
        me = 4 * x + 2 * y + c
        all_ref[me] = v_ref[...]
        copies = []
        for k in range(1, N_DEV):
            px = 1 - x if k & 4 else x
            py = 1 - y if k & 2 else y
            pc = 1 - c if k & 1 else c
            cp = pltpu.make_async_remote_copy(
                src_ref=v_ref, dst_ref=all_ref.at[me], send_sem=send_sems.at[k - 1], recv_sem=recv_sems.at[k - 1],
                device_id=(px, py, pc), device_id_type=MESH)
            cp.start()
            copies.append((cp, 4 * px + 2 * py + pc))
        for k, (cp, peer) in enumerate(copies):
            pltpu.make_async_remote_copy(
                src_ref=v_ref, dst_ref=all_ref.at[peer], send_sem=send_sems.at[k], recv_sem=recv_sems.at[k],
                device_id=(x, y, c), device_id_type=MESH).wait_recv()
        for cp, _ in copies:
            cp.wait_send()
        acc = all_ref[0]
        for d in range(1, N_DEV):
            acc = acc + all_ref[d]
        sum_ref[...] = acc

    vm = pl.BlockSpec(memory_space=pltpu.VMEM)
    return pl.pallas_call(
        body, name="allreduce_small",
        in_specs=[vm], out_specs=[vm, vm],
        out_shape=[jax.ShapeDtypeStruct((R, LANES), F32), jax.ShapeDtypeStruct((N_DEV, R, LANES), F32)],
        scratch_shapes=[pltpu.SemaphoreType.DMA((N_DEV - 1,)), pltpu.SemaphoreType.DMA((N_DEV - 1,))],
    )(v)[0]


def _heads_first(t, n_heads):
    T = t.shape[0]
    return t.reshape(T, n_heads, HEAD_DIM).transpose(1, 0, 2)


def _heads_last(t):
    H, T, dh = t.shape
    return t.transpose(1, 0, 2).reshape(T, H * dh)


SMALL_NAMES = ("ffn1_norm", "mix_norm", "ffn2_norm", "final_norm", "conv_b", "conv_ln_g", "conv_ln_b")


def _pack_small(vecs, bias, conv_w_rows, loss_tile):
    rows = [vecs[n].reshape(-1, LANES) for n in SMALL_NAMES]
    rows.append(bias.reshape(1, LANES))
    rows.append(conv_w_rows.reshape(-1, LANES))
    rows.append(loss_tile[0:1, :])
    packed = jnp.concatenate(rows, axis=0)
    pad = (-packed.shape[0]) % 8
    return jnp.pad(packed, ((0, pad), (0, 0)))


def _unpack_small(packed, sizes, n_conv_rows):
    out, r = {}, 0
    for n in SMALL_NAMES:
        k = sizes[n] // LANES
        out[n] = packed[r:r + k].reshape(-1)
        r += k
    out["fgate_bias"] = packed[r]
    r += 1
    out["conv_w"] = packed[r:r + n_conv_rows]
    r += n_conv_rows
    out["loss"] = packed[r, 0]
    return out


def kernel(x, ffn1_norm, ffn1_w_gate, ffn1_w_up, ffn1_w_down, mix_norm, w_in, fgate_bias, conv_w, conv_b, conv_ln_g, conv_ln_b, w_out, ffn2_norm, ffn2_w_gate, ffn2_w_up, ffn2_w_down, final_norm, loss_target, m_ffn1_norm, m_ffn1_w_gate, m_ffn1_w_up, m_ffn1_w_down, m_mix_norm, m_w_in, m_fgate_bias, m_conv_w, m_conv_b, m_conv_ln_g, m_conv_ln_b, m_w_out, m_ffn2_norm, m_ffn2_w_gate, m_ffn2_w_up, m_ffn2_w_down, m_final_norm, v_ffn1_norm, v_ffn1_w_gate, v_ffn1_w_up, v_ffn1_w_down, v_mix_norm, v_w_in, v_fgate_bias, v_conv_w, v_conv_b, v_conv_ln_g, v_conv_ln_b, v_w_out, v_ffn2_norm, v_ffn2_w_gate, v_ffn2_w_up, v_ffn2_w_down, v_final_norm):
    w = dict(ffn1_norm=ffn1_norm, ffn1_w_gate=ffn1_w_gate, ffn1_w_up=ffn1_w_up, ffn1_w_down=ffn1_w_down,
             mix_norm=mix_norm, w_in=w_in, fgate_bias=fgate_bias, conv_w=conv_w, conv_b=conv_b,
             conv_ln_g=conv_ln_g, conv_ln_b=conv_ln_b, w_out=w_out, ffn2_norm=ffn2_norm,
             ffn2_w_gate=ffn2_w_gate, ffn2_w_up=ffn2_w_up, ffn2_w_down=ffn2_w_down, final_norm=final_norm)
    m = dict(ffn1_norm=m_ffn1_norm, ffn1_w_gate=m_ffn1_w_gate, ffn1_w_up=m_ffn1_w_up, ffn1_w_down=m_ffn1_w_down,
             mix_norm=m_mix_norm, w_in=m_w_in, fgate_bias=m_fgate_bias, conv_w=m_conv_w, conv_b=m_conv_b,
             conv_ln_g=m_conv_ln_g, conv_ln_b=m_conv_ln_b, w_out=m_w_out, ffn2_norm=m_ffn2_norm,
             ffn2_w_gate=m_ffn2_w_gate, ffn2_w_up=m_ffn2_w_up, ffn2_w_down=m_ffn2_w_down, final_norm=m_final_norm)
    v = dict(ffn1_norm=v_ffn1_norm, ffn1_w_gate=v_ffn1_w_gate, ffn1_w_up=v_ffn1_w_up, ffn1_w_down=v_ffn1_w_down,
             mix_norm=v_mix_norm, w_in=v_w_in, fgate_bias=v_fgate_bias, conv_w=v_conv_w, conv_b=v_conv_b,
             conv_ln_g=v_conv_ln_g, conv_ln_b=v_conv_ln_b, w_out=v_w_out, ffn2_norm=v_ffn2_norm,
             ffn2_w_gate=v_ffn2_w_gate, ffn2_w_up=v_ffn2_w_up, ffn2_w_down=v_ffn2_w_down, final_norm=v_final_norm)
    names = list(w.keys())
    big = ("ffn1_w_gate", "ffn1_w_up", "ffn1_w_down", "w_in", "w_out", "ffn2_w_gate", "ffn2_w_up", "ffn2_w_down")

    T, D = x.shape[1], x.shape[2]
    C = conv_b.shape[0]
    H = fgate_bias.shape[0]
    cs = conv_w.shape[1]
    in_cols = N_CHIP * w_in.shape[1]
    p_main = in_cols - H

    x0, tgt = x[0], loss_target[0]
    tk = _tile(T, 512, 128)
    nkv = T // tk
    hd = H * HEAD_DIM
    row = lambda a: a.reshape(1, -1)
    wb = {n: w[n].astype(BF16) for n in big}
    grad, delta, new_m, new_v = {}, {}, {}, {}

    def update(n, parts, comm=None):
        grad[n], delta[n], new_m[n], new_v[n] = adamw(w[n], m[n], v[n], parts, comm=comm)

    g0 = gather_comm([wb["ffn1_w_gate"], wb["ffn1_w_up"], jnp.pad(conv_w, ((0, HALO - CONV_K), (0, 0)))])
    wg1, wu1, conv_w4 = _run_comm("gather_first", g0)
    conv_w_full = conv_w4.transpose(1, 0, 2).reshape(HALO, C)
    h1, r1 = rms_fwd(x0, row(ffn1_norm))
    g1 = gather_comm([wb["ffn1_w_down"], wb["w_in"]])
    a1, b1, mid1 = ffn_up(h1, wg1, wu1, comm=g1)
    wd1, w_in4 = g1.results
    g2 = gather_comm([wb["w_out"]])
    x1 = mm_residual("ffn_down_g", mid1, wd1, x0, 0.5, comm=g2)[0]
    w_out3 = g2.results[0]

    w_in_full = w_in4.transpose(1, 0, 2).reshape(D, in_cols)
    w_main = w_in_full[:, :p_main]
    w_f = jnp.pad(w_in_full[:, p_main:], ((0, 0), (0, LANES - H)))
    bias_pad = jnp.pad(row(fgate_bias), ((0, 0), (0, LANES - H)))
    h2, r2 = rms_fwd(x1, row(mix_norm))
    proj = proj_main(h2, w_main)
    f, cum = fgate_fwd(h2, w_f, bias_pad, H)
    ypre, yconv = conv_fwd(proj, conv_w_full, row(conv_b), row(conv_ln_g), row(conv_ln_b))
    q = _heads_first(proj[:, 2 * C:2 * C + hd], H) * (1.0 / math.sqrt(HEAD_DIM))
    k = _heads_first(proj[:, 2 * C + hd:2 * C + 2 * hd], H)
    vv = _heads_first(proj[:, 2 * C + 2 * hd:], H)
    c_ht = cum[:, :H].T
    cq = c_ht[:, :, None]
    ck4 = c_ht.reshape(H, nkv, 1, tk)
    g3 = gather_comm([wb["ffn2_w_gate"], wb["ffn2_w_up"], wb["ffn2_w_down"]])
    o, lse = attn_fwd(q, k, vv, cq, ck4, comm=g3)
    wg2, wu2, wd2 = g3.results
    ycat = jnp.concatenate([yconv, _heads_last(o).astype(BF16)], axis=1)
    x2 = mm_residual("out_proj", ycat, w_out3, x1, 1.0)[0]

    h3, r3 = rms_fwd(x2, row(ffn2_norm))
    a2, b2, mid2 = ffn_up(h3, wg2, wu2)
    x3 = mm_residual("ffn_down", mid2, wd2, x2, 0.5)[0]
    dx3, dx3b, loss_tile, d_final = final_loss(x3, tgt, row(final_norm))

    da2, db2 = ffn_bwd_mid(dx3b, wd2, a2, b2)
    dwd2 = dw_rowshard("ffn_dwd", mid2, dx3b, N_CHIP)[0]
    s1 = scatter_comm([dwd2])
    dwg2, dwu2 = dw_colshard("ffn_dwgu_s", h3, [da2, db2], N_CHIP, comm=s1)
    s2 = scatter_comm([dwg2])
    dh3 = ffn_dh(da2, db2, wg2, wu2, comm=s2)
    dx2, dx2b, d_ffn2_norm = rms_bwd(dh3, x2, r3, row(ffn2_norm), dx3, 1.0)

    dycat = mm_nt_bf16("out_proj_dy", dx2b, w_out3.reshape(-1, D))
    dw_out3 = dw_rowshard("out_proj_dw", ycat, dx2b, N_CHIP)[0]
    do = _heads_first(dycat[:, C:], H).astype(F32)
    s3 = scatter_comm([dwu2, dw_out3])
    dq, dk, dv, dcq, dck4 = attn_bwd(q, k, vv, o, do, lse, cq, ck4, comm=s3)
    dc = jnp.pad((dcq[:, :, 0] + dck4.reshape(H, T)).T, ((0, 0), (0, LANES - H)))
    df, d_bias = fgate_bwd(dc, f, H)
    dag, d_conv_w, d_conv_b, d_ln_g, d_ln_b = conv_bwd(proj, ypre, dycat, conv_w_full, row(conv_ln_g),
                                                       row(conv_ln_b))
    dproj = jnp.concatenate([dag, _heads_last(dq).astype(BF16), _heads_last(dk).astype(BF16),
                             _heads_last(dv).astype(BF16)], axis=1)
    early = ("ffn2_w_down", "ffn2_w_gate", "ffn2_w_up", "w_out")
    early_sums = [sum_chips(r) for r in (s1.results[0], s2.results[0], s3.results[0], s3.results[1])]
    sw1 = swap_comm(early_sums)
    dh2 = proj_dh(dproj, w_main, df, w_f, comm=sw1)
    dw_main = dw_plain("proj_dw_main", h2, dproj)
    dw_f = dw_plain("proj_dw_f", h2, df)
    dw_in_full = jnp.concatenate([dw_main, dw_f[:, :H]], axis=1)
    dw_in3 = dw_in_full.reshape(D, N_CHIP, in_cols // N_CHIP).transpose(1, 0, 2).astype(BF16)
    dx1, dx1b, d_mix_norm = rms_bwd(dh2, x1, r2, row(mix_norm), dx2, 0.5)

    s4 = scatter_comm([dw_in3])
    da1, db1 = ffn_bwd_mid(dx1b, wd1, a1, b1, comm=s4)
    dwg1, dwu1 = dw_colshard("ffn_dwgu", h1, [da1, db1], N_CHIP)
    s5 = scatter_comm([dwg1])
    dwd1 = dw_rowshard("ffn_dwd_s", mid1, dx1b, N_CHIP, comm=s5)[0]
    s6 = scatter_comm([dwu1])
    dh1 = ffn_dh(da1, db1, wg1, wu1, comm=s6)
    grad_x, _, d_ffn1_norm = rms_bwd(dh1, x0, r1, row(ffn1_norm), dx1, 1.0)

    s7 = scatter_comm([dwd1])
    for i, (n, mine, other) in enumerate(zip(early, early_sums, sw1.results)):
        update(n, [mine, other], comm=s7 if i == 0 else None)
    late = ("w_in", "ffn1_w_gate", "ffn1_w_up", "ffn1_w_down")
    late_sums = [sum_chips(r) for r in (s4.results[0], s5.results[0], s6.results[0], s7.results[0])]
    late_theirs = _run_comm("swap_last", swap_comm(late_sums))
    for n, mine, other in zip(late, late_sums, late_theirs):
        update(n, [mine, other])

    gl = dict(ffn1_norm=d_ffn1_norm, mix_norm=d_mix_norm, ffn2_norm=d_ffn2_norm, final_norm=d_final,
              conv_b=d_conv_b, conv_ln_g=d_ln_g, conv_ln_b=d_ln_b)
    small_sizes = {n: w[n].shape[0] for n in SMALL_NAMES}
    packed = _pack_small(gl, d_bias, d_conv_w, loss_tile)
    red = _unpack_small(allreduce_small(packed), small_sizes, HALO * C // LANES)
    loss = red["loss"]
    my_chip = 2 * lax.axis_index("x") + lax.axis_index("y")
    g_conv_w = lax.dynamic_slice_in_dim(red["conv_w"].reshape(HALO, C)[:CONV_K], my_chip * cs, cs, axis=1)
    update("conv_w", [g_conv_w])
    vec_names = SMALL_NAMES + ("fgate_bias",)
    stack = lambda d: jnp.concatenate(
        [jnp.pad(d[n], (0, (-d[n].shape[0]) % LANES)).reshape(-1, LANES) for n in vec_names], axis=0)
    g_stack = jnp.concatenate([red[n].reshape(-1, LANES) for n in SMALL_NAMES] + [red["fgate_bias"][None, :]],
                              axis=0)
    outs = adamw(stack(w), stack(m), stack(v), [g_stack])
    r = 0
    for n in vec_names:
        size = w[n].shape[0]
        k = -(-size // LANES)
        for dst, src in zip((grad, delta, new_m, new_v), outs):
            dst[n] = src[r:r + k].reshape(-1)[:size]
        r += k

    return (loss, grad_x[None], *[grad[n] for n in names], *[delta[n] for n in names],
            *[new_m[n] for n in names], *[new_v[n] for n in names])
```

```python
import functools
import math

import jax
import jax.numpy as jnp
from jax import lax
from jax.experimental import pallas as pl
from jax.experimental.pallas import tpu as pltpu

F32 = jnp.float32
BF16 = jnp.bfloat16
NORM_EPS = 1e-6
LN_EPS = 1e-5
NEG_INF = -1e30
HEAD_DIM = 64
CONV_K = 31
HALO = 32
LANES = 128
N_CHIP = 4
N_DEV = 8
VMEM_LIMIT = 52 * 1024 * 1024
MESH = pl.DeviceIdType.MESH

ADAM_LR = 0.001
ADAM_B1 = 0.9
ADAM_B2 = 0.999
ADAM_EPS = 1e-08
ADAM_WD = 0.01
ADAM_STEP = 10

NN = (((1,), (0,)), ((), ()))
NT = (((1,), (1,)), ((), ()))
TN = (((0,), (0,)), ((), ()))


def _tile(n, pref, unit=128):
    if n <= pref:
        return n
    t = (pref // unit) * unit
    while t > 0:
        if n % t == 0:
            return t
        t -= unit
    raise ValueError(f"no tile for {n} under {pref}")


class Comm:
    def __init__(self, operands, out_shape, sems, start, finish):
        self.operands, self.out_shape, self.sems = list(operands), list(out_shape), list(sems)
        self.start, self.finish = start, finish
        self.results = None


def _pcall(body, *, name, grid, in_specs, out_specs, out_shape, scratch=(), comm=None):
    params = pltpu.CompilerParams(dimension_semantics=("arbitrary",) * len(grid), vmem_limit_bytes=VMEM_LIMIT)
    scratch = list(scratch)
    if comm is None:
        return pl.pallas_call(body, name=name, grid=grid, in_specs=in_specs, out_specs=out_specs,
                              out_shape=out_shape, scratch_shapes=scratch, compiler_params=params)
    n_in, n_out, n_s = len(in_specs), len(out_shape), len(scratch)
    n_ci, n_co = len(comm.operands), len(comm.out_shape)
    any_spec = pl.BlockSpec(memory_space=pl.ANY)

    def carried(*refs):
        ins, refs = refs[:n_in], refs[n_in:]
        c_ins, refs = refs[:n_ci], refs[n_ci:]
        outs, refs = refs[:n_out], refs[n_out:]
        c_outs, refs = refs[:n_co], refs[n_co:]
        scr, c_sems = refs[:n_s], refs[n_s:]
        first = pl.program_id(0) == 0
        last = pl.program_id(0) == grid[0] - 1
        for d in range(1, len(grid)):
            first = jnp.logical_and(first, pl.program_id(d) == 0)
            last = jnp.logical_and(last, pl.program_id(d) == grid[d] - 1)

        @pl.when(first)
        def _():
            comm.start(c_ins, c_outs, c_sems)

        body(*ins, *outs, *scr)

        @pl.when(last)
        def _():
            comm.finish(c_ins, c_outs, c_sems)

    call = pl.pallas_call(
        carried, name=name, grid=grid, in_specs=list(in_specs) + [any_spec] * n_ci,
        out_specs=list(out_specs) + [any_spec] * n_co, out_shape=list(out_shape) + comm.out_shape,
        scratch_shapes=scratch + comm.sems, compiler_params=params)

    def run(*operands):
        res = call(*operands, *comm.operands)
        comm.results = list(res[n_out:])
        return list(res[:n_out])

    return run


def _run_comm(name, comm):
    n_ci, n_co = len(comm.operands), len(comm.out_shape)
    any_spec = pl.BlockSpec(memory_space=pl.ANY)

    def body(*refs):
        c_ins, c_outs, c_sems = refs[:n_ci], refs[n_ci:n_ci + n_co], refs[n_ci + n_co:]
        comm.start(c_ins, c_outs, c_sems)
        comm.finish(c_ins, c_outs, c_sems)

    return pl.pallas_call(body, name=name, in_specs=[any_spec] * n_ci, out_specs=[any_spec] * n_co,
                          out_shape=comm.out_shape, scratch_shapes=comm.sems)(*comm.operands)


def _sigmoid(x):
    return 1.0 / (1.0 + jnp.exp(-x))


def _mm(name, *, grid, pairs, once_pairs=(), extra=(), out_shape, out_specs, acc_shapes, nk, kaxis, epilogue,
        comm=None):
    all_pairs = list(pairs) + list(once_pairs)
    n_p, n_o = len(pairs), len(once_pairs)
    n_e, n_out, n_acc = len(extra), len(out_shape), len(acc_shapes)

    def body(*refs):
        ab = refs[: 2 * (n_p + n_o)]
        ex = refs[2 * (n_p + n_o): 2 * (n_p + n_o) + n_e]
        outs = refs[2 * (n_p + n_o) + n_e: 2 * (n_p + n_o) + n_e + n_out]
        accs = refs[2 * (n_p + n_o) + n_e + n_out:]

        def dots(idx_range):
            vals = [None] * n_acc
            for p in idx_range:
                d = lax.dot_general(ab[2 * p][...], ab[2 * p + 1][...], all_pairs[p][4],
                                    preferred_element_type=F32)
                ai = all_pairs[p][5]
                vals[ai] = d if vals[ai] is None else vals[ai] + d
            return vals

        if nk == 1:
            vals = dots(range(n_p + n_o))
            epilogue(vals, ex, outs)
            return

        k = pl.program_id(kaxis)

        @pl.when(k == 0)
        def _():
            vals = dots(range(n_p + n_o))
            for ai in range(n_acc):
                accs[ai][...] = vals[ai]

        @pl.when(k > 0)
        def _():
            vals = dots(range(n_p))
            for ai in range(n_acc):
                if vals[ai] is not None:
                    accs[ai][...] += vals[ai]

        @pl.when(k == nk - 1)
        def _():
            epilogue([a[...] for a in accs], ex, outs)

    operands, in_specs = [], []
    for p in all_pairs:
        operands += [p[0], p[2]]
        in_specs += [p[1], p[3]]
    for arr, spec in extra:
        operands.append(arr)
        in_specs.append(spec)
    scratch = [pltpu.VMEM(s, F32) for s in acc_shapes] if nk > 1 else []
    return _pcall(body, name=name, grid=grid, in_specs=in_specs, out_specs=out_specs, out_shape=out_shape,
                  scratch=scratch, comm=comm)(*operands)


def rms_fwd(x, g):
    T, D = x.shape
    tt = _tile(T, 512, 8)

    def body(x_ref, g_ref, h_ref, r_ref):
        xv = x_ref[...]
        r = lax.rsqrt(jnp.mean(xv * xv, axis=-1, keepdims=True) + NORM_EPS)
        h_ref[...] = (xv * r * g_ref[...]).astype(BF16)
        r_ref[...] = r

    return _pcall(
        body, name="rms_fwd", grid=(T // tt,),
        in_specs=[pl.BlockSpec((tt, D), lambda i: (i, 0)), pl.BlockSpec((1, D), lambda i: (0, 0))],
        out_specs=[pl.BlockSpec((tt, D), lambda i: (i, 0)), pl.BlockSpec((tt, 1), lambda i: (i, 0))],
        out_shape=[jax.ShapeDtypeStruct((T, D), BF16), jax.ShapeDtypeStruct((T, 1), F32)],
    )(x, g)


def rms_bwd(dh, x, r, g, dres, out_scale):
    T, D = x.shape
    tt = _tile(T, 256, 8)

    def body(dh_ref, x_ref, r_ref, g_ref, dres_ref, dx_ref, dxb_ref, dg_ref):
        i = pl.program_id(0)
        xh = x_ref[...] * r_ref[...]
        dhv = dh_ref[...]
        dxh = dhv * g_ref[...]
        dx = dres_ref[...] + r_ref[...] * (dxh - xh * jnp.mean(dxh * xh, axis=-1, keepdims=True))
        dx_ref[...] = dx
        dxb_ref[...] = (out_scale * dx).astype(BF16)
        part = jnp.sum(dhv * xh, axis=0, keepdims=True)

        @pl.when(i == 0)
        def _():
            dg_ref[...] = part

        @pl.when(i > 0)
        def _():
            dg_ref[...] += part

    row = pl.BlockSpec((tt, D), lambda i: (i, 0))
    return _pcall(
        body, name="rms_bwd", grid=(T // tt,),
        in_specs=[row, row, pl.BlockSpec((tt, 1), lambda i: (i, 0)), pl.BlockSpec((1, D), lambda i: (0, 0)), row],
        out_specs=[row, row, pl.BlockSpec((1, D), lambda i: (0, 0))],
        out_shape=[jax.ShapeDtypeStruct((T, D), F32), jax.ShapeDtypeStruct((T, D), BF16),
                   jax.ShapeDtypeStruct((1, D), F32)],
    )(dh, x, r, g, dres)


def final_loss(x, tgt, g):
    T, D = x.shape
    tt = _tile(T, 256, 8)

    def body(x_ref, t_ref, g_ref, dx_ref, dxb_ref, loss_ref, dg_ref):
        i = pl.program_id(0)
        xv = x_ref[...]
        r = lax.rsqrt(jnp.mean(xv * xv, axis=-1, keepdims=True) + NORM_EPS)
        xh = xv * r
        err = xh * g_ref[...] - t_ref[...]
        part_loss = 0.5 * jnp.sum(jnp.mean(err * err, axis=-1, keepdims=True), axis=0, keepdims=True)
        dy = err * (1.0 / D)
        dxh = dy * g_ref[...]
        dx = r * (dxh - xh * jnp.mean(dxh * xh, axis=-1, keepdims=True))
        dx_ref[...] = dx
        dxb_ref[...] = (0.5 * dx).astype(BF16)
        part_g = jnp.sum(dy * xh, axis=0, keepdims=True)
        part_l = jnp.broadcast_to(part_loss, (8, LANES))

        @pl.when(i == 0)
        def _():
            dg_ref[...] = part_g
            loss_ref[...] = part_l

        @pl.when(i > 0)
        def _():
            dg_ref[...] += part_g
            loss_ref[...] += part_l

    row = pl.BlockSpec((tt, D), lambda i: (i, 0))
    return _pcall(
        body, name="final_loss", grid=(T // tt,),
        in_specs=[row, row, pl.BlockSpec((1, D), lambda i: (0, 0))],
        out_specs=[row, row, pl.BlockSpec((8, LANES), lambda i: (0, 0)), pl.BlockSpec((1, D), lambda i: (0, 0))],
        out_shape=[jax.ShapeDtypeStruct((T, D), F32), jax.ShapeDtypeStruct((T, D), BF16),
                   jax.ShapeDtypeStruct((8, LANES), F32), jax.ShapeDtypeStruct((1, D), F32)],
    )(x, tgt, g)


def ffn_gate(h, wg3, comm=None):
    T, D = h.shape
    nc, _, fs = wg3.shape
    tm = _tile(T, 512, 8)

    def epilogue(vals, ex, outs):
        outs[0][...] = vals[0].astype(BF16)

    return _mm("ffn_gate", grid=(nc, T // tm),
               pairs=[(h, pl.BlockSpec((tm, D), lambda j, i: (i, 0)),
                       wg3, pl.BlockSpec((None, D, fs), lambda j, i: (j, 0, 0)), NN, 0)],
               out_shape=[jax.ShapeDtypeStruct((T, nc * fs), BF16)],
               out_specs=[pl.BlockSpec((tm, fs), lambda j, i: (i, j))],
               acc_shapes=[(tm, fs)], nk=1, kaxis=None, epilogue=epilogue, comm=comm)[0]


def ffn_upmul(h, wu3, a, comm=None):
    T, D = h.shape
    nc, _, fs = wu3.shape
    tm = _tile(T, 512, 8)

    def epilogue(vals, ex, outs):
        b = vals[0]
        av = ex[0][...].astype(F32)
        outs[0][...] = b.astype(BF16)
        outs[1][...] = (av * _sigmoid(av) * b).astype(BF16)

    t_spec = pl.BlockSpec((tm, fs), lambda j, i: (i, j))
    o_shape = jax.ShapeDtypeStruct((T, nc * fs), BF16)
    return _mm("ffn_upmul", grid=(nc, T // tm),
               pairs=[(h, pl.BlockSpec((tm, D), lambda j, i: (i, 0)),
                       wu3, pl.BlockSpec((None, D, fs), lambda j, i: (j, 0, 0)), NN, 0)],
               extra=[(a, t_spec)], out_shape=[o_shape] * 2, out_specs=[t_spec] * 2,
               acc_shapes=[(tm, fs)], nk=1, kaxis=None, epilogue=epilogue, comm=comm)


def ffn_up(h, wg3, wu3, comm=None, rows=256):
    T, D = h.shape
    nc, _, fs = wg3.shape
    tm = _tile(T, rows, 8)

    def epilogue(vals, ex, outs):
        a, b = vals
        outs[0][...] = a.astype(BF16)
        outs[1][...] = b.astype(BF16)
        outs[2][...] = (a * _sigmoid(a) * b).astype(BF16)

    h_spec = pl.BlockSpec((tm, D), lambda j, i: (i, 0))
    w_spec = pl.BlockSpec((None, D, fs), lambda j, i: (j, 0, 0))
    o_spec = pl.BlockSpec((tm, fs), lambda j, i: (i, j))
    o_shape = jax.ShapeDtypeStruct((T, nc * fs), BF16)
    return _mm("ffn_up", grid=(nc, T // tm),
               pairs=[(h, h_spec, wg3, w_spec, NN, 0), (h, h_spec, wu3, w_spec, NN, 1)],
               out_shape=[o_shape] * 3, out_specs=[o_spec] * 3, acc_shapes=[(tm, fs)] * 2, nk=1, kaxis=None,
               epilogue=epilogue, comm=comm)


def mm_residual(name, a, b3, res, scale, comm=None, cols=1024):
    T = a.shape[0]
    nk, tk, N = b3.shape
    tm, tn = _tile(T, 512, 8), _tile(N, cols)

    def epilogue(vals, ex, outs):
        outs[0][...] = ex[0][...] + scale * vals[0]

    return _mm(name, grid=(T // tm, N // tn, nk),
               pairs=[(a, pl.BlockSpec((tm, tk), lambda i, n, k: (i, k)),
                       b3, pl.BlockSpec((None, tk, tn), lambda i, n, k: (k, 0, n)), NN, 0)],
               extra=[(res, pl.BlockSpec((tm, tn), lambda i, n, k: (i, n)))],
               out_shape=[jax.ShapeDtypeStruct((T, N), F32)],
               out_specs=[pl.BlockSpec((tm, tn), lambda i, n, k: (i, n))],
               acc_shapes=[(tm, tn)], nk=nk, kaxis=2, epilogue=epilogue, comm=comm)


def ffn_bwd_mid(dout, wd3, a, b, comm=None, rows=256):
    T, D = dout.shape
    nc, fs, _ = wd3.shape
    tm = _tile(T, rows, 8)

    def epilogue(vals, ex, outs):
        dm = vals[0]
        av = ex[0][...].astype(F32)
        bv = ex[1][...].astype(F32)
        s = _sigmoid(av)
        outs[0][...] = (dm * bv * (s * (1.0 + av * (1.0 - s)))).astype(BF16)
        outs[1][...] = (dm * (av * s)).astype(BF16)

    t_spec = pl.BlockSpec((tm, fs), lambda j, i: (i, j))
    o_shape = jax.ShapeDtypeStruct((T, nc * fs), BF16)
    return _mm("ffn_bwd_mid", grid=(nc, T // tm),
               pairs=[(dout, pl.BlockSpec((tm, D), lambda j, i: (i, 0)),
                       wd3, pl.BlockSpec((None, fs, D), lambda j, i: (j, 0, 0)), NT, 0)],
               extra=[(a, t_spec), (b, t_spec)],
               out_shape=[o_shape] * 2, out_specs=[t_spec] * 2, acc_shapes=[(tm, fs)], nk=1, kaxis=None,
               epilogue=epilogue, comm=comm)


def dw_rowshard(name, a, b, nc, comm=None, depth=512):
    T, M = a.shape
    N = b.shape[1]
    ms = M // nc
    tn, tk = _tile(N, 1024), _tile(T, depth, 16)

    def epilogue(vals, ex, outs):
        outs[0][...] = vals[0].astype(BF16)

    return _mm(name, grid=(nc, N // tn, T // tk),
               pairs=[(a, pl.BlockSpec((tk, ms), lambda j, n, k: (k, j)),
                       b, pl.BlockSpec((tk, tn), lambda j, n, k: (k, n)), TN, 0)],
               out_shape=[jax.ShapeDtypeStruct((nc, ms, N), BF16)],
               out_specs=[pl.BlockSpec((None, ms, tn), lambda j, n, k: (j, 0, n))],
               acc_shapes=[(ms, tn)], nk=T // tk, kaxis=2, epilogue=epilogue, comm=comm)


def dw_colshard(name, a, bs, nc, comm=None, depth=512):
    T, M = a.shape
    ns = bs[0].shape[1] // nc
    tm, tk = _tile(M, 512), _tile(T, depth, 16)

    def epilogue(vals, ex, outs):
        for v, o in zip(vals, outs):
            o[...] = v.astype(BF16)

    a_spec = pl.BlockSpec((tk, tm), lambda j, m, k: (k, m))
    b_spec = pl.BlockSpec((tk, ns), lambda j, m, k: (k, j))
    return _mm(name, grid=(nc, M // tm, T // tk),
               pairs=[(a, a_spec, b, b_spec, TN, p) for p, b in enumerate(bs)],
               out_shape=[jax.ShapeDtypeStruct((nc, M, ns), BF16)] * len(bs),
               out_specs=[pl.BlockSpec((None, tm, ns), lambda j, m, k: (j, m, 0))] * len(bs),
               acc_shapes=[(tm, ns)] * len(bs), nk=T // tk, kaxis=2, epilogue=epilogue, comm=comm)


def dw_plain(name, a, b):
    T, M = a.shape
    N = b.shape[1]
    tm, tn, tk = _tile(M, 1024), _tile(N, 1024), _tile(T, 512, 16)

    def epilogue(vals, ex, outs):
        outs[0][...] = vals[0]

    return _mm(name, grid=(N // tn, M // tm, T // tk),
               pairs=[(a, pl.BlockSpec((tk, tm), lambda n, m, k: (k, m)),
                       b, pl.BlockSpec((tk, tn), lambda n, m, k: (k, n)), TN, 0)],
               out_shape=[jax.ShapeDtypeStruct((M, N), F32)],
               out_specs=[pl.BlockSpec((tm, tn), lambda n, m, k: (m, n))],
               acc_shapes=[(tm, tn)], nk=T // tk, kaxis=2, epilogue=epilogue)[0]


def ffn_dh(da, db, wg3, wu3, comm=None, rows=512):
    T = da.shape[0]
    nc, D, fs = wg3.shape
    tm, tn = _tile(T, rows, 8), _tile(D, 1024)

    def epilogue(vals, ex, outs):
        outs[0][...] = vals[0]

    a_spec = pl.BlockSpec((tm, fs), lambda i, n, k: (i, k))
    w_spec = pl.BlockSpec((None, tn, fs), lambda i, n, k: (k, n, 0))
    return _mm("ffn_dh", grid=(T // tm, D // tn, nc),
               pairs=[(da, a_spec, wg3, w_spec, NT, 0), (db, a_spec, wu3, w_spec, NT, 0)],
               out_shape=[jax.ShapeDtypeStruct((T, D), F32)],
               out_specs=[pl.BlockSpec((tm, tn), lambda i, n, k: (i, n))],
               acc_shapes=[(tm, tn)], nk=nc, kaxis=2, epilogue=epilogue, comm=comm)[0]


def proj_main(h, w):
    T, D = h.shape
    P = w.shape[1]
    tm, tn = _tile(T, 512, 8), _tile(P, 1024)

    def epilogue(vals, ex, outs):
        outs[0][...] = vals[0].astype(BF16)

    return _mm("proj_main", grid=(P // tn, T // tm),
               pairs=[(h, pl.BlockSpec((tm, D), lambda j, i: (i, 0)),
                       w, pl.BlockSpec((D, tn), lambda j, i: (0, j)), NN, 0)],
               out_shape=[jax.ShapeDtypeStruct((T, P), BF16)],
               out_specs=[pl.BlockSpec((tm, tn), lambda j, i: (i, j))],
               acc_shapes=[(tm, tn)], nk=1, kaxis=None, epilogue=epilogue)[0]


def mm_nt_bf16(name, a, w):
    T, K = a.shape
    M = w.shape[0]
    tm, tn = _tile(T, 512, 8), _tile(M, 1024)

    def epilogue(vals, ex, outs):
        outs[0][...] = vals[0].astype(BF16)

    return _mm(name, grid=(T // tm, M // tn),
               pairs=[(a, pl.BlockSpec((tm, K), lambda i, n: (i, 0)),
                       w, pl.BlockSpec((tn, K), lambda i, n: (n, 0)), NT, 0)],
               out_shape=[jax.ShapeDtypeStruct((T, M), BF16)],
               out_specs=[pl.BlockSpec((tm, tn), lambda i, n: (i, n))],
               acc_shapes=[(tm, tn)], nk=1, kaxis=None, epilogue=epilogue)[0]


def proj_dh(dproj, w_main, df, w_f, comm=None):
    T, P = dproj.shape
    D = w_main.shape[0]
    tm, tn, tk = _tile(T, 512, 8), _tile(D, 1024), _tile(P, 1280)

    def epilogue(vals, ex, outs):
        outs[0][...] = vals[0]

    return _mm("proj_dh", grid=(T // tm, D // tn, P // tk),
               pairs=[(dproj, pl.BlockSpec((tm, tk), lambda i, n, k: (i, k)),
                       w_main, pl.BlockSpec((tn, tk), lambda i, n, k: (n, k)), NT, 0)],
               once_pairs=[(df, pl.BlockSpec((tm, LANES), lambda i, n, k: (i, 0)),
                            w_f, pl.BlockSpec((tn, LANES), lambda i, n, k: (n, 0)), NT, 0)],
               out_shape=[jax.ShapeDtypeStruct((T, D), F32)],
               out_specs=[pl.BlockSpec((tm, tn), lambda i, n, k: (i, n))],
               acc_shapes=[(tm, tn)], nk=P // tk, kaxis=2, epilogue=epilogue, comm=comm)[0]


def fgate_fwd(h, w_f, bias, n_heads):
    T, D = h.shape
    tt = _tile(T, 512, 8)

    def body(h_ref, w_ref, b_ref, f_ref, c_ref, carry):
        i = pl.program_id(0)

        @pl.when(i == 0)
        def _():
            carry[...] = jnp.zeros_like(carry)

        f = jnp.dot(h_ref[...], w_ref[...], preferred_element_type=F32) + b_ref[...]
        logf = jnp.minimum(f, 0.0) - jnp.log(1.0 + jnp.exp(-jnp.abs(f)))
        tri = (lax.broadcasted_iota(jnp.int32, (tt, tt), 0) >= lax.broadcasted_iota(jnp.int32, (tt, tt), 1))
        cs = jnp.dot(tri.astype(F32), logf, preferred_element_type=F32, precision=lax.Precision.HIGHEST)
        c = cs + carry[...]
        f_ref[...] = f
        c_ref[...] = c
        carry[...] = c[tt - 1:tt, :]

    row = pl.BlockSpec((tt, LANES), lambda i: (i, 0))
    return _pcall(
        body, name="fgate_fwd", grid=(T // tt,),
        in_specs=[pl.BlockSpec((tt, D), lambda i: (i, 0)), pl.BlockSpec((D, LANES), lambda i: (0, 0)),
                  pl.BlockSpec((1, LANES), lambda i: (0, 0))],
        out_specs=[row, row],
        out_shape=[jax.ShapeDtypeStruct((T, LANES), F32)] * 2,
        scratch=[pltpu.VMEM((1, LANES), F32)],
    )(h, w_f, bias)


def fgate_bwd(dc, f, n_heads):
    T = dc.shape[0]
    tt = _tile(T, 512, 8)
    nt = T // tt

    def body(dc_ref, f_ref, df_ref, db_ref, carry):
        i = pl.program_id(0)

        @pl.when(i == 0)
        def _():
            carry[...] = jnp.zeros_like(carry)

        tri = (lax.broadcasted_iota(jnp.int32, (tt, tt), 1) >= lax.broadcasted_iota(jnp.int32, (tt, tt), 0))
        rs = jnp.dot(tri.astype(F32), dc_ref[...], preferred_element_type=F32,
                     precision=lax.Precision.HIGHEST) + carry[...]
        carry[...] = rs[0:1, :]
        lane = lax.broadcasted_iota(jnp.int32, (tt, LANES), 1)
        df = jnp.where(lane < n_heads, rs * _sigmoid(-f_ref[...]), 0.0)
        df_ref[...] = df.astype(BF16)
        part = jnp.sum(df, axis=0, keepdims=True)

        @pl.when(i == 0)
        def _():
            db_ref[...] = part

        @pl.when(i > 0)
        def _():
            db_ref[...] += part

    rev = pl.BlockSpec((tt, LANES), lambda i: (nt - 1 - i, 0))
    return _pcall(
        body, name="fgate_bwd", grid=(nt,),
        in_specs=[rev, rev],
        out_specs=[rev, pl.BlockSpec((1, LANES), lambda i: (0, 0))],
        out_shape=[jax.ShapeDtypeStruct((T, LANES), BF16), jax.ShapeDtypeStruct((1, LANES), F32)],
        scratch=[pltpu.VMEM((1, LANES), F32)],
    )(dc, f)


SUBLANES = 8
SHIFT_ROWS = HALO - SUBLANES


def _shifted_copies(buf, sh, tt):
    for r in range(1, SUBLANES):
        sh[r - 1, 0:tt + SHIFT_ROWS, :] = buf[pl.ds(r, tt + SHIFT_ROWS), :]


def _tap(buf, sh, offset, tt):
    q, r = divmod(offset, SUBLANES)
    if r == 0:
        return buf[pl.ds(SUBLANES * q, tt), :]
    return sh[r - 1, pl.ds(SUBLANES * q, tt), :]


def conv_fwd(proj, conv_w, conv_b, ln_g, ln_b):
    T = proj.shape[0]
    C = conv_w.shape[1]
    tt = _tile(T, 256, HALO)
    hb = tt // HALO

    def body(a_ref, g_ref, ah_ref, gh_ref, w_ref, cb_ref, lg_ref, lb_ref, ypre_ref, y_ref, ubuf, ush):
        i = pl.program_id(0)
        u = a_ref[...].astype(F32) * _sigmoid(g_ref[...].astype(F32))
        uh = ah_ref[...].astype(F32) * _sigmoid(gh_ref[...].astype(F32))
        ubuf[0:HALO, :] = jnp.where(i == 0, 0.0, uh)
        ubuf[HALO:HALO + tt, :] = u
        _shifted_copies(ubuf, ush, tt)
        acc = jnp.broadcast_to(cb_ref[...], (tt, C))
        for k in range(CONV_K):
            acc = acc + w_ref[k:k + 1, :] * _tap(ubuf, ush, HALO - (CONV_K - 1) + k, tt)
        ypre_ref[...] = acc
        mu = jnp.mean(acc, axis=-1, keepdims=True)
        d = acc - mu
        rstd = lax.rsqrt(jnp.mean(d * d, axis=-1, keepdims=True) + LN_EPS)
        z = d * rstd * lg_ref[...] + lb_ref[...]
        y_ref[...] = (z * _sigmoid(z)).astype(BF16)

    vec = pl.BlockSpec((1, C), lambda i: (0, 0))
    return _pcall(
        body, name="conv_fwd", grid=(T // tt,),
        in_specs=[pl.BlockSpec((tt, C), lambda i: (i, 0)), pl.BlockSpec((tt, C), lambda i: (i, 1)),
                  pl.BlockSpec((HALO, C), lambda i: (jnp.maximum(i * hb - 1, 0), 0)),
                  pl.BlockSpec((HALO, C), lambda i: (jnp.maximum(i * hb - 1, 0), 1)),
                  pl.BlockSpec((HALO, C), lambda i: (0, 0)), vec, vec, vec],
        out_specs=[pl.BlockSpec((tt, C), lambda i: (i, 0))] * 2,
        out_shape=[jax.ShapeDtypeStruct((T, C), F32), jax.ShapeDtypeStruct((T, C), BF16)],
        scratch=[pltpu.VMEM((tt + HALO, C), F32), pltpu.VMEM((SUBLANES - 1, tt + SHIFT_ROWS, C), F32)],
    )(proj, proj, proj, proj, conv_w, conv_b, ln_g, ln_b)


def conv_bwd(proj, ypre, dycat, conv_w, ln_g, ln_b):
    T = proj.shape[0]
    C = conv_w.shape[1]
    tt = _tile(T, 256, HALO)
    hb = tt // HALO
    nt = T // tt
    last_h = T // HALO - 1

    def ln_bwd(ypre_v, dout_v, lg, lb):
        mu = jnp.mean(ypre_v, axis=-1, keepdims=True)
        d = ypre_v - mu
        rstd = lax.rsqrt(jnp.mean(d * d, axis=-1, keepdims=True) + LN_EPS)
        yh = d * rstd
        z = yh * lg + lb
        s = _sigmoid(z)
        dz = dout_v * (s * (1.0 + z * (1.0 - s)))
        dyh = dz * lg
        dy = rstd * (dyh - jnp.mean(dyh, axis=-1, keepdims=True)
                     - yh * jnp.mean(dyh * yh, axis=-1, keepdims=True))
        return dy, dz, yh

    def body(a_ref, g_ref, ah_ref, gh_ref, yp_ref, ypn_ref, do_ref, don_ref, w_ref, lg_ref, lb_ref,
             dag_ref, dw_ref, dcb_ref, dlg_ref, dlb_ref, ubuf, dybuf, ush, dysh):
        i = pl.program_id(0)
        av = a_ref[...].astype(F32)
        sg = _sigmoid(g_ref[...].astype(F32))
        uh = ah_ref[...].astype(F32) * _sigmoid(gh_ref[...].astype(F32))
        ubuf[0:HALO, :] = jnp.where(i == 0, 0.0, uh)
        ubuf[HALO:HALO + tt, :] = av * sg
        lg, lb = lg_ref[...], lb_ref[...]
        dy, dz, yh = ln_bwd(yp_ref[...], do_ref[...].astype(F32), lg, lb)
        dyn, _, _ = ln_bwd(ypn_ref[...], don_ref[...].astype(F32), lg, lb)
        dybuf[0:tt, :] = dy
        dybuf[tt:tt + HALO, :] = jnp.where(i == nt - 1, 0.0, dyn)
        _shifted_copies(ubuf, ush, tt)
        _shifted_copies(dybuf, dysh, tt)

        @pl.when(i == 0)
        def _():
            dw_ref[...] = jnp.zeros_like(dw_ref)
            dcb_ref[...] = jnp.zeros_like(dcb_ref)
            dlg_ref[...] = jnp.zeros_like(dlg_ref)
            dlb_ref[...] = jnp.zeros_like(dlb_ref)

        du = jnp.zeros((tt, C), F32)
        for k in range(CONV_K):
            du = du + w_ref[k:k + 1, :] * _tap(dybuf, dysh, CONV_K - 1 - k, tt)
            dw_ref[k:k + 1, :] += jnp.sum(dy * _tap(ubuf, ush, HALO - (CONV_K - 1) + k, tt), axis=0, keepdims=True)
        dcb_ref[...] += jnp.sum(dy, axis=0, keepdims=True)
        dlg_ref[...] += jnp.sum(dz * yh, axis=0, keepdims=True)
        dlb_ref[...] += jnp.sum(dz, axis=0, keepdims=True)

        dag_ref[:, 0:C] = (du * sg).astype(BF16)
        dag_ref[:, C:2 * C] = (du * av * sg * (1.0 - sg)).astype(BF16)

    vec = pl.BlockSpec((1, C), lambda i: (0, 0))
    prev_h = lambda col: pl.BlockSpec((HALO, C), lambda i: (jnp.maximum(i * hb - 1, 0), col))
    next_h = pl.BlockSpec((HALO, C), lambda i: (jnp.minimum((i + 1) * hb, last_h), 0))
    return _pcall(
        body, name="conv_bwd", grid=(nt,),
        in_specs=[pl.BlockSpec((tt, C), lambda i: (i, 0)), pl.BlockSpec((tt, C), lambda i: (i, 1)),
                  prev_h(0), prev_h(1),
                  pl.BlockSpec((tt, C), lambda i: (i, 0)), next_h,
                  pl.BlockSpec((tt, C), lambda i: (i, 0)), next_h,
                  pl.BlockSpec((HALO, C), lambda i: (0, 0)), vec, vec],
        out_specs=[pl.BlockSpec((tt, 2 * C), lambda i: (i, 0)), pl.BlockSpec((HALO, C), lambda i: (0, 0)),
                   vec, vec, vec],
        out_shape=[jax.ShapeDtypeStruct((T, 2 * C), BF16), jax.ShapeDtypeStruct((HALO, C), F32),
                   jax.ShapeDtypeStruct((1, C), F32), jax.ShapeDtypeStruct((1, C), F32),
                   jax.ShapeDtypeStruct((1, C), F32)],
        scratch=[pltpu.VMEM((tt + HALO, C), F32), pltpu.VMEM((tt + HALO, C), F32),
                 pltpu.VMEM((SUBLANES - 1, tt + SHIFT_ROWS, C), F32),
                 pltpu.VMEM((SUBLANES - 1, tt + SHIFT_ROWS, C), F32)],
    )(proj, proj, proj, proj, ypre, ypre, dycat, dycat, conv_w, ln_g, ln_b)


def attn_fwd(q, k, v, cq, ck4, comm=None):
    H, T, dh = q.shape
    nkv, tk = ck4.shape[1], ck4.shape[3]
    tq = tk

    def body(q_ref, k_ref, v_ref, cq_ref, ck_ref, o_ref, lse_ref):
        i = pl.program_id(1)
        qv = q_ref[...]
        cqv = cq_ref[...]

        def step(j, carry, diagonal):
            m, l, acc = carry
            off = pl.multiple_of(j * tk, tk)
            kj = k_ref[pl.ds(off, tk), :]
            vj = v_ref[pl.ds(off, tk), :]
            s = lax.dot_general(qv, kj, NT, preferred_element_type=F32)
            s = s + cqv - ck_ref[j]
            if diagonal:
                keep = (lax.broadcasted_iota(jnp.int32, (tq, tk), 0)
                        >= lax.broadcasted_iota(jnp.int32, (tq, tk), 1))
                s = jnp.where(keep, s, NEG_INF)
            m_new = jnp.maximum(m, jnp.max(s, axis=-1, keepdims=True))
            alpha = jnp.exp(m - m_new)
            p = jnp.exp(s - m_new)
            l = alpha * l + jnp.sum(p, axis=-1, keepdims=True)
            acc = alpha * acc + jnp.dot(p.astype(BF16), vj, preferred_element_type=F32)
            return m_new, l, acc

        init = (jnp.full((tq, 1), -jnp.inf, F32), jnp.zeros((tq, 1), F32), jnp.zeros((tq, dh), F32))
        carry = lax.fori_loop(0, i, functools.partial(step, diagonal=False), init)
        m, l, acc = step(i, carry, True)
        o_ref[...] = acc / l
        lse_ref[...] = m + jnp.log(l)

    return _pcall(
        body, name="attn_fwd", grid=(H, T // tq),
        in_specs=[pl.BlockSpec((None, tq, dh), lambda h, i: (h, i, 0)),
                  pl.BlockSpec((None, T, dh), lambda h, i: (h, 0, 0)),
                  pl.BlockSpec((None, T, dh), lambda h, i: (h, 0, 0)),
                  pl.BlockSpec((None, tq, 1), lambda h, i: (h, i, 0)),
                  pl.BlockSpec((None, nkv, 1, tk), lambda h, i: (h, 0, 0, 0))],
        out_specs=[pl.BlockSpec((None, tq, dh), lambda h, i: (h, i, 0)),
                   pl.BlockSpec((None, tq, 1), lambda h, i: (h, i, 0))],
        out_shape=[jax.ShapeDtypeStruct((H, T, dh), F32), jax.ShapeDtypeStruct((H, T, 1), F32)],
        comm=comm,
    )(q, k, v, cq, ck4)


def attn_bwd(q, k, v, o, do, lse, cq, ck4, comm=None):
    H, T, dh = q.shape
    nkv, tk = ck4.shape[1], ck4.shape[3]
    tq = tk
    nq = T // tq
    scale = 1.0 / math.sqrt(dh)

    def body(q_ref, k_ref, v_ref, o_ref, do_ref, lse_ref, cq_ref, ck_ref,
             dq_ref, dk_ref, dv_ref, dcq_ref, dck_ref):
        j = pl.program_id(1)
        i = pl.program_id(2)

        def block(diagonal):
            qv, kv, vv = q_ref[...], k_ref[...], v_ref[...]
            dov = do_ref[...]
            dob = dov.astype(BF16)
            delta = jnp.sum(dov * o_ref[...], axis=-1, keepdims=True)
            s = lax.dot_general(qv, kv, NT, preferred_element_type=F32)
            s = s + cq_ref[...] - ck_ref[...]
            if diagonal:
                keep = (lax.broadcasted_iota(jnp.int32, (tq, tk), 0)
                        >= lax.broadcasted_iota(jnp.int32, (tq, tk), 1))
                s = jnp.where(keep, s, NEG_INF)
            p = jnp.exp(s - lse_ref[...])
            dp = lax.dot_general(dob, vv, NT, preferred_element_type=F32)
            ds = p * (dp - delta)
            dsb = ds.astype(BF16)
            dv_part = lax.dot_general(p.astype(BF16), dob, TN, preferred_element_type=F32)
            dk_part = lax.dot_general(dsb, qv, TN, preferred_element_type=F32)
            dq_part = jnp.dot(dsb, kv, preferred_element_type=F32) * scale
            dck_part = -jnp.sum(ds, axis=0, keepdims=True)
            dcq_part = jnp.sum(ds, axis=-1, keepdims=True)
            rows = pl.ds(pl.multiple_of(i * tq, tq), tq)

            @pl.when(j == 0)
            def _():
                dq_ref[rows, :] = dq_part
                dcq_ref[rows, :] = dcq_part

            @pl.when(j > 0)
            def _():
                dq_ref[rows, :] += dq_part
                dcq_ref[rows, :] += dcq_part

            if diagonal:
                dk_ref[...] = dk_part
                dv_ref[...] = dv_part
                dck_ref[...] = dck_part
            else:
                dk_ref[...] += dk_part
                dv_ref[...] += dv_part
                dck_ref[...] += dck_part

        @pl.when(i == j)
        def _():
            block(True)

        @pl.when(i > j)
        def _():
            block(False)

    qi = lambda h, j, i: (h, jnp.maximum(i, j), 0)
    q_spec = pl.BlockSpec((None, tq, dh), qi)
    col_spec = pl.BlockSpec((None, tq, 1), qi)
    kv_spec = pl.BlockSpec((None, tk, dh), lambda h, j, i: (h, j, 0))
    ck_spec = pl.BlockSpec((None, None, 1, tk), lambda h, j, i: (h, j, 0, 0))
    return _pcall(
        body, name="attn_bwd", grid=(H, nkv, nq),
        in_specs=[q_spec, kv_spec, kv_spec, q_spec, q_spec, col_spec, col_spec, ck_spec],
        out_specs=[pl.BlockSpec((None, T, dh), lambda h, j, i: (h, 0, 0)), kv_spec, kv_spec,
                   pl.BlockSpec((None, T, 1), lambda h, j, i: (h, 0, 0)), ck_spec],
        out_shape=[jax.ShapeDtypeStruct((H, T, dh), F32)] * 3
        + [jax.ShapeDtypeStruct((H, T, 1), F32), jax.ShapeDtypeStruct((H, nkv, 1, tk), F32)],
        comm=comm,
    )(q, k, v, o, do, lse, cq, ck4)


def _rows_tile(rows, cols, bytes_per_row_set):
    target = max(8, (2 * 1024 * 1024) // max(1, bytes_per_row_set))
    if rows <= target:
        return rows
    t = (target // 16) * 16
    while t >= 16:
        if rows % t == 0:
            return t
        t -= 16
    return rows


def sum_chips(recv):
    nc, R, C = recv.shape
    tr = _rows_tile(R, C, C * 4)

    def body(r_ref, o_ref):
        acc = r_ref[0].astype(F32)
        for j in range(1, nc):
            acc = acc + r_ref[j].astype(F32)
        o_ref[...] = acc

    return _pcall(
        body, name="sum_chips", grid=(R // tr,),
        in_specs=[pl.BlockSpec((nc, tr, C), lambda i: (0, i, 0))],
        out_specs=[pl.BlockSpec((tr, C), lambda i: (i, 0))],
        out_shape=[jax.ShapeDtypeStruct((R, C), F32)],
    )(recv)[0]


def adamw(w, m, v, g_parts, comm=None):
    R, C = w.shape
    tr = _rows_tile(R, C, C * 4 * 4)
    n_g = len(g_parts)
    c1 = 1.0 - ADAM_B1
    c2 = 1.0 - ADAM_B2
    bc1 = 1.0 - ADAM_B1 ** ADAM_STEP
    bc2 = 1.0 - ADAM_B2 ** ADAM_STEP

    def body(*refs):
        w_ref, m_ref, v_ref = refs[:3]
        g_refs = refs[3:3 + n_g]
        g_out, d_out, m_out, v_out = refs[3 + n_g:]
        g = g_refs[0][...]
        for r in g_refs[1:]:
            g = g + r[...]
        m_new = ADAM_B1 * m_ref[...] + c1 * g
        v_new = ADAM_B2 * v_ref[...] + c2 * (g * g)
        m_hat = m_new / bc1
        v_hat = v_new / bc2
        g_out[...] = g
        d_out[...] = -ADAM_LR * (m_hat / (jnp.sqrt(v_hat) + ADAM_EPS) + ADAM_WD * w_ref[...])
        m_out[...] = m_new
        v_out[...] = v_new

    spec = pl.BlockSpec((tr, C), lambda i: (i, 0))
    return _pcall(
        body, name="adamw", grid=(R // tr,),
        in_specs=[spec] * (3 + n_g), out_specs=[spec] * 4,
        out_shape=[jax.ShapeDtypeStruct((R, C), F32)] * 4, comm=comm,
    )(w, m, v, *g_parts)


def _chip_coords():
    x, y, c = lax.axis_index("x"), lax.axis_index("y"), lax.axis_index("c")
    others = [(1 - x, y), (x, 1 - y), (1 - x, 1 - y)]
    return x, y, c, others


def _remote(src, dst, send_sem, recv_sem, device):
    return pltpu.make_async_remote_copy(src_ref=src, dst_ref=dst, send_sem=send_sem, recv_sem=recv_sem,
                                        device_id=device, device_id_type=MESH)


def gather_comm(shards):
    n = len(shards)
    pieces = [(a, jj) for a in range(n) for jj in range(3)]

    def makers(ins, outs, sems):
        send_sems, recv_sems, local_sems = sems
        x, y, c, others = _chip_coords()
        me = 2 * x + y
        sibling = (x, y, 1 - c)
        half = lambda a: ins[a].shape[0] // 2

        def local(a):
            return pltpu.make_async_copy(ins[a], outs[a].at[me], local_sems.at[a])

        def ici(a, jj):
            ox, oy = others[jj]
            return _remote(ins[a].at[pl.ds(c * half(a), half(a))], outs[a].at[me, pl.ds(c * half(a), half(a))],
                           send_sems.at[6 * a + jj], recv_sems.at[6 * a + jj], (ox, oy, c))

        def landed(a, jj):
            ox, oy = others[jj]
            got = outs[a].at[2 * ox + oy, pl.ds(c * half(a), half(a))]
            return (_remote(got, got, send_sems.at[6 * a + jj], recv_sems.at[6 * a + jj], (ox, oy, c)),
                    _remote(got, got, send_sems.at[6 * a + 3 + jj], recv_sems.at[6 * a + 3 + jj], sibling))

        def theirs(a, jj):
            ox, oy = others[jj]
            sib = outs[a].at[2 * ox + oy, pl.ds((1 - c) * half(a), half(a))]
            return _remote(sib, sib, send_sems.at[6 * a + 3 + jj], recv_sems.at[6 * a + 3 + jj], sibling)

        return local, ici, landed, theirs

    def start(ins, outs, sems):
        local, ici, _, _ = makers(ins, outs, sems)
        for a in range(n):
            local(a).start()
        for a, jj in pieces:
            ici(a, jj).start()

    def finish(ins, outs, sems):
        local, ici, landed, theirs = makers(ins, outs, sems)
        forwards = []
        for a, jj in pieces:
            got, fwd = landed(a, jj)
            got.wait_recv()
            fwd.start()
            forwards.append(fwd)
        for a, jj in pieces:
            theirs(a, jj).wait_recv()
        for a, jj in pieces:
            ici(a, jj).wait_send()
        for fwd in forwards:
            fwd.wait_send()
        for a in range(n):
            local(a).wait()

    return Comm(shards, [jax.ShapeDtypeStruct((N_CHIP,) + s.shape, s.dtype) for s in shards],
                [pltpu.SemaphoreType.DMA((6 * n,)), pltpu.SemaphoreType.DMA((6 * n,)),
                 pltpu.SemaphoreType.DMA((n,))], start, finish)


def scatter_comm(grads):
    n = len(grads)
    pieces = [(a, jj) for a in range(n) for jj in range(3)]

    def makers(ins, outs, sems):
        send_sems, recv_sems, local_sems = sems
        x, y, c, others = _chip_coords()
        me = 2 * x + y

        def local(a):
            return pltpu.make_async_copy(ins[a].at[me], outs[a].at[me], local_sems.at[a])

        def ici(a, jj):
            ox, oy = others[jj]
            return _remote(ins[a].at[2 * ox + oy], outs[a].at[me], send_sems.at[3 * a + jj],
                           recv_sems.at[3 * a + jj], (ox, oy, c))

        def landed(a, jj):
            ox, oy = others[jj]
            slot = outs[a].at[2 * ox + oy]
            return _remote(slot, slot, send_sems.at[3 * a + jj], recv_sems.at[3 * a + jj], (ox, oy, c))

        return local, ici, landed

    def start(ins, outs, sems):
        local, ici, _ = makers(ins, outs, sems)
        for a in range(n):
            local(a).start()
        for a, jj in pieces:
            ici(a, jj).start()

    def finish(ins, outs, sems):
        local, ici, landed = makers(ins, outs, sems)
        for a, jj in pieces:
            landed(a, jj).wait_recv()
        for a, jj in pieces:
            ici(a, jj).wait_send()
        for a in range(n):
            local(a).wait()

    return Comm(grads, [jax.ShapeDtypeStruct(g.shape, g.dtype) for g in grads],
                [pltpu.SemaphoreType.DMA((3 * n,)), pltpu.SemaphoreType.DMA((3 * n,)),
                 pltpu.SemaphoreType.DMA((n,))], start, finish)


def swap_comm(parts):
    n = len(parts)

    def copies(ins, outs, sems):
        send_sems, recv_sems = sems
        x, y, c, _ = _chip_coords()
        return [_remote(ins[a], outs[a], send_sems.at[a], recv_sems.at[a], (x, y, 1 - c)) for a in range(n)]

    def start(ins, outs, sems):
        for cp in copies(ins, outs, sems):
            cp.start()

    def finish(ins, outs, sems):
        for cp in copies(ins, outs, sems):
            cp.wait()

    return Comm(parts, [jax.ShapeDtypeStruct(p.shape, p.dtype) for p in parts],
                [pltpu.SemaphoreType.DMA((n,)), pltpu.SemaphoreType.DMA((n,))], start, finish)


def allreduce_small(v):
    R = v.shape[0]

    def body(v_ref, sum_ref, all_ref, send_sems, recv_sems):
        x, y, c = lax.axis_index("x"), lax.axis_index("y"), lax.axis_index("c")
        me = 4 * x + 2 * y + c
        all_ref[me] = v_ref[...]
        copies = []
        for k in range(1, N_DEV):
            px = 1 - x if k & 4 else x
            py = 1 - y if k & 2 else y
            pc = 1 - c if k & 1 else c
            cp = pltpu.make_async_remote_copy(
                src_ref=v_ref, dst_ref=all_ref.at[me], send_sem=send_sems.at[k - 1], recv_sem=recv_sems.at[k - 1],
                device_id=(px, py, pc), device_id_type=MESH)
            cp.start()
            copies.append((cp, 4 * px + 2 * py + pc))
        for k, (cp, peer) in enumerate(copies):
            pltpu.make_async_remote_copy(
                src_ref=v_ref, dst_ref=all_ref.at[peer], send_sem=send_sems.at[k], recv_sem=recv_sems.at[k],
                device_id=(x, y, c), device_id_type=MESH).wait_recv()
        for cp, _ in copies:
            cp.wait_send()
        acc = all_ref[0]
        for d in range(1, N_DEV):
            acc = acc + all_ref[d]
        sum_ref[...] = acc

    vm = pl.BlockSpec(memory_space=pltpu.VMEM)
    return pl.pallas_call(
        body, name="allreduce_small",
        in_specs=[vm], out_specs=[vm, vm],
        out_shape=[jax.ShapeDtypeStruct((R, LANES), F32), jax.ShapeDtypeStruct((N_DEV, R, LANES), F32)],
        scratch_shapes=[pltpu.SemaphoreType.DMA((N_DEV - 1,)), pltpu.SemaphoreType.DMA((N_DEV - 1,))],
    )(v)[0]


def _heads_first(t, n_heads):
    T = t.shape[0]
    return t.reshape(T, n_heads, HEAD_DIM).transpose(1, 0, 2)


def _heads_last(t):
    H, T, dh = t.shape
    return t.transpose(1, 0, 2).reshape(T, H * dh)


SMALL_NAMES = ("ffn1_norm", "mix_norm", "ffn2_norm", "final_norm", "conv_b", "conv_ln_g", "conv_ln_b")


def _pack_small(vecs, bias, conv_w_rows, loss_tile):
    rows = [vecs[n].reshape(-1, LANES) for n in SMALL_NAMES]
    rows.append(bias.reshape(1, LANES))
    rows.append(conv_w_rows.reshape(-1, LANES))
    rows.append(loss_tile[0:1, :])
    packed = jnp.concatenate(rows, axis=0)
    pad = (-packed.shape[0]) % 8
    return jnp.pad(packed, ((0, pad), (0, 0)))


def _unpack_small(packed, sizes, n_conv_rows):
    out, r = {}, 0
    for n in SMALL_NAMES:
        k = sizes[n] // LANES
        out[n] = packed[r:r + k].reshape(-1)
        r += k
    out["fgate_bias"] = packed[r]
    r += 1
    out["conv_w"] = packed[r:r + n_conv_rows]
    r += n_conv_rows
    out["loss"] = packed[r, 0]
    return out


def kernel(x, ffn1_norm, ffn1_w_gate, ffn1_w_up, ffn1_w_down, mix_norm, w_in, fgate_bias, conv_w, conv_b, conv_ln_g, conv_ln_b, w_out, ffn2_norm, ffn2_w_gate, ffn2_w_up, ffn2_w_down, final_norm, loss_target, m_ffn1_norm, m_ffn1_w_gate, m_ffn1_w_up, m_ffn1_w_down, m_mix_norm, m_w_in, m_fgate_bias, m_conv_w, m_conv_b, m_conv_ln_g, m_conv_ln_b, m_w_out, m_ffn2_norm, m_ffn2_w_gate, m_ffn2_w_up, m_ffn2_w_down, m_final_norm, v_ffn1_norm, v_ffn1_w_gate, v_ffn1_w_up, v_ffn1_w_down, v_mix_norm, v_w_in, v_fgate_bias, v_conv_w, v_conv_b, v_conv_ln_g, v_conv_ln_b, v_w_out, v_ffn2_norm, v_ffn2_w_gate, v_ffn2_w_up, v_ffn2_w_down, v_final_norm):
    w = dict(ffn1_norm=ffn1_norm, ffn1_w_gate=ffn1_w_gate, ffn1_w_up=ffn1_w_up, ffn1_w_down=ffn1_w_down,
             mix_norm=mix_norm, w_in=w_in, fgate_bias=fgate_bias, conv_w=conv_w, conv_b=conv_b,
             conv_ln_g=conv_ln_g, conv_ln_b=conv_ln_b, w_out=w_out, ffn2_norm=ffn2_norm,
             ffn2_w_gate=ffn2_w_gate, ffn2_w_up=ffn2_w_up, ffn2_w_down=ffn2_w_down, final_norm=final_norm)
    m = dict(ffn1_norm=m_ffn1_norm, ffn1_w_gate=m_ffn1_w_gate, ffn1_w_up=m_ffn1_w_up, ffn1_w_down=m_ffn1_w_down,
             mix_norm=m_mix_norm, w_in=m_w_in, fgate_bias=m_fgate_bias, conv_w=m_conv_w, conv_b=m_conv_b,
             conv_ln_g=m_conv_ln_g, conv_ln_b=m_conv_ln_b, w_out=m_w_out, ffn2_norm=m_ffn2_norm,
             ffn2_w_gate=m_ffn2_w_gate, ffn2_w_up=m_ffn2_w_up, ffn2_w_down=m_ffn2_w_down, final_norm=m_final_norm)
    v = dict(ffn1_norm=v_ffn1_norm, ffn1_w_gate=v_ffn1_w_gate, ffn1_w_up=v_ffn1_w_up, ffn1_w_down=v_ffn1_w_down,
             mix_norm=v_mix_norm, w_in=v_w_in, fgate_bias=v_fgate_bias, conv_w=v_conv_w, conv_b=v_conv_b,
             conv_ln_g=v_conv_ln_g, conv_ln_b=v_conv_ln_b, w_out=v_w_out, ffn2_norm=v_ffn2_norm,
             ffn2_w_gate=v_ffn2_w_gate, ffn2_w_up=v_ffn2_w_up, ffn2_w_down=v_ffn2_w_down, final_norm=v_final_norm)
    names = list(w.keys())
    big = ("ffn1_w_gate", "ffn1_w_up", "ffn1_w_down", "w_in", "w_out", "ffn2_w_gate", "ffn2_w_up", "ffn2_w_down")

    T, D = x.shape[1], x.shape[2]
    C = conv_b.shape[0]
    H = fgate_bias.shape[0]
    cs = conv_w.shape[1]
    in_cols = N_CHIP * w_in.shape[1]
    p_main = in_cols - H

    x0, tgt = x[0], loss_target[0]
    tk = _tile(T, 512, 128)
    nkv = T // tk
    hd = H * HEAD_DIM
    row = lambda a: a.reshape(1, -1)
    wb = {n: w[n].astype(BF16) for n in big}
    grad, delta, new_m, new_v = {}, {}, {}, {}

    def update(n, parts, comm=None):
        grad[n], delta[n], new_m[n], new_v[n] = adamw(w[n], m[n], v[n], parts, comm=comm)

    g0 = gather_comm([wb["ffn1_w_gate"], jnp.pad(conv_w, ((0, HALO - CONV_K), (0, 0)))])
    wg1, conv_w4 = _run_comm("gather_first", g0)
    conv_w_full = conv_w4.transpose(1, 0, 2).reshape(HALO, C)
    h1, r1 = rms_fwd(x0, row(ffn1_norm))
    g1a = gather_comm([wb["ffn1_w_up"]])
    a1 = ffn_gate(h1, wg1, comm=g1a)
    wu1 = g1a.results[0]
    g1b = gather_comm([wb["ffn1_w_down"]])
    b1, mid1 = ffn_upmul(h1, wu1, a1, comm=g1b)
    wd1 = g1b.results[0]
    g2 = gather_comm([wb["w_in"], wb["w_out"]])
    x1 = mm_residual("ffn_down_g", mid1, wd1, x0, 0.5, comm=g2)[0]
    w_in4, w_out3 = g2.results

    w_in_full = w_in4.transpose(1, 0, 2).reshape(D, in_cols)
    w_main = w_in_full[:, :p_main]
    w_f = jnp.pad(w_in_full[:, p_main:], ((0, 0), (0, LANES - H)))
    bias_pad = jnp.pad(row(fgate_bias), ((0, 0), (0, LANES - H)))
    h2, r2 = rms_fwd(x1, row(mix_norm))
    proj = proj_main(h2, w_main)
    f, cum = fgate_fwd(h2, w_f, bias_pad, H)
    ypre, yconv = conv_fwd(proj, conv_w_full, row(conv_b), row(conv_ln_g), row(conv_ln_b))
    q = _heads_first(proj[:, 2 * C:2 * C + hd], H) * (1.0 / math.sqrt(HEAD_DIM))
    k = _heads_first(proj[:, 2 * C + hd:2 * C + 2 * hd], H)
    vv = _heads_first(proj[:, 2 * C + 2 * hd:], H)
    c_ht = cum[:, :H].T
    cq = c_ht[:, :, None]
    ck4 = c_ht.reshape(H, nkv, 1, tk)
    g3 = gather_comm([wb["ffn2_w_gate"], wb["ffn2_w_up"], wb["ffn2_w_down"]])
    o, lse = attn_fwd(q, k, vv, cq, ck4, comm=g3)
    wg2, wu2, wd2 = g3.results
    ycat = jnp.concatenate([yconv, _heads_last(o).astype(BF16)], axis=1)
    x2 = mm_residual("out_proj", ycat, w_out3, x1, 1.0)[0]

    h3, r3 = rms_fwd(x2, row(ffn2_norm))
    a2, b2, mid2 = ffn_up(h3, wg2, wu2, rows=512)
    x3 = mm_residual("ffn_down", mid2, wd2, x2, 0.5, cols=2048)[0]
    dx3, dx3b, loss_tile, d_final = final_loss(x3, tgt, row(final_norm))

    da2, db2 = ffn_bwd_mid(dx3b, wd2, a2, b2, rows=512)
    dwd2 = dw_rowshard("ffn_dwd", mid2, dx3b, N_CHIP, depth=1024)[0]
    s1 = scatter_comm([dwd2])
    dwg2, dwu2 = dw_colshard("ffn_dwgu_s", h3, [da2, db2], N_CHIP, comm=s1)
    s2 = scatter_comm([dwg2])
    dh3 = ffn_dh(da2, db2, wg2, wu2, comm=s2, rows=1024)
    dx2, dx2b, d_ffn2_norm = rms_bwd(dh3, x2, r3, row(ffn2_norm), dx3, 1.0)

    dycat = mm_nt_bf16("out_proj_dy", dx2b, w_out3.reshape(-1, D))
    dw_out3 = dw_rowshard("out_proj_dw", ycat, dx2b, N_CHIP)[0]
    do = _heads_first(dycat[:, C:], H).astype(F32)
    s3 = scatter_comm([dwu2, dw_out3])
    dq, dk, dv, dcq, dck4 = attn_bwd(q, k, vv, o, do, lse, cq, ck4, comm=s3)
    dc = jnp.pad((dcq[:, :, 0] + dck4.reshape(H, T)).T, ((0, 0), (0, LANES - H)))
    df, d_bias = fgate_bwd(dc, f, H)
    dag, d_conv_w, d_conv_b, d_ln_g, d_ln_b = conv_bwd(proj, ypre, dycat, conv_w_full, row(conv_ln_g),
                                                       row(conv_ln_b))
    dproj = jnp.concatenate([dag, _heads_last(dq).astype(BF16), _heads_last(dk).astype(BF16),
                             _heads_last(dv).astype(BF16)], axis=1)
    early = ("ffn2_w_down", "ffn2_w_gate", "ffn2_w_up", "w_out")
    early_sums = [sum_chips(r) for r in (s1.results[0], s2.results[0], s3.results[0], s3.results[1])]
    sw1 = swap_comm(early_sums)
    dh2 = proj_dh(dproj, w_main, df, w_f, comm=sw1)
    dw_main = dw_plain("proj_dw_main", h2, dproj)
    dw_f = dw_plain("proj_dw_f", h2, df)
    dw_in_full = jnp.concatenate([dw_main, dw_f[:, :H]], axis=1)
    dw_in3 = dw_in_full.reshape(D, N_CHIP, in_cols // N_CHIP).transpose(1, 0, 2).astype(BF16)
    dx1, dx1b, d_mix_norm = rms_bwd(dh2, x1, r2, row(mix_norm), dx2, 0.5)

    s4 = scatter_comm([dw_in3])
    da1, db1 = ffn_bwd_mid(dx1b, wd1, a1, b1, comm=s4)
    dwg1, dwu1 = dw_colshard("ffn_dwgu", h1, [da1, db1], N_CHIP, depth=1024)
    s5 = scatter_comm([dwg1])
    dwd1 = dw_rowshard("ffn_dwd_s", mid1, dx1b, N_CHIP, comm=s5)[0]
    s6 = scatter_comm([dwu1])
    dh1 = ffn_dh(da1, db1, wg1, wu1, comm=s6)
    grad_x, _, d_ffn1_norm = rms_bwd(dh1, x0, r1, row(ffn1_norm), dx1, 1.0)

    s7 = scatter_comm([dwd1])
    for i, (n, mine, other) in enumerate(zip(early, early_sums, sw1.results)):
        update(n, [mine, other], comm=s7 if i == 0 else None)
    late = ("w_in", "ffn1_w_gate", "ffn1_w_up", "ffn1_w_down")
    late_sums = [sum_chips(r) for r in (s4.results[0], s5.results[0], s6.results[0], s7.results[0])]
    late_theirs = _run_comm("swap_last", swap_comm(late_sums))
    for n, mine, other in zip(late, late_sums, late_theirs):
        update(n, [mine, other])

    gl = dict(ffn1_norm=d_ffn1_norm, mix_norm=d_mix_norm, ffn2_norm=d_ffn2_norm, final_norm=d_final,
              conv_b=d_conv_b, conv_ln_g=d_ln_g, conv_ln_b=d_ln_b)
    small_sizes = {n: w[n].shape[0] for n in SMALL_NAMES}
    packed = _pack_small(gl, d_bias, d_conv_w, loss_tile)
    red = _unpack_small(allreduce_small(packed), small_sizes, HALO * C // LANES)
    loss = red["loss"]
    my_chip = 2 * lax.axis_index("x") + lax.axis_index("y")
    g_conv_w = lax.dynamic_slice_in_dim(red["conv_w"].reshape(HALO, C)[:CONV_K], my_chip * cs, cs, axis=1)
    update("conv_w", [g_conv_w])
    vec_names = SMALL_NAMES + ("fgate_bias",)
    stack = lambda d: jnp.concatenate(
        [jnp.pad(d[n], (0, (-d[n].shape[0]) % LANES)).reshape(-1, LANES) for n in vec_names], axis=0)
    g_stack = jnp.concatenate([red[n].reshape(-1, LANES) for n in SMALL_NAMES] + [red["fgate_bias"][None, :]],
                              axis=0)
    outs = adamw(stack(w), stack(m), stack(v), [g_stack])
    r = 0
    for n in vec_names:
        size = w[n].shape[0]
        k = -(-size // LANES)
        for dst, src in zip((grad, delta, new_m, new_v), outs):
            dst[n] = src[r:r + k].reshape(-1)[:size]
        r += k

    return (loss, grad_x[None], *[grad[n] for n in names], *[delta[n] for n in names],
            *[new_m[n] for n in names], *[new_v[n] for n in names])
```

```python
import functools
import math

import jax
import jax.numpy as jnp
from jax import lax
from jax.experimental import pallas as pl
from jax.experimental.pallas import tpu as pltpu

F32 = jnp.float32
BF16 = jnp.bfloat16
NORM_EPS = 1e-6
LN_EPS = 1e-5
NEG_INF = -1e30
HEAD_DIM = 64
CONV_K = 31
HALO = 32
LANES = 128
N_CHIP = 4
N_DEV = 8
VMEM_LIMIT = 52 * 1024 * 1024
MESH = pl.DeviceIdType.MESH

ADAM_LR = 0.001
ADAM_B1 = 0.9
ADAM_B2 = 0.999
ADAM_EPS = 1e-08
ADAM_WD = 0.01
ADAM_STEP = 10

NN = (((1,), (0,)), ((), ()))
NT = (((1,), (1,)), ((), ()))
TN = (((0,), (0,)), ((), ()))


def _tile(n, pref, unit=128):
    if n <= pref:
        return n
    t = (pref // unit) * unit
    while t > 0:
        if n % t == 0:
            return t
        t -= unit
    raise ValueError(f"no tile for {n} under {pref}")


class Comm:
    def __init__(self, operands, out_shape, sems, start, finish):
        self.operands, self.out_shape, self.sems = list(operands), list(out_shape), list(sems)
        self.start, self.finish = start, finish
        self.results = None


def _pcall(body, *, name, grid, in_specs, out_specs, out_shape, scratch=(), comm=None):
    params = pltpu.CompilerParams(dimension_semantics=("arbitrary",) * len(grid), vmem_limit_bytes=VMEM_LIMIT)
    scratch = list(scratch)
    if comm is None:
        return pl.pallas_call(body, name=name, grid=grid, in_specs=in_specs, out_specs=out_specs,
                              out_shape=out_shape, scratch_shapes=scratch, compiler_params=params)
    n_in, n_out, n_s = len(in_specs), len(out_shape), len(scratch)
    n_ci, n_co = len(comm.operands), len(comm.out_shape)
    any_spec = pl.BlockSpec(memory_space=pl.ANY)

    def carried(*refs):
        ins, refs = refs[:n_in], refs[n_in:]
        c_ins, refs = refs[:n_ci], refs[n_ci:]
        outs, refs = refs[:n_out], refs[n_out:]
        c_outs, refs = refs[:n_co], refs[n_co:]
        scr, c_sems = refs[:n_s], refs[n_s:]
        first = pl.program_id(0) == 0
        last = pl.program_id(0) == grid[0] - 1
        for d in range(1, len(grid)):
            first = jnp.logical_and(first, pl.program_id(d) == 0)
            last = jnp.logical_and(last, pl.program_id(d) == grid[d] - 1)

        @pl.when(first)
        def _():
            comm.start(c_ins, c_outs, c_sems)

        body(*ins, *outs, *scr)

        @pl.when(last)
        def _():
            comm.finish(c_ins, c_outs, c_sems)

    call = pl.pallas_call(
        carried, name=name, grid=grid, in_specs=list(in_specs) + [any_spec] * n_ci,
        out_specs=list(out_specs) + [any_spec] * n_co, out_shape=list(out_shape) + comm.out_shape,
        scratch_shapes=scratch + comm.sems, compiler_params=params)

    def run(*operands):
        res = call(*operands, *comm.operands)
        comm.results = list(res[n_out:])
        return list(res[:n_out])

    return run


def _run_comm(name, comm):
    n_ci, n_co = len(comm.operands), len(comm.out_shape)
    any_spec = pl.BlockSpec(memory_space=pl.ANY)

    def body(*refs):
        c_ins, c_outs, c_sems = refs[:n_ci], refs[n_ci:n_ci + n_co], refs[n_ci + n_co:]
        comm.start(c_ins, c_outs, c_sems)
        comm.finish(c_ins, c_outs, c_sems)

    return pl.pallas_call(body, name=name, in_specs=[any_spec] * n_ci, out_specs=[any_spec] * n_co,
                          out_shape=comm.out_shape, scratch_shapes=comm.sems)(*comm.operands)


def _sigmoid(x):
    return 1.0 / (1.0 + jnp.exp(-x))


def _mm(name, *, grid, pairs, once_pairs=(), extra=(), out_shape, out_specs, acc_shapes, nk, kaxis, epilogue,
        comm=None):
    all_pairs = list(pairs) + list(once_pairs)
    n_p, n_o = len(pairs), len(once_pairs)
    n_e, n_out, n_acc = len(extra), len(out_shape), len(acc_shapes)

    def body(*refs):
        ab = refs[: 2 * (n_p + n_o)]
        ex = refs[2 * (n_p + n_o): 2 * (n_p + n_o) + n_e]
        outs = refs[2 * (n_p + n_o) + n_e: 2 * (n_p + n_o) + n_e + n_out]
        accs = refs[2 * (n_p + n_o) + n_e + n_out:]

        def dots(idx_range):
            vals = [None] * n_acc
            for p in idx_range:
                d = lax.dot_general(ab[2 * p][...], ab[2 * p + 1][...], all_pairs[p][4],
                                    preferred_element_type=F32)
                ai = all_pairs[p][5]
                vals[ai] = d if vals[ai] is None else vals[ai] + d
            return vals

        if nk == 1:
            vals = dots(range(n_p + n_o))
            epilogue(vals, ex, outs)
            return

        k = pl.program_id(kaxis)

        @pl.when(k == 0)
        def _():
            vals = dots(range(n_p + n_o))
            for ai in range(n_acc):
                accs[ai][...] = vals[ai]

        @pl.when(k > 0)
        def _():
            vals = dots(range(n_p))
            for ai in range(n_acc):
                if vals[ai] is not None:
                    accs[ai][...] += vals[ai]

        @pl.when(k == nk - 1)
        def _():
            epilogue([a[...] for a in accs], ex, outs)

    operands, in_specs = [], []
    for p in all_pairs:
        operands += [p[0], p[2]]
        in_specs += [p[1], p[3]]
    for arr, spec in extra:
        operands.append(arr)
        in_specs.append(spec)
    scratch = [pltpu.VMEM(s, F32) for s in acc_shapes] if nk > 1 else []
    return _pcall(body, name=name, grid=grid, in_specs=in_specs, out_specs=out_specs, out_shape=out_shape,
                  scratch=scratch, comm=comm)(*operands)


def rms_fwd(x, g):
    T, D = x.shape
    tt = _tile(T, 512, 8)

    def body(x_ref, g_ref, h_ref, r_ref):
        xv = x_ref[...]
        r = lax.rsqrt(jnp.mean(xv * xv, axis=-1, keepdims=True) + NORM_EPS)
        h_ref[...] = (xv * r * g_ref[...]).astype(BF16)
        r_ref[...] = r

    return _pcall(
        body, name="rms_fwd", grid=(T // tt,),
        in_specs=[pl.BlockSpec((tt, D), lambda i: (i, 0)), pl.BlockSpec((1, D), lambda i: (0, 0))],
        out_specs=[pl.BlockSpec((tt, D), lambda i: (i, 0)), pl.BlockSpec((tt, 1), lambda i: (i, 0))],
        out_shape=[jax.ShapeDtypeStruct((T, D), BF16), jax.ShapeDtypeStruct((T, 1), F32)],
    )(x, g)


def rms_bwd(dh, x, r, g, dres, out_scale):
    T, D = x.shape
    tt = _tile(T, 256, 8)

    def body(dh_ref, x_ref, r_ref, g_ref, dres_ref, dx_ref, dxb_ref, dg_ref):
        i = pl.program_id(0)
        xh = x_ref[...] * r_ref[...]
        dhv = dh_ref[...]
        dxh = dhv * g_ref[...]
        dx = dres_ref[...] + r_ref[...] * (dxh - xh * jnp.mean(dxh * xh, axis=-1, keepdims=True))
        dx_ref[...] = dx
        dxb_ref[...] = (out_scale * dx).astype(BF16)
        part = jnp.sum(dhv * xh, axis=0, keepdims=True)

        @pl.when(i == 0)
        def _():
            dg_ref[...] = part

        @pl.when(i > 0)
        def _():
            dg_ref[...] += part

    row = pl.BlockSpec((tt, D), lambda i: (i, 0))
    return _pcall(
        body, name="rms_bwd", grid=(T // tt,),
        in_specs=[row, row, pl.BlockSpec((tt, 1), lambda i: (i, 0)), pl.BlockSpec((1, D), lambda i: (0, 0)), row],
        out_specs=[row, row, pl.BlockSpec((1, D), lambda i: (0, 0))],
        out_shape=[jax.ShapeDtypeStruct((T, D), F32), jax.ShapeDtypeStruct((T, D), BF16),
                   jax.ShapeDtypeStruct((1, D), F32)],
    )(dh, x, r, g, dres)


def final_loss(x, tgt, g):
    T, D = x.shape
    tt = _tile(T, 256, 8)

    def body(x_ref, t_ref, g_ref, dx_ref, dxb_ref, loss_ref, dg_ref):
        i = pl.program_id(0)
        xv = x_ref[...]
        r = lax.rsqrt(jnp.mean(xv * xv, axis=-1, keepdims=True) + NORM_EPS)
        xh = xv * r
        err = xh * g_ref[...] - t_ref[...]
        part_loss = 0.5 * jnp.sum(jnp.mean(err * err, axis=-1, keepdims=True), axis=0, keepdims=True)
        dy = err * (1.0 / D)
        dxh = dy * g_ref[...]
        dx = r * (dxh - xh * jnp.mean(dxh * xh, axis=-1, keepdims=True))
        dx_ref[...] = dx
        dxb_ref[...] = (0.5 * dx).astype(BF16)
        part_g = jnp.sum(dy * xh, axis=0, keepdims=True)
        part_l = jnp.broadcast_to(part_loss, (8, LANES))

        @pl.when(i == 0)
        def _():
            dg_ref[...] = part_g
            loss_ref[...] = part_l

        @pl.when(i > 0)
        def _():
            dg_ref[...] += part_g
            loss_ref[...] += part_l

    row = pl.BlockSpec((tt, D), lambda i: (i, 0))
    return _pcall(
        body, name="final_loss", grid=(T // tt,),
        in_specs=[row, row, pl.BlockSpec((1, D), lambda i: (0, 0))],
        out_specs=[row, row, pl.BlockSpec((8, LANES), lambda i: (0, 0)), pl.BlockSpec((1, D), lambda i: (0, 0))],
        out_shape=[jax.ShapeDtypeStruct((T, D), F32), jax.ShapeDtypeStruct((T, D), BF16),
                   jax.ShapeDtypeStruct((8, LANES), F32), jax.ShapeDtypeStruct((1, D), F32)],
    )(x, tgt, g)


def ffn_gate(h, wg3, comm=None):
    T, D = h.shape
    nc, _, fs = wg3.shape
    tm = _tile(T, 512, 8)

    def epilogue(vals, ex, outs):
        outs[0][...] = vals[0].astype(BF16)

    return _mm("ffn_gate", grid=(nc, T // tm),
               pairs=[(h, pl.BlockSpec((tm, D), lambda j, i: (i, 0)),
                       wg3, pl.BlockSpec((None, D, fs), lambda j, i: (j, 0, 0)), NN, 0)],
               out_shape=[jax.ShapeDtypeStruct((T, nc * fs), BF16)],
               out_specs=[pl.BlockSpec((tm, fs), lambda j, i: (i, j))],
               acc_shapes=[(tm, fs)], nk=1, kaxis=None, epilogue=epilogue, comm=comm)[0]


def ffn_upmul(h, wu3, a, comm=None):
    T, D = h.shape
    nc, _, fs = wu3.shape
    tm = _tile(T, 512, 8)

    def epilogue(vals, ex, outs):
        b = vals[0]
        av = ex[0][...].astype(F32)
        outs[0][...] = b.astype(BF16)
        outs[1][...] = (av * _sigmoid(av) * b).astype(BF16)

    t_spec = pl.BlockSpec((tm, fs), lambda j, i: (i, j))
    o_shape = jax.ShapeDtypeStruct((T, nc * fs), BF16)
    return _mm("ffn_upmul", grid=(nc, T // tm),
               pairs=[(h, pl.BlockSpec((tm, D), lambda j, i: (i, 0)),
                       wu3, pl.BlockSpec((None, D, fs), lambda j, i: (j, 0, 0)), NN, 0)],
               extra=[(a, t_spec)], out_shape=[o_shape] * 2, out_specs=[t_spec] * 2,
               acc_shapes=[(tm, fs)], nk=1, kaxis=None, epilogue=epilogue, comm=comm)


def ffn_up(h, wg3, wu3, comm=None):
    T, D = h.shape
    nc, _, fs = wg3.shape
    tm = _tile(T, 512, 8)

    def epilogue(vals, ex, outs):
        a, b = vals
        outs[0][...] = a.astype(BF16)
        outs[1][...] = b.astype(BF16)
        outs[2][...] = (a * _sigmoid(a) * b).astype(BF16)

    h_spec = pl.BlockSpec((tm, D), lambda j, i: (i, 0))
    w_spec = pl.BlockSpec((None, D, fs), lambda j, i: (j, 0, 0))
    o_spec = pl.BlockSpec((tm, fs), lambda j, i: (i, j))
    o_shape = jax.ShapeDtypeStruct((T, nc * fs), BF16)
    return _mm("ffn_up", grid=(nc, T // tm),
               pairs=[(h, h_spec, wg3, w_spec, NN, 0), (h, h_spec, wu3, w_spec, NN, 1)],
               out_shape=[o_shape] * 3, out_specs=[o_spec] * 3, acc_shapes=[(tm, fs)] * 2, nk=1, kaxis=None,
               epilogue=epilogue, comm=comm)


def mm_residual(name, a, b3, res, scale, comm=None):
    T = a.shape[0]
    nk, tk, N = b3.shape
    tm, tn = _tile(T, 512, 8), _tile(N, 2048)

    def epilogue(vals, ex, outs):
        outs[0][...] = ex[0][...] + scale * vals[0]

    return _mm(name, grid=(T // tm, N // tn, nk),
               pairs=[(a, pl.BlockSpec((tm, tk), lambda i, n, k: (i, k)),
                       b3, pl.BlockSpec((None, tk, tn), lambda i, n, k: (k, 0, n)), NN, 0)],
               extra=[(res, pl.BlockSpec((tm, tn), lambda i, n, k: (i, n)))],
               out_shape=[jax.ShapeDtypeStruct((T, N), F32)],
               out_specs=[pl.BlockSpec((tm, tn), lambda i, n, k: (i, n))],
               acc_shapes=[(tm, tn)], nk=nk, kaxis=2, epilogue=epilogue, comm=comm)


def ffn_bwd_mid(dout, wd3, a, b, comm=None):
    T, D = dout.shape
    nc, fs, _ = wd3.shape
    tm = _tile(T, 512, 8)

    def epilogue(vals, ex, outs):
        dm = vals[0]
        av = ex[0][...].astype(F32)
        bv = ex[1][...].astype(F32)
        s = _sigmoid(av)
        outs[0][...] = (dm * bv * (s * (1.0 + av * (1.0 - s)))).astype(BF16)
        outs[1][...] = (dm * (av * s)).astype(BF16)

    t_spec = pl.BlockSpec((tm, fs), lambda j, i: (i, j))
    o_shape = jax.ShapeDtypeStruct((T, nc * fs), BF16)
    return _mm("ffn_bwd_mid", grid=(nc, T // tm),
               pairs=[(dout, pl.BlockSpec((tm, D), lambda j, i: (i, 0)),
                       wd3, pl.BlockSpec((None, fs, D), lambda j, i: (j, 0, 0)), NT, 0)],
               extra=[(a, t_spec), (b, t_spec)],
               out_shape=[o_shape] * 2, out_specs=[t_spec] * 2, acc_shapes=[(tm, fs)], nk=1, kaxis=None,
               epilogue=epilogue, comm=comm)


def dw_rowshard(name, a, b, nc, comm=None):
    T, M = a.shape
    N = b.shape[1]
    ms = M // nc
    tn, tk = _tile(N, 1024), _tile(T, 1024, 16)

    def epilogue(vals, ex, outs):
        outs[0][...] = vals[0].astype(BF16)

    return _mm(name, grid=(nc, N // tn, T // tk),
               pairs=[(a, pl.BlockSpec((tk, ms), lambda j, n, k: (k, j)),
                       b, pl.BlockSpec((tk, tn), lambda j, n, k: (k, n)), TN, 0)],
               out_shape=[jax.ShapeDtypeStruct((nc, ms, N), BF16)],
               out_specs=[pl.BlockSpec((None, ms, tn), lambda j, n, k: (j, 0, n))],
               acc_shapes=[(ms, tn)], nk=T // tk, kaxis=2, epilogue=epilogue, comm=comm)


def dw_colshard(name, a, bs, nc, comm=None):
    T, M = a.shape
    ns = bs[0].shape[1] // nc
    tm, tk = _tile(M, 512), _tile(T, 1024, 16)

    def epilogue(vals, ex, outs):
        for v, o in zip(vals, outs):
            o[...] = v.astype(BF16)

    a_spec = pl.BlockSpec((tk, tm), lambda j, m, k: (k, m))
    b_spec = pl.BlockSpec((tk, ns), lambda j, m, k: (k, j))
    return _mm(name, grid=(nc, M // tm, T // tk),
               pairs=[(a, a_spec, b, b_spec, TN, p) for p, b in enumerate(bs)],
               out_shape=[jax.ShapeDtypeStruct((nc, M, ns), BF16)] * len(bs),
               out_specs=[pl.BlockSpec((None, tm, ns), lambda j, m, k: (j, m, 0))] * len(bs),
               acc_shapes=[(tm, ns)] * len(bs), nk=T // tk, kaxis=2, epilogue=epilogue, comm=comm)


def dw_plain(name, a, b):
    T, M = a.shape
    N = b.shape[1]
    tm, tn, tk = _tile(M, 1024), _tile(N, 1024), _tile(T, 512, 16)

    def epilogue(vals, ex, outs):
        outs[0][...] = vals[0]

    return _mm(name, grid=(N // tn, M // tm, T // tk),
               pairs=[(a, pl.BlockSpec((tk, tm), lambda n, m, k: (k, m)),
                       b, pl.BlockSpec((tk, tn), lambda n, m, k: (k, n)), TN, 0)],
               out_shape=[jax.ShapeDtypeStruct((M, N), F32)],
               out_specs=[pl.BlockSpec((tm, tn), lambda n, m, k: (m, n))],
               acc_shapes=[(tm, tn)], nk=T // tk, kaxis=2, epilogue=epilogue)[0]


def ffn_dh(da, db, wg3, wu3, comm=None):
    T = da.shape[0]
    nc, D, fs = wg3.shape
    tm, tn = _tile(T, 512, 8), _tile(D, 1024)

    def epilogue(vals, ex, outs):
        outs[0][...] = vals[0]

    a_spec = pl.BlockSpec((tm, fs), lambda i, n, k: (i, k))
    w_spec = pl.BlockSpec((None, tn, fs), lambda i, n, k: (k, n, 0))
    return _mm("ffn_dh", grid=(T // tm, D // tn, nc),
               pairs=[(da, a_spec, wg3, w_spec, NT, 0), (db, a_spec, wu3, w_spec, NT, 0)],
               out_shape=[jax.ShapeDtypeStruct((T, D), F32)],
               out_specs=[pl.BlockSpec((tm, tn), lambda i, n, k: (i, n))],
               acc_shapes=[(tm, tn)], nk=nc, kaxis=2, epilogue=epilogue, comm=comm)[0]


def proj_main(h, w):
    T, D = h.shape
    P = w.shape[1]
    tm, tn = _tile(T, 512, 8), _tile(P, 1024)

    def epilogue(vals, ex, outs):
        outs[0][...] = vals[0].astype(BF16)

    return _mm("proj_main", grid=(P // tn, T // tm),
               pairs=[(h, pl.BlockSpec((tm, D), lambda j, i: (i, 0)),
                       w, pl.BlockSpec((D, tn), lambda j, i: (0, j)), NN, 0)],
               out_shape=[jax.ShapeDtypeStruct((T, P), BF16)],
               out_specs=[pl.BlockSpec((tm, tn), lambda j, i: (i, j))],
               acc_shapes=[(tm, tn)], nk=1, kaxis=None, epilogue=epilogue)[0]


def mm_nt_bf16(name, a, w):
    T, K = a.shape
    M = w.shape[0]
    tm, tn = _tile(T, 512, 8), _tile(M, 1024)

    def epilogue(vals, ex, outs):
        outs[0][...] = vals[0].astype(BF16)

    return _mm(name, grid=(T // tm, M // tn),
               pairs=[(a, pl.BlockSpec((tm, K), lambda i, n: (i, 0)),
                       w, pl.BlockSpec((tn, K), lambda i, n: (n, 0)), NT, 0)],
               out_shape=[jax.ShapeDtypeStruct((T, M), BF16)],
               out_specs=[pl.BlockSpec((tm, tn), lambda i, n: (i, n))],
               acc_shapes=[(tm, tn)], nk=1, kaxis=None, epilogue=epilogue)[0]


def proj_dh(dproj, w_main, df, w_f, comm=None):
    T, P = dproj.shape
    D = w_main.shape[0]
    tm, tn, tk = _tile(T, 512, 8), _tile(D, 1024), _tile(P, 1280)

    def epilogue(vals, ex, outs):
        outs[0][...] = vals[0]

    return _mm("proj_dh", grid=(T // tm, D // tn, P // tk),
               pairs=[(dproj, pl.BlockSpec((tm, tk), lambda i, n, k: (i, k)),
                       w_main, pl.BlockSpec((tn, tk), lambda i, n, k: (n, k)), NT, 0)],
               once_pairs=[(df, pl.BlockSpec((tm, LANES), lambda i, n, k: (i, 0)),
                            w_f, pl.BlockSpec((tn, LANES), lambda i, n, k: (n, 0)), NT, 0)],
               out_shape=[jax.ShapeDtypeStruct((T, D), F32)],
               out_specs=[pl.BlockSpec((tm, tn), lambda i, n, k: (i, n))],
               acc_shapes=[(tm, tn)], nk=P // tk, kaxis=2, epilogue=epilogue, comm=comm)[0]


def fgate_fwd(h, w_f, bias, n_heads):
    T, D = h.shape
    tt = _tile(T, 512, 8)

    def body(h_ref, w_ref, b_ref, f_ref, c_ref, carry):
        i = pl.program_id(0)

        @pl.when(i == 0)
        def _():
            carry[...] = jnp.zeros_like(carry)

        f = jnp.dot(h_ref[...], w_ref[...], preferred_element_type=F32) + b_ref[...]
        logf = jnp.minimum(f, 0.0) - jnp.log(1.0 + jnp.exp(-jnp.abs(f)))
        tri = (lax.broadcasted_iota(jnp.int32, (tt, tt), 0) >= lax.broadcasted_iota(jnp.int32, (tt, tt), 1))
        cs = jnp.dot(tri.astype(F32), logf, preferred_element_type=F32, precision=lax.Precision.HIGHEST)
        c = cs + carry[...]
        f_ref[...] = f
        c_ref[...] = c
        carry[...] = c[tt - 1:tt, :]

    row = pl.BlockSpec((tt, LANES), lambda i: (i, 0))
    return _pcall(
        body, name="fgate_fwd", grid=(T // tt,),
        in_specs=[pl.BlockSpec((tt, D), lambda i: (i, 0)), pl.BlockSpec((D, LANES), lambda i: (0, 0)),
                  pl.BlockSpec((1, LANES), lambda i: (0, 0))],
        out_specs=[row, row],
        out_shape=[jax.ShapeDtypeStruct((T, LANES), F32)] * 2,
        scratch=[pltpu.VMEM((1, LANES), F32)],
    )(h, w_f, bias)


def fgate_bwd(dc, f, n_heads):
    T = dc.shape[0]
    tt = _tile(T, 512, 8)
    nt = T // tt

    def body(dc_ref, f_ref, df_ref, db_ref, carry):
        i = pl.program_id(0)

        @pl.when(i == 0)
        def _():
            carry[...] = jnp.zeros_like(carry)

        tri = (lax.broadcasted_iota(jnp.int32, (tt, tt), 1) >= lax.broadcasted_iota(jnp.int32, (tt, tt), 0))
        rs = jnp.dot(tri.astype(F32), dc_ref[...], preferred_element_type=F32,
                     precision=lax.Precision.HIGHEST) + carry[...]
        carry[...] = rs[0:1, :]
        lane = lax.broadcasted_iota(jnp.int32, (tt, LANES), 1)
        df = jnp.where(lane < n_heads, rs * _sigmoid(-f_ref[...]), 0.0)
        df_ref[...] = df.astype(BF16)
        part = jnp.sum(df, axis=0, keepdims=True)

        @pl.when(i == 0)
        def _():
            db_ref[...] = part

        @pl.when(i > 0)
        def _():
            db_ref[...] += part

    rev = pl.BlockSpec((tt, LANES), lambda i: (nt - 1 - i, 0))
    return _pcall(
        body, name="fgate_bwd", grid=(nt,),
        in_specs=[rev, rev],
        out_specs=[rev, pl.BlockSpec((1, LANES), lambda i: (0, 0))],
        out_shape=[jax.ShapeDtypeStruct((T, LANES), BF16), jax.ShapeDtypeStruct((1, LANES), F32)],
        scratch=[pltpu.VMEM((1, LANES), F32)],
    )(dc, f)


SUBLANES = 8
SHIFT_ROWS = HALO - SUBLANES


def _shifted_copies(buf, sh, tt):
    for r in range(1, SUBLANES):
        sh[r - 1, 0:tt + SHIFT_ROWS, :] = buf[pl.ds(r, tt + SHIFT_ROWS), :]


def _tap(buf, sh, offset, tt):
    q, r = divmod(offset, SUBLANES)
    if r == 0:
        return buf[pl.ds(SUBLANES * q, tt), :]
    return sh[r - 1, pl.ds(SUBLANES * q, tt), :]


def conv_fwd(proj, conv_w, conv_b, ln_g, ln_b):
    T = proj.shape[0]
    C = conv_w.shape[1]
    tt = _tile(T, 256, HALO)
    hb = tt // HALO

    def body(a_ref, g_ref, ah_ref, gh_ref, w_ref, cb_ref, lg_ref, lb_ref, ypre_ref, y_ref, ubuf, ush):
        i = pl.program_id(0)
        u = a_ref[...].astype(F32) * _sigmoid(g_ref[...].astype(F32))
        uh = ah_ref[...].astype(F32) * _sigmoid(gh_ref[...].astype(F32))
        ubuf[0:HALO, :] = jnp.where(i == 0, 0.0, uh)
        ubuf[HALO:HALO + tt, :] = u
        _shifted_copies(ubuf, ush, tt)
        acc = jnp.broadcast_to(cb_ref[...], (tt, C))
        for k in range(CONV_K):
            acc = acc + w_ref[k:k + 1, :] * _tap(ubuf, ush, HALO - (CONV_K - 1) + k, tt)
        ypre_ref[...] = acc
        mu = jnp.mean(acc, axis=-1, keepdims=True)
        d = acc - mu
        rstd = lax.rsqrt(jnp.mean(d * d, axis=-1, keepdims=True) + LN_EPS)
        z = d * rstd * lg_ref[...] + lb_ref[...]
        y_ref[...] = (z * _sigmoid(z)).astype(BF16)

    vec = pl.BlockSpec((1, C), lambda i: (0, 0))
    return _pcall(
        body, name="conv_fwd", grid=(T // tt,),
        in_specs=[pl.BlockSpec((tt, C), lambda i: (i, 0)), pl.BlockSpec((tt, C), lambda i: (i, 1)),
                  pl.BlockSpec((HALO, C), lambda i: (jnp.maximum(i * hb - 1, 0), 0)),
                  pl.BlockSpec((HALO, C), lambda i: (jnp.maximum(i * hb - 1, 0), 1)),
                  pl.BlockSpec((HALO, C), lambda i: (0, 0)), vec, vec, vec],
        out_specs=[pl.BlockSpec((tt, C), lambda i: (i, 0))] * 2,
        out_shape=[jax.ShapeDtypeStruct((T, C), F32), jax.ShapeDtypeStruct((T, C), BF16)],
        scratch=[pltpu.VMEM((tt + HALO, C), F32), pltpu.VMEM((SUBLANES - 1, tt + SHIFT_ROWS, C), F32)],
    )(proj, proj, proj, proj, conv_w, conv_b, ln_g, ln_b)


def conv_bwd(proj, ypre, dycat, conv_w, ln_g, ln_b):
    T = proj.shape[0]
    C = conv_w.shape[1]
    tt = _tile(T, 256, HALO)
    hb = tt // HALO
    nt = T // tt
    last_h = T // HALO - 1

    def ln_bwd(ypre_v, dout_v, lg, lb):
        mu = jnp.mean(ypre_v, axis=-1, keepdims=True)
        d = ypre_v - mu
        rstd = lax.rsqrt(jnp.mean(d * d, axis=-1, keepdims=True) + LN_EPS)
        yh = d * rstd
        z = yh * lg + lb
        s = _sigmoid(z)
        dz = dout_v * (s * (1.0 + z * (1.0 - s)))
        dyh = dz * lg
        dy = rstd * (dyh - jnp.mean(dyh, axis=-1, keepdims=True)
                     - yh * jnp.mean(dyh * yh, axis=-1, keepdims=True))
        return dy, dz, yh

    def body(a_ref, g_ref, ah_ref, gh_ref, yp_ref, ypn_ref, do_ref, don_ref, w_ref, lg_ref, lb_ref,
             dag_ref, dw_ref, dcb_ref, dlg_ref, dlb_ref, ubuf, dybuf, ush, dysh):
        i = pl.program_id(0)
        av = a_ref[...].astype(F32)
        sg = _sigmoid(g_ref[...].astype(F32))
        uh = ah_ref[...].astype(F32) * _sigmoid(gh_ref[...].astype(F32))
        ubuf[0:HALO, :] = jnp.where(i == 0, 0.0, uh)
        ubuf[HALO:HALO + tt, :] = av * sg
        lg, lb = lg_ref[...], lb_ref[...]
        dy, dz, yh = ln_bwd(yp_ref[...], do_ref[...].astype(F32), lg, lb)
        dyn, _, _ = ln_bwd(ypn_ref[...], don_ref[...].astype(F32), lg, lb)
        dybuf[0:tt, :] = dy
        dybuf[tt:tt + HALO, :] = jnp.where(i == nt - 1, 0.0, dyn)
        _shifted_copies(ubuf, ush, tt)
        _shifted_copies(dybuf, dysh, tt)

        @pl.when(i == 0)
        def _():
            dw_ref[...] = jnp.zeros_like(dw_ref)
            dcb_ref[...] = jnp.zeros_like(dcb_ref)
            dlg_ref[...] = jnp.zeros_like(dlg_ref)
            dlb_ref[...] = jnp.zeros_like(dlb_ref)

        du = jnp.zeros((tt, C), F32)
        for k in range(CONV_K):
            du = du + w_ref[k:k + 1, :] * _tap(dybuf, dysh, CONV_K - 1 - k, tt)
            dw_ref[k:k + 1, :] += jnp.sum(dy * _tap(ubuf, ush, HALO - (CONV_K - 1) + k, tt), axis=0, keepdims=True)
        dcb_ref[...] += jnp.sum(dy, axis=0, keepdims=True)
        dlg_ref[...] += jnp.sum(dz * yh, axis=0, keepdims=True)
        dlb_ref[...] += jnp.sum(dz, axis=0, keepdims=True)

        dag_ref[:, 0:C] = (du * sg).astype(BF16)
        dag_ref[:, C:2 * C] = (du * av * sg * (1.0 - sg)).astype(BF16)

    vec = pl.BlockSpec((1, C), lambda i: (0, 0))
    prev_h = lambda col: pl.BlockSpec((HALO, C), lambda i: (jnp.maximum(i * hb - 1, 0), col))
    next_h = pl.BlockSpec((HALO, C), lambda i: (jnp.minimum((i + 1) * hb, last_h), 0))
    return _pcall(
        body, name="conv_bwd", grid=(nt,),
        in_specs=[pl.BlockSpec((tt, C), lambda i: (i, 0)), pl.BlockSpec((tt, C), lambda i: (i, 1)),
                  prev_h(0), prev_h(1),
                  pl.BlockSpec((tt, C), lambda i: (i, 0)), next_h,
                  pl.BlockSpec((tt, C), lambda i: (i, 0)), next_h,
                  pl.BlockSpec((HALO, C), lambda i: (0, 0)), vec, vec],
        out_specs=[pl.BlockSpec((tt, 2 * C), lambda i: (i, 0)), pl.BlockSpec((HALO, C), lambda i: (0, 0)),
                   vec, vec, vec],
        out_shape=[jax.ShapeDtypeStruct((T, 2 * C), BF16), jax.ShapeDtypeStruct((HALO, C), F32),
                   jax.ShapeDtypeStruct((1, C), F32), jax.ShapeDtypeStruct((1, C), F32),
                   jax.ShapeDtypeStruct((1, C), F32)],
        scratch=[pltpu.VMEM((tt + HALO, C), F32), pltpu.VMEM((tt + HALO, C), F32),
                 pltpu.VMEM((SUBLANES - 1, tt + SHIFT_ROWS, C), F32),
                 pltpu.VMEM((SUBLANES - 1, tt + SHIFT_ROWS, C), F32)],
    )(proj, proj, proj, proj, ypre, ypre, dycat, dycat, conv_w, ln_g, ln_b)


PAIR = LANES // HEAD_DIM


def _head_masks(rows):
    lane = lax.broadcasted_iota(jnp.int32, (rows, LANES), 1)
    return [jnp.logical_and(lane >= hh * HEAD_DIM, lane < (hh + 1) * HEAD_DIM) for hh in range(PAIR)]


def _causal(tq, tk):
    return lax.broadcasted_iota(jnp.int32, (tq, tk), 0) >= lax.broadcasted_iota(jnp.int32, (tq, tk), 1)


def attn_fwd(proj, cq, ck4, q_col, comm=None):
    T = proj.shape[0]
    H, nkv, _, tk = ck4.shape
    tq = tk
    hd = H * HEAD_DIM
    qb, kb, vb = q_col // LANES, (q_col + hd) // LANES, (q_col + 2 * hd) // LANES
    scale = 1.0 / math.sqrt(HEAD_DIM)

    def body(q_ref, k_ref, v_ref, cq_ref, ck_ref, o_ref, lse_ref):
        i = pl.program_id(1)
        masks = _head_masks(tq)
        q2 = q_ref[...] * scale
        qs = [jnp.where(mk, q2, jnp.zeros_like(q2)) for mk in masks]

        def step(j, carry, diagonal):
            off = pl.multiple_of(j * tk, tk)
            kj = k_ref[pl.ds(off, tk), :]
            vj = v_ref[pl.ds(off, tk), :]
            out = []
            for hh in range(PAIR):
                m, l, acc = carry[hh]
                s = lax.dot_general(qs[hh], kj, NT, preferred_element_type=F32)
                s = s + cq_ref[hh] - ck_ref[hh, j]
                if diagonal:
                    s = jnp.where(_causal(tq, tk), s, NEG_INF)
                m_new = jnp.maximum(m, jnp.max(s, axis=-1, keepdims=True))
                alpha = jnp.exp(m - m_new)
                p = jnp.exp(s - m_new)
                l = alpha * l + jnp.sum(p, axis=-1, keepdims=True)
                acc = alpha * acc + jnp.dot(p.astype(BF16), vj, preferred_element_type=F32)
                out.append((m_new, l, acc))
            return tuple(out)

        init = tuple((jnp.full((tq, 1), -jnp.inf, F32), jnp.zeros((tq, 1), F32), jnp.zeros((tq, LANES), F32))
                     for _ in range(PAIR))
        carry = lax.fori_loop(0, i, functools.partial(step, diagonal=False), init)
        carry = step(i, carry, True)
        o = carry[PAIR - 1][2] / carry[PAIR - 1][1]
        for hh in range(PAIR - 1):
            o = jnp.where(masks[hh], carry[hh][2] / carry[hh][1], o)
        o_ref[...] = o
        for hh in range(PAIR):
            lse_ref[hh] = carry[hh][0] + jnp.log(carry[hh][1])

    return _pcall(
        body, name="attn_fwd", grid=(H // PAIR, T // tq),
        in_specs=[pl.BlockSpec((tq, LANES), lambda hp, i: (i, qb + hp)),
                  pl.BlockSpec((T, LANES), lambda hp, i: (0, kb + hp)),
                  pl.BlockSpec((T, LANES), lambda hp, i: (0, vb + hp)),
                  pl.BlockSpec((PAIR, tq, 1), lambda hp, i: (hp, i, 0)),
                  pl.BlockSpec((PAIR, nkv, 1, tk), lambda hp, i: (hp, 0, 0, 0))],
        out_specs=[pl.BlockSpec((tq, LANES), lambda hp, i: (i, hp)),
                   pl.BlockSpec((PAIR, tq, 1), lambda hp, i: (hp, i, 0))],
        out_shape=[jax.ShapeDtypeStruct((T, hd), F32), jax.ShapeDtypeStruct((H, T, 1), F32)],
        comm=comm,
    )(proj, proj, proj, cq, ck4)


def attn_bwd(proj, o, dycat, lse, cq, ck4, q_col, do_col, comm=None):
    T = proj.shape[0]
    H, nkv, _, tk = ck4.shape
    tq = tk
    nq = T // tq
    hd = H * HEAD_DIM
    qb, kb, vb = q_col // LANES, (q_col + hd) // LANES, (q_col + 2 * hd) // LANES
    dob = do_col // LANES
    scale = 1.0 / math.sqrt(HEAD_DIM)

    def body(q_ref, k_ref, v_ref, o_ref, do_ref, lse_ref, cq_ref, ck_ref,
             dq_ref, dk_ref, dv_ref, dcq_ref, dck_ref):
        j = pl.program_id(1)
        i = pl.program_id(2)

        def block(diagonal):
            masks = _head_masks(tq)
            q2, k2, v2, do2 = q_ref[...] * scale, k_ref[...], v_ref[...], do_ref[...]
            zero = jnp.zeros_like(q2)
            prod = do2.astype(F32) * o_ref[...]
            rows = pl.ds(pl.multiple_of(i * tq, tq), tq)
            dq_part = dk_part = dv_part = None
            for hh in range(PAIR):
                qh = jnp.where(masks[hh], q2, zero)
                kh = jnp.where(masks[hh], k2, zero)
                doh = jnp.where(masks[hh], do2, zero)
                delta = jnp.sum(jnp.where(masks[hh], prod, 0.0), axis=-1, keepdims=True)
                s = lax.dot_general(qh, k2, NT, preferred_element_type=F32)
                s = s + cq_ref[hh] - ck_ref[hh]
                if diagonal:
                    s = jnp.where(_causal(tq, tk), s, NEG_INF)
                p = jnp.exp(s - lse_ref[hh])
                dp = lax.dot_general(doh, v2, NT, preferred_element_type=F32)
                ds = p * (dp - delta)
                dsb = ds.astype(BF16)
                dv_h = lax.dot_general(p.astype(BF16), doh, TN, preferred_element_type=F32)
                dk_h = lax.dot_general(dsb, qh, TN, preferred_element_type=F32)
                dq_h = jnp.dot(dsb, kh, preferred_element_type=F32)
                dq_part = dq_h if dq_part is None else dq_part + dq_h
                dk_part = dk_h if dk_part is None else dk_part + dk_h
                dv_part = dv_h if dv_part is None else dv_part + dv_h
                dck_h = -jnp.sum(ds, axis=0, keepdims=True)
                dcq_h = jnp.sum(ds, axis=-1, keepdims=True)

                @pl.when(j == 0)
                def _():
                    dcq_ref[hh, rows, :] = dcq_h

                @pl.when(j > 0)
                def _():
                    dcq_ref[hh, rows, :] += dcq_h

                if diagonal:
                    dck_ref[hh] = dck_h
                else:
                    dck_ref[hh] += dck_h
            dq_part = dq_part * scale

            @pl.when(j == 0)
            def _():
                dq_ref[rows, :] = dq_part

            @pl.when(j > 0)
            def _():
                dq_ref[rows, :] += dq_part

            if diagonal:
                dk_ref[...] = dk_part
                dv_ref[...] = dv_part
            else:
                dk_ref[...] += dk_part
                dv_ref[...] += dv_part

        @pl.when(i == j)
        def _():
            block(True)

        @pl.when(i > j)
        def _():
            block(False)

    at_q = lambda col: pl.BlockSpec((tq, LANES), lambda hp, j, i: (jnp.maximum(i, j), col + hp))
    at_k = lambda col: pl.BlockSpec((tk, LANES), lambda hp, j, i: (j, col + hp))
    col_spec = pl.BlockSpec((PAIR, tq, 1), lambda hp, j, i: (hp, jnp.maximum(i, j), 0))
    ck_spec = pl.BlockSpec((PAIR, None, 1, tk), lambda hp, j, i: (hp, j, 0, 0))
    return _pcall(
        body, name="attn_bwd", grid=(H // PAIR, nkv, nq),
        in_specs=[at_q(qb), at_k(kb), at_k(vb), at_q(0), at_q(dob), col_spec, col_spec, ck_spec],
        out_specs=[pl.BlockSpec((T, LANES), lambda hp, j, i: (0, hp)), at_k(0), at_k(0),
                   pl.BlockSpec((PAIR, T, 1), lambda hp, j, i: (hp, 0, 0)), ck_spec],
        out_shape=[jax.ShapeDtypeStruct((T, hd), F32)] * 3
        + [jax.ShapeDtypeStruct((H, T, 1), F32), jax.ShapeDtypeStruct((H, nkv, 1, tk), F32)],
        comm=comm,
    )(proj, proj, proj, o, dycat, lse, cq, ck4)


def _rows_tile(rows, cols, bytes_per_row_set):
    target = max(8, (2 * 1024 * 1024) // max(1, bytes_per_row_set))
    if rows <= target:
        return rows
    t = (target // 16) * 16
    while t >= 16:
        if rows % t == 0:
            return t
        t -= 16
    return rows


def sum_chips(recv):
    nc, R, C = recv.shape
    tr = _rows_tile(R, C, C * 4)

    def body(r_ref, o_ref):
        acc = r_ref[0].astype(F32)
        for j in range(1, nc):
            acc = acc + r_ref[j].astype(F32)
        o_ref[...] = acc

    return _pcall(
        body, name="sum_chips", grid=(R // tr,),
        in_specs=[pl.BlockSpec((nc, tr, C), lambda i: (0, i, 0))],
        out_specs=[pl.BlockSpec((tr, C), lambda i: (i, 0))],
        out_shape=[jax.ShapeDtypeStruct((R, C), F32)],
    )(recv)[0]


def adamw(w, m, v, g_parts, comm=None):
    R, C = w.shape
    tr = _rows_tile(R, C, C * 4 * 4)
    n_g = len(g_parts)
    c1 = 1.0 - ADAM_B1
    c2 = 1.0 - ADAM_B2
    bc1 = 1.0 - ADAM_B1 ** ADAM_STEP
    bc2 = 1.0 - ADAM_B2 ** ADAM_STEP

    def body(*refs):
        w_ref, m_ref, v_ref = refs[:3]
        g_refs = refs[3:3 + n_g]
        g_out, d_out, m_out, v_out = refs[3 + n_g:]
        g = g_refs[0][...]
        for r in g_refs[1:]:
            g = g + r[...]
        m_new = ADAM_B1 * m_ref[...] + c1 * g
        v_new = ADAM_B2 * v_ref[...] + c2 * (g * g)
        m_hat = m_new / bc1
        v_hat = v_new / bc2
        g_out[...] = g
        d_out[...] = -ADAM_LR * (m_hat / (jnp.sqrt(v_hat) + ADAM_EPS) + ADAM_WD * w_ref[...])
        m_out[...] = m_new
        v_out[...] = v_new

    spec = pl.BlockSpec((tr, C), lambda i: (i, 0))
    return _pcall(
        body, name="adamw", grid=(R // tr,),
        in_specs=[spec] * (3 + n_g), out_specs=[spec] * 4,
        out_shape=[jax.ShapeDtypeStruct((R, C), F32)] * 4, comm=comm,
    )(w, m, v, *g_parts)


def _chip_coords():
    x, y, c = lax.axis_index("x"), lax.axis_index("y"), lax.axis_index("c")
    others = [(1 - x, y), (x, 1 - y), (1 - x, 1 - y)]
    return x, y, c, others


def _remote(src, dst, send_sem, recv_sem, device):
    return pltpu.make_async_remote_copy(src_ref=src, dst_ref=dst, send_sem=send_sem, recv_sem=recv_sem,
                                        device_id=device, device_id_type=MESH)


def gather_comm(shards):
    n = len(shards)
    pieces = [(a, jj) for a in range(n) for jj in range(3)]

    def makers(ins, outs, sems):
        send_sems, recv_sems, local_sems = sems
        x, y, c, others = _chip_coords()
        me = 2 * x + y
        sibling = (x, y, 1 - c)
        half = lambda a: ins[a].shape[0] // 2

        def local(a):
            return pltpu.make_async_copy(ins[a], outs[a].at[me], local_sems.at[a])

        def ici(a, jj):
            ox, oy = others[jj]
            return _remote(ins[a].at[pl.ds(c * half(a), half(a))], outs[a].at[me, pl.ds(c * half(a), half(a))],
                           send_sems.at[6 * a + jj], recv_sems.at[6 * a + jj], (ox, oy, c))

        def landed(a, jj):
            ox, oy = others[jj]
            got = outs[a].at[2 * ox + oy, pl.ds(c * half(a), half(a))]
            return (_remote(got, got, send_sems.at[6 * a + jj], recv_sems.at[6 * a + jj], (ox, oy, c)),
                    _remote(got, got, send_sems.at[6 * a + 3 + jj], recv_sems.at[6 * a + 3 + jj], sibling))

        def theirs(a, jj):
            ox, oy = others[jj]
            sib = outs[a].at[2 * ox + oy, pl.ds((1 - c) * half(a), half(a))]
            return _remote(sib, sib, send_sems.at[6 * a + 3 + jj], recv_sems.at[6 * a + 3 + jj], sibling)

        return local, ici, landed, theirs

    def start(ins, outs, sems):
        local, ici, _, _ = makers(ins, outs, sems)
        for a in range(n):
            local(a).start()
        for a, jj in pieces:
            ici(a, jj).start()

    def finish(ins, outs, sems):
        local, ici, landed, theirs = makers(ins, outs, sems)
        forwards = []
        for a, jj in pieces:
            got, fwd = landed(a, jj)
            got.wait_recv()
            fwd.start()
            forwards.append(fwd)
        for a, jj in pieces:
            theirs(a, jj).wait_recv()
        for a, jj in pieces:
            ici(a, jj).wait_send()
        for fwd in forwards:
            fwd.wait_send()
        for a in range(n):
            local(a).wait()

    return Comm(shards, [jax.ShapeDtypeStruct((N_CHIP,) + s.shape, s.dtype) for s in shards],
                [pltpu.SemaphoreType.DMA((6 * n,)), pltpu.SemaphoreType.DMA((6 * n,)),
                 pltpu.SemaphoreType.DMA((n,))], start, finish)


def scatter_comm(grads):
    n = len(grads)
    pieces = [(a, jj) for a in range(n) for jj in range(3)]

    def makers(ins, outs, sems):
        send_sems, recv_sems, local_sems = sems
        x, y, c, others = _chip_coords()
        me = 2 * x + y

        def local(a):
            return pltpu.make_async_copy(ins[a].at[me], outs[a].at[me], local_sems.at[a])

        def ici(a, jj):
            ox, oy = others[jj]
            return _remote(ins[a].at[2 * ox + oy], outs[a].at[me], send_sems.at[3 * a + jj],
                           recv_sems.at[3 * a + jj], (ox, oy, c))

        def landed(a, jj):
            ox, oy = others[jj]
            slot = outs[a].at[2 * ox + oy]
            return _remote(slot, slot, send_sems.at[3 * a + jj], recv_sems.at[3 * a + jj], (ox, oy, c))

        return local, ici, landed

    def start(ins, outs, sems):
        local, ici, _ = makers(ins, outs, sems)
        for a in range(n):
            local(a).start()
        for a, jj in pieces:
            ici(a, jj).start()

    def finish(ins, outs, sems):
        local, ici, landed = makers(ins, outs, sems)
        for a, jj in pieces:
            landed(a, jj).wait_recv()
        for a, jj in pieces:
            ici(a, jj).wait_send()
        for a in range(n):
            local(a).wait()

    return Comm(grads, [jax.ShapeDtypeStruct(g.shape, g.dtype) for g in grads],
                [pltpu.SemaphoreType.DMA((3 * n,)), pltpu.SemaphoreType.DMA((3 * n,)),
                 pltpu.SemaphoreType.DMA((n,))], start, finish)


def swap_comm(parts):
    n = len(parts)

    def copies(ins, outs, sems):
        send_sems, recv_sems = sems
        x, y, c, _ = _chip_coords()
        return [_remote(ins[a], outs[a], send_sems.at[a], recv_sems.at[a], (x, y, 1 - c)) for a in range(n)]

    def start(ins, outs, sems):
        for cp in copies(ins, outs, sems):
            cp.start()

    def finish(ins, outs, sems):
        for cp in copies(ins, outs, sems):
            cp.wait()

    return Comm(parts, [jax.ShapeDtypeStruct(p.shape, p.dtype) for p in parts],
                [pltpu.SemaphoreType.DMA((n,)), pltpu.SemaphoreType.DMA((n,))], start, finish)


def allreduce_small(v):
    R = v.shape[0]

    def body(v_ref, sum_ref, all_ref, send_sems, recv_sems):
        x, y, c = lax.axis_index("x"), lax.axis_index("y"), lax.axis_index("c")
        me = 4 * x + 2 * y + c
        all_ref[me] = v_ref[...]
        copies = []
        for k in range(1, N_DEV):
            px = 1 - x if k & 4 else x
            py = 1 - y if k & 2 else y
            pc = 1 - c if k & 1 else c
            cp = pltpu.make_async_remote_copy(
                src_ref=v_ref, dst_ref=all_ref.at[me], send_sem=send_sems.at[k - 1], recv_sem=recv_sems.at[k - 1],
                device_id=(px, py, pc), device_id_type=MESH)
            cp.start()
            copies.append((cp, 4 * px + 2 * py + pc))
        for k, (cp, peer) in enumerate(copies):
            pltpu.make_async_remote_copy(
                src_ref=v_ref, dst_ref=all_ref.at[peer], send_sem=send_sems.at[k], recv_sem=recv_sems.at[k],
                device_id=(x, y, c), device_id_type=MESH).wait_recv()
        for cp, _ in copies:
            cp.wait_send()
        acc = all_ref[0]
        for d in range(1, N_DEV):
            acc = acc + all_ref[d]
        sum_ref[...] = acc

    vm = pl.BlockSpec(memory_space=pltpu.VMEM)
    return pl.pallas_call(
        body, name="allreduce_small",
        in_specs=[vm], out_specs=[vm, vm],
        out_shape=[jax.ShapeDtypeStruct((R, LANES), F32), jax.ShapeDtypeStruct((N_DEV, R, LANES), F32)],
        scratch_shapes=[pltpu.SemaphoreType.DMA((N_DEV - 1,)), pltpu.SemaphoreType.DMA((N_DEV - 1,))],
    )(v)[0]


SMALL_NAMES = ("ffn1_norm", "mix_norm", "ffn2_norm", "final_norm", "conv_b", "conv_ln_g", "conv_ln_b")


def _pack_small(vecs, bias, conv_w_rows, loss_tile):
    rows = [vecs[n].reshape(-1, LANES) for n in SMALL_NAMES]
    rows.append(bias.reshape(1, LANES))
    rows.append(conv_w_rows.reshape(-1, LANES))
    rows.append(loss_tile[0:1, :])
    packed = jnp.concatenate(rows, axis=0)
    pad = (-packed.shape[0]) % 8
    return jnp.pad(packed, ((0, pad), (0, 0)))


def _unpack_small(packed, sizes, n_conv_rows):
    out, r = {}, 0
    for n in SMALL_NAMES:
        k = sizes[n] // LANES
        out[n] = packed[r:r + k].reshape(-1)
        r += k
    out["fgate_bias"] = packed[r]
    r += 1
    out["conv_w"] = packed[r:r + n_conv_rows]
    r += n_conv_rows
    out["loss"] = packed[r, 0]
    return out


def kernel(x, ffn1_norm, ffn1_w_gate, ffn1_w_up, ffn1_w_down, mix_norm, w_in, fgate_bias, conv_w, conv_b, conv_ln_g, conv_ln_b, w_out, ffn2_norm, ffn2_w_gate, ffn2_w_up, ffn2_w_down, final_norm, loss_target, m_ffn1_norm, m_ffn1_w_gate, m_ffn1_w_up, m_ffn1_w_down, m_mix_norm, m_w_in, m_fgate_bias, m_conv_w, m_conv_b, m_conv_ln_g, m_conv_ln_b, m_w_out, m_ffn2_norm, m_ffn2_w_gate, m_ffn2_w_up, m_ffn2_w_down, m_final_norm, v_ffn1_norm, v_ffn1_w_gate, v_ffn1_w_up, v_ffn1_w_down, v_mix_norm, v_w_in, v_fgate_bias, v_conv_w, v_conv_b, v_conv_ln_g, v_conv_ln_b, v_w_out, v_ffn2_norm, v_ffn2_w_gate, v_ffn2_w_up, v_ffn2_w_down, v_final_norm):
    w = dict(ffn1_norm=ffn1_norm, ffn1_w_gate=ffn1_w_gate, ffn1_w_up=ffn1_w_up, ffn1_w_down=ffn1_w_down,
             mix_norm=mix_norm, w_in=w_in, fgate_bias=fgate_bias, conv_w=conv_w, conv_b=conv_b,
             conv_ln_g=conv_ln_g, conv_ln_b=conv_ln_b, w_out=w_out, ffn2_norm=ffn2_norm,
             ffn2_w_gate=ffn2_w_gate, ffn2_w_up=ffn2_w_up, ffn2_w_down=ffn2_w_down, final_norm=final_norm)
    m = dict(ffn1_norm=m_ffn1_norm, ffn1_w_gate=m_ffn1_w_gate, ffn1_w_up=m_ffn1_w_up, ffn1_w_down=m_ffn1_w_down,
             mix_norm=m_mix_norm, w_in=m_w_in, fgate_bias=m_fgate_bias, conv_w=m_conv_w, conv_b=m_conv_b,
             conv_ln_g=m_conv_ln_g, conv_ln_b=m_conv_ln_b, w_out=m_w_out, ffn2_norm=m_ffn2_norm,
             ffn2_w_gate=m_ffn2_w_gate, ffn2_w_up=m_ffn2_w_up, ffn2_w_down=m_ffn2_w_down, final_norm=m_final_norm)
    v = dict(ffn1_norm=v_ffn1_norm, ffn1_w_gate=v_ffn1_w_gate, ffn1_w_up=v_ffn1_w_up, ffn1_w_down=v_ffn1_w_down,
             mix_norm=v_mix_norm, w_in=v_w_in, fgate_bias=v_fgate_bias, conv_w=v_conv_w, conv_b=v_conv_b,
             conv_ln_g=v_conv_ln_g, conv_ln_b=v_conv_ln_b, w_out=v_w_out, ffn2_norm=v_ffn2_norm,
             ffn2_w_gate=v_ffn2_w_gate, ffn2_w_up=v_ffn2_w_up, ffn2_w_down=v_ffn2_w_down, final_norm=v_final_norm)
    names = list(w.keys())
    big = ("ffn1_w_gate", "ffn1_w_up", "ffn1_w_down", "w_in", "w_out", "ffn2_w_gate", "ffn2_w_up", "ffn2_w_down")

    T, D = x.shape[1], x.shape[2]
    C = conv_b.shape[0]
    H = fgate_bias.shape[0]
    cs = conv_w.shape[1]
    in_cols = N_CHIP * w_in.shape[1]
    p_main = in_cols - H

    x0, tgt = x[0], loss_target[0]
    tk = _tile(T, 512, 128)
    nkv = T // tk
    row = lambda a: a.reshape(1, -1)
    wb = {n: w[n].astype(BF16) for n in big}
    grad, delta, new_m, new_v = {}, {}, {}, {}

    def update(n, parts, comm=None):
        grad[n], delta[n], new_m[n], new_v[n] = adamw(w[n], m[n], v[n], parts, comm=comm)

    g0 = gather_comm([wb["ffn1_w_gate"], jnp.pad(conv_w, ((0, HALO - CONV_K), (0, 0)))])
    wg1, conv_w4 = _run_comm("gather_first", g0)
    conv_w_full = conv_w4.transpose(1, 0, 2).reshape(HALO, C)
    h1, r1 = rms_fwd(x0, row(ffn1_norm))
    g1a = gather_comm([wb["ffn1_w_up"]])
    a1 = ffn_gate(h1, wg1, comm=g1a)
    wu1 = g1a.results[0]
    g1b = gather_comm([wb["ffn1_w_down"]])
    b1, mid1 = ffn_upmul(h1, wu1, a1, comm=g1b)
    wd1 = g1b.results[0]
    g2 = gather_comm([wb["w_in"]])
    x1 = mm_residual("ffn_down_g", mid1, wd1, x0, 0.5, comm=g2)[0]
    w_in4 = g2.results[0]

    w_in_full = w_in4.transpose(1, 0, 2).reshape(D, in_cols)
    w_main = w_in_full[:, :p_main]
    w_f = jnp.pad(w_in_full[:, p_main:], ((0, 0), (0, LANES - H)))
    bias_pad = jnp.pad(row(fgate_bias), ((0, 0), (0, LANES - H)))
    h2, r2 = rms_fwd(x1, row(mix_norm))
    proj = proj_main(h2, w_main)
    f, cum = fgate_fwd(h2, w_f, bias_pad, H)
    ypre, yconv = conv_fwd(proj, conv_w_full, row(conv_b), row(conv_ln_g), row(conv_ln_b))
    c_ht = cum[:, :H].T
    cq = c_ht[:, :, None]
    ck4 = c_ht.reshape(H, nkv, 1, tk)
    g3 = gather_comm([wb["w_out"], wb["ffn2_w_gate"], wb["ffn2_w_up"], wb["ffn2_w_down"]])
    o, lse = attn_fwd(proj, cq, ck4, 2 * C, comm=g3)
    w_out3, wg2, wu2, wd2 = g3.results
    ycat = jnp.concatenate([yconv, o.astype(BF16)], axis=1)
    x2 = mm_residual("out_proj", ycat, w_out3, x1, 1.0)[0]

    h3, r3 = rms_fwd(x2, row(ffn2_norm))
    a2, b2, mid2 = ffn_up(h3, wg2, wu2)
    x3 = mm_residual("ffn_down", mid2, wd2, x2, 0.5)[0]
    dx3, dx3b, loss_tile, d_final = final_loss(x3, tgt, row(final_norm))

    da2, db2 = ffn_bwd_mid(dx3b, wd2, a2, b2)
    dwd2 = dw_rowshard("ffn_dwd", mid2, dx3b, N_CHIP)[0]
    s1 = scatter_comm([dwd2])
    dwg2, dwu2 = dw_colshard("ffn_dwgu_s", h3, [da2, db2], N_CHIP, comm=s1)
    s2 = scatter_comm([dwg2])
    dh3 = ffn_dh(da2, db2, wg2, wu2, comm=s2)
    dx2, dx2b, d_ffn2_norm = rms_bwd(dh3, x2, r3, row(ffn2_norm), dx3, 1.0)

    dycat = mm_nt_bf16("out_proj_dy", dx2b, w_out3.reshape(-1, D))
    dw_out3 = dw_rowshard("out_proj_dw", ycat, dx2b, N_CHIP)[0]
    s3 = scatter_comm([dwu2, dw_out3])
    dq, dk, dv, dcq, dck4 = attn_bwd(proj, o, dycat, lse, cq, ck4, 2 * C, C, comm=s3)
    dc = jnp.pad((dcq[:, :, 0] + dck4.reshape(H, T)).T, ((0, 0), (0, LANES - H)))
    df, d_bias = fgate_bwd(dc, f, H)
    dag, d_conv_w, d_conv_b, d_ln_g, d_ln_b = conv_bwd(proj, ypre, dycat, conv_w_full, row(conv_ln_g),
                                                       row(conv_ln_b))
    dproj = jnp.concatenate([dag, dq.astype(BF16), dk.astype(BF16), dv.astype(BF16)], axis=1)
    early = ("ffn2_w_down", "ffn2_w_gate", "ffn2_w_up", "w_out")
    early_sums = [sum_chips(r) for r in (s1.results[0], s2.results[0], s3.results[0], s3.results[1])]
    sw1 = swap_comm(early_sums)
    dh2 = proj_dh(dproj, w_main, df, w_f, comm=sw1)
    dw_main = dw_plain("proj_dw_main", h2, dproj)
    dw_f = dw_plain("proj_dw_f", h2, df)
    dw_in_full = jnp.concatenate([dw_main, dw_f[:, :H]], axis=1)
    dw_in3 = dw_in_full.reshape(D, N_CHIP, in_cols // N_CHIP).transpose(1, 0, 2).astype(BF16)
    dx1, dx1b, d_mix_norm = rms_bwd(dh2, x1, r2, row(mix_norm), dx2, 0.5)

    s4 = scatter_comm([dw_in3])
    da1, db1 = ffn_bwd_mid(dx1b, wd1, a1, b1, comm=s4)
    dwd1 = dw_rowshard("ffn_dwd", mid1, dx1b, N_CHIP)[0]
    s5 = scatter_comm([dwd1])
    dwg1, dwu1 = dw_colshard("ffn_dwgu_s", h1, [da1, db1], N_CHIP, comm=s5)
    s6 = scatter_comm([dwg1])
    dh1 = ffn_dh(da1, db1, wg1, wu1, comm=s6)
    grad_x, _, d_ffn1_norm = rms_bwd(dh1, x0, r1, row(ffn1_norm), dx1, 1.0)

    s7 = scatter_comm([dwu1])
    for i, (n, mine, other) in enumerate(zip(early, early_sums, sw1.results)):
        update(n, [mine, other], comm=s7 if i == 0 else None)
    late = ("w_in", "ffn1_w_down", "ffn1_w_gate", "ffn1_w_up")
    late_sums = [sum_chips(r) for r in (s4.results[0], s5.results[0], s6.results[0], s7.results[0])]
    late_theirs = _run_comm("swap_last", swap_comm(late_sums))
    for n, mine, other in zip(late, late_sums, late_theirs):
        update(n, [mine, other])

    gl = dict(ffn1_norm=d_ffn1_norm, mix_norm=d_mix_norm, ffn2_norm=d_ffn2_norm, final_norm=d_final,
              conv_b=d_conv_b, conv_ln_g=d_ln_g, conv_ln_b=d_ln_b)
    small_sizes = {n: w[n].shape[0] for n in SMALL_NAMES}
    packed = _pack_small(gl, d_bias, d_conv_w, loss_tile)
    red = _unpack_small(allreduce_small(packed), small_sizes, HALO * C // LANES)
    loss = red["loss"]
    my_chip = 2 * lax.axis_index("x") + lax.axis_index("y")
    g_conv_w = lax.dynamic_slice_in_dim(red["conv_w"].reshape(HALO, C)[:CONV_K], my_chip * cs, cs, axis=1)
    update("conv_w", [g_conv_w])
    vec_names = SMALL_NAMES + ("fgate_bias",)
    stack = lambda d: jnp.concatenate(
        [jnp.pad(d[n], (0, (-d[n].shape[0]) % LANES)).reshape(-1, LANES) for n in vec_names], axis=0)
    g_stack = jnp.concatenate([red[n].reshape(-1, LANES) for n in SMALL_NAMES] + [red["fgate_bias"][None, :]],
                              axis=0)
    outs = adamw(stack(w), stack(m), stack(v), [g_stack])
    r = 0
    for n in vec_names:
        size = w[n].shape[0]
        k = -(-size // LANES)
        for dst, src in zip((grad, delta, new_m, new_v), outs):
            dst[n] = src[r:r + k].reshape(-1)[:size]
        r += k

    return (loss, grad_x[None], *[grad[n] for n in names], *[delta[n] for n in names],
            *[new_m[n] for n in names], *[new_v[n] for n in names])
```

```python
import functools
import math

import jax
import jax.numpy as jnp
from jax import lax
from jax.experimental import pallas as pl
from jax.experimental.pallas import tpu as pltpu

F32 = jnp.float32
BF16 = jnp.bfloat16
NORM_EPS = 1e-6
LN_EPS = 1e-5
NEG_INF = -1e30
HEAD_DIM = 64
CONV_K = 31
HALO = 32
LANES = 128
N_CHIP = 4
N_DEV = 8
VMEM_LIMIT = 52 * 1024 * 1024
MESH = pl.DeviceIdType.MESH

ADAM_LR = 0.001
ADAM_B1 = 0.9
ADAM_B2 = 0.999
ADAM_EPS = 1e-08
ADAM_WD = 0.01
ADAM_STEP = 10

NN = (((1,), (0,)), ((), ()))
NT = (((1,), (1,)), ((), ()))
TN = (((0,), (0,)), ((), ()))


def _tile(n, pref, unit=128):
    if n <= pref:
        return n
    t = (pref // unit) * unit
    while t > 0:
        if n % t == 0:
            return t
        t -= unit
    raise ValueError(f"no tile for {n} under {pref}")


class Comm:
    def __init__(self, operands, out_shape, sems, start, finish):
        self.operands, self.out_shape, self.sems = list(operands), list(out_shape), list(sems)
        self.start, self.finish = start, finish
        self.results = None


def _pcall(body, *, name, grid, in_specs, out_specs, out_shape, scratch=(), comm=None):
    params = pltpu.CompilerParams(dimension_semantics=("arbitrary",) * len(grid), vmem_limit_bytes=VMEM_LIMIT)
    scratch = list(scratch)
    if comm is None:
        return pl.pallas_call(body, name=name, grid=grid, in_specs=in_specs, out_specs=out_specs,
                              out_shape=out_shape, scratch_shapes=scratch, compiler_params=params)
    n_in, n_out, n_s = len(in_specs), len(out_shape), len(scratch)
    n_ci, n_co = len(comm.operands), len(comm.out_shape)
    any_spec = pl.BlockSpec(memory_space=pl.ANY)

    def carried(*refs):
        ins, refs = refs[:n_in], refs[n_in:]
        c_ins, refs = refs[:n_ci], refs[n_ci:]
        outs, refs = refs[:n_out], refs[n_out:]
        c_outs, refs = refs[:n_co], refs[n_co:]
        scr, c_sems = refs[:n_s], refs[n_s:]
        first = pl.program_id(0) == 0
        last = pl.program_id(0) == grid[0] - 1
        for d in range(1, len(grid)):
            first = jnp.logical_and(first, pl.program_id(d) == 0)
            last = jnp.logical_and(last, pl.program_id(d) == grid[d] - 1)

        @pl.when(first)
        def _():
            comm.start(c_ins, c_outs, c_sems)

        body(*ins, *outs, *scr)

        @pl.when(last)
        def _():
            comm.finish(c_ins, c_outs, c_sems)

    call = pl.pallas_call(
        carried, name=name, grid=grid, in_specs=list(in_specs) + [any_spec] * n_ci,
        out_specs=list(out_specs) + [any_spec] * n_co, out_shape=list(out_shape) + comm.out_shape,
        scratch_shapes=scratch + comm.sems, compiler_params=params)

    def run(*operands):
        res = call(*operands, *comm.operands)
        comm.results = list(res[n_out:])
        return list(res[:n_out])

    return run


def _run_comm(name, comm):
    n_ci, n_co = len(comm.operands), len(comm.out_shape)
    any_spec = pl.BlockSpec(memory_space=pl.ANY)

    def body(*refs):
        c_ins, c_outs, c_sems = refs[:n_ci], refs[n_ci:n_ci + n_co], refs[n_ci + n_co:]
        comm.start(c_ins, c_outs, c_sems)
        comm.finish(c_ins, c_outs, c_sems)

    return pl.pallas_call(body, name=name, in_specs=[any_spec] * n_ci, out_specs=[any_spec] * n_co,
                          out_shape=comm.out_shape, scratch_shapes=comm.sems)(*comm.operands)


def _sigmoid(x):
    return 1.0 / (1.0 + jnp.exp(-x))


def _mm(name, *, grid, pairs, once_pairs=(), extra=(), out_shape, out_specs, acc_shapes, nk, kaxis, epilogue,
        comm=None):
    all_pairs = list(pairs) + list(once_pairs)
    n_p, n_o = len(pairs), len(once_pairs)
    n_e, n_out, n_acc = len(extra), len(out_shape), len(acc_shapes)

    def body(*refs):
        ab = refs[: 2 * (n_p + n_o)]
        ex = refs[2 * (n_p + n_o): 2 * (n_p + n_o) + n_e]
        outs = refs[2 * (n_p + n_o) + n_e: 2 * (n_p + n_o) + n_e + n_out]
        accs = refs[2 * (n_p + n_o) + n_e + n_out:]

        def dots(idx_range):
            vals = [None] * n_acc
            for p in idx_range:
                d = lax.dot_general(ab[2 * p][...], ab[2 * p + 1][...], all_pairs[p][4],
                                    preferred_element_type=F32)
                ai = all_pairs[p][5]
                vals[ai] = d if vals[ai] is None else vals[ai] + d
            return vals

        if nk == 1:
            vals = dots(range(n_p + n_o))
            epilogue(vals, ex, outs)
            return

        k = pl.program_id(kaxis)

        @pl.when(k == 0)
        def _():
            vals = dots(range(n_p + n_o))
            for ai in range(n_acc):
                accs[ai][...] = vals[ai]

        @pl.when(k > 0)
        def _():
            vals = dots(range(n_p))
            for ai in range(n_acc):
                if vals[ai] is not None:
                    accs[ai][...] += vals[ai]

        @pl.when(k == nk - 1)
        def _():
            epilogue([a[...] for a in accs], ex, outs)

    operands, in_specs = [], []
    for p in all_pairs:
        operands += [p[0], p[2]]
        in_specs += [p[1], p[3]]
    for arr, spec in extra:
        operands.append(arr)
        in_specs.append(spec)
    scratch = [pltpu.VMEM(s, F32) for s in acc_shapes] if nk > 1 else []
    return _pcall(body, name=name, grid=grid, in_specs=in_specs, out_specs=out_specs, out_shape=out_shape,
                  scratch=scratch, comm=comm)(*operands)


def rms_fwd(x, g):
    T, D = x.shape
    tt = _tile(T, 512, 8)

    def body(x_ref, g_ref, h_ref, r_ref):
        xv = x_ref[...]
        r = lax.rsqrt(jnp.mean(xv * xv, axis=-1, keepdims=True) + NORM_EPS)
        h_ref[...] = (xv * r * g_ref[...]).astype(BF16)
        r_ref[...] = r

    return _pcall(
        body, name="rms_fwd", grid=(T // tt,),
        in_specs=[pl.BlockSpec((tt, D), lambda i: (i, 0)), pl.BlockSpec((1, D), lambda i: (0, 0))],
        out_specs=[pl.BlockSpec((tt, D), lambda i: (i, 0)), pl.BlockSpec((tt, 1), lambda i: (i, 0))],
        out_shape=[jax.ShapeDtypeStruct((T, D), BF16), jax.ShapeDtypeStruct((T, 1), F32)],
    )(x, g)


def rms_bwd(dh, x, r, g, dres, out_scale):
    T, D = x.shape
    tt = _tile(T, 256, 8)

    def body(dh_ref, x_ref, r_ref, g_ref, dres_ref, dx_ref, dxb_ref, dg_ref):
        i = pl.program_id(0)
        xh = x_ref[...] * r_ref[...]
        dhv = dh_ref[...]
        dxh = dhv * g_ref[...]
        dx = dres_ref[...] + r_ref[...] * (dxh - xh * jnp.mean(dxh * xh, axis=-1, keepdims=True))
        dx_ref[...] = dx
        dxb_ref[...] = (out_scale * dx).astype(BF16)
        part = jnp.sum(dhv * xh, axis=0, keepdims=True)

        @pl.when(i == 0)
        def _():
            dg_ref[...] = part

        @pl.when(i > 0)
        def _():
            dg_ref[...] += part

    row = pl.BlockSpec((tt, D), lambda i: (i, 0))
    return _pcall(
        body, name="rms_bwd", grid=(T // tt,),
        in_specs=[row, row, pl.BlockSpec((tt, 1), lambda i: (i, 0)), pl.BlockSpec((1, D), lambda i: (0, 0)), row],
        out_specs=[row, row, pl.BlockSpec((1, D), lambda i: (0, 0))],
        out_shape=[jax.ShapeDtypeStruct((T, D), F32), jax.ShapeDtypeStruct((T, D), BF16),
                   jax.ShapeDtypeStruct((1, D), F32)],
    )(dh, x, r, g, dres)


def final_loss(x, tgt, g):
    T, D = x.shape
    tt = _tile(T, 256, 8)

    def body(x_ref, t_ref, g_ref, dx_ref, dxb_ref, loss_ref, dg_ref):
        i = pl.program_id(0)
        xv = x_ref[...]
        r = lax.rsqrt(jnp.mean(xv * xv, axis=-1, keepdims=True) + NORM_EPS)
        xh = xv * r
        err = xh * g_ref[...] - t_ref[...]
        part_loss = 0.5 * jnp.sum(jnp.mean(err * err, axis=-1, keepdims=True), axis=0, keepdims=True)
        dy = err * (1.0 / D)
        dxh = dy * g_ref[...]
        dx = r * (dxh - xh * jnp.mean(dxh * xh, axis=-1, keepdims=True))
        dx_ref[...] = dx
        dxb_ref[...] = (0.5 * dx).astype(BF16)
        part_g = jnp.sum(dy * xh, axis=0, keepdims=True)
        part_l = jnp.broadcast_to(part_loss, (8, LANES))

        @pl.when(i == 0)
        def _():
            dg_ref[...] = part_g
            loss_ref[...] = part_l

        @pl.when(i > 0)
        def _():
            dg_ref[...] += part_g
            loss_ref[...] += part_l

    row = pl.BlockSpec((tt, D), lambda i: (i, 0))
    return _pcall(
        body, name="final_loss", grid=(T // tt,),
        in_specs=[row, row, pl.BlockSpec((1, D), lambda i: (0, 0))],
        out_specs=[row, row, pl.BlockSpec((8, LANES), lambda i: (0, 0)), pl.BlockSpec((1, D), lambda i: (0, 0))],
        out_shape=[jax.ShapeDtypeStruct((T, D), F32), jax.ShapeDtypeStruct((T, D), BF16),
                   jax.ShapeDtypeStruct((8, LANES), F32), jax.ShapeDtypeStruct((1, D), F32)],
    )(x, tgt, g)


def ffn_gate(h, wg3, comm=None):
    T, D = h.shape
    nc, _, fs = wg3.shape
    tm = _tile(T, 512, 8)

    def epilogue(vals, ex, outs):
        outs[0][...] = vals[0].astype(BF16)

    return _mm("ffn_gate", grid=(nc, T // tm),
               pairs=[(h, pl.BlockSpec((tm, D), lambda j, i: (i, 0)),
                       wg3, pl.BlockSpec((None, D, fs), lambda j, i: (j, 0, 0)), NN, 0)],
               out_shape=[jax.ShapeDtypeStruct((T, nc * fs), BF16)],
               out_specs=[pl.BlockSpec((tm, fs), lambda j, i: (i, j))],
               acc_shapes=[(tm, fs)], nk=1, kaxis=None, epilogue=epilogue, comm=comm)[0]


def ffn_upmul(h, wu3, a, comm=None):
    T, D = h.shape
    nc, _, fs = wu3.shape
    tm = _tile(T, 512, 8)

    def epilogue(vals, ex, outs):
        b = vals[0]
        av = ex[0][...].astype(F32)
        outs[0][...] = b.astype(BF16)
        outs[1][...] = (av * _sigmoid(av) * b).astype(BF16)

    t_spec = pl.BlockSpec((tm, fs), lambda j, i: (i, j))
    o_shape = jax.ShapeDtypeStruct((T, nc * fs), BF16)
    return _mm("ffn_upmul", grid=(nc, T // tm),
               pairs=[(h, pl.BlockSpec((tm, D), lambda j, i: (i, 0)),
                       wu3, pl.BlockSpec((None, D, fs), lambda j, i: (j, 0, 0)), NN, 0)],
               extra=[(a, t_spec)], out_shape=[o_shape] * 2, out_specs=[t_spec] * 2,
               acc_shapes=[(tm, fs)], nk=1, kaxis=None, epilogue=epilogue, comm=comm)


def ffn_up(h, wg3, wu3, comm=None):
    T, D = h.shape
    nc, _, fs = wg3.shape
    tm = _tile(T, 512, 8)

    def epilogue(vals, ex, outs):
        a, b = vals
        outs[0][...] = a.astype(BF16)
        outs[1][...] = b.astype(BF16)
        outs[2][...] = (a * _sigmoid(a) * b).astype(BF16)

    h_spec = pl.BlockSpec((tm, D), lambda j, i: (i, 0))
    w_spec = pl.BlockSpec((None, D, fs), lambda j, i: (j, 0, 0))
    o_spec = pl.BlockSpec((tm, fs), lambda j, i: (i, j))
    o_shape = jax.ShapeDtypeStruct((T, nc * fs), BF16)
    return _mm("ffn_up", grid=(nc, T // tm),
               pairs=[(h, h_spec, wg3, w_spec, NN, 0), (h, h_spec, wu3, w_spec, NN, 1)],
               out_shape=[o_shape] * 3, out_specs=[o_spec] * 3, acc_shapes=[(tm, fs)] * 2, nk=1, kaxis=None,
               epilogue=epilogue, comm=comm)


def mm_residual(name, a, b3, res, scale, comm=None):
    T = a.shape[0]
    nk, tk, N = b3.shape
    tm, tn = _tile(T, 512, 8), _tile(N, 2048)

    def epilogue(vals, ex, outs):
        outs[0][...] = ex[0][...] + scale * vals[0]

    return _mm(name, grid=(T // tm, N // tn, nk),
               pairs=[(a, pl.BlockSpec((tm, tk), lambda i, n, k: (i, k)),
                       b3, pl.BlockSpec((None, tk, tn), lambda i, n, k: (k, 0, n)), NN, 0)],
               extra=[(res, pl.BlockSpec((tm, tn), lambda i, n, k: (i, n)))],
               out_shape=[jax.ShapeDtypeStruct((T, N), F32)],
               out_specs=[pl.BlockSpec((tm, tn), lambda i, n, k: (i, n))],
               acc_shapes=[(tm, tn)], nk=nk, kaxis=2, epilogue=epilogue, comm=comm)


def ffn_bwd_mid(dout, wd3, a, b, comm=None):
    T, D = dout.shape
    nc, fs, _ = wd3.shape
    tm = _tile(T, 512, 8)

    def epilogue(vals, ex, outs):
        dm = vals[0]
        av = ex[0][...].astype(F32)
        bv = ex[1][...].astype(F32)
        s = _sigmoid(av)
        outs[0][...] = (dm * bv * (s * (1.0 + av * (1.0 - s)))).astype(BF16)
        outs[1][...] = (dm * (av * s)).astype(BF16)

    t_spec = pl.BlockSpec((tm, fs), lambda j, i: (i, j))
    o_shape = jax.ShapeDtypeStruct((T, nc * fs), BF16)
    return _mm("ffn_bwd_mid", grid=(nc, T // tm),
               pairs=[(dout, pl.BlockSpec((tm, D), lambda j, i: (i, 0)),
                       wd3, pl.BlockSpec((None, fs, D), lambda j, i: (j, 0, 0)), NT, 0)],
               extra=[(a, t_spec), (b, t_spec)],
               out_shape=[o_shape] * 2, out_specs=[t_spec] * 2, acc_shapes=[(tm, fs)], nk=1, kaxis=None,
               epilogue=epilogue, comm=comm)


def dw_rowshard(name, a, b, nc, comm=None):
    T, M = a.shape
    N = b.shape[1]
    ms = M // nc
    tn, tk = _tile(N, 1024), _tile(T, 1024, 16)

    def epilogue(vals, ex, outs):
        outs[0][...] = vals[0].astype(BF16)

    return _mm(name, grid=(nc, N // tn, T // tk),
               pairs=[(a, pl.BlockSpec((tk, ms), lambda j, n, k: (k, j)),
                       b, pl.BlockSpec((tk, tn), lambda j, n, k: (k, n)), TN, 0)],
               out_shape=[jax.ShapeDtypeStruct((nc, ms, N), BF16)],
               out_specs=[pl.BlockSpec((None, ms, tn), lambda j, n, k: (j, 0, n))],
               acc_shapes=[(ms, tn)], nk=T // tk, kaxis=2, epilogue=epilogue, comm=comm)


def dw_colshard(name, a, bs, nc, comm=None):
    T, M = a.shape
    ns = bs[0].shape[1] // nc
    tm, tk = _tile(M, 512), _tile(T, 1024, 16)

    def epilogue(vals, ex, outs):
        for v, o in zip(vals, outs):
            o[...] = v.astype(BF16)

    a_spec = pl.BlockSpec((tk, tm), lambda j, m, k: (k, m))
    b_spec = pl.BlockSpec((tk, ns), lambda j, m, k: (k, j))
    return _mm(name, grid=(nc, M // tm, T // tk),
               pairs=[(a, a_spec, b, b_spec, TN, p) for p, b in enumerate(bs)],
               out_shape=[jax.ShapeDtypeStruct((nc, M, ns), BF16)] * len(bs),
               out_specs=[pl.BlockSpec((None, tm, ns), lambda j, m, k: (j, m, 0))] * len(bs),
               acc_shapes=[(tm, ns)] * len(bs), nk=T // tk, kaxis=2, epilogue=epilogue, comm=comm)


def ffn_dh(da, db, wg3, wu3, comm=None):
    T = da.shape[0]
    nc, D, fs = wg3.shape
    tm, tn = _tile(T, 512, 8), _tile(D, 1024)

    def epilogue(vals, ex, outs):
        outs[0][...] = vals[0]

    a_spec = pl.BlockSpec((tm, fs), lambda i, n, k: (i, k))
    w_spec = pl.BlockSpec((None, tn, fs), lambda i, n, k: (k, n, 0))
    return _mm("ffn_dh", grid=(T // tm, D // tn, nc),
               pairs=[(da, a_spec, wg3, w_spec, NT, 0), (db, a_spec, wu3, w_spec, NT, 0)],
               out_shape=[jax.ShapeDtypeStruct((T, D), F32)],
               out_specs=[pl.BlockSpec((tm, tn), lambda i, n, k: (i, n))],
               acc_shapes=[(tm, tn)], nk=nc, kaxis=2, epilogue=epilogue, comm=comm)[0]


def proj_main(h, w_t, P):
    T, D = h.shape
    tm, tn = _tile(T, 512, 8), _tile(P, 1024)

    def epilogue(vals, ex, outs):
        outs[0][...] = vals[0].astype(BF16)

    return _mm("proj_main", grid=(P // tn, T // tm),
               pairs=[(h, pl.BlockSpec((tm, D), lambda j, i: (i, 0)),
                       w_t, pl.BlockSpec((tn, D), lambda j, i: (j, 0)), NT, 0)],
               out_shape=[jax.ShapeDtypeStruct((T, P), BF16)],
               out_specs=[pl.BlockSpec((tm, tn), lambda j, i: (i, j))],
               acc_shapes=[(tm, tn)], nk=1, kaxis=None, epilogue=epilogue)[0]


def mm_nt_bf16(name, a, w):
    T, K = a.shape
    M = w.shape[0]
    tm, tn = _tile(T, 512, 8), _tile(M, 1024)

    def epilogue(vals, ex, outs):
        outs[0][...] = vals[0].astype(BF16)

    return _mm(name, grid=(T // tm, M // tn),
               pairs=[(a, pl.BlockSpec((tm, K), lambda i, n: (i, 0)),
                       w, pl.BlockSpec((tn, K), lambda i, n: (n, 0)), NT, 0)],
               out_shape=[jax.ShapeDtypeStruct((T, M), BF16)],
               out_specs=[pl.BlockSpec((tm, tn), lambda i, n: (i, n))],
               acc_shapes=[(tm, tn)], nk=1, kaxis=None, epilogue=epilogue)[0]


def proj_dh(dproj, w_t, df, wf_t, comm=None):
    T, P = dproj.shape
    D = w_t.shape[1]
    tm, tn, tk = _tile(T, 512, 8), _tile(D, 1024), _tile(P, 1280)

    def epilogue(vals, ex, outs):
        outs[0][...] = vals[0]

    return _mm("proj_dh", grid=(T // tm, D // tn, P // tk),
               pairs=[(dproj, pl.BlockSpec((tm, tk), lambda i, n, k: (i, k)),
                       w_t, pl.BlockSpec((tk, tn), lambda i, n, k: (k, n)), NN, 0)],
               once_pairs=[(df, pl.BlockSpec((tm, LANES), lambda i, n, k: (i, 0)),
                            wf_t, pl.BlockSpec((LANES, tn), lambda i, n, k: (0, n)), NN, 0)],
               out_shape=[jax.ShapeDtypeStruct((T, D), F32)],
               out_specs=[pl.BlockSpec((tm, tn), lambda i, n, k: (i, n))],
               acc_shapes=[(tm, tn)], nk=P // tk, kaxis=2, epilogue=epilogue, comm=comm)[0]


def proj_dw(dproj, df, h, rows):
    T, P = dproj.shape
    D = h.shape[1]
    tm, tn, tk = _tile(P, 1280), _tile(D, 1024), _tile(T, 1024, 16)

    def to_bf16(vals, ex, outs):
        outs[0][...] = vals[0].astype(BF16)

    def to_f32(vals, ex, outs):
        outs[0][...] = vals[0]

    main = _mm("proj_dw_main", grid=(P // tm, D // tn, T // tk),
               pairs=[(dproj, pl.BlockSpec((tk, tm), lambda m, n, k: (k, m)),
                       h, pl.BlockSpec((tk, tn), lambda m, n, k: (k, n)), TN, 0)],
               out_shape=[jax.ShapeDtypeStruct((rows, D), BF16)],
               out_specs=[pl.BlockSpec((tm, tn), lambda m, n, k: (m, n))],
               acc_shapes=[(tm, tn)], nk=T // tk, kaxis=2, epilogue=to_bf16)[0]
    gate = _mm("proj_dw_f", grid=(1, D // tn, T // tk),
               pairs=[(df, pl.BlockSpec((tk, LANES), lambda m, n, k: (k, 0)),
                       h, pl.BlockSpec((tk, tn), lambda m, n, k: (k, n)), TN, 0)],
               out_shape=[jax.ShapeDtypeStruct((LANES, D), F32)],
               out_specs=[pl.BlockSpec((LANES, tn), lambda m, n, k: (0, n))],
               acc_shapes=[(LANES, tn)], nk=T // tk, kaxis=2, epilogue=to_f32)[0]
    return main, gate


def fgate_fwd(h, wf_t, bias, n_heads):
    T, D = h.shape
    tt = _tile(T, 512, 8)

    def body(h_ref, w_ref, b_ref, f_ref, c_ref, carry):
        i = pl.program_id(0)

        @pl.when(i == 0)
        def _():
            carry[...] = jnp.zeros_like(carry)

        f = lax.dot_general(h_ref[...], w_ref[...], NT, preferred_element_type=F32) + b_ref[...]
        logf = jnp.minimum(f, 0.0) - jnp.log(1.0 + jnp.exp(-jnp.abs(f)))
        tri = (lax.broadcasted_iota(jnp.int32, (tt, tt), 0) >= lax.broadcasted_iota(jnp.int32, (tt, tt), 1))
        cs = jnp.dot(tri.astype(F32), logf, preferred_element_type=F32, precision=lax.Precision.HIGHEST)
        c = cs + carry[...]
        f_ref[...] = f
        c_ref[...] = c
        carry[...] = c[tt - 1:tt, :]

    row = pl.BlockSpec((tt, LANES), lambda i: (i, 0))
    return _pcall(
        body, name="fgate_fwd", grid=(T // tt,),
        in_specs=[pl.BlockSpec((tt, D), lambda i: (i, 0)), pl.BlockSpec((LANES, D), lambda i: (0, 0)),
                  pl.BlockSpec((1, LANES), lambda i: (0, 0))],
        out_specs=[row, row],
        out_shape=[jax.ShapeDtypeStruct((T, LANES), F32)] * 2,
        scratch=[pltpu.VMEM((1, LANES), F32)],
    )(h, wf_t, bias)


def fgate_bwd(dc, f, n_heads):
    T = dc.shape[0]
    tt = _tile(T, 512, 8)
    nt = T // tt

    def body(dc_ref, f_ref, df_ref, db_ref, carry):
        i = pl.program_id(0)

        @pl.when(i == 0)
        def _():
            carry[...] = jnp.zeros_like(carry)

        tri = (lax.broadcasted_iota(jnp.int32, (tt, tt), 1) >= lax.broadcasted_iota(jnp.int32, (tt, tt), 0))
        rs = jnp.dot(tri.astype(F32), dc_ref[...], preferred_element_type=F32,
                     precision=lax.Precision.HIGHEST) + carry[...]
        carry[...] = rs[0:1, :]
        lane = lax.broadcasted_iota(jnp.int32, (tt, LANES), 1)
        df = jnp.where(lane < n_heads, rs * _sigmoid(-f_ref[...]), 0.0)
        df_ref[...] = df.astype(BF16)
        part = jnp.sum(df, axis=0, keepdims=True)

        @pl.when(i == 0)
        def _():
            db_ref[...] = part

        @pl.when(i > 0)
        def _():
            db_ref[...] += part

    rev = pl.BlockSpec((tt, LANES), lambda i: (nt - 1 - i, 0))
    return _pcall(
        body, name="fgate_bwd", grid=(nt,),
        in_specs=[rev, rev],
        out_specs=[rev, pl.BlockSpec((1, LANES), lambda i: (0, 0))],
        out_shape=[jax.ShapeDtypeStruct((T, LANES), BF16), jax.ShapeDtypeStruct((1, LANES), F32)],
        scratch=[pltpu.VMEM((1, LANES), F32)],
    )(dc, f)


SUBLANES = 8
SHIFT_ROWS = HALO - SUBLANES


def _shifted_copies(buf, sh, tt):
    for r in range(1, SUBLANES):
        sh[r - 1, 0:tt + SHIFT_ROWS, :] = buf[pl.ds(r, tt + SHIFT_ROWS), :]


def _tap(buf, sh, offset, tt):
    q, r = divmod(offset, SUBLANES)
    if r == 0:
        return buf[pl.ds(SUBLANES * q, tt), :]
    return sh[r - 1, pl.ds(SUBLANES * q, tt), :]


def conv_fwd(proj, conv_w, conv_b, ln_g, ln_b):
    T = proj.shape[0]
    C = conv_w.shape[1]
    tt = _tile(T, 256, HALO)
    hb = tt // HALO

    def body(a_ref, g_ref, ah_ref, gh_ref, w_ref, cb_ref, lg_ref, lb_ref, ypre_ref, y_ref, ubuf, ush):
        i = pl.program_id(0)
        u = a_ref[...].astype(F32) * _sigmoid(g_ref[...].astype(F32))
        uh = ah_ref[...].astype(F32) * _sigmoid(gh_ref[...].astype(F32))
        ubuf[0:HALO, :] = jnp.where(i == 0, 0.0, uh)
        ubuf[HALO:HALO + tt, :] = u
        _shifted_copies(ubuf, ush, tt)
        acc = jnp.broadcast_to(cb_ref[...], (tt, C))
        for k in range(CONV_K):
            acc = acc + w_ref[k:k + 1, :] * _tap(ubuf, ush, HALO - (CONV_K - 1) + k, tt)
        ypre_ref[...] = acc
        mu = jnp.mean(acc, axis=-1, keepdims=True)
        d = acc - mu
        rstd = lax.rsqrt(jnp.mean(d * d, axis=-1, keepdims=True) + LN_EPS)
        z = d * rstd * lg_ref[...] + lb_ref[...]
        y_ref[...] = (z * _sigmoid(z)).astype(BF16)

    vec = pl.BlockSpec((1, C), lambda i: (0, 0))
    return _pcall(
        body, name="conv_fwd", grid=(T // tt,),
        in_specs=[pl.BlockSpec((tt, C), lambda i: (i, 0)), pl.BlockSpec((tt, C), lambda i: (i, 1)),
                  pl.BlockSpec((HALO, C), lambda i: (jnp.maximum(i * hb - 1, 0), 0)),
                  pl.BlockSpec((HALO, C), lambda i: (jnp.maximum(i * hb - 1, 0), 1)),
                  pl.BlockSpec((HALO, C), lambda i: (0, 0)), vec, vec, vec],
        out_specs=[pl.BlockSpec((tt, C), lambda i: (i, 0))] * 2,
        out_shape=[jax.ShapeDtypeStruct((T, C), F32), jax.ShapeDtypeStruct((T, C), BF16)],
        scratch=[pltpu.VMEM((tt + HALO, C), F32), pltpu.VMEM((SUBLANES - 1, tt + SHIFT_ROWS, C), F32)],
    )(proj, proj, proj, proj, conv_w, conv_b, ln_g, ln_b)


def conv_bwd(proj, ypre, dycat, conv_w, ln_g, ln_b):
    T = proj.shape[0]
    C = conv_w.shape[1]
    tt = _tile(T, 256, HALO)
    hb = tt // HALO
    nt = T // tt
    last_h = T // HALO - 1

    def ln_bwd(ypre_v, dout_v, lg, lb):
        mu = jnp.mean(ypre_v, axis=-1, keepdims=True)
        d = ypre_v - mu
        rstd = lax.rsqrt(jnp.mean(d * d, axis=-1, keepdims=True) + LN_EPS)
        yh = d * rstd
        z = yh * lg + lb
        s = _sigmoid(z)
        dz = dout_v * (s * (1.0 + z * (1.0 - s)))
        dyh = dz * lg
        dy = rstd * (dyh - jnp.mean(dyh, axis=-1, keepdims=True)
                     - yh * jnp.mean(dyh * yh, axis=-1, keepdims=True))
        return dy, dz, yh

    def body(a_ref, g_ref, ah_ref, gh_ref, yp_ref, ypn_ref, do_ref, don_ref, w_ref, lg_ref, lb_ref,
             dag_ref, dw_ref, dcb_ref, dlg_ref, dlb_ref, ubuf, dybuf, ush, dysh):
        i = pl.program_id(0)
        av = a_ref[...].astype(F32)
        sg = _sigmoid(g_ref[...].astype(F32))
        uh = ah_ref[...].astype(F32) * _sigmoid(gh_ref[...].astype(F32))
        ubuf[0:HALO, :] = jnp.where(i == 0, 0.0, uh)
        ubuf[HALO:HALO + tt, :] = av * sg
        lg, lb = lg_ref[...], lb_ref[...]
        dy, dz, yh = ln_bwd(yp_ref[...], do_ref[...].astype(F32), lg, lb)
        dyn, _, _ = ln_bwd(ypn_ref[...], don_ref[...].astype(F32), lg, lb)
        dybuf[0:tt, :] = dy
        dybuf[tt:tt + HALO, :] = jnp.where(i == nt - 1, 0.0, dyn)
        _shifted_copies(ubuf, ush, tt)
        _shifted_copies(dybuf, dysh, tt)

        @pl.when(i == 0)
        def _():
            dw_ref[...] = jnp.zeros_like(dw_ref)
            dcb_ref[...] = jnp.zeros_like(dcb_ref)
            dlg_ref[...] = jnp.zeros_like(dlg_ref)
            dlb_ref[...] = jnp.zeros_like(dlb_ref)

        du = jnp.zeros((tt, C), F32)
        for k in range(CONV_K):
            du = du + w_ref[k:k + 1, :] * _tap(dybuf, dysh, CONV_K - 1 - k, tt)
            dw_ref[k:k + 1, :] += jnp.sum(dy * _tap(ubuf, ush, HALO - (CONV_K - 1) + k, tt), axis=0, keepdims=True)
        dcb_ref[...] += jnp.sum(dy, axis=0, keepdims=True)
        dlg_ref[...] += jnp.sum(dz * yh, axis=0, keepdims=True)
        dlb_ref[...] += jnp.sum(dz, axis=0, keepdims=True)

        dag_ref[:, 0:C] = (du * sg).astype(BF16)
        dag_ref[:, C:2 * C] = (du * av * sg * (1.0 - sg)).astype(BF16)

    vec = pl.BlockSpec((1, C), lambda i: (0, 0))
    prev_h = lambda col: pl.BlockSpec((HALO, C), lambda i: (jnp.maximum(i * hb - 1, 0), col))
    next_h = pl.BlockSpec((HALO, C), lambda i: (jnp.minimum((i + 1) * hb, last_h), 0))
    return _pcall(
        body, name="conv_bwd", grid=(nt,),
        in_specs=[pl.BlockSpec((tt, C), lambda i: (i, 0)), pl.BlockSpec((tt, C), lambda i: (i, 1)),
                  prev_h(0), prev_h(1),
                  pl.BlockSpec((tt, C), lambda i: (i, 0)), next_h,
                  pl.BlockSpec((tt, C), lambda i: (i, 0)), next_h,
                  pl.BlockSpec((HALO, C), lambda i: (0, 0)), vec, vec],
        out_specs=[pl.BlockSpec((tt, 2 * C), lambda i: (i, 0)), pl.BlockSpec((HALO, C), lambda i: (0, 0)),
                   vec, vec, vec],
        out_shape=[jax.ShapeDtypeStruct((T, 2 * C), BF16), jax.ShapeDtypeStruct((HALO, C), F32),
                   jax.ShapeDtypeStruct((1, C), F32), jax.ShapeDtypeStruct((1, C), F32),
                   jax.ShapeDtypeStruct((1, C), F32)],
        scratch=[pltpu.VMEM((tt + HALO, C), F32), pltpu.VMEM((tt + HALO, C), F32),
                 pltpu.VMEM((SUBLANES - 1, tt + SHIFT_ROWS, C), F32),
                 pltpu.VMEM((SUBLANES - 1, tt + SHIFT_ROWS, C), F32)],
    )(proj, proj, proj, proj, ypre, ypre, dycat, dycat, conv_w, ln_g, ln_b)


PAIR = LANES // HEAD_DIM


def _head_masks(rows):
    lane = lax.broadcasted_iota(jnp.int32, (rows, LANES), 1)
    return [jnp.logical_and(lane >= hh * HEAD_DIM, lane < (hh + 1) * HEAD_DIM) for hh in range(PAIR)]


def _causal(tq, tk):
    return lax.broadcasted_iota(jnp.int32, (tq, tk), 0) >= lax.broadcasted_iota(jnp.int32, (tq, tk), 1)


def _lane_column(block, lane_index):
    lane = lax.broadcasted_iota(jnp.int32, block.shape, 1)
    return jnp.sum(jnp.where(lane == lane_index, block, 0.0), axis=-1, keepdims=True)


def attn_fwd(proj, cum, ck4, q_col, comm=None):
    T = proj.shape[0]
    H, nkv, _, tk = ck4.shape
    tq = tk
    hd = H * HEAD_DIM
    qb, kb, vb = q_col // LANES, (q_col + hd) // LANES, (q_col + 2 * hd) // LANES
    scale = 1.0 / math.sqrt(HEAD_DIM)

    def body(q_ref, k_ref, v_ref, cum_ref, ck_ref, o_ref, lse_ref):
        hp = pl.program_id(0)
        i = pl.program_id(1)
        masks = _head_masks(tq)
        q2 = q_ref[...] * scale
        qs = [jnp.where(mk, q2, jnp.zeros_like(q2)) for mk in masks]
        cqs = [_lane_column(cum_ref[...], PAIR * hp + hh) for hh in range(PAIR)]

        def step(j, carry, diagonal):
            off = pl.multiple_of(j * tk, tk)
            kj = k_ref[pl.ds(off, tk), :]
            vj = v_ref[pl.ds(off, tk), :]
            out = []
            for hh in range(PAIR):
                m, l, acc = carry[hh]
                s = lax.dot_general(qs[hh], kj, NT, preferred_element_type=F32)
                s = s + cqs[hh] - ck_ref[hh, j]
                if diagonal:
                    s = jnp.where(_causal(tq, tk), s, NEG_INF)
                m_new = jnp.maximum(m, jnp.max(s, axis=-1, keepdims=True))
                alpha = jnp.exp(m - m_new)
                p = jnp.exp(s - m_new)
                l = alpha * l + jnp.sum(p, axis=-1, keepdims=True)
                acc = alpha * acc + jnp.dot(p.astype(BF16), vj, preferred_element_type=F32)
                out.append((m_new, l, acc))
            return tuple(out)

        init = tuple((jnp.full((tq, 1), -jnp.inf, F32), jnp.zeros((tq, 1), F32), jnp.zeros((tq, LANES), F32))
                     for _ in range(PAIR))
        carry = lax.fori_loop(0, i, functools.partial(step, diagonal=False), init)
        carry = step(i, carry, True)
        o = carry[PAIR - 1][2] / carry[PAIR - 1][1]
        for hh in range(PAIR - 1):
            o = jnp.where(masks[hh], carry[hh][2] / carry[hh][1], o)
        o_ref[...] = o
        lse = jnp.broadcast_to(carry[PAIR - 1][0] + jnp.log(carry[PAIR - 1][1]), (tq, LANES))
        for hh in range(PAIR - 1):
            lse = jnp.where(masks[hh], carry[hh][0] + jnp.log(carry[hh][1]), lse)
        lse_ref[...] = lse

    return _pcall(
        body, name="attn_fwd", grid=(H // PAIR, T // tq),
        in_specs=[pl.BlockSpec((tq, LANES), lambda hp, i: (i, qb + hp)),
                  pl.BlockSpec((T, LANES), lambda hp, i: (0, kb + hp)),
                  pl.BlockSpec((T, LANES), lambda hp, i: (0, vb + hp)),
                  pl.BlockSpec((tq, LANES), lambda hp, i: (i, 0)),
                  pl.BlockSpec((PAIR, nkv, 1, tk), lambda hp, i: (hp, 0, 0, 0))],
        out_specs=[pl.BlockSpec((tq, LANES), lambda hp, i: (i, hp)),
                   pl.BlockSpec((None, tq, LANES), lambda hp, i: (hp, i, 0))],
        out_shape=[jax.ShapeDtypeStruct((T, hd), F32), jax.ShapeDtypeStruct((H // PAIR, T, LANES), F32)],
        comm=comm,
    )(proj, proj, proj, cum, ck4)


def attn_bwd(proj, o, dycat, lse, cum, ck4, q_col, do_col, comm=None):
    T = proj.shape[0]
    H, nkv, _, tk = ck4.shape
    tq = tk
    nq = T // tq
    hd = H * HEAD_DIM
    qb, kb, vb = q_col // LANES, (q_col + hd) // LANES, (q_col + 2 * hd) // LANES
    dob = do_col // LANES
    scale = 1.0 / math.sqrt(HEAD_DIM)

    def body(q_ref, k_ref, v_ref, o_ref, do_ref, lse_ref, cum_ref, ck_ref,
             dq_ref, dk_ref, dv_ref, dcq_ref, dck_ref):
        hp = pl.program_id(0)
        j = pl.program_id(1)
        i = pl.program_id(2)

        def block(diagonal):
            masks = _head_masks(tq)
            q2, k2, v2, do2 = q_ref[...] * scale, k_ref[...], v_ref[...], do_ref[...]
            zero = jnp.zeros_like(q2)
            prod = do2.astype(F32) * o_ref[...]
            rows = pl.ds(pl.multiple_of(i * tq, tq), tq)
            dq_part = dk_part = dv_part = None
            dcq_part = jnp.zeros((tq, LANES), F32)
            for hh in range(PAIR):
                qh = jnp.where(masks[hh], q2, zero)
                kh = jnp.where(masks[hh], k2, zero)
                doh = jnp.where(masks[hh], do2, zero)
                delta = jnp.sum(jnp.where(masks[hh], prod, 0.0), axis=-1, keepdims=True)
                s = lax.dot_general(qh, k2, NT, preferred_element_type=F32)
                s = s + _lane_column(cum_ref[...], PAIR * hp + hh) - ck_ref[hh]
                if diagonal:
                    s = jnp.where(_causal(tq, tk), s, NEG_INF)
                p = jnp.exp(s - _lane_column(lse_ref[...], hh * HEAD_DIM))
                dp = lax.dot_general(doh, v2, NT, preferred_element_type=F32)
                ds = p * (dp - delta)
                dsb = ds.astype(BF16)
                dv_h = lax.dot_general(p.astype(BF16), doh, TN, preferred_element_type=F32)
                dk_h = lax.dot_general(dsb, qh, TN, preferred_element_type=F32)
                dq_h = jnp.dot(dsb, kh, preferred_element_type=F32)
                dq_part = dq_h if dq_part is None else dq_part + dq_h
                dk_part = dk_h if dk_part is None else dk_part + dk_h
                dv_part = dv_h if dv_part is None else dv_part + dv_h
                dck_h = -jnp.sum(ds, axis=0, keepdims=True)
                dcq_part = jnp.where(masks[hh], jnp.sum(ds, axis=-1, keepdims=True), dcq_part)
                if diagonal:
                    dck_ref[hh] = dck_h
                else:
                    dck_ref[hh] += dck_h
            dq_part = dq_part * scale

            @pl.when(j == 0)
            def _():
                dq_ref[rows, :] = dq_part
                dcq_ref[rows, :] = dcq_part

            @pl.when(j > 0)
            def _():
                dq_ref[rows, :] += dq_part
                dcq_ref[rows, :] += dcq_part

            if diagonal:
                dk_ref[...] = dk_part
                dv_ref[...] = dv_part
            else:
                dk_ref[...] += dk_part
                dv_ref[...] += dv_part

        @pl.when(i == j)
        def _():
            block(True)

        @pl.when(i > j)
        def _():
            block(False)

    at_q = lambda col: pl.BlockSpec((tq, LANES), lambda hp, j, i: (jnp.maximum(i, j), col + hp))
    at_k = lambda col: pl.BlockSpec((tk, LANES), lambda hp, j, i: (j, col + hp))
    lse_spec = pl.BlockSpec((None, tq, LANES), lambda hp, j, i: (hp, jnp.maximum(i, j), 0))
    cum_spec = pl.BlockSpec((tq, LANES), lambda hp, j, i: (jnp.maximum(i, j), 0))
    ck_spec = pl.BlockSpec((PAIR, None, 1, tk), lambda hp, j, i: (hp, j, 0, 0))
    return _pcall(
        body, name="attn_bwd", grid=(H // PAIR, nkv, nq),
        in_specs=[at_q(qb), at_k(kb), at_k(vb), at_q(0), at_q(dob), lse_spec, cum_spec, ck_spec],
        out_specs=[pl.BlockSpec((T, LANES), lambda hp, j, i: (0, hp)), at_k(0), at_k(0),
                   pl.BlockSpec((None, T, LANES), lambda hp, j, i: (hp, 0, 0)), ck_spec],
        out_shape=[jax.ShapeDtypeStruct((T, hd), F32)] * 3
        + [jax.ShapeDtypeStruct((H // PAIR, T, LANES), F32), jax.ShapeDtypeStruct((H, nkv, 1, tk), F32)],
        comm=comm,
    )(proj, proj, proj, o, dycat, lse, cum, ck4)


ELEMENTWISE_BLOCK_BYTES = 2 * 1024 * 1024
BF16_ROWS = 16


def cast_bf16(arrays, comm=None):
    def slab(a, steps):
        R, C = a.shape
        if R % (steps * BF16_ROWS) == 0:
            return pl.BlockSpec((R // steps, C), lambda i: (i, 0))
        if C % (steps * LANES) == 0:
            return pl.BlockSpec((R, C // steps), lambda i: (0, i))
        return None

    steps = 8 if all(slab(a, 8) is not None for a in arrays) else 4
    specs = [slab(a, steps) for a in arrays]
    n = len(arrays)

    def body(*refs):
        for src, dst in zip(refs[:n], refs[n:]):
            dst[...] = src[...].astype(BF16)

    return _pcall(body, name="cast_bf16", grid=(steps,), in_specs=specs, out_specs=specs,
                  out_shape=[jax.ShapeDtypeStruct(a.shape, BF16) for a in arrays], comm=comm)(*arrays)


def _ew_tiles(rows, cols, bytes_per_element):
    target = max(8, ELEMENTWISE_BLOCK_BYTES // max(1, cols * bytes_per_element))
    if rows <= target:
        return rows, cols
    t = (target // 16) * 16
    while t >= 16:
        if rows % t == 0:
            return t, cols
        t -= 16
    tc = _tile(cols, max(LANES, (ELEMENTWISE_BLOCK_BYTES // (rows * bytes_per_element)) // LANES * LANES))
    return rows, tc


def sum_chips(recv):
    nc, R, C = recv.shape
    tr, tc = _ew_tiles(R, C, 4)

    def body(r_ref, o_ref):
        acc = r_ref[0].astype(F32)
        for j in range(1, nc):
            acc = acc + r_ref[j].astype(F32)
        o_ref[...] = acc

    return _pcall(
        body, name="sum_chips", grid=(R // tr, C // tc),
        in_specs=[pl.BlockSpec((nc, tr, tc), lambda i, j: (0, i, j))],
        out_specs=[pl.BlockSpec((tr, tc), lambda i, j: (i, j))],
        out_shape=[jax.ShapeDtypeStruct((R, C), F32)],
    )(recv)[0]


def adamw(w, m, v, g_parts, comm=None):
    R, C = w.shape
    tr, tc = _ew_tiles(R, C, 4 * 4)
    n_g = len(g_parts)
    c1 = 1.0 - ADAM_B1
    c2 = 1.0 - ADAM_B2
    bc1 = 1.0 - ADAM_B1 ** ADAM_STEP
    bc2 = 1.0 - ADAM_B2 ** ADAM_STEP

    def body(*refs):
        w_ref, m_ref, v_ref = refs[:3]
        g_refs = refs[3:3 + n_g]
        g_out, d_out, m_out, v_out = refs[3 + n_g:]
        g = g_refs[0][...]
        for r in g_refs[1:]:
            g = g + r[...]
        m_new = ADAM_B1 * m_ref[...] + c1 * g
        v_new = ADAM_B2 * v_ref[...] + c2 * (g * g)
        m_hat = m_new / bc1
        v_hat = v_new / bc2
        g_out[...] = g
        d_out[...] = -ADAM_LR * (m_hat / (jnp.sqrt(v_hat) + ADAM_EPS) + ADAM_WD * w_ref[...])
        m_out[...] = m_new
        v_out[...] = v_new

    spec = pl.BlockSpec((tr, tc), lambda i, j: (i, j))
    return _pcall(
        body, name="adamw", grid=(R // tr, C // tc),
        in_specs=[spec] * (3 + n_g), out_specs=[spec] * 4,
        out_shape=[jax.ShapeDtypeStruct((R, C), F32)] * 4, comm=comm,
    )(w, m, v, *g_parts)


def _chip_coords():
    x, y, c = lax.axis_index("x"), lax.axis_index("y"), lax.axis_index("c")
    others = [(1 - x, y), (x, 1 - y), (1 - x, 1 - y)]
    return x, y, c, others


def _remote(src, dst, send_sem, recv_sem, device):
    return pltpu.make_async_remote_copy(src_ref=src, dst_ref=dst, send_sem=send_sem, recv_sem=recv_sem,
                                        device_id=device, device_id_type=MESH)


def gather_comm(shards):
    n = len(shards)
    pieces = [(a, jj) for a in range(n) for jj in range(3)]

    def makers(ins, outs, sems):
        send_sems, recv_sems, local_sems = sems
        x, y, c, others = _chip_coords()
        me = 2 * x + y
        sibling = (x, y, 1 - c)

        def half(ref, a, which, chip=None):
            rows, cols = ins[a].shape[0], ins[a].shape[1]
            lead = () if chip is None else (chip,)
            if rows % (2 * BF16_ROWS) == 0:
                return ref.at[(*lead, pl.ds(which * (rows // 2), rows // 2))]
            return ref.at[(*lead, slice(None), pl.ds(which * (cols // 2), cols // 2))]

        def local(a):
            return pltpu.make_async_copy(ins[a], outs[a].at[me], local_sems.at[a])

        def ici(a, jj):
            ox, oy = others[jj]
            return _remote(half(ins[a], a, c), half(outs[a], a, c, me),
                           send_sems.at[6 * a + jj], recv_sems.at[6 * a + jj], (ox, oy, c))

        def landed(a, jj):
            ox, oy = others[jj]
            got = half(outs[a], a, c, 2 * ox + oy)
            return (_remote(got, got, send_sems.at[6 * a + jj], recv_sems.at[6 * a + jj], (ox, oy, c)),
                    _remote(got, got, send_sems.at[6 * a + 3 + jj], recv_sems.at[6 * a + 3 + jj], sibling))

        def theirs(a, jj):
            ox, oy = others[jj]
            sib = half(outs[a], a, 1 - c, 2 * ox + oy)
            return _remote(sib, sib, send_sems.at[6 * a + 3 + jj], recv_sems.at[6 * a + 3 + jj], sibling)

        return local, ici, landed, theirs

    def start(ins, outs, sems):
        local, ici, _, _ = makers(ins, outs, sems)
        for a in range(n):
            local(a).start()
        for a, jj in pieces:
            ici(a, jj).start()

    def finish(ins, outs, sems):
        local, ici, landed, theirs = makers(ins, outs, sems)
        forwards = []
        for a, jj in pieces:
            got, fwd = landed(a, jj)
            got.wait_recv()
            fwd.start()
            forwards.append(fwd)
        for a, jj in pieces:
            theirs(a, jj).wait_recv()
        for a, jj in pieces:
            ici(a, jj).wait_send()
        for fwd in forwards:
            fwd.wait_send()
        for a in range(n):
            local(a).wait()

    return Comm(shards, [jax.ShapeDtypeStruct((N_CHIP,) + s.shape, s.dtype) for s in shards],
                [pltpu.SemaphoreType.DMA((6 * n,)), pltpu.SemaphoreType.DMA((6 * n,)),
                 pltpu.SemaphoreType.DMA((n,))], start, finish)


def scatter_comm(grads):
    n = len(grads)
    pieces = [(a, jj) for a in range(n) for jj in range(3)]

    def makers(ins, outs, sems):
        send_sems, recv_sems, local_sems = sems
        x, y, c, others = _chip_coords()
        me = 2 * x + y

        def local(a):
            return pltpu.make_async_copy(ins[a].at[me], outs[a].at[me], local_sems.at[a])

        def ici(a, jj):
            ox, oy = others[jj]
            return _remote(ins[a].at[2 * ox + oy], outs[a].at[me], send_sems.at[3 * a + jj],
                           recv_sems.at[3 * a + jj], (ox, oy, c))

        def landed(a, jj):
            ox, oy = others[jj]
            slot = outs[a].at[2 * ox + oy]
            return _remote(slot, slot, send_sems.at[3 * a + jj], recv_sems.at[3 * a + jj], (ox, oy, c))

        return local, ici, landed

    def start(ins, outs, sems):
        local, ici, _ = makers(ins, outs, sems)
        for a in range(n):
            local(a).start()
        for a, jj in pieces:
            ici(a, jj).start()

    def finish(ins, outs, sems):
        local, ici, landed = makers(ins, outs, sems)
        for a, jj in pieces:
            landed(a, jj).wait_recv()
        for a, jj in pieces:
            ici(a, jj).wait_send()
        for a in range(n):
            local(a).wait()

    return Comm(grads, [jax.ShapeDtypeStruct(g.shape, g.dtype) for g in grads],
                [pltpu.SemaphoreType.DMA((3 * n,)), pltpu.SemaphoreType.DMA((3 * n,)),
                 pltpu.SemaphoreType.DMA((n,))], start, finish)


def swap_comm(parts):
    n = len(parts)

    def copies(ins, outs, sems):
        send_sems, recv_sems = sems
        x, y, c, _ = _chip_coords()
        return [_remote(ins[a], outs[a], send_sems.at[a], recv_sems.at[a], (x, y, 1 - c)) for a in range(n)]

    def start(ins, outs, sems):
        for cp in copies(ins, outs, sems):
            cp.start()

    def finish(ins, outs, sems):
        for cp in copies(ins, outs, sems):
            cp.wait()

    return Comm(parts, [jax.ShapeDtypeStruct(p.shape, p.dtype) for p in parts],
                [pltpu.SemaphoreType.DMA((n,)), pltpu.SemaphoreType.DMA((n,))], start, finish)


def allreduce_small(v):
    R = v.shape[0]

    def body(v_ref, sum_ref, all_ref, send_sems, recv_sems):
        x, y, c = lax.axis_index("x"), lax.axis_index("y"), lax.axis_index("c")
        me = 4 * x + 2 * y + c
        all_ref[me] = v_ref[...]
        copies = []
        for k in range(1, N_DEV):
            px = 1 - x if k & 4 else x
            py = 1 - y if k & 2 else y
            pc = 1 - c if k & 1 else c
            cp = pltpu.make_async_remote_copy(
                src_ref=v_ref, dst_ref=all_ref.at[me], send_sem=send_sems.at[k - 1], recv_sem=recv_sems.at[k - 1],
                device_id=(px, py, pc), device_id_type=MESH)
            cp.start()
            copies.append((cp, 4 * px + 2 * py + pc))
        for k, (cp, peer) in enumerate(copies):
            pltpu.make_async_remote_copy(
                src_ref=v_ref, dst_ref=all_ref.at[peer], send_sem=send_sems.at[k], recv_sem=recv_sems.at[k],
                device_id=(x, y, c), device_id_type=MESH).wait_recv()
        for cp, _ in copies:
            cp.wait_send()
        acc = all_ref[0]
        for d in range(1, N_DEV):
            acc = acc + all_ref[d]
        sum_ref[...] = acc

    vm = pl.BlockSpec(memory_space=pltpu.VMEM)
    return pl.pallas_call(
        body, name="allreduce_small",
        in_specs=[vm], out_specs=[vm, vm],
        out_shape=[jax.ShapeDtypeStruct((R, LANES), F32), jax.ShapeDtypeStruct((N_DEV, R, LANES), F32)],
        scratch_shapes=[pltpu.SemaphoreType.DMA((N_DEV - 1,)), pltpu.SemaphoreType.DMA((N_DEV - 1,))],
    )(v)[0]


SMALL_NAMES = ("ffn1_norm", "mix_norm", "ffn2_norm", "final_norm", "conv_b", "conv_ln_g", "conv_ln_b")


def _pack_small(vecs, bias, conv_w_rows, loss_tile):
    rows = [vecs[n].reshape(-1, LANES) for n in SMALL_NAMES]
    rows.append(bias.reshape(1, LANES))
    rows.append(conv_w_rows.reshape(-1, LANES))
    rows.append(loss_tile[0:1, :])
    packed = jnp.concatenate(rows, axis=0)
    pad = (-packed.shape[0]) % 8
    return jnp.pad(packed, ((0, pad), (0, 0)))


def _unpack_small(packed, sizes, n_conv_rows):
    out, r = {}, 0
    for n in SMALL_NAMES:
        k = sizes[n] // LANES
        out[n] = packed[r:r + k].reshape(-1)
        r += k
    out["fgate_bias"] = packed[r]
    r += 1
    out["conv_w"] = packed[r:r + n_conv_rows]
    r += n_conv_rows
    out["loss"] = packed[r, 0]
    return out


def kernel(x, ffn1_norm, ffn1_w_gate, ffn1_w_up, ffn1_w_down, mix_norm, w_in, fgate_bias, conv_w, conv_b, conv_ln_g, conv_ln_b, w_out, ffn2_norm, ffn2_w_gate, ffn2_w_up, ffn2_w_down, final_norm, loss_target, m_ffn1_norm, m_ffn1_w_gate, m_ffn1_w_up, m_ffn1_w_down, m_mix_norm, m_w_in, m_fgate_bias, m_conv_w, m_conv_b, m_conv_ln_g, m_conv_ln_b, m_w_out, m_ffn2_norm, m_ffn2_w_gate, m_ffn2_w_up, m_ffn2_w_down, m_final_norm, v_ffn1_norm, v_ffn1_w_gate, v_ffn1_w_up, v_ffn1_w_down, v_mix_norm, v_w_in, v_fgate_bias, v_conv_w, v_conv_b, v_conv_ln_g, v_conv_ln_b, v_w_out, v_ffn2_norm, v_ffn2_w_gate, v_ffn2_w_up, v_ffn2_w_down, v_final_norm):
    w = dict(ffn1_norm=ffn1_norm, ffn1_w_gate=ffn1_w_gate, ffn1_w_up=ffn1_w_up, ffn1_w_down=ffn1_w_down,
             mix_norm=mix_norm, w_in=w_in, fgate_bias=fgate_bias, conv_w=conv_w, conv_b=conv_b,
             conv_ln_g=conv_ln_g, conv_ln_b=conv_ln_b, w_out=w_out, ffn2_norm=ffn2_norm,
             ffn2_w_gate=ffn2_w_gate, ffn2_w_up=ffn2_w_up, ffn2_w_down=ffn2_w_down, final_norm=final_norm)
    m = dict(ffn1_norm=m_ffn1_norm, ffn1_w_gate=m_ffn1_w_gate, ffn1_w_up=m_ffn1_w_up, ffn1_w_down=m_ffn1_w_down,
             mix_norm=m_mix_norm, w_in=m_w_in, fgate_bias=m_fgate_bias, conv_w=m_conv_w, conv_b=m_conv_b,
             conv_ln_g=m_conv_ln_g, conv_ln_b=m_conv_ln_b, w_out=m_w_out, ffn2_norm=m_ffn2_norm,
             ffn2_w_gate=m_ffn2_w_gate, ffn2_w_up=m_ffn2_w_up, ffn2_w_down=m_ffn2_w_down, final_norm=m_final_norm)
    v = dict(ffn1_norm=v_ffn1_norm, ffn1_w_gate=v_ffn1_w_gate, ffn1_w_up=v_ffn1_w_up, ffn1_w_down=v_ffn1_w_down,
             mix_norm=v_mix_norm, w_in=v_w_in, fgate_bias=v_fgate_bias, conv_w=v_conv_w, conv_b=v_conv_b,
             conv_ln_g=v_conv_ln_g, conv_ln_b=v_conv_ln_b, w_out=v_w_out, ffn2_norm=v_ffn2_norm,
             ffn2_w_gate=v_ffn2_w_gate, ffn2_w_up=v_ffn2_w_up, ffn2_w_down=v_ffn2_w_down, final_norm=v_final_norm)
    names = list(w.keys())
    big = ("ffn1_w_gate", "ffn1_w_up", "ffn1_w_down", "w_in", "w_out", "ffn2_w_gate", "ffn2_w_up", "ffn2_w_down")

    T, D = x.shape[1], x.shape[2]
    C = conv_b.shape[0]
    H = fgate_bias.shape[0]
    cs = conv_w.shape[1]
    in_cols = N_CHIP * w_in.shape[1]
    p_main = in_cols - H

    x0, tgt = x[0], loss_target[0]
    tk = _tile(T, 512, 128)
    nkv = T // tk
    row = lambda a: a.reshape(1, -1)
    grad, delta, new_m, new_v = {}, {}, {}, {}

    def update(n, parts, comm=None):
        args = (w[n], m[n], v[n])
        if n == "w_in":
            outs = [t.T for t in adamw(*[a.T for a in args], parts, comm=comm)]
        else:
            outs = adamw(*args, parts, comm=comm)
        grad[n], delta[n], new_m[n], new_v[n] = outs

    rest = [n for n in big if n != "ffn1_w_gate"]
    g0 = gather_comm([w["ffn1_w_gate"].astype(BF16), jnp.pad(conv_w, ((0, HALO - CONV_K), (0, 0)))])
    wb = dict(zip(rest, cast_bf16([w[n].T if n == "w_in" else w[n] for n in rest], comm=g0)))
    wg1, conv_w4 = g0.results
    conv_w_full = conv_w4.transpose(1, 0, 2).reshape(HALO, C)
    h1, r1 = rms_fwd(x0, row(ffn1_norm))
    g1a = gather_comm([wb["ffn1_w_up"]])
    a1 = ffn_gate(h1, wg1, comm=g1a)
    wu1 = g1a.results[0]
    g1b = gather_comm([wb["ffn1_w_down"]])
    b1, mid1 = ffn_upmul(h1, wu1, a1, comm=g1b)
    wd1 = g1b.results[0]
    g2 = gather_comm([wb["w_in"]])
    x1 = mm_residual("ffn_down_g", mid1, wd1, x0, 0.5, comm=g2)[0]
    w_t = g2.results[0].reshape(in_cols, D)

    wf_t = jnp.pad(w_t[p_main:], ((0, LANES - H), (0, 0)))
    bias_pad = jnp.pad(row(fgate_bias), ((0, 0), (0, LANES - H)))
    h2, r2 = rms_fwd(x1, row(mix_norm))
    proj = proj_main(h2, w_t, p_main)
    f, cum = fgate_fwd(h2, wf_t, bias_pad, H)
    ypre, yconv = conv_fwd(proj, conv_w_full, row(conv_b), row(conv_ln_g), row(conv_ln_b))
    ck4 = cum[:, :H].T.reshape(H, nkv, 1, tk)
    g3 = gather_comm([wb["w_out"], wb["ffn2_w_gate"], wb["ffn2_w_up"], wb["ffn2_w_down"]])
    o, lse = attn_fwd(proj, cum, ck4, 2 * C, comm=g3)
    w_out3, wg2, wu2, wd2 = g3.results
    ycat = jnp.concatenate([yconv, o.astype(BF16)], axis=1)
    x2 = mm_residual("out_proj", ycat, w_out3, x1, 1.0)[0]

    h3, r3 = rms_fwd(x2, row(ffn2_norm))
    a2, b2, mid2 = ffn_up(h3, wg2, wu2)
    x3 = mm_residual("ffn_down", mid2, wd2, x2, 0.5)[0]
    dx3, dx3b, loss_tile, d_final = final_loss(x3, tgt, row(final_norm))

    da2, db2 = ffn_bwd_mid(dx3b, wd2, a2, b2)
    dwd2 = dw_rowshard("ffn_dwd", mid2, dx3b, N_CHIP)[0]
    s1 = scatter_comm([dwd2])
    dwg2, dwu2 = dw_colshard("ffn_dwgu_s", h3, [da2, db2], N_CHIP, comm=s1)
    s2 = scatter_comm([dwg2])
    dh3 = ffn_dh(da2, db2, wg2, wu2, comm=s2)
    dx2, dx2b, d_ffn2_norm = rms_bwd(dh3, x2, r3, row(ffn2_norm), dx3, 1.0)

    dycat = mm_nt_bf16("out_proj_dy", dx2b, w_out3.reshape(-1, D))
    dw_out3 = dw_rowshard("out_proj_dw", ycat, dx2b, N_CHIP)[0]
    s3 = scatter_comm([dwu2, dw_out3])
    dq, dk, dv, dcq, dck4 = attn_bwd(proj, o, dycat, lse, cum, ck4, 2 * C, C, comm=s3)
    dcq = dcq[:, :, ::HEAD_DIM].transpose(1, 0, 2).reshape(T, H)
    dc = jnp.pad(dcq + dck4.reshape(H, T).T, ((0, 0), (0, LANES - H)))
    df, d_bias = fgate_bwd(dc, f, H)
    dag, d_conv_w, d_conv_b, d_ln_g, d_ln_b = conv_bwd(proj, ypre, dycat, conv_w_full, row(conv_ln_g),
                                                       row(conv_ln_b))
    dproj = jnp.concatenate([dag, dq.astype(BF16), dk.astype(BF16), dv.astype(BF16)], axis=1)
    early = ("ffn2_w_down", "ffn2_w_gate", "ffn2_w_up", "w_out")
    early_sums = [sum_chips(r) for r in (s1.results[0], s2.results[0], s3.results[0], s3.results[1])]
    sw1 = swap_comm(early_sums)
    dh2 = proj_dh(dproj, w_t, df, wf_t, comm=sw1)
    dw_t, dwf_t = proj_dw(dproj, df, h2, in_cols)
    dw_t = lax.dynamic_update_slice(dw_t, dwf_t[:H].astype(BF16), (p_main, 0))
    dw_in3 = dw_t.reshape(N_CHIP, in_cols // N_CHIP, D)
    dx1, dx1b, d_mix_norm = rms_bwd(dh2, x1, r2, row(mix_norm), dx2, 0.5)

    s4 = scatter_comm([dw_in3])
    da1, db1 = ffn_bwd_mid(dx1b, wd1, a1, b1, comm=s4)
    dwd1 = dw_rowshard("ffn_dwd", mid1, dx1b, N_CHIP)[0]
    s5 = scatter_comm([dwd1])
    dwg1, dwu1 = dw_colshard("ffn_dwgu_s", h1, [da1, db1], N_CHIP, comm=s5)
    s6 = scatter_comm([dwg1])
    dh1 = ffn_dh(da1, db1, wg1, wu1, comm=s6)
    grad_x, _, d_ffn1_norm = rms_bwd(dh1, x0, r1, row(ffn1_norm), dx1, 1.0)

    s7 = scatter_comm([dwu1])
    for i, (n, mine, other) in enumerate(zip(early, early_sums, sw1.results)):
        update(n, [mine, other], comm=s7 if i == 0 else None)
    late = ("w_in", "ffn1_w_down", "ffn1_w_gate", "ffn1_w_up")
    late_sums = [sum_chips(r) for r in (s4.results[0], s5.results[0], s6.results[0], s7.results[0])]
    late_theirs = _run_comm("swap_last", swap_comm(late_sums))
    for n, mine, other in zip(late, late_sums, late_theirs):
        update(n, [mine, other])

    gl = dict(ffn1_norm=d_ffn1_norm, mix_norm=d_mix_norm, ffn2_norm=d_ffn2_norm, final_norm=d_final,
              conv_b=d_conv_b, conv_ln_g=d_ln_g, conv_ln_b=d_ln_b)
    small_sizes = {n: w[n].shape[0] for n in SMALL_NAMES}
    packed = _pack_small(gl, d_bias, d_conv_w, loss_tile)
    red = _unpack_small(allreduce_small(packed), small_sizes, HALO * C // LANES)
    loss = red["loss"]
    my_chip = 2 * lax.axis_index("x") + lax.axis_index("y")
    g_conv_w = lax.dynamic_slice_in_dim(red["conv_w"].reshape(HALO, C)[:CONV_K], my_chip * cs, cs, axis=1)
    update("conv_w", [g_conv_w])
    vec_names = SMALL_NAMES + ("fgate_bias",)
    stack = lambda d: jnp.concatenate(
        [jnp.pad(d[n], (0, (-d[n].shape[0]) % LANES)).reshape(-1, LANES) for n in vec_names], axis=0)
    g_stack = jnp.concatenate([red[n].reshape(-1, LANES) for n in SMALL_NAMES] + [red["fgate_bias"][None, :]],
                              axis=0)
    outs = adamw(stack(w), stack(m), stack(v), [g_stack])
    r = 0
    for n in vec_names:
        size = w[n].shape[0]
        k = -(-size // LANES)
        for dst, src in zip((grad, delta, new_m, new_v), outs):
            dst[n] = src[r:r + k].reshape(-1)[:size]
        r += k

    return (loss, grad_x[None], *[grad[n] for n in names], *[delta[n] for n in names],
            *[new_m[n] for n in names], *[new_v[n] for n in names])
```

```python
import functools
import math

import jax
import jax.numpy as jnp
from jax import lax
from jax.experimental import pallas as pl
from jax.experimental.pallas import tpu as pltpu

F32 = jnp.float32
BF16 = jnp.bfloat16
NORM_EPS = 1e-6
LN_EPS = 1e-5
NEG_INF = -1e30
HEAD_DIM = 64
CONV_K = 31
HALO = 32
LANES = 128
N_CHIP = 4
N_DEV = 8
VMEM_LIMIT = 52 * 1024 * 1024
MESH = pl.DeviceIdType.MESH

ADAM_LR = 0.001
ADAM_B1 = 0.9
ADAM_B2 = 0.999
ADAM_EPS = 1e-08
ADAM_WD = 0.01
ADAM_STEP = 10

NN = (((1,), (0,)), ((), ()))
NT = (((1,), (1,)), ((), ()))
TN = (((0,), (0,)), ((), ()))


def _tile(n, pref, unit=128):
    if n <= pref:
        return n
    t = (pref // unit) * unit
    while t > 0:
        if n % t == 0:
            return t
        t -= unit
    raise ValueError(f"no tile for {n} under {pref}")


class Comm:
    def __init__(self, operands, out_shape, sems, start, finish):
        self.operands, self.out_shape, self.sems = list(operands), list(out_shape), list(sems)
        self.start, self.finish = start, finish
        self.results = None


def _pcall(body, *, name, grid, in_specs, out_specs, out_shape, scratch=(), comm=None):
    params = pltpu.CompilerParams(dimension_semantics=("arbitrary",) * len(grid), vmem_limit_bytes=VMEM_LIMIT)
    scratch = list(scratch)
    if comm is None:
        return pl.pallas_call(body, name=name, grid=grid, in_specs=in_specs, out_specs=out_specs,
                              out_shape=out_shape, scratch_shapes=scratch, compiler_params=params)
    n_in, n_out, n_s = len(in_specs), len(out_shape), len(scratch)
    n_ci, n_co = len(comm.operands), len(comm.out_shape)
    any_spec = pl.BlockSpec(memory_space=pl.ANY)

    def carried(*refs):
        ins, refs = refs[:n_in], refs[n_in:]
        c_ins, refs = refs[:n_ci], refs[n_ci:]
        outs, refs = refs[:n_out], refs[n_out:]
        c_outs, refs = refs[:n_co], refs[n_co:]
        scr, c_sems = refs[:n_s], refs[n_s:]
        first = pl.program_id(0) == 0
        last = pl.program_id(0) == grid[0] - 1
        for d in range(1, len(grid)):
            first = jnp.logical_and(first, pl.program_id(d) == 0)
            last = jnp.logical_and(last, pl.program_id(d) == grid[d] - 1)

        @pl.when(first)
        def _():
            comm.start(c_ins, c_outs, c_sems)

        body(*ins, *outs, *scr)

        @pl.when(last)
        def _():
            comm.finish(c_ins, c_outs, c_sems)

    call = pl.pallas_call(
        carried, name=name, grid=grid, in_specs=list(in_specs) + [any_spec] * n_ci,
        out_specs=list(out_specs) + [any_spec] * n_co, out_shape=list(out_shape) + comm.out_shape,
        scratch_shapes=scratch + comm.sems, compiler_params=params)

    def run(*operands):
        res = call(*operands, *comm.operands)
        comm.results = list(res[n_out:])
        return list(res[:n_out])

    return run


def _run_comm(name, comm):
    n_ci, n_co = len(comm.operands), len(comm.out_shape)
    any_spec = pl.BlockSpec(memory_space=pl.ANY)

    def body(*refs):
        c_ins, c_outs, c_sems = refs[:n_ci], refs[n_ci:n_ci + n_co], refs[n_ci + n_co:]
        comm.start(c_ins, c_outs, c_sems)
        comm.finish(c_ins, c_outs, c_sems)

    return pl.pallas_call(body, name=name, in_specs=[any_spec] * n_ci, out_specs=[any_spec] * n_co,
                          out_shape=comm.out_shape, scratch_shapes=comm.sems)(*comm.operands)


def _sigmoid(x):
    return 1.0 / (1.0 + jnp.exp(-x))


def _mm(name, *, grid, pairs, once_pairs=(), extra=(), out_shape, out_specs, acc_shapes, nk, kaxis, epilogue,
        comm=None):
    all_pairs = list(pairs) + list(once_pairs)
    n_p, n_o = len(pairs), len(once_pairs)
    n_e, n_out, n_acc = len(extra), len(out_shape), len(acc_shapes)

    def body(*refs):
        ab = refs[: 2 * (n_p + n_o)]
        ex = refs[2 * (n_p + n_o): 2 * (n_p + n_o) + n_e]
        outs = refs[2 * (n_p + n_o) + n_e: 2 * (n_p + n_o) + n_e + n_out]
        accs = refs[2 * (n_p + n_o) + n_e + n_out:]

        def dots(idx_range):
            vals = [None] * n_acc
            for p in idx_range:
                d = lax.dot_general(ab[2 * p][...], ab[2 * p + 1][...], all_pairs[p][4],
                                    preferred_element_type=F32)
                ai = all_pairs[p][5]
                vals[ai] = d if vals[ai] is None else vals[ai] + d
            return vals

        if nk == 1:
            vals = dots(range(n_p + n_o))
            epilogue(vals, ex, outs)
            return

        k = pl.program_id(kaxis)

        @pl.when(k == 0)
        def _():
            vals = dots(range(n_p + n_o))
            for ai in range(n_acc):
                accs[ai][...] = vals[ai]

        @pl.when(k > 0)
        def _():
            vals = dots(range(n_p))
            for ai in range(n_acc):
                if vals[ai] is not None:
                    accs[ai][...] += vals[ai]

        @pl.when(k == nk - 1)
        def _():
            epilogue([a[...] for a in accs], ex, outs)

    operands, in_specs = [], []
    for p in all_pairs:
        operands += [p[0], p[2]]
        in_specs += [p[1], p[3]]
    for arr, spec in extra:
        operands.append(arr)
        in_specs.append(spec)
    scratch = [pltpu.VMEM(s, F32) for s in acc_shapes] if nk > 1 else []
    return _pcall(body, name=name, grid=grid, in_specs=in_specs, out_specs=out_specs, out_shape=out_shape,
                  scratch=scratch, comm=comm)(*operands)


def rms_fwd(x, g):
    T, D = x.shape
    tt = _tile(T, 512, 8)

    def body(x_ref, g_ref, h_ref, r_ref):
        xv = x_ref[...]
        r = lax.rsqrt(jnp.mean(xv * xv, axis=-1, keepdims=True) + NORM_EPS)
        h_ref[...] = (xv * r * g_ref[...]).astype(BF16)
        r_ref[...] = r

    return _pcall(
        body, name="rms_fwd", grid=(T // tt,),
        in_specs=[pl.BlockSpec((tt, D), lambda i: (i, 0)), pl.BlockSpec((1, D), lambda i: (0, 0))],
        out_specs=[pl.BlockSpec((tt, D), lambda i: (i, 0)), pl.BlockSpec((tt, 1), lambda i: (i, 0))],
        out_shape=[jax.ShapeDtypeStruct((T, D), BF16), jax.ShapeDtypeStruct((T, 1), F32)],
    )(x, g)


def rms_bwd(dh, x, r, g, dres, out_scale):
    T, D = x.shape
    tt = _tile(T, 256, 8)

    def body(dh_ref, x_ref, r_ref, g_ref, dres_ref, dx_ref, dxb_ref, dg_ref):
        i = pl.program_id(0)
        xh = x_ref[...] * r_ref[...]
        dhv = dh_ref[...]
        dxh = dhv * g_ref[...]
        dx = dres_ref[...] + r_ref[...] * (dxh - xh * jnp.mean(dxh * xh, axis=-1, keepdims=True))
        dx_ref[...] = dx
        dxb_ref[...] = (out_scale * dx).astype(BF16)
        part = jnp.sum(dhv * xh, axis=0, keepdims=True)

        @pl.when(i == 0)
        def _():
            dg_ref[...] = part

        @pl.when(i > 0)
        def _():
            dg_ref[...] += part

    row = pl.BlockSpec((tt, D), lambda i: (i, 0))
    return _pcall(
        body, name="rms_bwd", grid=(T // tt,),
        in_specs=[row, row, pl.BlockSpec((tt, 1), lambda i: (i, 0)), pl.BlockSpec((1, D), lambda i: (0, 0)), row],
        out_specs=[row, row, pl.BlockSpec((1, D), lambda i: (0, 0))],
        out_shape=[jax.ShapeDtypeStruct((T, D), F32), jax.ShapeDtypeStruct((T, D), BF16),
                   jax.ShapeDtypeStruct((1, D), F32)],
    )(dh, x, r, g, dres)


def final_loss(x, tgt, g):
    T, D = x.shape
    tt = _tile(T, 256, 8)

    def body(x_ref, t_ref, g_ref, dx_ref, dxb_ref, loss_ref, dg_ref):
        i = pl.program_id(0)
        xv = x_ref[...]
        r = lax.rsqrt(jnp.mean(xv * xv, axis=-1, keepdims=True) + NORM_EPS)
        xh = xv * r
        err = xh * g_ref[...] - t_ref[...]
        part_loss = 0.5 * jnp.sum(jnp.mean(err * err, axis=-1, keepdims=True), axis=0, keepdims=True)
        dy = err * (1.0 / D)
        dxh = dy * g_ref[...]
        dx = r * (dxh - xh * jnp.mean(dxh * xh, axis=-1, keepdims=True))
        dx_ref[...] = dx
        dxb_ref[...] = (0.5 * dx).astype(BF16)
        part_g = jnp.sum(dy * xh, axis=0, keepdims=True)
        part_l = jnp.broadcast_to(part_loss, (8, LANES))

        @pl.when(i == 0)
        def _():
            dg_ref[...] = part_g
            loss_ref[...] = part_l

        @pl.when(i > 0)
        def _():
            dg_ref[...] += part_g
            loss_ref[...] += part_l

    row = pl.BlockSpec((tt, D), lambda i: (i, 0))
    return _pcall(
        body, name="final_loss", grid=(T // tt,),
        in_specs=[row, row, pl.BlockSpec((1, D), lambda i: (0, 0))],
        out_specs=[row, row, pl.BlockSpec((8, LANES), lambda i: (0, 0)), pl.BlockSpec((1, D), lambda i: (0, 0))],
        out_shape=[jax.ShapeDtypeStruct((T, D), F32), jax.ShapeDtypeStruct((T, D), BF16),
                   jax.ShapeDtypeStruct((8, LANES), F32), jax.ShapeDtypeStruct((1, D), F32)],
    )(x, tgt, g)


def ffn_gate(h, wg3, comm=None):
    T, D = h.shape
    nc, _, fs = wg3.shape
    tm = _tile(T, 512, 8)

    def epilogue(vals, ex, outs):
        outs[0][...] = vals[0].astype(BF16)

    return _mm("ffn_gate", grid=(nc, T // tm),
               pairs=[(h, pl.BlockSpec((tm, D), lambda j, i: (i, 0)),
                       wg3, pl.BlockSpec((None, D, fs), lambda j, i: (j, 0, 0)), NN, 0)],
               out_shape=[jax.ShapeDtypeStruct((T, nc * fs), BF16)],
               out_specs=[pl.BlockSpec((tm, fs), lambda j, i: (i, j))],
               acc_shapes=[(tm, fs)], nk=1, kaxis=None, epilogue=epilogue, comm=comm)[0]


def ffn_upmul(h, wu3, a, comm=None):
    T, D = h.shape
    nc, _, fs = wu3.shape
    tm = _tile(T, 512, 8)

    def epilogue(vals, ex, outs):
        b = vals[0]
        av = ex[0][...].astype(F32)
        outs[0][...] = b.astype(BF16)
        outs[1][...] = (av * _sigmoid(av) * b).astype(BF16)

    t_spec = pl.BlockSpec((tm, fs), lambda j, i: (i, j))
    o_shape = jax.ShapeDtypeStruct((T, nc * fs), BF16)
    return _mm("ffn_upmul", grid=(nc, T // tm),
               pairs=[(h, pl.BlockSpec((tm, D), lambda j, i: (i, 0)),
                       wu3, pl.BlockSpec((None, D, fs), lambda j, i: (j, 0, 0)), NN, 0)],
               extra=[(a, t_spec)], out_shape=[o_shape] * 2, out_specs=[t_spec] * 2,
               acc_shapes=[(tm, fs)], nk=1, kaxis=None, epilogue=epilogue, comm=comm)


def ffn_up(h, wg3, wu3, comm=None):
    T, D = h.shape
    nc, _, fs = wg3.shape
    tm = _tile(T, 512, 8)

    def epilogue(vals, ex, outs):
        a, b = vals
        outs[0][...] = a.astype(BF16)
        outs[1][...] = b.astype(BF16)
        outs[2][...] = (a * _sigmoid(a) * b).astype(BF16)

    h_spec = pl.BlockSpec((tm, D), lambda j, i: (i, 0))
    w_spec = pl.BlockSpec((None, D, fs), lambda j, i: (j, 0, 0))
    o_spec = pl.BlockSpec((tm, fs), lambda j, i: (i, j))
    o_shape = jax.ShapeDtypeStruct((T, nc * fs), BF16)
    return _mm("ffn_up", grid=(nc, T // tm),
               pairs=[(h, h_spec, wg3, w_spec, NN, 0), (h, h_spec, wu3, w_spec, NN, 1)],
               out_shape=[o_shape] * 3, out_specs=[o_spec] * 3, acc_shapes=[(tm, fs)] * 2, nk=1, kaxis=None,
               epilogue=epilogue, comm=comm)


def mm_residual(name, a, b3, res, scale, comm=None):
    T = a.shape[0]
    nk, tk, N = b3.shape
    tm, tn = _tile(T, 512, 8), _tile(N, 2048)

    def epilogue(vals, ex, outs):
        outs[0][...] = ex[0][...] + scale * vals[0]

    return _mm(name, grid=(T // tm, N // tn, nk),
               pairs=[(a, pl.BlockSpec((tm, tk), lambda i, n, k: (i, k)),
                       b3, pl.BlockSpec((None, tk, tn), lambda i, n, k: (k, 0, n)), NN, 0)],
               extra=[(res, pl.BlockSpec((tm, tn), lambda i, n, k: (i, n)))],
               out_shape=[jax.ShapeDtypeStruct((T, N), F32)],
               out_specs=[pl.BlockSpec((tm, tn), lambda i, n, k: (i, n))],
               acc_shapes=[(tm, tn)], nk=nk, kaxis=2, epilogue=epilogue, comm=comm)


def ffn_bwd_mid(dout, wd3, a, b, comm=None):
    T, D = dout.shape
    nc, fs, _ = wd3.shape
    tm = _tile(T, 512, 8)

    def epilogue(vals, ex, outs):
        dm = vals[0]
        av = ex[0][...].astype(F32)
        bv = ex[1][...].astype(F32)
        s = _sigmoid(av)
        outs[0][...] = (dm * bv * (s * (1.0 + av * (1.0 - s)))).astype(BF16)
        outs[1][...] = (dm * (av * s)).astype(BF16)

    t_spec = pl.BlockSpec((tm, fs), lambda j, i: (i, j))
    o_shape = jax.ShapeDtypeStruct((T, nc * fs), BF16)
    return _mm("ffn_bwd_mid", grid=(nc, T // tm),
               pairs=[(dout, pl.BlockSpec((tm, D), lambda j, i: (i, 0)),
                       wd3, pl.BlockSpec((None, fs, D), lambda j, i: (j, 0, 0)), NT, 0)],
               extra=[(a, t_spec), (b, t_spec)],
               out_shape=[o_shape] * 2, out_specs=[t_spec] * 2, acc_shapes=[(tm, fs)], nk=1, kaxis=None,
               epilogue=epilogue, comm=comm)


def dw_rowshard(name, a, b, nc, comm=None):
    T, M = a.shape
    N = b.shape[1]
    ms = M // nc
    tn, tk = _tile(N, 1024), _tile(T, 1024, 16)

    def epilogue(vals, ex, outs):
        outs[0][...] = vals[0].astype(BF16)

    return _mm(name, grid=(nc, N // tn, T // tk),
               pairs=[(a, pl.BlockSpec((tk, ms), lambda j, n, k: (k, j)),
                       b, pl.BlockSpec((tk, tn), lambda j, n, k: (k, n)), TN, 0)],
               out_shape=[jax.ShapeDtypeStruct((nc, ms, N), BF16)],
               out_specs=[pl.BlockSpec((None, ms, tn), lambda j, n, k: (j, 0, n))],
               acc_shapes=[(ms, tn)], nk=T // tk, kaxis=2, epilogue=epilogue, comm=comm)


def dw_colshard(name, a, bs, nc, comm=None):
    T, M = a.shape
    ns = bs[0].shape[1] // nc
    tm, tk = _tile(M, 512), _tile(T, 1024, 16)

    def epilogue(vals, ex, outs):
        for v, o in zip(vals, outs):
            o[...] = v.astype(BF16)

    a_spec = pl.BlockSpec((tk, tm), lambda j, m, k: (k, m))
    b_spec = pl.BlockSpec((tk, ns), lambda j, m, k: (k, j))
    return _mm(name, grid=(nc, M // tm, T // tk),
               pairs=[(a, a_spec, b, b_spec, TN, p) for p, b in enumerate(bs)],
               out_shape=[jax.ShapeDtypeStruct((nc, M, ns), BF16)] * len(bs),
               out_specs=[pl.BlockSpec((None, tm, ns), lambda j, m, k: (j, m, 0))] * len(bs),
               acc_shapes=[(tm, ns)] * len(bs), nk=T // tk, kaxis=2, epilogue=epilogue, comm=comm)


def ffn_dh(da, db, wg3, wu3, comm=None):
    T = da.shape[0]
    nc, D, fs = wg3.shape
    tm, tn = _tile(T, 512, 8), _tile(D, 1024)

    def epilogue(vals, ex, outs):
        outs[0][...] = vals[0]

    a_spec = pl.BlockSpec((tm, fs), lambda i, n, k: (i, k))
    w_spec = pl.BlockSpec((None, tn, fs), lambda i, n, k: (k, n, 0))
    return _mm("ffn_dh", grid=(T // tm, D // tn, nc),
               pairs=[(da, a_spec, wg3, w_spec, NT, 0), (db, a_spec, wu3, w_spec, NT, 0)],
               out_shape=[jax.ShapeDtypeStruct((T, D), F32)],
               out_specs=[pl.BlockSpec((tm, tn), lambda i, n, k: (i, n))],
               acc_shapes=[(tm, tn)], nk=nc, kaxis=2, epilogue=epilogue, comm=comm)[0]


def proj_main(h, w_t, P):
    T, D = h.shape
    tm, tn = _tile(T, 512, 8), _tile(P, 1024)

    def epilogue(vals, ex, outs):
        outs[0][...] = vals[0].astype(BF16)

    return _mm("proj_main", grid=(P // tn, T // tm),
               pairs=[(h, pl.BlockSpec((tm, D), lambda j, i: (i, 0)),
                       w_t, pl.BlockSpec((tn, D), lambda j, i: (j, 0)), NT, 0)],
               out_shape=[jax.ShapeDtypeStruct((T, P), BF16)],
               out_specs=[pl.BlockSpec((tm, tn), lambda j, i: (i, j))],
               acc_shapes=[(tm, tn)], nk=1, kaxis=None, epilogue=epilogue)[0]


def mm_nt_bf16(name, a, w):
    T, K = a.shape
    M = w.shape[0]
    tm, tn = _tile(T, 512, 8), _tile(M, 1024)

    def epilogue(vals, ex, outs):
        outs[0][...] = vals[0].astype(BF16)

    return _mm(name, grid=(T // tm, M // tn),
               pairs=[(a, pl.BlockSpec((tm, K), lambda i, n: (i, 0)),
                       w, pl.BlockSpec((tn, K), lambda i, n: (n, 0)), NT, 0)],
               out_shape=[jax.ShapeDtypeStruct((T, M), BF16)],
               out_specs=[pl.BlockSpec((tm, tn), lambda i, n: (i, n))],
               acc_shapes=[(tm, tn)], nk=1, kaxis=None, epilogue=epilogue)[0]


def proj_dh(dproj, w_t, df, wf_t, comm=None):
    T, P = dproj.shape
    D = w_t.shape[1]
    tm, tn, tk = _tile(T, 512, 8), _tile(D, 1024), _tile(P, 1280)

    def epilogue(vals, ex, outs):
        outs[0][...] = vals[0]

    return _mm("proj_dh", grid=(T // tm, D // tn, P // tk),
               pairs=[(dproj, pl.BlockSpec((tm, tk), lambda i, n, k: (i, k)),
                       w_t, pl.BlockSpec((tk, tn), lambda i, n, k: (k, n)), NN, 0)],
               once_pairs=[(df, pl.BlockSpec((tm, LANES), lambda i, n, k: (i, 0)),
                            wf_t, pl.BlockSpec((LANES, tn), lambda i, n, k: (0, n)), NN, 0)],
               out_shape=[jax.ShapeDtypeStruct((T, D), F32)],
               out_specs=[pl.BlockSpec((tm, tn), lambda i, n, k: (i, n))],
               acc_shapes=[(tm, tn)], nk=P // tk, kaxis=2, epilogue=epilogue, comm=comm)[0]


def proj_dw(dproj, df, h, rows):
    T, P = dproj.shape
    D = h.shape[1]
    tm, tn, tk = _tile(P, 1280), _tile(D, 1024), _tile(T, 1024, 16)

    def to_bf16(vals, ex, outs):
        outs[0][...] = vals[0].astype(BF16)

    def to_f32(vals, ex, outs):
        outs[0][...] = vals[0]

    main = _mm("proj_dw_main", grid=(P // tm, D // tn, T // tk),
               pairs=[(dproj, pl.BlockSpec((tk, tm), lambda m, n, k: (k, m)),
                       h, pl.BlockSpec((tk, tn), lambda m, n, k: (k, n)), TN, 0)],
               out_shape=[jax.ShapeDtypeStruct((rows, D), BF16)],
               out_specs=[pl.BlockSpec((tm, tn), lambda m, n, k: (m, n))],
               acc_shapes=[(tm, tn)], nk=T // tk, kaxis=2, epilogue=to_bf16)[0]
    gate = _mm("proj_dw_f", grid=(1, D // tn, T // tk),
               pairs=[(df, pl.BlockSpec((tk, LANES), lambda m, n, k: (k, 0)),
                       h, pl.BlockSpec((tk, tn), lambda m, n, k: (k, n)), TN, 0)],
               out_shape=[jax.ShapeDtypeStruct((LANES, D), F32)],
               out_specs=[pl.BlockSpec((LANES, tn), lambda m, n, k: (0, n))],
               acc_shapes=[(LANES, tn)], nk=T // tk, kaxis=2, epilogue=to_f32)[0]
    return main, gate


def fgate_fwd(h, wf_t, bias, n_heads):
    T, D = h.shape
    tt = _tile(T, 512, 8)

    def body(h_ref, w_ref, b_ref, f_ref, c_ref, carry):
        i = pl.program_id(0)

        @pl.when(i == 0)
        def _():
            carry[...] = jnp.zeros_like(carry)

        f = lax.dot_general(h_ref[...], w_ref[...], NT, preferred_element_type=F32) + b_ref[...]
        logf = jnp.minimum(f, 0.0) - jnp.log(1.0 + jnp.exp(-jnp.abs(f)))
        tri = (lax.broadcasted_iota(jnp.int32, (tt, tt), 0) >= lax.broadcasted_iota(jnp.int32, (tt, tt), 1))
        cs = jnp.dot(tri.astype(F32), logf, preferred_element_type=F32, precision=lax.Precision.HIGHEST)
        c = cs + carry[...]
        f_ref[...] = f
        c_ref[...] = c
        carry[...] = c[tt - 1:tt, :]

    row = pl.BlockSpec((tt, LANES), lambda i: (i, 0))
    return _pcall(
        body, name="fgate_fwd", grid=(T // tt,),
        in_specs=[pl.BlockSpec((tt, D), lambda i: (i, 0)), pl.BlockSpec((LANES, D), lambda i: (0, 0)),
                  pl.BlockSpec((1, LANES), lambda i: (0, 0))],
        out_specs=[row, row],
        out_shape=[jax.ShapeDtypeStruct((T, LANES), F32)] * 2,
        scratch=[pltpu.VMEM((1, LANES), F32)],
    )(h, wf_t, bias)


def fgate_bwd(dc, f, n_heads):
    T = dc.shape[0]
    tt = _tile(T, 512, 8)
    nt = T // tt

    def body(dc_ref, f_ref, df_ref, db_ref, carry):
        i = pl.program_id(0)

        @pl.when(i == 0)
        def _():
            carry[...] = jnp.zeros_like(carry)

        tri = (lax.broadcasted_iota(jnp.int32, (tt, tt), 1) >= lax.broadcasted_iota(jnp.int32, (tt, tt), 0))
        rs = jnp.dot(tri.astype(F32), dc_ref[...], preferred_element_type=F32,
                     precision=lax.Precision.HIGHEST) + carry[...]
        carry[...] = rs[0:1, :]
        lane = lax.broadcasted_iota(jnp.int32, (tt, LANES), 1)
        df = jnp.where(lane < n_heads, rs * _sigmoid(-f_ref[...]), 0.0)
        df_ref[...] = df.astype(BF16)
        part = jnp.sum(df, axis=0, keepdims=True)

        @pl.when(i == 0)
        def _():
            db_ref[...] = part

        @pl.when(i > 0)
        def _():
            db_ref[...] += part

    rev = pl.BlockSpec((tt, LANES), lambda i: (nt - 1 - i, 0))
    return _pcall(
        body, name="fgate_bwd", grid=(nt,),
        in_specs=[rev, rev],
        out_specs=[rev, pl.BlockSpec((1, LANES), lambda i: (0, 0))],
        out_shape=[jax.ShapeDtypeStruct((T, LANES), BF16), jax.ShapeDtypeStruct((1, LANES), F32)],
        scratch=[pltpu.VMEM((1, LANES), F32)],
    )(dc, f)


SUBLANES = 8
SHIFT_ROWS = HALO - SUBLANES


def _shifted_copies(buf, sh, tt):
    for r in range(1, SUBLANES):
        sh[r - 1, 0:tt + SHIFT_ROWS, :] = buf[pl.ds(r, tt + SHIFT_ROWS), :]


def _tap(buf, sh, offset, tt):
    q, r = divmod(offset, SUBLANES)
    if r == 0:
        return buf[pl.ds(SUBLANES * q, tt), :]
    return sh[r - 1, pl.ds(SUBLANES * q, tt), :]


def conv_fwd(proj, conv_w, conv_b, ln_g, ln_b):
    T = proj.shape[0]
    C = conv_w.shape[1]
    tt = _tile(T, 256, HALO)
    hb = tt // HALO

    def body(a_ref, g_ref, ah_ref, gh_ref, w_ref, cb_ref, lg_ref, lb_ref, ypre_ref, y_ref, ubuf, ush):
        i = pl.program_id(0)
        u = a_ref[...].astype(F32) * _sigmoid(g_ref[...].astype(F32))
        uh = ah_ref[...].astype(F32) * _sigmoid(gh_ref[...].astype(F32))
        ubuf[0:HALO, :] = jnp.where(i == 0, 0.0, uh)
        ubuf[HALO:HALO + tt, :] = u
        _shifted_copies(ubuf, ush, tt)
        acc = jnp.broadcast_to(cb_ref[...], (tt, C))
        for k in range(CONV_K):
            acc = acc + w_ref[k:k + 1, :] * _tap(ubuf, ush, HALO - (CONV_K - 1) + k, tt)
        ypre_ref[...] = acc
        mu = jnp.mean(acc, axis=-1, keepdims=True)
        d = acc - mu
        rstd = lax.rsqrt(jnp.mean(d * d, axis=-1, keepdims=True) + LN_EPS)
        z = d * rstd * lg_ref[...] + lb_ref[...]
        y_ref[...] = (z * _sigmoid(z)).astype(BF16)

    vec = pl.BlockSpec((1, C), lambda i: (0, 0))
    return _pcall(
        body, name="conv_fwd", grid=(T // tt,),
        in_specs=[pl.BlockSpec((tt, C), lambda i: (i, 0)), pl.BlockSpec((tt, C), lambda i: (i, 1)),
                  pl.BlockSpec((HALO, C), lambda i: (jnp.maximum(i * hb - 1, 0), 0)),
                  pl.BlockSpec((HALO, C), lambda i: (jnp.maximum(i * hb - 1, 0), 1)),
                  pl.BlockSpec((HALO, C), lambda i: (0, 0)), vec, vec, vec],
        out_specs=[pl.BlockSpec((tt, C), lambda i: (i, 0))] * 2,
        out_shape=[jax.ShapeDtypeStruct((T, C), F32), jax.ShapeDtypeStruct((T, C), BF16)],
        scratch=[pltpu.VMEM((tt + HALO, C), F32), pltpu.VMEM((SUBLANES - 1, tt + SHIFT_ROWS, C), F32)],
    )(proj, proj, proj, proj, conv_w, conv_b, ln_g, ln_b)


def conv_bwd(proj, ypre, dycat, conv_w, ln_g, ln_b):
    T = proj.shape[0]
    C = conv_w.shape[1]
    tt = _tile(T, 256, HALO)
    hb = tt // HALO
    nt = T // tt
    last_h = T // HALO - 1

    def ln_bwd(ypre_v, dout_v, lg, lb):
        mu = jnp.mean(ypre_v, axis=-1, keepdims=True)
        d = ypre_v - mu
        rstd = lax.rsqrt(jnp.mean(d * d, axis=-1, keepdims=True) + LN_EPS)
        yh = d * rstd
        z = yh * lg + lb
        s = _sigmoid(z)
        dz = dout_v * (s * (1.0 + z * (1.0 - s)))
        dyh = dz * lg
        dy = rstd * (dyh - jnp.mean(dyh, axis=-1, keepdims=True)
                     - yh * jnp.mean(dyh * yh, axis=-1, keepdims=True))
        return dy, dz, yh

    def body(a_ref, g_ref, ah_ref, gh_ref, yp_ref, ypn_ref, do_ref, don_ref, w_ref, lg_ref, lb_ref,
             dag_ref, dw_ref, dcb_ref, dlg_ref, dlb_ref, ubuf, dybuf, ush, dysh):
        i = pl.program_id(0)
        av = a_ref[...].astype(F32)
        sg = _sigmoid(g_ref[...].astype(F32))
        uh = ah_ref[...].astype(F32) * _sigmoid(gh_ref[...].astype(F32))
        ubuf[0:HALO, :] = jnp.where(i == 0, 0.0, uh)
        ubuf[HALO:HALO + tt, :] = av * sg
        lg, lb = lg_ref[...], lb_ref[...]
        dy, dz, yh = ln_bwd(yp_ref[...], do_ref[...].astype(F32), lg, lb)
        dyn, _, _ = ln_bwd(ypn_ref[...], don_ref[...].astype(F32), lg, lb)
        dybuf[0:tt, :] = dy
        dybuf[tt:tt + HALO, :] = jnp.where(i == nt - 1, 0.0, dyn)
        _shifted_copies(ubuf, ush, tt)
        _shifted_copies(dybuf, dysh, tt)

        @pl.when(i == 0)
        def _():
            dw_ref[...] = jnp.zeros_like(dw_ref)
            dcb_ref[...] = jnp.zeros_like(dcb_ref)
            dlg_ref[...] = jnp.zeros_like(dlg_ref)
            dlb_ref[...] = jnp.zeros_like(dlb_ref)

        du = jnp.zeros((tt, C), F32)
        for k in range(CONV_K):
            du = du + w_ref[k:k + 1, :] * _tap(dybuf, dysh, CONV_K - 1 - k, tt)
            dw_ref[k:k + 1, :] += jnp.sum(dy * _tap(ubuf, ush, HALO - (CONV_K - 1) + k, tt), axis=0, keepdims=True)
        dcb_ref[...] += jnp.sum(dy, axis=0, keepdims=True)
        dlg_ref[...] += jnp.sum(dz * yh, axis=0, keepdims=True)
        dlb_ref[...] += jnp.sum(dz, axis=0, keepdims=True)

        dag_ref[:, 0:C] = (du * sg).astype(BF16)
        dag_ref[:, C:2 * C] = (du * av * sg * (1.0 - sg)).astype(BF16)

    vec = pl.BlockSpec((1, C), lambda i: (0, 0))
    prev_h = lambda col: pl.BlockSpec((HALO, C), lambda i: (jnp.maximum(i * hb - 1, 0), col))
    next_h = pl.BlockSpec((HALO, C), lambda i: (jnp.minimum((i + 1) * hb, last_h), 0))
    return _pcall(
        body, name="conv_bwd", grid=(nt,),
        in_specs=[pl.BlockSpec((tt, C), lambda i: (i, 0)), pl.BlockSpec((tt, C), lambda i: (i, 1)),
                  prev_h(0), prev_h(1),
                  pl.BlockSpec((tt, C), lambda i: (i, 0)), next_h,
                  pl.BlockSpec((tt, C), lambda i: (i, 0)), next_h,
                  pl.BlockSpec((HALO, C), lambda i: (0, 0)), vec, vec],
        out_specs=[pl.BlockSpec((tt, 2 * C), lambda i: (i, 0)), pl.BlockSpec((HALO, C), lambda i: (0, 0)),
                   vec, vec, vec],
        out_shape=[jax.ShapeDtypeStruct((T, 2 * C), BF16), jax.ShapeDtypeStruct((HALO, C), F32),
                   jax.ShapeDtypeStruct((1, C), F32), jax.ShapeDtypeStruct((1, C), F32),
                   jax.ShapeDtypeStruct((1, C), F32)],
        scratch=[pltpu.VMEM((tt + HALO, C), F32), pltpu.VMEM((tt + HALO, C), F32),
                 pltpu.VMEM((SUBLANES - 1, tt + SHIFT_ROWS, C), F32),
                 pltpu.VMEM((SUBLANES - 1, tt + SHIFT_ROWS, C), F32)],
    )(proj, proj, proj, proj, ypre, ypre, dycat, dycat, conv_w, ln_g, ln_b)


PAIR = LANES // HEAD_DIM


def _head_masks(rows):
    lane = lax.broadcasted_iota(jnp.int32, (rows, LANES), 1)
    return [jnp.logical_and(lane >= hh * HEAD_DIM, lane < (hh + 1) * HEAD_DIM) for hh in range(PAIR)]


def _causal(tq, tk):
    return lax.broadcasted_iota(jnp.int32, (tq, tk), 0) >= lax.broadcasted_iota(jnp.int32, (tq, tk), 1)


def _lane_column(block, lane_index):
    lane = lax.broadcasted_iota(jnp.int32, block.shape, 1)
    return jnp.sum(jnp.where(lane == lane_index, block, 0.0), axis=-1, keepdims=True)


def attn_fwd(proj, cum, ck4, q_col, comm=None):
    T = proj.shape[0]
    H, nkv, _, tk = ck4.shape
    tq = tk
    hd = H * HEAD_DIM
    qb, kb, vb = q_col // LANES, (q_col + hd) // LANES, (q_col + 2 * hd) // LANES
    scale = 1.0 / math.sqrt(HEAD_DIM)

    def body(q_ref, k_ref, v_ref, cum_ref, ck_ref, o_ref, lse_ref):
        hp = pl.program_id(0)
        i = pl.program_id(1)
        masks = _head_masks(tq)
        q2 = q_ref[...] * scale
        qs = [jnp.where(mk, q2, jnp.zeros_like(q2)) for mk in masks]
        cqs = [_lane_column(cum_ref[...], PAIR * hp + hh) for hh in range(PAIR)]

        def step(j, carry, diagonal):
            off = pl.multiple_of(j * tk, tk)
            kj = k_ref[pl.ds(off, tk), :]
            vj = v_ref[pl.ds(off, tk), :]
            out = []
            for hh in range(PAIR):
                m, l, acc = carry[hh]
                s = lax.dot_general(qs[hh], kj, NT, preferred_element_type=F32)
                s = s + cqs[hh] - ck_ref[hh, j]
                if diagonal:
                    s = jnp.where(_causal(tq, tk), s, NEG_INF)
                m_new = jnp.maximum(m, jnp.max(s, axis=-1, keepdims=True))
                alpha = jnp.exp(m - m_new)
                p = jnp.exp(s - m_new)
                l = alpha * l + jnp.sum(p, axis=-1, keepdims=True)
                acc = alpha * acc + jnp.dot(p.astype(BF16), vj, preferred_element_type=F32)
                out.append((m_new, l, acc))
            return tuple(out)

        init = tuple((jnp.full((tq, 1), -jnp.inf, F32), jnp.zeros((tq, 1), F32), jnp.zeros((tq, LANES), F32))
                     for _ in range(PAIR))
        carry = lax.fori_loop(0, i, functools.partial(step, diagonal=False), init)
        carry = step(i, carry, True)
        o = carry[PAIR - 1][2] / carry[PAIR - 1][1]
        for hh in range(PAIR - 1):
            o = jnp.where(masks[hh], carry[hh][2] / carry[hh][1], o)
        o_ref[...] = o
        lse = jnp.broadcast_to(carry[PAIR - 1][0] + jnp.log(carry[PAIR - 1][1]), (tq, LANES))
        for hh in range(PAIR - 1):
            lse = jnp.where(masks[hh], carry[hh][0] + jnp.log(carry[hh][1]), lse)
        lse_ref[...] = lse

    return _pcall(
        body, name="attn_fwd", grid=(H // PAIR, T // tq),
        in_specs=[pl.BlockSpec((tq, LANES), lambda hp, i: (i, qb + hp)),
                  pl.BlockSpec((T, LANES), lambda hp, i: (0, kb + hp)),
                  pl.BlockSpec((T, LANES), lambda hp, i: (0, vb + hp)),
                  pl.BlockSpec((tq, LANES), lambda hp, i: (i, 0)),
                  pl.BlockSpec((PAIR, nkv, 1, tk), lambda hp, i: (hp, 0, 0, 0))],
        out_specs=[pl.BlockSpec((tq, LANES), lambda hp, i: (i, hp)),
                   pl.BlockSpec((None, tq, LANES), lambda hp, i: (hp, i, 0))],
        out_shape=[jax.ShapeDtypeStruct((T, hd), F32), jax.ShapeDtypeStruct((H // PAIR, T, LANES), F32)],
        comm=comm,
    )(proj, proj, proj, cum, ck4)


def attn_bwd(proj, o, dycat, lse, cum, ck4, q_col, do_col, comm=None):
    T = proj.shape[0]
    H, nkv, _, tk = ck4.shape
    tq = tk
    nq = T // tq
    hd = H * HEAD_DIM
    qb, kb, vb = q_col // LANES, (q_col + hd) // LANES, (q_col + 2 * hd) // LANES
    dob = do_col // LANES
    scale = 1.0 / math.sqrt(HEAD_DIM)

    def body(q_ref, k_ref, v_ref, o_ref, do_ref, lse_ref, cum_ref, ck_ref,
             dq_ref, dk_ref, dv_ref, dcq_ref, dck_ref):
        hp = pl.program_id(0)
        j = pl.program_id(1)
        i = pl.program_id(2)

        def block(diagonal):
            masks = _head_masks(tq)
            q2, k2, v2, do2 = q_ref[...] * scale, k_ref[...], v_ref[...], do_ref[...]
            zero = jnp.zeros_like(q2)
            prod = do2.astype(F32) * o_ref[...]
            rows = pl.ds(pl.multiple_of(i * tq, tq), tq)
            dq_part = dk_part = dv_part = None
            dcq_part = jnp.zeros((tq, LANES), F32)
            lane = lax.broadcasted_iota(jnp.int32, (tq, LANES), 1)
            for hh in range(PAIR):
                qh = jnp.where(masks[hh], q2, zero)
                kh = jnp.where(masks[hh], k2, zero)
                doh = jnp.where(masks[hh], do2, zero)
                delta = jnp.sum(jnp.where(masks[hh], prod, 0.0), axis=-1, keepdims=True)
                s = lax.dot_general(qh, k2, NT, preferred_element_type=F32)
                s = s + _lane_column(cum_ref[...], PAIR * hp + hh) - ck_ref[hh]
                if diagonal:
                    s = jnp.where(_causal(tq, tk), s, NEG_INF)
                p = jnp.exp(s - _lane_column(lse_ref[...], hh * HEAD_DIM))
                dp = lax.dot_general(doh, v2, NT, preferred_element_type=F32)
                ds = p * (dp - delta)
                dsb = ds.astype(BF16)
                dv_h = lax.dot_general(p.astype(BF16), doh, TN, preferred_element_type=F32)
                dk_h = lax.dot_general(dsb, qh, TN, preferred_element_type=F32)
                dq_h = jnp.dot(dsb, kh, preferred_element_type=F32)
                dq_part = dq_h if dq_part is None else dq_part + dq_h
                dk_part = dk_h if dk_part is None else dk_part + dk_h
                dv_part = dv_h if dv_part is None else dv_part + dv_h
                dck_h = -jnp.sum(ds, axis=0, keepdims=True)
                dcq_part = jnp.where(lane == PAIR * hp + hh, jnp.sum(ds, axis=-1, keepdims=True), dcq_part)
                if diagonal:
                    dck_ref[hh] = dck_h
                else:
                    dck_ref[hh] += dck_h
            dq_part = dq_part * scale

            @pl.when(j == 0)
            def _():
                dq_ref[rows, :] = dq_part

            @pl.when(j > 0)
            def _():
                dq_ref[rows, :] += dq_part

            @pl.when(jnp.logical_and(hp == 0, j == 0))
            def _():
                dcq_ref[rows, :] = dcq_part

            @pl.when(jnp.logical_or(hp > 0, j > 0))
            def _():
                dcq_ref[rows, :] += dcq_part

            if diagonal:
                dk_ref[...] = dk_part
                dv_ref[...] = dv_part
            else:
                dk_ref[...] += dk_part
                dv_ref[...] += dv_part

        @pl.when(i == j)
        def _():
            block(True)

        @pl.when(i > j)
        def _():
            block(False)

    at_q = lambda col: pl.BlockSpec((tq, LANES), lambda hp, j, i: (jnp.maximum(i, j), col + hp))
    at_k = lambda col: pl.BlockSpec((tk, LANES), lambda hp, j, i: (j, col + hp))
    lse_spec = pl.BlockSpec((None, tq, LANES), lambda hp, j, i: (hp, jnp.maximum(i, j), 0))
    cum_spec = pl.BlockSpec((tq, LANES), lambda hp, j, i: (jnp.maximum(i, j), 0))
    ck_spec = pl.BlockSpec((PAIR, None, 1, tk), lambda hp, j, i: (hp, j, 0, 0))
    return _pcall(
        body, name="attn_bwd", grid=(H // PAIR, nkv, nq),
        in_specs=[at_q(qb), at_k(kb), at_k(vb), at_q(0), at_q(dob), lse_spec, cum_spec, ck_spec],
        out_specs=[pl.BlockSpec((T, LANES), lambda hp, j, i: (0, hp)), at_k(0), at_k(0),
                   pl.BlockSpec((T, LANES), lambda hp, j, i: (0, 0)), ck_spec],
        out_shape=[jax.ShapeDtypeStruct((T, hd), F32)] * 3
        + [jax.ShapeDtypeStruct((T, LANES), F32), jax.ShapeDtypeStruct((H, nkv, 1, tk), F32)],
        comm=comm,
    )(proj, proj, proj, o, dycat, lse, cum, ck4)


ELEMENTWISE_BLOCK_BYTES = 2 * 1024 * 1024
BF16_ROWS = 16


def cast_bf16(arrays, comm=None):
    def slab(a, steps):
        R, C = a.shape
        if R % (steps * BF16_ROWS) == 0:
            return pl.BlockSpec((R // steps, C), lambda i: (i, 0))
        if C % (steps * LANES) == 0:
            return pl.BlockSpec((R, C // steps), lambda i: (0, i))
        return None

    steps = 8 if all(slab(a, 8) is not None for a in arrays) else 4
    specs = [slab(a, steps) for a in arrays]
    n = len(arrays)

    def body(*refs):
        for src, dst in zip(refs[:n], refs[n:]):
            dst[...] = src[...].astype(BF16)

    return _pcall(body, name="cast_bf16", grid=(steps,), in_specs=specs, out_specs=specs,
                  out_shape=[jax.ShapeDtypeStruct(a.shape, BF16) for a in arrays], comm=comm)(*arrays)


def _ew_tiles(rows, cols, bytes_per_element):
    target = max(8, ELEMENTWISE_BLOCK_BYTES // max(1, cols * bytes_per_element))
    if rows <= target:
        return rows, cols
    t = (target // 16) * 16
    while t >= 16:
        if rows % t == 0:
            return t, cols
        t -= 16
    tc = _tile(cols, max(LANES, (ELEMENTWISE_BLOCK_BYTES // (rows * bytes_per_element)) // LANES * LANES))
    return rows, tc


def sum_chips(recv):
    nc, R, C = recv.shape
    tr, tc = _ew_tiles(R, C, 4)

    def body(r_ref, o_ref):
        acc = r_ref[0].astype(F32)
        for j in range(1, nc):
            acc = acc + r_ref[j].astype(F32)
        o_ref[...] = acc

    return _pcall(
        body, name="sum_chips", grid=(R // tr, C // tc),
        in_specs=[pl.BlockSpec((nc, tr, tc), lambda i, j: (0, i, j))],
        out_specs=[pl.BlockSpec((tr, tc), lambda i, j: (i, j))],
        out_shape=[jax.ShapeDtypeStruct((R, C), F32)],
    )(recv)[0]


def add_sibling_half(g, recv):
    nc, R, C = g.shape
    hr = R // 2
    tr, tc = _ew_tiles(hr, C, 4 * nc)

    def body(g_ref, r_ref, o_ref):
        c = lax.axis_index("c")
        for j in range(nc):
            o_ref[j] = (g_ref[j, c].astype(F32) + r_ref[j].astype(F32)).astype(BF16)

    return _pcall(
        body, name="add_sibling_half", grid=(hr // tr, C // tc),
        in_specs=[pl.BlockSpec((nc, 2, tr, tc), lambda i, j: (0, 0, i, j)),
                  pl.BlockSpec((nc, tr, tc), lambda i, j: (0, i, j))],
        out_specs=[pl.BlockSpec((nc, tr, tc), lambda i, j: (0, i, j))],
        out_shape=[jax.ShapeDtypeStruct((nc, hr, C), BF16)],
    )(g.reshape(nc, 2, hr, C), recv)[0]


def adamw(w, m, v, g_parts, comm=None, halves=False):
    R, C = w.shape
    tr, tc = _ew_tiles(R // 2 if halves else R, C, 4 * 4)
    n_g = len(g_parts)
    n_half = (R // 2) // tr
    c1 = 1.0 - ADAM_B1
    c2 = 1.0 - ADAM_B2
    bc1 = 1.0 - ADAM_B1 ** ADAM_STEP
    bc2 = 1.0 - ADAM_B2 ** ADAM_STEP

    def body(*refs):
        w_ref, m_ref, v_ref = refs[:3]
        g_refs = refs[3:3 + n_g]
        g_out, d_out, m_out, v_out = refs[3 + n_g:]
        if halves:
            mine = (pl.program_id(0) >= n_half) == (lax.axis_index("c") == 1)
            g = jnp.where(mine, g_refs[0][...], g_refs[1][...])
        else:
            g = g_refs[0][...]
            for r in g_refs[1:]:
                g = g + r[...]
        m_new = ADAM_B1 * m_ref[...] + c1 * g
        v_new = ADAM_B2 * v_ref[...] + c2 * (g * g)
        m_hat = m_new / bc1
        v_hat = v_new / bc2
        g_out[...] = g
        d_out[...] = -ADAM_LR * (m_hat / (jnp.sqrt(v_hat) + ADAM_EPS) + ADAM_WD * w_ref[...])
        m_out[...] = m_new
        v_out[...] = v_new

    spec = pl.BlockSpec((tr, tc), lambda i, j: (i, j))
    g_spec = pl.BlockSpec((tr, tc), lambda i, j: (i % n_half, j)) if halves else spec
    return _pcall(
        body, name="adamw", grid=(R // tr, C // tc),
        in_specs=[spec] * 3 + [g_spec] * n_g, out_specs=[spec] * 4,
        out_shape=[jax.ShapeDtypeStruct((R, C), F32)] * 4, comm=comm,
    )(w, m, v, *g_parts)


def _chip_coords():
    x, y, c = lax.axis_index("x"), lax.axis_index("y"), lax.axis_index("c")
    others = [(1 - x, y), (x, 1 - y), (1 - x, 1 - y)]
    return x, y, c, others


def _remote(src, dst, send_sem, recv_sem, device):
    return pltpu.make_async_remote_copy(src_ref=src, dst_ref=dst, send_sem=send_sem, recv_sem=recv_sem,
                                        device_id=device, device_id_type=MESH)


def gather_comm(shards):
    n = len(shards)
    pieces = [(a, jj) for a in range(n) for jj in range(3)]

    def makers(ins, outs, sems):
        send_sems, recv_sems, local_sems = sems
        x, y, c, others = _chip_coords()
        me = 2 * x + y
        sibling = (x, y, 1 - c)

        def half(ref, a, which, chip=None):
            rows, cols = ins[a].shape[0], ins[a].shape[1]
            lead = () if chip is None else (chip,)
            if rows % (2 * BF16_ROWS) == 0:
                return ref.at[(*lead, pl.ds(which * (rows // 2), rows // 2))]
            return ref.at[(*lead, slice(None), pl.ds(which * (cols // 2), cols // 2))]

        def local(a):
            return pltpu.make_async_copy(ins[a], outs[a].at[me], local_sems.at[a])

        def ici(a, jj):
            ox, oy = others[jj]
            return _remote(half(ins[a], a, c), half(outs[a], a, c, me),
                           send_sems.at[6 * a + jj], recv_sems.at[6 * a + jj], (ox, oy, c))

        def landed(a, jj):
            ox, oy = others[jj]
            got = half(outs[a], a, c, 2 * ox + oy)
            return (_remote(got, got, send_sems.at[6 * a + jj], recv_sems.at[6 * a + jj], (ox, oy, c)),
                    _remote(got, got, send_sems.at[6 * a + 3 + jj], recv_sems.at[6 * a + 3 + jj], sibling))

        def theirs(a, jj):
            ox, oy = others[jj]
            sib = half(outs[a], a, 1 - c, 2 * ox + oy)
            return _remote(sib, sib, send_sems.at[6 * a + 3 + jj], recv_sems.at[6 * a + 3 + jj], sibling)

        return local, ici, landed, theirs

    def start(ins, outs, sems):
        local, ici, _, _ = makers(ins, outs, sems)
        for a in range(n):
            local(a).start()
        for a, jj in pieces:
            ici(a, jj).start()

    def finish(ins, outs, sems):
        local, ici, landed, theirs = makers(ins, outs, sems)
        forwards = []
        for a, jj in pieces:
            got, fwd = landed(a, jj)
            got.wait_recv()
            fwd.start()
            forwards.append(fwd)
        for a, jj in pieces:
            theirs(a, jj).wait_recv()
        for a, jj in pieces:
            ici(a, jj).wait_send()
        for fwd in forwards:
            fwd.wait_send()
        for a in range(n):
            local(a).wait()

    return Comm(shards, [jax.ShapeDtypeStruct((N_CHIP,) + s.shape, s.dtype) for s in shards],
                [pltpu.SemaphoreType.DMA((6 * n,)), pltpu.SemaphoreType.DMA((6 * n,)),
                 pltpu.SemaphoreType.DMA((n,))], start, finish)


def scatter_comm(grads):
    n = len(grads)
    pieces = [(a, jj) for a in range(n) for jj in range(3)]

    def makers(ins, outs, sems):
        send_sems, recv_sems, local_sems = sems
        x, y, c, others = _chip_coords()
        me = 2 * x + y

        def local(a):
            return pltpu.make_async_copy(ins[a].at[me], outs[a].at[me], local_sems.at[a])

        def ici(a, jj):
            ox, oy = others[jj]
            return _remote(ins[a].at[2 * ox + oy], outs[a].at[me], send_sems.at[3 * a + jj],
                           recv_sems.at[3 * a + jj], (ox, oy, c))

        def landed(a, jj):
            ox, oy = others[jj]
            slot = outs[a].at[2 * ox + oy]
            return _remote(slot, slot, send_sems.at[3 * a + jj], recv_sems.at[3 * a + jj], (ox, oy, c))

        return local, ici, landed

    def start(ins, outs, sems):
        local, ici, _ = makers(ins, outs, sems)
        for a in range(n):
            local(a).start()
        for a, jj in pieces:
            ici(a, jj).start()

    def finish(ins, outs, sems):
        local, ici, landed = makers(ins, outs, sems)
        for a, jj in pieces:
            landed(a, jj).wait_recv()
        for a, jj in pieces:
            ici(a, jj).wait_send()
        for a in range(n):
            local(a).wait()

    return Comm(grads, [jax.ShapeDtypeStruct(g.shape, g.dtype) for g in grads],
                [pltpu.SemaphoreType.DMA((3 * n,)), pltpu.SemaphoreType.DMA((3 * n,)),
                 pltpu.SemaphoreType.DMA((n,))], start, finish)


def halfswap_comm(grads):
    n = len(grads)

    def copies(ins, outs, sems):
        send_sems, recv_sems = sems
        x, y, c, _ = _chip_coords()
        out = []
        for a in range(n):
            hr = ins[a].shape[1] // 2
            out.append(_remote(ins[a].at[:, pl.ds((1 - c) * hr, hr)], outs[a], send_sems.at[a], recv_sems.at[a],
                               (x, y, 1 - c)))
        return out

    def start(ins, outs, sems):
        for cp in copies(ins, outs, sems):
            cp.start()

    def finish(ins, outs, sems):
        for cp in copies(ins, outs, sems):
            cp.wait()

    return Comm(grads, [jax.ShapeDtypeStruct((g.shape[0], g.shape[1] // 2, g.shape[2]), g.dtype) for g in grads],
                [pltpu.SemaphoreType.DMA((n,)), pltpu.SemaphoreType.DMA((n,))], start, finish)


def join_comms(first, second):
    ni, no, ns = len(first.operands), len(first.out_shape), len(first.sems)

    def start(ins, outs, sems):
        first.start(ins[:ni], outs[:no], sems[:ns])
        second.start(ins[ni:], outs[no:], sems[ns:])

    def finish(ins, outs, sems):
        first.finish(ins[:ni], outs[:no], sems[:ns])
        second.finish(ins[ni:], outs[no:], sems[ns:])

    return Comm(first.operands + second.operands, first.out_shape + second.out_shape, first.sems + second.sems,
                start, finish)


def swap_comm(parts):
    n = len(parts)

    def copies(ins, outs, sems):
        send_sems, recv_sems = sems
        x, y, c, _ = _chip_coords()
        return [_remote(ins[a], outs[a], send_sems.at[a], recv_sems.at[a], (x, y, 1 - c)) for a in range(n)]

    def start(ins, outs, sems):
        for cp in copies(ins, outs, sems):
            cp.start()

    def finish(ins, outs, sems):
        for cp in copies(ins, outs, sems):
            cp.wait()

    return Comm(parts, [jax.ShapeDtypeStruct(p.shape, p.dtype) for p in parts],
                [pltpu.SemaphoreType.DMA((n,)), pltpu.SemaphoreType.DMA((n,))], start, finish)


def allreduce_small(v):
    R = v.shape[0]

    def body(v_ref, sum_ref, all_ref, send_sems, recv_sems):
        x, y, c = lax.axis_index("x"), lax.axis_index("y"), lax.axis_index("c")
        me = 4 * x + 2 * y + c
        all_ref[me] = v_ref[...]
        copies = []
        for k in range(1, N_DEV):
            px = 1 - x if k & 4 else x
            py = 1 - y if k & 2 else y
            pc = 1 - c if k & 1 else c
            cp = pltpu.make_async_remote_copy(
                src_ref=v_ref, dst_ref=all_ref.at[me], send_sem=send_sems.at[k - 1], recv_sem=recv_sems.at[k - 1],
                device_id=(px, py, pc), device_id_type=MESH)
            cp.start()
            copies.append((cp, 4 * px + 2 * py + pc))
        for k, (cp, peer) in enumerate(copies):
            pltpu.make_async_remote_copy(
                src_ref=v_ref, dst_ref=all_ref.at[peer], send_sem=send_sems.at[k], recv_sem=recv_sems.at[k],
                device_id=(x, y, c), device_id_type=MESH).wait_recv()
        for cp, _ in copies:
            cp.wait_send()
        acc = all_ref[0]
        for d in range(1, N_DEV):
            acc = acc + all_ref[d]
        sum_ref[...] = acc

    vm = pl.BlockSpec(memory_space=pltpu.VMEM)
    return pl.pallas_call(
        body, name="allreduce_small",
        in_specs=[vm], out_specs=[vm, vm],
        out_shape=[jax.ShapeDtypeStruct((R, LANES), F32), jax.ShapeDtypeStruct((N_DEV, R, LANES), F32)],
        scratch_shapes=[pltpu.SemaphoreType.DMA((N_DEV - 1,)), pltpu.SemaphoreType.DMA((N_DEV - 1,))],
    )(v)[0]


SMALL_NAMES = ("ffn1_norm", "mix_norm", "ffn2_norm", "final_norm", "conv_b", "conv_ln_g", "conv_ln_b")


def _pack_small(vecs, bias, conv_w_rows, loss_tile):
    rows = [vecs[n].reshape(-1, LANES) for n in SMALL_NAMES]
    rows.append(bias.reshape(1, LANES))
    rows.append(conv_w_rows.reshape(-1, LANES))
    rows.append(loss_tile[0:1, :])
    packed = jnp.concatenate(rows, axis=0)
    pad = (-packed.shape[0]) % 8
    return jnp.pad(packed, ((0, pad), (0, 0)))


def _unpack_small(packed, sizes, n_conv_rows):
    out, r = {}, 0
    for n in SMALL_NAMES:
        k = sizes[n] // LANES
        out[n] = packed[r:r + k].reshape(-1)
        r += k
    out["fgate_bias"] = packed[r]
    r += 1
    out["conv_w"] = packed[r:r + n_conv_rows]
    r += n_conv_rows
    out["loss"] = packed[r, 0]
    return out


def kernel(x, ffn1_norm, ffn1_w_gate, ffn1_w_up, ffn1_w_down, mix_norm, w_in, fgate_bias, conv_w, conv_b, conv_ln_g, conv_ln_b, w_out, ffn2_norm, ffn2_w_gate, ffn2_w_up, ffn2_w_down, final_norm, loss_target, m_ffn1_norm, m_ffn1_w_gate, m_ffn1_w_up, m_ffn1_w_down, m_mix_norm, m_w_in, m_fgate_bias, m_conv_w, m_conv_b, m_conv_ln_g, m_conv_ln_b, m_w_out, m_ffn2_norm, m_ffn2_w_gate, m_ffn2_w_up, m_ffn2_w_down, m_final_norm, v_ffn1_norm, v_ffn1_w_gate, v_ffn1_w_up, v_ffn1_w_down, v_mix_norm, v_w_in, v_fgate_bias, v_conv_w, v_conv_b, v_conv_ln_g, v_conv_ln_b, v_w_out, v_ffn2_norm, v_ffn2_w_gate, v_ffn2_w_up, v_ffn2_w_down, v_final_norm):
    w = dict(ffn1_norm=ffn1_norm, ffn1_w_gate=ffn1_w_gate, ffn1_w_up=ffn1_w_up, ffn1_w_down=ffn1_w_down,
             mix_norm=mix_norm, w_in=w_in, fgate_bias=fgate_bias, conv_w=conv_w, conv_b=conv_b,
             conv_ln_g=conv_ln_g, conv_ln_b=conv_ln_b, w_out=w_out, ffn2_norm=ffn2_norm,
             ffn2_w_gate=ffn2_w_gate, ffn2_w_up=ffn2_w_up, ffn2_w_down=ffn2_w_down, final_norm=final_norm)
    m = dict(ffn1_norm=m_ffn1_norm, ffn1_w_gate=m_ffn1_w_gate, ffn1_w_up=m_ffn1_w_up, ffn1_w_down=m_ffn1_w_down,
             mix_norm=m_mix_norm, w_in=m_w_in, fgate_bias=m_fgate_bias, conv_w=m_conv_w, conv_b=m_conv_b,
             conv_ln_g=m_conv_ln_g, conv_ln_b=m_conv_ln_b, w_out=m_w_out, ffn2_norm=m_ffn2_norm,
             ffn2_w_gate=m_ffn2_w_gate, ffn2_w_up=m_ffn2_w_up, ffn2_w_down=m_ffn2_w_down, final_norm=m_final_norm)
    v = dict(ffn1_norm=v_ffn1_norm, ffn1_w_gate=v_ffn1_w_gate, ffn1_w_up=v_ffn1_w_up, ffn1_w_down=v_ffn1_w_down,
             mix_norm=v_mix_norm, w_in=v_w_in, fgate_bias=v_fgate_bias, conv_w=v_conv_w, conv_b=v_conv_b,
             conv_ln_g=v_conv_ln_g, conv_ln_b=v_conv_ln_b, w_out=v_w_out, ffn2_norm=v_ffn2_norm,
             ffn2_w_gate=v_ffn2_w_gate, ffn2_w_up=v_ffn2_w_up, ffn2_w_down=v_ffn2_w_down, final_norm=v_final_norm)
    names = list(w.keys())
    big = ("ffn1_w_gate", "ffn1_w_up", "ffn1_w_down", "w_in", "w_out", "ffn2_w_gate", "ffn2_w_up", "ffn2_w_down")

    T, D = x.shape[1], x.shape[2]
    C = conv_b.shape[0]
    H = fgate_bias.shape[0]
    cs = conv_w.shape[1]
    in_cols = N_CHIP * w_in.shape[1]
    p_main = in_cols - H

    x0, tgt = x[0], loss_target[0]
    tk = _tile(T, 512, 128)
    nkv = T // tk
    row = lambda a: a.reshape(1, -1)
    grad, delta, new_m, new_v = {}, {}, {}, {}

    def update(n, parts, comm=None, halves=False):
        args = (w[n], m[n], v[n])
        if n == "w_in":
            outs = [t.T for t in adamw(*[a.T for a in args], parts, comm=comm)]
        else:
            outs = adamw(*args, parts, comm=comm, halves=halves)
        grad[n], delta[n], new_m[n], new_v[n] = outs

    rest = [n for n in big if n != "ffn1_w_gate"]
    g0 = gather_comm([w["ffn1_w_gate"].astype(BF16), jnp.pad(conv_w, ((0, HALO - CONV_K), (0, 0)))])
    wb = dict(zip(rest, cast_bf16([w[n].T if n == "w_in" else w[n] for n in rest], comm=g0)))
    wg1, conv_w4 = g0.results
    conv_w_full = conv_w4.transpose(1, 0, 2).reshape(HALO, C)
    h1, r1 = rms_fwd(x0, row(ffn1_norm))
    g1a = gather_comm([wb["ffn1_w_up"]])
    a1 = ffn_gate(h1, wg1, comm=g1a)
    wu1 = g1a.results[0]
    g1b = gather_comm([wb["ffn1_w_down"]])
    b1, mid1 = ffn_upmul(h1, wu1, a1, comm=g1b)
    wd1 = g1b.results[0]
    g2 = gather_comm([wb["w_in"]])
    x1 = mm_residual("ffn_down_g", mid1, wd1, x0, 0.5, comm=g2)[0]
    w_t = g2.results[0].reshape(in_cols, D)

    wf_t = jnp.pad(w_t[p_main:], ((0, LANES - H), (0, 0)))
    bias_pad = jnp.pad(row(fgate_bias), ((0, 0), (0, LANES - H)))
    h2, r2 = rms_fwd(x1, row(mix_norm))
    proj = proj_main(h2, w_t, p_main)
    f, cum = fgate_fwd(h2, wf_t, bias_pad, H)
    ypre, yconv = conv_fwd(proj, conv_w_full, row(conv_b), row(conv_ln_g), row(conv_ln_b))
    ck4 = cum[:, :H].T.reshape(H, nkv, 1, tk)
    g3 = gather_comm([wb["w_out"], wb["ffn2_w_gate"], wb["ffn2_w_up"], wb["ffn2_w_down"]])
    o, lse = attn_fwd(proj, cum, ck4, 2 * C, comm=g3)
    w_out3, wg2, wu2, wd2 = g3.results
    ycat = jnp.concatenate([yconv, o.astype(BF16)], axis=1)
    x2 = mm_residual("out_proj", ycat, w_out3, x1, 1.0)[0]

    h3, r3 = rms_fwd(x2, row(ffn2_norm))
    a2, b2, mid2 = ffn_up(h3, wg2, wu2)
    x3 = mm_residual("ffn_down", mid2, wd2, x2, 0.5)[0]
    dx3, dx3b, loss_tile, d_final = final_loss(x3, tgt, row(final_norm))

    da2, db2 = ffn_bwd_mid(dx3b, wd2, a2, b2)
    dwd2 = dw_rowshard("ffn_dwd", mid2, dx3b, N_CHIP)[0]
    s1 = scatter_comm([dwd2])
    dwg2, dwu2 = dw_colshard("ffn_dwgu_s", h3, [da2, db2], N_CHIP, comm=s1)
    s2 = scatter_comm([dwg2])
    dh3 = ffn_dh(da2, db2, wg2, wu2, comm=s2)
    dx2, dx2b, d_ffn2_norm = rms_bwd(dh3, x2, r3, row(ffn2_norm), dx3, 1.0)

    dycat = mm_nt_bf16("out_proj_dy", dx2b, w_out3.reshape(-1, D))
    dw_out3 = dw_rowshard("out_proj_dw", ycat, dx2b, N_CHIP)[0]
    s3 = scatter_comm([dwu2, dw_out3])
    dq, dk, dv, dcq, dck4 = attn_bwd(proj, o, dycat, lse, cum, ck4, 2 * C, C, comm=s3)
    dc = dcq + jnp.pad(dck4.reshape(H, T).T, ((0, 0), (0, LANES - H)))
    df, d_bias = fgate_bwd(dc, f, H)
    dag, d_conv_w, d_conv_b, d_ln_g, d_ln_b = conv_bwd(proj, ypre, dycat, conv_w_full, row(conv_ln_g),
                                                       row(conv_ln_b))
    dproj = jnp.concatenate([dag, dq.astype(BF16), dk.astype(BF16), dv.astype(BF16)], axis=1)
    early = ("ffn2_w_down", "ffn2_w_gate", "ffn2_w_up", "w_out")
    early_sums = [sum_chips(r) for r in (s1.results[0], s2.results[0], s3.results[0], s3.results[1])]
    sw1 = swap_comm(early_sums)
    dh2 = proj_dh(dproj, w_t, df, wf_t, comm=sw1)
    dw_t, dwf_t = proj_dw(dproj, df, h2, in_cols)
    dw_t = lax.dynamic_update_slice(dw_t, dwf_t[:H].astype(BF16), (p_main, 0))
    dw_in3 = dw_t.reshape(N_CHIP, in_cols // N_CHIP, D)
    dx1, dx1b, d_mix_norm = rms_bwd(dh2, x1, r2, row(mix_norm), dx2, 0.5)

    s4 = scatter_comm([dw_in3])
    da1, db1 = ffn_bwd_mid(dx1b, wd1, a1, b1, comm=s4)
    dwd1 = dw_rowshard("ffn_dwd", mid1, dx1b, N_CHIP)[0]
    s5 = scatter_comm([dwd1])
    dwg1, dwu1 = dw_colshard("ffn_dwgu_s", h1, [da1, db1], N_CHIP, comm=s5)
    mid_sums = [sum_chips(s4.results[0]), sum_chips(s5.results[0])]
    s6 = join_comms(join_comms(scatter_comm([dwg1]), halfswap_comm([dwu1])), swap_comm(mid_sums))
    dh1 = ffn_dh(da1, db1, wg1, wu1, comm=s6)
    recv_g1, sibling_u1, their_in, their_d1 = s6.results
    grad_x, _, d_ffn1_norm = rms_bwd(dh1, x0, r1, row(ffn1_norm), dx1, 1.0)

    s7 = scatter_comm([add_sibling_half(dwu1, sibling_u1)])
    for i, (n, mine, other) in enumerate(zip(early, early_sums, sw1.results)):
        update(n, [mine, other], comm=s7 if i == 0 else None)
    update("w_in", [mid_sums[0], their_in])
    update("ffn1_w_down", [mid_sums[1], their_d1])
    sum_g1, half_u1 = sum_chips(recv_g1), sum_chips(s7.results[0])
    their_g1, their_u1 = _run_comm("swap_last", swap_comm([sum_g1, half_u1]))
    update("ffn1_w_gate", [sum_g1, their_g1])
    update("ffn1_w_up", [half_u1, their_u1], halves=True)

    gl = dict(ffn1_norm=d_ffn1_norm, mix_norm=d_mix_norm, ffn2_norm=d_ffn2_norm, final_norm=d_final,
              conv_b=d_conv_b, conv_ln_g=d_ln_g, conv_ln_b=d_ln_b)
    small_sizes = {n: w[n].shape[0] for n in SMALL_NAMES}
    packed = _pack_small(gl, d_bias, d_conv_w, loss_tile)
    red = _unpack_small(allreduce_small(packed), small_sizes, HALO * C // LANES)
    loss = red["loss"]
    my_chip = 2 * lax.axis_index("x") + lax.axis_index("y")
    g_conv_w = lax.dynamic_slice_in_dim(red["conv_w"].reshape(HALO, C)[:CONV_K], my_chip * cs, cs, axis=1)
    update("conv_w", [g_conv_w])
    vec_names = SMALL_NAMES + ("fgate_bias",)
    stack = lambda d: jnp.concatenate(
        [jnp.pad(d[n], (0, (-d[n].shape[0]) % LANES)).reshape(-1, LANES) for n in vec_names], axis=0)
    g_stack = jnp.concatenate([red[n].reshape(-1, LANES) for n in SMALL_NAMES] + [red["fgate_bias"][None, :]],
                              axis=0)
    outs = adamw(stack(w), stack(m), stack(v), [g_stack])
    r = 0
    for n in vec_names:
        size = w[n].shape[0]
        k = -(-size // LANES)
        for dst, src in zip((grad, delta, new_m, new_v), outs):
            dst[n] = src[r:r + k].reshape(-1)[:size]
        r += k

    return (loss, grad_x[None], *[grad[n] for n in names], *[delta[n] for n in names],
            *[new_m[n] for n in names], *[new_v[n] for n in names])
```

```python
import functools
import math

import jax
import jax.numpy as jnp
from jax import lax
from jax.experimental import pallas as pl
from jax.experimental.pallas import tpu as pltpu

F32 = jnp.float32
BF16 = jnp.bfloat16
NORM_EPS = 1e-6
LN_EPS = 1e-5
NEG_INF = -1e30
HEAD_DIM = 64
CONV_K = 31
HALO = 32
LANES = 128
N_CHIP = 4
N_DEV = 8
VMEM_LIMIT = 52 * 1024 * 1024
MESH = pl.DeviceIdType.MESH

ADAM_LR = 0.001
ADAM_B1 = 0.9
ADAM_B2 = 0.999
ADAM_EPS = 1e-08
ADAM_WD = 0.01
ADAM_STEP = 10

NN = (((1,), (0,)), ((), ()))
NT = (((1,), (1,)), ((), ()))
TN = (((0,), (0,)), ((), ()))


def _tile(n, pref, unit=128):
    if n <= pref:
        return n
    t = (pref // unit) * unit
    while t > 0:
        if n % t == 0:
            return t
        t -= unit
    raise ValueError(f"no tile for {n} under {pref}")


class Comm:
    def __init__(self, operands, out_shape, sems, start, finish):
        self.operands, self.out_shape, self.sems = list(operands), list(out_shape), list(sems)
        self.start, self.finish = start, finish
        self.results = None


def _pcall(body, *, name, grid, in_specs, out_specs, out_shape, scratch=(), comm=None):
    params = pltpu.CompilerParams(dimension_semantics=("arbitrary",) * len(grid), vmem_limit_bytes=VMEM_LIMIT)
    scratch = list(scratch)
    if comm is None:
        return pl.pallas_call(body, name=name, grid=grid, in_specs=in_specs, out_specs=out_specs,
                              out_shape=out_shape, scratch_shapes=scratch, compiler_params=params)
    n_in, n_out, n_s = len(in_specs), len(out_shape), len(scratch)
    n_ci, n_co = len(comm.operands), len(comm.out_shape)
    any_spec = pl.BlockSpec(memory_space=pl.ANY)

    def carried(*refs):
        ins, refs = refs[:n_in], refs[n_in:]
        c_ins, refs = refs[:n_ci], refs[n_ci:]
        outs, refs = refs[:n_out], refs[n_out:]
        c_outs, refs = refs[:n_co], refs[n_co:]
        scr, c_sems = refs[:n_s], refs[n_s:]
        first = pl.program_id(0) == 0
        last = pl.program_id(0) == grid[0] - 1
        for d in range(1, len(grid)):
            first = jnp.logical_and(first, pl.program_id(d) == 0)
            last = jnp.logical_and(last, pl.program_id(d) == grid[d] - 1)

        @pl.when(first)
        def _():
            comm.start(c_ins, c_outs, c_sems)

        body(*ins, *outs, *scr)

        @pl.when(last)
        def _():
            comm.finish(c_ins, c_outs, c_sems)

    call = pl.pallas_call(
        carried, name=name, grid=grid, in_specs=list(in_specs) + [any_spec] * n_ci,
        out_specs=list(out_specs) + [any_spec] * n_co, out_shape=list(out_shape) + comm.out_shape,
        scratch_shapes=scratch + comm.sems, compiler_params=params)

    def run(*operands):
        res = call(*operands, *comm.operands)
        comm.results = list(res[n_out:])
        return list(res[:n_out])

    return run


def _run_comm(name, comm):
    n_ci, n_co = len(comm.operands), len(comm.out_shape)
    any_spec = pl.BlockSpec(memory_space=pl.ANY)

    def body(*refs):
        c_ins, c_outs, c_sems = refs[:n_ci], refs[n_ci:n_ci + n_co], refs[n_ci + n_co:]
        comm.start(c_ins, c_outs, c_sems)
        comm.finish(c_ins, c_outs, c_sems)

    return pl.pallas_call(body, name=name, in_specs=[any_spec] * n_ci, out_specs=[any_spec] * n_co,
                          out_shape=comm.out_shape, scratch_shapes=comm.sems)(*comm.operands)


def _sigmoid(x):
    return 1.0 / (1.0 + jnp.exp(-x))


def _mm(name, *, grid, pairs, once_pairs=(), extra=(), out_shape, out_specs, acc_shapes, nk, kaxis, epilogue,
        comm=None):
    all_pairs = list(pairs) + list(once_pairs)
    n_p, n_o = len(pairs), len(once_pairs)
    n_e, n_out, n_acc = len(extra), len(out_shape), len(acc_shapes)

    def body(*refs):
        ab = refs[: 2 * (n_p + n_o)]
        ex = refs[2 * (n_p + n_o): 2 * (n_p + n_o) + n_e]
        outs = refs[2 * (n_p + n_o) + n_e: 2 * (n_p + n_o) + n_e + n_out]
        accs = refs[2 * (n_p + n_o) + n_e + n_out:]

        def dots(idx_range):
            vals = [None] * n_acc
            for p in idx_range:
                d = lax.dot_general(ab[2 * p][...], ab[2 * p + 1][...], all_pairs[p][4],
                                    preferred_element_type=F32)
                ai = all_pairs[p][5]
                vals[ai] = d if vals[ai] is None else vals[ai] + d
            return vals

        if nk == 1:
            vals = dots(range(n_p + n_o))
            epilogue(vals, ex, outs)
            return

        k = pl.program_id(kaxis)

        @pl.when(k == 0)
        def _():
            vals = dots(range(n_p + n_o))
            for ai in range(n_acc):
                accs[ai][...] = vals[ai]

        @pl.when(k > 0)
        def _():
            vals = dots(range(n_p))
            for ai in range(n_acc):
                if vals[ai] is not None:
                    accs[ai][...] += vals[ai]

        @pl.when(k == nk - 1)
        def _():
            epilogue([a[...] for a in accs], ex, outs)

    operands, in_specs = [], []
    for p in all_pairs:
        operands += [p[0], p[2]]
        in_specs += [p[1], p[3]]
    for arr, spec in extra:
        operands.append(arr)
        in_specs.append(spec)
    scratch = [pltpu.VMEM(s, F32) for s in acc_shapes] if nk > 1 else []
    return _pcall(body, name=name, grid=grid, in_specs=in_specs, out_specs=out_specs, out_shape=out_shape,
                  scratch=scratch, comm=comm)(*operands)


def rms_fwd(x, g):
    T, D = x.shape
    tt = _tile(T, 512, 8)

    def body(x_ref, g_ref, h_ref, r_ref):
        xv = x_ref[...]
        r = lax.rsqrt(jnp.mean(xv * xv, axis=-1, keepdims=True) + NORM_EPS)
        h_ref[...] = (xv * r * g_ref[...]).astype(BF16)
        r_ref[...] = r

    return _pcall(
        body, name="rms_fwd", grid=(T // tt,),
        in_specs=[pl.BlockSpec((tt, D), lambda i: (i, 0)), pl.BlockSpec((1, D), lambda i: (0, 0))],
        out_specs=[pl.BlockSpec((tt, D), lambda i: (i, 0)), pl.BlockSpec((tt, 1), lambda i: (i, 0))],
        out_shape=[jax.ShapeDtypeStruct((T, D), BF16), jax.ShapeDtypeStruct((T, 1), F32)],
    )(x, g)


def rms_bwd(dh, x, r, g, dres, out_scale):
    T, D = x.shape
    tt = _tile(T, 256, 8)

    def body(dh_ref, x_ref, r_ref, g_ref, dres_ref, dx_ref, dxb_ref, dg_ref):
        i = pl.program_id(0)
        xh = x_ref[...] * r_ref[...]
        dhv = dh_ref[...]
        dxh = dhv * g_ref[...]
        dx = dres_ref[...] + r_ref[...] * (dxh - xh * jnp.mean(dxh * xh, axis=-1, keepdims=True))
        dx_ref[...] = dx
        dxb_ref[...] = (out_scale * dx).astype(BF16)
        part = jnp.sum(dhv * xh, axis=0, keepdims=True)

        @pl.when(i == 0)
        def _():
            dg_ref[...] = part

        @pl.when(i > 0)
        def _():
            dg_ref[...] += part

    row = pl.BlockSpec((tt, D), lambda i: (i, 0))
    return _pcall(
        body, name="rms_bwd", grid=(T // tt,),
        in_specs=[row, row, pl.BlockSpec((tt, 1), lambda i: (i, 0)), pl.BlockSpec((1, D), lambda i: (0, 0)), row],
        out_specs=[row, row, pl.BlockSpec((1, D), lambda i: (0, 0))],
        out_shape=[jax.ShapeDtypeStruct((T, D), F32), jax.ShapeDtypeStruct((T, D), BF16),
                   jax.ShapeDtypeStruct((1, D), F32)],
    )(dh, x, r, g, dres)


def final_loss(x, tgt, g):
    T, D = x.shape
    tt = _tile(T, 256, 8)

    def body(x_ref, t_ref, g_ref, dx_ref, dxb_ref, loss_ref, dg_ref):
        i = pl.program_id(0)
        xv = x_ref[...]
        r = lax.rsqrt(jnp.mean(xv * xv, axis=-1, keepdims=True) + NORM_EPS)
        xh = xv * r
        err = xh * g_ref[...] - t_ref[...]
        part_loss = 0.5 * jnp.sum(jnp.mean(err * err, axis=-1, keepdims=True), axis=0, keepdims=True)
        dy = err * (1.0 / D)
        dxh = dy * g_ref[...]
        dx = r * (dxh - xh * jnp.mean(dxh * xh, axis=-1, keepdims=True))
        dx_ref[...] = dx
        dxb_ref[...] = (0.5 * dx).astype(BF16)
        part_g = jnp.sum(dy * xh, axis=0, keepdims=True)
        part_l = jnp.broadcast_to(part_loss, (8, LANES))

        @pl.when(i == 0)
        def _():
            dg_ref[...] = part_g
            loss_ref[...] = part_l

        @pl.when(i > 0)
        def _():
            dg_ref[...] += part_g
            loss_ref[...] += part_l

    row = pl.BlockSpec((tt, D), lambda i: (i, 0))
    return _pcall(
        body, name="final_loss", grid=(T // tt,),
        in_specs=[row, row, pl.BlockSpec((1, D), lambda i: (0, 0))],
        out_specs=[row, row, pl.BlockSpec((8, LANES), lambda i: (0, 0)), pl.BlockSpec((1, D), lambda i: (0, 0))],
        out_shape=[jax.ShapeDtypeStruct((T, D), F32), jax.ShapeDtypeStruct((T, D), BF16),
                   jax.ShapeDtypeStruct((8, LANES), F32), jax.ShapeDtypeStruct((1, D), F32)],
    )(x, tgt, g)


def ffn_gate(h, wg3, comm=None):
    T, D = h.shape
    nc, _, fs = wg3.shape
    tm = _tile(T, 512, 8)

    def epilogue(vals, ex, outs):
        outs[0][...] = vals[0].astype(BF16)

    return _mm("ffn_gate", grid=(nc, T // tm),
               pairs=[(h, pl.BlockSpec((tm, D), lambda j, i: (i, 0)),
                       wg3, pl.BlockSpec((None, D, fs), lambda j, i: (j, 0, 0)), NN, 0)],
               out_shape=[jax.ShapeDtypeStruct((T, nc * fs), BF16)],
               out_specs=[pl.BlockSpec((tm, fs), lambda j, i: (i, j))],
               acc_shapes=[(tm, fs)], nk=1, kaxis=None, epilogue=epilogue, comm=comm)[0]


def ffn_upmul(h, wu3, a, comm=None):
    T, D = h.shape
    nc, _, fs = wu3.shape
    tm = _tile(T, 512, 8)

    def epilogue(vals, ex, outs):
        b = vals[0]
        av = ex[0][...].astype(F32)
        outs[0][...] = b.astype(BF16)
        outs[1][...] = (av * _sigmoid(av) * b).astype(BF16)

    t_spec = pl.BlockSpec((tm, fs), lambda j, i: (i, j))
    o_shape = jax.ShapeDtypeStruct((T, nc * fs), BF16)
    return _mm("ffn_upmul", grid=(nc, T // tm),
               pairs=[(h, pl.BlockSpec((tm, D), lambda j, i: (i, 0)),
                       wu3, pl.BlockSpec((None, D, fs), lambda j, i: (j, 0, 0)), NN, 0)],
               extra=[(a, t_spec)], out_shape=[o_shape] * 2, out_specs=[t_spec] * 2,
               acc_shapes=[(tm, fs)], nk=1, kaxis=None, epilogue=epilogue, comm=comm)


def ffn_up(h, wg3, wu3, comm=None):
    T, D = h.shape
    nc, _, fs = wg3.shape
    tm = _tile(T, 512, 8)

    def epilogue(vals, ex, outs):
        a, b = vals
        outs[0][...] = a.astype(BF16)
        outs[1][...] = b.astype(BF16)
        outs[2][...] = (a * _sigmoid(a) * b).astype(BF16)

    h_spec = pl.BlockSpec((tm, D), lambda j, i: (i, 0))
    w_spec = pl.BlockSpec((None, D, fs), lambda j, i: (j, 0, 0))
    o_spec = pl.BlockSpec((tm, fs), lambda j, i: (i, j))
    o_shape = jax.ShapeDtypeStruct((T, nc * fs), BF16)
    return _mm("ffn_up", grid=(nc, T // tm),
               pairs=[(h, h_spec, wg3, w_spec, NN, 0), (h, h_spec, wu3, w_spec, NN, 1)],
               out_shape=[o_shape] * 3, out_specs=[o_spec] * 3, acc_shapes=[(tm, fs)] * 2, nk=1, kaxis=None,
               epilogue=epilogue, comm=comm)


def mm_residual(name, a, b3, res, scale, comm=None):
    T = a.shape[0]
    nk, tk, N = b3.shape
    tm, tn = _tile(T, 512, 8), _tile(N, 2048)

    def epilogue(vals, ex, outs):
        outs[0][...] = ex[0][...] + scale * vals[0]

    return _mm(name, grid=(T // tm, N // tn, nk),
               pairs=[(a, pl.BlockSpec((tm, tk), lambda i, n, k: (i, k)),
                       b3, pl.BlockSpec((None, tk, tn), lambda i, n, k: (k, 0, n)), NN, 0)],
               extra=[(res, pl.BlockSpec((tm, tn), lambda i, n, k: (i, n)))],
               out_shape=[jax.ShapeDtypeStruct((T, N), F32)],
               out_specs=[pl.BlockSpec((tm, tn), lambda i, n, k: (i, n))],
               acc_shapes=[(tm, tn)], nk=nk, kaxis=2, epilogue=epilogue, comm=comm)


def ffn_bwd_mid(dout, wd3, a, b, comm=None):
    T, D = dout.shape
    nc, fs, _ = wd3.shape
    tm = _tile(T, 512, 8)

    def epilogue(vals, ex, outs):
        dm = vals[0]
        av = ex[0][...].astype(F32)
        bv = ex[1][...].astype(F32)
        s = _sigmoid(av)
        outs[0][...] = (dm * bv * (s * (1.0 + av * (1.0 - s)))).astype(BF16)
        outs[1][...] = (dm * (av * s)).astype(BF16)

    t_spec = pl.BlockSpec((tm, fs), lambda j, i: (i, j))
    o_shape = jax.ShapeDtypeStruct((T, nc * fs), BF16)
    return _mm("ffn_bwd_mid", grid=(nc, T // tm),
               pairs=[(dout, pl.BlockSpec((tm, D), lambda j, i: (i, 0)),
                       wd3, pl.BlockSpec((None, fs, D), lambda j, i: (j, 0, 0)), NT, 0)],
               extra=[(a, t_spec), (b, t_spec)],
               out_shape=[o_shape] * 2, out_specs=[t_spec] * 2, acc_shapes=[(tm, fs)], nk=1, kaxis=None,
               epilogue=epilogue, comm=comm)


def dw_rowshard(name, a, b, nc, comm=None):
    T, M = a.shape
    N = b.shape[1]
    ms = M // nc
    tn, tk = _tile(N, 2048 if ms <= 512 else 1024), _tile(T, 1024, 16)

    def epilogue(vals, ex, outs):
        outs[0][...] = vals[0].astype(BF16)

    return _mm(name, grid=(nc, N // tn, T // tk),
               pairs=[(a, pl.BlockSpec((tk, ms), lambda j, n, k: (k, j)),
                       b, pl.BlockSpec((tk, tn), lambda j, n, k: (k, n)), TN, 0)],
               out_shape=[jax.ShapeDtypeStruct((nc, ms, N), BF16)],
               out_specs=[pl.BlockSpec((None, ms, tn), lambda j, n, k: (j, 0, n))],
               acc_shapes=[(ms, tn)], nk=T // tk, kaxis=2, epilogue=epilogue, comm=comm)


def dw_colshard(name, a, bs, nc, comm=None):
    T, M = a.shape
    ns = bs[0].shape[1] // nc
    tm, tk = _tile(M, 512), _tile(T, 1024, 16)

    def epilogue(vals, ex, outs):
        for v, o in zip(vals, outs):
            o[...] = v.astype(BF16)

    a_spec = pl.BlockSpec((tk, tm), lambda j, m, k: (k, m))
    b_spec = pl.BlockSpec((tk, ns), lambda j, m, k: (k, j))
    return _mm(name, grid=(nc, M // tm, T // tk),
               pairs=[(a, a_spec, b, b_spec, TN, p) for p, b in enumerate(bs)],
               out_shape=[jax.ShapeDtypeStruct((nc, M, ns), BF16)] * len(bs),
               out_specs=[pl.BlockSpec((None, tm, ns), lambda j, m, k: (j, m, 0))] * len(bs),
               acc_shapes=[(tm, ns)] * len(bs), nk=T // tk, kaxis=2, epilogue=epilogue, comm=comm)


def ffn_dh(da, db, wg3, wu3, comm=None):
    T = da.shape[0]
    nc, D, fs = wg3.shape
    tm, tn = _tile(T, 512, 8), _tile(D, 1024)

    def epilogue(vals, ex, outs):
        outs[0][...] = vals[0]

    a_spec = pl.BlockSpec((tm, fs), lambda i, n, k: (i, k))
    w_spec = pl.BlockSpec((None, tn, fs), lambda i, n, k: (k, n, 0))
    return _mm("ffn_dh", grid=(T // tm, D // tn, nc),
               pairs=[(da, a_spec, wg3, w_spec, NT, 0), (db, a_spec, wu3, w_spec, NT, 0)],
               out_shape=[jax.ShapeDtypeStruct((T, D), F32)],
               out_specs=[pl.BlockSpec((tm, tn), lambda i, n, k: (i, n))],
               acc_shapes=[(tm, tn)], nk=nc, kaxis=2, epilogue=epilogue, comm=comm)[0]


def proj_main(h, w_t, P):
    T, D = h.shape
    tm, tn = _tile(T, 512, 8), _tile(P, 1024)

    def epilogue(vals, ex, outs):
        outs[0][...] = vals[0].astype(BF16)

    return _mm("proj_main", grid=(P // tn, T // tm),
               pairs=[(h, pl.BlockSpec((tm, D), lambda j, i: (i, 0)),
                       w_t, pl.BlockSpec((tn, D), lambda j, i: (j, 0)), NT, 0)],
               out_shape=[jax.ShapeDtypeStruct((T, P), BF16)],
               out_specs=[pl.BlockSpec((tm, tn), lambda j, i: (i, j))],
               acc_shapes=[(tm, tn)], nk=1, kaxis=None, epilogue=epilogue)[0]


def mm_nt_bf16(name, a, w):
    T, K = a.shape
    M = w.shape[0]
    tm, tn = _tile(T, 512, 8), _tile(M, 1024)

    def epilogue(vals, ex, outs):
        outs[0][...] = vals[0].astype(BF16)

    return _mm(name, grid=(T // tm, M // tn),
               pairs=[(a, pl.BlockSpec((tm, K), lambda i, n: (i, 0)),
                       w, pl.BlockSpec((tn, K), lambda i, n: (n, 0)), NT, 0)],
               out_shape=[jax.ShapeDtypeStruct((T, M), BF16)],
               out_specs=[pl.BlockSpec((tm, tn), lambda i, n: (i, n))],
               acc_shapes=[(tm, tn)], nk=1, kaxis=None, epilogue=epilogue)[0]


def proj_dh(dproj, w_t, df, wf_t, comm=None):
    T, P = dproj.shape
    D = w_t.shape[1]
    tm, tn, tk = _tile(T, 512, 8), _tile(D, 2048), _tile(P, 1280)

    def epilogue(vals, ex, outs):
        outs[0][...] = vals[0]

    return _mm("proj_dh", grid=(T // tm, D // tn, P // tk),
               pairs=[(dproj, pl.BlockSpec((tm, tk), lambda i, n, k: (i, k)),
                       w_t, pl.BlockSpec((tk, tn), lambda i, n, k: (k, n)), NN, 0)],
               once_pairs=[(df, pl.BlockSpec((tm, LANES), lambda i, n, k: (i, 0)),
                            wf_t, pl.BlockSpec((LANES, tn), lambda i, n, k: (0, n)), NN, 0)],
               out_shape=[jax.ShapeDtypeStruct((T, D), F32)],
               out_specs=[pl.BlockSpec((tm, tn), lambda i, n, k: (i, n))],
               acc_shapes=[(tm, tn)], nk=P // tk, kaxis=2, epilogue=epilogue, comm=comm)[0]


def proj_dw(dproj, df, h, rows):
    T, P = dproj.shape
    D = h.shape[1]
    tm, tn, tk = _tile(P, 1280), _tile(D, 1024), _tile(T, 1024, 16)

    def to_bf16(vals, ex, outs):
        outs[0][...] = vals[0].astype(BF16)

    def to_f32(vals, ex, outs):
        outs[0][...] = vals[0]

    main = _mm("proj_dw_main", grid=(P // tm, D // tn, T // tk),
               pairs=[(dproj, pl.BlockSpec((tk, tm), lambda m, n, k: (k, m)),
                       h, pl.BlockSpec((tk, tn), lambda m, n, k: (k, n)), TN, 0)],
               out_shape=[jax.ShapeDtypeStruct((rows, D), BF16)],
               out_specs=[pl.BlockSpec((tm, tn), lambda m, n, k: (m, n))],
               acc_shapes=[(tm, tn)], nk=T // tk, kaxis=2, epilogue=to_bf16)[0]
    gate = _mm("proj_dw_f", grid=(1, D // tn, T // tk),
               pairs=[(df, pl.BlockSpec((tk, LANES), lambda m, n, k: (k, 0)),
                       h, pl.BlockSpec((tk, tn), lambda m, n, k: (k, n)), TN, 0)],
               out_shape=[jax.ShapeDtypeStruct((LANES, D), F32)],
               out_specs=[pl.BlockSpec((LANES, tn), lambda m, n, k: (0, n))],
               acc_shapes=[(LANES, tn)], nk=T // tk, kaxis=2, epilogue=to_f32)[0]
    return main, gate


def fgate_fwd(h, wf_t, bias, n_heads):
    T, D = h.shape
    tt = _tile(T, 512, 8)

    def body(h_ref, w_ref, b_ref, f_ref, c_ref, carry):
        i = pl.program_id(0)

        @pl.when(i == 0)
        def _():
            carry[...] = jnp.zeros_like(carry)

        f = lax.dot_general(h_ref[...], w_ref[...], NT, preferred_element_type=F32) + b_ref[...]
        logf = jnp.minimum(f, 0.0) - jnp.log(1.0 + jnp.exp(-jnp.abs(f)))
        tri = (lax.broadcasted_iota(jnp.int32, (tt, tt), 0) >= lax.broadcasted_iota(jnp.int32, (tt, tt), 1))
        cs = jnp.dot(tri.astype(F32), logf, preferred_element_type=F32, precision=lax.Precision.HIGHEST)
        c = cs + carry[...]
        f_ref[...] = f
        c_ref[...] = c
        carry[...] = c[tt - 1:tt, :]

    row = pl.BlockSpec((tt, LANES), lambda i: (i, 0))
    return _pcall(
        body, name="fgate_fwd", grid=(T // tt,),
        in_specs=[pl.BlockSpec((tt, D), lambda i: (i, 0)), pl.BlockSpec((LANES, D), lambda i: (0, 0)),
                  pl.BlockSpec((1, LANES), lambda i: (0, 0))],
        out_specs=[row, row],
        out_shape=[jax.ShapeDtypeStruct((T, LANES), F32)] * 2,
        scratch=[pltpu.VMEM((1, LANES), F32)],
    )(h, wf_t, bias)


def fgate_bwd(dc, f, n_heads):
    T = dc.shape[0]
    tt = _tile(T, 512, 8)
    nt = T // tt

    def body(dc_ref, f_ref, df_ref, db_ref, carry):
        i = pl.program_id(0)

        @pl.when(i == 0)
        def _():
            carry[...] = jnp.zeros_like(carry)

        tri = (lax.broadcasted_iota(jnp.int32, (tt, tt), 1) >= lax.broadcasted_iota(jnp.int32, (tt, tt), 0))
        rs = jnp.dot(tri.astype(F32), dc_ref[...], preferred_element_type=F32,
                     precision=lax.Precision.HIGHEST) + carry[...]
        carry[...] = rs[0:1, :]
        lane = lax.broadcasted_iota(jnp.int32, (tt, LANES), 1)
        df = jnp.where(lane < n_heads, rs * _sigmoid(-f_ref[...]), 0.0)
        df_ref[...] = df.astype(BF16)
        part = jnp.sum(df, axis=0, keepdims=True)

        @pl.when(i == 0)
        def _():
            db_ref[...] = part

        @pl.when(i > 0)
        def _():
            db_ref[...] += part

    rev = pl.BlockSpec((tt, LANES), lambda i: (nt - 1 - i, 0))
    return _pcall(
        body, name="fgate_bwd", grid=(nt,),
        in_specs=[rev, rev],
        out_specs=[rev, pl.BlockSpec((1, LANES), lambda i: (0, 0))],
        out_shape=[jax.ShapeDtypeStruct((T, LANES), BF16), jax.ShapeDtypeStruct((1, LANES), F32)],
        scratch=[pltpu.VMEM((1, LANES), F32)],
    )(dc, f)


SUBLANES = 8
SHIFT_ROWS = HALO - SUBLANES


def _shifted_copies(buf, sh, tt):
    for r in range(1, SUBLANES):
        sh[r - 1, 0:tt + SHIFT_ROWS, :] = buf[pl.ds(r, tt + SHIFT_ROWS), :]


def _tap(buf, sh, offset, tt):
    q, r = divmod(offset, SUBLANES)
    if r == 0:
        return buf[pl.ds(SUBLANES * q, tt), :]
    return sh[r - 1, pl.ds(SUBLANES * q, tt), :]


def conv_fwd(proj, conv_w, conv_b, ln_g, ln_b):
    T = proj.shape[0]
    C = conv_w.shape[1]
    tt = _tile(T, 256, HALO)
    hb = tt // HALO

    def body(a_ref, g_ref, ah_ref, gh_ref, w_ref, cb_ref, lg_ref, lb_ref, ypre_ref, y_ref, ubuf, ush):
        i = pl.program_id(0)
        u = a_ref[...].astype(F32) * _sigmoid(g_ref[...].astype(F32))
        uh = ah_ref[...].astype(F32) * _sigmoid(gh_ref[...].astype(F32))
        ubuf[0:HALO, :] = jnp.where(i == 0, 0.0, uh)
        ubuf[HALO:HALO + tt, :] = u
        _shifted_copies(ubuf, ush, tt)
        acc = jnp.broadcast_to(cb_ref[...], (tt, C))
        for k in range(CONV_K):
            acc = acc + w_ref[k:k + 1, :] * _tap(ubuf, ush, HALO - (CONV_K - 1) + k, tt)
        ypre_ref[...] = acc
        mu = jnp.mean(acc, axis=-1, keepdims=True)
        d = acc - mu
        rstd = lax.rsqrt(jnp.mean(d * d, axis=-1, keepdims=True) + LN_EPS)
        z = d * rstd * lg_ref[...] + lb_ref[...]
        y_ref[...] = (z * _sigmoid(z)).astype(BF16)

    vec = pl.BlockSpec((1, C), lambda i: (0, 0))
    return _pcall(
        body, name="conv_fwd", grid=(T // tt,),
        in_specs=[pl.BlockSpec((tt, C), lambda i: (i, 0)), pl.BlockSpec((tt, C), lambda i: (i, 1)),
                  pl.BlockSpec((HALO, C), lambda i: (jnp.maximum(i * hb - 1, 0), 0)),
                  pl.BlockSpec((HALO, C), lambda i: (jnp.maximum(i * hb - 1, 0), 1)),
                  pl.BlockSpec((HALO, C), lambda i: (0, 0)), vec, vec, vec],
        out_specs=[pl.BlockSpec((tt, C), lambda i: (i, 0))] * 2,
        out_shape=[jax.ShapeDtypeStruct((T, C), F32), jax.ShapeDtypeStruct((T, C), BF16)],
        scratch=[pltpu.VMEM((tt + HALO, C), F32), pltpu.VMEM((SUBLANES - 1, tt + SHIFT_ROWS, C), F32)],
    )(proj, proj, proj, proj, conv_w, conv_b, ln_g, ln_b)


def conv_bwd(proj, ypre, dycat, conv_w, ln_g, ln_b):
    T = proj.shape[0]
    C = conv_w.shape[1]
    tt = _tile(T, 256, HALO)
    hb = tt // HALO
    nt = T // tt
    last_h = T // HALO - 1

    def ln_bwd(ypre_v, dout_v, lg, lb):
        mu = jnp.mean(ypre_v, axis=-1, keepdims=True)
        d = ypre_v - mu
        rstd = lax.rsqrt(jnp.mean(d * d, axis=-1, keepdims=True) + LN_EPS)
        yh = d * rstd
        z = yh * lg + lb
        s = _sigmoid(z)
        dz = dout_v * (s * (1.0 + z * (1.0 - s)))
        dyh = dz * lg
        dy = rstd * (dyh - jnp.mean(dyh, axis=-1, keepdims=True)
                     - yh * jnp.mean(dyh * yh, axis=-1, keepdims=True))
        return dy, dz, yh

    def body(a_ref, g_ref, ah_ref, gh_ref, yp_ref, ypn_ref, do_ref, don_ref, w_ref, lg_ref, lb_ref,
             dag_ref, dw_ref, dcb_ref, dlg_ref, dlb_ref, ubuf, dybuf, ush, dysh):
        i = pl.program_id(0)
        av = a_ref[...].astype(F32)
        sg = _sigmoid(g_ref[...].astype(F32))
        uh = ah_ref[...].astype(F32) * _sigmoid(gh_ref[...].astype(F32))
        ubuf[0:HALO, :] = jnp.where(i == 0, 0.0, uh)
        ubuf[HALO:HALO + tt, :] = av * sg
        lg, lb = lg_ref[...], lb_ref[...]
        dy, dz, yh = ln_bwd(yp_ref[...], do_ref[...].astype(F32), lg, lb)
        dyn, _, _ = ln_bwd(ypn_ref[...], don_ref[...].astype(F32), lg, lb)
        dybuf[0:tt, :] = dy
        dybuf[tt:tt + HALO, :] = jnp.where(i == nt - 1, 0.0, dyn)
        _shifted_copies(ubuf, ush, tt)
        _shifted_copies(dybuf, dysh, tt)

        @pl.when(i == 0)
        def _():
            dw_ref[...] = jnp.zeros_like(dw_ref)
            dcb_ref[...] = jnp.zeros_like(dcb_ref)
            dlg_ref[...] = jnp.zeros_like(dlg_ref)
            dlb_ref[...] = jnp.zeros_like(dlb_ref)

        du = jnp.zeros((tt, C), F32)
        for k in range(CONV_K):
            du = du + w_ref[k:k + 1, :] * _tap(dybuf, dysh, CONV_K - 1 - k, tt)
            dw_ref[k:k + 1, :] += jnp.sum(dy * _tap(ubuf, ush, HALO - (CONV_K - 1) + k, tt), axis=0, keepdims=True)
        dcb_ref[...] += jnp.sum(dy, axis=0, keepdims=True)
        dlg_ref[...] += jnp.sum(dz * yh, axis=0, keepdims=True)
        dlb_ref[...] += jnp.sum(dz, axis=0, keepdims=True)

        dag_ref[:, 0:C] = (du * sg).astype(BF16)
        dag_ref[:, C:2 * C] = (du * av * sg * (1.0 - sg)).astype(BF16)

    vec = pl.BlockSpec((1, C), lambda i: (0, 0))
    prev_h = lambda col: pl.BlockSpec((HALO, C), lambda i: (jnp.maximum(i * hb - 1, 0), col))
    next_h = pl.BlockSpec((HALO, C), lambda i: (jnp.minimum((i + 1) * hb, last_h), 0))
    return _pcall(
        body, name="conv_bwd", grid=(nt,),
        in_specs=[pl.BlockSpec((tt, C), lambda i: (i, 0)), pl.BlockSpec((tt, C), lambda i: (i, 1)),
                  prev_h(0), prev_h(1),
                  pl.BlockSpec((tt, C), lambda i: (i, 0)), next_h,
                  pl.BlockSpec((tt, C), lambda i: (i, 0)), next_h,
                  pl.BlockSpec((HALO, C), lambda i: (0, 0)), vec, vec],
        out_specs=[pl.BlockSpec((tt, 2 * C), lambda i: (i, 0)), pl.BlockSpec((HALO, C), lambda i: (0, 0)),
                   vec, vec, vec],
        out_shape=[jax.ShapeDtypeStruct((T, 2 * C), BF16), jax.ShapeDtypeStruct((HALO, C), F32),
                   jax.ShapeDtypeStruct((1, C), F32), jax.ShapeDtypeStruct((1, C), F32),
                   jax.ShapeDtypeStruct((1, C), F32)],
        scratch=[pltpu.VMEM((tt + HALO, C), F32), pltpu.VMEM((tt + HALO, C), F32),
                 pltpu.VMEM((SUBLANES - 1, tt + SHIFT_ROWS, C), F32),
                 pltpu.VMEM((SUBLANES - 1, tt + SHIFT_ROWS, C), F32)],
    )(proj, proj, proj, proj, ypre, ypre, dycat, dycat, conv_w, ln_g, ln_b)


PAIR = LANES // HEAD_DIM


def _head_masks(rows):
    lane = lax.broadcasted_iota(jnp.int32, (rows, LANES), 1)
    return [jnp.logical_and(lane >= hh * HEAD_DIM, lane < (hh + 1) * HEAD_DIM) for hh in range(PAIR)]


def _causal(tq, tk):
    return lax.broadcasted_iota(jnp.int32, (tq, tk), 0) >= lax.broadcasted_iota(jnp.int32, (tq, tk), 1)


def _lane_column(block, lane_index):
    lane = lax.broadcasted_iota(jnp.int32, block.shape, 1)
    return jnp.sum(jnp.where(lane == lane_index, block, 0.0), axis=-1, keepdims=True)


def attn_fwd(proj, cum, ck4, q_col, comm=None):
    T = proj.shape[0]
    H, nkv, _, tk = ck4.shape
    tq = tk
    hd = H * HEAD_DIM
    qb, kb, vb = q_col // LANES, (q_col + hd) // LANES, (q_col + 2 * hd) // LANES
    scale = 1.0 / math.sqrt(HEAD_DIM)

    def body(q_ref, k_ref, v_ref, cum_ref, ck_ref, o_ref, lse_ref):
        hp = pl.program_id(0)
        i = pl.program_id(1)
        masks = _head_masks(tq)
        q2 = q_ref[...] * scale
        qs = [jnp.where(mk, q2, jnp.zeros_like(q2)) for mk in masks]
        cqs = [_lane_column(cum_ref[...], PAIR * hp + hh) for hh in range(PAIR)]

        def step(j, carry, diagonal):
            off = pl.multiple_of(j * tk, tk)
            kj = k_ref[pl.ds(off, tk), :]
            vj = v_ref[pl.ds(off, tk), :]
            out = []
            for hh in range(PAIR):
                m, l, acc = carry[hh]
                s = lax.dot_general(qs[hh], kj, NT, preferred_element_type=F32)
                s = s + cqs[hh] - ck_ref[hh, j]
                if diagonal:
                    s = jnp.where(_causal(tq, tk), s, NEG_INF)
                m_new = jnp.maximum(m, jnp.max(s, axis=-1, keepdims=True))
                alpha = jnp.exp(m - m_new)
                p = jnp.exp(s - m_new)
                l = alpha * l + jnp.sum(p, axis=-1, keepdims=True)
                acc = alpha * acc + jnp.dot(p.astype(BF16), vj, preferred_element_type=F32)
                out.append((m_new, l, acc))
            return tuple(out)

        init = tuple((jnp.full((tq, 1), -jnp.inf, F32), jnp.zeros((tq, 1), F32), jnp.zeros((tq, LANES), F32))
                     for _ in range(PAIR))
        carry = lax.fori_loop(0, i, functools.partial(step, diagonal=False), init)
        carry = step(i, carry, True)
        o = carry[PAIR - 1][2] / carry[PAIR - 1][1]
        for hh in range(PAIR - 1):
            o = jnp.where(masks[hh], carry[hh][2] / carry[hh][1], o)
        o_ref[...] = o
        lse = jnp.broadcast_to(carry[PAIR - 1][0] + jnp.log(carry[PAIR - 1][1]), (tq, LANES))
        for hh in range(PAIR - 1):
            lse = jnp.where(masks[hh], carry[hh][0] + jnp.log(carry[hh][1]), lse)
        lse_ref[...] = lse

    return _pcall(
        body, name="attn_fwd", grid=(H // PAIR, T // tq),
        in_specs=[pl.BlockSpec((tq, LANES), lambda hp, i: (i, qb + hp)),
                  pl.BlockSpec((T, LANES), lambda hp, i: (0, kb + hp)),
                  pl.BlockSpec((T, LANES), lambda hp, i: (0, vb + hp)),
                  pl.BlockSpec((tq, LANES), lambda hp, i: (i, 0)),
                  pl.BlockSpec((PAIR, nkv, 1, tk), lambda hp, i: (hp, 0, 0, 0))],
        out_specs=[pl.BlockSpec((tq, LANES), lambda hp, i: (i, hp)),
                   pl.BlockSpec((None, tq, LANES), lambda hp, i: (hp, i, 0))],
        out_shape=[jax.ShapeDtypeStruct((T, hd), F32), jax.ShapeDtypeStruct((H // PAIR, T, LANES), F32)],
        comm=comm,
    )(proj, proj, proj, cum, ck4)


def attn_bwd(proj, o, dycat, lse, cum, ck4, q_col, do_col, comm=None):
    T = proj.shape[0]
    H, nkv, _, tk = ck4.shape
    tq = tk
    nq = T // tq
    hd = H * HEAD_DIM
    qb, kb, vb = q_col // LANES, (q_col + hd) // LANES, (q_col + 2 * hd) // LANES
    dob = do_col // LANES
    scale = 1.0 / math.sqrt(HEAD_DIM)

    def body(q_ref, k_ref, v_ref, o_ref, do_ref, lse_ref, cum_ref, ck_ref,
             dq_ref, dk_ref, dv_ref, dcq_ref, dck_ref):
        hp = pl.program_id(0)
        j = pl.program_id(1)

        def block(i, diagonal):
            masks = _head_masks(tq)
            rows = pl.ds(pl.multiple_of(i * tq, tq), tq)
            q2, k2, v2, do2 = q_ref[rows, :] * scale, k_ref[...], v_ref[...], do_ref[rows, :]
            zero = jnp.zeros_like(q2)
            prod = do2.astype(F32) * o_ref[rows, :]
            cum_q, lse_q = cum_ref[rows, :], lse_ref[rows, :]
            dq_part = dk_part = dv_part = None
            dcq_part = jnp.zeros((tq, LANES), F32)
            lane = lax.broadcasted_iota(jnp.int32, (tq, LANES), 1)
            for hh in range(PAIR):
                qh = jnp.where(masks[hh], q2, zero)
                kh = jnp.where(masks[hh], k2, zero)
                doh = jnp.where(masks[hh], do2, zero)
                delta = jnp.sum(jnp.where(masks[hh], prod, 0.0), axis=-1, keepdims=True)
                s = lax.dot_general(qh, k2, NT, preferred_element_type=F32)
                s = s + _lane_column(cum_q, PAIR * hp + hh) - ck_ref[hh]
                if diagonal:
                    s = jnp.where(_causal(tq, tk), s, NEG_INF)
                p = jnp.exp(s - _lane_column(lse_q, hh * HEAD_DIM))
                dp = lax.dot_general(doh, v2, NT, preferred_element_type=F32)
                ds = p * (dp - delta)
                dsb = ds.astype(BF16)
                dv_h = lax.dot_general(p.astype(BF16), doh, TN, preferred_element_type=F32)
                dk_h = lax.dot_general(dsb, qh, TN, preferred_element_type=F32)
                dq_h = jnp.dot(dsb, kh, preferred_element_type=F32)
                dq_part = dq_h if dq_part is None else dq_part + dq_h
                dk_part = dk_h if dk_part is None else dk_part + dk_h
                dv_part = dv_h if dv_part is None else dv_part + dv_h
                dck_h = -jnp.sum(ds, axis=0, keepdims=True)
                dcq_part = jnp.where(lane == PAIR * hp + hh, jnp.sum(ds, axis=-1, keepdims=True), dcq_part)
                if diagonal:
                    dck_ref[hh] = dck_h
                else:
                    dck_ref[hh] += dck_h
            dq_part = dq_part * scale

            @pl.when(j == 0)
            def _():
                dq_ref[rows, :] = dq_part

            @pl.when(j > 0)
            def _():
                dq_ref[rows, :] += dq_part

            @pl.when(jnp.logical_and(hp == 0, j == 0))
            def _():
                dcq_ref[rows, :] = dcq_part

            @pl.when(jnp.logical_or(hp > 0, j > 0))
            def _():
                dcq_ref[rows, :] += dcq_part

            if diagonal:
                dk_ref[...] = dk_part
                dv_ref[...] = dv_part
            else:
                dk_ref[...] += dk_part
                dv_ref[...] += dv_part

        block(j, True)

        def later(i, carry):
            block(i, False)
            return carry

        lax.fori_loop(j + 1, nq, later, 0)

    at_q = lambda col: pl.BlockSpec((T, LANES), lambda hp, j: (0, col + hp))
    at_k = lambda col: pl.BlockSpec((tk, LANES), lambda hp, j: (j, col + hp))
    lse_spec = pl.BlockSpec((None, T, LANES), lambda hp, j: (hp, 0, 0))
    cum_spec = pl.BlockSpec((T, LANES), lambda hp, j: (0, 0))
    ck_spec = pl.BlockSpec((PAIR, None, 1, tk), lambda hp, j: (hp, j, 0, 0))
    return _pcall(
        body, name="attn_bwd", grid=(H // PAIR, nkv),
        in_specs=[at_q(qb), at_k(kb), at_k(vb), at_q(0), at_q(dob), lse_spec, cum_spec, ck_spec],
        out_specs=[pl.BlockSpec((T, LANES), lambda hp, j: (0, hp)), at_k(0), at_k(0),
                   pl.BlockSpec((T, LANES), lambda hp, j: (0, 0)), ck_spec],
        out_shape=[jax.ShapeDtypeStruct((T, hd), F32)] * 3
        + [jax.ShapeDtypeStruct((T, LANES), F32), jax.ShapeDtypeStruct((H, nkv, 1, tk), F32)],
        comm=comm,
    )(proj, proj, proj, o, dycat, lse, cum, ck4)


ELEMENTWISE_BLOCK_BYTES = 2 * 1024 * 1024
BF16_ROWS = 16


def cast_bf16(arrays, comm=None):
    def slab(a, steps):
        R, C = a.shape
        if R % (steps * BF16_ROWS) == 0:
            return pl.BlockSpec((R // steps, C), lambda i: (i, 0))
        if C % (steps * LANES) == 0:
            return pl.BlockSpec((R, C // steps), lambda i: (0, i))
        return None

    steps = 8 if all(slab(a, 8) is not None for a in arrays) else 4
    specs = [slab(a, steps) for a in arrays]
    n = len(arrays)

    def body(*refs):
        for src, dst in zip(refs[:n], refs[n:]):
            dst[...] = src[...].astype(BF16)

    return _pcall(body, name="cast_bf16", grid=(steps,), in_specs=specs, out_specs=specs,
                  out_shape=[jax.ShapeDtypeStruct(a.shape, BF16) for a in arrays], comm=comm)(*arrays)


def _ew_tiles(rows, cols, bytes_per_element):
    target = max(8, ELEMENTWISE_BLOCK_BYTES // max(1, cols * bytes_per_element))
    if rows <= target:
        return rows, cols
    t = (target // 16) * 16
    while t >= 16:
        if rows % t == 0:
            return t, cols
        t -= 16
    tc = _tile(cols, max(LANES, (ELEMENTWISE_BLOCK_BYTES // (rows * bytes_per_element)) // LANES * LANES))
    return rows, tc


def sum_chips(recv):
    nc, R, C = recv.shape
    tr, tc = _ew_tiles(R, C, 4)

    def body(r_ref, o_ref):
        acc = r_ref[0].astype(F32)
        for j in range(1, nc):
            acc = acc + r_ref[j].astype(F32)
        o_ref[...] = acc

    return _pcall(
        body, name="sum_chips", grid=(R // tr, C // tc),
        in_specs=[pl.BlockSpec((nc, tr, tc), lambda i, j: (0, i, j))],
        out_specs=[pl.BlockSpec((tr, tc), lambda i, j: (i, j))],
        out_shape=[jax.ShapeDtypeStruct((R, C), F32)],
    )(recv)[0]


def add_sibling_half(g, recv):
    nc, R, C = g.shape
    hr = R // 2
    tr, tc = _ew_tiles(hr, C, 4 * nc)

    def body(g_ref, r_ref, o_ref):
        c = lax.axis_index("c")
        for j in range(nc):
            o_ref[j] = (g_ref[j, c].astype(F32) + r_ref[j].astype(F32)).astype(BF16)

    return _pcall(
        body, name="add_sibling_half", grid=(hr // tr, C // tc),
        in_specs=[pl.BlockSpec((nc, 2, tr, tc), lambda i, j: (0, 0, i, j)),
                  pl.BlockSpec((nc, tr, tc), lambda i, j: (0, i, j))],
        out_specs=[pl.BlockSpec((nc, tr, tc), lambda i, j: (0, i, j))],
        out_shape=[jax.ShapeDtypeStruct((nc, hr, C), BF16)],
    )(g.reshape(nc, 2, hr, C), recv)[0]


def adamw(w, m, v, g_parts, comm=None, halves=False):
    R, C = w.shape
    tr, tc = _ew_tiles(R // 2 if halves else R, C, 4 * 4)
    n_g = len(g_parts)
    n_half = (R // 2) // tr
    c1 = 1.0 - ADAM_B1
    c2 = 1.0 - ADAM_B2
    bc1 = 1.0 - ADAM_B1 ** ADAM_STEP
    bc2 = 1.0 - ADAM_B2 ** ADAM_STEP

    def body(*refs):
        w_ref, m_ref, v_ref = refs[:3]
        g_refs = refs[3:3 + n_g]
        g_out, d_out, m_out, v_out = refs[3 + n_g:]
        if halves:
            mine = (pl.program_id(0) >= n_half) == (lax.axis_index("c") == 1)
            g = jnp.where(mine, g_refs[0][...], g_refs[1][...])
        else:
            g = g_refs[0][...]
            for r in g_refs[1:]:
                g = g + r[...]
        m_new = ADAM_B1 * m_ref[...] + c1 * g
        v_new = ADAM_B2 * v_ref[...] + c2 * (g * g)
        m_hat = m_new / bc1
        v_hat = v_new / bc2
        g_out[...] = g
        d_out[...] = -ADAM_LR * (m_hat / (jnp.sqrt(v_hat) + ADAM_EPS) + ADAM_WD * w_ref[...])
        m_out[...] = m_new
        v_out[...] = v_new

    spec = pl.BlockSpec((tr, tc), lambda i, j: (i, j))
    g_spec = pl.BlockSpec((tr, tc), lambda i, j: (i % n_half, j)) if halves else spec
    return _pcall(
        body, name="adamw", grid=(R // tr, C // tc),
        in_specs=[spec] * 3 + [g_spec] * n_g, out_specs=[spec] * 4,
        out_shape=[jax.ShapeDtypeStruct((R, C), F32)] * 4, comm=comm,
    )(w, m, v, *g_parts)


def _chip_coords():
    x, y, c = lax.axis_index("x"), lax.axis_index("y"), lax.axis_index("c")
    others = [(1 - x, y), (x, 1 - y), (1 - x, 1 - y)]
    return x, y, c, others


def _remote(src, dst, send_sem, recv_sem, device):
    return pltpu.make_async_remote_copy(src_ref=src, dst_ref=dst, send_sem=send_sem, recv_sem=recv_sem,
                                        device_id=device, device_id_type=MESH)


def gather_comm(shards):
    n = len(shards)
    pieces = [(a, jj) for a in range(n) for jj in range(3)]

    def makers(ins, outs, sems):
        send_sems, recv_sems, local_sems = sems
        x, y, c, others = _chip_coords()
        me = 2 * x + y
        sibling = (x, y, 1 - c)

        def half(ref, a, which, chip=None):
            rows, cols = ins[a].shape[0], ins[a].shape[1]
            lead = () if chip is None else (chip,)
            if rows % (2 * BF16_ROWS) == 0:
                return ref.at[(*lead, pl.ds(which * (rows // 2), rows // 2))]
            return ref.at[(*lead, slice(None), pl.ds(which * (cols // 2), cols // 2))]

        def local(a):
            return pltpu.make_async_copy(ins[a], outs[a].at[me], local_sems.at[a])

        def ici(a, jj):
            ox, oy = others[jj]
            return _remote(half(ins[a], a, c), half(outs[a], a, c, me),
                           send_sems.at[6 * a + jj], recv_sems.at[6 * a + jj], (ox, oy, c))

        def landed(a, jj):
            ox, oy = others[jj]
            got = half(outs[a], a, c, 2 * ox + oy)
            return (_remote(got, got, send_sems.at[6 * a + jj], recv_sems.at[6 * a + jj], (ox, oy, c)),
                    _remote(got, got, send_sems.at[6 * a + 3 + jj], recv_sems.at[6 * a + 3 + jj], sibling))

        def theirs(a, jj):
            ox, oy = others[jj]
            sib = half(outs[a], a, 1 - c, 2 * ox + oy)
            return _remote(sib, sib, send_sems.at[6 * a + 3 + jj], recv_sems.at[6 * a + 3 + jj], sibling)

        return local, ici, landed, theirs

    def start(ins, outs, sems):
        local, ici, _, _ = makers(ins, outs, sems)
        for a in range(n):
            for jj in (2, 0, 1):
                ici(a, jj).start()
        for a in range(n):
            local(a).start()

    def finish(ins, outs, sems):
        local, ici, landed, theirs = makers(ins, outs, sems)
        forwards = []
        for a, jj in pieces:
            got, fwd = landed(a, jj)
            got.wait_recv()
            fwd.start()
            forwards.append(fwd)
        for a, jj in pieces:
            theirs(a, jj).wait_recv()
        for a, jj in pieces:
            ici(a, jj).wait_send()
        for fwd in forwards:
            fwd.wait_send()
        for a in range(n):
            local(a).wait()

    return Comm(shards, [jax.ShapeDtypeStruct((N_CHIP,) + s.shape, s.dtype) for s in shards],
                [pltpu.SemaphoreType.DMA((6 * n,)), pltpu.SemaphoreType.DMA((6 * n,)),
                 pltpu.SemaphoreType.DMA((n,))], start, finish)


def scatter_comm(grads):
    n = len(grads)
    pieces = [(a, jj) for a in range(n) for jj in range(3)]

    def makers(ins, outs, sems):
        send_sems, recv_sems, local_sems = sems
        x, y, c, others = _chip_coords()
        me = 2 * x + y

        def local(a):
            return pltpu.make_async_copy(ins[a].at[me], outs[a].at[me], local_sems.at[a])

        def ici(a, jj):
            ox, oy = others[jj]
            return _remote(ins[a].at[2 * ox + oy], outs[a].at[me], send_sems.at[3 * a + jj],
                           recv_sems.at[3 * a + jj], (ox, oy, c))

        def landed(a, jj):
            ox, oy = others[jj]
            slot = outs[a].at[2 * ox + oy]
            return _remote(slot, slot, send_sems.at[3 * a + jj], recv_sems.at[3 * a + jj], (ox, oy, c))

        return local, ici, landed

    def start(ins, outs, sems):
        local, ici, _ = makers(ins, outs, sems)
        for a in range(n):
            for jj in (2, 0, 1):
                ici(a, jj).start()
        for a in range(n):
            local(a).start()

    def finish(ins, outs, sems):
        local, ici, landed = makers(ins, outs, sems)
        for a, jj in pieces:
            landed(a, jj).wait_recv()
        for a, jj in pieces:
            ici(a, jj).wait_send()
        for a in range(n):
            local(a).wait()

    return Comm(grads, [jax.ShapeDtypeStruct(g.shape, g.dtype) for g in grads],
                [pltpu.SemaphoreType.DMA((3 * n,)), pltpu.SemaphoreType.DMA((3 * n,)),
                 pltpu.SemaphoreType.DMA((n,))], start, finish)


def halfswap_comm(grads):
    n = len(grads)

    def copies(ins, outs, sems):
        send_sems, recv_sems = sems
        x, y, c, _ = _chip_coords()
        out = []
        for a in range(n):
            hr = ins[a].shape[1] // 2
            out.append(_remote(ins[a].at[:, pl.ds((1 - c) * hr, hr)], outs[a], send_sems.at[a], recv_sems.at[a],
                               (x, y, 1 - c)))
        return out

    def start(ins, outs, sems):
        for cp in copies(ins, outs, sems):
            cp.start()

    def finish(ins, outs, sems):
        for cp in copies(ins, outs, sems):
            cp.wait()

    return Comm(grads, [jax.ShapeDtypeStruct((g.shape[0], g.shape[1] // 2, g.shape[2]), g.dtype) for g in grads],
                [pltpu.SemaphoreType.DMA((n,)), pltpu.SemaphoreType.DMA((n,))], start, finish)


def join_comms(first, second):
    ni, no, ns = len(first.operands), len(first.out_shape), len(first.sems)

    def start(ins, outs, sems):
        first.start(ins[:ni], outs[:no], sems[:ns])
        second.start(ins[ni:], outs[no:], sems[ns:])

    def finish(ins, outs, sems):
        first.finish(ins[:ni], outs[:no], sems[:ns])
        second.finish(ins[ni:], outs[no:], sems[ns:])

    return Comm(first.operands + second.operands, first.out_shape + second.out_shape, first.sems + second.sems,
                start, finish)


def swap_comm(parts):
    n = len(parts)

    def copies(ins, outs, sems):
        send_sems, recv_sems = sems
        x, y, c, _ = _chip_coords()
        return [_remote(ins[a], outs[a], send_sems.at[a], recv_sems.at[a], (x, y, 1 - c)) for a in range(n)]

    def start(ins, outs, sems):
        for cp in copies(ins, outs, sems):
            cp.start()

    def finish(ins, outs, sems):
        for cp in copies(ins, outs, sems):
            cp.wait()

    return Comm(parts, [jax.ShapeDtypeStruct(p.shape, p.dtype) for p in parts],
                [pltpu.SemaphoreType.DMA((n,)), pltpu.SemaphoreType.DMA((n,))], start, finish)


def allreduce_small(v):
    R = v.shape[0]

    def body(v_ref, sum_ref, all_ref, send_sems, recv_sems):
        x, y, c = lax.axis_index("x"), lax.axis_index("y"), lax.axis_index("c")
        me = 4 * x + 2 * y + c
        all_ref[me] = v_ref[...]
        copies = []
        for k in range(1, N_DEV):
            px = 1 - x if k & 4 else x
            py = 1 - y if k & 2 else y
            pc = 1 - c if k & 1 else c
            cp = pltpu.make_async_remote_copy(
                src_ref=v_ref, dst_ref=all_ref.at[me], send_sem=send_sems.at[k - 1], recv_sem=recv_sems.at[k - 1],
                device_id=(px, py, pc), device_id_type=MESH)
            cp.start()
            copies.append((cp, 4 * px + 2 * py + pc))
        for k, (cp, peer) in enumerate(copies):
            pltpu.make_async_remote_copy(
                src_ref=v_ref, dst_ref=all_ref.at[peer], send_sem=send_sems.at[k], recv_sem=recv_sems.at[k],
                device_id=(x, y, c), device_id_type=MESH).wait_recv()
        for cp, _ in copies:
            cp.wait_send()
        acc = all_ref[0]
        for d in range(1, N_DEV):
            acc = acc + all_ref[d]
        sum_ref[...] = acc

    vm = pl.BlockSpec(memory_space=pltpu.VMEM)
    return pl.pallas_call(
        body, name="allreduce_small",
        in_specs=[vm], out_specs=[vm, vm],
        out_shape=[jax.ShapeDtypeStruct((R, LANES), F32), jax.ShapeDtypeStruct((N_DEV, R, LANES), F32)],
        scratch_shapes=[pltpu.SemaphoreType.DMA((N_DEV - 1,)), pltpu.SemaphoreType.DMA((N_DEV - 1,))],
    )(v)[0]


SMALL_NAMES = ("ffn1_norm", "mix_norm", "ffn2_norm", "final_norm", "conv_b", "conv_ln_g", "conv_ln_b")


def _pack_small(vecs, bias, conv_w_rows, loss_tile):
    rows = [vecs[n].reshape(-1, LANES) for n in SMALL_NAMES]
    rows.append(bias.reshape(1, LANES))
    rows.append(conv_w_rows.reshape(-1, LANES))
    rows.append(loss_tile[0:1, :])
    packed = jnp.concatenate(rows, axis=0)
    pad = (-packed.shape[0]) % 8
    return jnp.pad(packed, ((0, pad), (0, 0)))


def _unpack_small(packed, sizes, n_conv_rows):
    out, r = {}, 0
    for n in SMALL_NAMES:
        k = sizes[n] // LANES
        out[n] = packed[r:r + k].reshape(-1)
        r += k
    out["fgate_bias"] = packed[r]
    r += 1
    out["conv_w"] = packed[r:r + n_conv_rows]
    r += n_conv_rows
    out["loss"] = packed[r, 0]
    return out


def kernel(x, ffn1_norm, ffn1_w_gate, ffn1_w_up, ffn1_w_down, mix_norm, w_in, fgate_bias, conv_w, conv_b, conv_ln_g, conv_ln_b, w_out, ffn2_norm, ffn2_w_gate, ffn2_w_up, ffn2_w_down, final_norm, loss_target, m_ffn1_norm, m_ffn1_w_gate, m_ffn1_w_up, m_ffn1_w_down, m_mix_norm, m_w_in, m_fgate_bias, m_conv_w, m_conv_b, m_conv_ln_g, m_conv_ln_b, m_w_out, m_ffn2_norm, m_ffn2_w_gate, m_ffn2_w_up, m_ffn2_w_down, m_final_norm, v_ffn1_norm, v_ffn1_w_gate, v_ffn1_w_up, v_ffn1_w_down, v_mix_norm, v_w_in, v_fgate_bias, v_conv_w, v_conv_b, v_conv_ln_g, v_conv_ln_b, v_w_out, v_ffn2_norm, v_ffn2_w_gate, v_ffn2_w_up, v_ffn2_w_down, v_final_norm):
    w = dict(ffn1_norm=ffn1_norm, ffn1_w_gate=ffn1_w_gate, ffn1_w_up=ffn1_w_up, ffn1_w_down=ffn1_w_down,
             mix_norm=mix_norm, w_in=w_in, fgate_bias=fgate_bias, conv_w=conv_w, conv_b=conv_b,
             conv_ln_g=conv_ln_g, conv_ln_b=conv_ln_b, w_out=w_out, ffn2_norm=ffn2_norm,
             ffn2_w_gate=ffn2_w_gate, ffn2_w_up=ffn2_w_up, ffn2_w_down=ffn2_w_down, final_norm=final_norm)
    m = dict(ffn1_norm=m_ffn1_norm, ffn1_w_gate=m_ffn1_w_gate, ffn1_w_up=m_ffn1_w_up, ffn1_w_down=m_ffn1_w_down,
             mix_norm=m_mix_norm, w_in=m_w_in, fgate_bias=m_fgate_bias, conv_w=m_conv_w, conv_b=m_conv_b,
             conv_ln_g=m_conv_ln_g, conv_ln_b=m_conv_ln_b, w_out=m_w_out, ffn2_norm=m_ffn2_norm,
             ffn2_w_gate=m_ffn2_w_gate, ffn2_w_up=m_ffn2_w_up, ffn2_w_down=m_ffn2_w_down, final_norm=m_final_norm)
    v = dict(ffn1_norm=v_ffn1_norm, ffn1_w_gate=v_ffn1_w_gate, ffn1_w_up=v_ffn1_w_up, ffn1_w_down=v_ffn1_w_down,
             mix_norm=v_mix_norm, w_in=v_w_in, fgate_bias=v_fgate_bias, conv_w=v_conv_w, conv_b=v_conv_b,
             conv_ln_g=v_conv_ln_g, conv_ln_b=v_conv_ln_b, w_out=v_w_out, ffn2_norm=v_ffn2_norm,
             ffn2_w_gate=v_ffn2_w_gate, ffn2_w_up=v_ffn2_w_up, ffn2_w_down=v_ffn2_w_down, final_norm=v_final_norm)
    names = list(w.keys())
    big = ("ffn1_w_gate", "ffn1_w_up", "ffn1_w_down", "w_in", "w_out", "ffn2_w_gate", "ffn2_w_up", "ffn2_w_down")

    T, D = x.shape[1], x.shape[2]
    C = conv_b.shape[0]
    H = fgate_bias.shape[0]
    cs = conv_w.shape[1]
    in_cols = N_CHIP * w_in.shape[1]
    p_main = in_cols - H

    x0, tgt = x[0], loss_target[0]
    tk = _tile(T, 512, 128)
    nkv = T // tk
    row = lambda a: a.reshape(1, -1)
    grad, delta, new_m, new_v = {}, {}, {}, {}

    def update(n, parts, comm=None, halves=False):
        args = (w[n], m[n], v[n])
        if n == "w_in":
            outs = [t.T for t in adamw(*[a.T for a in args], parts, comm=comm)]
        else:
            outs = adamw(*args, parts, comm=comm, halves=halves)
        grad[n], delta[n], new_m[n], new_v[n] = outs

    rest = [n for n in big if n != "ffn1_w_gate"]
    g0 = gather_comm([w["ffn1_w_gate"].astype(BF16), jnp.pad(conv_w, ((0, HALO - CONV_K), (0, 0)))])
    wb = dict(zip(rest, cast_bf16([w[n].T if n == "w_in" else w[n] for n in rest], comm=g0)))
    wg1, conv_w4 = g0.results
    conv_w_full = conv_w4.transpose(1, 0, 2).reshape(HALO, C)
    h1, r1 = rms_fwd(x0, row(ffn1_norm))
    g1a = gather_comm([wb["ffn1_w_up"]])
    a1 = ffn_gate(h1, wg1, comm=g1a)
    wu1 = g1a.results[0]
    g1b = gather_comm([wb["ffn1_w_down"]])
    b1, mid1 = ffn_upmul(h1, wu1, a1, comm=g1b)
    wd1 = g1b.results[0]
    g2 = gather_comm([wb["w_in"]])
    x1 = mm_residual("ffn_down_g", mid1, wd1, x0, 0.5, comm=g2)[0]
    w_t = g2.results[0].reshape(in_cols, D)

    wf_t = jnp.pad(w_t[p_main:], ((0, LANES - H), (0, 0)))
    bias_pad = jnp.pad(row(fgate_bias), ((0, 0), (0, LANES - H)))
    h2, r2 = rms_fwd(x1, row(mix_norm))
    proj = proj_main(h2, w_t, p_main)
    f, cum = fgate_fwd(h2, wf_t, bias_pad, H)
    ypre, yconv = conv_fwd(proj, conv_w_full, row(conv_b), row(conv_ln_g), row(conv_ln_b))
    ck4 = cum[:, :H].T.reshape(H, nkv, 1, tk)
    g3 = gather_comm([wb["w_out"], wb["ffn2_w_gate"], wb["ffn2_w_up"], wb["ffn2_w_down"]])
    o, lse = attn_fwd(proj, cum, ck4, 2 * C, comm=g3)
    w_out3, wg2, wu2, wd2 = g3.results
    ycat = jnp.concatenate([yconv, o.astype(BF16)], axis=1)
    x2 = mm_residual("out_proj", ycat, w_out3.reshape(2, -1, D), x1, 1.0)[0]

    h3, r3 = rms_fwd(x2, row(ffn2_norm))
    a2, b2, mid2 = ffn_up(h3, wg2, wu2)
    x3 = mm_residual("ffn_down", mid2, wd2, x2, 0.5)[0]
    dx3, dx3b, loss_tile, d_final = final_loss(x3, tgt, row(final_norm))

    da2, db2 = ffn_bwd_mid(dx3b, wd2, a2, b2)
    dwd2 = dw_rowshard("ffn_dwd", mid2, dx3b, N_CHIP)[0]
    s1 = scatter_comm([dwd2])
    dwg2, dwu2 = dw_colshard("ffn_dwgu_s", h3, [da2, db2], N_CHIP, comm=s1)
    s2 = scatter_comm([dwg2])
    dh3 = ffn_dh(da2, db2, wg2, wu2, comm=s2)
    dx2, dx2b, d_ffn2_norm = rms_bwd(dh3, x2, r3, row(ffn2_norm), dx3, 1.0)

    dycat = mm_nt_bf16("out_proj_dy", dx2b, w_out3.reshape(-1, D))
    dw_out3 = dw_rowshard("out_proj_dw", ycat, dx2b, N_CHIP)[0]
    s3 = scatter_comm([dwu2, dw_out3])
    dq, dk, dv, dcq, dck4 = attn_bwd(proj, o, dycat, lse, cum, ck4, 2 * C, C, comm=s3)
    dc = dcq + jnp.pad(dck4.reshape(H, T).T, ((0, 0), (0, LANES - H)))
    df, d_bias = fgate_bwd(dc, f, H)
    dag, d_conv_w, d_conv_b, d_ln_g, d_ln_b = conv_bwd(proj, ypre, dycat, conv_w_full, row(conv_ln_g),
                                                       row(conv_ln_b))
    dproj = jnp.concatenate([dag, dq.astype(BF16), dk.astype(BF16), dv.astype(BF16)], axis=1)
    early = ("ffn2_w_down", "ffn2_w_gate", "ffn2_w_up", "w_out")
    early_sums = [sum_chips(r) for r in (s1.results[0], s2.results[0], s3.results[0], s3.results[1])]
    sw1 = swap_comm(early_sums)
    dh2 = proj_dh(dproj, w_t, df, wf_t, comm=sw1)
    dw_t, dwf_t = proj_dw(dproj, df, h2, in_cols)
    dw_t = lax.dynamic_update_slice(dw_t, dwf_t[:H].astype(BF16), (p_main, 0))
    dw_in3 = dw_t.reshape(N_CHIP, in_cols // N_CHIP, D)
    dx1, dx1b, d_mix_norm = rms_bwd(dh2, x1, r2, row(mix_norm), dx2, 0.5)

    s4 = scatter_comm([dw_in3])
    da1, db1 = ffn_bwd_mid(dx1b, wd1, a1, b1, comm=s4)
    dwd1 = dw_rowshard("ffn_dwd", mid1, dx1b, N_CHIP)[0]
    s5 = scatter_comm([dwd1])
    dwg1, dwu1 = dw_colshard("ffn_dwgu_s", h1, [da1, db1], N_CHIP, comm=s5)
    mid_sums = [sum_chips(s4.results[0]), sum_chips(s5.results[0])]
    s6 = join_comms(join_comms(scatter_comm([dwg1]), halfswap_comm([dwu1])), swap_comm(mid_sums))
    dh1 = ffn_dh(da1, db1, wg1, wu1, comm=s6)
    recv_g1, sibling_u1, their_in, their_d1 = s6.results
    grad_x, _, d_ffn1_norm = rms_bwd(dh1, x0, r1, row(ffn1_norm), dx1, 1.0)

    s7 = scatter_comm([add_sibling_half(dwu1, sibling_u1)])
    for i, (n, mine, other) in enumerate(zip(early, early_sums, sw1.results)):
        update(n, [mine, other], comm=s7 if i == 0 else None)
    update("w_in", [mid_sums[0], their_in])
    update("ffn1_w_down", [mid_sums[1], their_d1])
    sum_g1, half_u1 = sum_chips(recv_g1), sum_chips(s7.results[0])
    their_g1, their_u1 = _run_comm("swap_last", swap_comm([sum_g1, half_u1]))
    update("ffn1_w_gate", [sum_g1, their_g1])
    update("ffn1_w_up", [half_u1, their_u1], halves=True)

    gl = dict(ffn1_norm=d_ffn1_norm, mix_norm=d_mix_norm, ffn2_norm=d_ffn2_norm, final_norm=d_final,
              conv_b=d_conv_b, conv_ln_g=d_ln_g, conv_ln_b=d_ln_b)
    small_sizes = {n: w[n].shape[0] for n in SMALL_NAMES}
    packed = _pack_small(gl, d_bias, d_conv_w, loss_tile)
    red = _unpack_small(allreduce_small(packed), small_sizes, HALO * C // LANES)
    loss = red["loss"]
    my_chip = 2 * lax.axis_index("x") + lax.axis_index("y")
    g_conv_w = lax.dynamic_slice_in_dim(red["conv_w"].reshape(HALO, C)[:CONV_K], my_chip * cs, cs, axis=1)
    update("conv_w", [g_conv_w])
    vec_names = SMALL_NAMES + ("fgate_bias",)
    stack = lambda d: jnp.concatenate(
        [jnp.pad(d[n], (0, (-d[n].shape[0]) % LANES)).reshape(-1, LANES) for n in vec_names], axis=0)
    g_stack = jnp.concatenate([red[n].reshape(-1, LANES) for n in SMALL_NAMES] + [red["fgate_bias"][None, :]],
                              axis=0)
    outs = adamw(stack(w), stack(m), stack(v), [g_stack])
    r = 0
    for n in vec_names:
        size = w[n].shape[0]
        k = -(-size // LANES)
        for dst, src in zip((grad, delta, new_m, new_v), outs):
            dst[n] = src[r:r + k].reshape(-1)[:size]
        r += k

    return (loss, grad_x[None], *[grad[n] for n in names], *[delta[n] for n in names],
            *[new_m[n] for n in names], *[new_v[n] for n in names])
```

```python
import functools
import math

import jax
import jax.numpy as jnp
from jax import lax
from jax.experimental import pallas as pl
from jax.experimental.pallas import tpu as pltpu

F32 = jnp.float32
BF16 = jnp.bfloat16
NORM_EPS = 1e-6
LN_EPS = 1e-5
NEG_INF = -1e30
HEAD_DIM = 64
CONV_K = 31
HALO = 32
LANES = 128
N_CHIP = 4
N_DEV = 8
VMEM_LIMIT = 52 * 1024 * 1024
MESH = pl.DeviceIdType.MESH

ADAM_LR = 0.001
ADAM_B1 = 0.9
ADAM_B2 = 0.999
ADAM_EPS = 1e-08
ADAM_WD = 0.01
ADAM_STEP = 10

NN = (((1,), (0,)), ((), ()))
NT = (((1,), (1,)), ((), ()))
TN = (((0,), (0,)), ((), ()))


def _tile(n, pref, unit=128):
    if n <= pref:
        return n
    t = (pref // unit) * unit
    while t > 0:
        if n % t == 0:
            return t
        t -= unit
    raise ValueError(f"no tile for {n} under {pref}")


class Comm:
    def __init__(self, operands, out_shape, sems, start, finish):
        self.operands, self.out_shape, self.sems = list(operands), list(out_shape), list(sems)
        self.start, self.finish = start, finish
        self.results = None


def _pcall(body, *, name, grid, in_specs, out_specs, out_shape, scratch=(), comm=None):
    params = pltpu.CompilerParams(dimension_semantics=("arbitrary",) * len(grid), vmem_limit_bytes=VMEM_LIMIT)
    scratch = list(scratch)
    if comm is None:
        return pl.pallas_call(body, name=name, grid=grid, in_specs=in_specs, out_specs=out_specs,
                              out_shape=out_shape, scratch_shapes=scratch, compiler_params=params)
    n_in, n_out, n_s = len(in_specs), len(out_shape), len(scratch)
    n_ci, n_co = len(comm.operands), len(comm.out_shape)
    any_spec = pl.BlockSpec(memory_space=pl.ANY)

    def carried(*refs):
        ins, refs = refs[:n_in], refs[n_in:]
        c_ins, refs = refs[:n_ci], refs[n_ci:]
        outs, refs = refs[:n_out], refs[n_out:]
        c_outs, refs = refs[:n_co], refs[n_co:]
        scr, c_sems = refs[:n_s], refs[n_s:]
        first = pl.program_id(0) == 0
        last = pl.program_id(0) == grid[0] - 1
        for d in range(1, len(grid)):
            first = jnp.logical_and(first, pl.program_id(d) == 0)
            last = jnp.logical_and(last, pl.program_id(d) == grid[d] - 1)

        @pl.when(first)
        def _():
            comm.start(c_ins, c_outs, c_sems)

        body(*ins, *outs, *scr)

        @pl.when(last)
        def _():
            comm.finish(c_ins, c_outs, c_sems)

    call = pl.pallas_call(
        carried, name=name, grid=grid, in_specs=list(in_specs) + [any_spec] * n_ci,
        out_specs=list(out_specs) + [any_spec] * n_co, out_shape=list(out_shape) + comm.out_shape,
        scratch_shapes=scratch + comm.sems, compiler_params=params)

    def run(*operands):
        res = call(*operands, *comm.operands)
        comm.results = list(res[n_out:])
        return list(res[:n_out])

    return run


def _run_comm(name, comm):
    n_ci, n_co = len(comm.operands), len(comm.out_shape)
    any_spec = pl.BlockSpec(memory_space=pl.ANY)

    def body(*refs):
        c_ins, c_outs, c_sems = refs[:n_ci], refs[n_ci:n_ci + n_co], refs[n_ci + n_co:]
        comm.start(c_ins, c_outs, c_sems)
        comm.finish(c_ins, c_outs, c_sems)

    return pl.pallas_call(body, name=name, in_specs=[any_spec] * n_ci, out_specs=[any_spec] * n_co,
                          out_shape=comm.out_shape, scratch_shapes=comm.sems)(*comm.operands)


def _sigmoid(x):
    return 0.5 * (1.0 + jnp.tanh(0.5 * x))


def _mm(name, *, grid, pairs, once_pairs=(), extra=(), out_shape, out_specs, acc_shapes, nk, kaxis, epilogue,
        comm=None):
    all_pairs = list(pairs) + list(once_pairs)
    n_p, n_o = len(pairs), len(once_pairs)
    n_e, n_out, n_acc = len(extra), len(out_shape), len(acc_shapes)

    def body(*refs):
        ab = refs[: 2 * (n_p + n_o)]
        ex = refs[2 * (n_p + n_o): 2 * (n_p + n_o) + n_e]
        outs = refs[2 * (n_p + n_o) + n_e: 2 * (n_p + n_o) + n_e + n_out]
        accs = refs[2 * (n_p + n_o) + n_e + n_out:]

        def dots(idx_range):
            vals = [None] * n_acc
            for p in idx_range:
                d = lax.dot_general(ab[2 * p][...], ab[2 * p + 1][...], all_pairs[p][4],
                                    preferred_element_type=F32)
                ai = all_pairs[p][5]
                vals[ai] = d if vals[ai] is None else vals[ai] + d
            return vals

        if nk == 1:
            vals = dots(range(n_p + n_o))
            epilogue(vals, ex, outs)
            return

        k = pl.program_id(kaxis)

        @pl.when(k == 0)
        def _():
            vals = dots(range(n_p + n_o))
            for ai in range(n_acc):
                accs[ai][...] = vals[ai]

        @pl.when(k > 0)
        def _():
            vals = dots(range(n_p))
            for ai in range(n_acc):
                if vals[ai] is not None:
                    accs[ai][...] += vals[ai]

        @pl.when(k == nk - 1)
        def _():
            epilogue([a[...] for a in accs], ex, outs)

    operands, in_specs = [], []
    for p in all_pairs:
        operands += [p[0], p[2]]
        in_specs += [p[1], p[3]]
    for arr, spec in extra:
        operands.append(arr)
        in_specs.append(spec)
    scratch = [pltpu.VMEM(s, F32) for s in acc_shapes] if nk > 1 else []
    return _pcall(body, name=name, grid=grid, in_specs=in_specs, out_specs=out_specs, out_shape=out_shape,
                  scratch=scratch, comm=comm)(*operands)


def rms_fwd(x, g):
    T, D = x.shape
    tt = _tile(T, 512, 8)

    def body(x_ref, g_ref, h_ref, r_ref):
        xv = x_ref[...]
        r = lax.rsqrt(jnp.mean(xv * xv, axis=-1, keepdims=True) + NORM_EPS)
        h_ref[...] = (xv * r * g_ref[...]).astype(BF16)
        r_ref[...] = r

    return _pcall(
        body, name="rms_fwd", grid=(T // tt,),
        in_specs=[pl.BlockSpec((tt, D), lambda i: (i, 0)), pl.BlockSpec((1, D), lambda i: (0, 0))],
        out_specs=[pl.BlockSpec((tt, D), lambda i: (i, 0)), pl.BlockSpec((tt, 1), lambda i: (i, 0))],
        out_shape=[jax.ShapeDtypeStruct((T, D), BF16), jax.ShapeDtypeStruct((T, 1), F32)],
    )(x, g)


def rms_bwd(dh, x, r, g, dres, out_scale):
    T, D = x.shape
    tt = _tile(T, 256, 8)

    def body(dh_ref, x_ref, r_ref, g_ref, dres_ref, dx_ref, dxb_ref, dg_ref):
        i = pl.program_id(0)
        xh = x_ref[...] * r_ref[...]
        dhv = dh_ref[...]
        dxh = dhv * g_ref[...]
        dx = dres_ref[...] + r_ref[...] * (dxh - xh * jnp.mean(dxh * xh, axis=-1, keepdims=True))
        dx_ref[...] = dx
        dxb_ref[...] = (out_scale * dx).astype(BF16)
        part = jnp.sum(dhv * xh, axis=0, keepdims=True)

        @pl.when(i == 0)
        def _():
            dg_ref[...] = part

        @pl.when(i > 0)
        def _():
            dg_ref[...] += part

    row = pl.BlockSpec((tt, D), lambda i: (i, 0))
    return _pcall(
        body, name="rms_bwd", grid=(T // tt,),
        in_specs=[row, row, pl.BlockSpec((tt, 1), lambda i: (i, 0)), pl.BlockSpec((1, D), lambda i: (0, 0)), row],
        out_specs=[row, row, pl.BlockSpec((1, D), lambda i: (0, 0))],
        out_shape=[jax.ShapeDtypeStruct((T, D), F32), jax.ShapeDtypeStruct((T, D), BF16),
                   jax.ShapeDtypeStruct((1, D), F32)],
    )(dh, x, r, g, dres)


def final_loss(x, tgt, g):
    T, D = x.shape
    tt = _tile(T, 256, 8)

    def body(x_ref, t_ref, g_ref, dx_ref, dxb_ref, loss_ref, dg_ref):
        i = pl.program_id(0)
        xv = x_ref[...]
        r = lax.rsqrt(jnp.mean(xv * xv, axis=-1, keepdims=True) + NORM_EPS)
        xh = xv * r
        err = xh * g_ref[...] - t_ref[...]
        part_loss = 0.5 * jnp.sum(jnp.mean(err * err, axis=-1, keepdims=True), axis=0, keepdims=True)
        dy = err * (1.0 / D)
        dxh = dy * g_ref[...]
        dx = r * (dxh - xh * jnp.mean(dxh * xh, axis=-1, keepdims=True))
        dx_ref[...] = dx
        dxb_ref[...] = (0.5 * dx).astype(BF16)
        part_g = jnp.sum(dy * xh, axis=0, keepdims=True)
        part_l = jnp.broadcast_to(part_loss, (8, LANES))

        @pl.when(i == 0)
        def _():
            dg_ref[...] = part_g
            loss_ref[...] = part_l

        @pl.when(i > 0)
        def _():
            dg_ref[...] += part_g
            loss_ref[...] += part_l

    row = pl.BlockSpec((tt, D), lambda i: (i, 0))
    return _pcall(
        body, name="final_loss", grid=(T // tt,),
        in_specs=[row, row, pl.BlockSpec((1, D), lambda i: (0, 0))],
        out_specs=[row, row, pl.BlockSpec((8, LANES), lambda i: (0, 0)), pl.BlockSpec((1, D), lambda i: (0, 0))],
        out_shape=[jax.ShapeDtypeStruct((T, D), F32), jax.ShapeDtypeStruct((T, D), BF16),
                   jax.ShapeDtypeStruct((8, LANES), F32), jax.ShapeDtypeStruct((1, D), F32)],
    )(x, tgt, g)


def ffn_gate(h, wg3, comm=None):
    T, D = h.shape
    nc, _, fs = wg3.shape
    tm = _tile(T, 512, 8)

    def epilogue(vals, ex, outs):
        outs[0][...] = vals[0].astype(BF16)

    return _mm("ffn_gate", grid=(nc, T // tm),
               pairs=[(h, pl.BlockSpec((tm, D), lambda j, i: (i, 0)),
                       wg3, pl.BlockSpec((None, D, fs), lambda j, i: (j, 0, 0)), NN, 0)],
               out_shape=[jax.ShapeDtypeStruct((T, nc * fs), BF16)],
               out_specs=[pl.BlockSpec((tm, fs), lambda j, i: (i, j))],
               acc_shapes=[(tm, fs)], nk=1, kaxis=None, epilogue=epilogue, comm=comm)[0]


def ffn_upmul(h, wu3, a, comm=None):
    T, D = h.shape
    nc, _, fs = wu3.shape
    tm = _tile(T, 512, 8)

    def epilogue(vals, ex, outs):
        b = vals[0]
        av = ex[0][...].astype(F32)
        outs[0][...] = b.astype(BF16)
        outs[1][...] = (av * _sigmoid(av) * b).astype(BF16)

    t_spec = pl.BlockSpec((tm, fs), lambda j, i: (i, j))
    o_shape = jax.ShapeDtypeStruct((T, nc * fs), BF16)
    return _mm("ffn_upmul", grid=(nc, T // tm),
               pairs=[(h, pl.BlockSpec((tm, D), lambda j, i: (i, 0)),
                       wu3, pl.BlockSpec((None, D, fs), lambda j, i: (j, 0, 0)), NN, 0)],
               extra=[(a, t_spec)], out_shape=[o_shape] * 2, out_specs=[t_spec] * 2,
               acc_shapes=[(tm, fs)], nk=1, kaxis=None, epilogue=epilogue, comm=comm)


def ffn_up(h, wg3, wu3, comm=None):
    T, D = h.shape
    nc, _, fs = wg3.shape
    tm = _tile(T, 512, 8)

    def epilogue(vals, ex, outs):
        a, b = vals
        outs[0][...] = a.astype(BF16)
        outs[1][...] = b.astype(BF16)
        outs[2][...] = (a * _sigmoid(a) * b).astype(BF16)

    h_spec = pl.BlockSpec((tm, D), lambda j, i: (i, 0))
    w_spec = pl.BlockSpec((None, D, fs), lambda j, i: (j, 0, 0))
    o_spec = pl.BlockSpec((tm, fs), lambda j, i: (i, j))
    o_shape = jax.ShapeDtypeStruct((T, nc * fs), BF16)
    return _mm("ffn_up", grid=(nc, T // tm),
               pairs=[(h, h_spec, wg3, w_spec, NN, 0), (h, h_spec, wu3, w_spec, NN, 1)],
               out_shape=[o_shape] * 3, out_specs=[o_spec] * 3, acc_shapes=[(tm, fs)] * 2, nk=1, kaxis=None,
               epilogue=epilogue, comm=comm)


def mm_residual(name, a, b3, res, scale, comm=None):
    T = a.shape[0]
    nk, tk, N = b3.shape
    tm, tn = _tile(T, 512, 8), _tile(N, 2048)

    def epilogue(vals, ex, outs):
        outs[0][...] = ex[0][...] + scale * vals[0]

    return _mm(name, grid=(T // tm, N // tn, nk),
               pairs=[(a, pl.BlockSpec((tm, tk), lambda i, n, k: (i, k)),
                       b3, pl.BlockSpec((None, tk, tn), lambda i, n, k: (k, 0, n)), NN, 0)],
               extra=[(res, pl.BlockSpec((tm, tn), lambda i, n, k: (i, n)))],
               out_shape=[jax.ShapeDtypeStruct((T, N), F32)],
               out_specs=[pl.BlockSpec((tm, tn), lambda i, n, k: (i, n))],
               acc_shapes=[(tm, tn)], nk=nk, kaxis=2, epilogue=epilogue, comm=comm)


def ffn_bwd_mid(dout, wd3, a, b, comm=None):
    T, D = dout.shape
    nc, fs, _ = wd3.shape
    tm = _tile(T, 512, 8)

    def epilogue(vals, ex, outs):
        dm = vals[0]
        av = ex[0][...].astype(F32)
        bv = ex[1][...].astype(F32)
        s = _sigmoid(av)
        outs[0][...] = (dm * bv * (s * (1.0 + av * (1.0 - s)))).astype(BF16)
        outs[1][...] = (dm * (av * s)).astype(BF16)

    t_spec = pl.BlockSpec((tm, fs), lambda j, i: (i, j))
    o_shape = jax.ShapeDtypeStruct((T, nc * fs), BF16)
    return _mm("ffn_bwd_mid", grid=(nc, T // tm),
               pairs=[(dout, pl.BlockSpec((tm, D), lambda j, i: (i, 0)),
                       wd3, pl.BlockSpec((None, fs, D), lambda j, i: (j, 0, 0)), NT, 0)],
               extra=[(a, t_spec), (b, t_spec)],
               out_shape=[o_shape] * 2, out_specs=[t_spec] * 2, acc_shapes=[(tm, fs)], nk=1, kaxis=None,
               epilogue=epilogue, comm=comm)


def dw_rowshard(name, a, b, nc, comm=None):
    T, M = a.shape
    N = b.shape[1]
    ms = M // nc
    tn, tk = _tile(N, 2048 if ms <= 512 else 1024), _tile(T, 1024, 16)

    def epilogue(vals, ex, outs):
        outs[0][...] = vals[0].astype(BF16)

    return _mm(name, grid=(nc, N // tn, T // tk),
               pairs=[(a, pl.BlockSpec((tk, ms), lambda j, n, k: (k, j)),
                       b, pl.BlockSpec((tk, tn), lambda j, n, k: (k, n)), TN, 0)],
               out_shape=[jax.ShapeDtypeStruct((nc, ms, N), BF16)],
               out_specs=[pl.BlockSpec((None, ms, tn), lambda j, n, k: (j, 0, n))],
               acc_shapes=[(ms, tn)], nk=T // tk, kaxis=2, epilogue=epilogue, comm=comm)


def dw_colshard(name, a, bs, nc, comm=None):
    T, M = a.shape
    ns = bs[0].shape[1] // nc
    tm, tk = _tile(M, 512), _tile(T, 1024, 16)

    def epilogue(vals, ex, outs):
        for v, o in zip(vals, outs):
            o[...] = v.astype(BF16)

    a_spec = pl.BlockSpec((tk, tm), lambda j, m, k: (k, m))
    b_spec = pl.BlockSpec((tk, ns), lambda j, m, k: (k, j))
    return _mm(name, grid=(nc, M // tm, T // tk),
               pairs=[(a, a_spec, b, b_spec, TN, p) for p, b in enumerate(bs)],
               out_shape=[jax.ShapeDtypeStruct((nc, M, ns), BF16)] * len(bs),
               out_specs=[pl.BlockSpec((None, tm, ns), lambda j, m, k: (j, m, 0))] * len(bs),
               acc_shapes=[(tm, ns)] * len(bs), nk=T // tk, kaxis=2, epilogue=epilogue, comm=comm)


def ffn_dh(da, db, wg3, wu3, comm=None):
    T = da.shape[0]
    nc, D, fs = wg3.shape
    tm, tn = _tile(T, 512, 8), _tile(D, 1024)

    def epilogue(vals, ex, outs):
        outs[0][...] = vals[0]

    a_spec = pl.BlockSpec((tm, fs), lambda i, n, k: (i, k))
    w_spec = pl.BlockSpec((None, tn, fs), lambda i, n, k: (k, n, 0))
    return _mm("ffn_dh", grid=(T // tm, D // tn, nc),
               pairs=[(da, a_spec, wg3, w_spec, NT, 0), (db, a_spec, wu3, w_spec, NT, 0)],
               out_shape=[jax.ShapeDtypeStruct((T, D), F32)],
               out_specs=[pl.BlockSpec((tm, tn), lambda i, n, k: (i, n))],
               acc_shapes=[(tm, tn)], nk=nc, kaxis=2, epilogue=epilogue, comm=comm)[0]


def proj_main(h, w_t, P):
    T, D = h.shape
    tm, tn = _tile(T, 512, 8), _tile(P, 1024)

    def epilogue(vals, ex, outs):
        outs[0][...] = vals[0].astype(BF16)

    return _mm("proj_main", grid=(P // tn, T // tm),
               pairs=[(h, pl.BlockSpec((tm, D), lambda j, i: (i, 0)),
                       w_t, pl.BlockSpec((tn, D), lambda j, i: (j, 0)), NT, 0)],
               out_shape=[jax.ShapeDtypeStruct((T, P), BF16)],
               out_specs=[pl.BlockSpec((tm, tn), lambda j, i: (i, j))],
               acc_shapes=[(tm, tn)], nk=1, kaxis=None, epilogue=epilogue)[0]


def mm_nt_bf16(name, a, w):
    T, K = a.shape
    M = w.shape[0]
    tm, tn = _tile(T, 512, 8), _tile(M, 1024)

    def epilogue(vals, ex, outs):
        outs[0][...] = vals[0].astype(BF16)

    return _mm(name, grid=(T // tm, M // tn),
               pairs=[(a, pl.BlockSpec((tm, K), lambda i, n: (i, 0)),
                       w, pl.BlockSpec((tn, K), lambda i, n: (n, 0)), NT, 0)],
               out_shape=[jax.ShapeDtypeStruct((T, M), BF16)],
               out_specs=[pl.BlockSpec((tm, tn), lambda i, n: (i, n))],
               acc_shapes=[(tm, tn)], nk=1, kaxis=None, epilogue=epilogue)[0]


def proj_dh(dproj, w_t, df, wf_t, comm=None):
    T, P = dproj.shape
    D = w_t.shape[1]
    tm, tn, tk = _tile(T, 512, 8), _tile(D, 2048), _tile(P, 1280)

    def epilogue(vals, ex, outs):
        outs[0][...] = vals[0]

    return _mm("proj_dh", grid=(T // tm, D // tn, P // tk),
               pairs=[(dproj, pl.BlockSpec((tm, tk), lambda i, n, k: (i, k)),
                       w_t, pl.BlockSpec((tk, tn), lambda i, n, k: (k, n)), NN, 0)],
               once_pairs=[(df, pl.BlockSpec((tm, LANES), lambda i, n, k: (i, 0)),
                            wf_t, pl.BlockSpec((LANES, tn), lambda i, n, k: (0, n)), NN, 0)],
               out_shape=[jax.ShapeDtypeStruct((T, D), F32)],
               out_specs=[pl.BlockSpec((tm, tn), lambda i, n, k: (i, n))],
               acc_shapes=[(tm, tn)], nk=P // tk, kaxis=2, epilogue=epilogue, comm=comm)[0]


def proj_dw(dproj, df, h, rows):
    T, P = dproj.shape
    D = h.shape[1]
    tm, tn, tk = _tile(P, 1280), _tile(D, 1024), _tile(T, 1024, 16)

    def to_bf16(vals, ex, outs):
        outs[0][...] = vals[0].astype(BF16)

    def to_f32(vals, ex, outs):
        outs[0][...] = vals[0]

    main = _mm("proj_dw_main", grid=(P // tm, D // tn, T // tk),
               pairs=[(dproj, pl.BlockSpec((tk, tm), lambda m, n, k: (k, m)),
                       h, pl.BlockSpec((tk, tn), lambda m, n, k: (k, n)), TN, 0)],
               out_shape=[jax.ShapeDtypeStruct((rows, D), BF16)],
               out_specs=[pl.BlockSpec((tm, tn), lambda m, n, k: (m, n))],
               acc_shapes=[(tm, tn)], nk=T // tk, kaxis=2, epilogue=to_bf16)[0]
    gate = _mm("proj_dw_f", grid=(1, D // tn, T // tk),
               pairs=[(df, pl.BlockSpec((tk, LANES), lambda m, n, k: (k, 0)),
                       h, pl.BlockSpec((tk, tn), lambda m, n, k: (k, n)), TN, 0)],
               out_shape=[jax.ShapeDtypeStruct((LANES, D), F32)],
               out_specs=[pl.BlockSpec((LANES, tn), lambda m, n, k: (0, n))],
               acc_shapes=[(LANES, tn)], nk=T // tk, kaxis=2, epilogue=to_f32)[0]
    return main, gate


def fgate_fwd(h, wf_t, bias, n_heads):
    T, D = h.shape
    tt = _tile(T, 512, 8)

    def body(h_ref, w_ref, b_ref, f_ref, c_ref, carry):
        i = pl.program_id(0)

        @pl.when(i == 0)
        def _():
            carry[...] = jnp.zeros_like(carry)

        f = lax.dot_general(h_ref[...], w_ref[...], NT, preferred_element_type=F32) + b_ref[...]
        logf = jnp.minimum(f, 0.0) - jnp.log(1.0 + jnp.exp(-jnp.abs(f)))
        tri = (lax.broadcasted_iota(jnp.int32, (tt, tt), 0) >= lax.broadcasted_iota(jnp.int32, (tt, tt), 1))
        cs = jnp.dot(tri.astype(F32), logf, preferred_element_type=F32, precision=lax.Precision.HIGHEST)
        c = cs + carry[...]
        f_ref[...] = f
        c_ref[...] = c
        carry[...] = c[tt - 1:tt, :]

    row = pl.BlockSpec((tt, LANES), lambda i: (i, 0))
    return _pcall(
        body, name="fgate_fwd", grid=(T // tt,),
        in_specs=[pl.BlockSpec((tt, D), lambda i: (i, 0)), pl.BlockSpec((LANES, D), lambda i: (0, 0)),
                  pl.BlockSpec((1, LANES), lambda i: (0, 0))],
        out_specs=[row, row],
        out_shape=[jax.ShapeDtypeStruct((T, LANES), F32)] * 2,
        scratch=[pltpu.VMEM((1, LANES), F32)],
    )(h, wf_t, bias)


def fgate_bwd(dc, f, n_heads):
    T = dc.shape[0]
    tt = _tile(T, 512, 8)
    nt = T // tt

    def body(dc_ref, f_ref, df_ref, db_ref, carry):
        i = pl.program_id(0)

        @pl.when(i == 0)
        def _():
            carry[...] = jnp.zeros_like(carry)

        tri = (lax.broadcasted_iota(jnp.int32, (tt, tt), 1) >= lax.broadcasted_iota(jnp.int32, (tt, tt), 0))
        rs = jnp.dot(tri.astype(F32), dc_ref[...], preferred_element_type=F32,
                     precision=lax.Precision.HIGHEST) + carry[...]
        carry[...] = rs[0:1, :]
        lane = lax.broadcasted_iota(jnp.int32, (tt, LANES), 1)
        df = jnp.where(lane < n_heads, rs * _sigmoid(-f_ref[...]), 0.0)
        df_ref[...] = df.astype(BF16)
        part = jnp.sum(df, axis=0, keepdims=True)

        @pl.when(i == 0)
        def _():
            db_ref[...] = part

        @pl.when(i > 0)
        def _():
            db_ref[...] += part

    rev = pl.BlockSpec((tt, LANES), lambda i: (nt - 1 - i, 0))
    return _pcall(
        body, name="fgate_bwd", grid=(nt,),
        in_specs=[rev, rev],
        out_specs=[rev, pl.BlockSpec((1, LANES), lambda i: (0, 0))],
        out_shape=[jax.ShapeDtypeStruct((T, LANES), BF16), jax.ShapeDtypeStruct((1, LANES), F32)],
        scratch=[pltpu.VMEM((1, LANES), F32)],
    )(dc, f)


SUBLANES = 8
SHIFT_ROWS = HALO - SUBLANES


def _shifted_copies(buf, sh, tt):
    for r in range(1, SUBLANES):
        sh[r - 1, 0:tt + SHIFT_ROWS, :] = buf[pl.ds(r, tt + SHIFT_ROWS), :]


def _tap(buf, sh, offset, tt):
    q, r = divmod(offset, SUBLANES)
    if r == 0:
        return buf[pl.ds(SUBLANES * q, tt), :]
    return sh[r - 1, pl.ds(SUBLANES * q, tt), :]


def conv_fwd(proj, conv_w, conv_b, ln_g, ln_b):
    T = proj.shape[0]
    C = conv_w.shape[1]
    tt = _tile(T, 256, HALO)
    hb = tt // HALO

    def body(a_ref, g_ref, ah_ref, gh_ref, w_ref, cb_ref, lg_ref, lb_ref, ypre_ref, y_ref, ubuf, ush):
        i = pl.program_id(0)
        u = a_ref[...].astype(F32) * _sigmoid(g_ref[...].astype(F32))
        uh = ah_ref[...].astype(F32) * _sigmoid(gh_ref[...].astype(F32))
        ubuf[0:HALO, :] = jnp.where(i == 0, 0.0, uh)
        ubuf[HALO:HALO + tt, :] = u
        _shifted_copies(ubuf, ush, tt)
        acc = jnp.broadcast_to(cb_ref[...], (tt, C))
        for k in range(CONV_K):
            acc = acc + w_ref[k:k + 1, :] * _tap(ubuf, ush, HALO - (CONV_K - 1) + k, tt)
        ypre_ref[...] = acc
        mu = jnp.mean(acc, axis=-1, keepdims=True)
        d = acc - mu
        rstd = lax.rsqrt(jnp.mean(d * d, axis=-1, keepdims=True) + LN_EPS)
        z = d * rstd * lg_ref[...] + lb_ref[...]
        y_ref[...] = (z * _sigmoid(z)).astype(BF16)

    vec = pl.BlockSpec((1, C), lambda i: (0, 0))
    return _pcall(
        body, name="conv_fwd", grid=(T // tt,),
        in_specs=[pl.BlockSpec((tt, C), lambda i: (i, 0)), pl.BlockSpec((tt, C), lambda i: (i, 1)),
                  pl.BlockSpec((HALO, C), lambda i: (jnp.maximum(i * hb - 1, 0), 0)),
                  pl.BlockSpec((HALO, C), lambda i: (jnp.maximum(i * hb - 1, 0), 1)),
                  pl.BlockSpec((HALO, C), lambda i: (0, 0)), vec, vec, vec],
        out_specs=[pl.BlockSpec((tt, C), lambda i: (i, 0))] * 2,
        out_shape=[jax.ShapeDtypeStruct((T, C), F32), jax.ShapeDtypeStruct((T, C), BF16)],
        scratch=[pltpu.VMEM((tt + HALO, C), F32), pltpu.VMEM((SUBLANES - 1, tt + SHIFT_ROWS, C), F32)],
    )(proj, proj, proj, proj, conv_w, conv_b, ln_g, ln_b)


def conv_bwd(proj, ypre, dycat, conv_w, ln_g, ln_b):
    T = proj.shape[0]
    C = conv_w.shape[1]
    tt = _tile(T, 256, HALO)
    hb = tt // HALO
    nt = T // tt
    last_h = T // HALO - 1

    def ln_bwd(ypre_v, dout_v, lg, lb):
        mu = jnp.mean(ypre_v, axis=-1, keepdims=True)
        d = ypre_v - mu
        rstd = lax.rsqrt(jnp.mean(d * d, axis=-1, keepdims=True) + LN_EPS)
        yh = d * rstd
        z = yh * lg + lb
        s = _sigmoid(z)
        dz = dout_v * (s * (1.0 + z * (1.0 - s)))
        dyh = dz * lg
        dy = rstd * (dyh - jnp.mean(dyh, axis=-1, keepdims=True)
                     - yh * jnp.mean(dyh * yh, axis=-1, keepdims=True))
        return dy, dz, yh

    def body(a_ref, g_ref, ah_ref, gh_ref, yp_ref, ypn_ref, do_ref, don_ref, w_ref, lg_ref, lb_ref,
             dag_ref, dw_ref, dcb_ref, dlg_ref, dlb_ref, ubuf, dybuf, ush, dysh):
        i = pl.program_id(0)
        av = a_ref[...].astype(F32)
        sg = _sigmoid(g_ref[...].astype(F32))
        uh = ah_ref[...].astype(F32) * _sigmoid(gh_ref[...].astype(F32))
        ubuf[0:HALO, :] = jnp.where(i == 0, 0.0, uh)
        ubuf[HALO:HALO + tt, :] = av * sg
        lg, lb = lg_ref[...], lb_ref[...]
        dy, dz, yh = ln_bwd(yp_ref[...], do_ref[...].astype(F32), lg, lb)
        dyn, _, _ = ln_bwd(ypn_ref[...], don_ref[...].astype(F32), lg, lb)
        dybuf[0:tt, :] = dy
        dybuf[tt:tt + HALO, :] = jnp.where(i == nt - 1, 0.0, dyn)
        _shifted_copies(ubuf, ush, tt)
        _shifted_copies(dybuf, dysh, tt)

        @pl.when(i == 0)
        def _():
            dw_ref[...] = jnp.zeros_like(dw_ref)
            dcb_ref[...] = jnp.zeros_like(dcb_ref)
            dlg_ref[...] = jnp.zeros_like(dlg_ref)
            dlb_ref[...] = jnp.zeros_like(dlb_ref)

        du = jnp.zeros((tt, C), F32)
        for k in range(CONV_K):
            du = du + w_ref[k:k + 1, :] * _tap(dybuf, dysh, CONV_K - 1 - k, tt)
            dw_ref[k:k + 1, :] += jnp.sum(dy * _tap(ubuf, ush, HALO - (CONV_K - 1) + k, tt), axis=0, keepdims=True)
        dcb_ref[...] += jnp.sum(dy, axis=0, keepdims=True)
        dlg_ref[...] += jnp.sum(dz * yh, axis=0, keepdims=True)
        dlb_ref[...] += jnp.sum(dz, axis=0, keepdims=True)

        dag_ref[:, 0:C] = (du * sg).astype(BF16)
        dag_ref[:, C:2 * C] = (du * av * sg * (1.0 - sg)).astype(BF16)

    vec = pl.BlockSpec((1, C), lambda i: (0, 0))
    prev_h = lambda col: pl.BlockSpec((HALO, C), lambda i: (jnp.maximum(i * hb - 1, 0), col))
    next_h = pl.BlockSpec((HALO, C), lambda i: (jnp.minimum((i + 1) * hb, last_h), 0))
    return _pcall(
        body, name="conv_bwd", grid=(nt,),
        in_specs=[pl.BlockSpec((tt, C), lambda i: (i, 0)), pl.BlockSpec((tt, C), lambda i: (i, 1)),
                  prev_h(0), prev_h(1),
                  pl.BlockSpec((tt, C), lambda i: (i, 0)), next_h,
                  pl.BlockSpec((tt, C), lambda i: (i, 0)), next_h,
                  pl.BlockSpec((HALO, C), lambda i: (0, 0)), vec, vec],
        out_specs=[pl.BlockSpec((tt, 2 * C), lambda i: (i, 0)), pl.BlockSpec((HALO, C), lambda i: (0, 0)),
                   vec, vec, vec],
        out_shape=[jax.ShapeDtypeStruct((T, 2 * C), BF16), jax.ShapeDtypeStruct((HALO, C), F32),
                   jax.ShapeDtypeStruct((1, C), F32), jax.ShapeDtypeStruct((1, C), F32),
                   jax.ShapeDtypeStruct((1, C), F32)],
        scratch=[pltpu.VMEM((tt + HALO, C), F32), pltpu.VMEM((tt + HALO, C), F32),
                 pltpu.VMEM((SUBLANES - 1, tt + SHIFT_ROWS, C), F32),
                 pltpu.VMEM((SUBLANES - 1, tt + SHIFT_ROWS, C), F32)],
    )(proj, proj, proj, proj, ypre, ypre, dycat, dycat, conv_w, ln_g, ln_b)


PAIR = LANES // HEAD_DIM


def _head_masks(rows):
    lane = lax.broadcasted_iota(jnp.int32, (rows, LANES), 1)
    return [jnp.logical_and(lane >= hh * HEAD_DIM, lane < (hh + 1) * HEAD_DIM) for hh in range(PAIR)]


def _causal(tq, tk):
    return lax.broadcasted_iota(jnp.int32, (tq, tk), 0) >= lax.broadcasted_iota(jnp.int32, (tq, tk), 1)


def _lane_column(block, lane_index):
    lane = lax.broadcasted_iota(jnp.int32, block.shape, 1)
    return jnp.sum(jnp.where(lane == lane_index, block, 0.0), axis=-1, keepdims=True)


def attn_fwd(proj, cum, ck4, q_col, comm=None):
    T = proj.shape[0]
    H, nkv, _, tk = ck4.shape
    tq = tk
    hd = H * HEAD_DIM
    qb, kb, vb = q_col // LANES, (q_col + hd) // LANES, (q_col + 2 * hd) // LANES
    scale = 1.0 / math.sqrt(HEAD_DIM)

    def body(q_ref, k_ref, v_ref, cum_ref, ck_ref, o_ref, lse_ref):
        hp = pl.program_id(0)
        i = pl.program_id(1)
        masks = _head_masks(tq)
        q2 = q_ref[...] * scale
        qs = [jnp.where(mk, q2, jnp.zeros_like(q2)) for mk in masks]
        cqs = [_lane_column(cum_ref[...], PAIR * hp + hh) for hh in range(PAIR)]

        def step(j, carry, diagonal):
            off = pl.multiple_of(j * tk, tk)
            kj = k_ref[pl.ds(off, tk), :]
            vj = v_ref[pl.ds(off, tk), :]
            out = []
            for hh in range(PAIR):
                m, l, acc = carry[hh]
                s = lax.dot_general(qs[hh], kj, NT, preferred_element_type=F32)
                s = s + cqs[hh] - ck_ref[hh, j]
                if diagonal:
                    s = jnp.where(_causal(tq, tk), s, NEG_INF)
                m_new = jnp.maximum(m, jnp.max(s, axis=-1, keepdims=True))
                alpha = jnp.exp(m - m_new)
                p = jnp.exp(s - m_new)
                l = alpha * l + jnp.sum(p, axis=-1, keepdims=True)
                acc = alpha * acc + jnp.dot(p.astype(BF16), vj, preferred_element_type=F32)
                out.append((m_new, l, acc))
            return tuple(out)

        init = tuple((jnp.full((tq, 1), -jnp.inf, F32), jnp.zeros((tq, 1), F32), jnp.zeros((tq, LANES), F32))
                     for _ in range(PAIR))
        carry = lax.fori_loop(0, i, functools.partial(step, diagonal=False), init)
        carry = step(i, carry, True)
        o = carry[PAIR - 1][2] / carry[PAIR - 1][1]
        for hh in range(PAIR - 1):
            o = jnp.where(masks[hh], carry[hh][2] / carry[hh][1], o)
        o_ref[...] = o
        lse = jnp.broadcast_to(carry[PAIR - 1][0] + jnp.log(carry[PAIR - 1][1]), (tq, LANES))
        for hh in range(PAIR - 1):
            lse = jnp.where(masks[hh], carry[hh][0] + jnp.log(carry[hh][1]), lse)
        lse_ref[...] = lse

    return _pcall(
        body, name="attn_fwd", grid=(H // PAIR, T // tq),
        in_specs=[pl.BlockSpec((tq, LANES), lambda hp, i: (i, qb + hp)),
                  pl.BlockSpec((T, LANES), lambda hp, i: (0, kb + hp)),
                  pl.BlockSpec((T, LANES), lambda hp, i: (0, vb + hp)),
                  pl.BlockSpec((tq, LANES), lambda hp, i: (i, 0)),
                  pl.BlockSpec((PAIR, nkv, 1, tk), lambda hp, i: (hp, 0, 0, 0))],
        out_specs=[pl.BlockSpec((tq, LANES), lambda hp, i: (i, hp)),
                   pl.BlockSpec((None, tq, LANES), lambda hp, i: (hp, i, 0))],
        out_shape=[jax.ShapeDtypeStruct((T, hd), F32), jax.ShapeDtypeStruct((H // PAIR, T, LANES), F32)],
        comm=comm,
    )(proj, proj, proj, cum, ck4)


def attn_bwd(proj, o, dycat, lse, cum, ck4, q_col, do_col, comm=None):
    T = proj.shape[0]
    H, nkv, _, tk = ck4.shape
    tq = tk
    nq = T // tq
    hd = H * HEAD_DIM
    qb, kb, vb = q_col // LANES, (q_col + hd) // LANES, (q_col + 2 * hd) // LANES
    dob = do_col // LANES
    scale = 1.0 / math.sqrt(HEAD_DIM)

    def body(q_ref, k_ref, v_ref, o_ref, do_ref, lse_ref, cum_ref, ck_ref,
             dq_ref, dk_ref, dv_ref, dcq_ref, dck_ref):
        hp = pl.program_id(0)
        j = pl.program_id(1)

        def block(i, diagonal):
            masks = _head_masks(tq)
            rows = pl.ds(pl.multiple_of(i * tq, tq), tq)
            q2, k2, v2, do2 = q_ref[rows, :] * scale, k_ref[...], v_ref[...], do_ref[rows, :]
            zero = jnp.zeros_like(q2)
            prod = do2.astype(F32) * o_ref[rows, :]
            cum_q, lse_q = cum_ref[rows, :], lse_ref[rows, :]
            dq_part = dk_part = dv_part = None
            dcq_part = jnp.zeros((tq, LANES), F32)
            lane = lax.broadcasted_iota(jnp.int32, (tq, LANES), 1)
            for hh in range(PAIR):
                qh = jnp.where(masks[hh], q2, zero)
                kh = jnp.where(masks[hh], k2, zero)
                doh = jnp.where(masks[hh], do2, zero)
                delta = jnp.sum(jnp.where(masks[hh], prod, 0.0), axis=-1, keepdims=True)
                s = lax.dot_general(qh, k2, NT, preferred_element_type=F32)
                s = s + _lane_column(cum_q, PAIR * hp + hh) - ck_ref[hh]
                if diagonal:
                    s = jnp.where(_causal(tq, tk), s, NEG_INF)
                p = jnp.exp(s - _lane_column(lse_q, hh * HEAD_DIM))
                dp = lax.dot_general(doh, v2, NT, preferred_element_type=F32)
                ds = p * (dp - delta)
                dsb = ds.astype(BF16)
                dv_h = lax.dot_general(p.astype(BF16), doh, TN, preferred_element_type=F32)
                dk_h = lax.dot_general(dsb, qh, TN, preferred_element_type=F32)
                dq_h = jnp.dot(dsb, kh, preferred_element_type=F32)
                dq_part = dq_h if dq_part is None else dq_part + dq_h
                dk_part = dk_h if dk_part is None else dk_part + dk_h
                dv_part = dv_h if dv_part is None else dv_part + dv_h
                dck_h = -jnp.sum(ds, axis=0, keepdims=True)
                dcq_part = jnp.where(lane == PAIR * hp + hh, jnp.sum(ds, axis=-1, keepdims=True), dcq_part)
                if diagonal:
                    dck_ref[hh] = dck_h
                else:
                    dck_ref[hh] += dck_h
            dq_part = dq_part * scale

            @pl.when(j == 0)
            def _():
                dq_ref[rows, :] = dq_part

            @pl.when(j > 0)
            def _():
                dq_ref[rows, :] += dq_part

            @pl.when(jnp.logical_and(hp == 0, j == 0))
            def _():
                dcq_ref[rows, :] = dcq_part

            @pl.when(jnp.logical_or(hp > 0, j > 0))
            def _():
                dcq_ref[rows, :] += dcq_part

            if diagonal:
                dk_ref[...] = dk_part
                dv_ref[...] = dv_part
            else:
                dk_ref[...] += dk_part
                dv_ref[...] += dv_part

        block(j, True)

        def later(i, carry):
            block(i, False)
            return carry

        lax.fori_loop(j + 1, nq, later, 0)

    at_q = lambda col: pl.BlockSpec((T, LANES), lambda hp, j: (0, col + hp))
    at_k = lambda col: pl.BlockSpec((tk, LANES), lambda hp, j: (j, col + hp))
    lse_spec = pl.BlockSpec((None, T, LANES), lambda hp, j: (hp, 0, 0))
    cum_spec = pl.BlockSpec((T, LANES), lambda hp, j: (0, 0))
    ck_spec = pl.BlockSpec((PAIR, None, 1, tk), lambda hp, j: (hp, j, 0, 0))
    return _pcall(
        body, name="attn_bwd", grid=(H // PAIR, nkv),
        in_specs=[at_q(qb), at_k(kb), at_k(vb), at_q(0), at_q(dob), lse_spec, cum_spec, ck_spec],
        out_specs=[pl.BlockSpec((T, LANES), lambda hp, j: (0, hp)), at_k(0), at_k(0),
                   pl.BlockSpec((T, LANES), lambda hp, j: (0, 0)), ck_spec],
        out_shape=[jax.ShapeDtypeStruct((T, hd), F32)] * 3
        + [jax.ShapeDtypeStruct((T, LANES), F32), jax.ShapeDtypeStruct((H, nkv, 1, tk), F32)],
        comm=comm,
    )(proj, proj, proj, o, dycat, lse, cum, ck4)


ELEMENTWISE_BLOCK_BYTES = 2 * 1024 * 1024
BF16_ROWS = 16


def cast_bf16(arrays, comm=None):
    def slab(a, steps):
        R, C = a.shape
        if R % (steps * BF16_ROWS) == 0:
            return pl.BlockSpec((R // steps, C), lambda i: (i, 0))
        if C % (steps * LANES) == 0:
            return pl.BlockSpec((R, C // steps), lambda i: (0, i))
        return None

    steps = 8 if all(slab(a, 8) is not None for a in arrays) else 4
    specs = [slab(a, steps) for a in arrays]
    n = len(arrays)

    def body(*refs):
        for src, dst in zip(refs[:n], refs[n:]):
            dst[...] = src[...].astype(BF16)

    return _pcall(body, name="cast_bf16", grid=(steps,), in_specs=specs, out_specs=specs,
                  out_shape=[jax.ShapeDtypeStruct(a.shape, BF16) for a in arrays], comm=comm)(*arrays)


def _ew_tiles(rows, cols, bytes_per_element):
    target = max(8, ELEMENTWISE_BLOCK_BYTES // max(1, cols * bytes_per_element))
    if rows <= target:
        return rows, cols
    t = (target // 16) * 16
    while t >= 16:
        if rows % t == 0:
            return t, cols
        t -= 16
    tc = _tile(cols, max(LANES, (ELEMENTWISE_BLOCK_BYTES // (rows * bytes_per_element)) // LANES * LANES))
    return rows, tc


def sum_chips(recv):
    nc, R, C = recv.shape
    tr, tc = _ew_tiles(R, C, 4)

    def body(r_ref, o_ref):
        acc = r_ref[0].astype(F32)
        for j in range(1, nc):
            acc = acc + r_ref[j].astype(F32)
        o_ref[...] = acc

    return _pcall(
        body, name="sum_chips", grid=(R // tr, C // tc),
        in_specs=[pl.BlockSpec((nc, tr, tc), lambda i, j: (0, i, j))],
        out_specs=[pl.BlockSpec((tr, tc), lambda i, j: (i, j))],
        out_shape=[jax.ShapeDtypeStruct((R, C), F32)],
    )(recv)[0]


def add_sibling_half(g, recv):
    nc, R, C = g.shape
    hr = R // 2
    tr, tc = _ew_tiles(hr, C, 4 * nc)

    def body(g_ref, r_ref, o_ref):
        c = lax.axis_index("c")
        for j in range(nc):
            o_ref[j] = (g_ref[j, c].astype(F32) + r_ref[j].astype(F32)).astype(BF16)

    return _pcall(
        body, name="add_sibling_half", grid=(hr // tr, C // tc),
        in_specs=[pl.BlockSpec((nc, 2, tr, tc), lambda i, j: (0, 0, i, j)),
                  pl.BlockSpec((nc, tr, tc), lambda i, j: (0, i, j))],
        out_specs=[pl.BlockSpec((nc, tr, tc), lambda i, j: (0, i, j))],
        out_shape=[jax.ShapeDtypeStruct((nc, hr, C), BF16)],
    )(g.reshape(nc, 2, hr, C), recv)[0]


def adamw(w, m, v, g_parts, comm=None, halves=False):
    R, C = w.shape
    tr, tc = _ew_tiles(R // 2 if halves else R, C, 4 * 4)
    n_g = len(g_parts)
    n_half = (R // 2) // tr
    c1 = 1.0 - ADAM_B1
    c2 = 1.0 - ADAM_B2
    bc1 = 1.0 - ADAM_B1 ** ADAM_STEP
    bc2 = 1.0 - ADAM_B2 ** ADAM_STEP

    def body(*refs):
        w_ref, m_ref, v_ref = refs[:3]
        g_refs = refs[3:3 + n_g]
        g_out, d_out, m_out, v_out = refs[3 + n_g:]
        if halves:
            mine = (pl.program_id(0) >= n_half) == (lax.axis_index("c") == 1)
            g = jnp.where(mine, g_refs[0][...], g_refs[1][...])
        else:
            g = g_refs[0][...]
            for r in g_refs[1:]:
                g = g + r[...]
        m_new = ADAM_B1 * m_ref[...] + c1 * g
        v_new = ADAM_B2 * v_ref[...] + c2 * (g * g)
        m_hat = m_new * (1.0 / bc1)
        v_hat = v_new * (1.0 / bc2)
        g_out[...] = g
        d_out[...] = -ADAM_LR * (m_hat / (jnp.sqrt(v_hat) + ADAM_EPS) + ADAM_WD * w_ref[...])
        m_out[...] = m_new
        v_out[...] = v_new

    spec = pl.BlockSpec((tr, tc), lambda i, j: (i, j))
    g_spec = pl.BlockSpec((tr, tc), lambda i, j: (i % n_half, j)) if halves else spec
    return _pcall(
        body, name="adamw", grid=(R // tr, C // tc),
        in_specs=[spec] * 3 + [g_spec] * n_g, out_specs=[spec] * 4,
        out_shape=[jax.ShapeDtypeStruct((R, C), F32)] * 4, comm=comm,
    )(w, m, v, *g_parts)


def _chip_coords():
    x, y, c = lax.axis_index("x"), lax.axis_index("y"), lax.axis_index("c")
    others = [(1 - x, y), (x, 1 - y), (1 - x, 1 - y)]
    return x, y, c, others


def _remote(src, dst, send_sem, recv_sem, device):
    return pltpu.make_async_remote_copy(src_ref=src, dst_ref=dst, send_sem=send_sem, recv_sem=recv_sem,
                                        device_id=device, device_id_type=MESH)


def gather_comm(shards):
    n = len(shards)
    pieces = [(a, jj) for a in range(n) for jj in range(3)]

    def makers(ins, outs, sems):
        send_sems, recv_sems, local_sems = sems
        x, y, c, others = _chip_coords()
        me = 2 * x + y
        sibling = (x, y, 1 - c)

        def half(ref, a, which, chip=None):
            rows, cols = ins[a].shape[0], ins[a].shape[1]
            lead = () if chip is None else (chip,)
            if rows % (2 * BF16_ROWS) == 0:
                return ref.at[(*lead, pl.ds(which * (rows // 2), rows // 2))]
            return ref.at[(*lead, slice(None), pl.ds(which * (cols // 2), cols // 2))]

        def local(a):
            return pltpu.make_async_copy(ins[a], outs[a].at[me], local_sems.at[a])

        def ici(a, jj):
            ox, oy = others[jj]
            return _remote(half(ins[a], a, c), half(outs[a], a, c, me),
                           send_sems.at[6 * a + jj], recv_sems.at[6 * a + jj], (ox, oy, c))

        def landed(a, jj):
            ox, oy = others[jj]
            got = half(outs[a], a, c, 2 * ox + oy)
            return (_remote(got, got, send_sems.at[6 * a + jj], recv_sems.at[6 * a + jj], (ox, oy, c)),
                    _remote(got, got, send_sems.at[6 * a + 3 + jj], recv_sems.at[6 * a + 3 + jj], sibling))

        def theirs(a, jj):
            ox, oy = others[jj]
            sib = half(outs[a], a, 1 - c, 2 * ox + oy)
            return _remote(sib, sib, send_sems.at[6 * a + 3 + jj], recv_sems.at[6 * a + 3 + jj], sibling)

        return local, ici, landed, theirs

    def start(ins, outs, sems):
        local, ici, _, _ = makers(ins, outs, sems)
        for a in range(n):
            for jj in (2, 0, 1):
                ici(a, jj).start()
        for a in range(n):
            local(a).start()

    def finish(ins, outs, sems):
        local, ici, landed, theirs = makers(ins, outs, sems)
        forwards = []
        for a, jj in pieces:
            got, fwd = landed(a, jj)
            got.wait_recv()
            fwd.start()
            forwards.append(fwd)
        for a, jj in pieces:
            theirs(a, jj).wait_recv()
        for a, jj in pieces:
            ici(a, jj).wait_send()
        for fwd in forwards:
            fwd.wait_send()
        for a in range(n):
            local(a).wait()

    return Comm(shards, [jax.ShapeDtypeStruct((N_CHIP,) + s.shape, s.dtype) for s in shards],
                [pltpu.SemaphoreType.DMA((6 * n,)), pltpu.SemaphoreType.DMA((6 * n,)),
                 pltpu.SemaphoreType.DMA((n,))], start, finish)


def scatter_comm(grads):
    n = len(grads)
    pieces = [(a, jj) for a in range(n) for jj in range(3)]

    def makers(ins, outs, sems):
        send_sems, recv_sems, local_sems = sems
        x, y, c, others = _chip_coords()
        me = 2 * x + y

        def local(a):
            return pltpu.make_async_copy(ins[a].at[me], outs[a].at[me], local_sems.at[a])

        def ici(a, jj):
            ox, oy = others[jj]
            return _remote(ins[a].at[2 * ox + oy], outs[a].at[me], send_sems.at[3 * a + jj],
                           recv_sems.at[3 * a + jj], (ox, oy, c))

        def landed(a, jj):
            ox, oy = others[jj]
            slot = outs[a].at[2 * ox + oy]
            return _remote(slot, slot, send_sems.at[3 * a + jj], recv_sems.at[3 * a + jj], (ox, oy, c))

        return local, ici, landed

    def start(ins, outs, sems):
        local, ici, _ = makers(ins, outs, sems)
        for a in range(n):
            for jj in (2, 0, 1):
                ici(a, jj).start()
        for a in range(n):
            local(a).start()

    def finish(ins, outs, sems):
        local, ici, landed = makers(ins, outs, sems)
        for a, jj in pieces:
            landed(a, jj).wait_recv()
        for a, jj in pieces:
            ici(a, jj).wait_send()
        for a in range(n):
            local(a).wait()

    return Comm(grads, [jax.ShapeDtypeStruct(g.shape, g.dtype) for g in grads],
                [pltpu.SemaphoreType.DMA((3 * n,)), pltpu.SemaphoreType.DMA((3 * n,)),
                 pltpu.SemaphoreType.DMA((n,))], start, finish)


def halfswap_comm(grads):
    n = len(grads)

    def copies(ins, outs, sems):
        send_sems, recv_sems = sems
        x, y, c, _ = _chip_coords()
        out = []
        for a in range(n):
            hr = ins[a].shape[1] // 2
            out.append(_remote(ins[a].at[:, pl.ds((1 - c) * hr, hr)], outs[a], send_sems.at[a], recv_sems.at[a],
                               (x, y, 1 - c)))
        return out

    def start(ins, outs, sems):
        for cp in copies(ins, outs, sems):
            cp.start()

    def finish(ins, outs, sems):
        for cp in copies(ins, outs, sems):
            cp.wait()

    return Comm(grads, [jax.ShapeDtypeStruct((g.shape[0], g.shape[1] // 2, g.shape[2]), g.dtype) for g in grads],
                [pltpu.SemaphoreType.DMA((n,)), pltpu.SemaphoreType.DMA((n,))], start, finish)


def join_comms(first, second):
    ni, no, ns = len(first.operands), len(first.out_shape), len(first.sems)

    def start(ins, outs, sems):
        first.start(ins[:ni], outs[:no], sems[:ns])
        second.start(ins[ni:], outs[no:], sems[ns:])

    def finish(ins, outs, sems):
        first.finish(ins[:ni], outs[:no], sems[:ns])
        second.finish(ins[ni:], outs[no:], sems[ns:])

    return Comm(first.operands + second.operands, first.out_shape + second.out_shape, first.sems + second.sems,
                start, finish)


def swap_comm(parts):
    n = len(parts)

    def copies(ins, outs, sems):
        send_sems, recv_sems = sems
        x, y, c, _ = _chip_coords()
        return [_remote(ins[a], outs[a], send_sems.at[a], recv_sems.at[a], (x, y, 1 - c)) for a in range(n)]

    def start(ins, outs, sems):
        for cp in copies(ins, outs, sems):
            cp.start()

    def finish(ins, outs, sems):
        for cp in copies(ins, outs, sems):
            cp.wait()

    return Comm(parts, [jax.ShapeDtypeStruct(p.shape, p.dtype) for p in parts],
                [pltpu.SemaphoreType.DMA((n,)), pltpu.SemaphoreType.DMA((n,))], start, finish)


def allreduce_small(v):
    R = v.shape[0]

    def body(v_ref, sum_ref, all_ref, send_sems, recv_sems):
        x, y, c = lax.axis_index("x"), lax.axis_index("y"), lax.axis_index("c")
        me = 4 * x + 2 * y + c
        all_ref[me] = v_ref[...]
        copies = []
        for k in range(1, N_DEV):
            px = 1 - x if k & 4 else x
            py = 1 - y if k & 2 else y
            pc = 1 - c if k & 1 else c
            cp = pltpu.make_async_remote_copy(
                src_ref=v_ref, dst_ref=all_ref.at[me], send_sem=send_sems.at[k - 1], recv_sem=recv_sems.at[k - 1],
                device_id=(px, py, pc), device_id_type=MESH)
            cp.start()
            copies.append((cp, 4 * px + 2 * py + pc))
        for k, (cp, peer) in enumerate(copies):
            pltpu.make_async_remote_copy(
                src_ref=v_ref, dst_ref=all_ref.at[peer], send_sem=send_sems.at[k], recv_sem=recv_sems.at[k],
                device_id=(x, y, c), device_id_type=MESH).wait_recv()
        for cp, _ in copies:
            cp.wait_send()
        acc = all_ref[0]
        for d in range(1, N_DEV):
            acc = acc + all_ref[d]
        sum_ref[...] = acc

    vm = pl.BlockSpec(memory_space=pltpu.VMEM)
    return pl.pallas_call(
        body, name="allreduce_small",
        in_specs=[vm], out_specs=[vm, vm],
        out_shape=[jax.ShapeDtypeStruct((R, LANES), F32), jax.ShapeDtypeStruct((N_DEV, R, LANES), F32)],
        scratch_shapes=[pltpu.SemaphoreType.DMA((N_DEV - 1,)), pltpu.SemaphoreType.DMA((N_DEV - 1,))],
    )(v)[0]


SMALL_NAMES = ("ffn1_norm", "mix_norm", "ffn2_norm", "final_norm", "conv_b", "conv_ln_g", "conv_ln_b")


def _pack_small(vecs, bias, conv_w_rows, loss_tile):
    rows = [vecs[n].reshape(-1, LANES) for n in SMALL_NAMES]
    rows.append(bias.reshape(1, LANES))
    rows.append(conv_w_rows.reshape(-1, LANES))
    rows.append(loss_tile[0:1, :])
    packed = jnp.concatenate(rows, axis=0)
    pad = (-packed.shape[0]) % 8
    return jnp.pad(packed, ((0, pad), (0, 0)))


def _unpack_small(packed, sizes, n_conv_rows):
    out, r = {}, 0
    for n in SMALL_NAMES:
        k = sizes[n] // LANES
        out[n] = packed[r:r + k].reshape(-1)
        r += k
    out["fgate_bias"] = packed[r]
    r += 1
    out["conv_w"] = packed[r:r + n_conv_rows]
    r += n_conv_rows
    out["loss"] = packed[r, 0]
    return out


def kernel(x, ffn1_norm, ffn1_w_gate, ffn1_w_up, ffn1_w_down, mix_norm, w_in, fgate_bias, conv_w, conv_b, conv_ln_g, conv_ln_b, w_out, ffn2_norm, ffn2_w_gate, ffn2_w_up, ffn2_w_down, final_norm, loss_target, m_ffn1_norm, m_ffn1_w_gate, m_ffn1_w_up, m_ffn1_w_down, m_mix_norm, m_w_in, m_fgate_bias, m_conv_w, m_conv_b, m_conv_ln_g, m_conv_ln_b, m_w_out, m_ffn2_norm, m_ffn2_w_gate, m_ffn2_w_up, m_ffn2_w_down, m_final_norm, v_ffn1_norm, v_ffn1_w_gate, v_ffn1_w_up, v_ffn1_w_down, v_mix_norm, v_w_in, v_fgate_bias, v_conv_w, v_conv_b, v_conv_ln_g, v_conv_ln_b, v_w_out, v_ffn2_norm, v_ffn2_w_gate, v_ffn2_w_up, v_ffn2_w_down, v_final_norm):
    w = dict(ffn1_norm=ffn1_norm, ffn1_w_gate=ffn1_w_gate, ffn1_w_up=ffn1_w_up, ffn1_w_down=ffn1_w_down,
             mix_norm=mix_norm, w_in=w_in, fgate_bias=fgate_bias, conv_w=conv_w, conv_b=conv_b,
             conv_ln_g=conv_ln_g, conv_ln_b=conv_ln_b, w_out=w_out, ffn2_norm=ffn2_norm,
             ffn2_w_gate=ffn2_w_gate, ffn2_w_up=ffn2_w_up, ffn2_w_down=ffn2_w_down, final_norm=final_norm)
    m = dict(ffn1_norm=m_ffn1_norm, ffn1_w_gate=m_ffn1_w_gate, ffn1_w_up=m_ffn1_w_up, ffn1_w_down=m_ffn1_w_down,
             mix_norm=m_mix_norm, w_in=m_w_in, fgate_bias=m_fgate_bias, conv_w=m_conv_w, conv_b=m_conv_b,
             conv_ln_g=m_conv_ln_g, conv_ln_b=m_conv_ln_b, w_out=m_w_out, ffn2_norm=m_ffn2_norm,
             ffn2_w_gate=m_ffn2_w_gate, ffn2_w_up=m_ffn2_w_up, ffn2_w_down=m_ffn2_w_down, final_norm=m_final_norm)
    v = dict(ffn1_norm=v_ffn1_norm, ffn1_w_gate=v_ffn1_w_gate, ffn1_w_up=v_ffn1_w_up, ffn1_w_down=v_ffn1_w_down,
             mix_norm=v_mix_norm, w_in=v_w_in, fgate_bias=v_fgate_bias, conv_w=v_conv_w, conv_b=v_conv_b,
             conv_ln_g=v_conv_ln_g, conv_ln_b=v_conv_ln_b, w_out=v_w_out, ffn2_norm=v_ffn2_norm,
             ffn2_w_gate=v_ffn2_w_gate, ffn2_w_up=v_ffn2_w_up, ffn2_w_down=v_ffn2_w_down, final_norm=v_final_norm)
    names = list(w.keys())
    big = ("ffn1_w_gate", "ffn1_w_up", "ffn1_w_down", "w_in", "w_out", "ffn2_w_gate", "ffn2_w_up", "ffn2_w_down")

    T, D = x.shape[1], x.shape[2]
    C = conv_b.shape[0]
    H = fgate_bias.shape[0]
    cs = conv_w.shape[1]
    in_cols = N_CHIP * w_in.shape[1]
    p_main = in_cols - H

    x0, tgt = x[0], loss_target[0]
    tk = _tile(T, 512, 128)
    nkv = T // tk
    row = lambda a: a.reshape(1, -1)
    grad, delta, new_m, new_v = {}, {}, {}, {}

    def update(n, parts, comm=None, halves=False):
        args = (w[n], m[n], v[n])
        if n == "w_in":
            outs = [t.T for t in adamw(*[a.T for a in args], parts, comm=comm)]
        else:
            outs = adamw(*args, parts, comm=comm, halves=halves)
        grad[n], delta[n], new_m[n], new_v[n] = outs

    rest = [n for n in big if n != "ffn1_w_gate"]
    g0 = gather_comm([w["ffn1_w_gate"].astype(BF16), jnp.pad(conv_w, ((0, HALO - CONV_K), (0, 0)))])
    wb = dict(zip(rest, cast_bf16([w[n].T if n == "w_in" else w[n] for n in rest], comm=g0)))
    wg1, conv_w4 = g0.results
    conv_w_full = conv_w4.transpose(1, 0, 2).reshape(HALO, C)
    h1, r1 = rms_fwd(x0, row(ffn1_norm))
    g1a = gather_comm([wb["ffn1_w_up"]])
    a1 = ffn_gate(h1, wg1, comm=g1a)
    wu1 = g1a.results[0]
    g1b = gather_comm([wb["ffn1_w_down"]])
    b1, mid1 = ffn_upmul(h1, wu1, a1, comm=g1b)
    wd1 = g1b.results[0]
    g2 = gather_comm([wb["w_in"]])
    x1 = mm_residual("ffn_down_g", mid1, wd1, x0, 0.5, comm=g2)[0]
    w_t = g2.results[0].reshape(in_cols, D)

    wf_t = jnp.pad(w_t[p_main:], ((0, LANES - H), (0, 0)))
    bias_pad = jnp.pad(row(fgate_bias), ((0, 0), (0, LANES - H)))
    h2, r2 = rms_fwd(x1, row(mix_norm))
    proj = proj_main(h2, w_t, p_main)
    f, cum = fgate_fwd(h2, wf_t, bias_pad, H)
    ypre, yconv = conv_fwd(proj, conv_w_full, row(conv_b), row(conv_ln_g), row(conv_ln_b))
    ck4 = cum[:, :H].T.reshape(H, nkv, 1, tk)
    g3 = gather_comm([wb["w_out"], wb["ffn2_w_gate"], wb["ffn2_w_up"], wb["ffn2_w_down"]])
    o, lse = attn_fwd(proj, cum, ck4, 2 * C, comm=g3)
    w_out3, wg2, wu2, wd2 = g3.results
    ycat = jnp.concatenate([yconv, o.astype(BF16)], axis=1)
    x2 = mm_residual("out_proj", ycat, w_out3.reshape(2, -1, D), x1, 1.0)[0]

    h3, r3 = rms_fwd(x2, row(ffn2_norm))
    a2, b2, mid2 = ffn_up(h3, wg2, wu2)
    x3 = mm_residual("ffn_down", mid2, wd2, x2, 0.5)[0]
    dx3, dx3b, loss_tile, d_final = final_loss(x3, tgt, row(final_norm))

    da2, db2 = ffn_bwd_mid(dx3b, wd2, a2, b2)
    dwd2 = dw_rowshard("ffn_dwd", mid2, dx3b, N_CHIP)[0]
    s1 = scatter_comm([dwd2])
    dwg2, dwu2 = dw_colshard("ffn_dwgu_s", h3, [da2, db2], N_CHIP, comm=s1)
    s2 = scatter_comm([dwg2])
    dh3 = ffn_dh(da2, db2, wg2, wu2, comm=s2)
    dx2, dx2b, d_ffn2_norm = rms_bwd(dh3, x2, r3, row(ffn2_norm), dx3, 1.0)

    dycat = mm_nt_bf16("out_proj_dy", dx2b, w_out3.reshape(-1, D))
    dw_out3 = dw_rowshard("out_proj_dw", ycat, dx2b, N_CHIP)[0]
    s3 = scatter_comm([dwu2, dw_out3])
    dq, dk, dv, dcq, dck4 = attn_bwd(proj, o, dycat, lse, cum, ck4, 2 * C, C, comm=s3)
    dc = dcq + jnp.pad(dck4.reshape(H, T).T, ((0, 0), (0, LANES - H)))
    df, d_bias = fgate_bwd(dc, f, H)
    dag, d_conv_w, d_conv_b, d_ln_g, d_ln_b = conv_bwd(proj, ypre, dycat, conv_w_full, row(conv_ln_g),
                                                       row(conv_ln_b))
    dproj = jnp.concatenate([dag, dq.astype(BF16), dk.astype(BF16), dv.astype(BF16)], axis=1)
    early = ("ffn2_w_down", "ffn2_w_gate", "ffn2_w_up", "w_out")
    early_sums = [sum_chips(r) for r in (s1.results[0], s2.results[0], s3.results[0], s3.results[1])]
    sw1 = swap_comm(early_sums)
    dh2 = proj_dh(dproj, w_t, df, wf_t, comm=sw1)
    dw_t, dwf_t = proj_dw(dproj, df, h2, in_cols)
    dw_t = lax.dynamic_update_slice(dw_t, dwf_t[:H].astype(BF16), (p_main, 0))
    dw_in3 = dw_t.reshape(N_CHIP, in_cols // N_CHIP, D)
    dx1, dx1b, d_mix_norm = rms_bwd(dh2, x1, r2, row(mix_norm), dx2, 0.5)

    s4 = scatter_comm([dw_in3])
    da1, db1 = ffn_bwd_mid(dx1b, wd1, a1, b1, comm=s4)
    dwd1 = dw_rowshard("ffn_dwd", mid1, dx1b, N_CHIP)[0]
    s5 = scatter_comm([dwd1])
    dwg1, dwu1 = dw_colshard("ffn_dwgu_s", h1, [da1, db1], N_CHIP, comm=s5)
    mid_sums = [sum_chips(s4.results[0]), sum_chips(s5.results[0])]
    s6 = join_comms(join_comms(scatter_comm([dwg1]), halfswap_comm([dwu1])), swap_comm(mid_sums))
    dh1 = ffn_dh(da1, db1, wg1, wu1, comm=s6)
    recv_g1, sibling_u1, their_in, their_d1 = s6.results
    grad_x, _, d_ffn1_norm = rms_bwd(dh1, x0, r1, row(ffn1_norm), dx1, 1.0)

    s7 = scatter_comm([add_sibling_half(dwu1, sibling_u1)])
    for i, (n, mine, other) in enumerate(zip(early, early_sums, sw1.results)):
        update(n, [mine, other], comm=s7 if i == 0 else None)
    update("w_in", [mid_sums[0], their_in])
    update("ffn1_w_down", [mid_sums[1], their_d1])
    sum_g1, half_u1 = sum_chips(recv_g1), sum_chips(s7.results[0])
    their_g1, their_u1 = _run_comm("swap_last", swap_comm([sum_g1, half_u1]))
    update("ffn1_w_gate", [sum_g1, their_g1])
    update("ffn1_w_up", [half_u1, their_u1], halves=True)

    gl = dict(ffn1_norm=d_ffn1_norm, mix_norm=d_mix_norm, ffn2_norm=d_ffn2_norm, final_norm=d_final,
              conv_b=d_conv_b, conv_ln_g=d_ln_g, conv_ln_b=d_ln_b)
    small_sizes = {n: w[n].shape[0] for n in SMALL_NAMES}
    packed = _pack_small(gl, d_bias, d_conv_w, loss_tile)
    red = _unpack_small(allreduce_small(packed), small_sizes, HALO * C // LANES)
    loss = red["loss"]
    my_chip = 2 * lax.axis_index("x") + lax.axis_index("y")
    g_conv_w = lax.dynamic_slice_in_dim(red["conv_w"].reshape(HALO, C)[:CONV_K], my_chip * cs, cs, axis=1)
    update("conv_w", [g_conv_w])
    vec_names = SMALL_NAMES + ("fgate_bias",)
    stack = lambda d: jnp.concatenate(
        [jnp.pad(d[n], (0, (-d[n].shape[0]) % LANES)).reshape(-1, LANES) for n in vec_names], axis=0)
    g_stack = jnp.concatenate([red[n].reshape(-1, LANES) for n in SMALL_NAMES] + [red["fgate_bias"][None, :]],
                              axis=0)
    outs = adamw(stack(w), stack(m), stack(v), [g_stack])
    r = 0
    for n in vec_names:
        size = w[n].shape[0]
        k = -(-size // LANES)
        for dst, src in zip((grad, delta, new_m, new_v), outs):
            dst[n] = src[r:r + k].reshape(-1)[:size]
        r += k

    return (loss, grad_x[None], *[grad[n] for n in names], *[delta[n] for n in names],
            *[new_m[n] for n in names], *[new_v[n] for n in names])
```

```python
import functools
import math

import jax
import jax.numpy as jnp
from jax import lax
from jax.experimental import pallas as pl
from jax.experimental.pallas import tpu as pltpu

F32 = jnp.float32
BF16 = jnp.bfloat16
NORM_EPS = 1e-6
LN_EPS = 1e-5
NEG_INF = -1e30
HEAD_DIM = 64
CONV_K = 31
HALO = 32
LANES = 128
N_CHIP = 4
N_DEV = 8
VMEM_LIMIT = 52 * 1024 * 1024
MESH = pl.DeviceIdType.MESH

ADAM_LR = 0.001
ADAM_B1 = 0.9
ADAM_B2 = 0.999
ADAM_EPS = 1e-08
ADAM_WD = 0.01
ADAM_STEP = 10

NN = (((1,), (0,)), ((), ()))
NT = (((1,), (1,)), ((), ()))
TN = (((0,), (0,)), ((), ()))


def _tile(n, pref, unit=128):
    if n <= pref:
        return n
    t = (pref // unit) * unit
    while t > 0:
        if n % t == 0:
            return t
        t -= unit
    raise ValueError(f"no tile for {n} under {pref}")


RELAY_AT = (3, 4)


class Comm:
    def __init__(self, operands, out_shape, sems, start, finish, relay=None):
        self.operands, self.out_shape, self.sems = list(operands), list(out_shape), list(sems)
        self.start, self.finish, self.relay = start, finish, relay
        self.results = None


def _pcall(body, *, name, grid, in_specs, out_specs, out_shape, scratch=(), comm=None):
    params = pltpu.CompilerParams(dimension_semantics=("arbitrary",) * len(grid), vmem_limit_bytes=VMEM_LIMIT)
    scratch = list(scratch)
    if comm is None:
        return pl.pallas_call(body, name=name, grid=grid, in_specs=in_specs, out_specs=out_specs,
                              out_shape=out_shape, scratch_shapes=scratch, compiler_params=params)
    n_in, n_out, n_s = len(in_specs), len(out_shape), len(scratch)
    n_ci, n_co = len(comm.operands), len(comm.out_shape)
    any_spec = pl.BlockSpec(memory_space=pl.ANY)

    def carried(*refs):
        ins, refs = refs[:n_in], refs[n_in:]
        c_ins, refs = refs[:n_ci], refs[n_ci:]
        outs, refs = refs[:n_out], refs[n_out:]
        c_outs, refs = refs[:n_co], refs[n_co:]
        scr, c_sems = refs[:n_s], refs[n_s:]
        step = pl.program_id(0)
        for d in range(1, len(grid)):
            step = step * grid[d] + pl.program_id(d)
        total = math.prod(grid)
        first, last = step == 0, step == total - 1

        @pl.when(first)
        def _():
            comm.start(c_ins, c_outs, c_sems)

        if comm.relay is not None:
            @pl.when(step == min(total - 1, (RELAY_AT[0] * total) // RELAY_AT[1]))
            def _():
                comm.relay(c_ins, c_outs, c_sems)

        body(*ins, *outs, *scr)

        @pl.when(last)
        def _():
            comm.finish(c_ins, c_outs, c_sems)

    call = pl.pallas_call(
        carried, name=name, grid=grid, in_specs=list(in_specs) + [any_spec] * n_ci,
        out_specs=list(out_specs) + [any_spec] * n_co, out_shape=list(out_shape) + comm.out_shape,
        scratch_shapes=scratch + comm.sems, compiler_params=params)

    def run(*operands):
        res = call(*operands, *comm.operands)
        comm.results = list(res[n_out:])
        return list(res[:n_out])

    return run


def _run_comm(name, comm):
    n_ci, n_co = len(comm.operands), len(comm.out_shape)
    any_spec = pl.BlockSpec(memory_space=pl.ANY)

    def body(*refs):
        c_ins, c_outs, c_sems = refs[:n_ci], refs[n_ci:n_ci + n_co], refs[n_ci + n_co:]
        comm.start(c_ins, c_outs, c_sems)
        if comm.relay is not None:
            comm.relay(c_ins, c_outs, c_sems)
        comm.finish(c_ins, c_outs, c_sems)

    return pl.pallas_call(body, name=name, in_specs=[any_spec] * n_ci, out_specs=[any_spec] * n_co,
                          out_shape=comm.out_shape, scratch_shapes=comm.sems)(*comm.operands)


def _sigmoid(x):
    return 1.0 / (1.0 + jnp.exp(-x))


def _mm(name, *, grid, pairs, once_pairs=(), extra=(), out_shape, out_specs, acc_shapes, nk, kaxis, epilogue,
        comm=None):
    all_pairs = list(pairs) + list(once_pairs)
    n_p, n_o = len(pairs), len(once_pairs)
    n_e, n_out, n_acc = len(extra), len(out_shape), len(acc_shapes)

    def body(*refs):
        ab = refs[: 2 * (n_p + n_o)]
        ex = refs[2 * (n_p + n_o): 2 * (n_p + n_o) + n_e]
        outs = refs[2 * (n_p + n_o) + n_e: 2 * (n_p + n_o) + n_e + n_out]
        accs = refs[2 * (n_p + n_o) + n_e + n_out:]

        def dots(idx_range):
            vals = [None] * n_acc
            for p in idx_range:
                d = lax.dot_general(ab[2 * p][...], ab[2 * p + 1][...], all_pairs[p][4],
                                    preferred_element_type=F32)
                ai = all_pairs[p][5]
                vals[ai] = d if vals[ai] is None else vals[ai] + d
            return vals

        if nk == 1:
            vals = dots(range(n_p + n_o))
            epilogue(vals, ex, outs)
            return

        k = pl.program_id(kaxis)

        @pl.when(k == 0)
        def _():
            vals = dots(range(n_p + n_o))
            for ai in range(n_acc):
                accs[ai][...] = vals[ai]

        @pl.when(k > 0)
        def _():
            vals = dots(range(n_p))
            for ai in range(n_acc):
                if vals[ai] is not None:
                    accs[ai][...] += vals[ai]

        @pl.when(k == nk - 1)
        def _():
            epilogue([a[...] for a in accs], ex, outs)

    operands, in_specs = [], []
    for p in all_pairs:
        operands += [p[0], p[2]]
        in_specs += [p[1], p[3]]
    for arr, spec in extra:
        operands.append(arr)
        in_specs.append(spec)
    scratch = [pltpu.VMEM(s, F32) for s in acc_shapes] if nk > 1 else []
    return _pcall(body, name=name, grid=grid, in_specs=in_specs, out_specs=out_specs, out_shape=out_shape,
                  scratch=scratch, comm=comm)(*operands)


def rms_fwd(x, g):
    T, D = x.shape
    tt = _tile(T, 512, 8)

    def body(x_ref, g_ref, h_ref, r_ref):
        xv = x_ref[...]
        r = lax.rsqrt(jnp.mean(xv * xv, axis=-1, keepdims=True) + NORM_EPS)
        h_ref[...] = (xv * r * g_ref[...]).astype(BF16)
        r_ref[...] = r

    return _pcall(
        body, name="rms_fwd", grid=(T // tt,),
        in_specs=[pl.BlockSpec((tt, D), lambda i: (i, 0)), pl.BlockSpec((1, D), lambda i: (0, 0))],
        out_specs=[pl.BlockSpec((tt, D), lambda i: (i, 0)), pl.BlockSpec((tt, 1), lambda i: (i, 0))],
        out_shape=[jax.ShapeDtypeStruct((T, D), BF16), jax.ShapeDtypeStruct((T, 1), F32)],
    )(x, g)


def rms_bwd(dh, x, r, g, dres, out_scale):
    T, D = x.shape
    tt = _tile(T, 256, 8)

    def body(dh_ref, x_ref, r_ref, g_ref, dres_ref, dx_ref, dxb_ref, dg_ref):
        i = pl.program_id(0)
        xh = x_ref[...] * r_ref[...]
        dhv = dh_ref[...]
        dxh = dhv * g_ref[...]
        dx = dres_ref[...] + r_ref[...] * (dxh - xh * jnp.mean(dxh * xh, axis=-1, keepdims=True))
        dx_ref[...] = dx
        dxb_ref[...] = (out_scale * dx).astype(BF16)
        part = jnp.sum(dhv * xh, axis=0, keepdims=True)

        @pl.when(i == 0)
        def _():
            dg_ref[...] = part

        @pl.when(i > 0)
        def _():
            dg_ref[...] += part

    row = pl.BlockSpec((tt, D), lambda i: (i, 0))
    return _pcall(
        body, name="rms_bwd", grid=(T // tt,),
        in_specs=[row, row, pl.BlockSpec((tt, 1), lambda i: (i, 0)), pl.BlockSpec((1, D), lambda i: (0, 0)), row],
        out_specs=[row, row, pl.BlockSpec((1, D), lambda i: (0, 0))],
        out_shape=[jax.ShapeDtypeStruct((T, D), F32), jax.ShapeDtypeStruct((T, D), BF16),
                   jax.ShapeDtypeStruct((1, D), F32)],
    )(dh, x, r, g, dres)


def final_loss(x, tgt, g):
    T, D = x.shape
    tt = _tile(T, 256, 8)

    def body(x_ref, t_ref, g_ref, dx_ref, dxb_ref, loss_ref, dg_ref):
        i = pl.program_id(0)
        xv = x_ref[...]
        r = lax.rsqrt(jnp.mean(xv * xv, axis=-1, keepdims=True) + NORM_EPS)
        xh = xv * r
        err = xh * g_ref[...] - t_ref[...]
        part_loss = 0.5 * jnp.sum(jnp.mean(err * err, axis=-1, keepdims=True), axis=0, keepdims=True)
        dy = err * (1.0 / D)
        dxh = dy * g_ref[...]
        dx = r * (dxh - xh * jnp.mean(dxh * xh, axis=-1, keepdims=True))
        dx_ref[...] = dx
        dxb_ref[...] = (0.5 * dx).astype(BF16)
        part_g = jnp.sum(dy * xh, axis=0, keepdims=True)
        part_l = jnp.broadcast_to(part_loss, (8, LANES))

        @pl.when(i == 0)
        def _():
            dg_ref[...] = part_g
            loss_ref[...] = part_l

        @pl.when(i > 0)
        def _():
            dg_ref[...] += part_g
            loss_ref[...] += part_l

    row = pl.BlockSpec((tt, D), lambda i: (i, 0))
    return _pcall(
        body, name="final_loss", grid=(T // tt,),
        in_specs=[row, row, pl.BlockSpec((1, D), lambda i: (0, 0))],
        out_specs=[row, row, pl.BlockSpec((8, LANES), lambda i: (0, 0)), pl.BlockSpec((1, D), lambda i: (0, 0))],
        out_shape=[jax.ShapeDtypeStruct((T, D), F32), jax.ShapeDtypeStruct((T, D), BF16),
                   jax.ShapeDtypeStruct((8, LANES), F32), jax.ShapeDtypeStruct((1, D), F32)],
    )(x, tgt, g)


def ffn_gate(h, wg3, comm=None):
    T, D = h.shape
    nc, _, fs = wg3.shape
    tm = _tile(T, 512, 8)

    def epilogue(vals, ex, outs):
        outs[0][...] = vals[0].astype(BF16)

    return _mm("ffn_gate", grid=(nc, T // tm),
               pairs=[(h, pl.BlockSpec((tm, D), lambda j, i: (i, 0)),
                       wg3, pl.BlockSpec((None, D, fs), lambda j, i: (j, 0, 0)), NN, 0)],
               out_shape=[jax.ShapeDtypeStruct((T, nc * fs), BF16)],
               out_specs=[pl.BlockSpec((tm, fs), lambda j, i: (i, j))],
               acc_shapes=[(tm, fs)], nk=1, kaxis=None, epilogue=epilogue, comm=comm)[0]


def ffn_upmul(h, wu3, a, comm=None):
    T, D = h.shape
    nc, _, fs = wu3.shape
    tm = _tile(T, 512, 8)

    def epilogue(vals, ex, outs):
        b = vals[0]
        av = ex[0][...].astype(F32)
        outs[0][...] = b.astype(BF16)
        outs[1][...] = (av * _sigmoid(av) * b).astype(BF16)

    t_spec = pl.BlockSpec((tm, fs), lambda j, i: (i, j))
    o_shape = jax.ShapeDtypeStruct((T, nc * fs), BF16)
    return _mm("ffn_upmul", grid=(nc, T // tm),
               pairs=[(h, pl.BlockSpec((tm, D), lambda j, i: (i, 0)),
                       wu3, pl.BlockSpec((None, D, fs), lambda j, i: (j, 0, 0)), NN, 0)],
               extra=[(a, t_spec)], out_shape=[o_shape] * 2, out_specs=[t_spec] * 2,
               acc_shapes=[(tm, fs)], nk=1, kaxis=None, epilogue=epilogue, comm=comm)


def ffn_up(h, wg3, wu3, comm=None):
    T, D = h.shape
    nc, _, fs = wg3.shape
    tm = _tile(T, 512, 8)

    def epilogue(vals, ex, outs):
        a, b = vals
        outs[0][...] = a.astype(BF16)
        outs[1][...] = b.astype(BF16)
        outs[2][...] = (a * _sigmoid(a) * b).astype(BF16)

    h_spec = pl.BlockSpec((tm, D), lambda j, i: (i, 0))
    w_spec = pl.BlockSpec((None, D, fs), lambda j, i: (j, 0, 0))
    o_spec = pl.BlockSpec((tm, fs), lambda j, i: (i, j))
    o_shape = jax.ShapeDtypeStruct((T, nc * fs), BF16)
    return _mm("ffn_up", grid=(nc, T // tm),
               pairs=[(h, h_spec, wg3, w_spec, NN, 0), (h, h_spec, wu3, w_spec, NN, 1)],
               out_shape=[o_shape] * 3, out_specs=[o_spec] * 3, acc_shapes=[(tm, fs)] * 2, nk=1, kaxis=None,
               epilogue=epilogue, comm=comm)


def mm_residual(name, a, b3, res, scale, comm=None):
    T = a.shape[0]
    nk, tk, N = b3.shape
    tm, tn = _tile(T, 512, 8), _tile(N, 2048)

    def epilogue(vals, ex, outs):
        outs[0][...] = ex[0][...] + scale * vals[0]

    return _mm(name, grid=(T // tm, N // tn, nk),
               pairs=[(a, pl.BlockSpec((tm, tk), lambda i, n, k: (i, k)),
                       b3, pl.BlockSpec((None, tk, tn), lambda i, n, k: (k, 0, n)), NN, 0)],
               extra=[(res, pl.BlockSpec((tm, tn), lambda i, n, k: (i, n)))],
               out_shape=[jax.ShapeDtypeStruct((T, N), F32)],
               out_specs=[pl.BlockSpec((tm, tn), lambda i, n, k: (i, n))],
               acc_shapes=[(tm, tn)], nk=nk, kaxis=2, epilogue=epilogue, comm=comm)


def ffn_bwd_mid(dout, wd3, a, b, comm=None):
    T, D = dout.shape
    nc, fs, _ = wd3.shape
    tm = _tile(T, 512, 8)

    def epilogue(vals, ex, outs):
        dm = vals[0]
        av = ex[0][...].astype(F32)
        bv = ex[1][...].astype(F32)
        s = _sigmoid(av)
        outs[0][...] = (dm * bv * (s * (1.0 + av * (1.0 - s)))).astype(BF16)
        outs[1][...] = (dm * (av * s)).astype(BF16)

    t_spec = pl.BlockSpec((tm, fs), lambda j, i: (i, j))
    o_shape = jax.ShapeDtypeStruct((T, nc * fs), BF16)
    return _mm("ffn_bwd_mid", grid=(nc, T // tm),
               pairs=[(dout, pl.BlockSpec((tm, D), lambda j, i: (i, 0)),
                       wd3, pl.BlockSpec((None, fs, D), lambda j, i: (j, 0, 0)), NT, 0)],
               extra=[(a, t_spec), (b, t_spec)],
               out_shape=[o_shape] * 2, out_specs=[t_spec] * 2, acc_shapes=[(tm, fs)], nk=1, kaxis=None,
               epilogue=epilogue, comm=comm)


def dw_rowshard(name, a, b, nc, comm=None):
    T, M = a.shape
    N = b.shape[1]
    ms = M // nc
    tn, tk = _tile(N, 2048 if ms <= 512 else 1024), _tile(T, 1024, 16)

    def epilogue(vals, ex, outs):
        outs[0][...] = vals[0].astype(BF16)

    return _mm(name, grid=(nc, N // tn, T // tk),
               pairs=[(a, pl.BlockSpec((tk, ms), lambda j, n, k: (k, j)),
                       b, pl.BlockSpec((tk, tn), lambda j, n, k: (k, n)), TN, 0)],
               out_shape=[jax.ShapeDtypeStruct((nc, ms, N), BF16)],
               out_specs=[pl.BlockSpec((None, ms, tn), lambda j, n, k: (j, 0, n))],
               acc_shapes=[(ms, tn)], nk=T // tk, kaxis=2, epilogue=epilogue, comm=comm)


def dw_colshard(name, a, bs, nc, comm=None):
    T, M = a.shape
    ns = bs[0].shape[1] // nc
    tm, tk = _tile(M, 512), _tile(T, 1024, 16)

    def epilogue(vals, ex, outs):
        for v, o in zip(vals, outs):
            o[...] = v.astype(BF16)

    a_spec = pl.BlockSpec((tk, tm), lambda j, m, k: (k, m))
    b_spec = pl.BlockSpec((tk, ns), lambda j, m, k: (k, j))
    return _mm(name, grid=(nc, M // tm, T // tk),
               pairs=[(a, a_spec, b, b_spec, TN, p) for p, b in enumerate(bs)],
               out_shape=[jax.ShapeDtypeStruct((nc, M, ns), BF16)] * len(bs),
               out_specs=[pl.BlockSpec((None, tm, ns), lambda j, m, k: (j, m, 0))] * len(bs),
               acc_shapes=[(tm, ns)] * len(bs), nk=T // tk, kaxis=2, epilogue=epilogue, comm=comm)


def ffn_dh(da, db, wg3, wu3, comm=None):
    T = da.shape[0]
    nc, D, fs = wg3.shape
    tm, tn = _tile(T, 512, 8), _tile(D, 1024)

    def epilogue(vals, ex, outs):
        outs[0][...] = vals[0]

    a_spec = pl.BlockSpec((tm, fs), lambda i, n, k: (i, k))
    w_spec = pl.BlockSpec((None, tn, fs), lambda i, n, k: (k, n, 0))
    return _mm("ffn_dh", grid=(T // tm, D // tn, nc),
               pairs=[(da, a_spec, wg3, w_spec, NT, 0), (db, a_spec, wu3, w_spec, NT, 0)],
               out_shape=[jax.ShapeDtypeStruct((T, D), F32)],
               out_specs=[pl.BlockSpec((tm, tn), lambda i, n, k: (i, n))],
               acc_shapes=[(tm, tn)], nk=nc, kaxis=2, epilogue=epilogue, comm=comm)[0]


def proj_main(h, w_t, P):
    T, D = h.shape
    tm, tn = _tile(T, 512, 8), _tile(P, 1024)

    def epilogue(vals, ex, outs):
        outs[0][...] = vals[0].astype(BF16)

    return _mm("proj_main", grid=(P // tn, T // tm),
               pairs=[(h, pl.BlockSpec((tm, D), lambda j, i: (i, 0)),
                       w_t, pl.BlockSpec((tn, D), lambda j, i: (j, 0)), NT, 0)],
               out_shape=[jax.ShapeDtypeStruct((T, P), BF16)],
               out_specs=[pl.BlockSpec((tm, tn), lambda j, i: (i, j))],
               acc_shapes=[(tm, tn)], nk=1, kaxis=None, epilogue=epilogue)[0]


def mm_nt_bf16(name, a, w):
    T, K = a.shape
    M = w.shape[0]
    tm, tn = _tile(T, 512, 8), _tile(M, 1024)

    def epilogue(vals, ex, outs):
        outs[0][...] = vals[0].astype(BF16)

    return _mm(name, grid=(T // tm, M // tn),
               pairs=[(a, pl.BlockSpec((tm, K), lambda i, n: (i, 0)),
                       w, pl.BlockSpec((tn, K), lambda i, n: (n, 0)), NT, 0)],
               out_shape=[jax.ShapeDtypeStruct((T, M), BF16)],
               out_specs=[pl.BlockSpec((tm, tn), lambda i, n: (i, n))],
               acc_shapes=[(tm, tn)], nk=1, kaxis=None, epilogue=epilogue)[0]


def proj_dh(dproj, w_t, df, wf_t, comm=None):
    T, P = dproj.shape
    D = w_t.shape[1]
    tm, tn, tk = _tile(T, 512, 8), _tile(D, 2048), _tile(P, 1280)

    def epilogue(vals, ex, outs):
        outs[0][...] = vals[0]

    return _mm("proj_dh", grid=(T // tm, D // tn, P // tk),
               pairs=[(dproj, pl.BlockSpec((tm, tk), lambda i, n, k: (i, k)),
                       w_t, pl.BlockSpec((tk, tn), lambda i, n, k: (k, n)), NN, 0)],
               once_pairs=[(df, pl.BlockSpec((tm, LANES), lambda i, n, k: (i, 0)),
                            wf_t, pl.BlockSpec((LANES, tn), lambda i, n, k: (0, n)), NN, 0)],
               out_shape=[jax.ShapeDtypeStruct((T, D), F32)],
               out_specs=[pl.BlockSpec((tm, tn), lambda i, n, k: (i, n))],
               acc_shapes=[(tm, tn)], nk=P // tk, kaxis=2, epilogue=epilogue, comm=comm)[0]


def proj_dw(dproj, df, h, rows):
    T, P = dproj.shape
    D = h.shape[1]
    tm, tn, tk = _tile(P, 1280), _tile(D, 1024), _tile(T, 1024, 16)

    def to_bf16(vals, ex, outs):
        outs[0][...] = vals[0].astype(BF16)

    def to_f32(vals, ex, outs):
        outs[0][...] = vals[0]

    main = _mm("proj_dw_main", grid=(P // tm, D // tn, T // tk),
               pairs=[(dproj, pl.BlockSpec((tk, tm), lambda m, n, k: (k, m)),
                       h, pl.BlockSpec((tk, tn), lambda m, n, k: (k, n)), TN, 0)],
               out_shape=[jax.ShapeDtypeStruct((rows, D), BF16)],
               out_specs=[pl.BlockSpec((tm, tn), lambda m, n, k: (m, n))],
               acc_shapes=[(tm, tn)], nk=T // tk, kaxis=2, epilogue=to_bf16)[0]
    gate = _mm("proj_dw_f", grid=(1, D // tn, T // tk),
               pairs=[(df, pl.BlockSpec((tk, LANES), lambda m, n, k: (k, 0)),
                       h, pl.BlockSpec((tk, tn), lambda m, n, k: (k, n)), TN, 0)],
               out_shape=[jax.ShapeDtypeStruct((LANES, D), F32)],
               out_specs=[pl.BlockSpec((LANES, tn), lambda m, n, k: (0, n))],
               acc_shapes=[(LANES, tn)], nk=T // tk, kaxis=2, epilogue=to_f32)[0]
    return main, gate


def fgate_fwd(h, wf_t, bias, n_heads):
    T, D = h.shape
    tt = _tile(T, 512, 8)

    def body(h_ref, w_ref, b_ref, f_ref, c_ref, carry):
        i = pl.program_id(0)

        @pl.when(i == 0)
        def _():
            carry[...] = jnp.zeros_like(carry)

        f = lax.dot_general(h_ref[...], w_ref[...], NT, preferred_element_type=F32) + b_ref[...]
        logf = jnp.minimum(f, 0.0) - jnp.log(1.0 + jnp.exp(-jnp.abs(f)))
        tri = (lax.broadcasted_iota(jnp.int32, (tt, tt), 0) >= lax.broadcasted_iota(jnp.int32, (tt, tt), 1))
        cs = jnp.dot(tri.astype(F32), logf, preferred_element_type=F32, precision=lax.Precision.HIGHEST)
        c = cs + carry[...]
        f_ref[...] = f
        c_ref[...] = c
        carry[...] = c[tt - 1:tt, :]

    row = pl.BlockSpec((tt, LANES), lambda i: (i, 0))
    return _pcall(
        body, name="fgate_fwd", grid=(T // tt,),
        in_specs=[pl.BlockSpec((tt, D), lambda i: (i, 0)), pl.BlockSpec((LANES, D), lambda i: (0, 0)),
                  pl.BlockSpec((1, LANES), lambda i: (0, 0))],
        out_specs=[row, row],
        out_shape=[jax.ShapeDtypeStruct((T, LANES), F32)] * 2,
        scratch=[pltpu.VMEM((1, LANES), F32)],
    )(h, wf_t, bias)


def fgate_bwd(dc, f, n_heads):
    T = dc.shape[0]
    tt = _tile(T, 512, 8)
    nt = T // tt

    def body(dc_ref, f_ref, df_ref, db_ref, carry):
        i = pl.program_id(0)

        @pl.when(i == 0)
        def _():
            carry[...] = jnp.zeros_like(carry)

        tri = (lax.broadcasted_iota(jnp.int32, (tt, tt), 1) >= lax.broadcasted_iota(jnp.int32, (tt, tt), 0))
        rs = jnp.dot(tri.astype(F32), dc_ref[...], preferred_element_type=F32,
                     precision=lax.Precision.HIGHEST) + carry[...]
        carry[...] = rs[0:1, :]
        lane = lax.broadcasted_iota(jnp.int32, (tt, LANES), 1)
        df = jnp.where(lane < n_heads, rs * _sigmoid(-f_ref[...]), 0.0)
        df_ref[...] = df.astype(BF16)
        part = jnp.sum(df, axis=0, keepdims=True)

        @pl.when(i == 0)
        def _():
            db_ref[...] = part

        @pl.when(i > 0)
        def _():
            db_ref[...] += part

    rev = pl.BlockSpec((tt, LANES), lambda i: (nt - 1 - i, 0))
    return _pcall(
        body, name="fgate_bwd", grid=(nt,),
        in_specs=[rev, rev],
        out_specs=[rev, pl.BlockSpec((1, LANES), lambda i: (0, 0))],
        out_shape=[jax.ShapeDtypeStruct((T, LANES), BF16), jax.ShapeDtypeStruct((1, LANES), F32)],
        scratch=[pltpu.VMEM((1, LANES), F32)],
    )(dc, f)


SUBLANES = 8
SHIFT_ROWS = HALO - SUBLANES


def _shifted_copies(buf, sh, tt):
    for r in range(1, SUBLANES):
        sh[r - 1, 0:tt + SHIFT_ROWS, :] = buf[pl.ds(r, tt + SHIFT_ROWS), :]


def _tap(buf, sh, offset, tt):
    q, r = divmod(offset, SUBLANES)
    if r == 0:
        return buf[pl.ds(SUBLANES * q, tt), :]
    return sh[r - 1, pl.ds(SUBLANES * q, tt), :]


def conv_fwd(proj, conv_w, conv_b, ln_g, ln_b):
    T = proj.shape[0]
    C = conv_w.shape[1]
    tt = _tile(T, 256, HALO)
    hb = tt // HALO

    def body(a_ref, g_ref, ah_ref, gh_ref, w_ref, cb_ref, lg_ref, lb_ref, ypre_ref, y_ref, ubuf, ush):
        i = pl.program_id(0)
        u = a_ref[...].astype(F32) * _sigmoid(g_ref[...].astype(F32))
        uh = ah_ref[...].astype(F32) * _sigmoid(gh_ref[...].astype(F32))
        ubuf[0:HALO, :] = jnp.where(i == 0, 0.0, uh)
        ubuf[HALO:HALO + tt, :] = u
        _shifted_copies(ubuf, ush, tt)
        acc = jnp.broadcast_to(cb_ref[...], (tt, C))
        for k in range(CONV_K):
            acc = acc + w_ref[k:k + 1, :] * _tap(ubuf, ush, HALO - (CONV_K - 1) + k, tt)
        ypre_ref[...] = acc
        mu = jnp.mean(acc, axis=-1, keepdims=True)
        d = acc - mu
        rstd = lax.rsqrt(jnp.mean(d * d, axis=-1, keepdims=True) + LN_EPS)
        z = d * rstd * lg_ref[...] + lb_ref[...]
        y_ref[...] = (z * _sigmoid(z)).astype(BF16)

    vec = pl.BlockSpec((1, C), lambda i: (0, 0))
    return _pcall(
        body, name="conv_fwd", grid=(T // tt,),
        in_specs=[pl.BlockSpec((tt, C), lambda i: (i, 0)), pl.BlockSpec((tt, C), lambda i: (i, 1)),
                  pl.BlockSpec((HALO, C), lambda i: (jnp.maximum(i * hb - 1, 0), 0)),
                  pl.BlockSpec((HALO, C), lambda i: (jnp.maximum(i * hb - 1, 0), 1)),
                  pl.BlockSpec((HALO, C), lambda i: (0, 0)), vec, vec, vec],
        out_specs=[pl.BlockSpec((tt, C), lambda i: (i, 0))] * 2,
        out_shape=[jax.ShapeDtypeStruct((T, C), F32), jax.ShapeDtypeStruct((T, C), BF16)],
        scratch=[pltpu.VMEM((tt + HALO, C), F32), pltpu.VMEM((SUBLANES - 1, tt + SHIFT_ROWS, C), F32)],
    )(proj, proj, proj, proj, conv_w, conv_b, ln_g, ln_b)


def conv_bwd(proj, ypre, dycat, conv_w, ln_g, ln_b):
    T = proj.shape[0]
    C = conv_w.shape[1]
    tt = _tile(T, 256, HALO)
    hb = tt // HALO
    nt = T // tt
    last_h = T // HALO - 1

    def ln_bwd(ypre_v, dout_v, lg, lb):
        mu = jnp.mean(ypre_v, axis=-1, keepdims=True)
        d = ypre_v - mu
        rstd = lax.rsqrt(jnp.mean(d * d, axis=-1, keepdims=True) + LN_EPS)
        yh = d * rstd
        z = yh * lg + lb
        s = _sigmoid(z)
        dz = dout_v * (s * (1.0 + z * (1.0 - s)))
        dyh = dz * lg
        dy = rstd * (dyh - jnp.mean(dyh, axis=-1, keepdims=True)
                     - yh * jnp.mean(dyh * yh, axis=-1, keepdims=True))
        return dy, dz, yh

    def body(a_ref, g_ref, ah_ref, gh_ref, yp_ref, ypn_ref, do_ref, don_ref, w_ref, lg_ref, lb_ref,
             dag_ref, dw_ref, dcb_ref, dlg_ref, dlb_ref, ubuf, dybuf, ush, dysh):
        i = pl.program_id(0)
        av = a_ref[...].astype(F32)
        sg = _sigmoid(g_ref[...].astype(F32))
        uh = ah_ref[...].astype(F32) * _sigmoid(gh_ref[...].astype(F32))
        ubuf[0:HALO, :] = jnp.where(i == 0, 0.0, uh)
        ubuf[HALO:HALO + tt, :] = av * sg
        lg, lb = lg_ref[...], lb_ref[...]
        dy, dz, yh = ln_bwd(yp_ref[...], do_ref[...].astype(F32), lg, lb)
        dyn, _, _ = ln_bwd(ypn_ref[...], don_ref[...].astype(F32), lg, lb)
        dybuf[0:tt, :] = dy
        dybuf[tt:tt + HALO, :] = jnp.where(i == nt - 1, 0.0, dyn)
        _shifted_copies(ubuf, ush, tt)
        _shifted_copies(dybuf, dysh, tt)

        @pl.when(i == 0)
        def _():
            dw_ref[...] = jnp.zeros_like(dw_ref)
            dcb_ref[...] = jnp.zeros_like(dcb_ref)
            dlg_ref[...] = jnp.zeros_like(dlg_ref)
            dlb_ref[...] = jnp.zeros_like(dlb_ref)

        du = jnp.zeros((tt, C), F32)
        for k in range(CONV_K):
            du = du + w_ref[k:k + 1, :] * _tap(dybuf, dysh, CONV_K - 1 - k, tt)
            dw_ref[k:k + 1, :] += jnp.sum(dy * _tap(ubuf, ush, HALO - (CONV_K - 1) + k, tt), axis=0, keepdims=True)
        dcb_ref[...] += jnp.sum(dy, axis=0, keepdims=True)
        dlg_ref[...] += jnp.sum(dz * yh, axis=0, keepdims=True)
        dlb_ref[...] += jnp.sum(dz, axis=0, keepdims=True)

        dag_ref[:, 0:C] = (du * sg).astype(BF16)
        dag_ref[:, C:2 * C] = (du * av * sg * (1.0 - sg)).astype(BF16)

    vec = pl.BlockSpec((1, C), lambda i: (0, 0))
    prev_h = lambda col: pl.BlockSpec((HALO, C), lambda i: (jnp.maximum(i * hb - 1, 0), col))
    next_h = pl.BlockSpec((HALO, C), lambda i: (jnp.minimum((i + 1) * hb, last_h), 0))
    return _pcall(
        body, name="conv_bwd", grid=(nt,),
        in_specs=[pl.BlockSpec((tt, C), lambda i: (i, 0)), pl.BlockSpec((tt, C), lambda i: (i, 1)),
                  prev_h(0), prev_h(1),
                  pl.BlockSpec((tt, C), lambda i: (i, 0)), next_h,
                  pl.BlockSpec((tt, C), lambda i: (i, 0)), next_h,
                  pl.BlockSpec((HALO, C), lambda i: (0, 0)), vec, vec],
        out_specs=[pl.BlockSpec((tt, 2 * C), lambda i: (i, 0)), pl.BlockSpec((HALO, C), lambda i: (0, 0)),
                   vec, vec, vec],
        out_shape=[jax.ShapeDtypeStruct((T, 2 * C), BF16), jax.ShapeDtypeStruct((HALO, C), F32),
                   jax.ShapeDtypeStruct((1, C), F32), jax.ShapeDtypeStruct((1, C), F32),
                   jax.ShapeDtypeStruct((1, C), F32)],
        scratch=[pltpu.VMEM((tt + HALO, C), F32), pltpu.VMEM((tt + HALO, C), F32),
                 pltpu.VMEM((SUBLANES - 1, tt + SHIFT_ROWS, C), F32),
                 pltpu.VMEM((SUBLANES - 1, tt + SHIFT_ROWS, C), F32)],
    )(proj, proj, proj, proj, ypre, ypre, dycat, dycat, conv_w, ln_g, ln_b)


PAIR = LANES // HEAD_DIM


def _head_masks(rows):
    lane = lax.broadcasted_iota(jnp.int32, (rows, LANES), 1)
    return [jnp.logical_and(lane >= hh * HEAD_DIM, lane < (hh + 1) * HEAD_DIM) for hh in range(PAIR)]


def _causal(tq, tk):
    return lax.broadcasted_iota(jnp.int32, (tq, tk), 0) >= lax.broadcasted_iota(jnp.int32, (tq, tk), 1)


def _lane_column(block, lane_index):
    lane = lax.broadcasted_iota(jnp.int32, block.shape, 1)
    return jnp.sum(jnp.where(lane == lane_index, block, 0.0), axis=-1, keepdims=True)


def attn_fwd(proj, cum, ck4, q_col, comm=None):
    T = proj.shape[0]
    H, nkv, _, tk = ck4.shape
    tq = tk
    hd = H * HEAD_DIM
    qb, kb, vb = q_col // LANES, (q_col + hd) // LANES, (q_col + 2 * hd) // LANES
    scale = 1.0 / math.sqrt(HEAD_DIM)

    def body(q_ref, k_ref, v_ref, cum_ref, ck_ref, o_ref, lse_ref):
        hp = pl.program_id(0)
        i = pl.program_id(1)
        masks = _head_masks(tq)
        q2 = q_ref[...] * scale
        qs = [jnp.where(mk, q2, jnp.zeros_like(q2)) for mk in masks]
        cqs = [_lane_column(cum_ref[...], PAIR * hp + hh) for hh in range(PAIR)]

        def step(j, carry, diagonal):
            off = pl.multiple_of(j * tk, tk)
            kj = k_ref[pl.ds(off, tk), :]
            vj = v_ref[pl.ds(off, tk), :]
            out = []
            for hh in range(PAIR):
                m, l, acc = carry[hh]
                s = lax.dot_general(qs[hh], kj, NT, preferred_element_type=F32)
                s = s + cqs[hh] - ck_ref[hh, j]
                if diagonal:
                    s = jnp.where(_causal(tq, tk), s, NEG_INF)
                m_new = jnp.maximum(m, jnp.max(s, axis=-1, keepdims=True))
                alpha = jnp.exp(m - m_new)
                p = jnp.exp(s - m_new)
                l = alpha * l + jnp.sum(p, axis=-1, keepdims=True)
                acc = alpha * acc + jnp.dot(p.astype(BF16), vj, preferred_element_type=F32)
                out.append((m_new, l, acc))
            return tuple(out)

        init = tuple((jnp.full((tq, 1), -jnp.inf, F32), jnp.zeros((tq, 1), F32), jnp.zeros((tq, LANES), F32))
                     for _ in range(PAIR))
        carry = lax.fori_loop(0, i, functools.partial(step, diagonal=False), init)
        carry = step(i, carry, True)
        o = carry[PAIR - 1][2] / carry[PAIR - 1][1]
        for hh in range(PAIR - 1):
            o = jnp.where(masks[hh], carry[hh][2] / carry[hh][1], o)
        o_ref[...] = o
        lse = jnp.broadcast_to(carry[PAIR - 1][0] + jnp.log(carry[PAIR - 1][1]), (tq, LANES))
        for hh in range(PAIR - 1):
            lse = jnp.where(masks[hh], carry[hh][0] + jnp.log(carry[hh][1]), lse)
        lse_ref[...] = lse

    return _pcall(
        body, name="attn_fwd", grid=(H // PAIR, T // tq),
        in_specs=[pl.BlockSpec((tq, LANES), lambda hp, i: (i, qb + hp)),
                  pl.BlockSpec((T, LANES), lambda hp, i: (0, kb + hp)),
                  pl.BlockSpec((T, LANES), lambda hp, i: (0, vb + hp)),
                  pl.BlockSpec((tq, LANES), lambda hp, i: (i, 0)),
                  pl.BlockSpec((PAIR, nkv, 1, tk), lambda hp, i: (hp, 0, 0, 0))],
        out_specs=[pl.BlockSpec((tq, LANES), lambda hp, i: (i, hp)),
                   pl.BlockSpec((None, tq, LANES), lambda hp, i: (hp, i, 0))],
        out_shape=[jax.ShapeDtypeStruct((T, hd), F32), jax.ShapeDtypeStruct((H // PAIR, T, LANES), F32)],
        comm=comm,
    )(proj, proj, proj, cum, ck4)


def attn_bwd(proj, o, dycat, lse, cum, ck4, q_col, do_col, comm=None):
    T = proj.shape[0]
    H, nkv, _, tk = ck4.shape
    tq = tk
    nq = T // tq
    hd = H * HEAD_DIM
    qb, kb, vb = q_col // LANES, (q_col + hd) // LANES, (q_col + 2 * hd) // LANES
    dob = do_col // LANES
    scale = 1.0 / math.sqrt(HEAD_DIM)

    def body(q_ref, k_ref, v_ref, o_ref, do_ref, lse_ref, cum_ref, ck_ref,
             dq_ref, dk_ref, dv_ref, dcq_ref, dck_ref):
        hp = pl.program_id(0)
        j = pl.program_id(1)

        def block(i, diagonal):
            masks = _head_masks(tq)
            rows = pl.ds(pl.multiple_of(i * tq, tq), tq)
            q2, k2, v2, do2 = q_ref[rows, :] * scale, k_ref[...], v_ref[...], do_ref[rows, :]
            zero = jnp.zeros_like(q2)
            prod = do2.astype(F32) * o_ref[rows, :]
            cum_q, lse_q = cum_ref[rows, :], lse_ref[rows, :]
            dq_part = dk_part = dv_part = None
            dcq_part = jnp.zeros((tq, LANES), F32)
            lane = lax.broadcasted_iota(jnp.int32, (tq, LANES), 1)
            for hh in range(PAIR):
                qh = jnp.where(masks[hh], q2, zero)
                kh = jnp.where(masks[hh], k2, zero)
                doh = jnp.where(masks[hh], do2, zero)
                delta = jnp.sum(jnp.where(masks[hh], prod, 0.0), axis=-1, keepdims=True)
                s = lax.dot_general(qh, k2, NT, preferred_element_type=F32)
                s = s + _lane_column(cum_q, PAIR * hp + hh) - ck_ref[hh]
                if diagonal:
                    s = jnp.where(_causal(tq, tk), s, NEG_INF)
                p = jnp.exp(s - _lane_column(lse_q, hh * HEAD_DIM))
                dp = lax.dot_general(doh, v2, NT, preferred_element_type=F32)
                ds = p * (dp - delta)
                dsb = ds.astype(BF16)
                dv_h = lax.dot_general(p.astype(BF16), doh, TN, preferred_element_type=F32)
                dk_h = lax.dot_general(dsb, qh, TN, preferred_element_type=F32)
                dq_h = jnp.dot(dsb, kh, preferred_element_type=F32)
                dq_part = dq_h if dq_part is None else dq_part + dq_h
                dk_part = dk_h if dk_part is None else dk_part + dk_h
                dv_part = dv_h if dv_part is None else dv_part + dv_h
                dck_h = -jnp.sum(ds, axis=0, keepdims=True)
                dcq_part = jnp.where(lane == PAIR * hp + hh, jnp.sum(ds, axis=-1, keepdims=True), dcq_part)
                if diagonal:
                    dck_ref[hh] = dck_h
                else:
                    dck_ref[hh] += dck_h
            dq_part = dq_part * scale

            @pl.when(j == 0)
            def _():
                dq_ref[rows, :] = dq_part

            @pl.when(j > 0)
            def _():
                dq_ref[rows, :] += dq_part

            @pl.when(jnp.logical_and(hp == 0, j == 0))
            def _():
                dcq_ref[rows, :] = dcq_part

            @pl.when(jnp.logical_or(hp > 0, j > 0))
            def _():
                dcq_ref[rows, :] += dcq_part

            if diagonal:
                dk_ref[...] = dk_part
                dv_ref[...] = dv_part
            else:
                dk_ref[...] += dk_part
                dv_ref[...] += dv_part

        block(j, True)

        def later(i, carry):
            block(i, False)
            return carry

        lax.fori_loop(j + 1, nq, later, 0)

    at_q = lambda col: pl.BlockSpec((T, LANES), lambda hp, j: (0, col + hp))
    at_k = lambda col: pl.BlockSpec((tk, LANES), lambda hp, j: (j, col + hp))
    lse_spec = pl.BlockSpec((None, T, LANES), lambda hp, j: (hp, 0, 0))
    cum_spec = pl.BlockSpec((T, LANES), lambda hp, j: (0, 0))
    ck_spec = pl.BlockSpec((PAIR, None, 1, tk), lambda hp, j: (hp, j, 0, 0))
    return _pcall(
        body, name="attn_bwd", grid=(H // PAIR, nkv),
        in_specs=[at_q(qb), at_k(kb), at_k(vb), at_q(0), at_q(dob), lse_spec, cum_spec, ck_spec],
        out_specs=[pl.BlockSpec((T, LANES), lambda hp, j: (0, hp)), at_k(0), at_k(0),
                   pl.BlockSpec((T, LANES), lambda hp, j: (0, 0)), ck_spec],
        out_shape=[jax.ShapeDtypeStruct((T, hd), F32)] * 3
        + [jax.ShapeDtypeStruct((T, LANES), F32), jax.ShapeDtypeStruct((H, nkv, 1, tk), F32)],
        comm=comm,
    )(proj, proj, proj, o, dycat, lse, cum, ck4)


ELEMENTWISE_BLOCK_BYTES = 2 * 1024 * 1024
BF16_ROWS = 16


def cast_bf16(arrays, comm=None):
    def slab(a, steps):
        R, C = a.shape
        if R % (steps * BF16_ROWS) == 0:
            return pl.BlockSpec((R // steps, C), lambda i: (i, 0))
        if C % (steps * LANES) == 0:
            return pl.BlockSpec((R, C // steps), lambda i: (0, i))
        return None

    steps = 8 if all(slab(a, 8) is not None for a in arrays) else 4
    specs = [slab(a, steps) for a in arrays]
    n = len(arrays)

    def body(*refs):
        for src, dst in zip(refs[:n], refs[n:]):
            dst[...] = src[...].astype(BF16)

    return _pcall(body, name="cast_bf16", grid=(steps,), in_specs=specs, out_specs=specs,
                  out_shape=[jax.ShapeDtypeStruct(a.shape, BF16) for a in arrays], comm=comm)(*arrays)


def _ew_tiles(rows, cols, bytes_per_element):
    target = max(8, ELEMENTWISE_BLOCK_BYTES // max(1, cols * bytes_per_element))
    if rows <= target:
        return rows, cols
    t = (target // 16) * 16
    while t >= 16:
        if rows % t == 0:
            return t, cols
        t -= 16
    tc = _tile(cols, max(LANES, (ELEMENTWISE_BLOCK_BYTES // (rows * bytes_per_element)) // LANES * LANES))
    return rows, tc


def sum_chips(recv):
    nc, R, C = recv.shape
    tr, tc = _ew_tiles(R, C, 4)

    def body(r_ref, o_ref):
        acc = r_ref[0].astype(F32)
        for j in range(1, nc):
            acc = acc + r_ref[j].astype(F32)
        o_ref[...] = acc

    return _pcall(
        body, name="sum_chips", grid=(R // tr, C // tc),
        in_specs=[pl.BlockSpec((nc, tr, tc), lambda i, j: (0, i, j))],
        out_specs=[pl.BlockSpec((tr, tc), lambda i, j: (i, j))],
        out_shape=[jax.ShapeDtypeStruct((R, C), F32)],
    )(recv)[0]


def add_sibling_half(g, recv):
    nc, R, C = g.shape
    hr = R // 2
    tr, tc = _ew_tiles(hr, C, 4 * nc)

    def body(g_ref, r_ref, o_ref):
        c = lax.axis_index("c")
        for j in range(nc):
            o_ref[j] = (g_ref[j, c].astype(F32) + r_ref[j].astype(F32)).astype(BF16)

    return _pcall(
        body, name="add_sibling_half", grid=(hr // tr, C // tc),
        in_specs=[pl.BlockSpec((nc, 2, tr, tc), lambda i, j: (0, 0, i, j)),
                  pl.BlockSpec((nc, tr, tc), lambda i, j: (0, i, j))],
        out_specs=[pl.BlockSpec((nc, tr, tc), lambda i, j: (0, i, j))],
        out_shape=[jax.ShapeDtypeStruct((nc, hr, C), BF16)],
    )(g.reshape(nc, 2, hr, C), recv)[0]


def adamw(w, m, v, g_parts, comm=None, halves=False):
    R, C = w.shape
    tr, tc = _ew_tiles(R // 2 if halves else R, C, 4 * 4)
    n_g = len(g_parts)
    n_half = (R // 2) // tr
    c1 = 1.0 - ADAM_B1
    c2 = 1.0 - ADAM_B2
    bc1 = 1.0 - ADAM_B1 ** ADAM_STEP
    bc2 = 1.0 - ADAM_B2 ** ADAM_STEP

    def body(*refs):
        w_ref, m_ref, v_ref = refs[:3]
        g_refs = refs[3:3 + n_g]
        g_out, d_out, m_out, v_out = refs[3 + n_g:]
        if halves:
            mine = (pl.program_id(0) >= n_half) == (lax.axis_index("c") == 1)
            g = jnp.where(mine, g_refs[0][...], g_refs[1][...])
        else:
            g = g_refs[0][...]
            for r in g_refs[1:]:
                g = g + r[...]
        m_new = ADAM_B1 * m_ref[...] + c1 * g
        v_new = ADAM_B2 * v_ref[...] + c2 * (g * g)
        m_hat = m_new / bc1
        v_hat = v_new / bc2
        g_out[...] = g
        d_out[...] = -ADAM_LR * (m_hat / (jnp.sqrt(v_hat) + ADAM_EPS) + ADAM_WD * w_ref[...])
        m_out[...] = m_new
        v_out[...] = v_new

    spec = pl.BlockSpec((tr, tc), lambda i, j: (i, j))
    g_spec = pl.BlockSpec((tr, tc), lambda i, j: (i % n_half, j)) if halves else spec
    return _pcall(
        body, name="adamw", grid=(R // tr, C // tc),
        in_specs=[spec] * 3 + [g_spec] * n_g, out_specs=[spec] * 4,
        out_shape=[jax.ShapeDtypeStruct((R, C), F32)] * 4, comm=comm,
    )(w, m, v, *g_parts)


def _chip_coords():
    x, y, c = lax.axis_index("x"), lax.axis_index("y"), lax.axis_index("c")
    others = [(1 - x, y), (x, 1 - y), (1 - x, 1 - y)]
    return x, y, c, others


def _remote(src, dst, send_sem, recv_sem, device):
    return pltpu.make_async_remote_copy(src_ref=src, dst_ref=dst, send_sem=send_sem, recv_sem=recv_sem,
                                        device_id=device, device_id_type=MESH)


def gather_comm(shards):
    n = len(shards)
    SLOTS = 7

    def makers(ins, outs, sems):
        send_sems, recv_sems, local_sems = sems
        x, y, c, _ = _chip_coords()
        me, xn, yn, dg = 2 * x + y, 2 * (1 - x) + y, 2 * x + (1 - y), 2 * (1 - x) + (1 - y)
        to_x, to_y, sibling = (1 - x, y, c), (x, 1 - y, c), (x, y, 1 - c)

        def part(ref, a, half, quarter=None, chip=None):
            rows, cols = ins[a].shape[0], ins[a].shape[1]
            lead = () if chip is None else (chip,)
            along_rows = rows % (4 * BF16_ROWS) == 0 or (ins[a].dtype == F32 and rows % (4 * SUBLANES) == 0)
            size = (rows if along_rows else cols) // 2
            start = half * size
            if quarter is not None:
                size = size // 2
                start = start + quarter * size
            if along_rows:
                return ref.at[(*lead, pl.ds(start, size))]
            return ref.at[(*lead, slice(None), pl.ds(start, size))]

        def copy(a, k, src, dst, device):
            return _remote(src, dst, send_sems.at[SLOTS * a + k], recv_sems.at[SLOTS * a + k], device)

        def local(a):
            return pltpu.make_async_copy(ins[a], outs[a].at[me], local_sems.at[a])

        def first_leg(a):
            mine = part(outs[a], a, c, chip=me)
            return [copy(a, 0, part(ins[a], a, c), mine, to_x), copy(a, 1, part(ins[a], a, c), mine, to_y)]

        def arrived(a, k):
            region = {0: part(outs[a], a, c, chip=xn), 1: part(outs[a], a, c, chip=yn),
                      2: part(outs[a], a, c, 0, chip=dg), 3: part(outs[a], a, c, 1, chip=dg),
                      4: part(outs[a], a, 1 - c, chip=xn), 5: part(outs[a], a, 1 - c, chip=yn),
                      6: part(outs[a], a, 1 - c, chip=dg)}[k]
            return copy(a, k, region, region, sibling if k >= 4 else (to_x if k in (0, 3) else to_y))

        def relays(a):
            qx, qy = part(outs[a], a, c, 0, chip=xn), part(outs[a], a, c, 1, chip=yn)
            return [copy(a, 2, qx, qx, to_y), copy(a, 3, qy, qy, to_x)]

        def handover(a, k):
            region = part(outs[a], a, c, chip={4: xn, 5: yn, 6: dg}[k])
            return copy(a, k, region, region, sibling)

        return local, first_leg, arrived, relays, handover

    def start(ins, outs, sems):
        local, first_leg, _, _, _ = makers(ins, outs, sems)
        for a in range(n):
            for cp in first_leg(a):
                cp.start()
        for a in range(n):
            local(a).start()

    def relay(ins, outs, sems):
        _, _, arrived, relays, handover = makers(ins, outs, sems)
        for a in range(n):
            to_y_nbr, to_x_nbr = relays(a)
            arrived(a, 0).wait_recv()
            to_y_nbr.start()
            handover(a, 4).start()
            arrived(a, 1).wait_recv()
            to_x_nbr.start()
            handover(a, 5).start()

    def finish(ins, outs, sems):
        local, first_leg, arrived, relays, handover = makers(ins, outs, sems)
        for a in range(n):
            arrived(a, 2).wait_recv()
            arrived(a, 3).wait_recv()
            handover(a, 6).start()
        for a in range(n):
            for k in (4, 5, 6):
                arrived(a, k).wait_recv()
        for a in range(n):
            for cp in first_leg(a) + relays(a) + [handover(a, k) for k in (4, 5, 6)]:
                cp.wait_send()
            local(a).wait()

    return Comm(shards, [jax.ShapeDtypeStruct((N_CHIP,) + s.shape, s.dtype) for s in shards],
                [pltpu.SemaphoreType.DMA((SLOTS * n,)), pltpu.SemaphoreType.DMA((SLOTS * n,)),
                 pltpu.SemaphoreType.DMA((n,))], start, finish, relay)


def scatter_comm(grads):
    n = len(grads)
    pieces = [(a, jj) for a in range(n) for jj in range(3)]

    def makers(ins, outs, sems):
        send_sems, recv_sems, local_sems = sems
        x, y, c, others = _chip_coords()
        me = 2 * x + y

        def local(a):
            return pltpu.make_async_copy(ins[a].at[me], outs[a].at[me], local_sems.at[a])

        def ici(a, jj):
            ox, oy = others[jj]
            return _remote(ins[a].at[2 * ox + oy], outs[a].at[me], send_sems.at[3 * a + jj],
                           recv_sems.at[3 * a + jj], (ox, oy, c))

        def landed(a, jj):
            ox, oy = others[jj]
            slot = outs[a].at[2 * ox + oy]
            return _remote(slot, slot, send_sems.at[3 * a + jj], recv_sems.at[3 * a + jj], (ox, oy, c))

        return local, ici, landed

    def start(ins, outs, sems):
        local, ici, _ = makers(ins, outs, sems)
        for a in range(n):
            for jj in (2, 0, 1):
                ici(a, jj).start()
        for a in range(n):
            local(a).start()

    def finish(ins, outs, sems):
        local, ici, landed = makers(ins, outs, sems)
        for a, jj in pieces:
            landed(a, jj).wait_recv()
        for a, jj in pieces:
            ici(a, jj).wait_send()
        for a in range(n):
            local(a).wait()

    return Comm(grads, [jax.ShapeDtypeStruct(g.shape, g.dtype) for g in grads],
                [pltpu.SemaphoreType.DMA((3 * n,)), pltpu.SemaphoreType.DMA((3 * n,)),
                 pltpu.SemaphoreType.DMA((n,))], start, finish)


def halfswap_comm(grads):
    n = len(grads)

    def copies(ins, outs, sems):
        send_sems, recv_sems = sems
        x, y, c, _ = _chip_coords()
        out = []
        for a in range(n):
            hr = ins[a].shape[1] // 2
            out.append(_remote(ins[a].at[:, pl.ds((1 - c) * hr, hr)], outs[a], send_sems.at[a], recv_sems.at[a],
                               (x, y, 1 - c)))
        return out

    def start(ins, outs, sems):
        for cp in copies(ins, outs, sems):
            cp.start()

    def finish(ins, outs, sems):
        for cp in copies(ins, outs, sems):
            cp.wait()

    return Comm(grads, [jax.ShapeDtypeStruct((g.shape[0], g.shape[1] // 2, g.shape[2]), g.dtype) for g in grads],
                [pltpu.SemaphoreType.DMA((n,)), pltpu.SemaphoreType.DMA((n,))], start, finish)


def join_comms(first, second):
    ni, no, ns = len(first.operands), len(first.out_shape), len(first.sems)

    def start(ins, outs, sems):
        first.start(ins[:ni], outs[:no], sems[:ns])
        second.start(ins[ni:], outs[no:], sems[ns:])

    def finish(ins, outs, sems):
        first.finish(ins[:ni], outs[:no], sems[:ns])
        second.finish(ins[ni:], outs[no:], sems[ns:])

    def relay(ins, outs, sems):
        if first.relay is not None:
            first.relay(ins[:ni], outs[:no], sems[:ns])
        if second.relay is not None:
            second.relay(ins[ni:], outs[no:], sems[ns:])

    return Comm(first.operands + second.operands, first.out_shape + second.out_shape, first.sems + second.sems,
                start, finish, relay if (first.relay or second.relay) else None)


def swap_comm(parts):
    n = len(parts)

    def copies(ins, outs, sems):
        send_sems, recv_sems = sems
        x, y, c, _ = _chip_coords()
        return [_remote(ins[a], outs[a], send_sems.at[a], recv_sems.at[a], (x, y, 1 - c)) for a in range(n)]

    def start(ins, outs, sems):
        for cp in copies(ins, outs, sems):
            cp.start()

    def finish(ins, outs, sems):
        for cp in copies(ins, outs, sems):
            cp.wait()

    return Comm(parts, [jax.ShapeDtypeStruct(p.shape, p.dtype) for p in parts],
                [pltpu.SemaphoreType.DMA((n,)), pltpu.SemaphoreType.DMA((n,))], start, finish)


def allreduce_small(v):
    R = v.shape[0]

    def body(v_ref, sum_ref, all_ref, send_sems, recv_sems):
        x, y, c = lax.axis_index("x"), lax.axis_index("y"), lax.axis_index("c")
        me = 4 * x + 2 * y + c
        all_ref[me] = v_ref[...]
        copies = []
        for k in range(1, N_DEV):
            px = 1 - x if k & 4 else x
            py = 1 - y if k & 2 else y
            pc = 1 - c if k & 1 else c
            cp = pltpu.make_async_remote_copy(
                src_ref=v_ref, dst_ref=all_ref.at[me], send_sem=send_sems.at[k - 1], recv_sem=recv_sems.at[k - 1],
                device_id=(px, py, pc), device_id_type=MESH)
            cp.start()
            copies.append((cp, 4 * px + 2 * py + pc))
        for k, (cp, peer) in enumerate(copies):
            pltpu.make_async_remote_copy(
                src_ref=v_ref, dst_ref=all_ref.at[peer], send_sem=send_sems.at[k], recv_sem=recv_sems.at[k],
                device_id=(x, y, c), device_id_type=MESH).wait_recv()
        for cp, _ in copies:
            cp.wait_send()
        acc = all_ref[0]
        for d in range(1, N_DEV):
            acc = acc + all_ref[d]
        sum_ref[...] = acc

    vm = pl.BlockSpec(memory_space=pltpu.VMEM)
    return pl.pallas_call(
        body, name="allreduce_small",
        in_specs=[vm], out_specs=[vm, vm],
        out_shape=[jax.ShapeDtypeStruct((R, LANES), F32), jax.ShapeDtypeStruct((N_DEV, R, LANES), F32)],
        scratch_shapes=[pltpu.SemaphoreType.DMA((N_DEV - 1,)), pltpu.SemaphoreType.DMA((N_DEV - 1,))],
    )(v)[0]


SMALL_NAMES = ("ffn1_norm", "mix_norm", "ffn2_norm", "final_norm", "conv_b", "conv_ln_g", "conv_ln_b")


def _pack_small(vecs, bias, conv_w_rows, loss_tile):
    rows = [vecs[n].reshape(-1, LANES) for n in SMALL_NAMES]
    rows.append(bias.reshape(1, LANES))
    rows.append(conv_w_rows.reshape(-1, LANES))
    rows.append(loss_tile[0:1, :])
    packed = jnp.concatenate(rows, axis=0)
    pad = (-packed.shape[0]) % 8
    return jnp.pad(packed, ((0, pad), (0, 0)))


def _unpack_small(packed, sizes, n_conv_rows):
    out, r = {}, 0
    for n in SMALL_NAMES:
        k = sizes[n] // LANES
        out[n] = packed[r:r + k].reshape(-1)
        r += k
    out["fgate_bias"] = packed[r]
    r += 1
    out["conv_w"] = packed[r:r + n_conv_rows]
    r += n_conv_rows
    out["loss"] = packed[r, 0]
    return out


def kernel(x, ffn1_norm, ffn1_w_gate, ffn1_w_up, ffn1_w_down, mix_norm, w_in, fgate_bias, conv_w, conv_b, conv_ln_g, conv_ln_b, w_out, ffn2_norm, ffn2_w_gate, ffn2_w_up, ffn2_w_down, final_norm, loss_target, m_ffn1_norm, m_ffn1_w_gate, m_ffn1_w_up, m_ffn1_w_down, m_mix_norm, m_w_in, m_fgate_bias, m_conv_w, m_conv_b, m_conv_ln_g, m_conv_ln_b, m_w_out, m_ffn2_norm, m_ffn2_w_gate, m_ffn2_w_up, m_ffn2_w_down, m_final_norm, v_ffn1_norm, v_ffn1_w_gate, v_ffn1_w_up, v_ffn1_w_down, v_mix_norm, v_w_in, v_fgate_bias, v_conv_w, v_conv_b, v_conv_ln_g, v_conv_ln_b, v_w_out, v_ffn2_norm, v_ffn2_w_gate, v_ffn2_w_up, v_ffn2_w_down, v_final_norm):
    w = dict(ffn1_norm=ffn1_norm, ffn1_w_gate=ffn1_w_gate, ffn1_w_up=ffn1_w_up, ffn1_w_down=ffn1_w_down,
             mix_norm=mix_norm, w_in=w_in, fgate_bias=fgate_bias, conv_w=conv_w, conv_b=conv_b,
             conv_ln_g=conv_ln_g, conv_ln_b=conv_ln_b, w_out=w_out, ffn2_norm=ffn2_norm,
             ffn2_w_gate=ffn2_w_gate, ffn2_w_up=ffn2_w_up, ffn2_w_down=ffn2_w_down, final_norm=final_norm)
    m = dict(ffn1_norm=m_ffn1_norm, ffn1_w_gate=m_ffn1_w_gate, ffn1_w_up=m_ffn1_w_up, ffn1_w_down=m_ffn1_w_down,
             mix_norm=m_mix_norm, w_in=m_w_in, fgate_bias=m_fgate_bias, conv_w=m_conv_w, conv_b=m_conv_b,
             conv_ln_g=m_conv_ln_g, conv_ln_b=m_conv_ln_b, w_out=m_w_out, ffn2_norm=m_ffn2_norm,
             ffn2_w_gate=m_ffn2_w_gate, ffn2_w_up=m_ffn2_w_up, ffn2_w_down=m_ffn2_w_down, final_norm=m_final_norm)
    v = dict(ffn1_norm=v_ffn1_norm, ffn1_w_gate=v_ffn1_w_gate, ffn1_w_up=v_ffn1_w_up, ffn1_w_down=v_ffn1_w_down,
             mix_norm=v_mix_norm, w_in=v_w_in, fgate_bias=v_fgate_bias, conv_w=v_conv_w, conv_b=v_conv_b,
             conv_ln_g=v_conv_ln_g, conv_ln_b=v_conv_ln_b, w_out=v_w_out, ffn2_norm=v_ffn2_norm,
             ffn2_w_gate=v_ffn2_w_gate, ffn2_w_up=v_ffn2_w_up, ffn2_w_down=v_ffn2_w_down, final_norm=v_final_norm)
    names = list(w.keys())
    big = ("ffn1_w_gate", "ffn1_w_up", "ffn1_w_down", "w_in", "w_out", "ffn2_w_gate", "ffn2_w_up", "ffn2_w_down")

    T, D = x.shape[1], x.shape[2]
    C = conv_b.shape[0]
    H = fgate_bias.shape[0]
    cs = conv_w.shape[1]
    in_cols = N_CHIP * w_in.shape[1]
    p_main = in_cols - H

    x0, tgt = x[0], loss_target[0]
    tk = _tile(T, 512, 128)
    nkv = T // tk
    row = lambda a: a.reshape(1, -1)
    grad, delta, new_m, new_v = {}, {}, {}, {}

    def update(n, parts, comm=None, halves=False):
        args = (w[n], m[n], v[n])
        if n == "w_in":
            outs = [t.T for t in adamw(*[a.T for a in args], parts, comm=comm)]
        else:
            outs = adamw(*args, parts, comm=comm, halves=halves)
        grad[n], delta[n], new_m[n], new_v[n] = outs

    rest = [n for n in big if n != "ffn1_w_gate"]
    g0 = gather_comm([w["ffn1_w_gate"].astype(BF16), jnp.pad(conv_w, ((0, HALO - CONV_K), (0, 0)))])
    wb = dict(zip(rest, cast_bf16([w[n].T if n == "w_in" else w[n] for n in rest], comm=g0)))
    wg1, conv_w4 = g0.results
    conv_w_full = conv_w4.transpose(1, 0, 2).reshape(HALO, C)
    h1, r1 = rms_fwd(x0, row(ffn1_norm))
    g1a = gather_comm([wb["ffn1_w_up"]])
    a1 = ffn_gate(h1, wg1, comm=g1a)
    wu1 = g1a.results[0]
    g1b = gather_comm([wb["ffn1_w_down"]])
    b1, mid1 = ffn_upmul(h1, wu1, a1, comm=g1b)
    wd1 = g1b.results[0]
    g2 = gather_comm([wb["w_in"]])
    x1 = mm_residual("ffn_down_g", mid1, wd1, x0, 0.5, comm=g2)[0]
    w_t = g2.results[0].reshape(in_cols, D)

    wf_t = jnp.pad(w_t[p_main:], ((0, LANES - H), (0, 0)))
    bias_pad = jnp.pad(row(fgate_bias), ((0, 0), (0, LANES - H)))
    h2, r2 = rms_fwd(x1, row(mix_norm))
    proj = proj_main(h2, w_t, p_main)
    f, cum = fgate_fwd(h2, wf_t, bias_pad, H)
    ypre, yconv = conv_fwd(proj, conv_w_full, row(conv_b), row(conv_ln_g), row(conv_ln_b))
    ck4 = cum[:, :H].T.reshape(H, nkv, 1, tk)
    g3 = gather_comm([wb["w_out"], wb["ffn2_w_gate"], wb["ffn2_w_up"], wb["ffn2_w_down"]])
    o, lse = attn_fwd(proj, cum, ck4, 2 * C, comm=g3)
    w_out3, wg2, wu2, wd2 = g3.results
    ycat = jnp.concatenate([yconv, o.astype(BF16)], axis=1)
    x2 = mm_residual("out_proj", ycat, w_out3.reshape(2, -1, D), x1, 1.0)[0]

    h3, r3 = rms_fwd(x2, row(ffn2_norm))
    a2, b2, mid2 = ffn_up(h3, wg2, wu2)
    x3 = mm_residual("ffn_down", mid2, wd2, x2, 0.5)[0]
    dx3, dx3b, loss_tile, d_final = final_loss(x3, tgt, row(final_norm))

    da2, db2 = ffn_bwd_mid(dx3b, wd2, a2, b2)
    dwd2 = dw_rowshard("ffn_dwd", mid2, dx3b, N_CHIP)[0]
    s1 = scatter_comm([dwd2])
    dwg2, dwu2 = dw_colshard("ffn_dwgu_s", h3, [da2, db2], N_CHIP, comm=s1)
    s2 = scatter_comm([dwg2])
    dh3 = ffn_dh(da2, db2, wg2, wu2, comm=s2)
    dx2, dx2b, d_ffn2_norm = rms_bwd(dh3, x2, r3, row(ffn2_norm), dx3, 1.0)

    dycat = mm_nt_bf16("out_proj_dy", dx2b, w_out3.reshape(-1, D))
    dw_out3 = dw_rowshard("out_proj_dw", ycat, dx2b, N_CHIP)[0]
    s3 = scatter_comm([dwu2, dw_out3])
    dq, dk, dv, dcq, dck4 = attn_bwd(proj, o, dycat, lse, cum, ck4, 2 * C, C, comm=s3)
    dc = dcq + jnp.pad(dck4.reshape(H, T).T, ((0, 0), (0, LANES - H)))
    df, d_bias = fgate_bwd(dc, f, H)
    dag, d_conv_w, d_conv_b, d_ln_g, d_ln_b = conv_bwd(proj, ypre, dycat, conv_w_full, row(conv_ln_g),
                                                       row(conv_ln_b))
    dproj = jnp.concatenate([dag, dq.astype(BF16), dk.astype(BF16), dv.astype(BF16)], axis=1)
    early = ("ffn2_w_down", "ffn2_w_gate", "ffn2_w_up", "w_out")
    early_sums = [sum_chips(r) for r in (s1.results[0], s2.results[0], s3.results[0], s3.results[1])]
    sw1 = swap_comm(early_sums)
    dh2 = proj_dh(dproj, w_t, df, wf_t, comm=sw1)
    dw_t, dwf_t = proj_dw(dproj, df, h2, in_cols)
    dw_t = lax.dynamic_update_slice(dw_t, dwf_t[:H].astype(BF16), (p_main, 0))
    dw_in3 = dw_t.reshape(N_CHIP, in_cols // N_CHIP, D)
    dx1, dx1b, d_mix_norm = rms_bwd(dh2, x1, r2, row(mix_norm), dx2, 0.5)

    s4 = scatter_comm([dw_in3])
    da1, db1 = ffn_bwd_mid(dx1b, wd1, a1, b1, comm=s4)
    dwd1 = dw_rowshard("ffn_dwd", mid1, dx1b, N_CHIP)[0]
    s5 = scatter_comm([dwd1])
    dwg1, dwu1 = dw_colshard("ffn_dwgu_s", h1, [da1, db1], N_CHIP, comm=s5)
    mid_sums = [sum_chips(s4.results[0]), sum_chips(s5.results[0])]
    s6 = join_comms(join_comms(scatter_comm([dwg1]), halfswap_comm([dwu1])), swap_comm(mid_sums))
    dh1 = ffn_dh(da1, db1, wg1, wu1, comm=s6)
    recv_g1, sibling_u1, their_in, their_d1 = s6.results
    grad_x, _, d_ffn1_norm = rms_bwd(dh1, x0, r1, row(ffn1_norm), dx1, 1.0)

    s7 = scatter_comm([add_sibling_half(dwu1, sibling_u1)])
    for i, (n, mine, other) in enumerate(zip(early, early_sums, sw1.results)):
        update(n, [mine, other], comm=s7 if i == 0 else None)
    update("w_in", [mid_sums[0], their_in])
    update("ffn1_w_down", [mid_sums[1], their_d1])
    sum_g1, half_u1 = sum_chips(recv_g1), sum_chips(s7.results[0])
    their_g1, their_u1 = _run_comm("swap_last", swap_comm([sum_g1, half_u1]))
    update("ffn1_w_gate", [sum_g1, their_g1])
    update("ffn1_w_up", [half_u1, their_u1], halves=True)

    gl = dict(ffn1_norm=d_ffn1_norm, mix_norm=d_mix_norm, ffn2_norm=d_ffn2_norm, final_norm=d_final,
              conv_b=d_conv_b, conv_ln_g=d_ln_g, conv_ln_b=d_ln_b)
    small_sizes = {n: w[n].shape[0] for n in SMALL_NAMES}
    packed = _pack_small(gl, d_bias, d_conv_w, loss_tile)
    red = _unpack_small(allreduce_small(packed), small_sizes, HALO * C // LANES)
    loss = red["loss"]
    my_chip = 2 * lax.axis_index("x") + lax.axis_index("y")
    g_conv_w = lax.dynamic_slice_in_dim(red["conv_w"].reshape(HALO, C)[:CONV_K], my_chip * cs, cs, axis=1)
    update("conv_w", [g_conv_w])
    vec_names = SMALL_NAMES + ("fgate_bias",)
    stack = lambda d: jnp.concatenate(
        [jnp.pad(d[n], (0, (-d[n].shape[0]) % LANES)).reshape(-1, LANES) for n in vec_names], axis=0)
    g_stack = jnp.concatenate([red[n].reshape(-1, LANES) for n in SMALL_NAMES] + [red["fgate_bias"][None, :]],
                              axis=0)
    outs = adamw(stack(w), stack(m), stack(v), [g_stack])
    r = 0
    for n in vec_names:
        size = w[n].shape[0]
        k = -(-size // LANES)
        for dst, src in zip((grad, delta, new_m, new_v), outs):
            dst[n] = src[r:r + k].reshape(-1)[:size]
        r += k

    return (loss, grad_x[None], *[grad[n] for n in names], *[delta[n] for n in names],
            *[new_m[n] for n in names], *[new_v[n] for n in names])
```

```python
import functools
import math

import jax
import jax.numpy as jnp
from jax import lax
from jax.experimental import pallas as pl
from jax.experimental.pallas import tpu as pltpu

F32 = jnp.float32
BF16 = jnp.bfloat16
NORM_EPS = 1e-6
LN_EPS = 1e-5
NEG_INF = -1e30
HEAD_DIM = 64
CONV_K = 31
HALO = 32
LANES = 128
N_CHIP = 4
N_DEV = 8
VMEM_LIMIT = 52 * 1024 * 1024
MESH = pl.DeviceIdType.MESH

ADAM_LR = 0.001
ADAM_B1 = 0.9
ADAM_B2 = 0.999
ADAM_EPS = 1e-08
ADAM_WD = 0.01
ADAM_STEP = 10

NN = (((1,), (0,)), ((), ()))
NT = (((1,), (1,)), ((), ()))
TN = (((0,), (0,)), ((), ()))


def _tile(n, pref, unit=128):
    if n <= pref:
        return n
    t = (pref // unit) * unit
    while t > 0:
        if n % t == 0:
            return t
        t -= unit
    raise ValueError(f"no tile for {n} under {pref}")


RELAY_AT = (3, 4)


class Comm:
    def __init__(self, operands, out_shape, sems, start, finish, relay=None):
        self.operands, self.out_shape, self.sems = list(operands), list(out_shape), list(sems)
        self.start, self.finish, self.relay = start, finish, relay
        self.results = None


def _pcall(body, *, name, grid, in_specs, out_specs, out_shape, scratch=(), comm=None):
    params = pltpu.CompilerParams(dimension_semantics=("arbitrary",) * len(grid), vmem_limit_bytes=VMEM_LIMIT)
    scratch = list(scratch)
    if comm is None:
        return pl.pallas_call(body, name=name, grid=grid, in_specs=in_specs, out_specs=out_specs,
                              out_shape=out_shape, scratch_shapes=scratch, compiler_params=params)
    n_in, n_out, n_s = len(in_specs), len(out_shape), len(scratch)
    n_ci, n_co = len(comm.operands), len(comm.out_shape)
    any_spec = pl.BlockSpec(memory_space=pl.ANY)

    def carried(*refs):
        ins, refs = refs[:n_in], refs[n_in:]
        c_ins, refs = refs[:n_ci], refs[n_ci:]
        outs, refs = refs[:n_out], refs[n_out:]
        c_outs, refs = refs[:n_co], refs[n_co:]
        scr, c_sems = refs[:n_s], refs[n_s:]
        step = pl.program_id(0)
        for d in range(1, len(grid)):
            step = step * grid[d] + pl.program_id(d)
        total = math.prod(grid)
        first, last = step == 0, step == total - 1

        @pl.when(first)
        def _():
            comm.start(c_ins, c_outs, c_sems)

        if comm.relay is not None:
            @pl.when(step == min(total - 1, (RELAY_AT[0] * total) // RELAY_AT[1]))
            def _():
                comm.relay(c_ins, c_outs, c_sems)

        body(*ins, *outs, *scr)

        @pl.when(last)
        def _():
            comm.finish(c_ins, c_outs, c_sems)

    call = pl.pallas_call(
        carried, name=name, grid=grid, in_specs=list(in_specs) + [any_spec] * n_ci,
        out_specs=list(out_specs) + [any_spec] * n_co, out_shape=list(out_shape) + comm.out_shape,
        scratch_shapes=scratch + comm.sems, compiler_params=params)

    def run(*operands):
        res = call(*operands, *comm.operands)
        comm.results = list(res[n_out:])
        return list(res[:n_out])

    return run


def _run_comm(name, comm):
    n_ci, n_co = len(comm.operands), len(comm.out_shape)
    any_spec = pl.BlockSpec(memory_space=pl.ANY)

    def body(*refs):
        c_ins, c_outs, c_sems = refs[:n_ci], refs[n_ci:n_ci + n_co], refs[n_ci + n_co:]
        comm.start(c_ins, c_outs, c_sems)
        if comm.relay is not None:
            comm.relay(c_ins, c_outs, c_sems)
        comm.finish(c_ins, c_outs, c_sems)

    return pl.pallas_call(body, name=name, in_specs=[any_spec] * n_ci, out_specs=[any_spec] * n_co,
                          out_shape=comm.out_shape, scratch_shapes=comm.sems)(*comm.operands)


def _sigmoid(x):
    return 1.0 / (1.0 + jnp.exp(-x))


def _mm(name, *, grid, pairs, once_pairs=(), extra=(), out_shape, out_specs, acc_shapes, nk, kaxis, epilogue,
        comm=None):
    all_pairs = list(pairs) + list(once_pairs)
    n_p, n_o = len(pairs), len(once_pairs)
    n_e, n_out, n_acc = len(extra), len(out_shape), len(acc_shapes)

    def body(*refs):
        ab = refs[: 2 * (n_p + n_o)]
        ex = refs[2 * (n_p + n_o): 2 * (n_p + n_o) + n_e]
        outs = refs[2 * (n_p + n_o) + n_e: 2 * (n_p + n_o) + n_e + n_out]
        accs = refs[2 * (n_p + n_o) + n_e + n_out:]

        def dots(idx_range):
            vals = [None] * n_acc
            for p in idx_range:
                d = lax.dot_general(ab[2 * p][...], ab[2 * p + 1][...], all_pairs[p][4],
                                    preferred_element_type=F32)
                ai = all_pairs[p][5]
                vals[ai] = d if vals[ai] is None else vals[ai] + d
            return vals

        if nk == 1:
            vals = dots(range(n_p + n_o))
            epilogue(vals, ex, outs)
            return

        k = pl.program_id(kaxis)

        @pl.when(k == 0)
        def _():
            vals = dots(range(n_p + n_o))
            for ai in range(n_acc):
                accs[ai][...] = vals[ai]

        @pl.when(k > 0)
        def _():
            vals = dots(range(n_p))
            for ai in range(n_acc):
                if vals[ai] is not None:
                    accs[ai][...] += vals[ai]

        @pl.when(k == nk - 1)
        def _():
            epilogue([a[...] for a in accs], ex, outs)

    operands, in_specs = [], []
    for p in all_pairs:
        operands += [p[0], p[2]]
        in_specs += [p[1], p[3]]
    for arr, spec in extra:
        operands.append(arr)
        in_specs.append(spec)
    scratch = [pltpu.VMEM(s, F32) for s in acc_shapes] if nk > 1 else []
    return _pcall(body, name=name, grid=grid, in_specs=in_specs, out_specs=out_specs, out_shape=out_shape,
                  scratch=scratch, comm=comm)(*operands)


def rms_fwd(x, g):
    T, D = x.shape
    tt = _tile(T, 512, 8)

    def body(x_ref, g_ref, h_ref, r_ref):
        xv = x_ref[...]
        r = lax.rsqrt(jnp.mean(xv * xv, axis=-1, keepdims=True) + NORM_EPS)
        h_ref[...] = (xv * r * g_ref[...]).astype(BF16)
        r_ref[...] = r

    return _pcall(
        body, name="rms_fwd", grid=(T // tt,),
        in_specs=[pl.BlockSpec((tt, D), lambda i: (i, 0)), pl.BlockSpec((1, D), lambda i: (0, 0))],
        out_specs=[pl.BlockSpec((tt, D), lambda i: (i, 0)), pl.BlockSpec((tt, 1), lambda i: (i, 0))],
        out_shape=[jax.ShapeDtypeStruct((T, D), BF16), jax.ShapeDtypeStruct((T, 1), F32)],
    )(x, g)


def rms_bwd(dh, x, r, g, dres, out_scale):
    T, D = x.shape
    tt = _tile(T, 256, 8)

    def body(dh_ref, x_ref, r_ref, g_ref, dres_ref, dx_ref, dxb_ref, dg_ref):
        i = pl.program_id(0)
        xh = x_ref[...] * r_ref[...]
        dhv = dh_ref[...]
        dxh = dhv * g_ref[...]
        dx = dres_ref[...] + r_ref[...] * (dxh - xh * jnp.mean(dxh * xh, axis=-1, keepdims=True))
        dx_ref[...] = dx
        dxb_ref[...] = (out_scale * dx).astype(BF16)
        part = jnp.sum(dhv * xh, axis=0, keepdims=True)

        @pl.when(i == 0)
        def _():
            dg_ref[...] = part

        @pl.when(i > 0)
        def _():
            dg_ref[...] += part

    row = pl.BlockSpec((tt, D), lambda i: (i, 0))
    return _pcall(
        body, name="rms_bwd", grid=(T // tt,),
        in_specs=[row, row, pl.BlockSpec((tt, 1), lambda i: (i, 0)), pl.BlockSpec((1, D), lambda i: (0, 0)), row],
        out_specs=[row, row, pl.BlockSpec((1, D), lambda i: (0, 0))],
        out_shape=[jax.ShapeDtypeStruct((T, D), F32), jax.ShapeDtypeStruct((T, D), BF16),
                   jax.ShapeDtypeStruct((1, D), F32)],
    )(dh, x, r, g, dres)


def final_loss(x, tgt, g):
    T, D = x.shape
    tt = _tile(T, 256, 8)

    def body(x_ref, t_ref, g_ref, dx_ref, dxb_ref, loss_ref, dg_ref):
        i = pl.program_id(0)
        xv = x_ref[...]
        r = lax.rsqrt(jnp.mean(xv * xv, axis=-1, keepdims=True) + NORM_EPS)
        xh = xv * r
        err = xh * g_ref[...] - t_ref[...]
        part_loss = 0.5 * jnp.sum(jnp.mean(err * err, axis=-1, keepdims=True), axis=0, keepdims=True)
        dy = err * (1.0 / D)
        dxh = dy * g_ref[...]
        dx = r * (dxh - xh * jnp.mean(dxh * xh, axis=-1, keepdims=True))
        dx_ref[...] = dx
        dxb_ref[...] = (0.5 * dx).astype(BF16)
        part_g = jnp.sum(dy * xh, axis=0, keepdims=True)
        part_l = jnp.broadcast_to(part_loss, (8, LANES))

        @pl.when(i == 0)
        def _():
            dg_ref[...] = part_g
            loss_ref[...] = part_l

        @pl.when(i > 0)
        def _():
            dg_ref[...] += part_g
            loss_ref[...] += part_l

    row = pl.BlockSpec((tt, D), lambda i: (i, 0))
    return _pcall(
        body, name="final_loss", grid=(T // tt,),
        in_specs=[row, row, pl.BlockSpec((1, D), lambda i: (0, 0))],
        out_specs=[row, row, pl.BlockSpec((8, LANES), lambda i: (0, 0)), pl.BlockSpec((1, D), lambda i: (0, 0))],
        out_shape=[jax.ShapeDtypeStruct((T, D), F32), jax.ShapeDtypeStruct((T, D), BF16),
                   jax.ShapeDtypeStruct((8, LANES), F32), jax.ShapeDtypeStruct((1, D), F32)],
    )(x, tgt, g)


def ffn_gate(h, wg3, comm=None):
    T, D = h.shape
    nc, _, fs = wg3.shape
    tm = _tile(T, 512, 8)

    def epilogue(vals, ex, outs):
        outs[0][...] = vals[0].astype(BF16)

    return _mm("ffn_gate", grid=(nc, T // tm),
               pairs=[(h, pl.BlockSpec((tm, D), lambda j, i: (i, 0)),
                       wg3, pl.BlockSpec((None, D, fs), lambda j, i: (j, 0, 0)), NN, 0)],
               out_shape=[jax.ShapeDtypeStruct((T, nc * fs), BF16)],
               out_specs=[pl.BlockSpec((tm, fs), lambda j, i: (i, j))],
               acc_shapes=[(tm, fs)], nk=1, kaxis=None, epilogue=epilogue, comm=comm)[0]


def ffn_upmul(h, wu3, a, comm=None):
    T, D = h.shape
    nc, _, fs = wu3.shape
    tm = _tile(T, 512, 8)

    def epilogue(vals, ex, outs):
        b = vals[0]
        av = ex[0][...].astype(F32)
        outs[0][...] = b.astype(BF16)
        outs[1][...] = (av * _sigmoid(av) * b).astype(BF16)

    t_spec = pl.BlockSpec((tm, fs), lambda j, i: (i, j))
    o_shape = jax.ShapeDtypeStruct((T, nc * fs), BF16)
    return _mm("ffn_upmul", grid=(nc, T // tm),
               pairs=[(h, pl.BlockSpec((tm, D), lambda j, i: (i, 0)),
                       wu3, pl.BlockSpec((None, D, fs), lambda j, i: (j, 0, 0)), NN, 0)],
               extra=[(a, t_spec)], out_shape=[o_shape] * 2, out_specs=[t_spec] * 2,
               acc_shapes=[(tm, fs)], nk=1, kaxis=None, epilogue=epilogue, comm=comm)


def ffn_up(h, wg3, wu3, comm=None):
    T, D = h.shape
    nc, _, fs = wg3.shape
    tm = _tile(T, 512, 8)

    def epilogue(vals, ex, outs):
        a, b = vals
        outs[0][...] = a.astype(BF16)
        outs[1][...] = b.astype(BF16)
        outs[2][...] = (a * _sigmoid(a) * b).astype(BF16)

    h_spec = pl.BlockSpec((tm, D), lambda j, i: (i, 0))
    w_spec = pl.BlockSpec((None, D, fs), lambda j, i: (j, 0, 0))
    o_spec = pl.BlockSpec((tm, fs), lambda j, i: (i, j))
    o_shape = jax.ShapeDtypeStruct((T, nc * fs), BF16)
    return _mm("ffn_up", grid=(nc, T // tm),
               pairs=[(h, h_spec, wg3, w_spec, NN, 0), (h, h_spec, wu3, w_spec, NN, 1)],
               out_shape=[o_shape] * 3, out_specs=[o_spec] * 3, acc_shapes=[(tm, fs)] * 2, nk=1, kaxis=None,
               epilogue=epilogue, comm=comm)


def mm_residual(name, a, b3, res, scale, comm=None):
    T = a.shape[0]
    nk, tk, N = b3.shape
    tm, tn = _tile(T, 512, 8), _tile(N, 2048)

    def epilogue(vals, ex, outs):
        outs[0][...] = ex[0][...] + scale * vals[0]

    return _mm(name, grid=(T // tm, N // tn, nk),
               pairs=[(a, pl.BlockSpec((tm, tk), lambda i, n, k: (i, k)),
                       b3, pl.BlockSpec((None, tk, tn), lambda i, n, k: (k, 0, n)), NN, 0)],
               extra=[(res, pl.BlockSpec((tm, tn), lambda i, n, k: (i, n)))],
               out_shape=[jax.ShapeDtypeStruct((T, N), F32)],
               out_specs=[pl.BlockSpec((tm, tn), lambda i, n, k: (i, n))],
               acc_shapes=[(tm, tn)], nk=nk, kaxis=2, epilogue=epilogue, comm=comm)


def ffn_bwd_mid(dout, wd3, a, b, comm=None):
    T, D = dout.shape
    nc, fs, _ = wd3.shape
    tm = _tile(T, 512, 8)

    def epilogue(vals, ex, outs):
        dm = vals[0]
        av = ex[0][...].astype(F32)
        bv = ex[1][...].astype(F32)
        s = _sigmoid(av)
        outs[0][...] = (dm * bv * (s * (1.0 + av * (1.0 - s)))).astype(BF16)
        outs[1][...] = (dm * (av * s)).astype(BF16)

    t_spec = pl.BlockSpec((tm, fs), lambda j, i: (i, j))
    o_shape = jax.ShapeDtypeStruct((T, nc * fs), BF16)
    return _mm("ffn_bwd_mid", grid=(nc, T // tm),
               pairs=[(dout, pl.BlockSpec((tm, D), lambda j, i: (i, 0)),
                       wd3, pl.BlockSpec((None, fs, D), lambda j, i: (j, 0, 0)), NT, 0)],
               extra=[(a, t_spec), (b, t_spec)],
               out_shape=[o_shape] * 2, out_specs=[t_spec] * 2, acc_shapes=[(tm, fs)], nk=1, kaxis=None,
               epilogue=epilogue, comm=comm)


def dw_rowshard(name, a, b, nc, comm=None):
    T, M = a.shape
    N = b.shape[1]
    ms = M // nc
    tn, tk = _tile(N, 2048 if ms <= 512 else 1024), _tile(T, 1024, 16)

    def epilogue(vals, ex, outs):
        outs[0][...] = vals[0].astype(BF16)

    return _mm(name, grid=(nc, N // tn, T // tk),
               pairs=[(a, pl.BlockSpec((tk, ms), lambda j, n, k: (k, j)),
                       b, pl.BlockSpec((tk, tn), lambda j, n, k: (k, n)), TN, 0)],
               out_shape=[jax.ShapeDtypeStruct((nc, ms, N), BF16)],
               out_specs=[pl.BlockSpec((None, ms, tn), lambda j, n, k: (j, 0, n))],
               acc_shapes=[(ms, tn)], nk=T // tk, kaxis=2, epilogue=epilogue, comm=comm)


def dw_colshard(name, a, bs, nc, comm=None):
    T, M = a.shape
    ns = bs[0].shape[1] // nc
    tm, tk = _tile(M, 512), _tile(T, 1024, 16)

    def epilogue(vals, ex, outs):
        for v, o in zip(vals, outs):
            o[...] = v.astype(BF16)

    a_spec = pl.BlockSpec((tk, tm), lambda j, m, k: (k, m))
    b_spec = pl.BlockSpec((tk, ns), lambda j, m, k: (k, j))
    return _mm(name, grid=(nc, M // tm, T // tk),
               pairs=[(a, a_spec, b, b_spec, TN, p) for p, b in enumerate(bs)],
               out_shape=[jax.ShapeDtypeStruct((nc, M, ns), BF16)] * len(bs),
               out_specs=[pl.BlockSpec((None, tm, ns), lambda j, m, k: (j, m, 0))] * len(bs),
               acc_shapes=[(tm, ns)] * len(bs), nk=T // tk, kaxis=2, epilogue=epilogue, comm=comm)


def ffn_dh(da, db, wg3, wu3, comm=None):
    T = da.shape[0]
    nc, D, fs = wg3.shape
    tm, tn = _tile(T, 512, 8), _tile(D, 1024)

    def epilogue(vals, ex, outs):
        outs[0][...] = vals[0]

    a_spec = pl.BlockSpec((tm, fs), lambda i, n, k: (i, k))
    w_spec = pl.BlockSpec((None, tn, fs), lambda i, n, k: (k, n, 0))
    return _mm("ffn_dh", grid=(T // tm, D // tn, nc),
               pairs=[(da, a_spec, wg3, w_spec, NT, 0), (db, a_spec, wu3, w_spec, NT, 0)],
               out_shape=[jax.ShapeDtypeStruct((T, D), F32)],
               out_specs=[pl.BlockSpec((tm, tn), lambda i, n, k: (i, n))],
               acc_shapes=[(tm, tn)], nk=nc, kaxis=2, epilogue=epilogue, comm=comm)[0]


def proj_main(h, w_t, P):
    T, D = h.shape
    tm, tn = _tile(T, 512, 8), _tile(P, 1024)

    def epilogue(vals, ex, outs):
        outs[0][...] = vals[0].astype(BF16)

    return _mm("proj_main", grid=(P // tn, T // tm),
               pairs=[(h, pl.BlockSpec((tm, D), lambda j, i: (i, 0)),
                       w_t, pl.BlockSpec((tn, D), lambda j, i: (j, 0)), NT, 0)],
               out_shape=[jax.ShapeDtypeStruct((T, P), BF16)],
               out_specs=[pl.BlockSpec((tm, tn), lambda j, i: (i, j))],
               acc_shapes=[(tm, tn)], nk=1, kaxis=None, epilogue=epilogue)[0]


def mm_nt_bf16(name, a, w):
    T, K = a.shape
    M = w.shape[0]
    tm, tn = _tile(T, 512, 8), _tile(M, 1024)

    def epilogue(vals, ex, outs):
        outs[0][...] = vals[0].astype(BF16)

    return _mm(name, grid=(T // tm, M // tn),
               pairs=[(a, pl.BlockSpec((tm, K), lambda i, n: (i, 0)),
                       w, pl.BlockSpec((tn, K), lambda i, n: (n, 0)), NT, 0)],
               out_shape=[jax.ShapeDtypeStruct((T, M), BF16)],
               out_specs=[pl.BlockSpec((tm, tn), lambda i, n: (i, n))],
               acc_shapes=[(tm, tn)], nk=1, kaxis=None, epilogue=epilogue)[0]


def proj_dh(dproj, w_t, df, wf_t, comm=None):
    T, P = dproj.shape
    D = w_t.shape[1]
    tm, tn, tk = _tile(T, 512, 8), _tile(D, 2048), _tile(P, 1280)

    def epilogue(vals, ex, outs):
        outs[0][...] = vals[0]

    return _mm("proj_dh", grid=(T // tm, D // tn, P // tk),
               pairs=[(dproj, pl.BlockSpec((tm, tk), lambda i, n, k: (i, k)),
                       w_t, pl.BlockSpec((tk, tn), lambda i, n, k: (k, n)), NN, 0)],
               once_pairs=[(df, pl.BlockSpec((tm, LANES), lambda i, n, k: (i, 0)),
                            wf_t, pl.BlockSpec((LANES, tn), lambda i, n, k: (0, n)), NN, 0)],
               out_shape=[jax.ShapeDtypeStruct((T, D), F32)],
               out_specs=[pl.BlockSpec((tm, tn), lambda i, n, k: (i, n))],
               acc_shapes=[(tm, tn)], nk=P // tk, kaxis=2, epilogue=epilogue, comm=comm)[0]


def proj_dw(dproj, df, h, rows):
    T, P = dproj.shape
    D = h.shape[1]
    tm, tn, tk = _tile(P, 1280), _tile(D, 1024), _tile(T, 1024, 16)

    def to_bf16(vals, ex, outs):
        outs[0][...] = vals[0].astype(BF16)

    def to_f32(vals, ex, outs):
        outs[0][...] = vals[0]

    main = _mm("proj_dw_main", grid=(P // tm, D // tn, T // tk),
               pairs=[(dproj, pl.BlockSpec((tk, tm), lambda m, n, k: (k, m)),
                       h, pl.BlockSpec((tk, tn), lambda m, n, k: (k, n)), TN, 0)],
               out_shape=[jax.ShapeDtypeStruct((rows, D), BF16)],
               out_specs=[pl.BlockSpec((tm, tn), lambda m, n, k: (m, n))],
               acc_shapes=[(tm, tn)], nk=T // tk, kaxis=2, epilogue=to_bf16)[0]
    gate = _mm("proj_dw_f", grid=(1, D // tn, T // tk),
               pairs=[(df, pl.BlockSpec((tk, LANES), lambda m, n, k: (k, 0)),
                       h, pl.BlockSpec((tk, tn), lambda m, n, k: (k, n)), TN, 0)],
               out_shape=[jax.ShapeDtypeStruct((LANES, D), F32)],
               out_specs=[pl.BlockSpec((LANES, tn), lambda m, n, k: (0, n))],
               acc_shapes=[(LANES, tn)], nk=T // tk, kaxis=2, epilogue=to_f32)[0]
    return main, gate


def fgate_fwd(h, wf_t, bias, n_heads):
    T, D = h.shape
    tt = _tile(T, 512, 8)

    def body(h_ref, w_ref, b_ref, f_ref, c_ref, carry):
        i = pl.program_id(0)

        @pl.when(i == 0)
        def _():
            carry[...] = jnp.zeros_like(carry)

        f = lax.dot_general(h_ref[...], w_ref[...], NT, preferred_element_type=F32) + b_ref[...]
        logf = jnp.minimum(f, 0.0) - jnp.log(1.0 + jnp.exp(-jnp.abs(f)))
        tri = (lax.broadcasted_iota(jnp.int32, (tt, tt), 0) >= lax.broadcasted_iota(jnp.int32, (tt, tt), 1))
        cs = jnp.dot(tri.astype(F32), logf, preferred_element_type=F32, precision=lax.Precision.HIGHEST)
        c = cs + carry[...]
        f_ref[...] = f
        c_ref[...] = c
        carry[...] = c[tt - 1:tt, :]

    row = pl.BlockSpec((tt, LANES), lambda i: (i, 0))
    return _pcall(
        body, name="fgate_fwd", grid=(T // tt,),
        in_specs=[pl.BlockSpec((tt, D), lambda i: (i, 0)), pl.BlockSpec((LANES, D), lambda i: (0, 0)),
                  pl.BlockSpec((1, LANES), lambda i: (0, 0))],
        out_specs=[row, row],
        out_shape=[jax.ShapeDtypeStruct((T, LANES), F32)] * 2,
        scratch=[pltpu.VMEM((1, LANES), F32)],
    )(h, wf_t, bias)


def fgate_bwd(dc, f, n_heads):
    T = dc.shape[0]
    tt = _tile(T, 512, 8)
    nt = T // tt

    def body(dc_ref, f_ref, df_ref, db_ref, carry):
        i = pl.program_id(0)

        @pl.when(i == 0)
        def _():
            carry[...] = jnp.zeros_like(carry)

        tri = (lax.broadcasted_iota(jnp.int32, (tt, tt), 1) >= lax.broadcasted_iota(jnp.int32, (tt, tt), 0))
        rs = jnp.dot(tri.astype(F32), dc_ref[...], preferred_element_type=F32,
                     precision=lax.Precision.HIGHEST) + carry[...]
        carry[...] = rs[0:1, :]
        lane = lax.broadcasted_iota(jnp.int32, (tt, LANES), 1)
        df = jnp.where(lane < n_heads, rs * _sigmoid(-f_ref[...]), 0.0)
        df_ref[...] = df.astype(BF16)
        part = jnp.sum(df, axis=0, keepdims=True)

        @pl.when(i == 0)
        def _():
            db_ref[...] = part

        @pl.when(i > 0)
        def _():
            db_ref[...] += part

    rev = pl.BlockSpec((tt, LANES), lambda i: (nt - 1 - i, 0))
    return _pcall(
        body, name="fgate_bwd", grid=(nt,),
        in_specs=[rev, rev],
        out_specs=[rev, pl.BlockSpec((1, LANES), lambda i: (0, 0))],
        out_shape=[jax.ShapeDtypeStruct((T, LANES), BF16), jax.ShapeDtypeStruct((1, LANES), F32)],
        scratch=[pltpu.VMEM((1, LANES), F32)],
    )(dc, f)


SUBLANES = 8
SHIFT_ROWS = HALO - SUBLANES


def _shifted_copies(buf, sh, tt):
    for r in range(1, SUBLANES):
        sh[r - 1, 0:tt + SHIFT_ROWS, :] = buf[pl.ds(r, tt + SHIFT_ROWS), :]


def _tap(buf, sh, offset, tt):
    q, r = divmod(offset, SUBLANES)
    if r == 0:
        return buf[pl.ds(SUBLANES * q, tt), :]
    return sh[r - 1, pl.ds(SUBLANES * q, tt), :]


def conv_fwd(proj, conv_w, conv_b, ln_g, ln_b):
    T = proj.shape[0]
    C = conv_w.shape[1]
    tt = _tile(T, 256, HALO)
    hb = tt // HALO

    def body(a_ref, g_ref, ah_ref, gh_ref, w_ref, cb_ref, lg_ref, lb_ref, ypre_ref, y_ref, ubuf, ush):
        i = pl.program_id(0)
        u = a_ref[...].astype(F32) * _sigmoid(g_ref[...].astype(F32))
        uh = ah_ref[...].astype(F32) * _sigmoid(gh_ref[...].astype(F32))
        ubuf[0:HALO, :] = jnp.where(i == 0, 0.0, uh)
        ubuf[HALO:HALO + tt, :] = u
        _shifted_copies(ubuf, ush, tt)
        acc = jnp.broadcast_to(cb_ref[...], (tt, C))
        for k in range(CONV_K):
            acc = acc + w_ref[k:k + 1, :] * _tap(ubuf, ush, HALO - (CONV_K - 1) + k, tt)
        ypre_ref[...] = acc
        mu = jnp.mean(acc, axis=-1, keepdims=True)
        d = acc - mu
        rstd = lax.rsqrt(jnp.mean(d * d, axis=-1, keepdims=True) + LN_EPS)
        z = d * rstd * lg_ref[...] + lb_ref[...]
        y_ref[...] = (z * _sigmoid(z)).astype(BF16)

    vec = pl.BlockSpec((1, C), lambda i: (0, 0))
    return _pcall(
        body, name="conv_fwd", grid=(T // tt,),
        in_specs=[pl.BlockSpec((tt, C), lambda i: (i, 0)), pl.BlockSpec((tt, C), lambda i: (i, 1)),
                  pl.BlockSpec((HALO, C), lambda i: (jnp.maximum(i * hb - 1, 0), 0)),
                  pl.BlockSpec((HALO, C), lambda i: (jnp.maximum(i * hb - 1, 0), 1)),
                  pl.BlockSpec((HALO, C), lambda i: (0, 0)), vec, vec, vec],
        out_specs=[pl.BlockSpec((tt, C), lambda i: (i, 0))] * 2,
        out_shape=[jax.ShapeDtypeStruct((T, C), F32), jax.ShapeDtypeStruct((T, C), BF16)],
        scratch=[pltpu.VMEM((tt + HALO, C), F32), pltpu.VMEM((SUBLANES - 1, tt + SHIFT_ROWS, C), F32)],
    )(proj, proj, proj, proj, conv_w, conv_b, ln_g, ln_b)


def conv_bwd(proj, ypre, dycat, conv_w, ln_g, ln_b):
    T = proj.shape[0]
    C = conv_w.shape[1]
    tt = _tile(T, 256, HALO)
    hb = tt // HALO
    nt = T // tt
    last_h = T // HALO - 1

    def ln_bwd(ypre_v, dout_v, lg, lb):
        mu = jnp.mean(ypre_v, axis=-1, keepdims=True)
        d = ypre_v - mu
        rstd = lax.rsqrt(jnp.mean(d * d, axis=-1, keepdims=True) + LN_EPS)
        yh = d * rstd
        z = yh * lg + lb
        s = _sigmoid(z)
        dz = dout_v * (s * (1.0 + z * (1.0 - s)))
        dyh = dz * lg
        dy = rstd * (dyh - jnp.mean(dyh, axis=-1, keepdims=True)
                     - yh * jnp.mean(dyh * yh, axis=-1, keepdims=True))
        return dy, dz, yh

    def body(a_ref, g_ref, ah_ref, gh_ref, yp_ref, ypn_ref, do_ref, don_ref, w_ref, lg_ref, lb_ref,
             dag_ref, dw_ref, dcb_ref, dlg_ref, dlb_ref, ubuf, dybuf, ush, dysh):
        i = pl.program_id(0)
        av = a_ref[...].astype(F32)
        sg = _sigmoid(g_ref[...].astype(F32))
        uh = ah_ref[...].astype(F32) * _sigmoid(gh_ref[...].astype(F32))
        ubuf[0:HALO, :] = jnp.where(i == 0, 0.0, uh)
        ubuf[HALO:HALO + tt, :] = av * sg
        lg, lb = lg_ref[...], lb_ref[...]
        dy, dz, yh = ln_bwd(yp_ref[...], do_ref[...].astype(F32), lg, lb)
        dyn, _, _ = ln_bwd(ypn_ref[...], don_ref[...].astype(F32), lg, lb)
        dybuf[0:tt, :] = dy
        dybuf[tt:tt + HALO, :] = jnp.where(i == nt - 1, 0.0, dyn)
        _shifted_copies(ubuf, ush, tt)
        _shifted_copies(dybuf, dysh, tt)

        @pl.when(i == 0)
        def _():
            dw_ref[...] = jnp.zeros_like(dw_ref)
            dcb_ref[...] = jnp.zeros_like(dcb_ref)
            dlg_ref[...] = jnp.zeros_like(dlg_ref)
            dlb_ref[...] = jnp.zeros_like(dlb_ref)

        du = jnp.zeros((tt, C), F32)
        for k in range(CONV_K):
            du = du + w_ref[k:k + 1, :] * _tap(dybuf, dysh, CONV_K - 1 - k, tt)
            dw_ref[k:k + 1, :] += jnp.sum(dy * _tap(ubuf, ush, HALO - (CONV_K - 1) + k, tt), axis=0, keepdims=True)
        dcb_ref[...] += jnp.sum(dy, axis=0, keepdims=True)
        dlg_ref[...] += jnp.sum(dz * yh, axis=0, keepdims=True)
        dlb_ref[...] += jnp.sum(dz, axis=0, keepdims=True)

        dag_ref[:, 0:C] = (du * sg).astype(BF16)
        dag_ref[:, C:2 * C] = (du * av * sg * (1.0 - sg)).astype(BF16)

    vec = pl.BlockSpec((1, C), lambda i: (0, 0))
    prev_h = lambda col: pl.BlockSpec((HALO, C), lambda i: (jnp.maximum(i * hb - 1, 0), col))
    next_h = pl.BlockSpec((HALO, C), lambda i: (jnp.minimum((i + 1) * hb, last_h), 0))
    return _pcall(
        body, name="conv_bwd", grid=(nt,),
        in_specs=[pl.BlockSpec((tt, C), lambda i: (i, 0)), pl.BlockSpec((tt, C), lambda i: (i, 1)),
                  prev_h(0), prev_h(1),
                  pl.BlockSpec((tt, C), lambda i: (i, 0)), next_h,
                  pl.BlockSpec((tt, C), lambda i: (i, 0)), next_h,
                  pl.BlockSpec((HALO, C), lambda i: (0, 0)), vec, vec],
        out_specs=[pl.BlockSpec((tt, 2 * C), lambda i: (i, 0)), pl.BlockSpec((HALO, C), lambda i: (0, 0)),
                   vec, vec, vec],
        out_shape=[jax.ShapeDtypeStruct((T, 2 * C), BF16), jax.ShapeDtypeStruct((HALO, C), F32),
                   jax.ShapeDtypeStruct((1, C), F32), jax.ShapeDtypeStruct((1, C), F32),
                   jax.ShapeDtypeStruct((1, C), F32)],
        scratch=[pltpu.VMEM((tt + HALO, C), F32), pltpu.VMEM((tt + HALO, C), F32),
                 pltpu.VMEM((SUBLANES - 1, tt + SHIFT_ROWS, C), F32),
                 pltpu.VMEM((SUBLANES - 1, tt + SHIFT_ROWS, C), F32)],
    )(proj, proj, proj, proj, ypre, ypre, dycat, dycat, conv_w, ln_g, ln_b)


PAIR = LANES // HEAD_DIM


def _head_masks(rows):
    lane = lax.broadcasted_iota(jnp.int32, (rows, LANES), 1)
    return [jnp.logical_and(lane >= hh * HEAD_DIM, lane < (hh + 1) * HEAD_DIM) for hh in range(PAIR)]


def _causal(tq, tk):
    return lax.broadcasted_iota(jnp.int32, (tq, tk), 0) >= lax.broadcasted_iota(jnp.int32, (tq, tk), 1)


def _lane_column(block, lane_index):
    lane = lax.broadcasted_iota(jnp.int32, block.shape, 1)
    return jnp.sum(jnp.where(lane == lane_index, block, 0.0), axis=-1, keepdims=True)


def attn_fwd(proj, cum, ck4, q_col, comm=None):
    T = proj.shape[0]
    H, nkv, _, tk = ck4.shape
    tq = tk
    hd = H * HEAD_DIM
    qb, kb, vb = q_col // LANES, (q_col + hd) // LANES, (q_col + 2 * hd) // LANES
    scale = 1.0 / math.sqrt(HEAD_DIM)

    def body(q_ref, k_ref, v_ref, cum_ref, ck_ref, o_ref, lse_ref):
        hp = pl.program_id(0)
        i = pl.program_id(1)
        masks = _head_masks(tq)
        q2 = q_ref[...] * scale
        qs = [jnp.where(mk, q2, jnp.zeros_like(q2)) for mk in masks]
        cqs = [_lane_column(cum_ref[...], PAIR * hp + hh) for hh in range(PAIR)]

        def step(j, carry, diagonal, blocks=1):
            scores, values = [], []
            for b in range(blocks):
                off = pl.multiple_of((j + b) * tk, tk)
                kj = k_ref[pl.ds(off, tk), :]
                values.append(v_ref[pl.ds(off, tk), :])
                per_head = []
                for hh in range(PAIR):
                    s = lax.dot_general(qs[hh], kj, NT, preferred_element_type=F32)
                    s = s + cqs[hh] - ck_ref[hh, j + b]
                    if diagonal:
                        s = jnp.where(_causal(tq, tk), s, NEG_INF)
                    per_head.append(s)
                scores.append(per_head)
            out = []
            for hh in range(PAIR):
                m, l, acc = carry[hh]
                m_new = m
                for b in range(blocks):
                    m_new = jnp.maximum(m_new, jnp.max(scores[b][hh], axis=-1, keepdims=True))
                alpha = jnp.exp(m - m_new)
                l, acc = alpha * l, alpha * acc
                for b in range(blocks):
                    p = jnp.exp(scores[b][hh] - m_new)
                    l = l + jnp.sum(p, axis=-1, keepdims=True)
                    acc = acc + jnp.dot(p.astype(BF16), values[b], preferred_element_type=F32)
                out.append((m_new, l, acc))
            return tuple(out)

        init = tuple((jnp.full((tq, 1), -jnp.inf, F32), jnp.zeros((tq, 1), F32), jnp.zeros((tq, LANES), F32))
                     for _ in range(PAIR))
        twos = i // 2
        carry = lax.fori_loop(0, twos, lambda t, c: step(2 * t, c, False, blocks=2), init)
        carry = lax.fori_loop(2 * twos, i, functools.partial(step, diagonal=False), carry)
        carry = step(i, carry, True)
        o = carry[PAIR - 1][2] / carry[PAIR - 1][1]
        for hh in range(PAIR - 1):
            o = jnp.where(masks[hh], carry[hh][2] / carry[hh][1], o)
        o_ref[...] = o
        lse = jnp.broadcast_to(carry[PAIR - 1][0] + jnp.log(carry[PAIR - 1][1]), (tq, LANES))
        for hh in range(PAIR - 1):
            lse = jnp.where(masks[hh], carry[hh][0] + jnp.log(carry[hh][1]), lse)
        lse_ref[...] = lse

    return _pcall(
        body, name="attn_fwd", grid=(H // PAIR, T // tq),
        in_specs=[pl.BlockSpec((tq, LANES), lambda hp, i: (i, qb + hp)),
                  pl.BlockSpec((T, LANES), lambda hp, i: (0, kb + hp)),
                  pl.BlockSpec((T, LANES), lambda hp, i: (0, vb + hp)),
                  pl.BlockSpec((tq, LANES), lambda hp, i: (i, 0)),
                  pl.BlockSpec((PAIR, nkv, 1, tk), lambda hp, i: (hp, 0, 0, 0))],
        out_specs=[pl.BlockSpec((tq, LANES), lambda hp, i: (i, hp)),
                   pl.BlockSpec((None, tq, LANES), lambda hp, i: (hp, i, 0))],
        out_shape=[jax.ShapeDtypeStruct((T, hd), F32), jax.ShapeDtypeStruct((H // PAIR, T, LANES), F32)],
        comm=comm,
    )(proj, proj, proj, cum, ck4)


def attn_bwd(proj, o, dycat, lse, cum, ck4, q_col, do_col, comm=None):
    T = proj.shape[0]
    H, nkv, _, tk = ck4.shape
    tq = tk
    nq = T // tq
    hd = H * HEAD_DIM
    qb, kb, vb = q_col // LANES, (q_col + hd) // LANES, (q_col + 2 * hd) // LANES
    dob = do_col // LANES
    scale = 1.0 / math.sqrt(HEAD_DIM)

    def body(q_ref, k_ref, v_ref, o_ref, do_ref, lse_ref, cum_ref, ck_ref,
             dq_ref, dk_ref, dv_ref, dcq_ref, dck_ref):
        hp = pl.program_id(0)
        j = pl.program_id(1)

        def block(i, diagonal):
            masks = _head_masks(tq)
            rows = pl.ds(pl.multiple_of(i * tq, tq), tq)
            q2, k2, v2, do2 = q_ref[rows, :] * scale, k_ref[...], v_ref[...], do_ref[rows, :]
            zero = jnp.zeros_like(q2)
            prod = do2.astype(F32) * o_ref[rows, :]
            cum_q, lse_q = cum_ref[rows, :], lse_ref[rows, :]
            dq_part = dk_part = dv_part = None
            dcq_part = jnp.zeros((tq, LANES), F32)
            lane = lax.broadcasted_iota(jnp.int32, (tq, LANES), 1)
            for hh in range(PAIR):
                qh = jnp.where(masks[hh], q2, zero)
                kh = jnp.where(masks[hh], k2, zero)
                doh = jnp.where(masks[hh], do2, zero)
                delta = jnp.sum(jnp.where(masks[hh], prod, 0.0), axis=-1, keepdims=True)
                s = lax.dot_general(qh, k2, NT, preferred_element_type=F32)
                s = s + _lane_column(cum_q, PAIR * hp + hh) - ck_ref[hh]
                if diagonal:
                    s = jnp.where(_causal(tq, tk), s, NEG_INF)
                p = jnp.exp(s - _lane_column(lse_q, hh * HEAD_DIM))
                dp = lax.dot_general(doh, v2, NT, preferred_element_type=F32)
                ds = p * (dp - delta)
                dsb = ds.astype(BF16)
                dv_h = lax.dot_general(p.astype(BF16), doh, TN, preferred_element_type=F32)
                dk_h = lax.dot_general(dsb, qh, TN, preferred_element_type=F32)
                dq_h = jnp.dot(dsb, kh, preferred_element_type=F32)
                dq_part = dq_h if dq_part is None else dq_part + dq_h
                dk_part = dk_h if dk_part is None else dk_part + dk_h
                dv_part = dv_h if dv_part is None else dv_part + dv_h
                dck_h = -jnp.sum(ds, axis=0, keepdims=True)
                dcq_part = jnp.where(lane == PAIR * hp + hh, jnp.sum(ds, axis=-1, keepdims=True), dcq_part)
                if diagonal:
                    dck_ref[hh] = dck_h
                else:
                    dck_ref[hh] += dck_h
            dq_part = dq_part * scale

            @pl.when(j == 0)
            def _():
                dq_ref[rows, :] = dq_part

            @pl.when(j > 0)
            def _():
                dq_ref[rows, :] += dq_part

            @pl.when(jnp.logical_and(hp == 0, j == 0))
            def _():
                dcq_ref[rows, :] = dcq_part

            @pl.when(jnp.logical_or(hp > 0, j > 0))
            def _():
                dcq_ref[rows, :] += dcq_part

            if diagonal:
                dk_ref[...] = dk_part
                dv_ref[...] = dv_part
            else:
                dk_ref[...] += dk_part
                dv_ref[...] += dv_part

        block(j, True)

        def later(i, carry):
            block(i, False)
            return carry

        lax.fori_loop(j + 1, nq, later, 0)

    at_q = lambda col: pl.BlockSpec((T, LANES), lambda hp, j: (0, col + hp))
    at_k = lambda col: pl.BlockSpec((tk, LANES), lambda hp, j: (j, col + hp))
    lse_spec = pl.BlockSpec((None, T, LANES), lambda hp, j: (hp, 0, 0))
    cum_spec = pl.BlockSpec((T, LANES), lambda hp, j: (0, 0))
    ck_spec = pl.BlockSpec((PAIR, None, 1, tk), lambda hp, j: (hp, j, 0, 0))
    return _pcall(
        body, name="attn_bwd", grid=(H // PAIR, nkv),
        in_specs=[at_q(qb), at_k(kb), at_k(vb), at_q(0), at_q(dob), lse_spec, cum_spec, ck_spec],
        out_specs=[pl.BlockSpec((T, LANES), lambda hp, j: (0, hp)), at_k(0), at_k(0),
                   pl.BlockSpec((T, LANES), lambda hp, j: (0, 0)), ck_spec],
        out_shape=[jax.ShapeDtypeStruct((T, hd), F32)] * 3
        + [jax.ShapeDtypeStruct((T, LANES), F32), jax.ShapeDtypeStruct((H, nkv, 1, tk), F32)],
        comm=comm,
    )(proj, proj, proj, o, dycat, lse, cum, ck4)


ELEMENTWISE_BLOCK_BYTES = 2 * 1024 * 1024
BF16_ROWS = 16


def cast_bf16(arrays, comm=None):
    def slab(a, steps):
        R, C = a.shape
        if R % (steps * BF16_ROWS) == 0:
            return pl.BlockSpec((R // steps, C), lambda i: (i, 0))
        if C % (steps * LANES) == 0:
            return pl.BlockSpec((R, C // steps), lambda i: (0, i))
        return None

    steps = 8 if all(slab(a, 8) is not None for a in arrays) else 4
    specs = [slab(a, steps) for a in arrays]
    n = len(arrays)

    def body(*refs):
        for src, dst in zip(refs[:n], refs[n:]):
            dst[...] = src[...].astype(BF16)

    return _pcall(body, name="cast_bf16", grid=(steps,), in_specs=specs, out_specs=specs,
                  out_shape=[jax.ShapeDtypeStruct(a.shape, BF16) for a in arrays], comm=comm)(*arrays)


def _ew_tiles(rows, cols, bytes_per_element):
    target = max(8, ELEMENTWISE_BLOCK_BYTES // max(1, cols * bytes_per_element))
    if rows <= target:
        return rows, cols
    t = (target // 16) * 16
    while t >= 16:
        if rows % t == 0:
            return t, cols
        t -= 16
    tc = _tile(cols, max(LANES, (ELEMENTWISE_BLOCK_BYTES // (rows * bytes_per_element)) // LANES * LANES))
    return rows, tc


def sum_chips(recv):
    nc, R, C = recv.shape
    tr, tc = _ew_tiles(R, C, 4)

    def body(r_ref, o_ref):
        acc = r_ref[0].astype(F32)
        for j in range(1, nc):
            acc = acc + r_ref[j].astype(F32)
        o_ref[...] = acc

    return _pcall(
        body, name="sum_chips", grid=(R // tr, C // tc),
        in_specs=[pl.BlockSpec((nc, tr, tc), lambda i, j: (0, i, j))],
        out_specs=[pl.BlockSpec((tr, tc), lambda i, j: (i, j))],
        out_shape=[jax.ShapeDtypeStruct((R, C), F32)],
    )(recv)[0]


def add_sibling_half(g, recv):
    nc, R, C = g.shape
    hr = R // 2
    tr, tc = _ew_tiles(hr, C, 4 * nc)

    def body(g_ref, r_ref, o_ref):
        c = lax.axis_index("c")
        for j in range(nc):
            o_ref[j] = (g_ref[j, c].astype(F32) + r_ref[j].astype(F32)).astype(BF16)

    return _pcall(
        body, name="add_sibling_half", grid=(hr // tr, C // tc),
        in_specs=[pl.BlockSpec((nc, 2, tr, tc), lambda i, j: (0, 0, i, j)),
                  pl.BlockSpec((nc, tr, tc), lambda i, j: (0, i, j))],
        out_specs=[pl.BlockSpec((nc, tr, tc), lambda i, j: (0, i, j))],
        out_shape=[jax.ShapeDtypeStruct((nc, hr, C), BF16)],
    )(g.reshape(nc, 2, hr, C), recv)[0]


def adamw(w, m, v, g_parts, comm=None, halves=False):
    R, C = w.shape
    tr, tc = _ew_tiles(R // 2 if halves else R, C, 4 * 4)
    n_g = len(g_parts)
    n_half = (R // 2) // tr
    c1 = 1.0 - ADAM_B1
    c2 = 1.0 - ADAM_B2
    bc1 = 1.0 - ADAM_B1 ** ADAM_STEP
    bc2 = 1.0 - ADAM_B2 ** ADAM_STEP

    def body(*refs):
        w_ref, m_ref, v_ref = refs[:3]
        g_refs = refs[3:3 + n_g]
        g_out, d_out, m_out, v_out = refs[3 + n_g:]
        if halves:
            mine = (pl.program_id(0) >= n_half) == (lax.axis_index("c") == 1)
            g = jnp.where(mine, g_refs[0][...], g_refs[1][...])
        else:
            g = g_refs[0][...]
            for r in g_refs[1:]:
                g = g + r[...]
        m_new = ADAM_B1 * m_ref[...] + c1 * g
        v_new = ADAM_B2 * v_ref[...] + c2 * (g * g)
        m_hat = m_new / bc1
        v_hat = v_new / bc2
        g_out[...] = g
        d_out[...] = -ADAM_LR * (m_hat / (jnp.sqrt(v_hat) + ADAM_EPS) + ADAM_WD * w_ref[...])
        m_out[...] = m_new
        v_out[...] = v_new

    spec = pl.BlockSpec((tr, tc), lambda i, j: (i, j))
    g_spec = pl.BlockSpec((tr, tc), lambda i, j: (i % n_half, j)) if halves else spec
    return _pcall(
        body, name="adamw", grid=(R // tr, C // tc),
        in_specs=[spec] * 3 + [g_spec] * n_g, out_specs=[spec] * 4,
        out_shape=[jax.ShapeDtypeStruct((R, C), F32)] * 4, comm=comm,
    )(w, m, v, *g_parts)


def _chip_coords():
    x, y, c = lax.axis_index("x"), lax.axis_index("y"), lax.axis_index("c")
    others = [(1 - x, y), (x, 1 - y), (1 - x, 1 - y)]
    return x, y, c, others


def _remote(src, dst, send_sem, recv_sem, device):
    return pltpu.make_async_remote_copy(src_ref=src, dst_ref=dst, send_sem=send_sem, recv_sem=recv_sem,
                                        device_id=device, device_id_type=MESH)


def gather_comm(shards):
    n = len(shards)
    SLOTS = 7

    def makers(ins, outs, sems):
        send_sems, recv_sems, local_sems = sems
        x, y, c, _ = _chip_coords()
        me, xn, yn, dg = 2 * x + y, 2 * (1 - x) + y, 2 * x + (1 - y), 2 * (1 - x) + (1 - y)
        to_x, to_y, sibling = (1 - x, y, c), (x, 1 - y, c), (x, y, 1 - c)

        def part(ref, a, half, quarter=None, chip=None):
            rows, cols = ins[a].shape[0], ins[a].shape[1]
            lead = () if chip is None else (chip,)
            along_rows = rows % (4 * BF16_ROWS) == 0 or (ins[a].dtype == F32 and rows % (4 * SUBLANES) == 0)
            size = (rows if along_rows else cols) // 2
            start = half * size
            if quarter is not None:
                size = size // 2
                start = start + quarter * size
            if along_rows:
                return ref.at[(*lead, pl.ds(start, size))]
            return ref.at[(*lead, slice(None), pl.ds(start, size))]

        def copy(a, k, src, dst, device):
            return _remote(src, dst, send_sems.at[SLOTS * a + k], recv_sems.at[SLOTS * a + k], device)

        def local(a):
            return pltpu.make_async_copy(ins[a], outs[a].at[me], local_sems.at[a])

        def first_leg(a):
            mine = part(outs[a], a, c, chip=me)
            return [copy(a, 0, part(ins[a], a, c), mine, to_x), copy(a, 1, part(ins[a], a, c), mine, to_y)]

        def arrived(a, k):
            region = {0: part(outs[a], a, c, chip=xn), 1: part(outs[a], a, c, chip=yn),
                      2: part(outs[a], a, c, 0, chip=dg), 3: part(outs[a], a, c, 1, chip=dg),
                      4: part(outs[a], a, 1 - c, chip=xn), 5: part(outs[a], a, 1 - c, chip=yn),
                      6: part(outs[a], a, 1 - c, chip=dg)}[k]
            return copy(a, k, region, region, sibling if k >= 4 else (to_x if k in (0, 3) else to_y))

        def relays(a):
            qx, qy = part(outs[a], a, c, 0, chip=xn), part(outs[a], a, c, 1, chip=yn)
            return [copy(a, 2, qx, qx, to_y), copy(a, 3, qy, qy, to_x)]

        def handover(a, k):
            region = part(outs[a], a, c, chip={4: xn, 5: yn, 6: dg}[k])
            return copy(a, k, region, region, sibling)

        return local, first_leg, arrived, relays, handover

    def start(ins, outs, sems):
        local, first_leg, _, _, _ = makers(ins, outs, sems)
        for a in range(n):
            for cp in first_leg(a):
                cp.start()
        for a in range(n):
            local(a).start()

    def relay(ins, outs, sems):
        _, _, arrived, relays, handover = makers(ins, outs, sems)
        for a in range(n):
            to_y_nbr, to_x_nbr = relays(a)
            arrived(a, 0).wait_recv()
            to_y_nbr.start()
            handover(a, 4).start()
            arrived(a, 1).wait_recv()
            to_x_nbr.start()
            handover(a, 5).start()

    def finish(ins, outs, sems):
        local, first_leg, arrived, relays, handover = makers(ins, outs, sems)
        for a in range(n):
            arrived(a, 2).wait_recv()
            arrived(a, 3).wait_recv()
            handover(a, 6).start()
        for a in range(n):
            for k in (4, 5, 6):
                arrived(a, k).wait_recv()
        for a in range(n):
            for cp in first_leg(a) + relays(a) + [handover(a, k) for k in (4, 5, 6)]:
                cp.wait_send()
            local(a).wait()

    return Comm(shards, [jax.ShapeDtypeStruct((N_CHIP,) + s.shape, s.dtype) for s in shards],
                [pltpu.SemaphoreType.DMA((SLOTS * n,)), pltpu.SemaphoreType.DMA((SLOTS * n,)),
                 pltpu.SemaphoreType.DMA((n,))], start, finish, relay)


def scatter_comm(grads):
    n = len(grads)
    pieces = [(a, jj) for a in range(n) for jj in range(3)]

    def makers(ins, outs, sems):
        send_sems, recv_sems, local_sems = sems
        x, y, c, others = _chip_coords()
        me = 2 * x + y

        def local(a):
            return pltpu.make_async_copy(ins[a].at[me], outs[a].at[me], local_sems.at[a])

        def ici(a, jj):
            ox, oy = others[jj]
            return _remote(ins[a].at[2 * ox + oy], outs[a].at[me], send_sems.at[3 * a + jj],
                           recv_sems.at[3 * a + jj], (ox, oy, c))

        def landed(a, jj):
            ox, oy = others[jj]
            slot = outs[a].at[2 * ox + oy]
            return _remote(slot, slot, send_sems.at[3 * a + jj], recv_sems.at[3 * a + jj], (ox, oy, c))

        return local, ici, landed

    def start(ins, outs, sems):
        local, ici, _ = makers(ins, outs, sems)
        for a in range(n):
            for jj in (2, 0, 1):
                ici(a, jj).start()
        for a in range(n):
            local(a).start()

    def finish(ins, outs, sems):
        local, ici, landed = makers(ins, outs, sems)
        for a, jj in pieces:
            landed(a, jj).wait_recv()
        for a, jj in pieces:
            ici(a, jj).wait_send()
        for a in range(n):
            local(a).wait()

    return Comm(grads, [jax.ShapeDtypeStruct(g.shape, g.dtype) for g in grads],
                [pltpu.SemaphoreType.DMA((3 * n,)), pltpu.SemaphoreType.DMA((3 * n,)),
                 pltpu.SemaphoreType.DMA((n,))], start, finish)


def halfswap_comm(grads):
    n = len(grads)

    def copies(ins, outs, sems):
        send_sems, recv_sems = sems
        x, y, c, _ = _chip_coords()
        out = []
        for a in range(n):
            hr = ins[a].shape[1] // 2
            out.append(_remote(ins[a].at[:, pl.ds((1 - c) * hr, hr)], outs[a], send_sems.at[a], recv_sems.at[a],
                               (x, y, 1 - c)))
        return out

    def start(ins, outs, sems):
        for cp in copies(ins, outs, sems):
            cp.start()

    def finish(ins, outs, sems):
        for cp in copies(ins, outs, sems):
            cp.wait()

    return Comm(grads, [jax.ShapeDtypeStruct((g.shape[0], g.shape[1] // 2, g.shape[2]), g.dtype) for g in grads],
                [pltpu.SemaphoreType.DMA((n,)), pltpu.SemaphoreType.DMA((n,))], start, finish)


def join_comms(first, second):
    ni, no, ns = len(first.operands), len(first.out_shape), len(first.sems)

    def start(ins, outs, sems):
        first.start(ins[:ni], outs[:no], sems[:ns])
        second.start(ins[ni:], outs[no:], sems[ns:])

    def finish(ins, outs, sems):
        first.finish(ins[:ni], outs[:no], sems[:ns])
        second.finish(ins[ni:], outs[no:], sems[ns:])

    def relay(ins, outs, sems):
        if first.relay is not None:
            first.relay(ins[:ni], outs[:no], sems[:ns])
        if second.relay is not None:
            second.relay(ins[ni:], outs[no:], sems[ns:])

    return Comm(first.operands + second.operands, first.out_shape + second.out_shape, first.sems + second.sems,
                start, finish, relay if (first.relay or second.relay) else None)


def swap_comm(parts):
    n = len(parts)

    def copies(ins, outs, sems):
        send_sems, recv_sems = sems
        x, y, c, _ = _chip_coords()
        return [_remote(ins[a], outs[a], send_sems.at[a], recv_sems.at[a], (x, y, 1 - c)) for a in range(n)]

    def start(ins, outs, sems):
        for cp in copies(ins, outs, sems):
            cp.start()

    def finish(ins, outs, sems):
        for cp in copies(ins, outs, sems):
            cp.wait()

    return Comm(parts, [jax.ShapeDtypeStruct(p.shape, p.dtype) for p in parts],
                [pltpu.SemaphoreType.DMA((n,)), pltpu.SemaphoreType.DMA((n,))], start, finish)


def allreduce_small(v):
    R = v.shape[0]

    def body(v_ref, sum_ref, all_ref, send_sems, recv_sems):
        x, y, c = lax.axis_index("x"), lax.axis_index("y"), lax.axis_index("c")
        me = 4 * x + 2 * y + c
        all_ref[me] = v_ref[...]
        copies = []
        for k in range(1, N_DEV):
            px = 1 - x if k & 4 else x
            py = 1 - y if k & 2 else y
            pc = 1 - c if k & 1 else c
            cp = pltpu.make_async_remote_copy(
                src_ref=v_ref, dst_ref=all_ref.at[me], send_sem=send_sems.at[k - 1], recv_sem=recv_sems.at[k - 1],
                device_id=(px, py, pc), device_id_type=MESH)
            cp.start()
            copies.append((cp, 4 * px + 2 * py + pc))
        for k, (cp, peer) in enumerate(copies):
            pltpu.make_async_remote_copy(
                src_ref=v_ref, dst_ref=all_ref.at[peer], send_sem=send_sems.at[k], recv_sem=recv_sems.at[k],
                device_id=(x, y, c), device_id_type=MESH).wait_recv()
        for cp, _ in copies:
            cp.wait_send()
        acc = all_ref[0]
        for d in range(1, N_DEV):
            acc = acc + all_ref[d]
        sum_ref[...] = acc

    vm = pl.BlockSpec(memory_space=pltpu.VMEM)
    return pl.pallas_call(
        body, name="allreduce_small",
        in_specs=[vm], out_specs=[vm, vm],
        out_shape=[jax.ShapeDtypeStruct((R, LANES), F32), jax.ShapeDtypeStruct((N_DEV, R, LANES), F32)],
        scratch_shapes=[pltpu.SemaphoreType.DMA((N_DEV - 1,)), pltpu.SemaphoreType.DMA((N_DEV - 1,))],
    )(v)[0]


SMALL_NAMES = ("ffn1_norm", "mix_norm", "ffn2_norm", "final_norm", "conv_b", "conv_ln_g", "conv_ln_b")


def _pack_small(vecs, bias, conv_w_rows, loss_tile):
    rows = [vecs[n].reshape(-1, LANES) for n in SMALL_NAMES]
    rows.append(bias.reshape(1, LANES))
    rows.append(conv_w_rows.reshape(-1, LANES))
    rows.append(loss_tile[0:1, :])
    packed = jnp.concatenate(rows, axis=0)
    pad = (-packed.shape[0]) % 8
    return jnp.pad(packed, ((0, pad), (0, 0)))


def _unpack_small(packed, sizes, n_conv_rows):
    out, r = {}, 0
    for n in SMALL_NAMES:
        k = sizes[n] // LANES
        out[n] = packed[r:r + k].reshape(-1)
        r += k
    out["fgate_bias"] = packed[r]
    r += 1
    out["conv_w"] = packed[r:r + n_conv_rows]
    r += n_conv_rows
    out["loss"] = packed[r, 0]
    return out


def kernel(x, ffn1_norm, ffn1_w_gate, ffn1_w_up, ffn1_w_down, mix_norm, w_in, fgate_bias, conv_w, conv_b, conv_ln_g, conv_ln_b, w_out, ffn2_norm, ffn2_w_gate, ffn2_w_up, ffn2_w_down, final_norm, loss_target, m_ffn1_norm, m_ffn1_w_gate, m_ffn1_w_up, m_ffn1_w_down, m_mix_norm, m_w_in, m_fgate_bias, m_conv_w, m_conv_b, m_conv_ln_g, m_conv_ln_b, m_w_out, m_ffn2_norm, m_ffn2_w_gate, m_ffn2_w_up, m_ffn2_w_down, m_final_norm, v_ffn1_norm, v_ffn1_w_gate, v_ffn1_w_up, v_ffn1_w_down, v_mix_norm, v_w_in, v_fgate_bias, v_conv_w, v_conv_b, v_conv_ln_g, v_conv_ln_b, v_w_out, v_ffn2_norm, v_ffn2_w_gate, v_ffn2_w_up, v_ffn2_w_down, v_final_norm):
    w = dict(ffn1_norm=ffn1_norm, ffn1_w_gate=ffn1_w_gate, ffn1_w_up=ffn1_w_up, ffn1_w_down=ffn1_w_down,
             mix_norm=mix_norm, w_in=w_in, fgate_bias=fgate_bias, conv_w=conv_w, conv_b=conv_b,
             conv_ln_g=conv_ln_g, conv_ln_b=conv_ln_b, w_out=w_out, ffn2_norm=ffn2_norm,
             ffn2_w_gate=ffn2_w_gate, ffn2_w_up=ffn2_w_up, ffn2_w_down=ffn2_w_down, final_norm=final_norm)
    m = dict(ffn1_norm=m_ffn1_norm, ffn1_w_gate=m_ffn1_w_gate, ffn1_w_up=m_ffn1_w_up, ffn1_w_down=m_ffn1_w_down,
             mix_norm=m_mix_norm, w_in=m_w_in, fgate_bias=m_fgate_bias, conv_w=m_conv_w, conv_b=m_conv_b,
             conv_ln_g=m_conv_ln_g, conv_ln_b=m_conv_ln_b, w_out=m_w_out, ffn2_norm=m_ffn2_norm,
             ffn2_w_gate=m_ffn2_w_gate, ffn2_w_up=m_ffn2_w_up, ffn2_w_down=m_ffn2_w_down, final_norm=m_final_norm)
    v = dict(ffn1_norm=v_ffn1_norm, ffn1_w_gate=v_ffn1_w_gate, ffn1_w_up=v_ffn1_w_up, ffn1_w_down=v_ffn1_w_down,
             mix_norm=v_mix_norm, w_in=v_w_in, fgate_bias=v_fgate_bias, conv_w=v_conv_w, conv_b=v_conv_b,
             conv_ln_g=v_conv_ln_g, conv_ln_b=v_conv_ln_b, w_out=v_w_out, ffn2_norm=v_ffn2_norm,
             ffn2_w_gate=v_ffn2_w_gate, ffn2_w_up=v_ffn2_w_up, ffn2_w_down=v_ffn2_w_down, final_norm=v_final_norm)
    names = list(w.keys())
    big = ("ffn1_w_gate", "ffn1_w_up", "ffn1_w_down", "w_in", "w_out", "ffn2_w_gate", "ffn2_w_up", "ffn2_w_down")

    T, D = x.shape[1], x.shape[2]
    C = conv_b.shape[0]
    H = fgate_bias.shape[0]
    cs = conv_w.shape[1]
    in_cols = N_CHIP * w_in.shape[1]
    p_main = in_cols - H

    x0, tgt = x[0], loss_target[0]
    tk = _tile(T, 512, 128)
    nkv = T // tk
    row = lambda a: a.reshape(1, -1)
    grad, delta, new_m, new_v = {}, {}, {}, {}

    def update(n, parts, comm=None, halves=False):
        args = (w[n], m[n], v[n])
        if n == "w_in":
            outs = [t.T for t in adamw(*[a.T for a in args], parts, comm=comm)]
        else:
            outs = adamw(*args, parts, comm=comm, halves=halves)
        grad[n], delta[n], new_m[n], new_v[n] = outs

    rest = [n for n in big if n != "ffn1_w_gate"]
    g0 = gather_comm([w["ffn1_w_gate"].astype(BF16), jnp.pad(conv_w, ((0, HALO - CONV_K), (0, 0)))])
    wb = dict(zip(rest, cast_bf16([w[n].T if n == "w_in" else w[n] for n in rest], comm=g0)))
    wg1, conv_w4 = g0.results
    conv_w_full = conv_w4.transpose(1, 0, 2).reshape(HALO, C)
    h1, r1 = rms_fwd(x0, row(ffn1_norm))
    g1a = gather_comm([wb["ffn1_w_up"]])
    a1 = ffn_gate(h1, wg1, comm=g1a)
    wu1 = g1a.results[0]
    g1b = gather_comm([wb["ffn1_w_down"]])
    b1, mid1 = ffn_upmul(h1, wu1, a1, comm=g1b)
    wd1 = g1b.results[0]
    g2 = gather_comm([wb["w_in"]])
    x1 = mm_residual("ffn_down_g", mid1, wd1, x0, 0.5, comm=g2)[0]
    w_t = g2.results[0].reshape(in_cols, D)

    wf_t = jnp.pad(w_t[p_main:], ((0, LANES - H), (0, 0)))
    bias_pad = jnp.pad(row(fgate_bias), ((0, 0), (0, LANES - H)))
    h2, r2 = rms_fwd(x1, row(mix_norm))
    proj = proj_main(h2, w_t, p_main)
    f, cum = fgate_fwd(h2, wf_t, bias_pad, H)
    ypre, yconv = conv_fwd(proj, conv_w_full, row(conv_b), row(conv_ln_g), row(conv_ln_b))
    ck4 = cum[:, :H].T.reshape(H, nkv, 1, tk)
    g3 = gather_comm([wb["w_out"], wb["ffn2_w_gate"], wb["ffn2_w_up"], wb["ffn2_w_down"]])
    o, lse = attn_fwd(proj, cum, ck4, 2 * C, comm=g3)
    w_out3, wg2, wu2, wd2 = g3.results
    ycat = jnp.concatenate([yconv, o.astype(BF16)], axis=1)
    x2 = mm_residual("out_proj", ycat, w_out3.reshape(2, -1, D), x1, 1.0)[0]

    h3, r3 = rms_fwd(x2, row(ffn2_norm))
    a2, b2, mid2 = ffn_up(h3, wg2, wu2)
    x3 = mm_residual("ffn_down", mid2, wd2, x2, 0.5)[0]
    dx3, dx3b, loss_tile, d_final = final_loss(x3, tgt, row(final_norm))

    da2, db2 = ffn_bwd_mid(dx3b, wd2, a2, b2)
    dwd2 = dw_rowshard("ffn_dwd", mid2, dx3b, N_CHIP)[0]
    s1 = scatter_comm([dwd2])
    dwg2, dwu2 = dw_colshard("ffn_dwgu_s", h3, [da2, db2], N_CHIP, comm=s1)
    s2 = scatter_comm([dwg2])
    dh3 = ffn_dh(da2, db2, wg2, wu2, comm=s2)
    dx2, dx2b, d_ffn2_norm = rms_bwd(dh3, x2, r3, row(ffn2_norm), dx3, 1.0)

    dycat = mm_nt_bf16("out_proj_dy", dx2b, w_out3.reshape(-1, D))
    dw_out3 = dw_rowshard("out_proj_dw", ycat, dx2b, N_CHIP)[0]
    s3 = scatter_comm([dwu2, dw_out3])
    dq, dk, dv, dcq, dck4 = attn_bwd(proj, o, dycat, lse, cum, ck4, 2 * C, C, comm=s3)
    dc = dcq + jnp.pad(dck4.reshape(H, T).T, ((0, 0), (0, LANES - H)))
    df, d_bias = fgate_bwd(dc, f, H)
    dag, d_conv_w, d_conv_b, d_ln_g, d_ln_b = conv_bwd(proj, ypre, dycat, conv_w_full, row(conv_ln_g),
                                                       row(conv_ln_b))
    dproj = jnp.concatenate([dag, dq.astype(BF16), dk.astype(BF16), dv.astype(BF16)], axis=1)
    early = ("ffn2_w_down", "ffn2_w_gate", "ffn2_w_up", "w_out")
    early_sums = [sum_chips(r) for r in (s1.results[0], s2.results[0], s3.results[0], s3.results[1])]
    sw1 = swap_comm(early_sums)
    dh2 = proj_dh(dproj, w_t, df, wf_t, comm=sw1)
    dw_t, dwf_t = proj_dw(dproj, df, h2, in_cols)
    dw_t = lax.dynamic_update_slice(dw_t, dwf_t[:H].astype(BF16), (p_main, 0))
    dw_in3 = dw_t.reshape(N_CHIP, in_cols // N_CHIP, D)
    dx1, dx1b, d_mix_norm = rms_bwd(dh2, x1, r2, row(mix_norm), dx2, 0.5)

    s4 = scatter_comm([dw_in3])
    da1, db1 = ffn_bwd_mid(dx1b, wd1, a1, b1, comm=s4)
    dwd1 = dw_rowshard("ffn_dwd", mid1, dx1b, N_CHIP)[0]
    s5 = scatter_comm([dwd1])
    dwg1, dwu1 = dw_colshard("ffn_dwgu_s", h1, [da1, db1], N_CHIP, comm=s5)
    mid_sums = [sum_chips(s4.results[0]), sum_chips(s5.results[0])]
    s6 = join_comms(join_comms(scatter_comm([dwg1]), halfswap_comm([dwu1])), swap_comm(mid_sums))
    dh1 = ffn_dh(da1, db1, wg1, wu1, comm=s6)
    recv_g1, sibling_u1, their_in, their_d1 = s6.results
    grad_x, _, d_ffn1_norm = rms_bwd(dh1, x0, r1, row(ffn1_norm), dx1, 1.0)

    s7 = scatter_comm([add_sibling_half(dwu1, sibling_u1)])
    for i, (n, mine, other) in enumerate(zip(early, early_sums, sw1.results)):
        update(n, [mine, other], comm=s7 if i == 0 else None)
    update("w_in", [mid_sums[0], their_in])
    update("ffn1_w_down", [mid_sums[1], their_d1])
    sum_g1, half_u1 = sum_chips(recv_g1), sum_chips(s7.results[0])
    their_g1, their_u1 = _run_comm("swap_last", swap_comm([sum_g1, half_u1]))
    update("ffn1_w_gate", [sum_g1, their_g1])
    update("ffn1_w_up", [half_u1, their_u1], halves=True)

    gl = dict(ffn1_norm=d_ffn1_norm, mix_norm=d_mix_norm, ffn2_norm=d_ffn2_norm, final_norm=d_final,
              conv_b=d_conv_b, conv_ln_g=d_ln_g, conv_ln_b=d_ln_b)
    small_sizes = {n: w[n].shape[0] for n in SMALL_NAMES}
    packed = _pack_small(gl, d_bias, d_conv_w, loss_tile)
    red = _unpack_small(allreduce_small(packed), small_sizes, HALO * C // LANES)
    loss = red["loss"]
    my_chip = 2 * lax.axis_index("x") + lax.axis_index("y")
    g_conv_w = lax.dynamic_slice_in_dim(red["conv_w"].reshape(HALO, C)[:CONV_K], my_chip * cs, cs, axis=1)
    update("conv_w", [g_conv_w])
    vec_names = SMALL_NAMES + ("fgate_bias",)
    stack = lambda d: jnp.concatenate(
        [jnp.pad(d[n], (0, (-d[n].shape[0]) % LANES)).reshape(-1, LANES) for n in vec_names], axis=0)
    g_stack = jnp.concatenate([red[n].reshape(-1, LANES) for n in SMALL_NAMES] + [red["fgate_bias"][None, :]],
                              axis=0)
    outs = adamw(stack(w), stack(m), stack(v), [g_stack])
    r = 0
    for n in vec_names:
        size = w[n].shape[0]
        k = -(-size // LANES)
        for dst, src in zip((grad, delta, new_m, new_v), outs):
            dst[n] = src[r:r + k].reshape(-1)[:size]
        r += k

    return (loss, grad_x[None], *[grad[n] for n in names], *[delta[n] for n in names],
            *[new_m[n] for n in names], *[new_v[n] for n in names])
```

```python
import functools
import math

import jax
import jax.numpy as jnp
from jax import lax
from jax.experimental import pallas as pl
from jax.experimental.pallas import tpu as pltpu

F32 = jnp.float32
BF16 = jnp.bfloat16
NORM_EPS = 1e-6
LN_EPS = 1e-5
NEG_INF = -1e30
HEAD_DIM = 64
CONV_K = 31
HALO = 32
LANES = 128
N_CHIP = 4
N_DEV = 8
VMEM_LIMIT = 52 * 1024 * 1024
MESH = pl.DeviceIdType.MESH

ADAM_LR = 0.001
ADAM_B1 = 0.9
ADAM_B2 = 0.999
ADAM_EPS = 1e-08
ADAM_WD = 0.01
ADAM_STEP = 10

NN = (((1,), (0,)), ((), ()))
NT = (((1,), (1,)), ((), ()))
TN = (((0,), (0,)), ((), ()))


def _tile(n, pref, unit=128):
    if n <= pref:
        return n
    t = (pref // unit) * unit
    while t > 0:
        if n % t == 0:
            return t
        t -= unit
    raise ValueError(f"no tile for {n} under {pref}")


RELAY_AT = (3, 4)


class Comm:
    def __init__(self, operands, out_shape, sems, start, finish, relay=None):
        self.operands, self.out_shape, self.sems = list(operands), list(out_shape), list(sems)
        self.start, self.finish, self.relay = start, finish, relay
        self.results = None


def _pcall(body, *, name, grid, in_specs, out_specs, out_shape, scratch=(), comm=None):
    params = pltpu.CompilerParams(dimension_semantics=("arbitrary",) * len(grid), vmem_limit_bytes=VMEM_LIMIT)
    scratch = list(scratch)
    if comm is None:
        return pl.pallas_call(body, name=name, grid=grid, in_specs=in_specs, out_specs=out_specs,
                              out_shape=out_shape, scratch_shapes=scratch, compiler_params=params)
    n_in, n_out, n_s = len(in_specs), len(out_shape), len(scratch)
    n_ci, n_co = len(comm.operands), len(comm.out_shape)
    any_spec = pl.BlockSpec(memory_space=pl.ANY)

    def carried(*refs):
        ins, refs = refs[:n_in], refs[n_in:]
        c_ins, refs = refs[:n_ci], refs[n_ci:]
        outs, refs = refs[:n_out], refs[n_out:]
        c_outs, refs = refs[:n_co], refs[n_co:]
        scr, c_sems = refs[:n_s], refs[n_s:]
        step = pl.program_id(0)
        for d in range(1, len(grid)):
            step = step * grid[d] + pl.program_id(d)
        total = math.prod(grid)
        first, last = step == 0, step == total - 1

        @pl.when(first)
        def _():
            comm.start(c_ins, c_outs, c_sems)

        if comm.relay is not None:
            @pl.when(step == min(total - 1, (RELAY_AT[0] * total) // RELAY_AT[1]))
            def _():
                comm.relay(c_ins, c_outs, c_sems)

        body(*ins, *outs, *scr)

        @pl.when(last)
        def _():
            comm.finish(c_ins, c_outs, c_sems)

    call = pl.pallas_call(
        carried, name=name, grid=grid, in_specs=list(in_specs) + [any_spec] * n_ci,
        out_specs=list(out_specs) + [any_spec] * n_co, out_shape=list(out_shape) + comm.out_shape,
        scratch_shapes=scratch + comm.sems, compiler_params=params)

    def run(*operands):
        res = call(*operands, *comm.operands)
        comm.results = list(res[n_out:])
        return list(res[:n_out])

    return run


def _run_comm(name, comm):
    n_ci, n_co = len(comm.operands), len(comm.out_shape)
    any_spec = pl.BlockSpec(memory_space=pl.ANY)

    def body(*refs):
        c_ins, c_outs, c_sems = refs[:n_ci], refs[n_ci:n_ci + n_co], refs[n_ci + n_co:]
        comm.start(c_ins, c_outs, c_sems)
        if comm.relay is not None:
            comm.relay(c_ins, c_outs, c_sems)
        comm.finish(c_ins, c_outs, c_sems)

    return pl.pallas_call(body, name=name, in_specs=[any_spec] * n_ci, out_specs=[any_spec] * n_co,
                          out_shape=comm.out_shape, scratch_shapes=comm.sems)(*comm.operands)


def _sigmoid(x):
    return 1.0 / (1.0 + jnp.exp(-x))


def _mm(name, *, grid, pairs, once_pairs=(), extra=(), out_shape, out_specs, acc_shapes, nk, kaxis, epilogue,
        comm=None):
    all_pairs = list(pairs) + list(once_pairs)
    n_p, n_o = len(pairs), len(once_pairs)
    n_e, n_out, n_acc = len(extra), len(out_shape), len(acc_shapes)

    def body(*refs):
        ab = refs[: 2 * (n_p + n_o)]
        ex = refs[2 * (n_p + n_o): 2 * (n_p + n_o) + n_e]
        outs = refs[2 * (n_p + n_o) + n_e: 2 * (n_p + n_o) + n_e + n_out]
        accs = refs[2 * (n_p + n_o) + n_e + n_out:]

        def dots(idx_range):
            vals = [None] * n_acc
            for p in idx_range:
                d = lax.dot_general(ab[2 * p][...], ab[2 * p + 1][...], all_pairs[p][4],
                                    preferred_element_type=F32)
                ai = all_pairs[p][5]
                vals[ai] = d if vals[ai] is None else vals[ai] + d
            return vals

        if nk == 1:
            vals = dots(range(n_p + n_o))
            epilogue(vals, ex, outs)
            return

        k = pl.program_id(kaxis)

        @pl.when(k == 0)
        def _():
            vals = dots(range(n_p + n_o))
            for ai in range(n_acc):
                accs[ai][...] = vals[ai]

        @pl.when(k > 0)
        def _():
            vals = dots(range(n_p))
            for ai in range(n_acc):
                if vals[ai] is not None:
                    accs[ai][...] += vals[ai]

        @pl.when(k == nk - 1)
        def _():
            epilogue([a[...] for a in accs], ex, outs)

    operands, in_specs = [], []
    for p in all_pairs:
        operands += [p[0], p[2]]
        in_specs += [p[1], p[3]]
    for arr, spec in extra:
        operands.append(arr)
        in_specs.append(spec)
    scratch = [pltpu.VMEM(s, F32) for s in acc_shapes] if nk > 1 else []
    return _pcall(body, name=name, grid=grid, in_specs=in_specs, out_specs=out_specs, out_shape=out_shape,
                  scratch=scratch, comm=comm)(*operands)


def rms_fwd(x, g):
    T, D = x.shape
    tt = _tile(T, 512, 8)

    def body(x_ref, g_ref, h_ref, r_ref):
        xv = x_ref[...]
        r = lax.rsqrt(jnp.mean(xv * xv, axis=-1, keepdims=True) + NORM_EPS)
        h_ref[...] = (xv * r * g_ref[...]).astype(BF16)
        r_ref[...] = r

    return _pcall(
        body, name="rms_fwd", grid=(T // tt,),
        in_specs=[pl.BlockSpec((tt, D), lambda i: (i, 0)), pl.BlockSpec((1, D), lambda i: (0, 0))],
        out_specs=[pl.BlockSpec((tt, D), lambda i: (i, 0)), pl.BlockSpec((tt, 1), lambda i: (i, 0))],
        out_shape=[jax.ShapeDtypeStruct((T, D), BF16), jax.ShapeDtypeStruct((T, 1), F32)],
    )(x, g)


def rms_bwd(dh, x, r, g, dres, out_scale):
    T, D = x.shape
    tt = _tile(T, 256, 8)

    def body(dh_ref, x_ref, r_ref, g_ref, dres_ref, dx_ref, dxb_ref, dg_ref):
        i = pl.program_id(0)
        xh = x_ref[...] * r_ref[...]
        dhv = dh_ref[...]
        dxh = dhv * g_ref[...]
        dx = dres_ref[...] + r_ref[...] * (dxh - xh * jnp.mean(dxh * xh, axis=-1, keepdims=True))
        dx_ref[...] = dx
        dxb_ref[...] = (out_scale * dx).astype(BF16)
        part = jnp.sum(dhv * xh, axis=0, keepdims=True)

        @pl.when(i == 0)
        def _():
            dg_ref[...] = part

        @pl.when(i > 0)
        def _():
            dg_ref[...] += part

    row = pl.BlockSpec((tt, D), lambda i: (i, 0))
    return _pcall(
        body, name="rms_bwd", grid=(T // tt,),
        in_specs=[row, row, pl.BlockSpec((tt, 1), lambda i: (i, 0)), pl.BlockSpec((1, D), lambda i: (0, 0)), row],
        out_specs=[row, row, pl.BlockSpec((1, D), lambda i: (0, 0))],
        out_shape=[jax.ShapeDtypeStruct((T, D), F32), jax.ShapeDtypeStruct((T, D), BF16),
                   jax.ShapeDtypeStruct((1, D), F32)],
    )(dh, x, r, g, dres)


def final_loss(x, tgt, g):
    T, D = x.shape
    tt = _tile(T, 256, 8)

    def body(x_ref, t_ref, g_ref, dx_ref, dxb_ref, loss_ref, dg_ref):
        i = pl.program_id(0)
        xv = x_ref[...]
        r = lax.rsqrt(jnp.mean(xv * xv, axis=-1, keepdims=True) + NORM_EPS)
        xh = xv * r
        err = xh * g_ref[...] - t_ref[...]
        part_loss = 0.5 * jnp.sum(jnp.mean(err * err, axis=-1, keepdims=True), axis=0, keepdims=True)
        dy = err * (1.0 / D)
        dxh = dy * g_ref[...]
        dx = r * (dxh - xh * jnp.mean(dxh * xh, axis=-1, keepdims=True))
        dx_ref[...] = dx
        dxb_ref[...] = (0.5 * dx).astype(BF16)
        part_g = jnp.sum(dy * xh, axis=0, keepdims=True)
        part_l = jnp.broadcast_to(part_loss, (8, LANES))

        @pl.when(i == 0)
        def _():
            dg_ref[...] = part_g
            loss_ref[...] = part_l

        @pl.when(i > 0)
        def _():
            dg_ref[...] += part_g
            loss_ref[...] += part_l

    row = pl.BlockSpec((tt, D), lambda i: (i, 0))
    return _pcall(
        body, name="final_loss", grid=(T // tt,),
        in_specs=[row, row, pl.BlockSpec((1, D), lambda i: (0, 0))],
        out_specs=[row, row, pl.BlockSpec((8, LANES), lambda i: (0, 0)), pl.BlockSpec((1, D), lambda i: (0, 0))],
        out_shape=[jax.ShapeDtypeStruct((T, D), F32), jax.ShapeDtypeStruct((T, D), BF16),
                   jax.ShapeDtypeStruct((8, LANES), F32), jax.ShapeDtypeStruct((1, D), F32)],
    )(x, tgt, g)


def ffn_gate(h, wg3, comm=None):
    T, D = h.shape
    nc, _, fs = wg3.shape
    tm = _tile(T, 512, 8)

    def epilogue(vals, ex, outs):
        outs[0][...] = vals[0].astype(BF16)

    return _mm("ffn_gate", grid=(nc, T // tm),
               pairs=[(h, pl.BlockSpec((tm, D), lambda j, i: (i, 0)),
                       wg3, pl.BlockSpec((None, D, fs), lambda j, i: (j, 0, 0)), NN, 0)],
               out_shape=[jax.ShapeDtypeStruct((T, nc * fs), BF16)],
               out_specs=[pl.BlockSpec((tm, fs), lambda j, i: (i, j))],
               acc_shapes=[(tm, fs)], nk=1, kaxis=None, epilogue=epilogue, comm=comm)[0]


def ffn_upmul(h, wu3, a, comm=None):
    T, D = h.shape
    nc, _, fs = wu3.shape
    tm = _tile(T, 512, 8)

    def epilogue(vals, ex, outs):
        b = vals[0]
        av = ex[0][...].astype(F32)
        outs[0][...] = b.astype(BF16)
        outs[1][...] = (av * _sigmoid(av) * b).astype(BF16)

    t_spec = pl.BlockSpec((tm, fs), lambda j, i: (i, j))
    o_shape = jax.ShapeDtypeStruct((T, nc * fs), BF16)
    return _mm("ffn_upmul", grid=(nc, T // tm),
               pairs=[(h, pl.BlockSpec((tm, D), lambda j, i: (i, 0)),
                       wu3, pl.BlockSpec((None, D, fs), lambda j, i: (j, 0, 0)), NN, 0)],
               extra=[(a, t_spec)], out_shape=[o_shape] * 2, out_specs=[t_spec] * 2,
               acc_shapes=[(tm, fs)], nk=1, kaxis=None, epilogue=epilogue, comm=comm)


def ffn_up(h, wg3, wu3, comm=None):
    T, D = h.shape
    nc, _, fs = wg3.shape
    tm = _tile(T, 512, 8)

    def epilogue(vals, ex, outs):
        a, b = vals
        outs[0][...] = a.astype(BF16)
        outs[1][...] = b.astype(BF16)
        outs[2][...] = (a * _sigmoid(a) * b).astype(BF16)

    h_spec = pl.BlockSpec((tm, D), lambda j, i: (i, 0))
    w_spec = pl.BlockSpec((None, D, fs), lambda j, i: (j, 0, 0))
    o_spec = pl.BlockSpec((tm, fs), lambda j, i: (i, j))
    o_shape = jax.ShapeDtypeStruct((T, nc * fs), BF16)
    return _mm("ffn_up", grid=(nc, T // tm),
               pairs=[(h, h_spec, wg3, w_spec, NN, 0), (h, h_spec, wu3, w_spec, NN, 1)],
               out_shape=[o_shape] * 3, out_specs=[o_spec] * 3, acc_shapes=[(tm, fs)] * 2, nk=1, kaxis=None,
               epilogue=epilogue, comm=comm)


def mm_residual(name, a, b3, res, scale, comm=None):
    T = a.shape[0]
    nk, tk, N = b3.shape
    tm, tn = _tile(T, 512, 8), _tile(N, 2048)

    def epilogue(vals, ex, outs):
        outs[0][...] = ex[0][...] + scale * vals[0]

    return _mm(name, grid=(T // tm, N // tn, nk),
               pairs=[(a, pl.BlockSpec((tm, tk), lambda i, n, k: (i, k)),
                       b3, pl.BlockSpec((None, tk, tn), lambda i, n, k: (k, 0, n)), NN, 0)],
               extra=[(res, pl.BlockSpec((tm, tn), lambda i, n, k: (i, n)))],
               out_shape=[jax.ShapeDtypeStruct((T, N), F32)],
               out_specs=[pl.BlockSpec((tm, tn), lambda i, n, k: (i, n))],
               acc_shapes=[(tm, tn)], nk=nk, kaxis=2, epilogue=epilogue, comm=comm)


def ffn_bwd_mid(dout, wd3, a, b, comm=None):
    T, D = dout.shape
    nc, fs, _ = wd3.shape
    tm = _tile(T, 512, 8)

    def epilogue(vals, ex, outs):
        dm = vals[0]
        av = ex[0][...].astype(F32)
        bv = ex[1][...].astype(F32)
        s = _sigmoid(av)
        outs[0][...] = (dm * bv * (s * (1.0 + av * (1.0 - s)))).astype(BF16)
        outs[1][...] = (dm * (av * s)).astype(BF16)

    t_spec = pl.BlockSpec((tm, fs), lambda j, i: (i, j))
    o_shape = jax.ShapeDtypeStruct((T, nc * fs), BF16)
    return _mm("ffn_bwd_mid", grid=(nc, T // tm),
               pairs=[(dout, pl.BlockSpec((tm, D), lambda j, i: (i, 0)),
                       wd3, pl.BlockSpec((None, fs, D), lambda j, i: (j, 0, 0)), NT, 0)],
               extra=[(a, t_spec), (b, t_spec)],
               out_shape=[o_shape] * 2, out_specs=[t_spec] * 2, acc_shapes=[(tm, fs)], nk=1, kaxis=None,
               epilogue=epilogue, comm=comm)


def dw_rowshard(name, a, b, nc, comm=None):
    T, M = a.shape
    N = b.shape[1]
    ms = M // nc
    tn, tk = _tile(N, 2048 if ms <= 512 else 1024), _tile(T, 1024, 16)

    def epilogue(vals, ex, outs):
        outs[0][...] = vals[0].astype(BF16)

    return _mm(name, grid=(nc, N // tn, T // tk),
               pairs=[(a, pl.BlockSpec((tk, ms), lambda j, n, k: (k, j)),
                       b, pl.BlockSpec((tk, tn), lambda j, n, k: (k, n)), TN, 0)],
               out_shape=[jax.ShapeDtypeStruct((nc, ms, N), BF16)],
               out_specs=[pl.BlockSpec((None, ms, tn), lambda j, n, k: (j, 0, n))],
               acc_shapes=[(ms, tn)], nk=T // tk, kaxis=2, epilogue=epilogue, comm=comm)


def dw_colshard(name, a, bs, nc, comm=None):
    T, M = a.shape
    ns = bs[0].shape[1] // nc
    tm, tk = _tile(M, 512), _tile(T, 1024, 16)

    def epilogue(vals, ex, outs):
        for v, o in zip(vals, outs):
            o[...] = v.astype(BF16)

    a_spec = pl.BlockSpec((tk, tm), lambda j, m, k: (k, m))
    b_spec = pl.BlockSpec((tk, ns), lambda j, m, k: (k, j))
    return _mm(name, grid=(nc, M // tm, T // tk),
               pairs=[(a, a_spec, b, b_spec, TN, p) for p, b in enumerate(bs)],
               out_shape=[jax.ShapeDtypeStruct((nc, M, ns), BF16)] * len(bs),
               out_specs=[pl.BlockSpec((None, tm, ns), lambda j, m, k: (j, m, 0))] * len(bs),
               acc_shapes=[(tm, ns)] * len(bs), nk=T // tk, kaxis=2, epilogue=epilogue, comm=comm)


def ffn_dh(da, db, wg3, wu3, comm=None):
    T = da.shape[0]
    nc, D, fs = wg3.shape
    tm, tn = _tile(T, 512, 8), _tile(D, 1024)

    def epilogue(vals, ex, outs):
        outs[0][...] = vals[0]

    a_spec = pl.BlockSpec((tm, fs), lambda i, n, k: (i, k))
    w_spec = pl.BlockSpec((None, tn, fs), lambda i, n, k: (k, n, 0))
    return _mm("ffn_dh", grid=(T // tm, D // tn, nc),
               pairs=[(da, a_spec, wg3, w_spec, NT, 0), (db, a_spec, wu3, w_spec, NT, 0)],
               out_shape=[jax.ShapeDtypeStruct((T, D), F32)],
               out_specs=[pl.BlockSpec((tm, tn), lambda i, n, k: (i, n))],
               acc_shapes=[(tm, tn)], nk=nc, kaxis=2, epilogue=epilogue, comm=comm)[0]


def proj_main(h, w_t, P, comm=None):
    T, D = h.shape
    tm, tn = _tile(T, 512, 8), _tile(P, 1024)

    def epilogue(vals, ex, outs):
        outs[0][...] = vals[0].astype(BF16)

    return _mm("proj_main", grid=(P // tn, T // tm),
               pairs=[(h, pl.BlockSpec((tm, D), lambda j, i: (i, 0)),
                       w_t, pl.BlockSpec((tn, D), lambda j, i: (j, 0)), NT, 0)],
               out_shape=[jax.ShapeDtypeStruct((T, P), BF16)],
               out_specs=[pl.BlockSpec((tm, tn), lambda j, i: (i, j))],
               acc_shapes=[(tm, tn)], nk=1, kaxis=None, epilogue=epilogue, comm=comm)[0]


def mm_nt_bf16(name, a, w):
    T, K = a.shape
    M = w.shape[0]
    tm, tn = _tile(T, 512, 8), _tile(M, 1024)

    def epilogue(vals, ex, outs):
        outs[0][...] = vals[0].astype(BF16)

    return _mm(name, grid=(T // tm, M // tn),
               pairs=[(a, pl.BlockSpec((tm, K), lambda i, n: (i, 0)),
                       w, pl.BlockSpec((tn, K), lambda i, n: (n, 0)), NT, 0)],
               out_shape=[jax.ShapeDtypeStruct((T, M), BF16)],
               out_specs=[pl.BlockSpec((tm, tn), lambda i, n: (i, n))],
               acc_shapes=[(tm, tn)], nk=1, kaxis=None, epilogue=epilogue)[0]


def proj_dh(dproj, w_t, df, wf_t, comm=None):
    T, P = dproj.shape
    D = w_t.shape[1]
    tm, tn, tk = _tile(T, 512, 8), _tile(D, 2048), _tile(P, 1280)

    def epilogue(vals, ex, outs):
        outs[0][...] = vals[0]

    return _mm("proj_dh", grid=(T // tm, D // tn, P // tk),
               pairs=[(dproj, pl.BlockSpec((tm, tk), lambda i, n, k: (i, k)),
                       w_t, pl.BlockSpec((tk, tn), lambda i, n, k: (k, n)), NN, 0)],
               once_pairs=[(df, pl.BlockSpec((tm, LANES), lambda i, n, k: (i, 0)),
                            wf_t, pl.BlockSpec((LANES, tn), lambda i, n, k: (0, n)), NN, 0)],
               out_shape=[jax.ShapeDtypeStruct((T, D), F32)],
               out_specs=[pl.BlockSpec((tm, tn), lambda i, n, k: (i, n))],
               acc_shapes=[(tm, tn)], nk=P // tk, kaxis=2, epilogue=epilogue, comm=comm)[0]


def proj_dw(dproj, df, h, rows):
    T, P = dproj.shape
    D = h.shape[1]
    tm, tn, tk = _tile(P, 1280), _tile(D, 1024), _tile(T, 1024, 16)

    def to_bf16(vals, ex, outs):
        outs[0][...] = vals[0].astype(BF16)

    def to_f32(vals, ex, outs):
        outs[0][...] = vals[0]

    main = _mm("proj_dw_main", grid=(P // tm, D // tn, T // tk),
               pairs=[(dproj, pl.BlockSpec((tk, tm), lambda m, n, k: (k, m)),
                       h, pl.BlockSpec((tk, tn), lambda m, n, k: (k, n)), TN, 0)],
               out_shape=[jax.ShapeDtypeStruct((rows, D), BF16)],
               out_specs=[pl.BlockSpec((tm, tn), lambda m, n, k: (m, n))],
               acc_shapes=[(tm, tn)], nk=T // tk, kaxis=2, epilogue=to_bf16)[0]
    gate = _mm("proj_dw_f", grid=(1, D // tn, T // tk),
               pairs=[(df, pl.BlockSpec((tk, LANES), lambda m, n, k: (k, 0)),
                       h, pl.BlockSpec((tk, tn), lambda m, n, k: (k, n)), TN, 0)],
               out_shape=[jax.ShapeDtypeStruct((LANES, D), F32)],
               out_specs=[pl.BlockSpec((LANES, tn), lambda m, n, k: (0, n))],
               acc_shapes=[(LANES, tn)], nk=T // tk, kaxis=2, epilogue=to_f32)[0]
    return main, gate


def fgate_fwd(h, wf_t, bias, n_heads):
    T, D = h.shape
    tt = _tile(T, 512, 8)

    def body(h_ref, w_ref, b_ref, f_ref, c_ref, carry):
        i = pl.program_id(0)

        @pl.when(i == 0)
        def _():
            carry[...] = jnp.zeros_like(carry)

        f = lax.dot_general(h_ref[...], w_ref[...], NT, preferred_element_type=F32) + b_ref[...]
        logf = jnp.minimum(f, 0.0) - jnp.log(1.0 + jnp.exp(-jnp.abs(f)))
        tri = (lax.broadcasted_iota(jnp.int32, (tt, tt), 0) >= lax.broadcasted_iota(jnp.int32, (tt, tt), 1))
        cs = jnp.dot(tri.astype(F32), logf, preferred_element_type=F32, precision=lax.Precision.HIGHEST)
        c = cs + carry[...]
        f_ref[...] = f
        c_ref[...] = c
        carry[...] = c[tt - 1:tt, :]

    row = pl.BlockSpec((tt, LANES), lambda i: (i, 0))
    return _pcall(
        body, name="fgate_fwd", grid=(T // tt,),
        in_specs=[pl.BlockSpec((tt, D), lambda i: (i, 0)), pl.BlockSpec((LANES, D), lambda i: (0, 0)),
                  pl.BlockSpec((1, LANES), lambda i: (0, 0))],
        out_specs=[row, row],
        out_shape=[jax.ShapeDtypeStruct((T, LANES), F32)] * 2,
        scratch=[pltpu.VMEM((1, LANES), F32)],
    )(h, wf_t, bias)


def fgate_bwd(dc, f, n_heads):
    T = dc.shape[0]
    tt = _tile(T, 512, 8)
    nt = T // tt

    def body(dc_ref, f_ref, df_ref, db_ref, carry):
        i = pl.program_id(0)

        @pl.when(i == 0)
        def _():
            carry[...] = jnp.zeros_like(carry)

        tri = (lax.broadcasted_iota(jnp.int32, (tt, tt), 1) >= lax.broadcasted_iota(jnp.int32, (tt, tt), 0))
        rs = jnp.dot(tri.astype(F32), dc_ref[...], preferred_element_type=F32,
                     precision=lax.Precision.HIGHEST) + carry[...]
        carry[...] = rs[0:1, :]
        lane = lax.broadcasted_iota(jnp.int32, (tt, LANES), 1)
        df = jnp.where(lane < n_heads, rs * _sigmoid(-f_ref[...]), 0.0)
        df_ref[...] = df.astype(BF16)
        part = jnp.sum(df, axis=0, keepdims=True)

        @pl.when(i == 0)
        def _():
            db_ref[...] = part

        @pl.when(i > 0)
        def _():
            db_ref[...] += part

    rev = pl.BlockSpec((tt, LANES), lambda i: (nt - 1 - i, 0))
    return _pcall(
        body, name="fgate_bwd", grid=(nt,),
        in_specs=[rev, rev],
        out_specs=[rev, pl.BlockSpec((1, LANES), lambda i: (0, 0))],
        out_shape=[jax.ShapeDtypeStruct((T, LANES), BF16), jax.ShapeDtypeStruct((1, LANES), F32)],
        scratch=[pltpu.VMEM((1, LANES), F32)],
    )(dc, f)


SUBLANES = 8
SHIFT_ROWS = HALO - SUBLANES


def _shifted_copies(buf, sh, tt):
    for r in range(1, SUBLANES):
        sh[r - 1, 0:tt + SHIFT_ROWS, :] = buf[pl.ds(r, tt + SHIFT_ROWS), :]


def _tap(buf, sh, offset, tt):
    q, r = divmod(offset, SUBLANES)
    if r == 0:
        return buf[pl.ds(SUBLANES * q, tt), :]
    return sh[r - 1, pl.ds(SUBLANES * q, tt), :]


def conv_fwd(proj, conv_w, conv_b, ln_g, ln_b):
    T = proj.shape[0]
    C = conv_w.shape[1]
    tt = _tile(T, 256, HALO)
    hb = tt // HALO

    def body(a_ref, g_ref, ah_ref, gh_ref, w_ref, cb_ref, lg_ref, lb_ref, ypre_ref, y_ref, ubuf, ush):
        i = pl.program_id(0)
        u = a_ref[...].astype(F32) * _sigmoid(g_ref[...].astype(F32))
        uh = ah_ref[...].astype(F32) * _sigmoid(gh_ref[...].astype(F32))
        ubuf[0:HALO, :] = jnp.where(i == 0, 0.0, uh)
        ubuf[HALO:HALO + tt, :] = u
        _shifted_copies(ubuf, ush, tt)
        acc = jnp.broadcast_to(cb_ref[...], (tt, C))
        for k in range(CONV_K):
            acc = acc + w_ref[k:k + 1, :] * _tap(ubuf, ush, HALO - (CONV_K - 1) + k, tt)
        ypre_ref[...] = acc
        mu = jnp.mean(acc, axis=-1, keepdims=True)
        d = acc - mu
        rstd = lax.rsqrt(jnp.mean(d * d, axis=-1, keepdims=True) + LN_EPS)
        z = d * rstd * lg_ref[...] + lb_ref[...]
        y_ref[...] = (z * _sigmoid(z)).astype(BF16)

    vec = pl.BlockSpec((1, C), lambda i: (0, 0))
    return _pcall(
        body, name="conv_fwd", grid=(T // tt,),
        in_specs=[pl.BlockSpec((tt, C), lambda i: (i, 0)), pl.BlockSpec((tt, C), lambda i: (i, 1)),
                  pl.BlockSpec((HALO, C), lambda i: (jnp.maximum(i * hb - 1, 0), 0)),
                  pl.BlockSpec((HALO, C), lambda i: (jnp.maximum(i * hb - 1, 0), 1)),
                  pl.BlockSpec((HALO, C), lambda i: (0, 0)), vec, vec, vec],
        out_specs=[pl.BlockSpec((tt, C), lambda i: (i, 0))] * 2,
        out_shape=[jax.ShapeDtypeStruct((T, C), F32), jax.ShapeDtypeStruct((T, C), BF16)],
        scratch=[pltpu.VMEM((tt + HALO, C), F32), pltpu.VMEM((SUBLANES - 1, tt + SHIFT_ROWS, C), F32)],
    )(proj, proj, proj, proj, conv_w, conv_b, ln_g, ln_b)


def conv_bwd(proj, ypre, dycat, conv_w, ln_g, ln_b):
    T = proj.shape[0]
    C = conv_w.shape[1]
    tt = _tile(T, 256, HALO)
    hb = tt // HALO
    nt = T // tt
    last_h = T // HALO - 1

    def ln_bwd(ypre_v, dout_v, lg, lb):
        mu = jnp.mean(ypre_v, axis=-1, keepdims=True)
        d = ypre_v - mu
        rstd = lax.rsqrt(jnp.mean(d * d, axis=-1, keepdims=True) + LN_EPS)
        yh = d * rstd
        z = yh * lg + lb
        s = _sigmoid(z)
        dz = dout_v * (s * (1.0 + z * (1.0 - s)))
        dyh = dz * lg
        dy = rstd * (dyh - jnp.mean(dyh, axis=-1, keepdims=True)
                     - yh * jnp.mean(dyh * yh, axis=-1, keepdims=True))
        return dy, dz, yh

    def body(a_ref, g_ref, ah_ref, gh_ref, yp_ref, ypn_ref, do_ref, don_ref, w_ref, lg_ref, lb_ref,
             dag_ref, dw_ref, dcb_ref, dlg_ref, dlb_ref, ubuf, dybuf, ush, dysh):
        i = pl.program_id(0)
        av = a_ref[...].astype(F32)
        sg = _sigmoid(g_ref[...].astype(F32))
        uh = ah_ref[...].astype(F32) * _sigmoid(gh_ref[...].astype(F32))
        ubuf[0:HALO, :] = jnp.where(i == 0, 0.0, uh)
        ubuf[HALO:HALO + tt, :] = av * sg
        lg, lb = lg_ref[...], lb_ref[...]
        dy, dz, yh = ln_bwd(yp_ref[...], do_ref[...].astype(F32), lg, lb)
        dyn, _, _ = ln_bwd(ypn_ref[...], don_ref[...].astype(F32), lg, lb)
        dybuf[0:tt, :] = dy
        dybuf[tt:tt + HALO, :] = jnp.where(i == nt - 1, 0.0, dyn)
        _shifted_copies(ubuf, ush, tt)
        _shifted_copies(dybuf, dysh, tt)

        @pl.when(i == 0)
        def _():
            dw_ref[...] = jnp.zeros_like(dw_ref)
            dcb_ref[...] = jnp.zeros_like(dcb_ref)
            dlg_ref[...] = jnp.zeros_like(dlg_ref)
            dlb_ref[...] = jnp.zeros_like(dlb_ref)

        du = jnp.zeros((tt, C), F32)
        for k in range(CONV_K):
            du = du + w_ref[k:k + 1, :] * _tap(dybuf, dysh, CONV_K - 1 - k, tt)
            dw_ref[k:k + 1, :] += jnp.sum(dy * _tap(ubuf, ush, HALO - (CONV_K - 1) + k, tt), axis=0, keepdims=True)
        dcb_ref[...] += jnp.sum(dy, axis=0, keepdims=True)
        dlg_ref[...] += jnp.sum(dz * yh, axis=0, keepdims=True)
        dlb_ref[...] += jnp.sum(dz, axis=0, keepdims=True)

        dag_ref[:, 0:C] = (du * sg).astype(BF16)
        dag_ref[:, C:2 * C] = (du * av * sg * (1.0 - sg)).astype(BF16)

    vec = pl.BlockSpec((1, C), lambda i: (0, 0))
    prev_h = lambda col: pl.BlockSpec((HALO, C), lambda i: (jnp.maximum(i * hb - 1, 0), col))
    next_h = pl.BlockSpec((HALO, C), lambda i: (jnp.minimum((i + 1) * hb, last_h), 0))
    return _pcall(
        body, name="conv_bwd", grid=(nt,),
        in_specs=[pl.BlockSpec((tt, C), lambda i: (i, 0)), pl.BlockSpec((tt, C), lambda i: (i, 1)),
                  prev_h(0), prev_h(1),
                  pl.BlockSpec((tt, C), lambda i: (i, 0)), next_h,
                  pl.BlockSpec((tt, C), lambda i: (i, 0)), next_h,
                  pl.BlockSpec((HALO, C), lambda i: (0, 0)), vec, vec],
        out_specs=[pl.BlockSpec((tt, 2 * C), lambda i: (i, 0)), pl.BlockSpec((HALO, C), lambda i: (0, 0)),
                   vec, vec, vec],
        out_shape=[jax.ShapeDtypeStruct((T, 2 * C), BF16), jax.ShapeDtypeStruct((HALO, C), F32),
                   jax.ShapeDtypeStruct((1, C), F32), jax.ShapeDtypeStruct((1, C), F32),
                   jax.ShapeDtypeStruct((1, C), F32)],
        scratch=[pltpu.VMEM((tt + HALO, C), F32), pltpu.VMEM((tt + HALO, C), F32),
                 pltpu.VMEM((SUBLANES - 1, tt + SHIFT_ROWS, C), F32),
                 pltpu.VMEM((SUBLANES - 1, tt + SHIFT_ROWS, C), F32)],
    )(proj, proj, proj, proj, ypre, ypre, dycat, dycat, conv_w, ln_g, ln_b)


PAIR = LANES // HEAD_DIM


def _head_masks(rows):
    lane = lax.broadcasted_iota(jnp.int32, (rows, LANES), 1)
    return [jnp.logical_and(lane >= hh * HEAD_DIM, lane < (hh + 1) * HEAD_DIM) for hh in range(PAIR)]


def _causal(tq, tk):
    return lax.broadcasted_iota(jnp.int32, (tq, tk), 0) >= lax.broadcasted_iota(jnp.int32, (tq, tk), 1)


def _lane_column(block, lane_index):
    lane = lax.broadcasted_iota(jnp.int32, block.shape, 1)
    return jnp.sum(jnp.where(lane == lane_index, block, 0.0), axis=-1, keepdims=True)


def attn_fwd(proj, cum, ck4, q_col, comm=None):
    T = proj.shape[0]
    H, nkv, _, tk = ck4.shape
    tq = tk
    hd = H * HEAD_DIM
    qb, kb, vb = q_col // LANES, (q_col + hd) // LANES, (q_col + 2 * hd) // LANES
    scale = 1.0 / math.sqrt(HEAD_DIM)

    def body(q_ref, k_ref, v_ref, cum_ref, ck_ref, o_ref, lse_ref):
        hp = pl.program_id(0)
        i = pl.program_id(1)
        masks = _head_masks(tq)
        q2 = q_ref[...] * scale
        qs = [jnp.where(mk, q2, jnp.zeros_like(q2)) for mk in masks]
        cqs = [_lane_column(cum_ref[...], PAIR * hp + hh) for hh in range(PAIR)]

        def step(j, carry, diagonal, blocks=1):
            scores, values = [], []
            for b in range(blocks):
                off = pl.multiple_of((j + b) * tk, tk)
                kj = k_ref[pl.ds(off, tk), :]
                values.append(v_ref[pl.ds(off, tk), :])
                per_head = []
                for hh in range(PAIR):
                    s = lax.dot_general(qs[hh], kj, NT, preferred_element_type=F32)
                    s = s + cqs[hh] - ck_ref[hh, j + b]
                    if diagonal:
                        s = jnp.where(_causal(tq, tk), s, NEG_INF)
                    per_head.append(s)
                scores.append(per_head)
            out = []
            for hh in range(PAIR):
                m, l, acc = carry[hh]
                m_new = m
                for b in range(blocks):
                    m_new = jnp.maximum(m_new, jnp.max(scores[b][hh], axis=-1, keepdims=True))
                alpha = jnp.exp(m - m_new)
                l, acc = alpha * l, alpha * acc
                for b in range(blocks):
                    p = jnp.exp(scores[b][hh] - m_new)
                    l = l + jnp.sum(p, axis=-1, keepdims=True)
                    acc = acc + jnp.dot(p.astype(BF16), values[b], preferred_element_type=F32)
                out.append((m_new, l, acc))
            return tuple(out)

        init = tuple((jnp.full((tq, 1), -jnp.inf, F32), jnp.zeros((tq, 1), F32), jnp.zeros((tq, LANES), F32))
                     for _ in range(PAIR))
        twos = i // 2
        carry = lax.fori_loop(0, twos, lambda t, c: step(2 * t, c, False, blocks=2), init)
        carry = lax.fori_loop(2 * twos, i, functools.partial(step, diagonal=False), carry)
        carry = step(i, carry, True)
        o = carry[PAIR - 1][2] / carry[PAIR - 1][1]
        for hh in range(PAIR - 1):
            o = jnp.where(masks[hh], carry[hh][2] / carry[hh][1], o)
        o_ref[...] = o
        lse = jnp.broadcast_to(carry[PAIR - 1][0] + jnp.log(carry[PAIR - 1][1]), (tq, LANES))
        for hh in range(PAIR - 1):
            lse = jnp.where(masks[hh], carry[hh][0] + jnp.log(carry[hh][1]), lse)
        lse_ref[...] = lse

    return _pcall(
        body, name="attn_fwd", grid=(H // PAIR, T // tq),
        in_specs=[pl.BlockSpec((tq, LANES), lambda hp, i: (i, qb + hp)),
                  pl.BlockSpec((T, LANES), lambda hp, i: (0, kb + hp)),
                  pl.BlockSpec((T, LANES), lambda hp, i: (0, vb + hp)),
                  pl.BlockSpec((tq, LANES), lambda hp, i: (i, 0)),
                  pl.BlockSpec((PAIR, nkv, 1, tk), lambda hp, i: (hp, 0, 0, 0))],
        out_specs=[pl.BlockSpec((tq, LANES), lambda hp, i: (i, hp)),
                   pl.BlockSpec((None, tq, LANES), lambda hp, i: (hp, i, 0))],
        out_shape=[jax.ShapeDtypeStruct((T, hd), F32), jax.ShapeDtypeStruct((H // PAIR, T, LANES), F32)],
        comm=comm,
    )(proj, proj, proj, cum, ck4)


def attn_bwd(proj, o, dycat, lse, cum, ck4, q_col, do_col, comm=None):
    T = proj.shape[0]
    H, nkv, _, tk = ck4.shape
    tq = tk
    nq = T // tq
    hd = H * HEAD_DIM
    qb, kb, vb = q_col // LANES, (q_col + hd) // LANES, (q_col + 2 * hd) // LANES
    dob = do_col // LANES
    scale = 1.0 / math.sqrt(HEAD_DIM)

    def body(q_ref, k_ref, v_ref, o_ref, do_ref, lse_ref, cum_ref, ck_ref,
             dq_ref, dk_ref, dv_ref, dcq_ref, dck_ref):
        hp = pl.program_id(0)
        j = pl.program_id(1)

        def block(i, diagonal):
            masks = _head_masks(tq)
            rows = pl.ds(pl.multiple_of(i * tq, tq), tq)
            q2, k2, v2, do2 = q_ref[rows, :] * scale, k_ref[...], v_ref[...], do_ref[rows, :]
            zero = jnp.zeros_like(q2)
            prod = do2.astype(F32) * o_ref[rows, :]
            cum_q, lse_q = cum_ref[rows, :], lse_ref[rows, :]
            dq_part = dk_part = dv_part = None
            dcq_part = jnp.zeros((tq, LANES), F32)
            lane = lax.broadcasted_iota(jnp.int32, (tq, LANES), 1)
            for hh in range(PAIR):
                qh = jnp.where(masks[hh], q2, zero)
                kh = jnp.where(masks[hh], k2, zero)
                doh = jnp.where(masks[hh], do2, zero)
                delta = jnp.sum(jnp.where(masks[hh], prod, 0.0), axis=-1, keepdims=True)
                s = lax.dot_general(qh, k2, NT, preferred_element_type=F32)
                s = s + _lane_column(cum_q, PAIR * hp + hh) - ck_ref[hh]
                if diagonal:
                    s = jnp.where(_causal(tq, tk), s, NEG_INF)
                p = jnp.exp(s - _lane_column(lse_q, hh * HEAD_DIM))
                dp = lax.dot_general(doh, v2, NT, preferred_element_type=F32)
                ds = p * (dp - delta)
                dsb = ds.astype(BF16)
                dv_h = lax.dot_general(p.astype(BF16), doh, TN, preferred_element_type=F32)
                dk_h = lax.dot_general(dsb, qh, TN, preferred_element_type=F32)
                dq_h = jnp.dot(dsb, kh, preferred_element_type=F32)
                dq_part = dq_h if dq_part is None else dq_part + dq_h
                dk_part = dk_h if dk_part is None else dk_part + dk_h
                dv_part = dv_h if dv_part is None else dv_part + dv_h
                dck_h = -jnp.sum(ds, axis=0, keepdims=True)
                dcq_part = jnp.where(lane == PAIR * hp + hh, jnp.sum(ds, axis=-1, keepdims=True), dcq_part)
                if diagonal:
                    dck_ref[hh] = dck_h
                else:
                    dck_ref[hh] += dck_h
            dq_part = dq_part * scale

            @pl.when(j == 0)
            def _():
                dq_ref[rows, :] = dq_part

            @pl.when(j > 0)
            def _():
                dq_ref[rows, :] += dq_part

            @pl.when(jnp.logical_and(hp == 0, j == 0))
            def _():
                dcq_ref[rows, :] = dcq_part

            @pl.when(jnp.logical_or(hp > 0, j > 0))
            def _():
                dcq_ref[rows, :] += dcq_part

            if diagonal:
                dk_ref[...] = dk_part
                dv_ref[...] = dv_part
            else:
                dk_ref[...] += dk_part
                dv_ref[...] += dv_part

        block(j, True)

        def later(i, carry):
            block(i, False)
            return carry

        lax.fori_loop(j + 1, nq, later, 0)

    at_q = lambda col: pl.BlockSpec((T, LANES), lambda hp, j: (0, col + hp))
    at_k = lambda col: pl.BlockSpec((tk, LANES), lambda hp, j: (j, col + hp))
    lse_spec = pl.BlockSpec((None, T, LANES), lambda hp, j: (hp, 0, 0))
    cum_spec = pl.BlockSpec((T, LANES), lambda hp, j: (0, 0))
    ck_spec = pl.BlockSpec((PAIR, None, 1, tk), lambda hp, j: (hp, j, 0, 0))
    return _pcall(
        body, name="attn_bwd", grid=(H // PAIR, nkv),
        in_specs=[at_q(qb), at_k(kb), at_k(vb), at_q(0), at_q(dob), lse_spec, cum_spec, ck_spec],
        out_specs=[pl.BlockSpec((T, LANES), lambda hp, j: (0, hp)), at_k(0), at_k(0),
                   pl.BlockSpec((T, LANES), lambda hp, j: (0, 0)), ck_spec],
        out_shape=[jax.ShapeDtypeStruct((T, hd), F32)] * 3
        + [jax.ShapeDtypeStruct((T, LANES), F32), jax.ShapeDtypeStruct((H, nkv, 1, tk), F32)],
        comm=comm,
    )(proj, proj, proj, o, dycat, lse, cum, ck4)


ELEMENTWISE_BLOCK_BYTES = 2 * 1024 * 1024
BF16_ROWS = 16


def cast_bf16(arrays, comm=None):
    def slab(a, steps):
        R, C = a.shape
        if R % (steps * BF16_ROWS) == 0:
            return pl.BlockSpec((R // steps, C), lambda i: (i, 0))
        if C % (steps * LANES) == 0:
            return pl.BlockSpec((R, C // steps), lambda i: (0, i))
        return None

    steps = 8 if all(slab(a, 8) is not None for a in arrays) else 4
    specs = [slab(a, steps) for a in arrays]
    n = len(arrays)

    def body(*refs):
        for src, dst in zip(refs[:n], refs[n:]):
            dst[...] = src[...].astype(BF16)

    return _pcall(body, name="cast_bf16", grid=(steps,), in_specs=specs, out_specs=specs,
                  out_shape=[jax.ShapeDtypeStruct(a.shape, BF16) for a in arrays], comm=comm)(*arrays)


def _ew_tiles(rows, cols, bytes_per_element):
    target = max(8, ELEMENTWISE_BLOCK_BYTES // max(1, cols * bytes_per_element))
    if rows <= target:
        return rows, cols
    t = (target // 16) * 16
    while t >= 16:
        if rows % t == 0:
            return t, cols
        t -= 16
    tc = _tile(cols, max(LANES, (ELEMENTWISE_BLOCK_BYTES // (rows * bytes_per_element)) // LANES * LANES))
    return rows, tc


def sum_chips(recv):
    nc, R, C = recv.shape
    tr, tc = _ew_tiles(R, C, 4)

    def body(r_ref, o_ref):
        acc = r_ref[0].astype(F32)
        for j in range(1, nc):
            acc = acc + r_ref[j].astype(F32)
        o_ref[...] = acc

    return _pcall(
        body, name="sum_chips", grid=(R // tr, C // tc),
        in_specs=[pl.BlockSpec((nc, tr, tc), lambda i, j: (0, i, j))],
        out_specs=[pl.BlockSpec((tr, tc), lambda i, j: (i, j))],
        out_shape=[jax.ShapeDtypeStruct((R, C), F32)],
    )(recv)[0]


def add_sibling_half(g, recv):
    nc, R, C = g.shape
    hr = R // 2
    tr, tc = _ew_tiles(hr, C, 4 * nc)

    def body(g_ref, r_ref, o_ref):
        c = lax.axis_index("c")
        for j in range(nc):
            o_ref[j] = (g_ref[j, c].astype(F32) + r_ref[j].astype(F32)).astype(BF16)

    return _pcall(
        body, name="add_sibling_half", grid=(hr // tr, C // tc),
        in_specs=[pl.BlockSpec((nc, 2, tr, tc), lambda i, j: (0, 0, i, j)),
                  pl.BlockSpec((nc, tr, tc), lambda i, j: (0, i, j))],
        out_specs=[pl.BlockSpec((nc, tr, tc), lambda i, j: (0, i, j))],
        out_shape=[jax.ShapeDtypeStruct((nc, hr, C), BF16)],
    )(g.reshape(nc, 2, hr, C), recv)[0]


def adamw(w, m, v, g_parts, comm=None, halves=False):
    R, C = w.shape
    tr, tc = _ew_tiles(R // 2 if halves else R, C, 4 * 4)
    n_g = len(g_parts)
    n_half = (R // 2) // tr
    c1 = 1.0 - ADAM_B1
    c2 = 1.0 - ADAM_B2
    bc1 = 1.0 - ADAM_B1 ** ADAM_STEP
    bc2 = 1.0 - ADAM_B2 ** ADAM_STEP

    def body(*refs):
        w_ref, m_ref, v_ref = refs[:3]
        g_refs = refs[3:3 + n_g]
        g_out, d_out, m_out, v_out = refs[3 + n_g:]
        if halves:
            mine = (pl.program_id(0) >= n_half) == (lax.axis_index("c") == 1)
            g = jnp.where(mine, g_refs[0][...], g_refs[1][...])
        else:
            g = g_refs[0][...]
            for r in g_refs[1:]:
                g = g + r[...]
        m_new = ADAM_B1 * m_ref[...] + c1 * g
        v_new = ADAM_B2 * v_ref[...] + c2 * (g * g)
        m_hat = m_new / bc1
        v_hat = v_new / bc2
        g_out[...] = g
        d_out[...] = -ADAM_LR * (m_hat / (jnp.sqrt(v_hat) + ADAM_EPS) + ADAM_WD * w_ref[...])
        m_out[...] = m_new
        v_out[...] = v_new

    spec = pl.BlockSpec((tr, tc), lambda i, j: (i, j))
    g_spec = pl.BlockSpec((tr, tc), lambda i, j: (i % n_half, j)) if halves else spec
    return _pcall(
        body, name="adamw", grid=(R // tr, C // tc),
        in_specs=[spec] * 3 + [g_spec] * n_g, out_specs=[spec] * 4,
        out_shape=[jax.ShapeDtypeStruct((R, C), F32)] * 4, comm=comm,
    )(w, m, v, *g_parts)


def _chip_coords():
    x, y, c = lax.axis_index("x"), lax.axis_index("y"), lax.axis_index("c")
    others = [(1 - x, y), (x, 1 - y), (1 - x, 1 - y)]
    return x, y, c, others


def _remote(src, dst, send_sem, recv_sem, device):
    return pltpu.make_async_remote_copy(src_ref=src, dst_ref=dst, send_sem=send_sem, recv_sem=recv_sem,
                                        device_id=device, device_id_type=MESH)


def gather_comm(shards):
    n = len(shards)
    SLOTS = 7

    def makers(ins, outs, sems):
        send_sems, recv_sems, local_sems = sems
        x, y, c, _ = _chip_coords()
        me, xn, yn, dg = 2 * x + y, 2 * (1 - x) + y, 2 * x + (1 - y), 2 * (1 - x) + (1 - y)
        to_x, to_y, sibling = (1 - x, y, c), (x, 1 - y, c), (x, y, 1 - c)

        def part(ref, a, half, quarter=None, chip=None):
            rows, cols = ins[a].shape[0], ins[a].shape[1]
            lead = () if chip is None else (chip,)
            along_rows = rows % (4 * BF16_ROWS) == 0 or (ins[a].dtype == F32 and rows % (4 * SUBLANES) == 0)
            size = (rows if along_rows else cols) // 2
            start = half * size
            if quarter is not None:
                size = size // 2
                start = start + quarter * size
            if along_rows:
                return ref.at[(*lead, pl.ds(start, size))]
            return ref.at[(*lead, slice(None), pl.ds(start, size))]

        def copy(a, k, src, dst, device):
            return _remote(src, dst, send_sems.at[SLOTS * a + k], recv_sems.at[SLOTS * a + k], device)

        def local(a):
            return pltpu.make_async_copy(ins[a], outs[a].at[me], local_sems.at[a])

        def first_leg(a):
            mine = part(outs[a], a, c, chip=me)
            return [copy(a, 0, part(ins[a], a, c), mine, to_x), copy(a, 1, part(ins[a], a, c), mine, to_y)]

        def arrived(a, k):
            region = {0: part(outs[a], a, c, chip=xn), 1: part(outs[a], a, c, chip=yn),
                      2: part(outs[a], a, c, 0, chip=dg), 3: part(outs[a], a, c, 1, chip=dg),
                      4: part(outs[a], a, 1 - c, chip=xn), 5: part(outs[a], a, 1 - c, chip=yn),
                      6: part(outs[a], a, 1 - c, chip=dg)}[k]
            return copy(a, k, region, region, sibling if k >= 4 else (to_x if k in (0, 3) else to_y))

        def relays(a):
            qx, qy = part(outs[a], a, c, 0, chip=xn), part(outs[a], a, c, 1, chip=yn)
            return [copy(a, 2, qx, qx, to_y), copy(a, 3, qy, qy, to_x)]

        def handover(a, k):
            region = part(outs[a], a, c, chip={4: xn, 5: yn, 6: dg}[k])
            return copy(a, k, region, region, sibling)

        return local, first_leg, arrived, relays, handover

    def start(ins, outs, sems):
        local, first_leg, _, _, _ = makers(ins, outs, sems)
        for a in range(n):
            for cp in first_leg(a):
                cp.start()
        for a in range(n):
            local(a).start()

    def relay(ins, outs, sems):
        _, _, arrived, relays, handover = makers(ins, outs, sems)
        for a in range(n):
            to_y_nbr, to_x_nbr = relays(a)
            arrived(a, 0).wait_recv()
            to_y_nbr.start()
            handover(a, 4).start()
            arrived(a, 1).wait_recv()
            to_x_nbr.start()
            handover(a, 5).start()

    def finish(ins, outs, sems):
        local, first_leg, arrived, relays, handover = makers(ins, outs, sems)
        for a in range(n):
            arrived(a, 2).wait_recv()
            arrived(a, 3).wait_recv()
            handover(a, 6).start()
        for a in range(n):
            for k in (4, 5, 6):
                arrived(a, k).wait_recv()
        for a in range(n):
            for cp in first_leg(a) + relays(a) + [handover(a, k) for k in (4, 5, 6)]:
                cp.wait_send()
            local(a).wait()

    return Comm(shards, [jax.ShapeDtypeStruct((N_CHIP,) + s.shape, s.dtype) for s in shards],
                [pltpu.SemaphoreType.DMA((SLOTS * n,)), pltpu.SemaphoreType.DMA((SLOTS * n,)),
                 pltpu.SemaphoreType.DMA((n,))], start, finish, relay)


def scatter_comm(grads):
    n = len(grads)
    pieces = [(a, jj) for a in range(n) for jj in range(3)]

    def makers(ins, outs, sems):
        send_sems, recv_sems, local_sems = sems
        x, y, c, others = _chip_coords()
        me = 2 * x + y

        def local(a):
            return pltpu.make_async_copy(ins[a].at[me], outs[a].at[me], local_sems.at[a])

        def ici(a, jj):
            ox, oy = others[jj]
            return _remote(ins[a].at[2 * ox + oy], outs[a].at[me], send_sems.at[3 * a + jj],
                           recv_sems.at[3 * a + jj], (ox, oy, c))

        def landed(a, jj):
            ox, oy = others[jj]
            slot = outs[a].at[2 * ox + oy]
            return _remote(slot, slot, send_sems.at[3 * a + jj], recv_sems.at[3 * a + jj], (ox, oy, c))

        return local, ici, landed

    def start(ins, outs, sems):
        local, ici, _ = makers(ins, outs, sems)
        for a in range(n):
            for jj in (2, 0, 1):
                ici(a, jj).start()
        for a in range(n):
            local(a).start()

    def finish(ins, outs, sems):
        local, ici, landed = makers(ins, outs, sems)
        for a, jj in pieces:
            landed(a, jj).wait_recv()
        for a, jj in pieces:
            ici(a, jj).wait_send()
        for a in range(n):
            local(a).wait()

    return Comm(grads, [jax.ShapeDtypeStruct(g.shape, g.dtype) for g in grads],
                [pltpu.SemaphoreType.DMA((3 * n,)), pltpu.SemaphoreType.DMA((3 * n,)),
                 pltpu.SemaphoreType.DMA((n,))], start, finish)


def halfswap_comm(grads):
    n = len(grads)

    def copies(ins, outs, sems):
        send_sems, recv_sems = sems
        x, y, c, _ = _chip_coords()
        out = []
        for a in range(n):
            hr = ins[a].shape[1] // 2
            out.append(_remote(ins[a].at[:, pl.ds((1 - c) * hr, hr)], outs[a], send_sems.at[a], recv_sems.at[a],
                               (x, y, 1 - c)))
        return out

    def start(ins, outs, sems):
        for cp in copies(ins, outs, sems):
            cp.start()

    def finish(ins, outs, sems):
        for cp in copies(ins, outs, sems):
            cp.wait()

    return Comm(grads, [jax.ShapeDtypeStruct((g.shape[0], g.shape[1] // 2, g.shape[2]), g.dtype) for g in grads],
                [pltpu.SemaphoreType.DMA((n,)), pltpu.SemaphoreType.DMA((n,))], start, finish)


def join_comms(first, second):
    ni, no, ns = len(first.operands), len(first.out_shape), len(first.sems)

    def start(ins, outs, sems):
        first.start(ins[:ni], outs[:no], sems[:ns])
        second.start(ins[ni:], outs[no:], sems[ns:])

    def finish(ins, outs, sems):
        first.finish(ins[:ni], outs[:no], sems[:ns])
        second.finish(ins[ni:], outs[no:], sems[ns:])

    def relay(ins, outs, sems):
        if first.relay is not None:
            first.relay(ins[:ni], outs[:no], sems[:ns])
        if second.relay is not None:
            second.relay(ins[ni:], outs[no:], sems[ns:])

    return Comm(first.operands + second.operands, first.out_shape + second.out_shape, first.sems + second.sems,
                start, finish, relay if (first.relay or second.relay) else None)


def swap_comm(parts):
    n = len(parts)

    def copies(ins, outs, sems):
        send_sems, recv_sems = sems
        x, y, c, _ = _chip_coords()
        return [_remote(ins[a], outs[a], send_sems.at[a], recv_sems.at[a], (x, y, 1 - c)) for a in range(n)]

    def start(ins, outs, sems):
        for cp in copies(ins, outs, sems):
            cp.start()

    def finish(ins, outs, sems):
        for cp in copies(ins, outs, sems):
            cp.wait()

    return Comm(parts, [jax.ShapeDtypeStruct(p.shape, p.dtype) for p in parts],
                [pltpu.SemaphoreType.DMA((n,)), pltpu.SemaphoreType.DMA((n,))], start, finish)


def allreduce_small(v):
    R = v.shape[0]

    def body(v_ref, sum_ref, all_ref, send_sems, recv_sems):
        x, y, c = lax.axis_index("x"), lax.axis_index("y"), lax.axis_index("c")
        me = 4 * x + 2 * y + c
        all_ref[me] = v_ref[...]
        copies = []
        for k in range(1, N_DEV):
            px = 1 - x if k & 4 else x
            py = 1 - y if k & 2 else y
            pc = 1 - c if k & 1 else c
            cp = pltpu.make_async_remote_copy(
                src_ref=v_ref, dst_ref=all_ref.at[me], send_sem=send_sems.at[k - 1], recv_sem=recv_sems.at[k - 1],
                device_id=(px, py, pc), device_id_type=MESH)
            cp.start()
            copies.append((cp, 4 * px + 2 * py + pc))
        for k, (cp, peer) in enumerate(copies):
            pltpu.make_async_remote_copy(
                src_ref=v_ref, dst_ref=all_ref.at[peer], send_sem=send_sems.at[k], recv_sem=recv_sems.at[k],
                device_id=(x, y, c), device_id_type=MESH).wait_recv()
        for cp, _ in copies:
            cp.wait_send()
        acc = all_ref[0]
        for d in range(1, N_DEV):
            acc = acc + all_ref[d]
        sum_ref[...] = acc

    vm = pl.BlockSpec(memory_space=pltpu.VMEM)
    return pl.pallas_call(
        body, name="allreduce_small",
        in_specs=[vm], out_specs=[vm, vm],
        out_shape=[jax.ShapeDtypeStruct((R, LANES), F32), jax.ShapeDtypeStruct((N_DEV, R, LANES), F32)],
        scratch_shapes=[pltpu.SemaphoreType.DMA((N_DEV - 1,)), pltpu.SemaphoreType.DMA((N_DEV - 1,))],
    )(v)[0]


SMALL_NAMES = ("ffn1_norm", "mix_norm", "ffn2_norm", "final_norm", "conv_b", "conv_ln_g", "conv_ln_b")


def _pack_small(vecs, bias, conv_w_rows, loss_tile):
    rows = [vecs[n].reshape(-1, LANES) for n in SMALL_NAMES]
    rows.append(bias.reshape(1, LANES))
    rows.append(conv_w_rows.reshape(-1, LANES))
    rows.append(loss_tile[0:1, :])
    packed = jnp.concatenate(rows, axis=0)
    pad = (-packed.shape[0]) % 8
    return jnp.pad(packed, ((0, pad), (0, 0)))


def _unpack_small(packed, sizes, n_conv_rows):
    out, r = {}, 0
    for n in SMALL_NAMES:
        k = sizes[n] // LANES
        out[n] = packed[r:r + k].reshape(-1)
        r += k
    out["fgate_bias"] = packed[r]
    r += 1
    out["conv_w"] = packed[r:r + n_conv_rows]
    r += n_conv_rows
    out["loss"] = packed[r, 0]
    return out


def kernel(x, ffn1_norm, ffn1_w_gate, ffn1_w_up, ffn1_w_down, mix_norm, w_in, fgate_bias, conv_w, conv_b, conv_ln_g, conv_ln_b, w_out, ffn2_norm, ffn2_w_gate, ffn2_w_up, ffn2_w_down, final_norm, loss_target, m_ffn1_norm, m_ffn1_w_gate, m_ffn1_w_up, m_ffn1_w_down, m_mix_norm, m_w_in, m_fgate_bias, m_conv_w, m_conv_b, m_conv_ln_g, m_conv_ln_b, m_w_out, m_ffn2_norm, m_ffn2_w_gate, m_ffn2_w_up, m_ffn2_w_down, m_final_norm, v_ffn1_norm, v_ffn1_w_gate, v_ffn1_w_up, v_ffn1_w_down, v_mix_norm, v_w_in, v_fgate_bias, v_conv_w, v_conv_b, v_conv_ln_g, v_conv_ln_b, v_w_out, v_ffn2_norm, v_ffn2_w_gate, v_ffn2_w_up, v_ffn2_w_down, v_final_norm):
    w = dict(ffn1_norm=ffn1_norm, ffn1_w_gate=ffn1_w_gate, ffn1_w_up=ffn1_w_up, ffn1_w_down=ffn1_w_down,
             mix_norm=mix_norm, w_in=w_in, fgate_bias=fgate_bias, conv_w=conv_w, conv_b=conv_b,
             conv_ln_g=conv_ln_g, conv_ln_b=conv_ln_b, w_out=w_out, ffn2_norm=ffn2_norm,
             ffn2_w_gate=ffn2_w_gate, ffn2_w_up=ffn2_w_up, ffn2_w_down=ffn2_w_down, final_norm=final_norm)
    m = dict(ffn1_norm=m_ffn1_norm, ffn1_w_gate=m_ffn1_w_gate, ffn1_w_up=m_ffn1_w_up, ffn1_w_down=m_ffn1_w_down,
             mix_norm=m_mix_norm, w_in=m_w_in, fgate_bias=m_fgate_bias, conv_w=m_conv_w, conv_b=m_conv_b,
             conv_ln_g=m_conv_ln_g, conv_ln_b=m_conv_ln_b, w_out=m_w_out, ffn2_norm=m_ffn2_norm,
             ffn2_w_gate=m_ffn2_w_gate, ffn2_w_up=m_ffn2_w_up, ffn2_w_down=m_ffn2_w_down, final_norm=m_final_norm)
    v = dict(ffn1_norm=v_ffn1_norm, ffn1_w_gate=v_ffn1_w_gate, ffn1_w_up=v_ffn1_w_up, ffn1_w_down=v_ffn1_w_down,
             mix_norm=v_mix_norm, w_in=v_w_in, fgate_bias=v_fgate_bias, conv_w=v_conv_w, conv_b=v_conv_b,
             conv_ln_g=v_conv_ln_g, conv_ln_b=v_conv_ln_b, w_out=v_w_out, ffn2_norm=v_ffn2_norm,
             ffn2_w_gate=v_ffn2_w_gate, ffn2_w_up=v_ffn2_w_up, ffn2_w_down=v_ffn2_w_down, final_norm=v_final_norm)
    names = list(w.keys())
    big = ("ffn1_w_gate", "ffn1_w_up", "ffn1_w_down", "w_in", "w_out", "ffn2_w_gate", "ffn2_w_up", "ffn2_w_down")

    T, D = x.shape[1], x.shape[2]
    C = conv_b.shape[0]
    H = fgate_bias.shape[0]
    cs = conv_w.shape[1]
    in_cols = N_CHIP * w_in.shape[1]
    p_main = in_cols - H

    x0, tgt = x[0], loss_target[0]
    tk = _tile(T, 512, 128)
    nkv = T // tk
    row = lambda a: a.reshape(1, -1)
    grad, delta, new_m, new_v = {}, {}, {}, {}

    def update(n, parts, comm=None, halves=False):
        args = (w[n], m[n], v[n])
        if n == "w_in":
            outs = [t.T for t in adamw(*[a.T for a in args], parts, comm=comm)]
        else:
            outs = adamw(*args, parts, comm=comm, halves=halves)
        grad[n], delta[n], new_m[n], new_v[n] = outs

    rest = [n for n in big if n != "ffn1_w_gate"]
    g0 = gather_comm([w["ffn1_w_gate"].astype(BF16), jnp.pad(conv_w, ((0, HALO - CONV_K), (0, 0)))])
    wb = dict(zip(rest, cast_bf16([w[n].T if n == "w_in" else w[n] for n in rest], comm=g0)))
    wg1, conv_w4 = g0.results
    conv_w_full = conv_w4.transpose(1, 0, 2).reshape(HALO, C)
    h1, r1 = rms_fwd(x0, row(ffn1_norm))
    g1a = gather_comm([wb["ffn1_w_up"]])
    a1 = ffn_gate(h1, wg1, comm=g1a)
    wu1 = g1a.results[0]
    g1b = gather_comm([wb["ffn1_w_down"]])
    b1, mid1 = ffn_upmul(h1, wu1, a1, comm=g1b)
    wd1 = g1b.results[0]
    g2 = gather_comm([wb["w_in"]])
    x1 = mm_residual("ffn_down_g", mid1, wd1, x0, 0.5, comm=g2)[0]
    w_t = g2.results[0].reshape(in_cols, D)

    wf_t = jnp.pad(w_t[p_main:], ((0, LANES - H), (0, 0)))
    bias_pad = jnp.pad(row(fgate_bias), ((0, 0), (0, LANES - H)))
    h2, r2 = rms_fwd(x1, row(mix_norm))
    g_out = gather_comm([wb["w_out"]])
    proj = proj_main(h2, w_t, p_main, comm=g_out)
    w_out3 = g_out.results[0]
    f, cum = fgate_fwd(h2, wf_t, bias_pad, H)
    ypre, yconv = conv_fwd(proj, conv_w_full, row(conv_b), row(conv_ln_g), row(conv_ln_b))
    ck4 = cum[:, :H].T.reshape(H, nkv, 1, tk)
    g3 = gather_comm([wb["ffn2_w_gate"], wb["ffn2_w_up"]])
    o, lse = attn_fwd(proj, cum, ck4, 2 * C, comm=g3)
    wg2, wu2 = g3.results
    ycat = jnp.concatenate([yconv, o.astype(BF16)], axis=1)
    x2 = mm_residual("out_proj", ycat, w_out3.reshape(2, -1, D), x1, 1.0)[0]

    h3, r3 = rms_fwd(x2, row(ffn2_norm))
    g4 = gather_comm([wb["ffn2_w_down"]])
    a2, b2, mid2 = ffn_up(h3, wg2, wu2, comm=g4)
    wd2 = g4.results[0]
    x3 = mm_residual("ffn_down", mid2, wd2, x2, 0.5)[0]
    dx3, dx3b, loss_tile, d_final = final_loss(x3, tgt, row(final_norm))

    da2, db2 = ffn_bwd_mid(dx3b, wd2, a2, b2)
    dwd2 = dw_rowshard("ffn_dwd", mid2, dx3b, N_CHIP)[0]
    s1 = scatter_comm([dwd2])
    dwg2, dwu2 = dw_colshard("ffn_dwgu_s", h3, [da2, db2], N_CHIP, comm=s1)
    s2 = scatter_comm([dwg2])
    dh3 = ffn_dh(da2, db2, wg2, wu2, comm=s2)
    dx2, dx2b, d_ffn2_norm = rms_bwd(dh3, x2, r3, row(ffn2_norm), dx3, 1.0)

    dycat = mm_nt_bf16("out_proj_dy", dx2b, w_out3.reshape(-1, D))
    dw_out3 = dw_rowshard("out_proj_dw", ycat, dx2b, N_CHIP)[0]
    s3 = scatter_comm([dwu2, dw_out3])
    dq, dk, dv, dcq, dck4 = attn_bwd(proj, o, dycat, lse, cum, ck4, 2 * C, C, comm=s3)
    dc = dcq + jnp.pad(dck4.reshape(H, T).T, ((0, 0), (0, LANES - H)))
    df, d_bias = fgate_bwd(dc, f, H)
    dag, d_conv_w, d_conv_b, d_ln_g, d_ln_b = conv_bwd(proj, ypre, dycat, conv_w_full, row(conv_ln_g),
                                                       row(conv_ln_b))
    dproj = jnp.concatenate([dag, dq.astype(BF16), dk.astype(BF16), dv.astype(BF16)], axis=1)
    early = ("ffn2_w_down", "ffn2_w_gate", "ffn2_w_up", "w_out")
    early_sums = [sum_chips(r) for r in (s1.results[0], s2.results[0], s3.results[0], s3.results[1])]
    sw1 = swap_comm(early_sums)
    dh2 = proj_dh(dproj, w_t, df, wf_t, comm=sw1)
    dw_t, dwf_t = proj_dw(dproj, df, h2, in_cols)
    dw_t = lax.dynamic_update_slice(dw_t, dwf_t[:H].astype(BF16), (p_main, 0))
    dw_in3 = dw_t.reshape(N_CHIP, in_cols // N_CHIP, D)
    dx1, dx1b, d_mix_norm = rms_bwd(dh2, x1, r2, row(mix_norm), dx2, 0.5)

    s4 = scatter_comm([dw_in3])
    da1, db1 = ffn_bwd_mid(dx1b, wd1, a1, b1, comm=s4)
    dwd1 = dw_rowshard("ffn_dwd", mid1, dx1b, N_CHIP)[0]
    s5 = scatter_comm([dwd1])
    dwg1, dwu1 = dw_colshard("ffn_dwgu_s", h1, [da1, db1], N_CHIP, comm=s5)
    mid_sums = [sum_chips(s4.results[0]), sum_chips(s5.results[0])]
    s6 = join_comms(join_comms(scatter_comm([dwg1]), halfswap_comm([dwu1])), swap_comm(mid_sums))
    dh1 = ffn_dh(da1, db1, wg1, wu1, comm=s6)
    recv_g1, sibling_u1, their_in, their_d1 = s6.results
    grad_x, _, d_ffn1_norm = rms_bwd(dh1, x0, r1, row(ffn1_norm), dx1, 1.0)

    s7 = scatter_comm([add_sibling_half(dwu1, sibling_u1)])
    for i, (n, mine, other) in enumerate(zip(early, early_sums, sw1.results)):
        update(n, [mine, other], comm=s7 if i == 0 else None)
    update("w_in", [mid_sums[0], their_in])
    update("ffn1_w_down", [mid_sums[1], their_d1])
    sum_g1, half_u1 = sum_chips(recv_g1), sum_chips(s7.results[0])
    their_g1, their_u1 = _run_comm("swap_last", swap_comm([sum_g1, half_u1]))
    update("ffn1_w_gate", [sum_g1, their_g1])
    update("ffn1_w_up", [half_u1, their_u1], halves=True)

    gl = dict(ffn1_norm=d_ffn1_norm, mix_norm=d_mix_norm, ffn2_norm=d_ffn2_norm, final_norm=d_final,
              conv_b=d_conv_b, conv_ln_g=d_ln_g, conv_ln_b=d_ln_b)
    small_sizes = {n: w[n].shape[0] for n in SMALL_NAMES}
    packed = _pack_small(gl, d_bias, d_conv_w, loss_tile)
    red = _unpack_small(allreduce_small(packed), small_sizes, HALO * C // LANES)
    loss = red["loss"]
    my_chip = 2 * lax.axis_index("x") + lax.axis_index("y")
    g_conv_w = lax.dynamic_slice_in_dim(red["conv_w"].reshape(HALO, C)[:CONV_K], my_chip * cs, cs, axis=1)
    update("conv_w", [g_conv_w])
    vec_names = SMALL_NAMES + ("fgate_bias",)
    stack = lambda d: jnp.concatenate(
        [jnp.pad(d[n], (0, (-d[n].shape[0]) % LANES)).reshape(-1, LANES) for n in vec_names], axis=0)
    g_stack = jnp.concatenate([red[n].reshape(-1, LANES) for n in SMALL_NAMES] + [red["fgate_bias"][None, :]],
                              axis=0)
    outs = adamw(stack(w), stack(m), stack(v), [g_stack])
    r = 0
    for n in vec_names:
        size = w[n].shape[0]
        k = -(-size // LANES)
        for dst, src in zip((grad, delta, new_m, new_v), outs):
            dst[n] = src[r:r + k].reshape(-1)[:size]
        r += k

    return (loss, grad_x[None], *[grad[n] for n in names], *[delta[n] for n in names],
            *[new_m[n] for n in names], *[new_v[n] for n in names])
```

```python
import functools
import math

import jax
import jax.numpy as jnp
from jax import lax
from jax.experimental import pallas as pl
from jax.experimental.pallas import tpu as pltpu

F32 = jnp.float32
BF16 = jnp.bfloat16
NORM_EPS = 1e-6
LN_EPS = 1e-5
NEG_INF = -1e30
HEAD_DIM = 64
CONV_K = 31
HALO = 32
LANES = 128
N_CHIP = 4
N_DEV = 8
VMEM_LIMIT = 52 * 1024 * 1024
MESH = pl.DeviceIdType.MESH

ADAM_LR = 0.001
ADAM_B1 = 0.9
ADAM_B2 = 0.999
ADAM_EPS = 1e-08
ADAM_WD = 0.01
ADAM_STEP = 10

NN = (((1,), (0,)), ((), ()))
NT = (((1,), (1,)), ((), ()))
TN = (((0,), (0,)), ((), ()))


def _tile(n, pref, unit=128):
    if n <= pref:
        return n
    t = (pref // unit) * unit
    while t > 0:
        if n % t == 0:
            return t
        t -= unit
    raise ValueError(f"no tile for {n} under {pref}")


RELAY_AT = (3, 4)


class Comm:
    def __init__(self, operands, out_shape, sems, start, finish, relay=None):
        self.operands, self.out_shape, self.sems = list(operands), list(out_shape), list(sems)
        self.start, self.finish, self.relay = start, finish, relay
        self.results = None


def _pcall(body, *, name, grid, in_specs, out_specs, out_shape, scratch=(), comm=None):
    params = pltpu.CompilerParams(dimension_semantics=("arbitrary",) * len(grid), vmem_limit_bytes=VMEM_LIMIT)
    scratch = list(scratch)
    if comm is None:
        return pl.pallas_call(body, name=name, grid=grid, in_specs=in_specs, out_specs=out_specs,
                              out_shape=out_shape, scratch_shapes=scratch, compiler_params=params)
    n_in, n_out, n_s = len(in_specs), len(out_shape), len(scratch)
    n_ci, n_co = len(comm.operands), len(comm.out_shape)
    any_spec = pl.BlockSpec(memory_space=pl.ANY)

    def carried(*refs):
        ins, refs = refs[:n_in], refs[n_in:]
        c_ins, refs = refs[:n_ci], refs[n_ci:]
        outs, refs = refs[:n_out], refs[n_out:]
        c_outs, refs = refs[:n_co], refs[n_co:]
        scr, c_sems = refs[:n_s], refs[n_s:]
        step = pl.program_id(0)
        for d in range(1, len(grid)):
            step = step * grid[d] + pl.program_id(d)
        total = math.prod(grid)
        first, last = step == 0, step == total - 1

        @pl.when(first)
        def _():
            comm.start(c_ins, c_outs, c_sems)

        if comm.relay is not None:
            @pl.when(step == min(total - 1, (RELAY_AT[0] * total) // RELAY_AT[1]))
            def _():
                comm.relay(c_ins, c_outs, c_sems)

        body(*ins, *outs, *scr)

        @pl.when(last)
        def _():
            comm.finish(c_ins, c_outs, c_sems)

    call = pl.pallas_call(
        carried, name=name, grid=grid, in_specs=list(in_specs) + [any_spec] * n_ci,
        out_specs=list(out_specs) + [any_spec] * n_co, out_shape=list(out_shape) + comm.out_shape,
        scratch_shapes=scratch + comm.sems, compiler_params=params)

    def run(*operands):
        res = call(*operands, *comm.operands)
        comm.results = list(res[n_out:])
        return list(res[:n_out])

    return run


def _run_comm(name, comm):
    n_ci, n_co = len(comm.operands), len(comm.out_shape)
    any_spec = pl.BlockSpec(memory_space=pl.ANY)

    def body(*refs):
        c_ins, c_outs, c_sems = refs[:n_ci], refs[n_ci:n_ci + n_co], refs[n_ci + n_co:]
        comm.start(c_ins, c_outs, c_sems)
        if comm.relay is not None:
            comm.relay(c_ins, c_outs, c_sems)
        comm.finish(c_ins, c_outs, c_sems)

    return pl.pallas_call(body, name=name, in_specs=[any_spec] * n_ci, out_specs=[any_spec] * n_co,
                          out_shape=comm.out_shape, scratch_shapes=comm.sems)(*comm.operands)


def _sigmoid(x):
    return 1.0 / (1.0 + jnp.exp(-x))


def _mm(name, *, grid, pairs, once_pairs=(), extra=(), out_shape, out_specs, acc_shapes, nk, kaxis, epilogue,
        comm=None):
    all_pairs = list(pairs) + list(once_pairs)
    n_p, n_o = len(pairs), len(once_pairs)
    n_e, n_out, n_acc = len(extra), len(out_shape), len(acc_shapes)

    def body(*refs):
        ab = refs[: 2 * (n_p + n_o)]
        ex = refs[2 * (n_p + n_o): 2 * (n_p + n_o) + n_e]
        outs = refs[2 * (n_p + n_o) + n_e: 2 * (n_p + n_o) + n_e + n_out]
        accs = refs[2 * (n_p + n_o) + n_e + n_out:]

        def dots(idx_range):
            vals = [None] * n_acc
            for p in idx_range:
                d = lax.dot_general(ab[2 * p][...], ab[2 * p + 1][...], all_pairs[p][4],
                                    preferred_element_type=F32)
                ai = all_pairs[p][5]
                vals[ai] = d if vals[ai] is None else vals[ai] + d
            return vals

        if nk == 1:
            vals = dots(range(n_p + n_o))
            epilogue(vals, ex, outs)
            return

        k = pl.program_id(kaxis)

        @pl.when(k == 0)
        def _():
            vals = dots(range(n_p + n_o))
            for ai in range(n_acc):
                accs[ai][...] = vals[ai]

        @pl.when(k > 0)
        def _():
            vals = dots(range(n_p))
            for ai in range(n_acc):
                if vals[ai] is not None:
                    accs[ai][...] += vals[ai]

        @pl.when(k == nk - 1)
        def _():
            epilogue([a[...] for a in accs], ex, outs)

    operands, in_specs = [], []
    for p in all_pairs:
        operands += [p[0], p[2]]
        in_specs += [p[1], p[3]]
    for arr, spec in extra:
        operands.append(arr)
        in_specs.append(spec)
    scratch = [pltpu.VMEM(s, F32) for s in acc_shapes] if nk > 1 else []
    return _pcall(body, name=name, grid=grid, in_specs=in_specs, out_specs=out_specs, out_shape=out_shape,
                  scratch=scratch, comm=comm)(*operands)


def rms_fwd(x, g):
    T, D = x.shape
    tt = _tile(T, 512, 8)

    def body(x_ref, g_ref, h_ref, r_ref):
        xv = x_ref[...]
        r = lax.rsqrt(jnp.mean(xv * xv, axis=-1, keepdims=True) + NORM_EPS)
        h_ref[...] = (xv * r * g_ref[...]).astype(BF16)
        r_ref[...] = r

    return _pcall(
        body, name="rms_fwd", grid=(T // tt,),
        in_specs=[pl.BlockSpec((tt, D), lambda i: (i, 0)), pl.BlockSpec((1, D), lambda i: (0, 0))],
        out_specs=[pl.BlockSpec((tt, D), lambda i: (i, 0)), pl.BlockSpec((tt, 1), lambda i: (i, 0))],
        out_shape=[jax.ShapeDtypeStruct((T, D), BF16), jax.ShapeDtypeStruct((T, 1), F32)],
    )(x, g)


def rms_bwd(dh, x, r, g, dres, out_scale):
    T, D = x.shape
    tt = _tile(T, 256, 8)

    def body(dh_ref, x_ref, r_ref, g_ref, dres_ref, dx_ref, dxb_ref, dg_ref):
        i = pl.program_id(0)
        xh = x_ref[...] * r_ref[...]
        dhv = dh_ref[...]
        dxh = dhv * g_ref[...]
        dx = dres_ref[...] + r_ref[...] * (dxh - xh * jnp.mean(dxh * xh, axis=-1, keepdims=True))
        dx_ref[...] = dx
        dxb_ref[...] = (out_scale * dx).astype(BF16)
        part = jnp.sum(dhv * xh, axis=0, keepdims=True)

        @pl.when(i == 0)
        def _():
            dg_ref[...] = part

        @pl.when(i > 0)
        def _():
            dg_ref[...] += part

    row = pl.BlockSpec((tt, D), lambda i: (i, 0))
    return _pcall(
        body, name="rms_bwd", grid=(T // tt,),
        in_specs=[row, row, pl.BlockSpec((tt, 1), lambda i: (i, 0)), pl.BlockSpec((1, D), lambda i: (0, 0)), row],
        out_specs=[row, row, pl.BlockSpec((1, D), lambda i: (0, 0))],
        out_shape=[jax.ShapeDtypeStruct((T, D), F32), jax.ShapeDtypeStruct((T, D), BF16),
                   jax.ShapeDtypeStruct((1, D), F32)],
    )(dh, x, r, g, dres)


def final_loss(x, tgt, g):
    T, D = x.shape
    tt = _tile(T, 256, 8)

    def body(x_ref, t_ref, g_ref, dx_ref, dxb_ref, loss_ref, dg_ref):
        i = pl.program_id(0)
        xv = x_ref[...]
        r = lax.rsqrt(jnp.mean(xv * xv, axis=-1, keepdims=True) + NORM_EPS)
        xh = xv * r
        err = xh * g_ref[...] - t_ref[...]
        part_loss = 0.5 * jnp.sum(jnp.mean(err * err, axis=-1, keepdims=True), axis=0, keepdims=True)
        dy = err * (1.0 / D)
        dxh = dy * g_ref[...]
        dx = r * (dxh - xh * jnp.mean(dxh * xh, axis=-1, keepdims=True))
        dx_ref[...] = dx
        dxb_ref[...] = (0.5 * dx).astype(BF16)
        part_g = jnp.sum(dy * xh, axis=0, keepdims=True)
        part_l = jnp.broadcast_to(part_loss, (8, LANES))

        @pl.when(i == 0)
        def _():
            dg_ref[...] = part_g
            loss_ref[...] = part_l

        @pl.when(i > 0)
        def _():
            dg_ref[...] += part_g
            loss_ref[...] += part_l

    row = pl.BlockSpec((tt, D), lambda i: (i, 0))
    return _pcall(
        body, name="final_loss", grid=(T // tt,),
        in_specs=[row, row, pl.BlockSpec((1, D), lambda i: (0, 0))],
        out_specs=[row, row, pl.BlockSpec((8, LANES), lambda i: (0, 0)), pl.BlockSpec((1, D), lambda i: (0, 0))],
        out_shape=[jax.ShapeDtypeStruct((T, D), F32), jax.ShapeDtypeStruct((T, D), BF16),
                   jax.ShapeDtypeStruct((8, LANES), F32), jax.ShapeDtypeStruct((1, D), F32)],
    )(x, tgt, g)


def ffn_gate(h, wg3, comm=None):
    T, D = h.shape
    nc, _, fs = wg3.shape
    tm = _tile(T, 512, 8)

    def epilogue(vals, ex, outs):
        outs[0][...] = vals[0].astype(BF16)

    return _mm("ffn_gate", grid=(nc, T // tm),
               pairs=[(h, pl.BlockSpec((tm, D), lambda j, i: (i, 0)),
                       wg3, pl.BlockSpec((None, D, fs), lambda j, i: (j, 0, 0)), NN, 0)],
               out_shape=[jax.ShapeDtypeStruct((T, nc * fs), BF16)],
               out_specs=[pl.BlockSpec((tm, fs), lambda j, i: (i, j))],
               acc_shapes=[(tm, fs)], nk=1, kaxis=None, epilogue=epilogue, comm=comm)[0]


def ffn_upmul(h, wu3, a, comm=None):
    T, D = h.shape
    nc, _, fs = wu3.shape
    tm = _tile(T, 512, 8)

    def epilogue(vals, ex, outs):
        b = vals[0]
        av = ex[0][...].astype(F32)
        outs[0][...] = b.astype(BF16)
        outs[1][...] = (av * _sigmoid(av) * b).astype(BF16)

    t_spec = pl.BlockSpec((tm, fs), lambda j, i: (i, j))
    o_shape = jax.ShapeDtypeStruct((T, nc * fs), BF16)
    return _mm("ffn_upmul", grid=(nc, T // tm),
               pairs=[(h, pl.BlockSpec((tm, D), lambda j, i: (i, 0)),
                       wu3, pl.BlockSpec((None, D, fs), lambda j, i: (j, 0, 0)), NN, 0)],
               extra=[(a, t_spec)], out_shape=[o_shape] * 2, out_specs=[t_spec] * 2,
               acc_shapes=[(tm, fs)], nk=1, kaxis=None, epilogue=epilogue, comm=comm)


def ffn_up(h, wg3, wu3, comm=None):
    T, D = h.shape
    nc, _, fs = wg3.shape
    tm = _tile(T, 512, 8)

    def epilogue(vals, ex, outs):
        a, b = vals
        outs[0][...] = a.astype(BF16)
        outs[1][...] = b.astype(BF16)
        outs[2][...] = (a * _sigmoid(a) * b).astype(BF16)

    h_spec = pl.BlockSpec((tm, D), lambda j, i: (i, 0))
    w_spec = pl.BlockSpec((None, D, fs), lambda j, i: (j, 0, 0))
    o_spec = pl.BlockSpec((tm, fs), lambda j, i: (i, j))
    o_shape = jax.ShapeDtypeStruct((T, nc * fs), BF16)
    return _mm("ffn_up", grid=(nc, T // tm),
               pairs=[(h, h_spec, wg3, w_spec, NN, 0), (h, h_spec, wu3, w_spec, NN, 1)],
               out_shape=[o_shape] * 3, out_specs=[o_spec] * 3, acc_shapes=[(tm, fs)] * 2, nk=1, kaxis=None,
               epilogue=epilogue, comm=comm)


def mm_residual(name, a, b3, res, scale, comm=None):
    T = a.shape[0]
    nk, tk, N = b3.shape
    tm, tn = _tile(T, 512, 8), _tile(N, 2048)

    def epilogue(vals, ex, outs):
        outs[0][...] = ex[0][...] + scale * vals[0]

    return _mm(name, grid=(T // tm, N // tn, nk),
               pairs=[(a, pl.BlockSpec((tm, tk), lambda i, n, k: (i, k)),
                       b3, pl.BlockSpec((None, tk, tn), lambda i, n, k: (k, 0, n)), NN, 0)],
               extra=[(res, pl.BlockSpec((tm, tn), lambda i, n, k: (i, n)))],
               out_shape=[jax.ShapeDtypeStruct((T, N), F32)],
               out_specs=[pl.BlockSpec((tm, tn), lambda i, n, k: (i, n))],
               acc_shapes=[(tm, tn)], nk=nk, kaxis=2, epilogue=epilogue, comm=comm)


def ffn_bwd_mid(dout, wd3, a, b, comm=None):
    T, D = dout.shape
    nc, fs, _ = wd3.shape
    tm = _tile(T, 512, 8)

    def epilogue(vals, ex, outs):
        dm = vals[0]
        av = ex[0][...].astype(F32)
        bv = ex[1][...].astype(F32)
        s = _sigmoid(av)
        outs[0][...] = (dm * bv * (s * (1.0 + av * (1.0 - s)))).astype(BF16)
        outs[1][...] = (dm * (av * s)).astype(BF16)

    t_spec = pl.BlockSpec((tm, fs), lambda j, i: (i, j))
    o_shape = jax.ShapeDtypeStruct((T, nc * fs), BF16)
    return _mm("ffn_bwd_mid", grid=(nc, T // tm),
               pairs=[(dout, pl.BlockSpec((tm, D), lambda j, i: (i, 0)),
                       wd3, pl.BlockSpec((None, fs, D), lambda j, i: (j, 0, 0)), NT, 0)],
               extra=[(a, t_spec), (b, t_spec)],
               out_shape=[o_shape] * 2, out_specs=[t_spec] * 2, acc_shapes=[(tm, fs)], nk=1, kaxis=None,
               epilogue=epilogue, comm=comm)


def dw_rowshard(name, a, b, nc, comm=None):
    T, M = a.shape
    N = b.shape[1]
    ms = M // nc
    tn, tk = _tile(N, 2048 if ms <= 512 else 1024), _tile(T, 1024, 16)

    def epilogue(vals, ex, outs):
        outs[0][...] = vals[0].astype(BF16)

    return _mm(name, grid=(nc, N // tn, T // tk),
               pairs=[(a, pl.BlockSpec((tk, ms), lambda j, n, k: (k, j)),
                       b, pl.BlockSpec((tk, tn), lambda j, n, k: (k, n)), TN, 0)],
               out_shape=[jax.ShapeDtypeStruct((nc, ms, N), BF16)],
               out_specs=[pl.BlockSpec((None, ms, tn), lambda j, n, k: (j, 0, n))],
               acc_shapes=[(ms, tn)], nk=T // tk, kaxis=2, epilogue=epilogue, comm=comm)


def dw_colshard(name, a, bs, nc, comm=None):
    T, M = a.shape
    ns = bs[0].shape[1] // nc
    tm, tk = _tile(M, 512), _tile(T, 1024, 16)

    def epilogue(vals, ex, outs):
        for v, o in zip(vals, outs):
            o[...] = v.astype(BF16)

    a_spec = pl.BlockSpec((tk, tm), lambda j, m, k: (k, m))
    b_spec = pl.BlockSpec((tk, ns), lambda j, m, k: (k, j))
    return _mm(name, grid=(nc, M // tm, T // tk),
               pairs=[(a, a_spec, b, b_spec, TN, p) for p, b in enumerate(bs)],
               out_shape=[jax.ShapeDtypeStruct((nc, M, ns), BF16)] * len(bs),
               out_specs=[pl.BlockSpec((None, tm, ns), lambda j, m, k: (j, m, 0))] * len(bs),
               acc_shapes=[(tm, ns)] * len(bs), nk=T // tk, kaxis=2, epilogue=epilogue, comm=comm)


def ffn_dh(da, db, wg3, wu3, comm=None):
    T = da.shape[0]
    nc, D, fs = wg3.shape
    tm, tn = _tile(T, 512, 8), _tile(D, 1024)

    def epilogue(vals, ex, outs):
        outs[0][...] = vals[0]

    a_spec = pl.BlockSpec((tm, fs), lambda i, n, k: (i, k))
    w_spec = pl.BlockSpec((None, tn, fs), lambda i, n, k: (k, n, 0))
    return _mm("ffn_dh", grid=(T // tm, D // tn, nc),
               pairs=[(da, a_spec, wg3, w_spec, NT, 0), (db, a_spec, wu3, w_spec, NT, 0)],
               out_shape=[jax.ShapeDtypeStruct((T, D), F32)],
               out_specs=[pl.BlockSpec((tm, tn), lambda i, n, k: (i, n))],
               acc_shapes=[(tm, tn)], nk=nc, kaxis=2, epilogue=epilogue, comm=comm)[0]


def proj_main(h, w_t, P, comm=None):
    T, D = h.shape
    tm, tn = _tile(T, 512, 8), _tile(P, 1024)

    def epilogue(vals, ex, outs):
        outs[0][...] = vals[0].astype(BF16)

    return _mm("proj_main", grid=(P // tn, T // tm),
               pairs=[(h, pl.BlockSpec((tm, D), lambda j, i: (i, 0)),
                       w_t, pl.BlockSpec((tn, D), lambda j, i: (j, 0)), NT, 0)],
               out_shape=[jax.ShapeDtypeStruct((T, P), BF16)],
               out_specs=[pl.BlockSpec((tm, tn), lambda j, i: (i, j))],
               acc_shapes=[(tm, tn)], nk=1, kaxis=None, epilogue=epilogue, comm=comm)[0]


def mm_nt_bf16(name, a, w):
    T, K = a.shape
    M = w.shape[0]
    tm, tn = _tile(T, 512, 8), _tile(M, 1024)

    def epilogue(vals, ex, outs):
        outs[0][...] = vals[0].astype(BF16)

    return _mm(name, grid=(T // tm, M // tn),
               pairs=[(a, pl.BlockSpec((tm, K), lambda i, n: (i, 0)),
                       w, pl.BlockSpec((tn, K), lambda i, n: (n, 0)), NT, 0)],
               out_shape=[jax.ShapeDtypeStruct((T, M), BF16)],
               out_specs=[pl.BlockSpec((tm, tn), lambda i, n: (i, n))],
               acc_shapes=[(tm, tn)], nk=1, kaxis=None, epilogue=epilogue)[0]


def proj_dh(dproj, w_t, df, wf_t, comm=None):
    T, P = dproj.shape
    D = w_t.shape[1]
    tm, tn, tk = _tile(T, 512, 8), _tile(D, 2048), _tile(P, 1280)

    def epilogue(vals, ex, outs):
        outs[0][...] = vals[0]

    return _mm("proj_dh", grid=(T // tm, D // tn, P // tk),
               pairs=[(dproj, pl.BlockSpec((tm, tk), lambda i, n, k: (i, k)),
                       w_t, pl.BlockSpec((tk, tn), lambda i, n, k: (k, n)), NN, 0)],
               once_pairs=[(df, pl.BlockSpec((tm, LANES), lambda i, n, k: (i, 0)),
                            wf_t, pl.BlockSpec((LANES, tn), lambda i, n, k: (0, n)), NN, 0)],
               out_shape=[jax.ShapeDtypeStruct((T, D), F32)],
               out_specs=[pl.BlockSpec((tm, tn), lambda i, n, k: (i, n))],
               acc_shapes=[(tm, tn)], nk=P // tk, kaxis=2, epilogue=epilogue, comm=comm)[0]


def proj_dw(dproj, df, h, rows):
    T, P = dproj.shape
    D = h.shape[1]
    tm, tn, tk = _tile(P, 1280), D // 2, _tile(T, 1024, 16)

    def to_bf16(vals, ex, outs):
        outs[0][...] = vals[0].astype(BF16)

    def to_f32(vals, ex, outs):
        outs[0][...] = vals[0]

    main = _mm("proj_dw_main", grid=(P // tm, D // tn, T // tk),
               pairs=[(dproj, pl.BlockSpec((tk, tm), lambda m, n, k: (k, m)),
                       h, pl.BlockSpec((tk, tn), lambda m, n, k: (k, n)), TN, 0)],
               out_shape=[jax.ShapeDtypeStruct((2, rows, tn), BF16)],
               out_specs=[pl.BlockSpec((None, tm, tn), lambda m, n, k: (n, m, 0))],
               acc_shapes=[(tm, tn)], nk=T // tk, kaxis=2, epilogue=to_bf16)[0]
    gate = _mm("proj_dw_f", grid=(1, D // tn, T // tk),
               pairs=[(df, pl.BlockSpec((tk, LANES), lambda m, n, k: (k, 0)),
                       h, pl.BlockSpec((tk, tn), lambda m, n, k: (k, n)), TN, 0)],
               out_shape=[jax.ShapeDtypeStruct((LANES, D), F32)],
               out_specs=[pl.BlockSpec((LANES, tn), lambda m, n, k: (0, n))],
               acc_shapes=[(LANES, tn)], nk=T // tk, kaxis=2, epilogue=to_f32)[0]
    return main, gate


def fgate_fwd(h, wf_t, bias, n_heads):
    T, D = h.shape
    tt = _tile(T, 512, 8)

    def body(h_ref, w_ref, b_ref, f_ref, c_ref, carry):
        i = pl.program_id(0)

        @pl.when(i == 0)
        def _():
            carry[...] = jnp.zeros_like(carry)

        f = lax.dot_general(h_ref[...], w_ref[...], NT, preferred_element_type=F32) + b_ref[...]
        logf = jnp.minimum(f, 0.0) - jnp.log(1.0 + jnp.exp(-jnp.abs(f)))
        tri = (lax.broadcasted_iota(jnp.int32, (tt, tt), 0) >= lax.broadcasted_iota(jnp.int32, (tt, tt), 1))
        cs = jnp.dot(tri.astype(F32), logf, preferred_element_type=F32, precision=lax.Precision.HIGHEST)
        c = cs + carry[...]
        f_ref[...] = f
        c_ref[...] = c
        carry[...] = c[tt - 1:tt, :]

    row = pl.BlockSpec((tt, LANES), lambda i: (i, 0))
    return _pcall(
        body, name="fgate_fwd", grid=(T // tt,),
        in_specs=[pl.BlockSpec((tt, D), lambda i: (i, 0)), pl.BlockSpec((LANES, D), lambda i: (0, 0)),
                  pl.BlockSpec((1, LANES), lambda i: (0, 0))],
        out_specs=[row, row],
        out_shape=[jax.ShapeDtypeStruct((T, LANES), F32)] * 2,
        scratch=[pltpu.VMEM((1, LANES), F32)],
    )(h, wf_t, bias)


def fgate_bwd(dc, f, n_heads):
    T = dc.shape[0]
    tt = _tile(T, 512, 8)
    nt = T // tt

    def body(dc_ref, f_ref, df_ref, db_ref, carry):
        i = pl.program_id(0)

        @pl.when(i == 0)
        def _():
            carry[...] = jnp.zeros_like(carry)

        tri = (lax.broadcasted_iota(jnp.int32, (tt, tt), 1) >= lax.broadcasted_iota(jnp.int32, (tt, tt), 0))
        rs = jnp.dot(tri.astype(F32), dc_ref[...], preferred_element_type=F32,
                     precision=lax.Precision.HIGHEST) + carry[...]
        carry[...] = rs[0:1, :]
        lane = lax.broadcasted_iota(jnp.int32, (tt, LANES), 1)
        df = jnp.where(lane < n_heads, rs * _sigmoid(-f_ref[...]), 0.0)
        df_ref[...] = df.astype(BF16)
        part = jnp.sum(df, axis=0, keepdims=True)

        @pl.when(i == 0)
        def _():
            db_ref[...] = part

        @pl.when(i > 0)
        def _():
            db_ref[...] += part

    rev = pl.BlockSpec((tt, LANES), lambda i: (nt - 1 - i, 0))
    return _pcall(
        body, name="fgate_bwd", grid=(nt,),
        in_specs=[rev, rev],
        out_specs=[rev, pl.BlockSpec((1, LANES), lambda i: (0, 0))],
        out_shape=[jax.ShapeDtypeStruct((T, LANES), BF16), jax.ShapeDtypeStruct((1, LANES), F32)],
        scratch=[pltpu.VMEM((1, LANES), F32)],
    )(dc, f)


SUBLANES = 8
SHIFT_ROWS = HALO - SUBLANES


def _shifted_copies(buf, sh, tt):
    for r in range(1, SUBLANES):
        sh[r - 1, 0:tt + SHIFT_ROWS, :] = buf[pl.ds(r, tt + SHIFT_ROWS), :]


def _tap(buf, sh, offset, tt):
    q, r = divmod(offset, SUBLANES)
    if r == 0:
        return buf[pl.ds(SUBLANES * q, tt), :]
    return sh[r - 1, pl.ds(SUBLANES * q, tt), :]


def conv_fwd(proj, conv_w, conv_b, ln_g, ln_b):
    T = proj.shape[0]
    C = conv_w.shape[1]
    tt = _tile(T, 256, HALO)
    hb = tt // HALO

    def body(a_ref, g_ref, ah_ref, gh_ref, w_ref, cb_ref, lg_ref, lb_ref, ypre_ref, y_ref, ubuf, ush):
        i = pl.program_id(0)
        u = a_ref[...].astype(F32) * _sigmoid(g_ref[...].astype(F32))
        uh = ah_ref[...].astype(F32) * _sigmoid(gh_ref[...].astype(F32))
        ubuf[0:HALO, :] = jnp.where(i == 0, 0.0, uh)
        ubuf[HALO:HALO + tt, :] = u
        _shifted_copies(ubuf, ush, tt)
        acc = jnp.broadcast_to(cb_ref[...], (tt, C))
        for k in range(CONV_K):
            acc = acc + w_ref[k:k + 1, :] * _tap(ubuf, ush, HALO - (CONV_K - 1) + k, tt)
        ypre_ref[...] = acc
        mu = jnp.mean(acc, axis=-1, keepdims=True)
        d = acc - mu
        rstd = lax.rsqrt(jnp.mean(d * d, axis=-1, keepdims=True) + LN_EPS)
        z = d * rstd * lg_ref[...] + lb_ref[...]
        y_ref[...] = (z * _sigmoid(z)).astype(BF16)

    vec = pl.BlockSpec((1, C), lambda i: (0, 0))
    return _pcall(
        body, name="conv_fwd", grid=(T // tt,),
        in_specs=[pl.BlockSpec((tt, C), lambda i: (i, 0)), pl.BlockSpec((tt, C), lambda i: (i, 1)),
                  pl.BlockSpec((HALO, C), lambda i: (jnp.maximum(i * hb - 1, 0), 0)),
                  pl.BlockSpec((HALO, C), lambda i: (jnp.maximum(i * hb - 1, 0), 1)),
                  pl.BlockSpec((HALO, C), lambda i: (0, 0)), vec, vec, vec],
        out_specs=[pl.BlockSpec((tt, C), lambda i: (i, 0))] * 2,
        out_shape=[jax.ShapeDtypeStruct((T, C), F32), jax.ShapeDtypeStruct((T, C), BF16)],
        scratch=[pltpu.VMEM((tt + HALO, C), F32), pltpu.VMEM((SUBLANES - 1, tt + SHIFT_ROWS, C), F32)],
    )(proj, proj, proj, proj, conv_w, conv_b, ln_g, ln_b)


def conv_bwd(proj, ypre, dycat, conv_w, ln_g, ln_b):
    T = proj.shape[0]
    C = conv_w.shape[1]
    tt = _tile(T, 256, HALO)
    hb = tt // HALO
    nt = T // tt
    last_h = T // HALO - 1

    def ln_bwd(ypre_v, dout_v, lg, lb):
        mu = jnp.mean(ypre_v, axis=-1, keepdims=True)
        d = ypre_v - mu
        rstd = lax.rsqrt(jnp.mean(d * d, axis=-1, keepdims=True) + LN_EPS)
        yh = d * rstd
        z = yh * lg + lb
        s = _sigmoid(z)
        dz = dout_v * (s * (1.0 + z * (1.0 - s)))
        dyh = dz * lg
        dy = rstd * (dyh - jnp.mean(dyh, axis=-1, keepdims=True)
                     - yh * jnp.mean(dyh * yh, axis=-1, keepdims=True))
        return dy, dz, yh

    def body(a_ref, g_ref, ah_ref, gh_ref, yp_ref, ypn_ref, do_ref, don_ref, w_ref, lg_ref, lb_ref,
             dag_ref, dw_ref, dcb_ref, dlg_ref, dlb_ref, ubuf, dybuf, ush, dysh):
        i = pl.program_id(0)
        av = a_ref[...].astype(F32)
        sg = _sigmoid(g_ref[...].astype(F32))
        uh = ah_ref[...].astype(F32) * _sigmoid(gh_ref[...].astype(F32))
        ubuf[0:HALO, :] = jnp.where(i == 0, 0.0, uh)
        ubuf[HALO:HALO + tt, :] = av * sg
        lg, lb = lg_ref[...], lb_ref[...]
        dy, dz, yh = ln_bwd(yp_ref[...], do_ref[...].astype(F32), lg, lb)
        dyn, _, _ = ln_bwd(ypn_ref[...], don_ref[...].astype(F32), lg, lb)
        dybuf[0:tt, :] = dy
        dybuf[tt:tt + HALO, :] = jnp.where(i == nt - 1, 0.0, dyn)
        _shifted_copies(ubuf, ush, tt)
        _shifted_copies(dybuf, dysh, tt)

        @pl.when(i == 0)
        def _():
            dw_ref[...] = jnp.zeros_like(dw_ref)
            dcb_ref[...] = jnp.zeros_like(dcb_ref)
            dlg_ref[...] = jnp.zeros_like(dlg_ref)
            dlb_ref[...] = jnp.zeros_like(dlb_ref)

        du = jnp.zeros((tt, C), F32)
        for k in range(CONV_K):
            du = du + w_ref[k:k + 1, :] * _tap(dybuf, dysh, CONV_K - 1 - k, tt)
            dw_ref[k:k + 1, :] += jnp.sum(dy * _tap(ubuf, ush, HALO - (CONV_K - 1) + k, tt), axis=0, keepdims=True)
        dcb_ref[...] += jnp.sum(dy, axis=0, keepdims=True)
        dlg_ref[...] += jnp.sum(dz * yh, axis=0, keepdims=True)
        dlb_ref[...] += jnp.sum(dz, axis=0, keepdims=True)

        dag_ref[:, 0:C] = (du * sg).astype(BF16)
        dag_ref[:, C:2 * C] = (du * av * sg * (1.0 - sg)).astype(BF16)

    vec = pl.BlockSpec((1, C), lambda i: (0, 0))
    prev_h = lambda col: pl.BlockSpec((HALO, C), lambda i: (jnp.maximum(i * hb - 1, 0), col))
    next_h = pl.BlockSpec((HALO, C), lambda i: (jnp.minimum((i + 1) * hb, last_h), 0))
    return _pcall(
        body, name="conv_bwd", grid=(nt,),
        in_specs=[pl.BlockSpec((tt, C), lambda i: (i, 0)), pl.BlockSpec((tt, C), lambda i: (i, 1)),
                  prev_h(0), prev_h(1),
                  pl.BlockSpec((tt, C), lambda i: (i, 0)), next_h,
                  pl.BlockSpec((tt, C), lambda i: (i, 0)), next_h,
                  pl.BlockSpec((HALO, C), lambda i: (0, 0)), vec, vec],
        out_specs=[pl.BlockSpec((tt, 2 * C), lambda i: (i, 0)), pl.BlockSpec((HALO, C), lambda i: (0, 0)),
                   vec, vec, vec],
        out_shape=[jax.ShapeDtypeStruct((T, 2 * C), BF16), jax.ShapeDtypeStruct((HALO, C), F32),
                   jax.ShapeDtypeStruct((1, C), F32), jax.ShapeDtypeStruct((1, C), F32),
                   jax.ShapeDtypeStruct((1, C), F32)],
        scratch=[pltpu.VMEM((tt + HALO, C), F32), pltpu.VMEM((tt + HALO, C), F32),
                 pltpu.VMEM((SUBLANES - 1, tt + SHIFT_ROWS, C), F32),
                 pltpu.VMEM((SUBLANES - 1, tt + SHIFT_ROWS, C), F32)],
    )(proj, proj, proj, proj, ypre, ypre, dycat, dycat, conv_w, ln_g, ln_b)


PAIR = LANES // HEAD_DIM


def _head_masks(rows):
    lane = lax.broadcasted_iota(jnp.int32, (rows, LANES), 1)
    return [jnp.logical_and(lane >= hh * HEAD_DIM, lane < (hh + 1) * HEAD_DIM) for hh in range(PAIR)]


def _causal(tq, tk):
    return lax.broadcasted_iota(jnp.int32, (tq, tk), 0) >= lax.broadcasted_iota(jnp.int32, (tq, tk), 1)


def _lane_column(block, lane_index):
    lane = lax.broadcasted_iota(jnp.int32, block.shape, 1)
    return jnp.sum(jnp.where(lane == lane_index, block, 0.0), axis=-1, keepdims=True)


def attn_fwd(proj, cum, ck4, q_col, comm=None):
    T = proj.shape[0]
    H, nkv, _, tk = ck4.shape
    tq = tk
    hd = H * HEAD_DIM
    qb, kb, vb = q_col // LANES, (q_col + hd) // LANES, (q_col + 2 * hd) // LANES
    scale = 1.0 / math.sqrt(HEAD_DIM)

    def body(q_ref, k_ref, v_ref, cum_ref, ck_ref, o_ref, lse_ref):
        hp = pl.program_id(0)
        i = pl.program_id(1)
        masks = _head_masks(tq)
        q2 = q_ref[...] * scale
        qs = [jnp.where(mk, q2, jnp.zeros_like(q2)) for mk in masks]
        cqs = [_lane_column(cum_ref[...], PAIR * hp + hh) for hh in range(PAIR)]

        def step(j, carry, diagonal, blocks=1):
            scores, values = [], []
            for b in range(blocks):
                off = pl.multiple_of((j + b) * tk, tk)
                kj = k_ref[pl.ds(off, tk), :]
                values.append(v_ref[pl.ds(off, tk), :])
                per_head = []
                for hh in range(PAIR):
                    s = lax.dot_general(qs[hh], kj, NT, preferred_element_type=F32)
                    s = s + cqs[hh] - ck_ref[hh, j + b]
                    if diagonal:
                        s = jnp.where(_causal(tq, tk), s, NEG_INF)
                    per_head.append(s)
                scores.append(per_head)
            out = []
            for hh in range(PAIR):
                m, l, acc = carry[hh]
                m_new = m
                for b in range(blocks):
                    m_new = jnp.maximum(m_new, jnp.max(scores[b][hh], axis=-1, keepdims=True))
                alpha = jnp.exp(m - m_new)
                l, acc = alpha * l, alpha * acc
                for b in range(blocks):
                    p = jnp.exp(scores[b][hh] - m_new)
                    l = l + jnp.sum(p, axis=-1, keepdims=True)
                    acc = acc + jnp.dot(p.astype(BF16), values[b], preferred_element_type=F32)
                out.append((m_new, l, acc))
            return tuple(out)

        init = tuple((jnp.full((tq, 1), -jnp.inf, F32), jnp.zeros((tq, 1), F32), jnp.zeros((tq, LANES), F32))
                     for _ in range(PAIR))
        twos = i // 2
        carry = lax.fori_loop(0, twos, lambda t, c: step(2 * t, c, False, blocks=2), init)
        carry = lax.fori_loop(2 * twos, i, functools.partial(step, diagonal=False), carry)
        carry = step(i, carry, True)
        o = carry[PAIR - 1][2] / carry[PAIR - 1][1]
        for hh in range(PAIR - 1):
            o = jnp.where(masks[hh], carry[hh][2] / carry[hh][1], o)
        o_ref[...] = o
        lse = jnp.broadcast_to(carry[PAIR - 1][0] + jnp.log(carry[PAIR - 1][1]), (tq, LANES))
        for hh in range(PAIR - 1):
            lse = jnp.where(masks[hh], carry[hh][0] + jnp.log(carry[hh][1]), lse)
        lse_ref[...] = lse

    return _pcall(
        body, name="attn_fwd", grid=(H // PAIR, T // tq),
        in_specs=[pl.BlockSpec((tq, LANES), lambda hp, i: (i, qb + hp)),
                  pl.BlockSpec((T, LANES), lambda hp, i: (0, kb + hp)),
                  pl.BlockSpec((T, LANES), lambda hp, i: (0, vb + hp)),
                  pl.BlockSpec((tq, LANES), lambda hp, i: (i, 0)),
                  pl.BlockSpec((PAIR, nkv, 1, tk), lambda hp, i: (hp, 0, 0, 0))],
        out_specs=[pl.BlockSpec((tq, LANES), lambda hp, i: (i, hp)),
                   pl.BlockSpec((None, tq, LANES), lambda hp, i: (hp, i, 0))],
        out_shape=[jax.ShapeDtypeStruct((T, hd), F32), jax.ShapeDtypeStruct((H // PAIR, T, LANES), F32)],
        comm=comm,
    )(proj, proj, proj, cum, ck4)


def attn_bwd(proj, o, dycat, lse, cum, ck4, q_col, do_col, comm=None):
    T = proj.shape[0]
    H, nkv, _, tk = ck4.shape
    tq = tk
    nq = T // tq
    hd = H * HEAD_DIM
    qb, kb, vb = q_col // LANES, (q_col + hd) // LANES, (q_col + 2 * hd) // LANES
    dob = do_col // LANES
    scale = 1.0 / math.sqrt(HEAD_DIM)

    def body(q_ref, k_ref, v_ref, o_ref, do_ref, lse_ref, cum_ref, ck_ref,
             dq_ref, dk_ref, dv_ref, dcq_ref, dck_ref):
        hp = pl.program_id(0)
        j = pl.program_id(1)

        def block(i, diagonal):
            masks = _head_masks(tq)
            rows = pl.ds(pl.multiple_of(i * tq, tq), tq)
            q2, k2, v2, do2 = q_ref[rows, :] * scale, k_ref[...], v_ref[...], do_ref[rows, :]
            zero = jnp.zeros_like(q2)
            prod = do2.astype(F32) * o_ref[rows, :]
            cum_q, lse_q = cum_ref[rows, :], lse_ref[rows, :]
            dq_part = dk_part = dv_part = None
            dcq_part = jnp.zeros((tq, LANES), F32)
            lane = lax.broadcasted_iota(jnp.int32, (tq, LANES), 1)
            for hh in range(PAIR):
                qh = jnp.where(masks[hh], q2, zero)
                kh = jnp.where(masks[hh], k2, zero)
                doh = jnp.where(masks[hh], do2, zero)
                delta = jnp.sum(jnp.where(masks[hh], prod, 0.0), axis=-1, keepdims=True)
                s = lax.dot_general(qh, k2, NT, preferred_element_type=F32)
                s = s + _lane_column(cum_q, PAIR * hp + hh) - ck_ref[hh]
                if diagonal:
                    s = jnp.where(_causal(tq, tk), s, NEG_INF)
                p = jnp.exp(s - _lane_column(lse_q, hh * HEAD_DIM))
                dp = lax.dot_general(doh, v2, NT, preferred_element_type=F32)
                ds = p * (dp - delta)
                dsb = ds.astype(BF16)
                dv_h = lax.dot_general(p.astype(BF16), doh, TN, preferred_element_type=F32)
                dk_h = lax.dot_general(dsb, qh, TN, preferred_element_type=F32)
                dq_h = jnp.dot(dsb, kh, preferred_element_type=F32)
                dq_part = dq_h if dq_part is None else dq_part + dq_h
                dk_part = dk_h if dk_part is None else dk_part + dk_h
                dv_part = dv_h if dv_part is None else dv_part + dv_h
                dck_h = -jnp.sum(ds, axis=0, keepdims=True)
                dcq_part = jnp.where(lane == PAIR * hp + hh, jnp.sum(ds, axis=-1, keepdims=True), dcq_part)
                if diagonal:
                    dck_ref[hh] = dck_h
                else:
                    dck_ref[hh] += dck_h
            dq_part = dq_part * scale

            @pl.when(j == 0)
            def _():
                dq_ref[rows, :] = dq_part

            @pl.when(j > 0)
            def _():
                dq_ref[rows, :] += dq_part

            @pl.when(jnp.logical_and(hp == 0, j == 0))
            def _():
                dcq_ref[rows, :] = dcq_part

            @pl.when(jnp.logical_or(hp > 0, j > 0))
            def _():
                dcq_ref[rows, :] += dcq_part

            if diagonal:
                dk_ref[...] = dk_part
                dv_ref[...] = dv_part
            else:
                dk_ref[...] += dk_part
                dv_ref[...] += dv_part

        block(j, True)

        def later(i, carry):
            block(i, False)
            return carry

        lax.fori_loop(j + 1, nq, later, 0)

    at_q = lambda col: pl.BlockSpec((T, LANES), lambda hp, j: (0, col + hp))
    at_k = lambda col: pl.BlockSpec((tk, LANES), lambda hp, j: (j, col + hp))
    lse_spec = pl.BlockSpec((None, T, LANES), lambda hp, j: (hp, 0, 0))
    cum_spec = pl.BlockSpec((T, LANES), lambda hp, j: (0, 0))
    ck_spec = pl.BlockSpec((PAIR, None, 1, tk), lambda hp, j: (hp, j, 0, 0))
    return _pcall(
        body, name="attn_bwd", grid=(H // PAIR, nkv),
        in_specs=[at_q(qb), at_k(kb), at_k(vb), at_q(0), at_q(dob), lse_spec, cum_spec, ck_spec],
        out_specs=[pl.BlockSpec((T, LANES), lambda hp, j: (0, hp)), at_k(0), at_k(0),
                   pl.BlockSpec((T, LANES), lambda hp, j: (0, 0)), ck_spec],
        out_shape=[jax.ShapeDtypeStruct((T, hd), F32)] * 3
        + [jax.ShapeDtypeStruct((T, LANES), F32), jax.ShapeDtypeStruct((H, nkv, 1, tk), F32)],
        comm=comm,
    )(proj, proj, proj, o, dycat, lse, cum, ck4)


ELEMENTWISE_BLOCK_BYTES = 2 * 1024 * 1024
BF16_ROWS = 16


def cast_bf16(arrays, comm=None):
    def slab(a, steps):
        R, C = a.shape
        if R % (steps * BF16_ROWS) == 0:
            return pl.BlockSpec((R // steps, C), lambda i: (i, 0))
        if C % (steps * LANES) == 0:
            return pl.BlockSpec((R, C // steps), lambda i: (0, i))
        return None

    steps = 8 if all(slab(a, 8) is not None for a in arrays) else 4
    specs = [slab(a, steps) for a in arrays]
    n = len(arrays)

    def body(*refs):
        for src, dst in zip(refs[:n], refs[n:]):
            dst[...] = src[...].astype(BF16)

    return _pcall(body, name="cast_bf16", grid=(steps,), in_specs=specs, out_specs=specs,
                  out_shape=[jax.ShapeDtypeStruct(a.shape, BF16) for a in arrays], comm=comm)(*arrays)


def _ew_tiles(rows, cols, bytes_per_element):
    target = max(8, ELEMENTWISE_BLOCK_BYTES // max(1, cols * bytes_per_element))
    if rows <= target:
        return rows, cols
    t = (target // 16) * 16
    while t >= 16:
        if rows % t == 0:
            return t, cols
        t -= 16
    tc = _tile(cols, max(LANES, (ELEMENTWISE_BLOCK_BYTES // (rows * bytes_per_element)) // LANES * LANES))
    return rows, tc


def sum_chips(recv):
    nc, R, C = recv.shape
    tr, tc = _ew_tiles(R, C, 4)

    def body(r_ref, o_ref):
        acc = r_ref[0].astype(F32)
        for j in range(1, nc):
            acc = acc + r_ref[j].astype(F32)
        o_ref[...] = acc

    return _pcall(
        body, name="sum_chips", grid=(R // tr, C // tc),
        in_specs=[pl.BlockSpec((nc, tr, tc), lambda i, j: (0, i, j))],
        out_specs=[pl.BlockSpec((tr, tc), lambda i, j: (i, j))],
        out_shape=[jax.ShapeDtypeStruct((R, C), F32)],
    )(recv)[0]


def add_sibling_half(g, recv):
    nc, R, C = g.shape
    hr = R // 2
    tr, tc = _ew_tiles(hr // 2, C, 4 * nc)
    per_quarter = (hr // 2) // tr

    def body(g_ref, r_ref, o_ref):
        c = lax.axis_index("c")
        for j in range(nc):
            o_ref[j] = (g_ref[j, c].astype(F32) + r_ref[j].astype(F32)).astype(BF16)

    return _pcall(
        body, name="add_sibling_half", grid=(hr // tr, C // tc),
        in_specs=[pl.BlockSpec((nc, 2, tr, tc), lambda i, j: (0, 0, i, j)),
                  pl.BlockSpec((nc, tr, tc), lambda i, j: (0, i, j))],
        out_specs=[pl.BlockSpec((None, nc, tr, tc), lambda i, j: (i // per_quarter, 0, i % per_quarter, j))],
        out_shape=[jax.ShapeDtypeStruct((2, nc, hr // 2, C), BF16)],
    )(g.reshape(nc, 2, hr, C), recv)[0]


def adamw(w, m, v, g_parts, comm=None, halves=False):
    R, C = w.shape
    tr, tc = _ew_tiles(R // 2 if halves else R, C, 4 * 4)
    n_g = len(g_parts)
    n_half = (R // 2) // tr
    c1 = 1.0 - ADAM_B1
    c2 = 1.0 - ADAM_B2
    bc1 = 1.0 - ADAM_B1 ** ADAM_STEP
    bc2 = 1.0 - ADAM_B2 ** ADAM_STEP

    def body(*refs):
        w_ref, m_ref, v_ref = refs[:3]
        g_refs = refs[3:3 + n_g]
        g_out, d_out, m_out, v_out = refs[3 + n_g:]
        if halves:
            mine = (pl.program_id(0) >= n_half) == (lax.axis_index("c") == 1)
            g = jnp.where(mine, g_refs[0][...], g_refs[1][...])
        else:
            g = g_refs[0][...]
            for r in g_refs[1:]:
                g = g + r[...]
        m_new = ADAM_B1 * m_ref[...] + c1 * g
        v_new = ADAM_B2 * v_ref[...] + c2 * (g * g)
        m_hat = m_new / bc1
        v_hat = v_new / bc2
        g_out[...] = g
        d_out[...] = -ADAM_LR * (m_hat / (jnp.sqrt(v_hat) + ADAM_EPS) + ADAM_WD * w_ref[...])
        m_out[...] = m_new
        v_out[...] = v_new

    spec = pl.BlockSpec((tr, tc), lambda i, j: (i, j))
    g_spec = pl.BlockSpec((tr, tc), lambda i, j: (i % n_half, j)) if halves else spec
    return _pcall(
        body, name="adamw", grid=(R // tr, C // tc),
        in_specs=[spec] * 3 + [g_spec] * n_g, out_specs=[spec] * 4,
        out_shape=[jax.ShapeDtypeStruct((R, C), F32)] * 4, comm=comm,
    )(w, m, v, *g_parts)


def _chip_coords():
    x, y, c = lax.axis_index("x"), lax.axis_index("y"), lax.axis_index("c")
    others = [(1 - x, y), (x, 1 - y), (1 - x, 1 - y)]
    return x, y, c, others


def _remote(src, dst, send_sem, recv_sem, device):
    return pltpu.make_async_remote_copy(src_ref=src, dst_ref=dst, send_sem=send_sem, recv_sem=recv_sem,
                                        device_id=device, device_id_type=MESH)


def gather_comm(shards):
    n = len(shards)
    SLOTS = 7

    def makers(ins, outs, sems):
        send_sems, recv_sems, local_sems = sems
        x, y, c, _ = _chip_coords()
        me, xn, yn, dg = 2 * x + y, 2 * (1 - x) + y, 2 * x + (1 - y), 2 * (1 - x) + (1 - y)
        to_x, to_y, sibling = (1 - x, y, c), (x, 1 - y, c), (x, y, 1 - c)

        def part(ref, a, half, quarter=None, chip=None):
            rows, cols = ins[a].shape[0], ins[a].shape[1]
            lead = () if chip is None else (chip,)
            along_rows = rows % (4 * BF16_ROWS) == 0 or (ins[a].dtype == F32 and rows % (4 * SUBLANES) == 0)
            size = (rows if along_rows else cols) // 2
            start = half * size
            if quarter is not None:
                size = size // 2
                start = start + quarter * size
            if along_rows:
                return ref.at[(*lead, pl.ds(start, size))]
            return ref.at[(*lead, slice(None), pl.ds(start, size))]

        def copy(a, k, src, dst, device):
            return _remote(src, dst, send_sems.at[SLOTS * a + k], recv_sems.at[SLOTS * a + k], device)

        def local(a):
            return pltpu.make_async_copy(ins[a], outs[a].at[me], local_sems.at[a])

        def first_leg(a):
            mine = part(outs[a], a, c, chip=me)
            return [copy(a, 0, part(ins[a], a, c), mine, to_x), copy(a, 1, part(ins[a], a, c), mine, to_y)]

        def arrived(a, k):
            region = {0: part(outs[a], a, c, chip=xn), 1: part(outs[a], a, c, chip=yn),
                      2: part(outs[a], a, c, 0, chip=dg), 3: part(outs[a], a, c, 1, chip=dg),
                      4: part(outs[a], a, 1 - c, chip=xn), 5: part(outs[a], a, 1 - c, chip=yn),
                      6: part(outs[a], a, 1 - c, chip=dg)}[k]
            return copy(a, k, region, region, sibling if k >= 4 else (to_x if k in (0, 3) else to_y))

        def relays(a):
            qx, qy = part(outs[a], a, c, 0, chip=xn), part(outs[a], a, c, 1, chip=yn)
            return [copy(a, 2, qx, qx, to_y), copy(a, 3, qy, qy, to_x)]

        def handover(a, k):
            region = part(outs[a], a, c, chip={4: xn, 5: yn, 6: dg}[k])
            return copy(a, k, region, region, sibling)

        return local, first_leg, arrived, relays, handover

    def start(ins, outs, sems):
        local, first_leg, _, _, _ = makers(ins, outs, sems)
        for a in range(n):
            for cp in first_leg(a):
                cp.start()
        for a in range(n):
            local(a).start()

    def relay(ins, outs, sems):
        _, _, arrived, relays, handover = makers(ins, outs, sems)
        for a in range(n):
            to_y_nbr, to_x_nbr = relays(a)
            arrived(a, 0).wait_recv()
            to_y_nbr.start()
            handover(a, 4).start()
            arrived(a, 1).wait_recv()
            to_x_nbr.start()
            handover(a, 5).start()

    def finish(ins, outs, sems):
        local, first_leg, arrived, relays, handover = makers(ins, outs, sems)
        for a in range(n):
            arrived(a, 2).wait_recv()
            arrived(a, 3).wait_recv()
            handover(a, 6).start()
        for a in range(n):
            for k in (4, 5, 6):
                arrived(a, k).wait_recv()
        for a in range(n):
            for cp in first_leg(a) + relays(a) + [handover(a, k) for k in (4, 5, 6)]:
                cp.wait_send()
            local(a).wait()

    return Comm(shards, [jax.ShapeDtypeStruct((N_CHIP,) + s.shape, s.dtype) for s in shards],
                [pltpu.SemaphoreType.DMA((SLOTS * n,)), pltpu.SemaphoreType.DMA((SLOTS * n,)),
                 pltpu.SemaphoreType.DMA((n,))], start, finish, relay)


def scatter_comm(grads):
    n = len(grads)
    pieces = [(a, jj) for a in range(n) for jj in range(3)]

    def makers(ins, outs, sems):
        send_sems, recv_sems, local_sems = sems
        x, y, c, others = _chip_coords()
        me = 2 * x + y

        def local(a):
            return pltpu.make_async_copy(ins[a].at[me], outs[a].at[me], local_sems.at[a])

        def ici(a, jj):
            ox, oy = others[jj]
            return _remote(ins[a].at[2 * ox + oy], outs[a].at[me], send_sems.at[3 * a + jj],
                           recv_sems.at[3 * a + jj], (ox, oy, c))

        def landed(a, jj):
            ox, oy = others[jj]
            slot = outs[a].at[2 * ox + oy]
            return _remote(slot, slot, send_sems.at[3 * a + jj], recv_sems.at[3 * a + jj], (ox, oy, c))

        return local, ici, landed

    def start(ins, outs, sems):
        local, ici, _ = makers(ins, outs, sems)
        for a in range(n):
            for jj in (2, 0, 1):
                ici(a, jj).start()
        for a in range(n):
            local(a).start()

    def finish(ins, outs, sems):
        local, ici, landed = makers(ins, outs, sems)
        for a, jj in pieces:
            landed(a, jj).wait_recv()
        for a, jj in pieces:
            ici(a, jj).wait_send()
        for a in range(n):
            local(a).wait()

    return Comm(grads, [jax.ShapeDtypeStruct(g.shape, g.dtype) for g in grads],
                [pltpu.SemaphoreType.DMA((3 * n,)), pltpu.SemaphoreType.DMA((3 * n,)),
                 pltpu.SemaphoreType.DMA((n,))], start, finish)


def halfswap_comm(grads):
    n = len(grads)

    def copies(ins, outs, sems):
        send_sems, recv_sems = sems
        x, y, c, _ = _chip_coords()
        out = []
        for a in range(n):
            hr = ins[a].shape[1] // 2
            out.append(_remote(ins[a].at[:, pl.ds((1 - c) * hr, hr)], outs[a], send_sems.at[a], recv_sems.at[a],
                               (x, y, 1 - c)))
        return out

    def start(ins, outs, sems):
        for cp in copies(ins, outs, sems):
            cp.start()

    def finish(ins, outs, sems):
        for cp in copies(ins, outs, sems):
            cp.wait()

    return Comm(grads, [jax.ShapeDtypeStruct((g.shape[0], g.shape[1] // 2, g.shape[2]), g.dtype) for g in grads],
                [pltpu.SemaphoreType.DMA((n,)), pltpu.SemaphoreType.DMA((n,))], start, finish)


def join_comms(first, second):
    ni, no, ns = len(first.operands), len(first.out_shape), len(first.sems)

    def start(ins, outs, sems):
        first.start(ins[:ni], outs[:no], sems[:ns])
        second.start(ins[ni:], outs[no:], sems[ns:])

    def finish(ins, outs, sems):
        first.finish(ins[:ni], outs[:no], sems[:ns])
        second.finish(ins[ni:], outs[no:], sems[ns:])

    def relay(ins, outs, sems):
        if first.relay is not None:
            first.relay(ins[:ni], outs[:no], sems[:ns])
        if second.relay is not None:
            second.relay(ins[ni:], outs[no:], sems[ns:])

    return Comm(first.operands + second.operands, first.out_shape + second.out_shape, first.sems + second.sems,
                start, finish, relay if (first.relay or second.relay) else None)


def swap_comm(parts):
    n = len(parts)

    def copies(ins, outs, sems):
        send_sems, recv_sems = sems
        x, y, c, _ = _chip_coords()
        return [_remote(ins[a], outs[a], send_sems.at[a], recv_sems.at[a], (x, y, 1 - c)) for a in range(n)]

    def start(ins, outs, sems):
        for cp in copies(ins, outs, sems):
            cp.start()

    def finish(ins, outs, sems):
        for cp in copies(ins, outs, sems):
            cp.wait()

    return Comm(parts, [jax.ShapeDtypeStruct(p.shape, p.dtype) for p in parts],
                [pltpu.SemaphoreType.DMA((n,)), pltpu.SemaphoreType.DMA((n,))], start, finish)


def allreduce_small(v):
    R = v.shape[0]

    def body(v_ref, sum_ref, all_ref, send_sems, recv_sems):
        x, y, c = lax.axis_index("x"), lax.axis_index("y"), lax.axis_index("c")
        me = 4 * x + 2 * y + c
        all_ref[me] = v_ref[...]
        copies = []
        for k in range(1, N_DEV):
            px = 1 - x if k & 4 else x
            py = 1 - y if k & 2 else y
            pc = 1 - c if k & 1 else c
            cp = pltpu.make_async_remote_copy(
                src_ref=v_ref, dst_ref=all_ref.at[me], send_sem=send_sems.at[k - 1], recv_sem=recv_sems.at[k - 1],
                device_id=(px, py, pc), device_id_type=MESH)
            cp.start()
            copies.append((cp, 4 * px + 2 * py + pc))
        for k, (cp, peer) in enumerate(copies):
            pltpu.make_async_remote_copy(
                src_ref=v_ref, dst_ref=all_ref.at[peer], send_sem=send_sems.at[k], recv_sem=recv_sems.at[k],
                device_id=(x, y, c), device_id_type=MESH).wait_recv()
        for cp, _ in copies:
            cp.wait_send()
        acc = all_ref[0]
        for d in range(1, N_DEV):
            acc = acc + all_ref[d]
        sum_ref[...] = acc

    vm = pl.BlockSpec(memory_space=pltpu.VMEM)
    return pl.pallas_call(
        body, name="allreduce_small",
        in_specs=[vm], out_specs=[vm, vm],
        out_shape=[jax.ShapeDtypeStruct((R, LANES), F32), jax.ShapeDtypeStruct((N_DEV, R, LANES), F32)],
        scratch_shapes=[pltpu.SemaphoreType.DMA((N_DEV - 1,)), pltpu.SemaphoreType.DMA((N_DEV - 1,))],
    )(v)[0]


SMALL_NAMES = ("ffn1_norm", "mix_norm", "ffn2_norm", "final_norm", "conv_b", "conv_ln_g", "conv_ln_b")


def _pack_small(vecs, bias, conv_w_rows, loss_tile):
    rows = [vecs[n].reshape(-1, LANES) for n in SMALL_NAMES]
    rows.append(bias.reshape(1, LANES))
    rows.append(conv_w_rows.reshape(-1, LANES))
    rows.append(loss_tile[0:1, :])
    packed = jnp.concatenate(rows, axis=0)
    pad = (-packed.shape[0]) % 8
    return jnp.pad(packed, ((0, pad), (0, 0)))


def _unpack_small(packed, sizes, n_conv_rows):
    out, r = {}, 0
    for n in SMALL_NAMES:
        k = sizes[n] // LANES
        out[n] = packed[r:r + k].reshape(-1)
        r += k
    out["fgate_bias"] = packed[r]
    r += 1
    out["conv_w"] = packed[r:r + n_conv_rows]
    r += n_conv_rows
    out["loss"] = packed[r, 0]
    return out


def kernel(x, ffn1_norm, ffn1_w_gate, ffn1_w_up, ffn1_w_down, mix_norm, w_in, fgate_bias, conv_w, conv_b, conv_ln_g, conv_ln_b, w_out, ffn2_norm, ffn2_w_gate, ffn2_w_up, ffn2_w_down, final_norm, loss_target, m_ffn1_norm, m_ffn1_w_gate, m_ffn1_w_up, m_ffn1_w_down, m_mix_norm, m_w_in, m_fgate_bias, m_conv_w, m_conv_b, m_conv_ln_g, m_conv_ln_b, m_w_out, m_ffn2_norm, m_ffn2_w_gate, m_ffn2_w_up, m_ffn2_w_down, m_final_norm, v_ffn1_norm, v_ffn1_w_gate, v_ffn1_w_up, v_ffn1_w_down, v_mix_norm, v_w_in, v_fgate_bias, v_conv_w, v_conv_b, v_conv_ln_g, v_conv_ln_b, v_w_out, v_ffn2_norm, v_ffn2_w_gate, v_ffn2_w_up, v_ffn2_w_down, v_final_norm):
    w = dict(ffn1_norm=ffn1_norm, ffn1_w_gate=ffn1_w_gate, ffn1_w_up=ffn1_w_up, ffn1_w_down=ffn1_w_down,
             mix_norm=mix_norm, w_in=w_in, fgate_bias=fgate_bias, conv_w=conv_w, conv_b=conv_b,
             conv_ln_g=conv_ln_g, conv_ln_b=conv_ln_b, w_out=w_out, ffn2_norm=ffn2_norm,
             ffn2_w_gate=ffn2_w_gate, ffn2_w_up=ffn2_w_up, ffn2_w_down=ffn2_w_down, final_norm=final_norm)
    m = dict(ffn1_norm=m_ffn1_norm, ffn1_w_gate=m_ffn1_w_gate, ffn1_w_up=m_ffn1_w_up, ffn1_w_down=m_ffn1_w_down,
             mix_norm=m_mix_norm, w_in=m_w_in, fgate_bias=m_fgate_bias, conv_w=m_conv_w, conv_b=m_conv_b,
             conv_ln_g=m_conv_ln_g, conv_ln_b=m_conv_ln_b, w_out=m_w_out, ffn2_norm=m_ffn2_norm,
             ffn2_w_gate=m_ffn2_w_gate, ffn2_w_up=m_ffn2_w_up, ffn2_w_down=m_ffn2_w_down, final_norm=m_final_norm)
    v = dict(ffn1_norm=v_ffn1_norm, ffn1_w_gate=v_ffn1_w_gate, ffn1_w_up=v_ffn1_w_up, ffn1_w_down=v_ffn1_w_down,
             mix_norm=v_mix_norm, w_in=v_w_in, fgate_bias=v_fgate_bias, conv_w=v_conv_w, conv_b=v_conv_b,
             conv_ln_g=v_conv_ln_g, conv_ln_b=v_conv_ln_b, w_out=v_w_out, ffn2_norm=v_ffn2_norm,
             ffn2_w_gate=v_ffn2_w_gate, ffn2_w_up=v_ffn2_w_up, ffn2_w_down=v_ffn2_w_down, final_norm=v_final_norm)
    names = list(w.keys())
    big = ("ffn1_w_gate", "ffn1_w_up", "ffn1_w_down", "w_in", "w_out", "ffn2_w_gate", "ffn2_w_up", "ffn2_w_down")

    T, D = x.shape[1], x.shape[2]
    C = conv_b.shape[0]
    H = fgate_bias.shape[0]
    cs = conv_w.shape[1]
    in_cols = N_CHIP * w_in.shape[1]
    p_main = in_cols - H

    x0, tgt = x[0], loss_target[0]
    tk = _tile(T, 512, 128)
    nkv = T // tk
    row = lambda a: a.reshape(1, -1)
    grad, delta, new_m, new_v = {}, {}, {}, {}

    def update(n, parts, comm=None, halves=False):
        args = (w[n], m[n], v[n])
        if n == "w_in":
            outs = [t.T for t in adamw(*[a.T for a in args], parts, comm=comm)]
        else:
            outs = adamw(*args, parts, comm=comm, halves=halves)
        grad[n], delta[n], new_m[n], new_v[n] = outs

    rest = [n for n in big if n != "ffn1_w_gate"]
    g0 = gather_comm([w["ffn1_w_gate"].astype(BF16), jnp.pad(conv_w, ((0, HALO - CONV_K), (0, 0)))])
    wb = dict(zip(rest, cast_bf16([w[n].T if n == "w_in" else w[n] for n in rest], comm=g0)))
    wg1, conv_w4 = g0.results
    conv_w_full = conv_w4.transpose(1, 0, 2).reshape(HALO, C)
    h1, r1 = rms_fwd(x0, row(ffn1_norm))
    g1a = gather_comm([wb["ffn1_w_up"]])
    a1 = ffn_gate(h1, wg1, comm=g1a)
    wu1 = g1a.results[0]
    g1b = gather_comm([wb["ffn1_w_down"]])
    b1, mid1 = ffn_upmul(h1, wu1, a1, comm=g1b)
    wd1 = g1b.results[0]
    g2 = gather_comm([wb["w_in"]])
    x1 = mm_residual("ffn_down_g", mid1, wd1, x0, 0.5, comm=g2)[0]
    w_t = g2.results[0].reshape(in_cols, D)

    wf_t = jnp.pad(w_t[p_main:], ((0, LANES - H), (0, 0)))
    bias_pad = jnp.pad(row(fgate_bias), ((0, 0), (0, LANES - H)))
    h2, r2 = rms_fwd(x1, row(mix_norm))
    g_out = gather_comm([wb["w_out"]])
    proj = proj_main(h2, w_t, p_main, comm=g_out)
    w_out3 = g_out.results[0]
    f, cum = fgate_fwd(h2, wf_t, bias_pad, H)
    ypre, yconv = conv_fwd(proj, conv_w_full, row(conv_b), row(conv_ln_g), row(conv_ln_b))
    ck4 = cum[:, :H].T.reshape(H, nkv, 1, tk)
    g3 = gather_comm([wb["ffn2_w_gate"], wb["ffn2_w_up"]])
    o, lse = attn_fwd(proj, cum, ck4, 2 * C, comm=g3)
    wg2, wu2 = g3.results
    ycat = jnp.concatenate([yconv, o.astype(BF16)], axis=1)
    x2 = mm_residual("out_proj", ycat, w_out3.reshape(2, -1, D), x1, 1.0)[0]

    h3, r3 = rms_fwd(x2, row(ffn2_norm))
    g4 = gather_comm([wb["ffn2_w_down"]])
    a2, b2, mid2 = ffn_up(h3, wg2, wu2, comm=g4)
    wd2 = g4.results[0]
    x3 = mm_residual("ffn_down", mid2, wd2, x2, 0.5)[0]
    dx3, dx3b, loss_tile, d_final = final_loss(x3, tgt, row(final_norm))

    da2, db2 = ffn_bwd_mid(dx3b, wd2, a2, b2)
    dwd2 = dw_rowshard("ffn_dwd", mid2, dx3b, N_CHIP)[0]
    s1 = scatter_comm([dwd2])
    dwg2, dwu2 = dw_colshard("ffn_dwgu_s", h3, [da2, db2], N_CHIP, comm=s1)
    s2 = scatter_comm([dwg2])
    dh3 = ffn_dh(da2, db2, wg2, wu2, comm=s2)
    dx2, dx2b, d_ffn2_norm = rms_bwd(dh3, x2, r3, row(ffn2_norm), dx3, 1.0)

    dycat = mm_nt_bf16("out_proj_dy", dx2b, w_out3.reshape(-1, D))
    dw_out3 = dw_rowshard("out_proj_dw", ycat, dx2b, N_CHIP)[0]
    s3 = scatter_comm([dwu2, dw_out3])
    dq, dk, dv, dcq, dck4 = attn_bwd(proj, o, dycat, lse, cum, ck4, 2 * C, C, comm=s3)
    dc = dcq + jnp.pad(dck4.reshape(H, T).T, ((0, 0), (0, LANES - H)))
    df, d_bias = fgate_bwd(dc, f, H)
    dag, d_conv_w, d_conv_b, d_ln_g, d_ln_b = conv_bwd(proj, ypre, dycat, conv_w_full, row(conv_ln_g),
                                                       row(conv_ln_b))
    dproj = jnp.concatenate([dag, dq.astype(BF16), dk.astype(BF16), dv.astype(BF16)], axis=1)
    early = ("ffn2_w_down", "ffn2_w_gate", "ffn2_w_up", "w_out")
    early_sums = [sum_chips(r) for r in (s1.results[0], s2.results[0], s3.results[0], s3.results[1])]
    sw1 = swap_comm(early_sums)
    dh2 = proj_dh(dproj, w_t, df, wf_t, comm=sw1)
    dw_t, dwf_t = proj_dw(dproj, df, h2, in_cols)
    gate_rows = dwf_t[:H].astype(BF16).reshape(H, 2, D // 2).transpose(1, 0, 2)
    dw_t = lax.dynamic_update_slice(dw_t, gate_rows, (0, p_main, 0))
    dw_in_halves = [dw_t[half].reshape(N_CHIP, in_cols // N_CHIP, D // 2) for half in range(2)]
    dx1, dx1b, d_mix_norm = rms_bwd(dh2, x1, r2, row(mix_norm), dx2, 0.5)

    s4a = scatter_comm([dw_in_halves[0]])
    da1, db1 = ffn_bwd_mid(dx1b, wd1, a1, b1, comm=s4a)
    s4b = scatter_comm([dw_in_halves[1]])
    dwd1 = dw_rowshard("ffn_dwd_s", mid1, dx1b, N_CHIP, comm=s4b)[0]
    s5 = scatter_comm([dwd1])
    dwg1, dwu1 = dw_colshard("ffn_dwgu_s", h1, [da1, db1], N_CHIP, comm=s5)
    sum_in = jnp.concatenate([sum_chips(s4a.results[0]), sum_chips(s4b.results[0])], axis=1)
    mid_sums = [sum_in, sum_chips(s5.results[0])]
    s6 = join_comms(join_comms(scatter_comm([dwg1]), halfswap_comm([dwu1])), swap_comm(mid_sums))
    dh1 = ffn_dh(da1, db1, wg1, wu1, comm=s6)
    recv_g1, sibling_u1, their_in, their_d1 = s6.results
    grad_x, _, d_ffn1_norm = rms_bwd(dh1, x0, r1, row(ffn1_norm), dx1, 1.0)

    quarters = add_sibling_half(dwu1, sibling_u1)
    s7 = [scatter_comm([quarters[0]]), scatter_comm([quarters[1]])]
    for i, (n, mine, other) in enumerate(zip(early, early_sums, sw1.results)):
        update(n, [mine, other], comm=s7[i] if i < 2 else None)
    update("w_in", [mid_sums[0], their_in])
    update("ffn1_w_down", [mid_sums[1], their_d1])
    sum_g1 = sum_chips(recv_g1)
    half_u1 = jnp.concatenate([sum_chips(s7[0].results[0]), sum_chips(s7[1].results[0])], axis=0)
    their_g1, their_u1 = _run_comm("swap_last", swap_comm([sum_g1, half_u1]))
    update("ffn1_w_gate", [sum_g1, their_g1])
    update("ffn1_w_up", [half_u1, their_u1], halves=True)

    gl = dict(ffn1_norm=d_ffn1_norm, mix_norm=d_mix_norm, ffn2_norm=d_ffn2_norm, final_norm=d_final,
              conv_b=d_conv_b, conv_ln_g=d_ln_g, conv_ln_b=d_ln_b)
    small_sizes = {n: w[n].shape[0] for n in SMALL_NAMES}
    packed = _pack_small(gl, d_bias, d_conv_w, loss_tile)
    red = _unpack_small(allreduce_small(packed), small_sizes, HALO * C // LANES)
    loss = red["loss"]
    my_chip = 2 * lax.axis_index("x") + lax.axis_index("y")
    g_conv_w = lax.dynamic_slice_in_dim(red["conv_w"].reshape(HALO, C)[:CONV_K], my_chip * cs, cs, axis=1)
    update("conv_w", [g_conv_w])
    vec_names = SMALL_NAMES + ("fgate_bias",)
    stack = lambda d: jnp.concatenate(
        [jnp.pad(d[n], (0, (-d[n].shape[0]) % LANES)).reshape(-1, LANES) for n in vec_names], axis=0)
    g_stack = jnp.concatenate([red[n].reshape(-1, LANES) for n in SMALL_NAMES] + [red["fgate_bias"][None, :]],
                              axis=0)
    outs = adamw(stack(w), stack(m), stack(v), [g_stack])
    r = 0
    for n in vec_names:
        size = w[n].shape[0]
        k = -(-size // LANES)
        for dst, src in zip((grad, delta, new_m, new_v), outs):
            dst[n] = src[r:r + k].reshape(-1)[:size]
        r += k

    return (loss, grad_x[None], *[grad[n] for n in names], *[delta[n] for n in names],
            *[new_m[n] for n in names], *[new_v[n] for n in names])
```

```python
import functools
import math

import jax
import jax.numpy as jnp
from jax import lax
from jax.experimental import pallas as pl
from jax.experimental.pallas import tpu as pltpu

F32 = jnp.float32
BF16 = jnp.bfloat16
NORM_EPS = 1e-6
LN_EPS = 1e-5
NEG_INF = -1e30
HEAD_DIM = 64
CONV_K = 31
HALO = 32
LANES = 128
N_CHIP = 4
N_DEV = 8
VMEM_LIMIT = 52 * 1024 * 1024
MESH = pl.DeviceIdType.MESH

ADAM_LR = 0.001
ADAM_B1 = 0.9
ADAM_B2 = 0.999
ADAM_EPS = 1e-08
ADAM_WD = 0.01
ADAM_STEP = 10

NN = (((1,), (0,)), ((), ()))
NT = (((1,), (1,)), ((), ()))
TN = (((0,), (0,)), ((), ()))


def _tile(n, pref, unit=128):
    if n <= pref:
        return n
    t = (pref // unit) * unit
    while t > 0:
        if n % t == 0:
            return t
        t -= unit
    raise ValueError(f"no tile for {n} under {pref}")


RELAY_AT = (3, 4)


class Comm:
    def __init__(self, operands, out_shape, sems, start, finish, relay=None):
        self.operands, self.out_shape, self.sems = list(operands), list(out_shape), list(sems)
        self.start, self.finish, self.relay = start, finish, relay
        self.results = None


def _pcall(body, *, name, grid, in_specs, out_specs, out_shape, scratch=(), comm=None):
    params = pltpu.CompilerParams(dimension_semantics=("arbitrary",) * len(grid), vmem_limit_bytes=VMEM_LIMIT)
    scratch = list(scratch)
    if comm is None:
        return pl.pallas_call(body, name=name, grid=grid, in_specs=in_specs, out_specs=out_specs,
                              out_shape=out_shape, scratch_shapes=scratch, compiler_params=params)
    n_in, n_out, n_s = len(in_specs), len(out_shape), len(scratch)
    n_ci, n_co = len(comm.operands), len(comm.out_shape)
    any_spec = pl.BlockSpec(memory_space=pl.ANY)

    def carried(*refs):
        ins, refs = refs[:n_in], refs[n_in:]
        c_ins, refs = refs[:n_ci], refs[n_ci:]
        outs, refs = refs[:n_out], refs[n_out:]
        c_outs, refs = refs[:n_co], refs[n_co:]
        scr, c_sems = refs[:n_s], refs[n_s:]
        step = pl.program_id(0)
        for d in range(1, len(grid)):
            step = step * grid[d] + pl.program_id(d)
        total = math.prod(grid)
        first, last = step == 0, step == total - 1

        @pl.when(first)
        def _():
            comm.start(c_ins, c_outs, c_sems)

        if comm.relay is not None:
            @pl.when(step == min(total - 1, (RELAY_AT[0] * total) // RELAY_AT[1]))
            def _():
                comm.relay(c_ins, c_outs, c_sems)

        body(*ins, *outs, *scr)

        @pl.when(last)
        def _():
            comm.finish(c_ins, c_outs, c_sems)

    call = pl.pallas_call(
        carried, name=name, grid=grid, in_specs=list(in_specs) + [any_spec] * n_ci,
        out_specs=list(out_specs) + [any_spec] * n_co, out_shape=list(out_shape) + comm.out_shape,
        scratch_shapes=scratch + comm.sems, compiler_params=params)

    def run(*operands):
        res = call(*operands, *comm.operands)
        comm.results = list(res[n_out:])
        return list(res[:n_out])

    return run


def _run_comm(name, comm):
    n_ci, n_co = len(comm.operands), len(comm.out_shape)
    any_spec = pl.BlockSpec(memory_space=pl.ANY)

    def body(*refs):
        c_ins, c_outs, c_sems = refs[:n_ci], refs[n_ci:n_ci + n_co], refs[n_ci + n_co:]
        comm.start(c_ins, c_outs, c_sems)
        if comm.relay is not None:
            comm.relay(c_ins, c_outs, c_sems)
        comm.finish(c_ins, c_outs, c_sems)

    return pl.pallas_call(body, name=name, in_specs=[any_spec] * n_ci, out_specs=[any_spec] * n_co,
                          out_shape=comm.out_shape, scratch_shapes=comm.sems)(*comm.operands)


def _sigmoid(x):
    return 1.0 / (1.0 + jnp.exp(-x))


def _mm(name, *, grid, pairs, once_pairs=(), extra=(), out_shape, out_specs, acc_shapes, nk, kaxis, epilogue,
        comm=None):
    all_pairs = list(pairs) + list(once_pairs)
    n_p, n_o = len(pairs), len(once_pairs)
    n_e, n_out, n_acc = len(extra), len(out_shape), len(acc_shapes)

    def body(*refs):
        ab = refs[: 2 * (n_p + n_o)]
        ex = refs[2 * (n_p + n_o): 2 * (n_p + n_o) + n_e]
        outs = refs[2 * (n_p + n_o) + n_e: 2 * (n_p + n_o) + n_e + n_out]
        accs = refs[2 * (n_p + n_o) + n_e + n_out:]

        def dots(idx_range):
            vals = [None] * n_acc
            for p in idx_range:
                d = lax.dot_general(ab[2 * p][...], ab[2 * p + 1][...], all_pairs[p][4],
                                    preferred_element_type=F32)
                ai = all_pairs[p][5]
                vals[ai] = d if vals[ai] is None else vals[ai] + d
            return vals

        if nk == 1:
            vals = dots(range(n_p + n_o))
            epilogue(vals, ex, outs)
            return

        k = pl.program_id(kaxis)

        @pl.when(k == 0)
        def _():
            vals = dots(range(n_p + n_o))
            for ai in range(n_acc):
                accs[ai][...] = vals[ai]

        @pl.when(k > 0)
        def _():
            vals = dots(range(n_p))
            for ai in range(n_acc):
                if vals[ai] is not None:
                    accs[ai][...] += vals[ai]

        @pl.when(k == nk - 1)
        def _():
            epilogue([a[...] for a in accs], ex, outs)

    operands, in_specs = [], []
    for p in all_pairs:
        operands += [p[0], p[2]]
        in_specs += [p[1], p[3]]
    for arr, spec in extra:
        operands.append(arr)
        in_specs.append(spec)
    scratch = [pltpu.VMEM(s, F32) for s in acc_shapes] if nk > 1 else []
    return _pcall(body, name=name, grid=grid, in_specs=in_specs, out_specs=out_specs, out_shape=out_shape,
                  scratch=scratch, comm=comm)(*operands)


def rms_fwd(x, g):
    T, D = x.shape
    tt = _tile(T, 512, 8)

    def body(x_ref, g_ref, h_ref, r_ref):
        xv = x_ref[...]
        r = lax.rsqrt(jnp.mean(xv * xv, axis=-1, keepdims=True) + NORM_EPS)
        h_ref[...] = (xv * r * g_ref[...]).astype(BF16)
        r_ref[...] = r

    return _pcall(
        body, name="rms_fwd", grid=(T // tt,),
        in_specs=[pl.BlockSpec((tt, D), lambda i: (i, 0)), pl.BlockSpec((1, D), lambda i: (0, 0))],
        out_specs=[pl.BlockSpec((tt, D), lambda i: (i, 0)), pl.BlockSpec((tt, 1), lambda i: (i, 0))],
        out_shape=[jax.ShapeDtypeStruct((T, D), BF16), jax.ShapeDtypeStruct((T, 1), F32)],
    )(x, g)


def rms_bwd(dh, x, r, g, dres, out_scale, comm=None):
    T, D = x.shape
    tt = _tile(T, 256, 8)

    def body(dh_ref, x_ref, r_ref, g_ref, dres_ref, dx_ref, dxb_ref, dg_ref):
        i = pl.program_id(0)
        xh = x_ref[...] * r_ref[...]
        dhv = dh_ref[...]
        dxh = dhv * g_ref[...]
        dx = dres_ref[...] + r_ref[...] * (dxh - xh * jnp.mean(dxh * xh, axis=-1, keepdims=True))
        dx_ref[...] = dx
        dxb_ref[...] = (out_scale * dx).astype(BF16)
        part = jnp.sum(dhv * xh, axis=0, keepdims=True)

        @pl.when(i == 0)
        def _():
            dg_ref[...] = part

        @pl.when(i > 0)
        def _():
            dg_ref[...] += part

    row = pl.BlockSpec((tt, D), lambda i: (i, 0))
    return _pcall(
        body, name="rms_bwd", grid=(T // tt,),
        in_specs=[row, row, pl.BlockSpec((tt, 1), lambda i: (i, 0)), pl.BlockSpec((1, D), lambda i: (0, 0)), row],
        out_specs=[row, row, pl.BlockSpec((1, D), lambda i: (0, 0))],
        out_shape=[jax.ShapeDtypeStruct((T, D), F32), jax.ShapeDtypeStruct((T, D), BF16),
                   jax.ShapeDtypeStruct((1, D), F32)], comm=comm,
    )(dh, x, r, g, dres)


def final_loss(x, tgt, g):
    T, D = x.shape
    tt = _tile(T, 256, 8)

    def body(x_ref, t_ref, g_ref, dx_ref, dxb_ref, loss_ref, dg_ref):
        i = pl.program_id(0)
        xv = x_ref[...]
        r = lax.rsqrt(jnp.mean(xv * xv, axis=-1, keepdims=True) + NORM_EPS)
        xh = xv * r
        err = xh * g_ref[...] - t_ref[...]
        part_loss = 0.5 * jnp.sum(jnp.mean(err * err, axis=-1, keepdims=True), axis=0, keepdims=True)
        dy = err * (1.0 / D)
        dxh = dy * g_ref[...]
        dx = r * (dxh - xh * jnp.mean(dxh * xh, axis=-1, keepdims=True))
        dx_ref[...] = dx
        dxb_ref[...] = (0.5 * dx).astype(BF16)
        part_g = jnp.sum(dy * xh, axis=0, keepdims=True)
        part_l = jnp.broadcast_to(part_loss, (8, LANES))

        @pl.when(i == 0)
        def _():
            dg_ref[...] = part_g
            loss_ref[...] = part_l

        @pl.when(i > 0)
        def _():
            dg_ref[...] += part_g
            loss_ref[...] += part_l

    row = pl.BlockSpec((tt, D), lambda i: (i, 0))
    return _pcall(
        body, name="final_loss", grid=(T // tt,),
        in_specs=[row, row, pl.BlockSpec((1, D), lambda i: (0, 0))],
        out_specs=[row, row, pl.BlockSpec((8, LANES), lambda i: (0, 0)), pl.BlockSpec((1, D), lambda i: (0, 0))],
        out_shape=[jax.ShapeDtypeStruct((T, D), F32), jax.ShapeDtypeStruct((T, D), BF16),
                   jax.ShapeDtypeStruct((8, LANES), F32), jax.ShapeDtypeStruct((1, D), F32)],
    )(x, tgt, g)


def ffn_gate(h, wg3, comm=None):
    T, D = h.shape
    nc, _, fs = wg3.shape
    tm = _tile(T, 512, 8)

    def epilogue(vals, ex, outs):
        outs[0][...] = vals[0].astype(BF16)

    return _mm("ffn_gate", grid=(nc, T // tm),
               pairs=[(h, pl.BlockSpec((tm, D), lambda j, i: (i, 0)),
                       wg3, pl.BlockSpec((None, D, fs), lambda j, i: (j, 0, 0)), NN, 0)],
               out_shape=[jax.ShapeDtypeStruct((T, nc * fs), BF16)],
               out_specs=[pl.BlockSpec((tm, fs), lambda j, i: (i, j))],
               acc_shapes=[(tm, fs)], nk=1, kaxis=None, epilogue=epilogue, comm=comm)[0]


def ffn_upmul(h, wu3, a, comm=None):
    T, D = h.shape
    nc, _, fs = wu3.shape
    tm = _tile(T, 512, 8)

    def epilogue(vals, ex, outs):
        b = vals[0]
        av = ex[0][...].astype(F32)
        outs[0][...] = b.astype(BF16)
        outs[1][...] = (av * _sigmoid(av) * b).astype(BF16)

    t_spec = pl.BlockSpec((tm, fs), lambda j, i: (i, j))
    o_shape = jax.ShapeDtypeStruct((T, nc * fs), BF16)
    return _mm("ffn_upmul", grid=(nc, T // tm),
               pairs=[(h, pl.BlockSpec((tm, D), lambda j, i: (i, 0)),
                       wu3, pl.BlockSpec((None, D, fs), lambda j, i: (j, 0, 0)), NN, 0)],
               extra=[(a, t_spec)], out_shape=[o_shape] * 2, out_specs=[t_spec] * 2,
               acc_shapes=[(tm, fs)], nk=1, kaxis=None, epilogue=epilogue, comm=comm)


def ffn_up(h, wg3, wu3, comm=None):
    T, D = h.shape
    nc, _, fs = wg3.shape
    tm = _tile(T, 512, 8)

    def epilogue(vals, ex, outs):
        a, b = vals
        outs[0][...] = a.astype(BF16)
        outs[1][...] = b.astype(BF16)
        outs[2][...] = (a * _sigmoid(a) * b).astype(BF16)

    h_spec = pl.BlockSpec((tm, D), lambda j, i: (i, 0))
    w_spec = pl.BlockSpec((None, D, fs), lambda j, i: (j, 0, 0))
    o_spec = pl.BlockSpec((tm, fs), lambda j, i: (i, j))
    o_shape = jax.ShapeDtypeStruct((T, nc * fs), BF16)
    return _mm("ffn_up", grid=(nc, T // tm),
               pairs=[(h, h_spec, wg3, w_spec, NN, 0), (h, h_spec, wu3, w_spec, NN, 1)],
               out_shape=[o_shape] * 3, out_specs=[o_spec] * 3, acc_shapes=[(tm, fs)] * 2, nk=1, kaxis=None,
               epilogue=epilogue, comm=comm)


def mm_residual(name, a, b3, res, scale, comm=None):
    T = a.shape[0]
    nk, tk, N = b3.shape
    tm, tn = _tile(T, 512, 8), _tile(N, 2048)

    def epilogue(vals, ex, outs):
        outs[0][...] = ex[0][...] + scale * vals[0]

    return _mm(name, grid=(T // tm, N // tn, nk),
               pairs=[(a, pl.BlockSpec((tm, tk), lambda i, n, k: (i, k)),
                       b3, pl.BlockSpec((None, tk, tn), lambda i, n, k: (k, 0, n)), NN, 0)],
               extra=[(res, pl.BlockSpec((tm, tn), lambda i, n, k: (i, n)))],
               out_shape=[jax.ShapeDtypeStruct((T, N), F32)],
               out_specs=[pl.BlockSpec((tm, tn), lambda i, n, k: (i, n))],
               acc_shapes=[(tm, tn)], nk=nk, kaxis=2, epilogue=epilogue, comm=comm)


def ffn_bwd_mid(dout, wd3, a, b, comm=None):
    T, D = dout.shape
    nc, fs, _ = wd3.shape
    tm = _tile(T, 512, 8)

    def epilogue(vals, ex, outs):
        dm = vals[0]
        av = ex[0][...].astype(F32)
        bv = ex[1][...].astype(F32)
        s = _sigmoid(av)
        outs[0][...] = (dm * bv * (s * (1.0 + av * (1.0 - s)))).astype(BF16)
        outs[1][...] = (dm * (av * s)).astype(BF16)

    t_spec = pl.BlockSpec((tm, fs), lambda j, i: (i, j))
    o_shape = jax.ShapeDtypeStruct((T, nc * fs), BF16)
    return _mm("ffn_bwd_mid", grid=(nc, T // tm),
               pairs=[(dout, pl.BlockSpec((tm, D), lambda j, i: (i, 0)),
                       wd3, pl.BlockSpec((None, fs, D), lambda j, i: (j, 0, 0)), NT, 0)],
               extra=[(a, t_spec), (b, t_spec)],
               out_shape=[o_shape] * 2, out_specs=[t_spec] * 2, acc_shapes=[(tm, fs)], nk=1, kaxis=None,
               epilogue=epilogue, comm=comm)


def dw_rowshard(name, a, b, nc, comm=None):
    T, M = a.shape
    N = b.shape[1]
    ms = M // nc
    tn, tk = _tile(N, 2048 if ms <= 512 else 1024), _tile(T, 1024, 16)

    def epilogue(vals, ex, outs):
        outs[0][...] = vals[0].astype(BF16)

    return _mm(name, grid=(nc, N // tn, T // tk),
               pairs=[(a, pl.BlockSpec((tk, ms), lambda j, n, k: (k, j)),
                       b, pl.BlockSpec((tk, tn), lambda j, n, k: (k, n)), TN, 0)],
               out_shape=[jax.ShapeDtypeStruct((nc, ms, N), BF16)],
               out_specs=[pl.BlockSpec((None, ms, tn), lambda j, n, k: (j, 0, n))],
               acc_shapes=[(ms, tn)], nk=T // tk, kaxis=2, epilogue=epilogue, comm=comm)


def dw_colshard(name, a, bs, nc, comm=None):
    T, M = a.shape
    ns = bs[0].shape[1] // nc
    tm, tk = _tile(M, 512), _tile(T, 1024, 16)

    def epilogue(vals, ex, outs):
        for v, o in zip(vals, outs):
            o[...] = v.astype(BF16)

    a_spec = pl.BlockSpec((tk, tm), lambda j, m, k: (k, m))
    b_spec = pl.BlockSpec((tk, ns), lambda j, m, k: (k, j))
    return _mm(name, grid=(nc, M // tm, T // tk),
               pairs=[(a, a_spec, b, b_spec, TN, p) for p, b in enumerate(bs)],
               out_shape=[jax.ShapeDtypeStruct((nc, M, ns), BF16)] * len(bs),
               out_specs=[pl.BlockSpec((None, tm, ns), lambda j, m, k: (j, m, 0))] * len(bs),
               acc_shapes=[(tm, ns)] * len(bs), nk=T // tk, kaxis=2, epilogue=epilogue, comm=comm)


def ffn_dh(da, db, wg3, wu3, comm=None):
    T = da.shape[0]
    nc, D, fs = wg3.shape
    tm, tn = _tile(T, 512, 8), _tile(D, 1024)

    def epilogue(vals, ex, outs):
        outs[0][...] = vals[0]

    a_spec = pl.BlockSpec((tm, fs), lambda i, n, k: (i, k))
    w_spec = pl.BlockSpec((None, tn, fs), lambda i, n, k: (k, n, 0))
    return _mm("ffn_dh", grid=(T // tm, D // tn, nc),
               pairs=[(da, a_spec, wg3, w_spec, NT, 0), (db, a_spec, wu3, w_spec, NT, 0)],
               out_shape=[jax.ShapeDtypeStruct((T, D), F32)],
               out_specs=[pl.BlockSpec((tm, tn), lambda i, n, k: (i, n))],
               acc_shapes=[(tm, tn)], nk=nc, kaxis=2, epilogue=epilogue, comm=comm)[0]


def proj_main(h, w_t, P, comm=None):
    T, D = h.shape
    tm, tn = _tile(T, 512, 8), _tile(P, 1024)

    def epilogue(vals, ex, outs):
        outs[0][...] = vals[0].astype(BF16)

    return _mm("proj_main", grid=(P // tn, T // tm),
               pairs=[(h, pl.BlockSpec((tm, D), lambda j, i: (i, 0)),
                       w_t, pl.BlockSpec((tn, D), lambda j, i: (j, 0)), NT, 0)],
               out_shape=[jax.ShapeDtypeStruct((T, P), BF16)],
               out_specs=[pl.BlockSpec((tm, tn), lambda j, i: (i, j))],
               acc_shapes=[(tm, tn)], nk=1, kaxis=None, epilogue=epilogue, comm=comm)[0]


def mm_nt_bf16(name, a, w):
    T, K = a.shape
    M = w.shape[0]
    tm, tn = _tile(T, 512, 8), _tile(M, 1024)

    def epilogue(vals, ex, outs):
        outs[0][...] = vals[0].astype(BF16)

    return _mm(name, grid=(T // tm, M // tn),
               pairs=[(a, pl.BlockSpec((tm, K), lambda i, n: (i, 0)),
                       w, pl.BlockSpec((tn, K), lambda i, n: (n, 0)), NT, 0)],
               out_shape=[jax.ShapeDtypeStruct((T, M), BF16)],
               out_specs=[pl.BlockSpec((tm, tn), lambda i, n: (i, n))],
               acc_shapes=[(tm, tn)], nk=1, kaxis=None, epilogue=epilogue)[0]


def proj_dh(dproj, w_t, df, wf_t, comm=None):
    T, P = dproj.shape
    D = w_t.shape[1]
    tm, tn, tk = _tile(T, 512, 8), _tile(D, 2048), _tile(P, 1280)

    def epilogue(vals, ex, outs):
        outs[0][...] = vals[0]

    return _mm("proj_dh", grid=(T // tm, D // tn, P // tk),
               pairs=[(dproj, pl.BlockSpec((tm, tk), lambda i, n, k: (i, k)),
                       w_t, pl.BlockSpec((tk, tn), lambda i, n, k: (k, n)), NN, 0)],
               once_pairs=[(df, pl.BlockSpec((tm, LANES), lambda i, n, k: (i, 0)),
                            wf_t, pl.BlockSpec((LANES, tn), lambda i, n, k: (0, n)), NN, 0)],
               out_shape=[jax.ShapeDtypeStruct((T, D), F32)],
               out_specs=[pl.BlockSpec((tm, tn), lambda i, n, k: (i, n))],
               acc_shapes=[(tm, tn)], nk=P // tk, kaxis=2, epilogue=epilogue, comm=comm)[0]


def proj_dw(dproj, df, h, rows):
    T, P = dproj.shape
    D = h.shape[1]
    tm, tn, tk = _tile(P, 1280), D // 2, _tile(T, 1024, 16)

    def to_bf16(vals, ex, outs):
        outs[0][...] = vals[0].astype(BF16)

    def to_f32(vals, ex, outs):
        outs[0][...] = vals[0]

    main = _mm("proj_dw_main", grid=(P // tm, D // tn, T // tk),
               pairs=[(dproj, pl.BlockSpec((tk, tm), lambda m, n, k: (k, m)),
                       h, pl.BlockSpec((tk, tn), lambda m, n, k: (k, n)), TN, 0)],
               out_shape=[jax.ShapeDtypeStruct((2, rows, tn), BF16)],
               out_specs=[pl.BlockSpec((None, tm, tn), lambda m, n, k: (n, m, 0))],
               acc_shapes=[(tm, tn)], nk=T // tk, kaxis=2, epilogue=to_bf16)[0]
    gate = _mm("proj_dw_f", grid=(1, D // tn, T // tk),
               pairs=[(df, pl.BlockSpec((tk, LANES), lambda m, n, k: (k, 0)),
                       h, pl.BlockSpec((tk, tn), lambda m, n, k: (k, n)), TN, 0)],
               out_shape=[jax.ShapeDtypeStruct((LANES, D), F32)],
               out_specs=[pl.BlockSpec((LANES, tn), lambda m, n, k: (0, n))],
               acc_shapes=[(LANES, tn)], nk=T // tk, kaxis=2, epilogue=to_f32)[0]
    return main, gate


def fgate_fwd(h, wf_t, bias, n_heads):
    T, D = h.shape
    tt = _tile(T, 512, 8)

    def body(h_ref, w_ref, b_ref, f_ref, c_ref, carry):
        i = pl.program_id(0)

        @pl.when(i == 0)
        def _():
            carry[...] = jnp.zeros_like(carry)

        f = lax.dot_general(h_ref[...], w_ref[...], NT, preferred_element_type=F32) + b_ref[...]
        logf = jnp.minimum(f, 0.0) - jnp.log(1.0 + jnp.exp(-jnp.abs(f)))
        tri = (lax.broadcasted_iota(jnp.int32, (tt, tt), 0) >= lax.broadcasted_iota(jnp.int32, (tt, tt), 1))
        cs = jnp.dot(tri.astype(F32), logf, preferred_element_type=F32, precision=lax.Precision.HIGHEST)
        c = cs + carry[...]
        f_ref[...] = f
        c_ref[...] = c
        carry[...] = c[tt - 1:tt, :]

    row = pl.BlockSpec((tt, LANES), lambda i: (i, 0))
    return _pcall(
        body, name="fgate_fwd", grid=(T // tt,),
        in_specs=[pl.BlockSpec((tt, D), lambda i: (i, 0)), pl.BlockSpec((LANES, D), lambda i: (0, 0)),
                  pl.BlockSpec((1, LANES), lambda i: (0, 0))],
        out_specs=[row, row],
        out_shape=[jax.ShapeDtypeStruct((T, LANES), F32)] * 2,
        scratch=[pltpu.VMEM((1, LANES), F32)],
    )(h, wf_t, bias)


def fgate_bwd(dc, f, n_heads):
    T = dc.shape[0]
    tt = _tile(T, 512, 8)
    nt = T // tt

    def body(dc_ref, f_ref, df_ref, db_ref, carry):
        i = pl.program_id(0)

        @pl.when(i == 0)
        def _():
            carry[...] = jnp.zeros_like(carry)

        tri = (lax.broadcasted_iota(jnp.int32, (tt, tt), 1) >= lax.broadcasted_iota(jnp.int32, (tt, tt), 0))
        rs = jnp.dot(tri.astype(F32), dc_ref[...], preferred_element_type=F32,
                     precision=lax.Precision.HIGHEST) + carry[...]
        carry[...] = rs[0:1, :]
        lane = lax.broadcasted_iota(jnp.int32, (tt, LANES), 1)
        df = jnp.where(lane < n_heads, rs * _sigmoid(-f_ref[...]), 0.0)
        df_ref[...] = df.astype(BF16)
        part = jnp.sum(df, axis=0, keepdims=True)

        @pl.when(i == 0)
        def _():
            db_ref[...] = part

        @pl.when(i > 0)
        def _():
            db_ref[...] += part

    rev = pl.BlockSpec((tt, LANES), lambda i: (nt - 1 - i, 0))
    return _pcall(
        body, name="fgate_bwd", grid=(nt,),
        in_specs=[rev, rev],
        out_specs=[rev, pl.BlockSpec((1, LANES), lambda i: (0, 0))],
        out_shape=[jax.ShapeDtypeStruct((T, LANES), BF16), jax.ShapeDtypeStruct((1, LANES), F32)],
        scratch=[pltpu.VMEM((1, LANES), F32)],
    )(dc, f)


SUBLANES = 8
SHIFT_ROWS = HALO - SUBLANES


def _shifted_copies(buf, sh, tt):
    for r in range(1, SUBLANES):
        sh[r - 1, 0:tt + SHIFT_ROWS, :] = buf[pl.ds(r, tt + SHIFT_ROWS), :]


def _tap(buf, sh, offset, tt):
    q, r = divmod(offset, SUBLANES)
    if r == 0:
        return buf[pl.ds(SUBLANES * q, tt), :]
    return sh[r - 1, pl.ds(SUBLANES * q, tt), :]


def conv_fwd(proj, conv_w, conv_b, ln_g, ln_b):
    T = proj.shape[0]
    C = conv_w.shape[1]
    tt = _tile(T, 256, HALO)
    hb = tt // HALO

    def body(a_ref, g_ref, ah_ref, gh_ref, w_ref, cb_ref, lg_ref, lb_ref, ypre_ref, y_ref, ubuf, ush):
        i = pl.program_id(0)
        u = a_ref[...].astype(F32) * _sigmoid(g_ref[...].astype(F32))
        uh = ah_ref[...].astype(F32) * _sigmoid(gh_ref[...].astype(F32))
        ubuf[0:HALO, :] = jnp.where(i == 0, 0.0, uh)
        ubuf[HALO:HALO + tt, :] = u
        _shifted_copies(ubuf, ush, tt)
        acc = jnp.broadcast_to(cb_ref[...], (tt, C))
        for k in range(CONV_K):
            acc = acc + w_ref[k:k + 1, :] * _tap(ubuf, ush, HALO - (CONV_K - 1) + k, tt)
        ypre_ref[...] = acc
        mu = jnp.mean(acc, axis=-1, keepdims=True)
        d = acc - mu
        rstd = lax.rsqrt(jnp.mean(d * d, axis=-1, keepdims=True) + LN_EPS)
        z = d * rstd * lg_ref[...] + lb_ref[...]
        y_ref[...] = (z * _sigmoid(z)).astype(BF16)

    vec = pl.BlockSpec((1, C), lambda i: (0, 0))
    return _pcall(
        body, name="conv_fwd", grid=(T // tt,),
        in_specs=[pl.BlockSpec((tt, C), lambda i: (i, 0)), pl.BlockSpec((tt, C), lambda i: (i, 1)),
                  pl.BlockSpec((HALO, C), lambda i: (jnp.maximum(i * hb - 1, 0), 0)),
                  pl.BlockSpec((HALO, C), lambda i: (jnp.maximum(i * hb - 1, 0), 1)),
                  pl.BlockSpec((HALO, C), lambda i: (0, 0)), vec, vec, vec],
        out_specs=[pl.BlockSpec((tt, C), lambda i: (i, 0))] * 2,
        out_shape=[jax.ShapeDtypeStruct((T, C), F32), jax.ShapeDtypeStruct((T, C), BF16)],
        scratch=[pltpu.VMEM((tt + HALO, C), F32), pltpu.VMEM((SUBLANES - 1, tt + SHIFT_ROWS, C), F32)],
    )(proj, proj, proj, proj, conv_w, conv_b, ln_g, ln_b)


def conv_bwd(proj, ypre, dycat, conv_w, ln_g, ln_b):
    T = proj.shape[0]
    C = conv_w.shape[1]
    tt = _tile(T, 256, HALO)
    hb = tt // HALO
    nt = T // tt
    last_h = T // HALO - 1

    def ln_bwd(ypre_v, dout_v, lg, lb):
        mu = jnp.mean(ypre_v, axis=-1, keepdims=True)
        d = ypre_v - mu
        rstd = lax.rsqrt(jnp.mean(d * d, axis=-1, keepdims=True) + LN_EPS)
        yh = d * rstd
        z = yh * lg + lb
        s = _sigmoid(z)
        dz = dout_v * (s * (1.0 + z * (1.0 - s)))
        dyh = dz * lg
        dy = rstd * (dyh - jnp.mean(dyh, axis=-1, keepdims=True)
                     - yh * jnp.mean(dyh * yh, axis=-1, keepdims=True))
        return dy, dz, yh

    def body(a_ref, g_ref, ah_ref, gh_ref, yp_ref, ypn_ref, do_ref, don_ref, w_ref, lg_ref, lb_ref,
             dag_ref, dw_ref, dcb_ref, dlg_ref, dlb_ref, ubuf, dybuf, ush, dysh):
        i = pl.program_id(0)
        av = a_ref[...].astype(F32)
        sg = _sigmoid(g_ref[...].astype(F32))
        uh = ah_ref[...].astype(F32) * _sigmoid(gh_ref[...].astype(F32))
        ubuf[0:HALO, :] = jnp.where(i == 0, 0.0, uh)
        ubuf[HALO:HALO + tt, :] = av * sg
        lg, lb = lg_ref[...], lb_ref[...]
        dy, dz, yh = ln_bwd(yp_ref[...], do_ref[...].astype(F32), lg, lb)
        dyn, _, _ = ln_bwd(ypn_ref[...], don_ref[...].astype(F32), lg, lb)
        dybuf[0:tt, :] = dy
        dybuf[tt:tt + HALO, :] = jnp.where(i == nt - 1, 0.0, dyn)
        _shifted_copies(ubuf, ush, tt)
        _shifted_copies(dybuf, dysh, tt)

        @pl.when(i == 0)
        def _():
            dw_ref[...] = jnp.zeros_like(dw_ref)
            dcb_ref[...] = jnp.zeros_like(dcb_ref)
            dlg_ref[...] = jnp.zeros_like(dlg_ref)
            dlb_ref[...] = jnp.zeros_like(dlb_ref)

        du = jnp.zeros((tt, C), F32)
        for k in range(CONV_K):
            du = du + w_ref[k:k + 1, :] * _tap(dybuf, dysh, CONV_K - 1 - k, tt)
            dw_ref[k:k + 1, :] += jnp.sum(dy * _tap(ubuf, ush, HALO - (CONV_K - 1) + k, tt), axis=0, keepdims=True)
        dcb_ref[...] += jnp.sum(dy, axis=0, keepdims=True)
        dlg_ref[...] += jnp.sum(dz * yh, axis=0, keepdims=True)
        dlb_ref[...] += jnp.sum(dz, axis=0, keepdims=True)

        dag_ref[:, 0:C] = (du * sg).astype(BF16)
        dag_ref[:, C:2 * C] = (du * av * sg * (1.0 - sg)).astype(BF16)

    vec = pl.BlockSpec((1, C), lambda i: (0, 0))
    prev_h = lambda col: pl.BlockSpec((HALO, C), lambda i: (jnp.maximum(i * hb - 1, 0), col))
    next_h = pl.BlockSpec((HALO, C), lambda i: (jnp.minimum((i + 1) * hb, last_h), 0))
    return _pcall(
        body, name="conv_bwd", grid=(nt,),
        in_specs=[pl.BlockSpec((tt, C), lambda i: (i, 0)), pl.BlockSpec((tt, C), lambda i: (i, 1)),
                  prev_h(0), prev_h(1),
                  pl.BlockSpec((tt, C), lambda i: (i, 0)), next_h,
                  pl.BlockSpec((tt, C), lambda i: (i, 0)), next_h,
                  pl.BlockSpec((HALO, C), lambda i: (0, 0)), vec, vec],
        out_specs=[pl.BlockSpec((tt, 2 * C), lambda i: (i, 0)), pl.BlockSpec((HALO, C), lambda i: (0, 0)),
                   vec, vec, vec],
        out_shape=[jax.ShapeDtypeStruct((T, 2 * C), BF16), jax.ShapeDtypeStruct((HALO, C), F32),
                   jax.ShapeDtypeStruct((1, C), F32), jax.ShapeDtypeStruct((1, C), F32),
                   jax.ShapeDtypeStruct((1, C), F32)],
        scratch=[pltpu.VMEM((tt + HALO, C), F32), pltpu.VMEM((tt + HALO, C), F32),
                 pltpu.VMEM((SUBLANES - 1, tt + SHIFT_ROWS, C), F32),
                 pltpu.VMEM((SUBLANES - 1, tt + SHIFT_ROWS, C), F32)],
    )(proj, proj, proj, proj, ypre, ypre, dycat, dycat, conv_w, ln_g, ln_b)


PAIR = LANES // HEAD_DIM


def _head_masks(rows):
    lane = lax.broadcasted_iota(jnp.int32, (rows, LANES), 1)
    return [jnp.logical_and(lane >= hh * HEAD_DIM, lane < (hh + 1) * HEAD_DIM) for hh in range(PAIR)]


def _causal(tq, tk):
    return lax.broadcasted_iota(jnp.int32, (tq, tk), 0) >= lax.broadcasted_iota(jnp.int32, (tq, tk), 1)


def _lane_column(block, lane_index):
    lane = lax.broadcasted_iota(jnp.int32, block.shape, 1)
    return jnp.sum(jnp.where(lane == lane_index, block, 0.0), axis=-1, keepdims=True)


def attn_fwd(proj, cum, ck4, q_col, comm=None):
    T = proj.shape[0]
    H, nkv, _, tk = ck4.shape
    tq = tk
    hd = H * HEAD_DIM
    qb, kb, vb = q_col // LANES, (q_col + hd) // LANES, (q_col + 2 * hd) // LANES
    scale = 1.0 / math.sqrt(HEAD_DIM)

    def body(q_ref, k_ref, v_ref, cum_ref, ck_ref, o_ref, lse_ref):
        hp = pl.program_id(0)
        i = pl.program_id(1)
        masks = _head_masks(tq)
        q2 = q_ref[...] * scale
        qs = [jnp.where(mk, q2, jnp.zeros_like(q2)) for mk in masks]
        cqs = [_lane_column(cum_ref[...], PAIR * hp + hh) for hh in range(PAIR)]

        def step(j, carry, diagonal, blocks=1):
            scores, values = [], []
            for b in range(blocks):
                off = pl.multiple_of((j + b) * tk, tk)
                kj = k_ref[pl.ds(off, tk), :]
                values.append(v_ref[pl.ds(off, tk), :])
                per_head = []
                for hh in range(PAIR):
                    s = lax.dot_general(qs[hh], kj, NT, preferred_element_type=F32)
                    s = s + cqs[hh] - ck_ref[hh, j + b]
                    if diagonal:
                        s = jnp.where(_causal(tq, tk), s, NEG_INF)
                    per_head.append(s)
                scores.append(per_head)
            out = []
            for hh in range(PAIR):
                m, l, acc = carry[hh]
                m_new = m
                for b in range(blocks):
                    m_new = jnp.maximum(m_new, jnp.max(scores[b][hh], axis=-1, keepdims=True))
                alpha = jnp.exp(m - m_new)
                l, acc = alpha * l, alpha * acc
                for b in range(blocks):
                    p = jnp.exp(scores[b][hh] - m_new)
                    l = l + jnp.sum(p, axis=-1, keepdims=True)
                    acc = acc + jnp.dot(p.astype(BF16), values[b], preferred_element_type=F32)
                out.append((m_new, l, acc))
            return tuple(out)

        init = tuple((jnp.full((tq, 1), -jnp.inf, F32), jnp.zeros((tq, 1), F32), jnp.zeros((tq, LANES), F32))
                     for _ in range(PAIR))
        twos = i // 2
        carry = lax.fori_loop(0, twos, lambda t, c: step(2 * t, c, False, blocks=2), init)
        carry = lax.fori_loop(2 * twos, i, functools.partial(step, diagonal=False), carry)
        carry = step(i, carry, True)
        o = carry[PAIR - 1][2] / carry[PAIR - 1][1]
        for hh in range(PAIR - 1):
            o = jnp.where(masks[hh], carry[hh][2] / carry[hh][1], o)
        o_ref[...] = o
        lse = jnp.broadcast_to(carry[PAIR - 1][0] + jnp.log(carry[PAIR - 1][1]), (tq, LANES))
        for hh in range(PAIR - 1):
            lse = jnp.where(masks[hh], carry[hh][0] + jnp.log(carry[hh][1]), lse)
        lse_ref[...] = lse

    return _pcall(
        body, name="attn_fwd", grid=(H // PAIR, T // tq),
        in_specs=[pl.BlockSpec((tq, LANES), lambda hp, i: (i, qb + hp)),
                  pl.BlockSpec((T, LANES), lambda hp, i: (0, kb + hp)),
                  pl.BlockSpec((T, LANES), lambda hp, i: (0, vb + hp)),
                  pl.BlockSpec((tq, LANES), lambda hp, i: (i, 0)),
                  pl.BlockSpec((PAIR, nkv, 1, tk), lambda hp, i: (hp, 0, 0, 0))],
        out_specs=[pl.BlockSpec((tq, LANES), lambda hp, i: (i, hp)),
                   pl.BlockSpec((None, tq, LANES), lambda hp, i: (hp, i, 0))],
        out_shape=[jax.ShapeDtypeStruct((T, hd), F32), jax.ShapeDtypeStruct((H // PAIR, T, LANES), F32)],
        comm=comm,
    )(proj, proj, proj, cum, ck4)


def attn_bwd(proj, o, dycat, lse, cum, ck4, q_col, do_col, comm=None):
    T = proj.shape[0]
    H, nkv, _, tk = ck4.shape
    tq = tk
    nq = T // tq
    hd = H * HEAD_DIM
    qb, kb, vb = q_col // LANES, (q_col + hd) // LANES, (q_col + 2 * hd) // LANES
    dob = do_col // LANES
    scale = 1.0 / math.sqrt(HEAD_DIM)

    def body(q_ref, k_ref, v_ref, o_ref, do_ref, lse_ref, cum_ref, ck_ref,
             dq_ref, dk_ref, dv_ref, dcq_ref, dck_ref, dk_acc, dv_acc):
        hp = pl.program_id(0)
        j = pl.program_id(1)

        def block(i, diagonal):
            masks = _head_masks(tq)
            rows = pl.ds(pl.multiple_of(i * tq, tq), tq)
            q2, k2, v2, do2 = q_ref[rows, :] * scale, k_ref[...], v_ref[...], do_ref[rows, :]
            zero = jnp.zeros_like(q2)
            prod = do2.astype(F32) * o_ref[rows, :]
            cum_q, lse_q = cum_ref[rows, :], lse_ref[rows, :]
            dq_part = dk_part = dv_part = None
            dcq_part = jnp.zeros((tq, LANES), F32)
            lane = lax.broadcasted_iota(jnp.int32, (tq, LANES), 1)
            for hh in range(PAIR):
                qh = jnp.where(masks[hh], q2, zero)
                kh = jnp.where(masks[hh], k2, zero)
                doh = jnp.where(masks[hh], do2, zero)
                delta = jnp.sum(jnp.where(masks[hh], prod, 0.0), axis=-1, keepdims=True)
                s = lax.dot_general(qh, k2, NT, preferred_element_type=F32)
                s = s + _lane_column(cum_q, PAIR * hp + hh) - ck_ref[hh]
                if diagonal:
                    s = jnp.where(_causal(tq, tk), s, NEG_INF)
                p = jnp.exp(s - _lane_column(lse_q, hh * HEAD_DIM))
                dp = lax.dot_general(doh, v2, NT, preferred_element_type=F32)
                ds = p * (dp - delta)
                dsb = ds.astype(BF16)
                dv_h = lax.dot_general(p.astype(BF16), doh, TN, preferred_element_type=F32)
                dk_h = lax.dot_general(dsb, qh, TN, preferred_element_type=F32)
                dq_h = jnp.dot(dsb, kh, preferred_element_type=F32)
                dq_part = dq_h if dq_part is None else dq_part + dq_h
                dk_part = dk_h if dk_part is None else dk_part + dk_h
                dv_part = dv_h if dv_part is None else dv_part + dv_h
                dck_h = -jnp.sum(ds, axis=0, keepdims=True)
                dcq_part = jnp.where(lane == PAIR * hp + hh, jnp.sum(ds, axis=-1, keepdims=True), dcq_part)
                if diagonal:
                    dck_ref[hh] = dck_h
                else:
                    dck_ref[hh] += dck_h
            dq_part = dq_part * scale

            @pl.when(j == 0)
            def _():
                dq_ref[rows, :] = dq_part

            @pl.when(j > 0)
            def _():
                dq_ref[rows, :] += dq_part

            @pl.when(jnp.logical_and(hp == 0, j == 0))
            def _():
                dcq_ref[rows, :] = dcq_part

            @pl.when(jnp.logical_or(hp > 0, j > 0))
            def _():
                dcq_ref[rows, :] += dcq_part

            if diagonal:
                dk_acc[...] = dk_part
                dv_acc[...] = dv_part
            else:
                dk_acc[...] += dk_part
                dv_acc[...] += dv_part

        block(j, True)

        def later(i, carry):
            block(i, False)
            return carry

        lax.fori_loop(j + 1, nq, later, 0)
        dk_ref[...] = dk_acc[...].astype(BF16)
        dv_ref[...] = dv_acc[...].astype(BF16)

    at_q = lambda col: pl.BlockSpec((T, LANES), lambda hp, j: (0, col + hp))
    at_k = lambda col: pl.BlockSpec((tk, LANES), lambda hp, j: (j, col + hp))
    lse_spec = pl.BlockSpec((None, T, LANES), lambda hp, j: (hp, 0, 0))
    cum_spec = pl.BlockSpec((T, LANES), lambda hp, j: (0, 0))
    ck_spec = pl.BlockSpec((PAIR, None, 1, tk), lambda hp, j: (hp, j, 0, 0))
    return _pcall(
        body, name="attn_bwd", grid=(H // PAIR, nkv),
        in_specs=[at_q(qb), at_k(kb), at_k(vb), at_q(0), at_q(dob), lse_spec, cum_spec, ck_spec],
        out_specs=[pl.BlockSpec((T, LANES), lambda hp, j: (0, hp)), at_k(0), at_k(0),
                   pl.BlockSpec((T, LANES), lambda hp, j: (0, 0)), ck_spec],
        out_shape=[jax.ShapeDtypeStruct((T, hd), F32), jax.ShapeDtypeStruct((T, hd), BF16),
                   jax.ShapeDtypeStruct((T, hd), BF16),
                   jax.ShapeDtypeStruct((T, LANES), F32), jax.ShapeDtypeStruct((H, nkv, 1, tk), F32)],
        scratch=[pltpu.VMEM((tk, LANES), F32), pltpu.VMEM((tk, LANES), F32)],
        comm=comm,
    )(proj, proj, proj, o, dycat, lse, cum, ck4)


ELEMENTWISE_BLOCK_BYTES = 2 * 1024 * 1024
BF16_ROWS = 16


def cast_bf16(arrays, comm=None):
    def slab(a, steps):
        R, C = a.shape
        if R % (steps * BF16_ROWS) == 0:
            return pl.BlockSpec((R // steps, C), lambda i: (i, 0))
        if C % (steps * LANES) == 0:
            return pl.BlockSpec((R, C // steps), lambda i: (0, i))
        return None

    steps = 8 if all(slab(a, 8) is not None for a in arrays) else 4
    specs = [slab(a, steps) for a in arrays]
    n = len(arrays)

    def body(*refs):
        for src, dst in zip(refs[:n], refs[n:]):
            dst[...] = src[...].astype(BF16)

    return _pcall(body, name="cast_bf16", grid=(steps,), in_specs=specs, out_specs=specs,
                  out_shape=[jax.ShapeDtypeStruct(a.shape, BF16) for a in arrays], comm=comm)(*arrays)


def _ew_tiles(rows, cols, bytes_per_element):
    target = max(8, ELEMENTWISE_BLOCK_BYTES // max(1, cols * bytes_per_element))
    if rows <= target:
        return rows, cols
    t = (target // 16) * 16
    while t >= 16:
        if rows % t == 0:
            return t, cols
        t -= 16
    tc = _tile(cols, max(LANES, (ELEMENTWISE_BLOCK_BYTES // (rows * bytes_per_element)) // LANES * LANES))
    return rows, tc


def sum_chips(recv):
    nc, R, C = recv.shape
    tr, tc = _ew_tiles(R, C, 4)

    def body(r_ref, o_ref):
        acc = r_ref[0].astype(F32)
        for j in range(1, nc):
            acc = acc + r_ref[j].astype(F32)
        o_ref[...] = acc

    return _pcall(
        body, name="sum_chips", grid=(R // tr, C // tc),
        in_specs=[pl.BlockSpec((nc, tr, tc), lambda i, j: (0, i, j))],
        out_specs=[pl.BlockSpec((tr, tc), lambda i, j: (i, j))],
        out_shape=[jax.ShapeDtypeStruct((R, C), F32)],
    )(recv)[0]


def add_sibling_half(g, recv):
    nc, R, C = g.shape
    hr = R // 2
    tr, tc = _ew_tiles(hr // 2, C, 4 * nc)
    per_quarter = (hr // 2) // tr

    def body(g_ref, r_ref, o_ref):
        c = lax.axis_index("c")
        for j in range(nc):
            o_ref[j] = (g_ref[j, c].astype(F32) + r_ref[j].astype(F32)).astype(BF16)

    return _pcall(
        body, name="add_sibling_half", grid=(hr // tr, C // tc),
        in_specs=[pl.BlockSpec((nc, 2, tr, tc), lambda i, j: (0, 0, i, j)),
                  pl.BlockSpec((nc, tr, tc), lambda i, j: (0, i, j))],
        out_specs=[pl.BlockSpec((None, nc, tr, tc), lambda i, j: (i // per_quarter, 0, i % per_quarter, j))],
        out_shape=[jax.ShapeDtypeStruct((2, nc, hr // 2, C), BF16)],
    )(g.reshape(nc, 2, hr, C), recv)[0]


def adamw(w, m, v, g_parts, comm=None, halves=False):
    R, C = w.shape
    tr, tc = _ew_tiles(R // 2 if halves else R, C, 4 * 4)
    n_g = len(g_parts)
    n_half = (R // 2) // tr
    c1 = 1.0 - ADAM_B1
    c2 = 1.0 - ADAM_B2
    bc1 = 1.0 - ADAM_B1 ** ADAM_STEP
    bc2 = 1.0 - ADAM_B2 ** ADAM_STEP

    def body(*refs):
        w_ref, m_ref, v_ref = refs[:3]
        g_refs = refs[3:3 + n_g]
        g_out, d_out, m_out, v_out = refs[3 + n_g:]
        if halves:
            mine = (pl.program_id(0) >= n_half) == (lax.axis_index("c") == 1)
            g = jnp.where(mine, g_refs[0][...], g_refs[1][...])
        else:
            g = g_refs[0][...]
            for r in g_refs[1:]:
                g = g + r[...]
        m_new = ADAM_B1 * m_ref[...] + c1 * g
        v_new = ADAM_B2 * v_ref[...] + c2 * (g * g)
        m_hat = m_new / bc1
        v_hat = v_new / bc2
        g_out[...] = g
        d_out[...] = -ADAM_LR * (m_hat / (jnp.sqrt(v_hat) + ADAM_EPS) + ADAM_WD * w_ref[...])
        m_out[...] = m_new
        v_out[...] = v_new

    spec = pl.BlockSpec((tr, tc), lambda i, j: (i, j))
    g_spec = pl.BlockSpec((tr, tc), lambda i, j: (i % n_half, j)) if halves else spec
    return _pcall(
        body, name="adamw", grid=(R // tr, C // tc),
        in_specs=[spec] * 3 + [g_spec] * n_g, out_specs=[spec] * 4,
        out_shape=[jax.ShapeDtypeStruct((R, C), F32)] * 4, comm=comm,
    )(w, m, v, *g_parts)


def _chip_coords():
    x, y, c = lax.axis_index("x"), lax.axis_index("y"), lax.axis_index("c")
    others = [(1 - x, y), (x, 1 - y), (1 - x, 1 - y)]
    return x, y, c, others


def _remote(src, dst, send_sem, recv_sem, device):
    return pltpu.make_async_remote_copy(src_ref=src, dst_ref=dst, send_sem=send_sem, recv_sem=recv_sem,
                                        device_id=device, device_id_type=MESH)


def gather_comm(shards):
    n = len(shards)
    SLOTS = 7

    def makers(ins, outs, sems):
        send_sems, recv_sems, local_sems = sems
        x, y, c, _ = _chip_coords()
        me, xn, yn, dg = 2 * x + y, 2 * (1 - x) + y, 2 * x + (1 - y), 2 * (1 - x) + (1 - y)
        to_x, to_y, sibling = (1 - x, y, c), (x, 1 - y, c), (x, y, 1 - c)

        def part(ref, a, half, quarter=None, chip=None):
            rows, cols = ins[a].shape[0], ins[a].shape[1]
            lead = () if chip is None else (chip,)
            along_rows = rows % (4 * BF16_ROWS) == 0 or (ins[a].dtype == F32 and rows % (4 * SUBLANES) == 0)
            size = (rows if along_rows else cols) // 2
            start = half * size
            if quarter is not None:
                size = size // 2
                start = start + quarter * size
            if along_rows:
                return ref.at[(*lead, pl.ds(start, size))]
            return ref.at[(*lead, slice(None), pl.ds(start, size))]

        def copy(a, k, src, dst, device):
            return _remote(src, dst, send_sems.at[SLOTS * a + k], recv_sems.at[SLOTS * a + k], device)

        def local(a):
            return pltpu.make_async_copy(ins[a], outs[a].at[me], local_sems.at[a])

        def first_leg(a):
            mine = part(outs[a], a, c, chip=me)
            return [copy(a, 0, part(ins[a], a, c), mine, to_x), copy(a, 1, part(ins[a], a, c), mine, to_y)]

        def arrived(a, k):
            region = {0: part(outs[a], a, c, chip=xn), 1: part(outs[a], a, c, chip=yn),
                      2: part(outs[a], a, c, 0, chip=dg), 3: part(outs[a], a, c, 1, chip=dg),
                      4: part(outs[a], a, 1 - c, chip=xn), 5: part(outs[a], a, 1 - c, chip=yn),
                      6: part(outs[a], a, 1 - c, chip=dg)}[k]
            return copy(a, k, region, region, sibling if k >= 4 else (to_x if k in (0, 3) else to_y))

        def relays(a):
            qx, qy = part(outs[a], a, c, 0, chip=xn), part(outs[a], a, c, 1, chip=yn)
            return [copy(a, 2, qx, qx, to_y), copy(a, 3, qy, qy, to_x)]

        def handover(a, k):
            region = part(outs[a], a, c, chip={4: xn, 5: yn, 6: dg}[k])
            return copy(a, k, region, region, sibling)

        return local, first_leg, arrived, relays, handover

    def start(ins, outs, sems):
        local, first_leg, _, _, _ = makers(ins, outs, sems)
        for a in range(n):
            for cp in first_leg(a):
                cp.start()
        for a in range(n):
            local(a).start()

    def relay(ins, outs, sems):
        _, _, arrived, relays, handover = makers(ins, outs, sems)
        for a in range(n):
            to_y_nbr, to_x_nbr = relays(a)
            arrived(a, 0).wait_recv()
            to_y_nbr.start()
            handover(a, 4).start()
            arrived(a, 1).wait_recv()
            to_x_nbr.start()
            handover(a, 5).start()

    def finish(ins, outs, sems):
        local, first_leg, arrived, relays, handover = makers(ins, outs, sems)
        for a in range(n):
            arrived(a, 2).wait_recv()
            arrived(a, 3).wait_recv()
            handover(a, 6).start()
        for a in range(n):
            for k in (4, 5, 6):
                arrived(a, k).wait_recv()
        for a in range(n):
            for cp in first_leg(a) + relays(a) + [handover(a, k) for k in (4, 5, 6)]:
                cp.wait_send()
            local(a).wait()

    return Comm(shards, [jax.ShapeDtypeStruct((N_CHIP,) + s.shape, s.dtype) for s in shards],
                [pltpu.SemaphoreType.DMA((SLOTS * n,)), pltpu.SemaphoreType.DMA((SLOTS * n,)),
                 pltpu.SemaphoreType.DMA((n,))], start, finish, relay)


def scatter_comm(grads):
    n = len(grads)
    pieces = [(a, jj) for a in range(n) for jj in range(3)]

    def makers(ins, outs, sems):
        send_sems, recv_sems, local_sems = sems
        x, y, c, others = _chip_coords()
        me = 2 * x + y

        def local(a):
            return pltpu.make_async_copy(ins[a].at[me], outs[a].at[me], local_sems.at[a])

        def ici(a, jj):
            ox, oy = others[jj]
            return _remote(ins[a].at[2 * ox + oy], outs[a].at[me], send_sems.at[3 * a + jj],
                           recv_sems.at[3 * a + jj], (ox, oy, c))

        def landed(a, jj):
            ox, oy = others[jj]
            slot = outs[a].at[2 * ox + oy]
            return _remote(slot, slot, send_sems.at[3 * a + jj], recv_sems.at[3 * a + jj], (ox, oy, c))

        return local, ici, landed

    def start(ins, outs, sems):
        local, ici, _ = makers(ins, outs, sems)
        for a in range(n):
            for jj in (2, 0, 1):
                ici(a, jj).start()
        for a in range(n):
            local(a).start()

    def finish(ins, outs, sems):
        local, ici, landed = makers(ins, outs, sems)
        for a, jj in pieces:
            landed(a, jj).wait_recv()
        for a, jj in pieces:
            ici(a, jj).wait_send()
        for a in range(n):
            local(a).wait()

    return Comm(grads, [jax.ShapeDtypeStruct(g.shape, g.dtype) for g in grads],
                [pltpu.SemaphoreType.DMA((3 * n,)), pltpu.SemaphoreType.DMA((3 * n,)),
                 pltpu.SemaphoreType.DMA((n,))], start, finish)


def halfswap_comm(grads):
    n = len(grads)

    def copies(ins, outs, sems):
        send_sems, recv_sems = sems
        x, y, c, _ = _chip_coords()
        out = []
        for a in range(n):
            hr = ins[a].shape[1] // 2
            out.append(_remote(ins[a].at[:, pl.ds((1 - c) * hr, hr)], outs[a], send_sems.at[a], recv_sems.at[a],
                               (x, y, 1 - c)))
        return out

    def start(ins, outs, sems):
        for cp in copies(ins, outs, sems):
            cp.start()

    def finish(ins, outs, sems):
        for cp in copies(ins, outs, sems):
            cp.wait()

    return Comm(grads, [jax.ShapeDtypeStruct((g.shape[0], g.shape[1] // 2, g.shape[2]), g.dtype) for g in grads],
                [pltpu.SemaphoreType.DMA((n,)), pltpu.SemaphoreType.DMA((n,))], start, finish)


def join_comms(first, second):
    ni, no, ns = len(first.operands), len(first.out_shape), len(first.sems)

    def start(ins, outs, sems):
        first.start(ins[:ni], outs[:no], sems[:ns])
        second.start(ins[ni:], outs[no:], sems[ns:])

    def finish(ins, outs, sems):
        first.finish(ins[:ni], outs[:no], sems[:ns])
        second.finish(ins[ni:], outs[no:], sems[ns:])

    def relay(ins, outs, sems):
        if first.relay is not None:
            first.relay(ins[:ni], outs[:no], sems[:ns])
        if second.relay is not None:
            second.relay(ins[ni:], outs[no:], sems[ns:])

    return Comm(first.operands + second.operands, first.out_shape + second.out_shape, first.sems + second.sems,
                start, finish, relay if (first.relay or second.relay) else None)


def swap_comm(parts):
    n = len(parts)

    def copies(ins, outs, sems):
        send_sems, recv_sems = sems
        x, y, c, _ = _chip_coords()
        return [_remote(ins[a], outs[a], send_sems.at[a], recv_sems.at[a], (x, y, 1 - c)) for a in range(n)]

    def start(ins, outs, sems):
        for cp in copies(ins, outs, sems):
            cp.start()

    def finish(ins, outs, sems):
        for cp in copies(ins, outs, sems):
            cp.wait()

    return Comm(parts, [jax.ShapeDtypeStruct(p.shape, p.dtype) for p in parts],
                [pltpu.SemaphoreType.DMA((n,)), pltpu.SemaphoreType.DMA((n,))], start, finish)


def allreduce_small(v):
    R = v.shape[0]

    def body(v_ref, sum_ref, all_ref, send_sems, recv_sems):
        x, y, c = lax.axis_index("x"), lax.axis_index("y"), lax.axis_index("c")
        me = 4 * x + 2 * y + c
        all_ref[me] = v_ref[...]
        copies = []
        for k in range(1, N_DEV):
            px = 1 - x if k & 4 else x
            py = 1 - y if k & 2 else y
            pc = 1 - c if k & 1 else c
            cp = pltpu.make_async_remote_copy(
                src_ref=v_ref, dst_ref=all_ref.at[me], send_sem=send_sems.at[k - 1], recv_sem=recv_sems.at[k - 1],
                device_id=(px, py, pc), device_id_type=MESH)
            cp.start()
            copies.append((cp, 4 * px + 2 * py + pc))
        for k, (cp, peer) in enumerate(copies):
            pltpu.make_async_remote_copy(
                src_ref=v_ref, dst_ref=all_ref.at[peer], send_sem=send_sems.at[k], recv_sem=recv_sems.at[k],
                device_id=(x, y, c), device_id_type=MESH).wait_recv()
        for cp, _ in copies:
            cp.wait_send()
        acc = all_ref[0]
        for d in range(1, N_DEV):
            acc = acc + all_ref[d]
        sum_ref[...] = acc

    vm = pl.BlockSpec(memory_space=pltpu.VMEM)
    return pl.pallas_call(
        body, name="allreduce_small",
        in_specs=[vm], out_specs=[vm, vm],
        out_shape=[jax.ShapeDtypeStruct((R, LANES), F32), jax.ShapeDtypeStruct((N_DEV, R, LANES), F32)],
        scratch_shapes=[pltpu.SemaphoreType.DMA((N_DEV - 1,)), pltpu.SemaphoreType.DMA((N_DEV - 1,))],
    )(v)[0]


SMALL_NAMES = ("ffn1_norm", "mix_norm", "ffn2_norm", "final_norm", "conv_b", "conv_ln_g", "conv_ln_b")


def _pack_small(vecs, bias, conv_w_rows, loss_tile):
    rows = [vecs[n].reshape(-1, LANES) for n in SMALL_NAMES]
    rows.append(bias.reshape(1, LANES))
    rows.append(conv_w_rows.reshape(-1, LANES))
    rows.append(loss_tile[0:1, :])
    packed = jnp.concatenate(rows, axis=0)
    pad = (-packed.shape[0]) % 8
    return jnp.pad(packed, ((0, pad), (0, 0)))


def _unpack_small(packed, sizes, n_conv_rows):
    out, r = {}, 0
    for n in SMALL_NAMES:
        k = sizes[n] // LANES
        out[n] = packed[r:r + k].reshape(-1)
        r += k
    out["fgate_bias"] = packed[r]
    r += 1
    out["conv_w"] = packed[r:r + n_conv_rows]
    r += n_conv_rows
    out["loss"] = packed[r, 0]
    return out


def kernel(x, ffn1_norm, ffn1_w_gate, ffn1_w_up, ffn1_w_down, mix_norm, w_in, fgate_bias, conv_w, conv_b, conv_ln_g, conv_ln_b, w_out, ffn2_norm, ffn2_w_gate, ffn2_w_up, ffn2_w_down, final_norm, loss_target, m_ffn1_norm, m_ffn1_w_gate, m_ffn1_w_up, m_ffn1_w_down, m_mix_norm, m_w_in, m_fgate_bias, m_conv_w, m_conv_b, m_conv_ln_g, m_conv_ln_b, m_w_out, m_ffn2_norm, m_ffn2_w_gate, m_ffn2_w_up, m_ffn2_w_down, m_final_norm, v_ffn1_norm, v_ffn1_w_gate, v_ffn1_w_up, v_ffn1_w_down, v_mix_norm, v_w_in, v_fgate_bias, v_conv_w, v_conv_b, v_conv_ln_g, v_conv_ln_b, v_w_out, v_ffn2_norm, v_ffn2_w_gate, v_ffn2_w_up, v_ffn2_w_down, v_final_norm):
    w = dict(ffn1_norm=ffn1_norm, ffn1_w_gate=ffn1_w_gate, ffn1_w_up=ffn1_w_up, ffn1_w_down=ffn1_w_down,
             mix_norm=mix_norm, w_in=w_in, fgate_bias=fgate_bias, conv_w=conv_w, conv_b=conv_b,
             conv_ln_g=conv_ln_g, conv_ln_b=conv_ln_b, w_out=w_out, ffn2_norm=ffn2_norm,
             ffn2_w_gate=ffn2_w_gate, ffn2_w_up=ffn2_w_up, ffn2_w_down=ffn2_w_down, final_norm=final_norm)
    m = dict(ffn1_norm=m_ffn1_norm, ffn1_w_gate=m_ffn1_w_gate, ffn1_w_up=m_ffn1_w_up, ffn1_w_down=m_ffn1_w_down,
             mix_norm=m_mix_norm, w_in=m_w_in, fgate_bias=m_fgate_bias, conv_w=m_conv_w, conv_b=m_conv_b,
             conv_ln_g=m_conv_ln_g, conv_ln_b=m_conv_ln_b, w_out=m_w_out, ffn2_norm=m_ffn2_norm,
             ffn2_w_gate=m_ffn2_w_gate, ffn2_w_up=m_ffn2_w_up, ffn2_w_down=m_ffn2_w_down, final_norm=m_final_norm)
    v = dict(ffn1_norm=v_ffn1_norm, ffn1_w_gate=v_ffn1_w_gate, ffn1_w_up=v_ffn1_w_up, ffn1_w_down=v_ffn1_w_down,
             mix_norm=v_mix_norm, w_in=v_w_in, fgate_bias=v_fgate_bias, conv_w=v_conv_w, conv_b=v_conv_b,
             conv_ln_g=v_conv_ln_g, conv_ln_b=v_conv_ln_b, w_out=v_w_out, ffn2_norm=v_ffn2_norm,
             ffn2_w_gate=v_ffn2_w_gate, ffn2_w_up=v_ffn2_w_up, ffn2_w_down=v_ffn2_w_down, final_norm=v_final_norm)
    names = list(w.keys())
    big = ("ffn1_w_gate", "ffn1_w_up", "ffn1_w_down", "w_in", "w_out", "ffn2_w_gate", "ffn2_w_up", "ffn2_w_down")

    T, D = x.shape[1], x.shape[2]
    C = conv_b.shape[0]
    H = fgate_bias.shape[0]
    cs = conv_w.shape[1]
    in_cols = N_CHIP * w_in.shape[1]
    p_main = in_cols - H

    x0, tgt = x[0], loss_target[0]
    tk = _tile(T, 512, 128)
    nkv = T // tk
    row = lambda a: a.reshape(1, -1)
    grad, delta, new_m, new_v = {}, {}, {}, {}

    def update(n, parts, comm=None, halves=False):
        args = (w[n], m[n], v[n])
        if n == "w_in":
            outs = [t.T for t in adamw(*[a.T for a in args], parts, comm=comm)]
        else:
            outs = adamw(*args, parts, comm=comm, halves=halves)
        grad[n], delta[n], new_m[n], new_v[n] = outs

    rest = [n for n in big if n != "ffn1_w_gate"]
    g0 = gather_comm([w["ffn1_w_gate"].astype(BF16), jnp.pad(conv_w, ((0, HALO - CONV_K), (0, 0)))])
    wb = dict(zip(rest, cast_bf16([w[n].T if n == "w_in" else w[n] for n in rest], comm=g0)))
    wg1, conv_w4 = g0.results
    conv_w_full = conv_w4.transpose(1, 0, 2).reshape(HALO, C)
    h1, r1 = rms_fwd(x0, row(ffn1_norm))
    g1a = gather_comm([wb["ffn1_w_up"]])
    a1 = ffn_gate(h1, wg1, comm=g1a)
    wu1 = g1a.results[0]
    g1b = gather_comm([wb["ffn1_w_down"]])
    b1, mid1 = ffn_upmul(h1, wu1, a1, comm=g1b)
    wd1 = g1b.results[0]
    g2 = gather_comm([wb["w_in"]])
    x1 = mm_residual("ffn_down_g", mid1, wd1, x0, 0.5, comm=g2)[0]
    w_t = g2.results[0].reshape(in_cols, D)

    wf_t = jnp.pad(w_t[p_main:], ((0, LANES - H), (0, 0)))
    bias_pad = jnp.pad(row(fgate_bias), ((0, 0), (0, LANES - H)))
    h2, r2 = rms_fwd(x1, row(mix_norm))
    g_out = gather_comm([wb["w_out"]])
    proj = proj_main(h2, w_t, p_main, comm=g_out)
    w_out3 = g_out.results[0]
    f, cum = fgate_fwd(h2, wf_t, bias_pad, H)
    ypre, yconv = conv_fwd(proj, conv_w_full, row(conv_b), row(conv_ln_g), row(conv_ln_b))
    ck4 = cum[:, :H].T.reshape(H, nkv, 1, tk)
    g3 = gather_comm([wb["ffn2_w_gate"], wb["ffn2_w_up"]])
    o, lse = attn_fwd(proj, cum, ck4, 2 * C, comm=g3)
    wg2, wu2 = g3.results
    ycat = jnp.concatenate([yconv, o.astype(BF16)], axis=1)
    x2 = mm_residual("out_proj", ycat, w_out3.reshape(2, -1, D), x1, 1.0)[0]

    h3, r3 = rms_fwd(x2, row(ffn2_norm))
    g4 = gather_comm([wb["ffn2_w_down"]])
    a2, b2, mid2 = ffn_up(h3, wg2, wu2, comm=g4)
    wd2 = g4.results[0]
    x3 = mm_residual("ffn_down", mid2, wd2, x2, 0.5)[0]
    dx3, dx3b, loss_tile, d_final = final_loss(x3, tgt, row(final_norm))

    da2, db2 = ffn_bwd_mid(dx3b, wd2, a2, b2)
    dwd2 = dw_rowshard("ffn_dwd", mid2, dx3b, N_CHIP)[0]
    s1 = scatter_comm([dwd2])
    dwg2, dwu2 = dw_colshard("ffn_dwgu_s", h3, [da2, db2], N_CHIP, comm=s1)
    s2 = scatter_comm([dwg2])
    dh3 = ffn_dh(da2, db2, wg2, wu2, comm=s2)
    dx2, dx2b, d_ffn2_norm = rms_bwd(dh3, x2, r3, row(ffn2_norm), dx3, 1.0)

    dycat = mm_nt_bf16("out_proj_dy", dx2b, w_out3.reshape(-1, D))
    dw_out3 = dw_rowshard("out_proj_dw", ycat, dx2b, N_CHIP)[0]
    s3 = scatter_comm([dwu2, dw_out3])
    dq, dk, dv, dcq, dck4 = attn_bwd(proj, o, dycat, lse, cum, ck4, 2 * C, C, comm=s3)
    dc = dcq + jnp.pad(dck4.reshape(H, T).T, ((0, 0), (0, LANES - H)))
    df, d_bias = fgate_bwd(dc, f, H)
    dag, d_conv_w, d_conv_b, d_ln_g, d_ln_b = conv_bwd(proj, ypre, dycat, conv_w_full, row(conv_ln_g),
                                                       row(conv_ln_b))
    dproj = jnp.concatenate([dag, dq.astype(BF16), dk, dv], axis=1)
    early = ("ffn2_w_down", "ffn2_w_gate", "ffn2_w_up", "w_out")
    early_sums = [sum_chips(r) for r in (s1.results[0], s2.results[0], s3.results[0], s3.results[1])]
    sw1 = swap_comm(early_sums)
    dh2 = proj_dh(dproj, w_t, df, wf_t, comm=sw1)
    dw_t, dwf_t = proj_dw(dproj, df, h2, in_cols)
    gate_rows = dwf_t[:H].astype(BF16).reshape(H, 2, D // 2).transpose(1, 0, 2)
    dw_t = lax.dynamic_update_slice(dw_t, gate_rows, (0, p_main, 0))
    dw_in_halves = [dw_t[half].reshape(N_CHIP, in_cols // N_CHIP, D // 2) for half in range(2)]
    dx1, dx1b, d_mix_norm = rms_bwd(dh2, x1, r2, row(mix_norm), dx2, 0.5)

    s4a = scatter_comm([dw_in_halves[0]])
    da1, db1 = ffn_bwd_mid(dx1b, wd1, a1, b1, comm=s4a)
    s4b = scatter_comm([dw_in_halves[1]])
    dwd1 = dw_rowshard("ffn_dwd_s", mid1, dx1b, N_CHIP, comm=s4b)[0]
    s5 = scatter_comm([dwd1])
    dwg1, dwu1 = dw_colshard("ffn_dwgu_s", h1, [da1, db1], N_CHIP, comm=s5)
    sum_in = jnp.concatenate([sum_chips(s4a.results[0]), sum_chips(s4b.results[0])], axis=1)
    mid_sums = [sum_in, sum_chips(s5.results[0])]
    s6 = join_comms(join_comms(scatter_comm([dwg1]), halfswap_comm([dwu1])), swap_comm(mid_sums))
    dh1 = ffn_dh(da1, db1, wg1, wu1, comm=s6)
    recv_g1, sibling_u1, their_in, their_d1 = s6.results
    quarters = add_sibling_half(dwu1, sibling_u1)
    s7 = [scatter_comm([quarters[0]]), scatter_comm([quarters[1]])]
    grad_x, _, d_ffn1_norm = rms_bwd(dh1, x0, r1, row(ffn1_norm), dx1, 1.0, comm=s7[0])

    for i, (n, mine, other) in enumerate(zip(early, early_sums, sw1.results)):
        update(n, [mine, other], comm=s7[1] if i == 0 else None)
    update("w_in", [mid_sums[0], their_in])
    update("ffn1_w_down", [mid_sums[1], their_d1])
    sum_g1 = sum_chips(recv_g1)
    half_u1 = jnp.concatenate([sum_chips(s7[0].results[0]), sum_chips(s7[1].results[0])], axis=0)
    their_g1, their_u1 = _run_comm("swap_last", swap_comm([sum_g1, half_u1]))
    update("ffn1_w_gate", [sum_g1, their_g1])
    update("ffn1_w_up", [half_u1, their_u1], halves=True)

    gl = dict(ffn1_norm=d_ffn1_norm, mix_norm=d_mix_norm, ffn2_norm=d_ffn2_norm, final_norm=d_final,
              conv_b=d_conv_b, conv_ln_g=d_ln_g, conv_ln_b=d_ln_b)
    small_sizes = {n: w[n].shape[0] for n in SMALL_NAMES}
    packed = _pack_small(gl, d_bias, d_conv_w, loss_tile)
    red = _unpack_small(allreduce_small(packed), small_sizes, HALO * C // LANES)
    loss = red["loss"]
    my_chip = 2 * lax.axis_index("x") + lax.axis_index("y")
    g_conv_w = lax.dynamic_slice_in_dim(red["conv_w"].reshape(HALO, C)[:CONV_K], my_chip * cs, cs, axis=1)
    update("conv_w", [g_conv_w])
    vec_names = SMALL_NAMES + ("fgate_bias",)
    stack = lambda d: jnp.concatenate(
        [jnp.pad(d[n], (0, (-d[n].shape[0]) % LANES)).reshape(-1, LANES) for n in vec_names], axis=0)
    g_stack = jnp.concatenate([red[n].reshape(-1, LANES) for n in SMALL_NAMES] + [red["fgate_bias"][None, :]],
                              axis=0)
    outs = adamw(stack(w), stack(m), stack(v), [g_stack])
    r = 0
    for n in vec_names:
        size = w[n].shape[0]
        k = -(-size // LANES)
        for dst, src in zip((grad, delta, new_m, new_v), outs):
            dst[n] = src[r:r + k].reshape(-1)[:size]
        r += k

    return (loss, grad_x[None], *[grad[n] for n in names], *[delta[n] for n in names],
            *[new_m[n] for n in names], *[new_v[n] for n in names])
```

```python
import functools
import math

import jax
import jax.numpy as jnp
from jax import lax
from jax.experimental import pallas as pl
from jax.experimental.pallas import tpu as pltpu

F32 = jnp.float32
BF16 = jnp.bfloat16
NORM_EPS = 1e-6
LN_EPS = 1e-5
NEG_INF = -1e30
HEAD_DIM = 64
CONV_K = 31
HALO = 32
LANES = 128
N_CHIP = 4
N_DEV = 8
VMEM_LIMIT = 52 * 1024 * 1024
MESH = pl.DeviceIdType.MESH

ADAM_LR = 0.001
ADAM_B1 = 0.9
ADAM_B2 = 0.999
ADAM_EPS = 1e-08
ADAM_WD = 0.01
ADAM_STEP = 10

NN = (((1,), (0,)), ((), ()))
NT = (((1,), (1,)), ((), ()))
TN = (((0,), (0,)), ((), ()))


def _tile(n, pref, unit=128):
    if n <= pref:
        return n
    t = (pref // unit) * unit
    while t > 0:
        if n % t == 0:
            return t
        t -= unit
    raise ValueError(f"no tile for {n} under {pref}")


RELAY_AT = (3, 4)


class Comm:
    def __init__(self, operands, out_shape, sems, start, finish, relay=None):
        self.operands, self.out_shape, self.sems = list(operands), list(out_shape), list(sems)
        self.start, self.finish, self.relay = start, finish, relay
        self.results = None


def _pcall(body, *, name, grid, in_specs, out_specs, out_shape, scratch=(), comm=None):
    params = pltpu.CompilerParams(dimension_semantics=("arbitrary",) * len(grid), vmem_limit_bytes=VMEM_LIMIT)
    scratch = list(scratch)
    if comm is None:
        return pl.pallas_call(body, name=name, grid=grid, in_specs=in_specs, out_specs=out_specs,
                              out_shape=out_shape, scratch_shapes=scratch, compiler_params=params)
    n_in, n_out, n_s = len(in_specs), len(out_shape), len(scratch)
    n_ci, n_co = len(comm.operands), len(comm.out_shape)
    any_spec = pl.BlockSpec(memory_space=pl.ANY)

    def carried(*refs):
        ins, refs = refs[:n_in], refs[n_in:]
        c_ins, refs = refs[:n_ci], refs[n_ci:]
        outs, refs = refs[:n_out], refs[n_out:]
        c_outs, refs = refs[:n_co], refs[n_co:]
        scr, c_sems = refs[:n_s], refs[n_s:]
        step = pl.program_id(0)
        for d in range(1, len(grid)):
            step = step * grid[d] + pl.program_id(d)
        total = math.prod(grid)
        first, last = step == 0, step == total - 1

        @pl.when(first)
        def _():
            comm.start(c_ins, c_outs, c_sems)

        if comm.relay is not None:
            @pl.when(step == min(total - 1, (RELAY_AT[0] * total) // RELAY_AT[1]))
            def _():
                comm.relay(c_ins, c_outs, c_sems)

        body(*ins, *outs, *scr)

        @pl.when(last)
        def _():
            comm.finish(c_ins, c_outs, c_sems)

    call = pl.pallas_call(
        carried, name=name, grid=grid, in_specs=list(in_specs) + [any_spec] * n_ci,
        out_specs=list(out_specs) + [any_spec] * n_co, out_shape=list(out_shape) + comm.out_shape,
        scratch_shapes=scratch + comm.sems, compiler_params=params)

    def run(*operands):
        res = call(*operands, *comm.operands)
        comm.results = list(res[n_out:])
        return list(res[:n_out])

    return run


def _run_comm(name, comm):
    n_ci, n_co = len(comm.operands), len(comm.out_shape)
    any_spec = pl.BlockSpec(memory_space=pl.ANY)

    def body(*refs):
        c_ins, c_outs, c_sems = refs[:n_ci], refs[n_ci:n_ci + n_co], refs[n_ci + n_co:]
        comm.start(c_ins, c_outs, c_sems)
        if comm.relay is not None:
            comm.relay(c_ins, c_outs, c_sems)
        comm.finish(c_ins, c_outs, c_sems)

    return pl.pallas_call(body, name=name, in_specs=[any_spec] * n_ci, out_specs=[any_spec] * n_co,
                          out_shape=comm.out_shape, scratch_shapes=comm.sems)(*comm.operands)


def _sigmoid(x):
    return 1.0 / (1.0 + jnp.exp(-x))


def _mm(name, *, grid, pairs, once_pairs=(), extra=(), out_shape, out_specs, acc_shapes, nk, kaxis, epilogue,
        comm=None):
    all_pairs = list(pairs) + list(once_pairs)
    n_p, n_o = len(pairs), len(once_pairs)
    n_e, n_out, n_acc = len(extra), len(out_shape), len(acc_shapes)

    def body(*refs):
        ab = refs[: 2 * (n_p + n_o)]
        ex = refs[2 * (n_p + n_o): 2 * (n_p + n_o) + n_e]
        outs = refs[2 * (n_p + n_o) + n_e: 2 * (n_p + n_o) + n_e + n_out]
        accs = refs[2 * (n_p + n_o) + n_e + n_out:]

        def dots(idx_range):
            vals = [None] * n_acc
            for p in idx_range:
                d = lax.dot_general(ab[2 * p][...], ab[2 * p + 1][...], all_pairs[p][4],
                                    preferred_element_type=F32)
                ai = all_pairs[p][5]
                vals[ai] = d if vals[ai] is None else vals[ai] + d
            return vals

        if nk == 1:
            vals = dots(range(n_p + n_o))
            epilogue(vals, ex, outs)
            return

        k = pl.program_id(kaxis)

        @pl.when(k == 0)
        def _():
            vals = dots(range(n_p + n_o))
            for ai in range(n_acc):
                accs[ai][...] = vals[ai]

        @pl.when(k > 0)
        def _():
            vals = dots(range(n_p))
            for ai in range(n_acc):
                if vals[ai] is not None:
                    accs[ai][...] += vals[ai]

        @pl.when(k == nk - 1)
        def _():
            epilogue([a[...] for a in accs], ex, outs)

    operands, in_specs = [], []
    for p in all_pairs:
        operands += [p[0], p[2]]
        in_specs += [p[1], p[3]]
    for arr, spec in extra:
        operands.append(arr)
        in_specs.append(spec)
    scratch = [pltpu.VMEM(s, F32) for s in acc_shapes] if nk > 1 else []
    return _pcall(body, name=name, grid=grid, in_specs=in_specs, out_specs=out_specs, out_shape=out_shape,
                  scratch=scratch, comm=comm)(*operands)


def rms_fwd(x, g):
    T, D = x.shape
    tt = _tile(T, 512, 8)

    def body(x_ref, g_ref, h_ref, r_ref):
        xv = x_ref[...]
        r = lax.rsqrt(jnp.mean(xv * xv, axis=-1, keepdims=True) + NORM_EPS)
        h_ref[...] = (xv * r * g_ref[...]).astype(BF16)
        r_ref[...] = r

    return _pcall(
        body, name="rms_fwd", grid=(T // tt,),
        in_specs=[pl.BlockSpec((tt, D), lambda i: (i, 0)), pl.BlockSpec((1, D), lambda i: (0, 0))],
        out_specs=[pl.BlockSpec((tt, D), lambda i: (i, 0)), pl.BlockSpec((tt, 1), lambda i: (i, 0))],
        out_shape=[jax.ShapeDtypeStruct((T, D), BF16), jax.ShapeDtypeStruct((T, 1), F32)],
    )(x, g)


def rms_bwd(dh, x, r, g, dres, out_scale, comm=None):
    T, D = x.shape
    tt = _tile(T, 256, 8)

    def body(dh_ref, x_ref, r_ref, g_ref, dres_ref, dx_ref, dxb_ref, dg_ref):
        i = pl.program_id(0)
        xh = x_ref[...] * r_ref[...]
        dhv = dh_ref[...]
        dxh = dhv * g_ref[...]
        dx = dres_ref[...] + r_ref[...] * (dxh - xh * jnp.mean(dxh * xh, axis=-1, keepdims=True))
        dx_ref[...] = dx
        dxb_ref[...] = (out_scale * dx).astype(BF16)
        part = jnp.sum(dhv * xh, axis=0, keepdims=True)

        @pl.when(i == 0)
        def _():
            dg_ref[...] = part

        @pl.when(i > 0)
        def _():
            dg_ref[...] += part

    row = pl.BlockSpec((tt, D), lambda i: (i, 0))
    return _pcall(
        body, name="rms_bwd", grid=(T // tt,),
        in_specs=[row, row, pl.BlockSpec((tt, 1), lambda i: (i, 0)), pl.BlockSpec((1, D), lambda i: (0, 0)), row],
        out_specs=[row, row, pl.BlockSpec((1, D), lambda i: (0, 0))],
        out_shape=[jax.ShapeDtypeStruct((T, D), F32), jax.ShapeDtypeStruct((T, D), BF16),
                   jax.ShapeDtypeStruct((1, D), F32)], comm=comm,
    )(dh, x, r, g, dres)


def final_loss(x, tgt, g):
    T, D = x.shape
    tt = _tile(T, 256, 8)

    def body(x_ref, t_ref, g_ref, dx_ref, dxb_ref, loss_ref, dg_ref):
        i = pl.program_id(0)
        xv = x_ref[...]
        r = lax.rsqrt(jnp.mean(xv * xv, axis=-1, keepdims=True) + NORM_EPS)
        xh = xv * r
        err = xh * g_ref[...] - t_ref[...]
        part_loss = 0.5 * jnp.sum(jnp.mean(err * err, axis=-1, keepdims=True), axis=0, keepdims=True)
        dy = err * (1.0 / D)
        dxh = dy * g_ref[...]
        dx = r * (dxh - xh * jnp.mean(dxh * xh, axis=-1, keepdims=True))
        dx_ref[...] = dx
        dxb_ref[...] = (0.5 * dx).astype(BF16)
        part_g = jnp.sum(dy * xh, axis=0, keepdims=True)
        part_l = jnp.broadcast_to(part_loss, (8, LANES))

        @pl.when(i == 0)
        def _():
            dg_ref[...] = part_g
            loss_ref[...] = part_l

        @pl.when(i > 0)
        def _():
            dg_ref[...] += part_g
            loss_ref[...] += part_l

    row = pl.BlockSpec((tt, D), lambda i: (i, 0))
    return _pcall(
        body, name="final_loss", grid=(T // tt,),
        in_specs=[row, row, pl.BlockSpec((1, D), lambda i: (0, 0))],
        out_specs=[row, row, pl.BlockSpec((8, LANES), lambda i: (0, 0)), pl.BlockSpec((1, D), lambda i: (0, 0))],
        out_shape=[jax.ShapeDtypeStruct((T, D), F32), jax.ShapeDtypeStruct((T, D), BF16),
                   jax.ShapeDtypeStruct((8, LANES), F32), jax.ShapeDtypeStruct((1, D), F32)],
    )(x, tgt, g)


def ffn_gate(h, wg3, comm=None):
    T, D = h.shape
    nc, _, fs = wg3.shape
    tm = _tile(T, 512, 8)

    def epilogue(vals, ex, outs):
        outs[0][...] = vals[0].astype(BF16)

    return _mm("ffn_gate", grid=(nc, T // tm),
               pairs=[(h, pl.BlockSpec((tm, D), lambda j, i: (i, 0)),
                       wg3, pl.BlockSpec((None, D, fs), lambda j, i: (j, 0, 0)), NN, 0)],
               out_shape=[jax.ShapeDtypeStruct((T, nc * fs), BF16)],
               out_specs=[pl.BlockSpec((tm, fs), lambda j, i: (i, j))],
               acc_shapes=[(tm, fs)], nk=1, kaxis=None, epilogue=epilogue, comm=comm)[0]


def ffn_upmul(h, wu3, a, comm=None):
    T, D = h.shape
    nc, _, fs = wu3.shape
    tm = _tile(T, 512, 8)

    def epilogue(vals, ex, outs):
        b = vals[0]
        av = ex[0][...].astype(F32)
        outs[0][...] = b.astype(BF16)
        outs[1][...] = (av * _sigmoid(av) * b).astype(BF16)

    t_spec = pl.BlockSpec((tm, fs), lambda j, i: (i, j))
    o_shape = jax.ShapeDtypeStruct((T, nc * fs), BF16)
    return _mm("ffn_upmul", grid=(nc, T // tm),
               pairs=[(h, pl.BlockSpec((tm, D), lambda j, i: (i, 0)),
                       wu3, pl.BlockSpec((None, D, fs), lambda j, i: (j, 0, 0)), NN, 0)],
               extra=[(a, t_spec)], out_shape=[o_shape] * 2, out_specs=[t_spec] * 2,
               acc_shapes=[(tm, fs)], nk=1, kaxis=None, epilogue=epilogue, comm=comm)


def ffn_up(h, wg3, wu3, comm=None):
    T, D = h.shape
    nc, _, fs = wg3.shape
    tm = _tile(T, 512, 8)

    def epilogue(vals, ex, outs):
        a, b = vals
        outs[0][...] = a.astype(BF16)
        outs[1][...] = b.astype(BF16)
        outs[2][...] = (a * _sigmoid(a) * b).astype(BF16)

    h_spec = pl.BlockSpec((tm, D), lambda j, i: (i, 0))
    w_spec = pl.BlockSpec((None, D, fs), lambda j, i: (j, 0, 0))
    o_spec = pl.BlockSpec((tm, fs), lambda j, i: (i, j))
    o_shape = jax.ShapeDtypeStruct((T, nc * fs), BF16)
    return _mm("ffn_up", grid=(nc, T // tm),
               pairs=[(h, h_spec, wg3, w_spec, NN, 0), (h, h_spec, wu3, w_spec, NN, 1)],
               out_shape=[o_shape] * 3, out_specs=[o_spec] * 3, acc_shapes=[(tm, fs)] * 2, nk=1, kaxis=None,
               epilogue=epilogue, comm=comm)


def mm_residual(name, a, b3, res, scale, comm=None):
    T = a.shape[0]
    nk, tk, N = b3.shape
    tm, tn = _tile(T, 512, 8), _tile(N, 2048)

    def epilogue(vals, ex, outs):
        outs[0][...] = ex[0][...] + scale * vals[0]

    return _mm(name, grid=(T // tm, N // tn, nk),
               pairs=[(a, pl.BlockSpec((tm, tk), lambda i, n, k: (i, k)),
                       b3, pl.BlockSpec((None, tk, tn), lambda i, n, k: (k, 0, n)), NN, 0)],
               extra=[(res, pl.BlockSpec((tm, tn), lambda i, n, k: (i, n)))],
               out_shape=[jax.ShapeDtypeStruct((T, N), F32)],
               out_specs=[pl.BlockSpec((tm, tn), lambda i, n, k: (i, n))],
               acc_shapes=[(tm, tn)], nk=nk, kaxis=2, epilogue=epilogue, comm=comm)


def ffn_bwd_mid(dout, wd3, a, b, comm=None):
    T, D = dout.shape
    nc, fs, _ = wd3.shape
    tm = _tile(T, 512, 8)

    def epilogue(vals, ex, outs):
        dm = vals[0]
        av = ex[0][...].astype(F32)
        bv = ex[1][...].astype(F32)
        s = _sigmoid(av)
        outs[0][...] = (dm * bv * (s * (1.0 + av * (1.0 - s)))).astype(BF16)
        outs[1][...] = (dm * (av * s)).astype(BF16)

    t_spec = pl.BlockSpec((tm, fs), lambda j, i: (i, j))
    o_shape = jax.ShapeDtypeStruct((T, nc * fs), BF16)
    return _mm("ffn_bwd_mid", grid=(nc, T // tm),
               pairs=[(dout, pl.BlockSpec((tm, D), lambda j, i: (i, 0)),
                       wd3, pl.BlockSpec((None, fs, D), lambda j, i: (j, 0, 0)), NT, 0)],
               extra=[(a, t_spec), (b, t_spec)],
               out_shape=[o_shape] * 2, out_specs=[t_spec] * 2, acc_shapes=[(tm, fs)], nk=1, kaxis=None,
               epilogue=epilogue, comm=comm)


def dw_rowshard(name, a, b, nc, comm=None):
    T, M = a.shape
    N = b.shape[1]
    ms = M // nc
    tn, tk = _tile(N, 2048 if ms <= 512 else 1024), _tile(T, 1024, 16)

    def epilogue(vals, ex, outs):
        outs[0][...] = vals[0].astype(BF16)

    return _mm(name, grid=(nc, N // tn, T // tk),
               pairs=[(a, pl.BlockSpec((tk, ms), lambda j, n, k: (k, j)),
                       b, pl.BlockSpec((tk, tn), lambda j, n, k: (k, n)), TN, 0)],
               out_shape=[jax.ShapeDtypeStruct((nc, ms, N), BF16)],
               out_specs=[pl.BlockSpec((None, ms, tn), lambda j, n, k: (j, 0, n))],
               acc_shapes=[(ms, tn)], nk=T // tk, kaxis=2, epilogue=epilogue, comm=comm)


def dw_colshard(name, a, bs, nc, comm=None):
    T, M = a.shape
    ns = bs[0].shape[1] // nc
    tm, tk = _tile(M, 512), _tile(T, 1024, 16)

    def epilogue(vals, ex, outs):
        for v, o in zip(vals, outs):
            o[...] = v.astype(BF16)

    a_spec = pl.BlockSpec((tk, tm), lambda j, m, k: (k, m))
    b_spec = pl.BlockSpec((tk, ns), lambda j, m, k: (k, j))
    return _mm(name, grid=(nc, M // tm, T // tk),
               pairs=[(a, a_spec, b, b_spec, TN, p) for p, b in enumerate(bs)],
               out_shape=[jax.ShapeDtypeStruct((nc, M, ns), BF16)] * len(bs),
               out_specs=[pl.BlockSpec((None, tm, ns), lambda j, m, k: (j, m, 0))] * len(bs),
               acc_shapes=[(tm, ns)] * len(bs), nk=T // tk, kaxis=2, epilogue=epilogue, comm=comm)


def ffn_dh(da, db, wg3, wu3, comm=None):
    T = da.shape[0]
    nc, D, fs = wg3.shape
    tm, tn = _tile(T, 512, 8), _tile(D, 1024)

    def epilogue(vals, ex, outs):
        outs[0][...] = vals[0]

    a_spec = pl.BlockSpec((tm, fs), lambda i, n, k: (i, k))
    w_spec = pl.BlockSpec((None, tn, fs), lambda i, n, k: (k, n, 0))
    return _mm("ffn_dh", grid=(T // tm, D // tn, nc),
               pairs=[(da, a_spec, wg3, w_spec, NT, 0), (db, a_spec, wu3, w_spec, NT, 0)],
               out_shape=[jax.ShapeDtypeStruct((T, D), F32)],
               out_specs=[pl.BlockSpec((tm, tn), lambda i, n, k: (i, n))],
               acc_shapes=[(tm, tn)], nk=nc, kaxis=2, epilogue=epilogue, comm=comm)[0]


def proj_main(h, w_t, P, comm=None):
    T, D = h.shape
    tm, tn = _tile(T, 512, 8), _tile(P, 1024)

    def epilogue(vals, ex, outs):
        outs[0][...] = vals[0].astype(BF16)

    return _mm("proj_main", grid=(P // tn, T // tm),
               pairs=[(h, pl.BlockSpec((tm, D), lambda j, i: (i, 0)),
                       w_t, pl.BlockSpec((tn, D), lambda j, i: (j, 0)), NT, 0)],
               out_shape=[jax.ShapeDtypeStruct((T, P), BF16)],
               out_specs=[pl.BlockSpec((tm, tn), lambda j, i: (i, j))],
               acc_shapes=[(tm, tn)], nk=1, kaxis=None, epilogue=epilogue, comm=comm)[0]


def mm_nt_bf16(name, a, w):
    T, K = a.shape
    M = w.shape[0]
    tm, tn = _tile(T, 512, 8), _tile(M, 1024)

    def epilogue(vals, ex, outs):
        outs[0][...] = vals[0].astype(BF16)

    return _mm(name, grid=(T // tm, M // tn),
               pairs=[(a, pl.BlockSpec((tm, K), lambda i, n: (i, 0)),
                       w, pl.BlockSpec((tn, K), lambda i, n: (n, 0)), NT, 0)],
               out_shape=[jax.ShapeDtypeStruct((T, M), BF16)],
               out_specs=[pl.BlockSpec((tm, tn), lambda i, n: (i, n))],
               acc_shapes=[(tm, tn)], nk=1, kaxis=None, epilogue=epilogue)[0]


def proj_dh(dproj, w_t, df, wf_t, comm=None):
    T, P = dproj.shape
    D = w_t.shape[1]
    tm, tn, tk = _tile(T, 512, 8), _tile(D, 2048), _tile(P, 1280)

    def epilogue(vals, ex, outs):
        outs[0][...] = vals[0]

    return _mm("proj_dh", grid=(T // tm, D // tn, P // tk),
               pairs=[(dproj, pl.BlockSpec((tm, tk), lambda i, n, k: (i, k)),
                       w_t, pl.BlockSpec((tk, tn), lambda i, n, k: (k, n)), NN, 0)],
               once_pairs=[(df, pl.BlockSpec((tm, LANES), lambda i, n, k: (i, 0)),
                            wf_t, pl.BlockSpec((LANES, tn), lambda i, n, k: (0, n)), NN, 0)],
               out_shape=[jax.ShapeDtypeStruct((T, D), F32)],
               out_specs=[pl.BlockSpec((tm, tn), lambda i, n, k: (i, n))],
               acc_shapes=[(tm, tn)], nk=P // tk, kaxis=2, epilogue=epilogue, comm=comm)[0]


def proj_dw(dproj, df, h, rows):
    T, P = dproj.shape
    D = h.shape[1]
    tm, tn, tk = _tile(P, 1280), D // 2, _tile(T, 1024, 16)

    def to_bf16(vals, ex, outs):
        outs[0][...] = vals[0].astype(BF16)

    def to_f32(vals, ex, outs):
        outs[0][...] = vals[0]

    main = _mm("proj_dw_main", grid=(P // tm, D // tn, T // tk),
               pairs=[(dproj, pl.BlockSpec((tk, tm), lambda m, n, k: (k, m)),
                       h, pl.BlockSpec((tk, tn), lambda m, n, k: (k, n)), TN, 0)],
               out_shape=[jax.ShapeDtypeStruct((2, rows, tn), BF16)],
               out_specs=[pl.BlockSpec((None, tm, tn), lambda m, n, k: (n, m, 0))],
               acc_shapes=[(tm, tn)], nk=T // tk, kaxis=2, epilogue=to_bf16)[0]
    gate = _mm("proj_dw_f", grid=(1, D // tn, T // tk),
               pairs=[(df, pl.BlockSpec((tk, LANES), lambda m, n, k: (k, 0)),
                       h, pl.BlockSpec((tk, tn), lambda m, n, k: (k, n)), TN, 0)],
               out_shape=[jax.ShapeDtypeStruct((LANES, D), F32)],
               out_specs=[pl.BlockSpec((LANES, tn), lambda m, n, k: (0, n))],
               acc_shapes=[(LANES, tn)], nk=T // tk, kaxis=2, epilogue=to_f32)[0]
    return main, gate


def fgate_fwd(h, wf_t, bias, n_heads):
    T, D = h.shape
    tt = _tile(T, 512, 8)

    def body(h_ref, w_ref, b_ref, f_ref, c_ref, carry):
        i = pl.program_id(0)

        @pl.when(i == 0)
        def _():
            carry[...] = jnp.zeros_like(carry)

        f = lax.dot_general(h_ref[...], w_ref[...], NT, preferred_element_type=F32) + b_ref[...]
        logf = jnp.minimum(f, 0.0) - jnp.log(1.0 + jnp.exp(-jnp.abs(f)))
        tri = (lax.broadcasted_iota(jnp.int32, (tt, tt), 0) >= lax.broadcasted_iota(jnp.int32, (tt, tt), 1))
        cs = jnp.dot(tri.astype(F32), logf, preferred_element_type=F32, precision=lax.Precision.HIGHEST)
        c = cs + carry[...]
        f_ref[...] = f
        c_ref[...] = c
        carry[...] = c[tt - 1:tt, :]

    row = pl.BlockSpec((tt, LANES), lambda i: (i, 0))
    return _pcall(
        body, name="fgate_fwd", grid=(T // tt,),
        in_specs=[pl.BlockSpec((tt, D), lambda i: (i, 0)), pl.BlockSpec((LANES, D), lambda i: (0, 0)),
                  pl.BlockSpec((1, LANES), lambda i: (0, 0))],
        out_specs=[row, row],
        out_shape=[jax.ShapeDtypeStruct((T, LANES), F32)] * 2,
        scratch=[pltpu.VMEM((1, LANES), F32)],
    )(h, wf_t, bias)


def fgate_bwd(dc, f, n_heads):
    T = dc.shape[0]
    tt = _tile(T, 512, 8)
    nt = T // tt

    def body(dc_ref, f_ref, df_ref, db_ref, carry):
        i = pl.program_id(0)

        @pl.when(i == 0)
        def _():
            carry[...] = jnp.zeros_like(carry)

        tri = (lax.broadcasted_iota(jnp.int32, (tt, tt), 1) >= lax.broadcasted_iota(jnp.int32, (tt, tt), 0))
        rs = jnp.dot(tri.astype(F32), dc_ref[...], preferred_element_type=F32,
                     precision=lax.Precision.HIGHEST) + carry[...]
        carry[...] = rs[0:1, :]
        lane = lax.broadcasted_iota(jnp.int32, (tt, LANES), 1)
        df = jnp.where(lane < n_heads, rs * _sigmoid(-f_ref[...]), 0.0)
        df_ref[...] = df.astype(BF16)
        part = jnp.sum(df, axis=0, keepdims=True)

        @pl.when(i == 0)
        def _():
            db_ref[...] = part

        @pl.when(i > 0)
        def _():
            db_ref[...] += part

    rev = pl.BlockSpec((tt, LANES), lambda i: (nt - 1 - i, 0))
    return _pcall(
        body, name="fgate_bwd", grid=(nt,),
        in_specs=[rev, rev],
        out_specs=[rev, pl.BlockSpec((1, LANES), lambda i: (0, 0))],
        out_shape=[jax.ShapeDtypeStruct((T, LANES), BF16), jax.ShapeDtypeStruct((1, LANES), F32)],
        scratch=[pltpu.VMEM((1, LANES), F32)],
    )(dc, f)


SUBLANES = 8
SHIFT_ROWS = HALO - SUBLANES


def _shifted_copies(buf, sh, tt):
    for r in range(1, SUBLANES):
        sh[r - 1, 0:tt + SHIFT_ROWS, :] = buf[pl.ds(r, tt + SHIFT_ROWS), :]


def _tap(buf, sh, offset, tt):
    q, r = divmod(offset, SUBLANES)
    if r == 0:
        return buf[pl.ds(SUBLANES * q, tt), :]
    return sh[r - 1, pl.ds(SUBLANES * q, tt), :]


def conv_fwd(proj, conv_w, conv_b, ln_g, ln_b):
    T = proj.shape[0]
    C = conv_w.shape[1]
    tt = _tile(T, 256, HALO)
    hb = tt // HALO

    def body(a_ref, g_ref, ah_ref, gh_ref, w_ref, cb_ref, lg_ref, lb_ref, ypre_ref, y_ref, ubuf, ush):
        i = pl.program_id(0)
        u = a_ref[...].astype(F32) * _sigmoid(g_ref[...].astype(F32))
        uh = ah_ref[...].astype(F32) * _sigmoid(gh_ref[...].astype(F32))
        ubuf[0:HALO, :] = jnp.where(i == 0, 0.0, uh)
        ubuf[HALO:HALO + tt, :] = u
        _shifted_copies(ubuf, ush, tt)
        acc = jnp.broadcast_to(cb_ref[...], (tt, C))
        for k in range(CONV_K):
            acc = acc + w_ref[k:k + 1, :] * _tap(ubuf, ush, HALO - (CONV_K - 1) + k, tt)
        ypre_ref[...] = acc
        mu = jnp.mean(acc, axis=-1, keepdims=True)
        d = acc - mu
        rstd = lax.rsqrt(jnp.mean(d * d, axis=-1, keepdims=True) + LN_EPS)
        z = d * rstd * lg_ref[...] + lb_ref[...]
        y_ref[...] = (z * _sigmoid(z)).astype(BF16)

    vec = pl.BlockSpec((1, C), lambda i: (0, 0))
    return _pcall(
        body, name="conv_fwd", grid=(T // tt,),
        in_specs=[pl.BlockSpec((tt, C), lambda i: (i, 0)), pl.BlockSpec((tt, C), lambda i: (i, 1)),
                  pl.BlockSpec((HALO, C), lambda i: (jnp.maximum(i * hb - 1, 0), 0)),
                  pl.BlockSpec((HALO, C), lambda i: (jnp.maximum(i * hb - 1, 0), 1)),
                  pl.BlockSpec((HALO, C), lambda i: (0, 0)), vec, vec, vec],
        out_specs=[pl.BlockSpec((tt, C), lambda i: (i, 0))] * 2,
        out_shape=[jax.ShapeDtypeStruct((T, C), F32), jax.ShapeDtypeStruct((T, C), BF16)],
        scratch=[pltpu.VMEM((tt + HALO, C), F32), pltpu.VMEM((SUBLANES - 1, tt + SHIFT_ROWS, C), F32)],
    )(proj, proj, proj, proj, conv_w, conv_b, ln_g, ln_b)


def conv_bwd(proj, ypre, dycat, conv_w, ln_g, ln_b):
    T = proj.shape[0]
    C = conv_w.shape[1]
    tt = _tile(T, 256, HALO)
    hb = tt // HALO
    nt = T // tt
    last_h = T // HALO - 1

    def ln_bwd(ypre_v, dout_v, lg, lb):
        mu = jnp.mean(ypre_v, axis=-1, keepdims=True)
        d = ypre_v - mu
        rstd = lax.rsqrt(jnp.mean(d * d, axis=-1, keepdims=True) + LN_EPS)
        yh = d * rstd
        z = yh * lg + lb
        s = _sigmoid(z)
        dz = dout_v * (s * (1.0 + z * (1.0 - s)))
        dyh = dz * lg
        dy = rstd * (dyh - jnp.mean(dyh, axis=-1, keepdims=True)
                     - yh * jnp.mean(dyh * yh, axis=-1, keepdims=True))
        return dy, dz, yh

    def body(a_ref, g_ref, ah_ref, gh_ref, yp_ref, ypn_ref, do_ref, don_ref, w_ref, lg_ref, lb_ref,
             dag_ref, dw_ref, dcb_ref, dlg_ref, dlb_ref, ubuf, dybuf, ush, dysh):
        i = pl.program_id(0)
        av = a_ref[...].astype(F32)
        sg = _sigmoid(g_ref[...].astype(F32))
        uh = ah_ref[...].astype(F32) * _sigmoid(gh_ref[...].astype(F32))
        ubuf[0:HALO, :] = jnp.where(i == 0, 0.0, uh)
        ubuf[HALO:HALO + tt, :] = av * sg
        lg, lb = lg_ref[...], lb_ref[...]
        dy, dz, yh = ln_bwd(yp_ref[...], do_ref[...].astype(F32), lg, lb)
        dyn, _, _ = ln_bwd(ypn_ref[...], don_ref[...].astype(F32), lg, lb)
        dybuf[0:tt, :] = dy
        dybuf[tt:tt + HALO, :] = jnp.where(i == nt - 1, 0.0, dyn)
        _shifted_copies(ubuf, ush, tt)
        _shifted_copies(dybuf, dysh, tt)

        @pl.when(i == 0)
        def _():
            dw_ref[...] = jnp.zeros_like(dw_ref)
            dcb_ref[...] = jnp.zeros_like(dcb_ref)
            dlg_ref[...] = jnp.zeros_like(dlg_ref)
            dlb_ref[...] = jnp.zeros_like(dlb_ref)

        du = jnp.zeros((tt, C), F32)
        for k in range(CONV_K):
            du = du + w_ref[k:k + 1, :] * _tap(dybuf, dysh, CONV_K - 1 - k, tt)
            dw_ref[k:k + 1, :] += jnp.sum(dy * _tap(ubuf, ush, HALO - (CONV_K - 1) + k, tt), axis=0, keepdims=True)
        dcb_ref[...] += jnp.sum(dy, axis=0, keepdims=True)
        dlg_ref[...] += jnp.sum(dz * yh, axis=0, keepdims=True)
        dlb_ref[...] += jnp.sum(dz, axis=0, keepdims=True)

        dag_ref[:, 0:C] = (du * sg).astype(BF16)
        dag_ref[:, C:2 * C] = (du * av * sg * (1.0 - sg)).astype(BF16)

    vec = pl.BlockSpec((1, C), lambda i: (0, 0))
    prev_h = lambda col: pl.BlockSpec((HALO, C), lambda i: (jnp.maximum(i * hb - 1, 0), col))
    next_h = pl.BlockSpec((HALO, C), lambda i: (jnp.minimum((i + 1) * hb, last_h), 0))
    return _pcall(
        body, name="conv_bwd", grid=(nt,),
        in_specs=[pl.BlockSpec((tt, C), lambda i: (i, 0)), pl.BlockSpec((tt, C), lambda i: (i, 1)),
                  prev_h(0), prev_h(1),
                  pl.BlockSpec((tt, C), lambda i: (i, 0)), next_h,
                  pl.BlockSpec((tt, C), lambda i: (i, 0)), next_h,
                  pl.BlockSpec((HALO, C), lambda i: (0, 0)), vec, vec],
        out_specs=[pl.BlockSpec((tt, 2 * C), lambda i: (i, 0)), pl.BlockSpec((HALO, C), lambda i: (0, 0)),
                   vec, vec, vec],
        out_shape=[jax.ShapeDtypeStruct((T, 2 * C), BF16), jax.ShapeDtypeStruct((HALO, C), F32),
                   jax.ShapeDtypeStruct((1, C), F32), jax.ShapeDtypeStruct((1, C), F32),
                   jax.ShapeDtypeStruct((1, C), F32)],
        scratch=[pltpu.VMEM((tt + HALO, C), F32), pltpu.VMEM((tt + HALO, C), F32),
                 pltpu.VMEM((SUBLANES - 1, tt + SHIFT_ROWS, C), F32),
                 pltpu.VMEM((SUBLANES - 1, tt + SHIFT_ROWS, C), F32)],
    )(proj, proj, proj, proj, ypre, ypre, dycat, dycat, conv_w, ln_g, ln_b)


PAIR = LANES // HEAD_DIM


def _head_masks(rows):
    lane = lax.broadcasted_iota(jnp.int32, (rows, LANES), 1)
    return [jnp.logical_and(lane >= hh * HEAD_DIM, lane < (hh + 1) * HEAD_DIM) for hh in range(PAIR)]


def _causal(tq, tk):
    return lax.broadcasted_iota(jnp.int32, (tq, tk), 0) >= lax.broadcasted_iota(jnp.int32, (tq, tk), 1)


def _lane_column(block, lane_index):
    lane = lax.broadcasted_iota(jnp.int32, block.shape, 1)
    return jnp.sum(jnp.where(lane == lane_index, block, 0.0), axis=-1, keepdims=True)


def attn_fwd(proj, cum, ck4, q_col, comm=None):
    T = proj.shape[0]
    H, nkv, _, tk = ck4.shape
    tq = tk
    hd = H * HEAD_DIM
    qb, kb, vb = q_col // LANES, (q_col + hd) // LANES, (q_col + 2 * hd) // LANES
    scale = 1.0 / math.sqrt(HEAD_DIM)

    def body(q_ref, k_ref, v_ref, cum_ref, ck_ref, o_ref, lse_ref):
        hp = pl.program_id(0)
        i = pl.program_id(1)
        masks = _head_masks(tq)
        q2 = q_ref[...] * scale
        qs = [jnp.where(mk, q2, jnp.zeros_like(q2)) for mk in masks]
        cqs = [_lane_column(cum_ref[...], PAIR * hp + hh) for hh in range(PAIR)]

        def step(j, carry, diagonal, blocks=1):
            scores, values = [], []
            for b in range(blocks):
                off = pl.multiple_of((j + b) * tk, tk)
                kj = k_ref[pl.ds(off, tk), :]
                values.append(v_ref[pl.ds(off, tk), :])
                per_head = []
                for hh in range(PAIR):
                    s = lax.dot_general(qs[hh], kj, NT, preferred_element_type=F32)
                    s = s + cqs[hh] - ck_ref[hh, j + b]
                    if diagonal:
                        s = jnp.where(_causal(tq, tk), s, NEG_INF)
                    per_head.append(s)
                scores.append(per_head)
            out = []
            for hh in range(PAIR):
                m, l, acc = carry[hh]
                m_new = m
                for b in range(blocks):
                    m_new = jnp.maximum(m_new, jnp.max(scores[b][hh], axis=-1, keepdims=True))
                alpha = jnp.exp(m - m_new)
                l, acc = alpha * l, alpha * acc
                for b in range(blocks):
                    p = jnp.exp(scores[b][hh] - m_new)
                    l = l + jnp.sum(p, axis=-1, keepdims=True)
                    acc = acc + jnp.dot(p.astype(BF16), values[b], preferred_element_type=F32)
                out.append((m_new, l, acc))
            return tuple(out)

        init = tuple((jnp.full((tq, 1), -jnp.inf, F32), jnp.zeros((tq, 1), F32), jnp.zeros((tq, LANES), F32))
                     for _ in range(PAIR))
        twos = i // 2
        carry = lax.fori_loop(0, twos, lambda t, c: step(2 * t, c, False, blocks=2), init)
        carry = lax.fori_loop(2 * twos, i, functools.partial(step, diagonal=False), carry)
        carry = step(i, carry, True)
        o = carry[PAIR - 1][2] / carry[PAIR - 1][1]
        for hh in range(PAIR - 1):
            o = jnp.where(masks[hh], carry[hh][2] / carry[hh][1], o)
        o_ref[...] = o
        lse = jnp.broadcast_to(carry[PAIR - 1][0] + jnp.log(carry[PAIR - 1][1]), (tq, LANES))
        for hh in range(PAIR - 1):
            lse = jnp.where(masks[hh], carry[hh][0] + jnp.log(carry[hh][1]), lse)
        lse_ref[...] = lse

    return _pcall(
        body, name="attn_fwd", grid=(H // PAIR, T // tq),
        in_specs=[pl.BlockSpec((tq, LANES), lambda hp, i: (i, qb + hp)),
                  pl.BlockSpec((T, LANES), lambda hp, i: (0, kb + hp)),
                  pl.BlockSpec((T, LANES), lambda hp, i: (0, vb + hp)),
                  pl.BlockSpec((tq, LANES), lambda hp, i: (i, 0)),
                  pl.BlockSpec((PAIR, nkv, 1, tk), lambda hp, i: (hp, 0, 0, 0))],
        out_specs=[pl.BlockSpec((tq, LANES), lambda hp, i: (i, hp)),
                   pl.BlockSpec((None, tq, LANES), lambda hp, i: (hp, i, 0))],
        out_shape=[jax.ShapeDtypeStruct((T, hd), F32), jax.ShapeDtypeStruct((H // PAIR, T, LANES), F32)],
        comm=comm,
    )(proj, proj, proj, cum, ck4)


def attn_bwd(proj, o, dycat, lse, cum, ck4, q_col, do_col, comm=None):
    T = proj.shape[0]
    H, nkv, _, tk = ck4.shape
    tq = tk
    nq = T // tq
    hd = H * HEAD_DIM
    qb, kb, vb = q_col // LANES, (q_col + hd) // LANES, (q_col + 2 * hd) // LANES
    dob = do_col // LANES
    scale = 1.0 / math.sqrt(HEAD_DIM)

    def body(q_ref, k_ref, v_ref, o_ref, do_ref, lse_ref, cum_ref, ck_ref,
             dq_ref, dk_ref, dv_ref, dcq_ref, dck_ref, dk_acc, dv_acc):
        hp = pl.program_id(0)
        j = pl.program_id(1)

        def block(i, diagonal):
            masks = _head_masks(tq)
            rows = pl.ds(pl.multiple_of(i * tq, tq), tq)
            q2, k2, v2, do2 = q_ref[rows, :] * scale, k_ref[...], v_ref[...], do_ref[rows, :]
            zero = jnp.zeros_like(q2)
            prod = do2.astype(F32) * o_ref[rows, :]
            cum_q, lse_q = cum_ref[rows, :], lse_ref[rows, :]
            dq_part = dk_part = dv_part = None
            dcq_part = jnp.zeros((tq, LANES), F32)
            lane = lax.broadcasted_iota(jnp.int32, (tq, LANES), 1)
            for hh in range(PAIR):
                qh = jnp.where(masks[hh], q2, zero)
                kh = jnp.where(masks[hh], k2, zero)
                doh = jnp.where(masks[hh], do2, zero)
                delta = jnp.sum(jnp.where(masks[hh], prod, 0.0), axis=-1, keepdims=True)
                s = lax.dot_general(qh, k2, NT, preferred_element_type=F32)
                s = s + _lane_column(cum_q, PAIR * hp + hh) - ck_ref[hh]
                if diagonal:
                    s = jnp.where(_causal(tq, tk), s, NEG_INF)
                p = jnp.exp(s - _lane_column(lse_q, hh * HEAD_DIM))
                dp = lax.dot_general(doh, v2, NT, preferred_element_type=F32)
                ds = p * (dp - delta)
                dsb = ds.astype(BF16)
                dv_h = lax.dot_general(p.astype(BF16), doh, TN, preferred_element_type=F32)
                dk_h = lax.dot_general(dsb, qh, TN, preferred_element_type=F32)
                dq_h = jnp.dot(dsb, kh, preferred_element_type=F32)
                dq_part = dq_h if dq_part is None else dq_part + dq_h
                dk_part = dk_h if dk_part is None else dk_part + dk_h
                dv_part = dv_h if dv_part is None else dv_part + dv_h
                dck_h = -jnp.sum(ds, axis=0, keepdims=True)
                dcq_part = jnp.where(lane == PAIR * hp + hh, jnp.sum(ds, axis=-1, keepdims=True), dcq_part)
                if diagonal:
                    dck_ref[hh] = dck_h
                else:
                    dck_ref[hh] += dck_h
            dq_part = dq_part * scale

            @pl.when(j == 0)
            def _():
                dq_ref[rows, :] = dq_part

            @pl.when(j > 0)
            def _():
                dq_ref[rows, :] += dq_part

            @pl.when(jnp.logical_and(hp == 0, j == 0))
            def _():
                dcq_ref[rows, :] = dcq_part

            @pl.when(jnp.logical_or(hp > 0, j > 0))
            def _():
                dcq_ref[rows, :] += dcq_part

            if diagonal:
                dk_acc[...] = dk_part
                dv_acc[...] = dv_part
            else:
                dk_acc[...] += dk_part
                dv_acc[...] += dv_part

        block(j, True)

        def later(i, carry):
            block(i, False)
            return carry

        lax.fori_loop(j + 1, nq, later, 0)
        dk_ref[...] = dk_acc[...].astype(BF16)
        dv_ref[...] = dv_acc[...].astype(BF16)

    at_q = lambda col: pl.BlockSpec((T, LANES), lambda hp, j: (0, col + hp))
    at_k = lambda col: pl.BlockSpec((tk, LANES), lambda hp, j: (j, col + hp))
    lse_spec = pl.BlockSpec((None, T, LANES), lambda hp, j: (hp, 0, 0))
    cum_spec = pl.BlockSpec((T, LANES), lambda hp, j: (0, 0))
    ck_spec = pl.BlockSpec((PAIR, None, 1, tk), lambda hp, j: (hp, j, 0, 0))
    return _pcall(
        body, name="attn_bwd", grid=(H // PAIR, nkv),
        in_specs=[at_q(qb), at_k(kb), at_k(vb), at_q(0), at_q(dob), lse_spec, cum_spec, ck_spec],
        out_specs=[pl.BlockSpec((T, LANES), lambda hp, j: (0, hp)), at_k(0), at_k(0),
                   pl.BlockSpec((T, LANES), lambda hp, j: (0, 0)), ck_spec],
        out_shape=[jax.ShapeDtypeStruct((T, hd), F32), jax.ShapeDtypeStruct((T, hd), BF16),
                   jax.ShapeDtypeStruct((T, hd), BF16),
                   jax.ShapeDtypeStruct((T, LANES), F32), jax.ShapeDtypeStruct((H, nkv, 1, tk), F32)],
        scratch=[pltpu.VMEM((tk, LANES), F32), pltpu.VMEM((tk, LANES), F32)],
        comm=comm,
    )(proj, proj, proj, o, dycat, lse, cum, ck4)


ELEMENTWISE_BLOCK_BYTES = 2 * 1024 * 1024
BF16_ROWS = 16


def cast_bf16(arrays, comm=None):
    def slab(a, steps):
        R, C = a.shape
        if R % (steps * BF16_ROWS) == 0:
            return pl.BlockSpec((R // steps, C), lambda i: (i, 0))
        if C % (steps * LANES) == 0:
            return pl.BlockSpec((R, C // steps), lambda i: (0, i))
        return None

    steps = 8 if all(slab(a, 8) is not None for a in arrays) else 4
    specs = [slab(a, steps) for a in arrays]
    n = len(arrays)

    def body(*refs):
        for src, dst in zip(refs[:n], refs[n:]):
            dst[...] = src[...].astype(BF16)

    return _pcall(body, name="cast_bf16", grid=(steps,), in_specs=specs, out_specs=specs,
                  out_shape=[jax.ShapeDtypeStruct(a.shape, BF16) for a in arrays], comm=comm)(*arrays)


def _ew_tiles(rows, cols, bytes_per_element):
    target = max(8, ELEMENTWISE_BLOCK_BYTES // max(1, cols * bytes_per_element))
    if rows <= target:
        return rows, cols
    t = (target // 16) * 16
    while t >= 16:
        if rows % t == 0:
            return t, cols
        t -= 16
    tc = _tile(cols, max(LANES, (ELEMENTWISE_BLOCK_BYTES // (rows * bytes_per_element)) // LANES * LANES))
    return rows, tc


def sum_chips(recv):
    nc, R, C = recv.shape
    tr, tc = _ew_tiles(R, C, 4)

    def body(r_ref, o_ref):
        acc = r_ref[0].astype(F32)
        for j in range(1, nc):
            acc = acc + r_ref[j].astype(F32)
        o_ref[...] = acc

    return _pcall(
        body, name="sum_chips", grid=(R // tr, C // tc),
        in_specs=[pl.BlockSpec((nc, tr, tc), lambda i, j: (0, i, j))],
        out_specs=[pl.BlockSpec((tr, tc), lambda i, j: (i, j))],
        out_shape=[jax.ShapeDtypeStruct((R, C), F32)],
    )(recv)[0]


def add_sibling_half(g, recv):
    nc, R, C = g.shape
    hr = R // 2
    tr, tc = _ew_tiles(hr // 2, C, 4 * nc)
    per_quarter = (hr // 2) // tr

    def body(g_ref, r_ref, o_ref):
        c = lax.axis_index("c")
        for j in range(nc):
            o_ref[j] = (g_ref[j, c].astype(F32) + r_ref[j].astype(F32)).astype(BF16)

    return _pcall(
        body, name="add_sibling_half", grid=(hr // tr, C // tc),
        in_specs=[pl.BlockSpec((nc, 2, tr, tc), lambda i, j: (0, 0, i, j)),
                  pl.BlockSpec((nc, tr, tc), lambda i, j: (0, i, j))],
        out_specs=[pl.BlockSpec((None, nc, tr, tc), lambda i, j: (i // per_quarter, 0, i % per_quarter, j))],
        out_shape=[jax.ShapeDtypeStruct((2, nc, hr // 2, C), BF16)],
    )(g.reshape(nc, 2, hr, C), recv)[0]


def adamw(w, m, v, g_parts, comm=None, halves=False):
    R, C = w.shape
    tr, tc = _ew_tiles(R // 2 if halves else R, C, 4)
    n_g = len(g_parts)
    n_half = (R // 2) // tr
    c1 = 1.0 - ADAM_B1
    c2 = 1.0 - ADAM_B2
    bc1 = 1.0 - ADAM_B1 ** ADAM_STEP
    bc2 = 1.0 - ADAM_B2 ** ADAM_STEP

    def body(*refs):
        w_ref, m_ref, v_ref = refs[:3]
        g_refs = refs[3:3 + n_g]
        g_out, d_out, m_out, v_out = refs[3 + n_g:]
        if halves:
            mine = (pl.program_id(0) >= n_half) == (lax.axis_index("c") == 1)
            g = jnp.where(mine, g_refs[0][...], g_refs[1][...])
        else:
            g = g_refs[0][...]
            for r in g_refs[1:]:
                g = g + r[...]
        m_new = ADAM_B1 * m_ref[...] + c1 * g
        v_new = ADAM_B2 * v_ref[...] + c2 * (g * g)
        m_hat = m_new / bc1
        v_hat = v_new / bc2
        g_out[...] = g
        d_out[...] = -ADAM_LR * (m_hat / (jnp.sqrt(v_hat) + ADAM_EPS) + ADAM_WD * w_ref[...])
        m_out[...] = m_new
        v_out[...] = v_new

    spec = pl.BlockSpec((tr, tc), lambda i, j: (i, j))
    g_spec = pl.BlockSpec((tr, tc), lambda i, j: (i % n_half, j)) if halves else spec
    return _pcall(
        body, name="adamw", grid=(R // tr, C // tc),
        in_specs=[spec] * 3 + [g_spec] * n_g, out_specs=[spec] * 4,
        out_shape=[jax.ShapeDtypeStruct((R, C), F32)] * 4, comm=comm,
    )(w, m, v, *g_parts)


def _chip_coords():
    x, y, c = lax.axis_index("x"), lax.axis_index("y"), lax.axis_index("c")
    others = [(1 - x, y), (x, 1 - y), (1 - x, 1 - y)]
    return x, y, c, others


def _remote(src, dst, send_sem, recv_sem, device):
    return pltpu.make_async_remote_copy(src_ref=src, dst_ref=dst, send_sem=send_sem, recv_sem=recv_sem,
                                        device_id=device, device_id_type=MESH)


def gather_comm(shards):
    n = len(shards)
    SLOTS = 7

    def makers(ins, outs, sems):
        send_sems, recv_sems, local_sems = sems
        x, y, c, _ = _chip_coords()
        me, xn, yn, dg = 2 * x + y, 2 * (1 - x) + y, 2 * x + (1 - y), 2 * (1 - x) + (1 - y)
        to_x, to_y, sibling = (1 - x, y, c), (x, 1 - y, c), (x, y, 1 - c)

        def part(ref, a, half, quarter=None, chip=None):
            rows, cols = ins[a].shape[0], ins[a].shape[1]
            lead = () if chip is None else (chip,)
            along_rows = rows % (4 * BF16_ROWS) == 0 or (ins[a].dtype == F32 and rows % (4 * SUBLANES) == 0)
            size = (rows if along_rows else cols) // 2
            start = half * size
            if quarter is not None:
                size = size // 2
                start = start + quarter * size
            if along_rows:
                return ref.at[(*lead, pl.ds(start, size))]
            return ref.at[(*lead, slice(None), pl.ds(start, size))]

        def copy(a, k, src, dst, device):
            return _remote(src, dst, send_sems.at[SLOTS * a + k], recv_sems.at[SLOTS * a + k], device)

        def local(a):
            return pltpu.make_async_copy(ins[a], outs[a].at[me], local_sems.at[a])

        def first_leg(a):
            mine = part(outs[a], a, c, chip=me)
            return [copy(a, 0, part(ins[a], a, c), mine, to_x), copy(a, 1, part(ins[a], a, c), mine, to_y)]

        def arrived(a, k):
            region = {0: part(outs[a], a, c, chip=xn), 1: part(outs[a], a, c, chip=yn),
                      2: part(outs[a], a, c, 0, chip=dg), 3: part(outs[a], a, c, 1, chip=dg),
                      4: part(outs[a], a, 1 - c, chip=xn), 5: part(outs[a], a, 1 - c, chip=yn),
                      6: part(outs[a], a, 1 - c, chip=dg)}[k]
            return copy(a, k, region, region, sibling if k >= 4 else (to_x if k in (0, 3) else to_y))

        def relays(a):
            qx, qy = part(outs[a], a, c, 0, chip=xn), part(outs[a], a, c, 1, chip=yn)
            return [copy(a, 2, qx, qx, to_y), copy(a, 3, qy, qy, to_x)]

        def handover(a, k):
            region = part(outs[a], a, c, chip={4: xn, 5: yn, 6: dg}[k])
            return copy(a, k, region, region, sibling)

        return local, first_leg, arrived, relays, handover

    def start(ins, outs, sems):
        local, first_leg, _, _, _ = makers(ins, outs, sems)
        for a in range(n):
            for cp in first_leg(a):
                cp.start()
        for a in range(n):
            local(a).start()

    def relay(ins, outs, sems):
        _, _, arrived, relays, handover = makers(ins, outs, sems)
        for a in range(n):
            to_y_nbr, to_x_nbr = relays(a)
            arrived(a, 0).wait_recv()
            to_y_nbr.start()
            handover(a, 4).start()
            arrived(a, 1).wait_recv()
            to_x_nbr.start()
            handover(a, 5).start()

    def finish(ins, outs, sems):
        local, first_leg, arrived, relays, handover = makers(ins, outs, sems)
        for a in range(n):
            arrived(a, 2).wait_recv()
            arrived(a, 3).wait_recv()
            handover(a, 6).start()
        for a in range(n):
            for k in (4, 5, 6):
                arrived(a, k).wait_recv()
        for a in range(n):
            for cp in first_leg(a) + relays(a) + [handover(a, k) for k in (4, 5, 6)]:
                cp.wait_send()
            local(a).wait()

    return Comm(shards, [jax.ShapeDtypeStruct((N_CHIP,) + s.shape, s.dtype) for s in shards],
                [pltpu.SemaphoreType.DMA((SLOTS * n,)), pltpu.SemaphoreType.DMA((SLOTS * n,)),
                 pltpu.SemaphoreType.DMA((n,))], start, finish, relay)


def scatter_comm(grads):
    n = len(grads)
    pieces = [(a, jj) for a in range(n) for jj in range(3)]

    def makers(ins, outs, sems):
        send_sems, recv_sems, local_sems = sems
        x, y, c, others = _chip_coords()
        me = 2 * x + y

        def local(a):
            return pltpu.make_async_copy(ins[a].at[me], outs[a].at[me], local_sems.at[a])

        def ici(a, jj):
            ox, oy = others[jj]
            return _remote(ins[a].at[2 * ox + oy], outs[a].at[me], send_sems.at[3 * a + jj],
                           recv_sems.at[3 * a + jj], (ox, oy, c))

        def landed(a, jj):
            ox, oy = others[jj]
            slot = outs[a].at[2 * ox + oy]
            return _remote(slot, slot, send_sems.at[3 * a + jj], recv_sems.at[3 * a + jj], (ox, oy, c))

        return local, ici, landed

    def start(ins, outs, sems):
        local, ici, _ = makers(ins, outs, sems)
        for a in range(n):
            for jj in (2, 0, 1):
                ici(a, jj).start()
        for a in range(n):
            local(a).start()

    def finish(ins, outs, sems):
        local, ici, landed = makers(ins, outs, sems)
        for a, jj in pieces:
            landed(a, jj).wait_recv()
        for a, jj in pieces:
            ici(a, jj).wait_send()
        for a in range(n):
            local(a).wait()

    return Comm(grads, [jax.ShapeDtypeStruct(g.shape, g.dtype) for g in grads],
                [pltpu.SemaphoreType.DMA((3 * n,)), pltpu.SemaphoreType.DMA((3 * n,)),
                 pltpu.SemaphoreType.DMA((n,))], start, finish)


def halfswap_comm(grads):
    n = len(grads)

    def copies(ins, outs, sems):
        send_sems, recv_sems = sems
        x, y, c, _ = _chip_coords()
        out = []
        for a in range(n):
            hr = ins[a].shape[1] // 2
            out.append(_remote(ins[a].at[:, pl.ds((1 - c) * hr, hr)], outs[a], send_sems.at[a], recv_sems.at[a],
                               (x, y, 1 - c)))
        return out

    def start(ins, outs, sems):
        for cp in copies(ins, outs, sems):
            cp.start()

    def finish(ins, outs, sems):
        for cp in copies(ins, outs, sems):
            cp.wait()

    return Comm(grads, [jax.ShapeDtypeStruct((g.shape[0], g.shape[1] // 2, g.shape[2]), g.dtype) for g in grads],
                [pltpu.SemaphoreType.DMA((n,)), pltpu.SemaphoreType.DMA((n,))], start, finish)


def join_comms(first, second):
    ni, no, ns = len(first.operands), len(first.out_shape), len(first.sems)

    def start(ins, outs, sems):
        first.start(ins[:ni], outs[:no], sems[:ns])
        second.start(ins[ni:], outs[no:], sems[ns:])

    def finish(ins, outs, sems):
        first.finish(ins[:ni], outs[:no], sems[:ns])
        second.finish(ins[ni:], outs[no:], sems[ns:])

    def relay(ins, outs, sems):
        if first.relay is not None:
            first.relay(ins[:ni], outs[:no], sems[:ns])
        if second.relay is not None:
            second.relay(ins[ni:], outs[no:], sems[ns:])

    return Comm(first.operands + second.operands, first.out_shape + second.out_shape, first.sems + second.sems,
                start, finish, relay if (first.relay or second.relay) else None)


def swap_comm(parts):
    n = len(parts)

    def copies(ins, outs, sems):
        send_sems, recv_sems = sems
        x, y, c, _ = _chip_coords()
        return [_remote(ins[a], outs[a], send_sems.at[a], recv_sems.at[a], (x, y, 1 - c)) for a in range(n)]

    def start(ins, outs, sems):
        for cp in copies(ins, outs, sems):
            cp.start()

    def finish(ins, outs, sems):
        for cp in copies(ins, outs, sems):
            cp.wait()

    return Comm(parts, [jax.ShapeDtypeStruct(p.shape, p.dtype) for p in parts],
                [pltpu.SemaphoreType.DMA((n,)), pltpu.SemaphoreType.DMA((n,))], start, finish)


def allreduce_small(v):
    R = v.shape[0]

    def body(v_ref, sum_ref, all_ref, send_sems, recv_sems):
        x, y, c = lax.axis_index("x"), lax.axis_index("y"), lax.axis_index("c")
        me = 4 * x + 2 * y + c
        all_ref[me] = v_ref[...]
        copies = []
        for k in range(1, N_DEV):
            px = 1 - x if k & 4 else x
            py = 1 - y if k & 2 else y
            pc = 1 - c if k & 1 else c
            cp = pltpu.make_async_remote_copy(
                src_ref=v_ref, dst_ref=all_ref.at[me], send_sem=send_sems.at[k - 1], recv_sem=recv_sems.at[k - 1],
                device_id=(px, py, pc), device_id_type=MESH)
            cp.start()
            copies.append((cp, 4 * px + 2 * py + pc))
        for k, (cp, peer) in enumerate(copies):
            pltpu.make_async_remote_copy(
                src_ref=v_ref, dst_ref=all_ref.at[peer], send_sem=send_sems.at[k], recv_sem=recv_sems.at[k],
                device_id=(x, y, c), device_id_type=MESH).wait_recv()
        for cp, _ in copies:
            cp.wait_send()
        acc = all_ref[0]
        for d in range(1, N_DEV):
            acc = acc + all_ref[d]
        sum_ref[...] = acc

    vm = pl.BlockSpec(memory_space=pltpu.VMEM)
    return pl.pallas_call(
        body, name="allreduce_small",
        in_specs=[vm], out_specs=[vm, vm],
        out_shape=[jax.ShapeDtypeStruct((R, LANES), F32), jax.ShapeDtypeStruct((N_DEV, R, LANES), F32)],
        scratch_shapes=[pltpu.SemaphoreType.DMA((N_DEV - 1,)), pltpu.SemaphoreType.DMA((N_DEV - 1,))],
    )(v)[0]


SMALL_NAMES = ("ffn1_norm", "mix_norm", "ffn2_norm", "final_norm", "conv_b", "conv_ln_g", "conv_ln_b")


def _pack_small(vecs, bias, conv_w_rows, loss_tile):
    rows = [vecs[n].reshape(-1, LANES) for n in SMALL_NAMES]
    rows.append(bias.reshape(1, LANES))
    rows.append(conv_w_rows.reshape(-1, LANES))
    rows.append(loss_tile[0:1, :])
    packed = jnp.concatenate(rows, axis=0)
    pad = (-packed.shape[0]) % 8
    return jnp.pad(packed, ((0, pad), (0, 0)))


def _unpack_small(packed, sizes, n_conv_rows):
    out, r = {}, 0
    for n in SMALL_NAMES:
        k = sizes[n] // LANES
        out[n] = packed[r:r + k].reshape(-1)
        r += k
    out["fgate_bias"] = packed[r]
    r += 1
    out["conv_w"] = packed[r:r + n_conv_rows]
    r += n_conv_rows
    out["loss"] = packed[r, 0]
    return out


def kernel(x, ffn1_norm, ffn1_w_gate, ffn1_w_up, ffn1_w_down, mix_norm, w_in, fgate_bias, conv_w, conv_b, conv_ln_g, conv_ln_b, w_out, ffn2_norm, ffn2_w_gate, ffn2_w_up, ffn2_w_down, final_norm, loss_target, m_ffn1_norm, m_ffn1_w_gate, m_ffn1_w_up, m_ffn1_w_down, m_mix_norm, m_w_in, m_fgate_bias, m_conv_w, m_conv_b, m_conv_ln_g, m_conv_ln_b, m_w_out, m_ffn2_norm, m_ffn2_w_gate, m_ffn2_w_up, m_ffn2_w_down, m_final_norm, v_ffn1_norm, v_ffn1_w_gate, v_ffn1_w_up, v_ffn1_w_down, v_mix_norm, v_w_in, v_fgate_bias, v_conv_w, v_conv_b, v_conv_ln_g, v_conv_ln_b, v_w_out, v_ffn2_norm, v_ffn2_w_gate, v_ffn2_w_up, v_ffn2_w_down, v_final_norm):
    w = dict(ffn1_norm=ffn1_norm, ffn1_w_gate=ffn1_w_gate, ffn1_w_up=ffn1_w_up, ffn1_w_down=ffn1_w_down,
             mix_norm=mix_norm, w_in=w_in, fgate_bias=fgate_bias, conv_w=conv_w, conv_b=conv_b,
             conv_ln_g=conv_ln_g, conv_ln_b=conv_ln_b, w_out=w_out, ffn2_norm=ffn2_norm,
             ffn2_w_gate=ffn2_w_gate, ffn2_w_up=ffn2_w_up, ffn2_w_down=ffn2_w_down, final_norm=final_norm)
    m = dict(ffn1_norm=m_ffn1_norm, ffn1_w_gate=m_ffn1_w_gate, ffn1_w_up=m_ffn1_w_up, ffn1_w_down=m_ffn1_w_down,
             mix_norm=m_mix_norm, w_in=m_w_in, fgate_bias=m_fgate_bias, conv_w=m_conv_w, conv_b=m_conv_b,
             conv_ln_g=m_conv_ln_g, conv_ln_b=m_conv_ln_b, w_out=m_w_out, ffn2_norm=m_ffn2_norm,
             ffn2_w_gate=m_ffn2_w_gate, ffn2_w_up=m_ffn2_w_up, ffn2_w_down=m_ffn2_w_down, final_norm=m_final_norm)
    v = dict(ffn1_norm=v_ffn1_norm, ffn1_w_gate=v_ffn1_w_gate, ffn1_w_up=v_ffn1_w_up, ffn1_w_down=v_ffn1_w_down,
             mix_norm=v_mix_norm, w_in=v_w_in, fgate_bias=v_fgate_bias, conv_w=v_conv_w, conv_b=v_conv_b,
             conv_ln_g=v_conv_ln_g, conv_ln_b=v_conv_ln_b, w_out=v_w_out, ffn2_norm=v_ffn2_norm,
             ffn2_w_gate=v_ffn2_w_gate, ffn2_w_up=v_ffn2_w_up, ffn2_w_down=v_ffn2_w_down, final_norm=v_final_norm)
    names = list(w.keys())
    big = ("ffn1_w_gate", "ffn1_w_up", "ffn1_w_down", "w_in", "w_out", "ffn2_w_gate", "ffn2_w_up", "ffn2_w_down")

    T, D = x.shape[1], x.shape[2]
    C = conv_b.shape[0]
    H = fgate_bias.shape[0]
    cs = conv_w.shape[1]
    in_cols = N_CHIP * w_in.shape[1]
    p_main = in_cols - H

    x0, tgt = x[0], loss_target[0]
    tk = _tile(T, 512, 128)
    nkv = T // tk
    row = lambda a: a.reshape(1, -1)
    grad, delta, new_m, new_v = {}, {}, {}, {}

    def update(n, parts, comm=None, halves=False):
        args = (w[n], m[n], v[n])
        if n == "w_in":
            outs = [t.T for t in adamw(*[a.T for a in args], parts, comm=comm)]
        else:
            outs = adamw(*args, parts, comm=comm, halves=halves)
        grad[n], delta[n], new_m[n], new_v[n] = outs

    rest = [n for n in big if n != "ffn1_w_gate"]
    g0 = gather_comm([w["ffn1_w_gate"].astype(BF16), jnp.pad(conv_w, ((0, HALO - CONV_K), (0, 0)))])
    wb = dict(zip(rest, cast_bf16([w[n].T if n == "w_in" else w[n] for n in rest], comm=g0)))
    wg1, conv_w4 = g0.results
    conv_w_full = conv_w4.transpose(1, 0, 2).reshape(HALO, C)
    h1, r1 = rms_fwd(x0, row(ffn1_norm))
    g1a = gather_comm([wb["ffn1_w_up"]])
    a1 = ffn_gate(h1, wg1, comm=g1a)
    wu1 = g1a.results[0]
    g1b = gather_comm([wb["ffn1_w_down"]])
    b1, mid1 = ffn_upmul(h1, wu1, a1, comm=g1b)
    wd1 = g1b.results[0]
    g2 = gather_comm([wb["w_in"]])
    x1 = mm_residual("ffn_down_g", mid1, wd1, x0, 0.5, comm=g2)[0]
    w_t = g2.results[0].reshape(in_cols, D)

    wf_t = jnp.pad(w_t[p_main:], ((0, LANES - H), (0, 0)))
    bias_pad = jnp.pad(row(fgate_bias), ((0, 0), (0, LANES - H)))
    h2, r2 = rms_fwd(x1, row(mix_norm))
    g_out = gather_comm([wb["w_out"]])
    proj = proj_main(h2, w_t, p_main, comm=g_out)
    w_out3 = g_out.results[0]
    f, cum = fgate_fwd(h2, wf_t, bias_pad, H)
    ypre, yconv = conv_fwd(proj, conv_w_full, row(conv_b), row(conv_ln_g), row(conv_ln_b))
    ck4 = cum[:, :H].T.reshape(H, nkv, 1, tk)
    g3 = gather_comm([wb["ffn2_w_gate"], wb["ffn2_w_up"]])
    o, lse = attn_fwd(proj, cum, ck4, 2 * C, comm=g3)
    wg2, wu2 = g3.results
    ycat = jnp.concatenate([yconv, o.astype(BF16)], axis=1)
    x2 = mm_residual("out_proj", ycat, w_out3.reshape(2, -1, D), x1, 1.0)[0]

    h3, r3 = rms_fwd(x2, row(ffn2_norm))
    g4 = gather_comm([wb["ffn2_w_down"]])
    a2, b2, mid2 = ffn_up(h3, wg2, wu2, comm=g4)
    wd2 = g4.results[0]
    x3 = mm_residual("ffn_down", mid2, wd2, x2, 0.5)[0]
    dx3, dx3b, loss_tile, d_final = final_loss(x3, tgt, row(final_norm))

    da2, db2 = ffn_bwd_mid(dx3b, wd2, a2, b2)
    dwd2 = dw_rowshard("ffn_dwd", mid2, dx3b, N_CHIP)[0]
    s1 = scatter_comm([dwd2])
    dwg2, dwu2 = dw_colshard("ffn_dwgu_s", h3, [da2, db2], N_CHIP, comm=s1)
    s2 = scatter_comm([dwg2])
    dh3 = ffn_dh(da2, db2, wg2, wu2, comm=s2)
    dx2, dx2b, d_ffn2_norm = rms_bwd(dh3, x2, r3, row(ffn2_norm), dx3, 1.0)

    dycat = mm_nt_bf16("out_proj_dy", dx2b, w_out3.reshape(-1, D))
    dw_out3 = dw_rowshard("out_proj_dw", ycat, dx2b, N_CHIP)[0]
    s3 = scatter_comm([dwu2, dw_out3])
    dq, dk, dv, dcq, dck4 = attn_bwd(proj, o, dycat, lse, cum, ck4, 2 * C, C, comm=s3)
    dc = dcq + jnp.pad(dck4.reshape(H, T).T, ((0, 0), (0, LANES - H)))
    df, d_bias = fgate_bwd(dc, f, H)
    dag, d_conv_w, d_conv_b, d_ln_g, d_ln_b = conv_bwd(proj, ypre, dycat, conv_w_full, row(conv_ln_g),
                                                       row(conv_ln_b))
    dproj = jnp.concatenate([dag, dq.astype(BF16), dk, dv], axis=1)
    early = ("ffn2_w_down", "ffn2_w_gate", "ffn2_w_up", "w_out")
    early_sums = [sum_chips(r) for r in (s1.results[0], s2.results[0], s3.results[0], s3.results[1])]
    sw1 = swap_comm(early_sums)
    dh2 = proj_dh(dproj, w_t, df, wf_t, comm=sw1)
    dw_t, dwf_t = proj_dw(dproj, df, h2, in_cols)
    gate_rows = dwf_t[:H].astype(BF16).reshape(H, 2, D // 2).transpose(1, 0, 2)
    dw_t = lax.dynamic_update_slice(dw_t, gate_rows, (0, p_main, 0))
    dw_in_halves = [dw_t[half].reshape(N_CHIP, in_cols // N_CHIP, D // 2) for half in range(2)]
    dx1, dx1b, d_mix_norm = rms_bwd(dh2, x1, r2, row(mix_norm), dx2, 0.5)

    s4a = scatter_comm([dw_in_halves[0]])
    da1, db1 = ffn_bwd_mid(dx1b, wd1, a1, b1, comm=s4a)
    s4b = scatter_comm([dw_in_halves[1]])
    dwd1 = dw_rowshard("ffn_dwd_s", mid1, dx1b, N_CHIP, comm=s4b)[0]
    s5 = scatter_comm([dwd1])
    dwg1, dwu1 = dw_colshard("ffn_dwgu_s", h1, [da1, db1], N_CHIP, comm=s5)
    sum_in = jnp.concatenate([sum_chips(s4a.results[0]), sum_chips(s4b.results[0])], axis=1)
    mid_sums = [sum_in, sum_chips(s5.results[0])]
    s6 = join_comms(join_comms(scatter_comm([dwg1]), halfswap_comm([dwu1])), swap_comm(mid_sums))
    dh1 = ffn_dh(da1, db1, wg1, wu1, comm=s6)
    recv_g1, sibling_u1, their_in, their_d1 = s6.results
    quarters = add_sibling_half(dwu1, sibling_u1)
    sum_g1 = sum_chips(recv_g1)
    s7 = [join_comms(scatter_comm([quarters[0]]), swap_comm([sum_g1])), scatter_comm([quarters[1]])]
    grad_x, _, d_ffn1_norm = rms_bwd(dh1, x0, r1, row(ffn1_norm), dx1, 1.0, comm=s7[0])
    their_g1 = s7[0].results[1]

    for i, (n, mine, other) in enumerate(zip(early, early_sums, sw1.results)):
        update(n, [mine, other], comm=s7[1] if i == 0 else None)
    update("w_in", [mid_sums[0], their_in])
    update("ffn1_w_down", [mid_sums[1], their_d1])
    half_u1 = jnp.concatenate([sum_chips(s7[0].results[0]), sum_chips(s7[1].results[0])], axis=0)
    their_u1 = _run_comm("swap_last", swap_comm([half_u1]))[0]
    update("ffn1_w_gate", [sum_g1, their_g1])
    update("ffn1_w_up", [half_u1, their_u1], halves=True)

    gl = dict(ffn1_norm=d_ffn1_norm, mix_norm=d_mix_norm, ffn2_norm=d_ffn2_norm, final_norm=d_final,
              conv_b=d_conv_b, conv_ln_g=d_ln_g, conv_ln_b=d_ln_b)
    small_sizes = {n: w[n].shape[0] for n in SMALL_NAMES}
    packed = _pack_small(gl, d_bias, d_conv_w, loss_tile)
    red = _unpack_small(allreduce_small(packed), small_sizes, HALO * C // LANES)
    loss = red["loss"]
    my_chip = 2 * lax.axis_index("x") + lax.axis_index("y")
    g_conv_w = lax.dynamic_slice_in_dim(red["conv_w"].reshape(HALO, C)[:CONV_K], my_chip * cs, cs, axis=1)
    update("conv_w", [g_conv_w])
    vec_names = SMALL_NAMES + ("fgate_bias",)
    stack = lambda d: jnp.concatenate(
        [jnp.pad(d[n], (0, (-d[n].shape[0]) % LANES)).reshape(-1, LANES) for n in vec_names], axis=0)
    g_stack = jnp.concatenate([red[n].reshape(-1, LANES) for n in SMALL_NAMES] + [red["fgate_bias"][None, :]],
                              axis=0)
    outs = adamw(stack(w), stack(m), stack(v), [g_stack])
    r = 0
    for n in vec_names:
        size = w[n].shape[0]
        k = -(-size // LANES)
        for dst, src in zip((grad, delta, new_m, new_v), outs):
            dst[n] = src[r:r + k].reshape(-1)[:size]
        r += k

    return (loss, grad_x[None], *[grad[n] for n in names], *[delta[n] for n in names],
            *[new_m[n] for n in names], *[new_v[n] for n in names])
```

```python
import functools
import math

import jax
import jax.numpy as jnp
from jax import lax
from jax.experimental import pallas as pl
from jax.experimental.pallas import tpu as pltpu

F32 = jnp.float32
BF16 = jnp.bfloat16
NORM_EPS = 1e-6
LN_EPS = 1e-5
NEG_INF = -1e30
HEAD_DIM = 64
CONV_K = 31
HALO = 32
LANES = 128
N_CHIP = 4
N_DEV = 8
VMEM_LIMIT = 52 * 1024 * 1024
MESH = pl.DeviceIdType.MESH

ADAM_LR = 0.001
ADAM_B1 = 0.9
ADAM_B2 = 0.999
ADAM_EPS = 1e-08
ADAM_WD = 0.01
ADAM_STEP = 10

NN = (((1,), (0,)), ((), ()))
NT = (((1,), (1,)), ((), ()))
TN = (((0,), (0,)), ((), ()))


def _tile(n, pref, unit=128):
    if n <= pref:
        return n
    t = (pref // unit) * unit
    while t > 0:
        if n % t == 0:
            return t
        t -= unit
    raise ValueError(f"no tile for {n} under {pref}")


RELAY_AT = (3, 4)


class Comm:
    def __init__(self, operands, out_shape, sems, start, finish, relay=None):
        self.operands, self.out_shape, self.sems = list(operands), list(out_shape), list(sems)
        self.start, self.finish, self.relay = start, finish, relay
        self.results = None


def _pcall(body, *, name, grid, in_specs, out_specs, out_shape, scratch=(), comm=None):
    params = pltpu.CompilerParams(dimension_semantics=("arbitrary",) * len(grid), vmem_limit_bytes=VMEM_LIMIT)
    scratch = list(scratch)
    if comm is None:
        return pl.pallas_call(body, name=name, grid=grid, in_specs=in_specs, out_specs=out_specs,
                              out_shape=out_shape, scratch_shapes=scratch, compiler_params=params)
    n_in, n_out, n_s = len(in_specs), len(out_shape), len(scratch)
    n_ci, n_co = len(comm.operands), len(comm.out_shape)
    any_spec = pl.BlockSpec(memory_space=pl.ANY)

    def carried(*refs):
        ins, refs = refs[:n_in], refs[n_in:]
        c_ins, refs = refs[:n_ci], refs[n_ci:]
        outs, refs = refs[:n_out], refs[n_out:]
        c_outs, refs = refs[:n_co], refs[n_co:]
        scr, c_sems = refs[:n_s], refs[n_s:]
        step = pl.program_id(0)
        for d in range(1, len(grid)):
            step = step * grid[d] + pl.program_id(d)
        total = math.prod(grid)
        first, last = step == 0, step == total - 1

        @pl.when(first)
        def _():
            comm.start(c_ins, c_outs, c_sems)

        if comm.relay is not None:
            @pl.when(step == min(total - 1, (RELAY_AT[0] * total) // RELAY_AT[1]))
            def _():
                comm.relay(c_ins, c_outs, c_sems)

        body(*ins, *outs, *scr)

        @pl.when(last)
        def _():
            comm.finish(c_ins, c_outs, c_sems)

    call = pl.pallas_call(
        carried, name=name, grid=grid, in_specs=list(in_specs) + [any_spec] * n_ci,
        out_specs=list(out_specs) + [any_spec] * n_co, out_shape=list(out_shape) + comm.out_shape,
        scratch_shapes=scratch + comm.sems, compiler_params=params)

    def run(*operands):
        res = call(*operands, *comm.operands)
        comm.results = list(res[n_out:])
        return list(res[:n_out])

    return run


def _run_comm(name, comm):
    n_ci, n_co = len(comm.operands), len(comm.out_shape)
    any_spec = pl.BlockSpec(memory_space=pl.ANY)

    def body(*refs):
        c_ins, c_outs, c_sems = refs[:n_ci], refs[n_ci:n_ci + n_co], refs[n_ci + n_co:]
        comm.start(c_ins, c_outs, c_sems)
        if comm.relay is not None:
            comm.relay(c_ins, c_outs, c_sems)
        comm.finish(c_ins, c_outs, c_sems)

    return pl.pallas_call(body, name=name, in_specs=[any_spec] * n_ci, out_specs=[any_spec] * n_co,
                          out_shape=comm.out_shape, scratch_shapes=comm.sems)(*comm.operands)


def _sigmoid(x):
    return 1.0 / (1.0 + jnp.exp(-x))


def _mm(name, *, grid, pairs, once_pairs=(), extra=(), out_shape, out_specs, acc_shapes, nk, kaxis, epilogue,
        comm=None):
    all_pairs = list(pairs) + list(once_pairs)
    n_p, n_o = len(pairs), len(once_pairs)
    n_e, n_out, n_acc = len(extra), len(out_shape), len(acc_shapes)

    def body(*refs):
        ab = refs[: 2 * (n_p + n_o)]
        ex = refs[2 * (n_p + n_o): 2 * (n_p + n_o) + n_e]
        outs = refs[2 * (n_p + n_o) + n_e: 2 * (n_p + n_o) + n_e + n_out]
        accs = refs[2 * (n_p + n_o) + n_e + n_out:]

        def dots(idx_range):
            vals = [None] * n_acc
            for p in idx_range:
                d = lax.dot_general(ab[2 * p][...], ab[2 * p + 1][...], all_pairs[p][4],
                                    preferred_element_type=F32)
                ai = all_pairs[p][5]
                vals[ai] = d if vals[ai] is None else vals[ai] + d
            return vals

        if nk == 1:
            vals = dots(range(n_p + n_o))
            epilogue(vals, ex, outs)
            return

        k = pl.program_id(kaxis)

        @pl.when(k == 0)
        def _():
            vals = dots(range(n_p + n_o))
            for ai in range(n_acc):
                accs[ai][...] = vals[ai]

        @pl.when(k > 0)
        def _():
            vals = dots(range(n_p))
            for ai in range(n_acc):
                if vals[ai] is not None:
                    accs[ai][...] += vals[ai]

        @pl.when(k == nk - 1)
        def _():
            epilogue([a[...] for a in accs], ex, outs)

    operands, in_specs = [], []
    for p in all_pairs:
        operands += [p[0], p[2]]
        in_specs += [p[1], p[3]]
    for arr, spec in extra:
        operands.append(arr)
        in_specs.append(spec)
    scratch = [pltpu.VMEM(s, F32) for s in acc_shapes] if nk > 1 else []
    return _pcall(body, name=name, grid=grid, in_specs=in_specs, out_specs=out_specs, out_shape=out_shape,
                  scratch=scratch, comm=comm)(*operands)


def rms_fwd(x, g):
    T, D = x.shape
    tt = _tile(T, 512, 8)

    def body(x_ref, g_ref, h_ref, r_ref):
        xv = x_ref[...]
        r = lax.rsqrt(jnp.mean(xv * xv, axis=-1, keepdims=True) + NORM_EPS)
        h_ref[...] = (xv * r * g_ref[...]).astype(BF16)
        r_ref[...] = r

    return _pcall(
        body, name="rms_fwd", grid=(T // tt,),
        in_specs=[pl.BlockSpec((tt, D), lambda i: (i, 0)), pl.BlockSpec((1, D), lambda i: (0, 0))],
        out_specs=[pl.BlockSpec((tt, D), lambda i: (i, 0)), pl.BlockSpec((tt, 1), lambda i: (i, 0))],
        out_shape=[jax.ShapeDtypeStruct((T, D), BF16), jax.ShapeDtypeStruct((T, 1), F32)],
    )(x, g)


def rms_bwd(dh, x, r, g, dres, out_scale, comm=None):
    T, D = x.shape
    tt = _tile(T, 256, 8)

    def body(dh_ref, x_ref, r_ref, g_ref, dres_ref, dx_ref, dxb_ref, dg_ref):
        i = pl.program_id(0)
        xh = x_ref[...] * r_ref[...]
        dhv = dh_ref[...]
        dxh = dhv * g_ref[...]
        dx = dres_ref[...] + r_ref[...] * (dxh - xh * jnp.mean(dxh * xh, axis=-1, keepdims=True))
        dx_ref[...] = dx
        dxb_ref[...] = (out_scale * dx).astype(BF16)
        part = jnp.sum(dhv * xh, axis=0, keepdims=True)

        @pl.when(i == 0)
        def _():
            dg_ref[...] = part

        @pl.when(i > 0)
        def _():
            dg_ref[...] += part

    row = pl.BlockSpec((tt, D), lambda i: (i, 0))
    return _pcall(
        body, name="rms_bwd", grid=(T // tt,),
        in_specs=[row, row, pl.BlockSpec((tt, 1), lambda i: (i, 0)), pl.BlockSpec((1, D), lambda i: (0, 0)), row],
        out_specs=[row, row, pl.BlockSpec((1, D), lambda i: (0, 0))],
        out_shape=[jax.ShapeDtypeStruct((T, D), F32), jax.ShapeDtypeStruct((T, D), BF16),
                   jax.ShapeDtypeStruct((1, D), F32)], comm=comm,
    )(dh, x, r, g, dres)


def final_loss(x, tgt, g):
    T, D = x.shape
    tt = _tile(T, 256, 8)

    def body(x_ref, t_ref, g_ref, dx_ref, dxb_ref, loss_ref, dg_ref):
        i = pl.program_id(0)
        xv = x_ref[...]
        r = lax.rsqrt(jnp.mean(xv * xv, axis=-1, keepdims=True) + NORM_EPS)
        xh = xv * r
        err = xh * g_ref[...] - t_ref[...]
        part_loss = 0.5 * jnp.sum(jnp.mean(err * err, axis=-1, keepdims=True), axis=0, keepdims=True)
        dy = err * (1.0 / D)
        dxh = dy * g_ref[...]
        dx = r * (dxh - xh * jnp.mean(dxh * xh, axis=-1, keepdims=True))
        dx_ref[...] = dx
        dxb_ref[...] = (0.5 * dx).astype(BF16)
        part_g = jnp.sum(dy * xh, axis=0, keepdims=True)
        part_l = jnp.broadcast_to(part_loss, (8, LANES))

        @pl.when(i == 0)
        def _():
            dg_ref[...] = part_g
            loss_ref[...] = part_l

        @pl.when(i > 0)
        def _():
            dg_ref[...] += part_g
            loss_ref[...] += part_l

    row = pl.BlockSpec((tt, D), lambda i: (i, 0))
    return _pcall(
        body, name="final_loss", grid=(T // tt,),
        in_specs=[row, row, pl.BlockSpec((1, D), lambda i: (0, 0))],
        out_specs=[row, row, pl.BlockSpec((8, LANES), lambda i: (0, 0)), pl.BlockSpec((1, D), lambda i: (0, 0))],
        out_shape=[jax.ShapeDtypeStruct((T, D), F32), jax.ShapeDtypeStruct((T, D), BF16),
                   jax.ShapeDtypeStruct((8, LANES), F32), jax.ShapeDtypeStruct((1, D), F32)],
    )(x, tgt, g)


def ffn_gate(h, wg3, comm=None):
    T, D = h.shape
    nc, _, fs = wg3.shape
    tm = _tile(T, 512, 8)

    def epilogue(vals, ex, outs):
        outs[0][...] = vals[0].astype(BF16)

    return _mm("ffn_gate", grid=(nc, T // tm),
               pairs=[(h, pl.BlockSpec((tm, D), lambda j, i: (i, 0)),
                       wg3, pl.BlockSpec((None, D, fs), lambda j, i: (j, 0, 0)), NN, 0)],
               out_shape=[jax.ShapeDtypeStruct((T, nc * fs), BF16)],
               out_specs=[pl.BlockSpec((tm, fs), lambda j, i: (i, j))],
               acc_shapes=[(tm, fs)], nk=1, kaxis=None, epilogue=epilogue, comm=comm)[0]


def ffn_upmul(h, wu3, a, comm=None):
    T, D = h.shape
    nc, _, fs = wu3.shape
    tm = _tile(T, 512, 8)

    def epilogue(vals, ex, outs):
        b = vals[0]
        av = ex[0][...].astype(F32)
        outs[0][...] = b.astype(BF16)
        outs[1][...] = (av * _sigmoid(av) * b).astype(BF16)

    t_spec = pl.BlockSpec((tm, fs), lambda j, i: (i, j))
    o_shape = jax.ShapeDtypeStruct((T, nc * fs), BF16)
    return _mm("ffn_upmul", grid=(nc, T // tm),
               pairs=[(h, pl.BlockSpec((tm, D), lambda j, i: (i, 0)),
                       wu3, pl.BlockSpec((None, D, fs), lambda j, i: (j, 0, 0)), NN, 0)],
               extra=[(a, t_spec)], out_shape=[o_shape] * 2, out_specs=[t_spec] * 2,
               acc_shapes=[(tm, fs)], nk=1, kaxis=None, epilogue=epilogue, comm=comm)


def ffn_up(h, wg3, wu3, comm=None):
    T, D = h.shape
    nc, _, fs = wg3.shape
    tm = _tile(T, 512, 8)

    def epilogue(vals, ex, outs):
        a, b = vals
        outs[0][...] = a.astype(BF16)
        outs[1][...] = b.astype(BF16)
        outs[2][...] = (a * _sigmoid(a) * b).astype(BF16)

    h_spec = pl.BlockSpec((tm, D), lambda j, i: (i, 0))
    w_spec = pl.BlockSpec((None, D, fs), lambda j, i: (j, 0, 0))
    o_spec = pl.BlockSpec((tm, fs), lambda j, i: (i, j))
    o_shape = jax.ShapeDtypeStruct((T, nc * fs), BF16)
    return _mm("ffn_up", grid=(nc, T // tm),
               pairs=[(h, h_spec, wg3, w_spec, NN, 0), (h, h_spec, wu3, w_spec, NN, 1)],
               out_shape=[o_shape] * 3, out_specs=[o_spec] * 3, acc_shapes=[(tm, fs)] * 2, nk=1, kaxis=None,
               epilogue=epilogue, comm=comm)


def mm_residual(name, a, b3, res, scale, comm=None):
    T = a.shape[0]
    nk, tk, N = b3.shape
    tm, tn = _tile(T, 512, 8), _tile(N, 2048)

    def epilogue(vals, ex, outs):
        outs[0][...] = ex[0][...] + scale * vals[0]

    return _mm(name, grid=(T // tm, N // tn, nk),
               pairs=[(a, pl.BlockSpec((tm, tk), lambda i, n, k: (i, k)),
                       b3, pl.BlockSpec((None, tk, tn), lambda i, n, k: (k, 0, n)), NN, 0)],
               extra=[(res, pl.BlockSpec((tm, tn), lambda i, n, k: (i, n)))],
               out_shape=[jax.ShapeDtypeStruct((T, N), F32)],
               out_specs=[pl.BlockSpec((tm, tn), lambda i, n, k: (i, n))],
               acc_shapes=[(tm, tn)], nk=nk, kaxis=2, epilogue=epilogue, comm=comm)


def ffn_bwd_mid(dout, wd3, a, b, comm=None):
    T, D = dout.shape
    nc, fs, _ = wd3.shape
    tm = _tile(T, 512, 8)

    def epilogue(vals, ex, outs):
        dm = vals[0]
        av = ex[0][...].astype(F32)
        bv = ex[1][...].astype(F32)
        s = _sigmoid(av)
        outs[0][...] = (dm * bv * (s * (1.0 + av * (1.0 - s)))).astype(BF16)
        outs[1][...] = (dm * (av * s)).astype(BF16)

    t_spec = pl.BlockSpec((tm, fs), lambda j, i: (i, j))
    o_shape = jax.ShapeDtypeStruct((T, nc * fs), BF16)
    return _mm("ffn_bwd_mid", grid=(nc, T // tm),
               pairs=[(dout, pl.BlockSpec((tm, D), lambda j, i: (i, 0)),
                       wd3, pl.BlockSpec((None, fs, D), lambda j, i: (j, 0, 0)), NT, 0)],
               extra=[(a, t_spec), (b, t_spec)],
               out_shape=[o_shape] * 2, out_specs=[t_spec] * 2, acc_shapes=[(tm, fs)], nk=1, kaxis=None,
               epilogue=epilogue, comm=comm)


def dw_rowshard(name, a, b, nc, comm=None):
    T, M = a.shape
    N = b.shape[1]
    ms = M // nc
    tn, tk = _tile(N, 2048 if ms <= 512 else 1024), _tile(T, 1024, 16)

    def epilogue(vals, ex, outs):
        outs[0][...] = vals[0].astype(BF16)

    return _mm(name, grid=(nc, N // tn, T // tk),
               pairs=[(a, pl.BlockSpec((tk, ms), lambda j, n, k: (k, j)),
                       b, pl.BlockSpec((tk, tn), lambda j, n, k: (k, n)), TN, 0)],
               out_shape=[jax.ShapeDtypeStruct((nc, ms, N), BF16)],
               out_specs=[pl.BlockSpec((None, ms, tn), lambda j, n, k: (j, 0, n))],
               acc_shapes=[(ms, tn)], nk=T // tk, kaxis=2, epilogue=epilogue, comm=comm)


def dw_colshard(name, a, bs, nc, comm=None):
    T, M = a.shape
    ns = bs[0].shape[1] // nc
    tm, tk = _tile(M, 512), _tile(T, 1024, 16)

    def epilogue(vals, ex, outs):
        for v, o in zip(vals, outs):
            o[...] = v.astype(BF16)

    a_spec = pl.BlockSpec((tk, tm), lambda j, m, k: (k, m))
    b_spec = pl.BlockSpec((tk, ns), lambda j, m, k: (k, j))
    return _mm(name, grid=(nc, M // tm, T // tk),
               pairs=[(a, a_spec, b, b_spec, TN, p) for p, b in enumerate(bs)],
               out_shape=[jax.ShapeDtypeStruct((nc, M, ns), BF16)] * len(bs),
               out_specs=[pl.BlockSpec((None, tm, ns), lambda j, m, k: (j, m, 0))] * len(bs),
               acc_shapes=[(tm, ns)] * len(bs), nk=T // tk, kaxis=2, epilogue=epilogue, comm=comm)


def ffn_dh(da, db, wg3, wu3, comm=None):
    T = da.shape[0]
    nc, D, fs = wg3.shape
    tm, tn = _tile(T, 512, 8), _tile(D, 1024)

    def epilogue(vals, ex, outs):
        outs[0][...] = vals[0]

    a_spec = pl.BlockSpec((tm, fs), lambda i, n, k: (i, k))
    w_spec = pl.BlockSpec((None, tn, fs), lambda i, n, k: (k, n, 0))
    return _mm("ffn_dh", grid=(T // tm, D // tn, nc),
               pairs=[(da, a_spec, wg3, w_spec, NT, 0), (db, a_spec, wu3, w_spec, NT, 0)],
               out_shape=[jax.ShapeDtypeStruct((T, D), F32)],
               out_specs=[pl.BlockSpec((tm, tn), lambda i, n, k: (i, n))],
               acc_shapes=[(tm, tn)], nk=nc, kaxis=2, epilogue=epilogue, comm=comm)[0]


def proj_main(h, w_t, P, comm=None):
    T, D = h.shape
    tm, tn = _tile(T, 512, 8), _tile(P, 1024)

    def epilogue(vals, ex, outs):
        outs[0][...] = vals[0].astype(BF16)

    return _mm("proj_main", grid=(P // tn, T // tm),
               pairs=[(h, pl.BlockSpec((tm, D), lambda j, i: (i, 0)),
                       w_t, pl.BlockSpec((tn, D), lambda j, i: (j, 0)), NT, 0)],
               out_shape=[jax.ShapeDtypeStruct((T, P), BF16)],
               out_specs=[pl.BlockSpec((tm, tn), lambda j, i: (i, j))],
               acc_shapes=[(tm, tn)], nk=1, kaxis=None, epilogue=epilogue, comm=comm)[0]


def mm_nt_bf16(name, a, w):
    T, K = a.shape
    M = w.shape[0]
    tm, tn = _tile(T, 512, 8), _tile(M, 1024)

    def epilogue(vals, ex, outs):
        outs[0][...] = vals[0].astype(BF16)

    return _mm(name, grid=(T // tm, M // tn),
               pairs=[(a, pl.BlockSpec((tm, K), lambda i, n: (i, 0)),
                       w, pl.BlockSpec((tn, K), lambda i, n: (n, 0)), NT, 0)],
               out_shape=[jax.ShapeDtypeStruct((T, M), BF16)],
               out_specs=[pl.BlockSpec((tm, tn), lambda i, n: (i, n))],
               acc_shapes=[(tm, tn)], nk=1, kaxis=None, epilogue=epilogue)[0]


def proj_dh(dproj, w_t, df, wf_t, comm=None):
    T, P = dproj.shape
    D = w_t.shape[1]
    tm, tn, tk = _tile(T, 512, 8), _tile(D, 2048), _tile(P, 1280)

    def epilogue(vals, ex, outs):
        outs[0][...] = vals[0]

    return _mm("proj_dh", grid=(T // tm, D // tn, P // tk),
               pairs=[(dproj, pl.BlockSpec((tm, tk), lambda i, n, k: (i, k)),
                       w_t, pl.BlockSpec((tk, tn), lambda i, n, k: (k, n)), NN, 0)],
               once_pairs=[(df, pl.BlockSpec((tm, LANES), lambda i, n, k: (i, 0)),
                            wf_t, pl.BlockSpec((LANES, tn), lambda i, n, k: (0, n)), NN, 0)],
               out_shape=[jax.ShapeDtypeStruct((T, D), F32)],
               out_specs=[pl.BlockSpec((tm, tn), lambda i, n, k: (i, n))],
               acc_shapes=[(tm, tn)], nk=P // tk, kaxis=2, epilogue=epilogue, comm=comm)[0]


def proj_dw(dproj, df, h, rows):
    T, P = dproj.shape
    D = h.shape[1]
    tm, tn, tk = _tile(P, 1280), D // 2, _tile(T, 1024, 16)

    def to_bf16(vals, ex, outs):
        outs[0][...] = vals[0].astype(BF16)

    def to_f32(vals, ex, outs):
        outs[0][...] = vals[0]

    main = _mm("proj_dw_main", grid=(P // tm, D // tn, T // tk),
               pairs=[(dproj, pl.BlockSpec((tk, tm), lambda m, n, k: (k, m)),
                       h, pl.BlockSpec((tk, tn), lambda m, n, k: (k, n)), TN, 0)],
               out_shape=[jax.ShapeDtypeStruct((2, rows, tn), BF16)],
               out_specs=[pl.BlockSpec((None, tm, tn), lambda m, n, k: (n, m, 0))],
               acc_shapes=[(tm, tn)], nk=T // tk, kaxis=2, epilogue=to_bf16)[0]
    gate = _mm("proj_dw_f", grid=(1, D // tn, T // tk),
               pairs=[(df, pl.BlockSpec((tk, LANES), lambda m, n, k: (k, 0)),
                       h, pl.BlockSpec((tk, tn), lambda m, n, k: (k, n)), TN, 0)],
               out_shape=[jax.ShapeDtypeStruct((LANES, D), F32)],
               out_specs=[pl.BlockSpec((LANES, tn), lambda m, n, k: (0, n))],
               acc_shapes=[(LANES, tn)], nk=T // tk, kaxis=2, epilogue=to_f32)[0]
    return main, gate


def fgate_fwd(h, wf_t, bias, n_heads):
    T, D = h.shape
    tt = _tile(T, 512, 8)

    def body(h_ref, w_ref, b_ref, f_ref, c_ref, carry):
        i = pl.program_id(0)

        @pl.when(i == 0)
        def _():
            carry[...] = jnp.zeros_like(carry)

        f = lax.dot_general(h_ref[...], w_ref[...], NT, preferred_element_type=F32) + b_ref[...]
        logf = jnp.minimum(f, 0.0) - jnp.log(1.0 + jnp.exp(-jnp.abs(f)))
        tri = (lax.broadcasted_iota(jnp.int32, (tt, tt), 0) >= lax.broadcasted_iota(jnp.int32, (tt, tt), 1))
        cs = jnp.dot(tri.astype(F32), logf, preferred_element_type=F32, precision=lax.Precision.HIGHEST)
        c = cs + carry[...]
        f_ref[...] = f
        c_ref[...] = c
        carry[...] = c[tt - 1:tt, :]

    row = pl.BlockSpec((tt, LANES), lambda i: (i, 0))
    return _pcall(
        body, name="fgate_fwd", grid=(T // tt,),
        in_specs=[pl.BlockSpec((tt, D), lambda i: (i, 0)), pl.BlockSpec((LANES, D), lambda i: (0, 0)),
                  pl.BlockSpec((1, LANES), lambda i: (0, 0))],
        out_specs=[row, row],
        out_shape=[jax.ShapeDtypeStruct((T, LANES), F32)] * 2,
        scratch=[pltpu.VMEM((1, LANES), F32)],
    )(h, wf_t, bias)


def fgate_bwd(dc, f, n_heads):
    T = dc.shape[0]
    tt = _tile(T, 512, 8)
    nt = T // tt

    def body(dc_ref, f_ref, df_ref, db_ref, carry):
        i = pl.program_id(0)

        @pl.when(i == 0)
        def _():
            carry[...] = jnp.zeros_like(carry)

        tri = (lax.broadcasted_iota(jnp.int32, (tt, tt), 1) >= lax.broadcasted_iota(jnp.int32, (tt, tt), 0))
        rs = jnp.dot(tri.astype(F32), dc_ref[...], preferred_element_type=F32,
                     precision=lax.Precision.HIGHEST) + carry[...]
        carry[...] = rs[0:1, :]
        lane = lax.broadcasted_iota(jnp.int32, (tt, LANES), 1)
        df = jnp.where(lane < n_heads, rs * _sigmoid(-f_ref[...]), 0.0)
        df_ref[...] = df.astype(BF16)
        part = jnp.sum(df, axis=0, keepdims=True)

        @pl.when(i == 0)
        def _():
            db_ref[...] = part

        @pl.when(i > 0)
        def _():
            db_ref[...] += part

    rev = pl.BlockSpec((tt, LANES), lambda i: (nt - 1 - i, 0))
    return _pcall(
        body, name="fgate_bwd", grid=(nt,),
        in_specs=[rev, rev],
        out_specs=[rev, pl.BlockSpec((1, LANES), lambda i: (0, 0))],
        out_shape=[jax.ShapeDtypeStruct((T, LANES), BF16), jax.ShapeDtypeStruct((1, LANES), F32)],
        scratch=[pltpu.VMEM((1, LANES), F32)],
    )(dc, f)


SUBLANES = 8
SHIFT_ROWS = HALO - SUBLANES


def _shifted_copies(buf, sh, tt):
    for r in range(1, SUBLANES):
        sh[r - 1, 0:tt + SHIFT_ROWS, :] = buf[pl.ds(r, tt + SHIFT_ROWS), :]


def _tap(buf, sh, offset, tt):
    q, r = divmod(offset, SUBLANES)
    if r == 0:
        return buf[pl.ds(SUBLANES * q, tt), :]
    return sh[r - 1, pl.ds(SUBLANES * q, tt), :]


def conv_fwd(proj, conv_w, conv_b, ln_g, ln_b):
    T = proj.shape[0]
    C = conv_w.shape[1]
    tt = _tile(T, 256, HALO)
    hb = tt // HALO

    def body(a_ref, g_ref, ah_ref, gh_ref, w_ref, cb_ref, lg_ref, lb_ref, ypre_ref, y_ref, ubuf, ush):
        i = pl.program_id(0)
        u = a_ref[...].astype(F32) * _sigmoid(g_ref[...].astype(F32))
        uh = ah_ref[...].astype(F32) * _sigmoid(gh_ref[...].astype(F32))
        ubuf[0:HALO, :] = jnp.where(i == 0, 0.0, uh)
        ubuf[HALO:HALO + tt, :] = u
        _shifted_copies(ubuf, ush, tt)
        acc = jnp.broadcast_to(cb_ref[...], (tt, C))
        for k in range(CONV_K):
            acc = acc + w_ref[k:k + 1, :] * _tap(ubuf, ush, HALO - (CONV_K - 1) + k, tt)
        ypre_ref[...] = acc
        mu = jnp.mean(acc, axis=-1, keepdims=True)
        d = acc - mu
        rstd = lax.rsqrt(jnp.mean(d * d, axis=-1, keepdims=True) + LN_EPS)
        z = d * rstd * lg_ref[...] + lb_ref[...]
        y_ref[...] = (z * _sigmoid(z)).astype(BF16)

    vec = pl.BlockSpec((1, C), lambda i: (0, 0))
    return _pcall(
        body, name="conv_fwd", grid=(T // tt,),
        in_specs=[pl.BlockSpec((tt, C), lambda i: (i, 0)), pl.BlockSpec((tt, C), lambda i: (i, 1)),
                  pl.BlockSpec((HALO, C), lambda i: (jnp.maximum(i * hb - 1, 0), 0)),
                  pl.BlockSpec((HALO, C), lambda i: (jnp.maximum(i * hb - 1, 0), 1)),
                  pl.BlockSpec((HALO, C), lambda i: (0, 0)), vec, vec, vec],
        out_specs=[pl.BlockSpec((tt, C), lambda i: (i, 0))] * 2,
        out_shape=[jax.ShapeDtypeStruct((T, C), F32), jax.ShapeDtypeStruct((T, C), BF16)],
        scratch=[pltpu.VMEM((tt + HALO, C), F32), pltpu.VMEM((SUBLANES - 1, tt + SHIFT_ROWS, C), F32)],
    )(proj, proj, proj, proj, conv_w, conv_b, ln_g, ln_b)


def conv_bwd(proj, ypre, dycat, conv_w, ln_g, ln_b):
    T = proj.shape[0]
    C = conv_w.shape[1]
    tt = _tile(T, 256, HALO)
    hb = tt // HALO
    nt = T // tt
    last_h = T // HALO - 1

    def ln_bwd(ypre_v, dout_v, lg, lb):
        mu = jnp.mean(ypre_v, axis=-1, keepdims=True)
        d = ypre_v - mu
        rstd = lax.rsqrt(jnp.mean(d * d, axis=-1, keepdims=True) + LN_EPS)
        yh = d * rstd
        z = yh * lg + lb
        s = _sigmoid(z)
        dz = dout_v * (s * (1.0 + z * (1.0 - s)))
        dyh = dz * lg
        dy = rstd * (dyh - jnp.mean(dyh, axis=-1, keepdims=True)
                     - yh * jnp.mean(dyh * yh, axis=-1, keepdims=True))
        return dy, dz, yh

    def body(a_ref, g_ref, ah_ref, gh_ref, yp_ref, ypn_ref, do_ref, don_ref, w_ref, lg_ref, lb_ref,
             dag_ref, dw_ref, dcb_ref, dlg_ref, dlb_ref, ubuf, dybuf, ush, dysh):
        i = pl.program_id(0)
        av = a_ref[...].astype(F32)
        sg = _sigmoid(g_ref[...].astype(F32))
        uh = ah_ref[...].astype(F32) * _sigmoid(gh_ref[...].astype(F32))
        ubuf[0:HALO, :] = jnp.where(i == 0, 0.0, uh)
        ubuf[HALO:HALO + tt, :] = av * sg
        lg, lb = lg_ref[...], lb_ref[...]
        dy, dz, yh = ln_bwd(yp_ref[...], do_ref[...].astype(F32), lg, lb)
        dyn, _, _ = ln_bwd(ypn_ref[...], don_ref[...].astype(F32), lg, lb)
        dybuf[0:tt, :] = dy
        dybuf[tt:tt + HALO, :] = jnp.where(i == nt - 1, 0.0, dyn)
        _shifted_copies(ubuf, ush, tt)
        _shifted_copies(dybuf, dysh, tt)

        @pl.when(i == 0)
        def _():
            dw_ref[...] = jnp.zeros_like(dw_ref)
            dcb_ref[...] = jnp.zeros_like(dcb_ref)
            dlg_ref[...] = jnp.zeros_like(dlg_ref)
            dlb_ref[...] = jnp.zeros_like(dlb_ref)

        du = jnp.zeros((tt, C), F32)
        for k in range(CONV_K):
            du = du + w_ref[k:k + 1, :] * _tap(dybuf, dysh, CONV_K - 1 - k, tt)
            dw_ref[k:k + 1, :] += jnp.sum(dy * _tap(ubuf, ush, HALO - (CONV_K - 1) + k, tt), axis=0, keepdims=True)
        dcb_ref[...] += jnp.sum(dy, axis=0, keepdims=True)
        dlg_ref[...] += jnp.sum(dz * yh, axis=0, keepdims=True)
        dlb_ref[...] += jnp.sum(dz, axis=0, keepdims=True)

        dag_ref[:, 0:C] = (du * sg).astype(BF16)
        dag_ref[:, C:2 * C] = (du * av * sg * (1.0 - sg)).astype(BF16)

    vec = pl.BlockSpec((1, C), lambda i: (0, 0))
    prev_h = lambda col: pl.BlockSpec((HALO, C), lambda i: (jnp.maximum(i * hb - 1, 0), col))
    next_h = pl.BlockSpec((HALO, C), lambda i: (jnp.minimum((i + 1) * hb, last_h), 0))
    return _pcall(
        body, name="conv_bwd", grid=(nt,),
        in_specs=[pl.BlockSpec((tt, C), lambda i: (i, 0)), pl.BlockSpec((tt, C), lambda i: (i, 1)),
                  prev_h(0), prev_h(1),
                  pl.BlockSpec((tt, C), lambda i: (i, 0)), next_h,
                  pl.BlockSpec((tt, C), lambda i: (i, 0)), next_h,
                  pl.BlockSpec((HALO, C), lambda i: (0, 0)), vec, vec],
        out_specs=[pl.BlockSpec((tt, 2 * C), lambda i: (i, 0)), pl.BlockSpec((HALO, C), lambda i: (0, 0)),
                   vec, vec, vec],
        out_shape=[jax.ShapeDtypeStruct((T, 2 * C), BF16), jax.ShapeDtypeStruct((HALO, C), F32),
                   jax.ShapeDtypeStruct((1, C), F32), jax.ShapeDtypeStruct((1, C), F32),
                   jax.ShapeDtypeStruct((1, C), F32)],
        scratch=[pltpu.VMEM((tt + HALO, C), F32), pltpu.VMEM((tt + HALO, C), F32),
                 pltpu.VMEM((SUBLANES - 1, tt + SHIFT_ROWS, C), F32),
                 pltpu.VMEM((SUBLANES - 1, tt + SHIFT_ROWS, C), F32)],
    )(proj, proj, proj, proj, ypre, ypre, dycat, dycat, conv_w, ln_g, ln_b)


PAIR = LANES // HEAD_DIM


def _head_masks(rows):
    lane = lax.broadcasted_iota(jnp.int32, (rows, LANES), 1)
    return [jnp.logical_and(lane >= hh * HEAD_DIM, lane < (hh + 1) * HEAD_DIM) for hh in range(PAIR)]


def _causal(tq, tk):
    return lax.broadcasted_iota(jnp.int32, (tq, tk), 0) >= lax.broadcasted_iota(jnp.int32, (tq, tk), 1)


def _lane_column(block, lane_index):
    lane = lax.broadcasted_iota(jnp.int32, block.shape, 1)
    return jnp.sum(jnp.where(lane == lane_index, block, 0.0), axis=-1, keepdims=True)


def attn_fwd(proj, cum, ck4, q_col, comm=None):
    T = proj.shape[0]
    H, nkv, _, tk = ck4.shape
    tq = tk
    hd = H * HEAD_DIM
    qb, kb, vb = q_col // LANES, (q_col + hd) // LANES, (q_col + 2 * hd) // LANES
    scale = 1.0 / math.sqrt(HEAD_DIM)

    def body(q_ref, k_ref, v_ref, cum_ref, ck_ref, o_ref, lse_ref):
        hp = pl.program_id(0)
        i = pl.program_id(1)
        masks = _head_masks(tq)
        q2 = q_ref[...] * scale
        qs = [jnp.where(mk, q2, jnp.zeros_like(q2)) for mk in masks]
        cqs = [_lane_column(cum_ref[...], PAIR * hp + hh) for hh in range(PAIR)]

        def step(j, carry, diagonal, blocks=1):
            scores, values = [], []
            for b in range(blocks):
                off = pl.multiple_of((j + b) * tk, tk)
                kj = k_ref[pl.ds(off, tk), :]
                values.append(v_ref[pl.ds(off, tk), :])
                per_head = []
                for hh in range(PAIR):
                    s = lax.dot_general(qs[hh], kj, NT, preferred_element_type=F32)
                    s = s + cqs[hh] - ck_ref[hh, j + b]
                    if diagonal:
                        s = jnp.where(_causal(tq, tk), s, NEG_INF)
                    per_head.append(s)
                scores.append(per_head)
            out = []
            for hh in range(PAIR):
                m, l, acc = carry[hh]
                m_new = m
                for b in range(blocks):
                    m_new = jnp.maximum(m_new, jnp.max(scores[b][hh], axis=-1, keepdims=True))
                alpha = jnp.exp(m - m_new)
                l, acc = alpha * l, alpha * acc
                for b in range(blocks):
                    p = jnp.exp(scores[b][hh] - m_new)
                    l = l + jnp.sum(p, axis=-1, keepdims=True)
                    acc = acc + jnp.dot(p.astype(BF16), values[b], preferred_element_type=F32)
                out.append((m_new, l, acc))
            return tuple(out)

        init = tuple((jnp.full((tq, 1), -jnp.inf, F32), jnp.zeros((tq, 1), F32), jnp.zeros((tq, LANES), F32))
                     for _ in range(PAIR))
        twos = i // 2
        carry = lax.fori_loop(0, twos, lambda t, c: step(2 * t, c, False, blocks=2), init)
        carry = lax.fori_loop(2 * twos, i, functools.partial(step, diagonal=False), carry)
        carry = step(i, carry, True)
        o = carry[PAIR - 1][2] / carry[PAIR - 1][1]
        for hh in range(PAIR - 1):
            o = jnp.where(masks[hh], carry[hh][2] / carry[hh][1], o)
        o_ref[...] = o
        lse = jnp.broadcast_to(carry[PAIR - 1][0] + jnp.log(carry[PAIR - 1][1]), (tq, LANES))
        for hh in range(PAIR - 1):
            lse = jnp.where(masks[hh], carry[hh][0] + jnp.log(carry[hh][1]), lse)
        lse_ref[...] = lse

    return _pcall(
        body, name="attn_fwd", grid=(H // PAIR, T // tq),
        in_specs=[pl.BlockSpec((tq, LANES), lambda hp, i: (i, qb + hp)),
                  pl.BlockSpec((T, LANES), lambda hp, i: (0, kb + hp)),
                  pl.BlockSpec((T, LANES), lambda hp, i: (0, vb + hp)),
                  pl.BlockSpec((tq, LANES), lambda hp, i: (i, 0)),
                  pl.BlockSpec((PAIR, nkv, 1, tk), lambda hp, i: (hp, 0, 0, 0))],
        out_specs=[pl.BlockSpec((tq, LANES), lambda hp, i: (i, hp)),
                   pl.BlockSpec((None, tq, LANES), lambda hp, i: (hp, i, 0))],
        out_shape=[jax.ShapeDtypeStruct((T, hd), F32), jax.ShapeDtypeStruct((H // PAIR, T, LANES), F32)],
        comm=comm,
    )(proj, proj, proj, cum, ck4)


def attn_bwd(proj, o, dycat, lse, cum, ck4, q_col, do_col, comm=None):
    T = proj.shape[0]
    H, nkv, _, tk = ck4.shape
    tq = tk
    nq = T // tq
    hd = H * HEAD_DIM
    qb, kb, vb = q_col // LANES, (q_col + hd) // LANES, (q_col + 2 * hd) // LANES
    dob = do_col // LANES
    scale = 1.0 / math.sqrt(HEAD_DIM)

    def body(q_ref, k_ref, v_ref, o_ref, do_ref, lse_ref, cum_ref, ck_ref,
             dq_ref, dk_ref, dv_ref, dcq_ref, dck_ref, dk_acc, dv_acc):
        hp = pl.program_id(0)
        j = pl.program_id(1)

        def block(i, diagonal):
            masks = _head_masks(tq)
            rows = pl.ds(pl.multiple_of(i * tq, tq), tq)
            q2, k2, v2, do2 = q_ref[rows, :] * scale, k_ref[...], v_ref[...], do_ref[rows, :]
            zero = jnp.zeros_like(q2)
            prod = do2.astype(F32) * o_ref[rows, :]
            cum_q, lse_q = cum_ref[rows, :], lse_ref[rows, :]
            dq_part = dk_part = dv_part = None
            dcq_part = jnp.zeros((tq, LANES), F32)
            lane = lax.broadcasted_iota(jnp.int32, (tq, LANES), 1)
            for hh in range(PAIR):
                qh = jnp.where(masks[hh], q2, zero)
                kh = jnp.where(masks[hh], k2, zero)
                doh = jnp.where(masks[hh], do2, zero)
                delta = jnp.sum(jnp.where(masks[hh], prod, 0.0), axis=-1, keepdims=True)
                s = lax.dot_general(qh, k2, NT, preferred_element_type=F32)
                s = s + _lane_column(cum_q, PAIR * hp + hh) - ck_ref[hh]
                if diagonal:
                    s = jnp.where(_causal(tq, tk), s, NEG_INF)
                p = jnp.exp(s - _lane_column(lse_q, hh * HEAD_DIM))
                dp = lax.dot_general(doh, v2, NT, preferred_element_type=F32)
                ds = p * (dp - delta)
                dsb = ds.astype(BF16)
                dv_h = lax.dot_general(p.astype(BF16), doh, TN, preferred_element_type=F32)
                dk_h = lax.dot_general(dsb, qh, TN, preferred_element_type=F32)
                dq_h = jnp.dot(dsb, kh, preferred_element_type=F32)
                dq_part = dq_h if dq_part is None else dq_part + dq_h
                dk_part = dk_h if dk_part is None else dk_part + dk_h
                dv_part = dv_h if dv_part is None else dv_part + dv_h
                dck_h = -jnp.sum(ds, axis=0, keepdims=True)
                dcq_part = jnp.where(lane == PAIR * hp + hh, jnp.sum(ds, axis=-1, keepdims=True), dcq_part)
                if diagonal:
                    dck_ref[hh] = dck_h
                else:
                    dck_ref[hh] += dck_h
            dq_part = dq_part * scale

            @pl.when(j == 0)
            def _():
                dq_ref[rows, :] = dq_part

            @pl.when(j > 0)
            def _():
                dq_ref[rows, :] += dq_part

            @pl.when(jnp.logical_and(hp == 0, j == 0))
            def _():
                dcq_ref[rows, :] = dcq_part

            @pl.when(jnp.logical_or(hp > 0, j > 0))
            def _():
                dcq_ref[rows, :] += dcq_part

            if diagonal:
                dk_acc[...] = dk_part
                dv_acc[...] = dv_part
            else:
                dk_acc[...] += dk_part
                dv_acc[...] += dv_part

        block(j, True)

        def later(i, carry):
            block(i, False)
            return carry

        lax.fori_loop(j + 1, nq, later, 0)
        dk_ref[...] = dk_acc[...].astype(BF16)
        dv_ref[...] = dv_acc[...].astype(BF16)

    at_q = lambda col: pl.BlockSpec((T, LANES), lambda hp, j: (0, col + hp))
    at_k = lambda col: pl.BlockSpec((tk, LANES), lambda hp, j: (j, col + hp))
    lse_spec = pl.BlockSpec((None, T, LANES), lambda hp, j: (hp, 0, 0))
    cum_spec = pl.BlockSpec((T, LANES), lambda hp, j: (0, 0))
    ck_spec = pl.BlockSpec((PAIR, None, 1, tk), lambda hp, j: (hp, j, 0, 0))
    return _pcall(
        body, name="attn_bwd", grid=(H // PAIR, nkv),
        in_specs=[at_q(qb), at_k(kb), at_k(vb), at_q(0), at_q(dob), lse_spec, cum_spec, ck_spec],
        out_specs=[pl.BlockSpec((T, LANES), lambda hp, j: (0, hp)), at_k(0), at_k(0),
                   pl.BlockSpec((T, LANES), lambda hp, j: (0, 0)), ck_spec],
        out_shape=[jax.ShapeDtypeStruct((T, hd), F32), jax.ShapeDtypeStruct((T, hd), BF16),
                   jax.ShapeDtypeStruct((T, hd), BF16),
                   jax.ShapeDtypeStruct((T, LANES), F32), jax.ShapeDtypeStruct((H, nkv, 1, tk), F32)],
        scratch=[pltpu.VMEM((tk, LANES), F32), pltpu.VMEM((tk, LANES), F32)],
        comm=comm,
    )(proj, proj, proj, o, dycat, lse, cum, ck4)


ELEMENTWISE_BLOCK_BYTES = 2 * 1024 * 1024
BF16_ROWS = 16


def cast_bf16(arrays, comm=None):
    def slab(a, steps):
        R, C = a.shape
        if R % (steps * BF16_ROWS) == 0:
            return pl.BlockSpec((R // steps, C), lambda i: (i, 0))
        if C % (steps * LANES) == 0:
            return pl.BlockSpec((R, C // steps), lambda i: (0, i))
        return None

    steps = 8 if all(slab(a, 8) is not None for a in arrays) else 4
    specs = [slab(a, steps) for a in arrays]
    n = len(arrays)

    def body(*refs):
        for src, dst in zip(refs[:n], refs[n:]):
            dst[...] = src[...].astype(BF16)

    return _pcall(body, name="cast_bf16", grid=(steps,), in_specs=specs, out_specs=specs,
                  out_shape=[jax.ShapeDtypeStruct(a.shape, BF16) for a in arrays], comm=comm)(*arrays)


def _ew_tiles(rows, cols, bytes_per_element):
    target = max(8, ELEMENTWISE_BLOCK_BYTES // max(1, cols * bytes_per_element))
    if rows <= target:
        return rows, cols
    t = (target // 16) * 16
    while t >= 16:
        if rows % t == 0:
            return t, cols
        t -= 16
    tc = _tile(cols, max(LANES, (ELEMENTWISE_BLOCK_BYTES // (rows * bytes_per_element)) // LANES * LANES))
    return rows, tc


def sum_chips(recv):
    nc, R, C = recv.shape
    tr, tc = _ew_tiles(R, C, 4)

    def body(r_ref, o_ref):
        acc = r_ref[0].astype(F32)
        for j in range(1, nc):
            acc = acc + r_ref[j].astype(F32)
        o_ref[...] = acc

    return _pcall(
        body, name="sum_chips", grid=(R // tr, C // tc),
        in_specs=[pl.BlockSpec((nc, tr, tc), lambda i, j: (0, i, j))],
        out_specs=[pl.BlockSpec((tr, tc), lambda i, j: (i, j))],
        out_shape=[jax.ShapeDtypeStruct((R, C), F32)],
    )(recv)[0]


def add_sibling_half(g, recv):
    nc, R, C = g.shape
    hr = R // 2
    tr, tc = _ew_tiles(hr // 2, C, 4 * nc)
    per_quarter = (hr // 2) // tr

    def body(g_ref, r_ref, o_ref):
        c = lax.axis_index("c")
        for j in range(nc):
            o_ref[j] = (g_ref[j, c].astype(F32) + r_ref[j].astype(F32)).astype(BF16)

    return _pcall(
        body, name="add_sibling_half", grid=(hr // tr, C // tc),
        in_specs=[pl.BlockSpec((nc, 2, tr, tc), lambda i, j: (0, 0, i, j)),
                  pl.BlockSpec((nc, tr, tc), lambda i, j: (0, i, j))],
        out_specs=[pl.BlockSpec((None, nc, tr, tc), lambda i, j: (i // per_quarter, 0, i % per_quarter, j))],
        out_shape=[jax.ShapeDtypeStruct((2, nc, hr // 2, C), BF16)],
    )(g.reshape(nc, 2, hr, C), recv)[0]


def adamw(w, m, v, g_parts, comm=None, halves=False):
    R, C = w.shape
    tr, tc = _ew_tiles(R // 2 if halves else R, C, 4)
    n_g = len(g_parts)
    n_half = (R // 2) // tr
    c1 = 1.0 - ADAM_B1
    c2 = 1.0 - ADAM_B2
    bc1 = 1.0 - ADAM_B1 ** ADAM_STEP
    bc2 = 1.0 - ADAM_B2 ** ADAM_STEP

    def body(*refs):
        w_ref, m_ref, v_ref = refs[:3]
        g_refs = refs[3:3 + n_g]
        g_out, d_out, m_out, v_out = refs[3 + n_g:]
        if halves:
            mine = (pl.program_id(0) >= n_half) == (lax.axis_index("c") == 1)
            g = jnp.where(mine, g_refs[0][...], g_refs[1][...])
        else:
            g = g_refs[0][...]
            for r in g_refs[1:]:
                g = g + r[...]
        m_new = ADAM_B1 * m_ref[...] + c1 * g
        v_new = ADAM_B2 * v_ref[...] + c2 * (g * g)
        m_hat = m_new / bc1
        v_hat = v_new / bc2
        g_out[...] = g
        d_out[...] = -ADAM_LR * (m_hat / (jnp.sqrt(v_hat) + ADAM_EPS) + ADAM_WD * w_ref[...])
        m_out[...] = m_new
        v_out[...] = v_new

    spec = pl.BlockSpec((tr, tc), lambda i, j: (i, j))
    g_spec = pl.BlockSpec((tr, tc), lambda i, j: (i % n_half, j)) if halves else spec
    return _pcall(
        body, name="adamw", grid=(R // tr, C // tc),
        in_specs=[spec] * 3 + [g_spec] * n_g, out_specs=[spec] * 4,
        out_shape=[jax.ShapeDtypeStruct((R, C), F32)] * 4, comm=comm,
    )(w, m, v, *g_parts)


def _chip_coords():
    x, y, c = lax.axis_index("x"), lax.axis_index("y"), lax.axis_index("c")
    others = [(1 - x, y), (x, 1 - y), (1 - x, 1 - y)]
    return x, y, c, others


def _remote(src, dst, send_sem, recv_sem, device):
    return pltpu.make_async_remote_copy(src_ref=src, dst_ref=dst, send_sem=send_sem, recv_sem=recv_sem,
                                        device_id=device, device_id_type=MESH)


def gather_comm(shards):
    n = len(shards)
    SLOTS = 7

    def makers(ins, outs, sems):
        send_sems, recv_sems, local_sems = sems
        x, y, c, _ = _chip_coords()
        me, xn, yn, dg = 2 * x + y, 2 * (1 - x) + y, 2 * x + (1 - y), 2 * (1 - x) + (1 - y)
        to_x, to_y, sibling = (1 - x, y, c), (x, 1 - y, c), (x, y, 1 - c)

        def part(ref, a, half, quarter=None, chip=None):
            rows, cols = ins[a].shape[0], ins[a].shape[1]
            lead = () if chip is None else (chip,)
            along_rows = rows % (4 * BF16_ROWS) == 0 or (ins[a].dtype == F32 and rows % (4 * SUBLANES) == 0)
            size = (rows if along_rows else cols) // 2
            start = half * size
            if quarter is not None:
                size = size // 2
                start = start + quarter * size
            if along_rows:
                return ref.at[(*lead, pl.ds(start, size))]
            return ref.at[(*lead, slice(None), pl.ds(start, size))]

        def copy(a, k, src, dst, device):
            return _remote(src, dst, send_sems.at[SLOTS * a + k], recv_sems.at[SLOTS * a + k], device)

        def local(a):
            return pltpu.make_async_copy(ins[a], outs[a].at[me], local_sems.at[a])

        def first_leg(a):
            mine = part(outs[a], a, c, chip=me)
            return [copy(a, 0, part(ins[a], a, c), mine, to_x), copy(a, 1, part(ins[a], a, c), mine, to_y)]

        def arrived(a, k):
            region = {0: part(outs[a], a, c, chip=xn), 1: part(outs[a], a, c, chip=yn),
                      2: part(outs[a], a, c, 0, chip=dg), 3: part(outs[a], a, c, 1, chip=dg),
                      4: part(outs[a], a, 1 - c, chip=xn), 5: part(outs[a], a, 1 - c, chip=yn),
                      6: part(outs[a], a, 1 - c, chip=dg)}[k]
            return copy(a, k, region, region, sibling if k >= 4 else (to_x if k in (0, 3) else to_y))

        def relays(a):
            qx, qy = part(outs[a], a, c, 0, chip=xn), part(outs[a], a, c, 1, chip=yn)
            return [copy(a, 2, qx, qx, to_y), copy(a, 3, qy, qy, to_x)]

        def handover(a, k):
            region = part(outs[a], a, c, chip={4: xn, 5: yn, 6: dg}[k])
            return copy(a, k, region, region, sibling)

        return local, first_leg, arrived, relays, handover

    def start(ins, outs, sems):
        local, first_leg, _, _, _ = makers(ins, outs, sems)
        for a in range(n):
            for cp in first_leg(a):
                cp.start()
        for a in range(n):
            local(a).start()

    def relay(ins, outs, sems):
        _, _, arrived, relays, handover = makers(ins, outs, sems)
        for a in range(n):
            to_y_nbr, to_x_nbr = relays(a)
            arrived(a, 0).wait_recv()
            to_y_nbr.start()
            handover(a, 4).start()
            arrived(a, 1).wait_recv()
            to_x_nbr.start()
            handover(a, 5).start()

    def finish(ins, outs, sems):
        local, first_leg, arrived, relays, handover = makers(ins, outs, sems)
        for a in range(n):
            arrived(a, 2).wait_recv()
            arrived(a, 3).wait_recv()
            handover(a, 6).start()
        for a in range(n):
            for k in (4, 5, 6):
                arrived(a, k).wait_recv()
        for a in range(n):
            for cp in first_leg(a) + relays(a) + [handover(a, k) for k in (4, 5, 6)]:
                cp.wait_send()
            local(a).wait()

    return Comm(shards, [jax.ShapeDtypeStruct((N_CHIP,) + s.shape, s.dtype) for s in shards],
                [pltpu.SemaphoreType.DMA((SLOTS * n,)), pltpu.SemaphoreType.DMA((SLOTS * n,)),
                 pltpu.SemaphoreType.DMA((n,))], start, finish, relay)


def scatter_comm(grads):
    n = len(grads)
    pieces = [(a, jj) for a in range(n) for jj in range(3)]

    def makers(ins, outs, sems):
        send_sems, recv_sems, local_sems = sems
        x, y, c, others = _chip_coords()
        me = 2 * x + y

        def local(a):
            return pltpu.make_async_copy(ins[a].at[me], outs[a].at[me], local_sems.at[a])

        def ici(a, jj):
            ox, oy = others[jj]
            return _remote(ins[a].at[2 * ox + oy], outs[a].at[me], send_sems.at[3 * a + jj],
                           recv_sems.at[3 * a + jj], (ox, oy, c))

        def landed(a, jj):
            ox, oy = others[jj]
            slot = outs[a].at[2 * ox + oy]
            return _remote(slot, slot, send_sems.at[3 * a + jj], recv_sems.at[3 * a + jj], (ox, oy, c))

        return local, ici, landed

    def start(ins, outs, sems):
        local, ici, _ = makers(ins, outs, sems)
        for a in range(n):
            for jj in (2, 0, 1):
                ici(a, jj).start()
        for a in range(n):
            local(a).start()

    def finish(ins, outs, sems):
        local, ici, landed = makers(ins, outs, sems)
        for a, jj in pieces:
            landed(a, jj).wait_recv()
        for a, jj in pieces:
            ici(a, jj).wait_send()
        for a in range(n):
            local(a).wait()

    return Comm(grads, [jax.ShapeDtypeStruct(g.shape, g.dtype) for g in grads],
                [pltpu.SemaphoreType.DMA((3 * n,)), pltpu.SemaphoreType.DMA((3 * n,)),
                 pltpu.SemaphoreType.DMA((n,))], start, finish)


def halfswap_comm(grads):
    n = len(grads)

    def copies(ins, outs, sems):
        send_sems, recv_sems = sems
        x, y, c, _ = _chip_coords()
        out = []
        for a in range(n):
            hr = ins[a].shape[1] // 2
            out.append(_remote(ins[a].at[:, pl.ds((1 - c) * hr, hr)], outs[a], send_sems.at[a], recv_sems.at[a],
                               (x, y, 1 - c)))
        return out

    def start(ins, outs, sems):
        for cp in copies(ins, outs, sems):
            cp.start()

    def finish(ins, outs, sems):
        for cp in copies(ins, outs, sems):
            cp.wait()

    return Comm(grads, [jax.ShapeDtypeStruct((g.shape[0], g.shape[1] // 2, g.shape[2]), g.dtype) for g in grads],
                [pltpu.SemaphoreType.DMA((n,)), pltpu.SemaphoreType.DMA((n,))], start, finish)


def join_comms(first, second):
    ni, no, ns = len(first.operands), len(first.out_shape), len(first.sems)

    def start(ins, outs, sems):
        first.start(ins[:ni], outs[:no], sems[:ns])
        second.start(ins[ni:], outs[no:], sems[ns:])

    def finish(ins, outs, sems):
        first.finish(ins[:ni], outs[:no], sems[:ns])
        second.finish(ins[ni:], outs[no:], sems[ns:])

    def relay(ins, outs, sems):
        if first.relay is not None:
            first.relay(ins[:ni], outs[:no], sems[:ns])
        if second.relay is not None:
            second.relay(ins[ni:], outs[no:], sems[ns:])

    return Comm(first.operands + second.operands, first.out_shape + second.out_shape, first.sems + second.sems,
                start, finish, relay if (first.relay or second.relay) else None)


def swap_comm(parts):
    n = len(parts)

    def copies(ins, outs, sems):
        send_sems, recv_sems = sems
        x, y, c, _ = _chip_coords()
        return [_remote(ins[a], outs[a], send_sems.at[a], recv_sems.at[a], (x, y, 1 - c)) for a in range(n)]

    def start(ins, outs, sems):
        for cp in copies(ins, outs, sems):
            cp.start()

    def finish(ins, outs, sems):
        for cp in copies(ins, outs, sems):
            cp.wait()

    return Comm(parts, [jax.ShapeDtypeStruct(p.shape, p.dtype) for p in parts],
                [pltpu.SemaphoreType.DMA((n,)), pltpu.SemaphoreType.DMA((n,))], start, finish)


def allreduce_small(v):
    R = v.shape[0]

    def body(v_ref, sum_ref, all_ref, send_sems, recv_sems):
        x, y, c = lax.axis_index("x"), lax.axis_index("y"), lax.axis_index("c")
        me = 4 * x + 2 * y + c
        all_ref[me] = v_ref[...]
        copies = []
        for k in range(1, N_DEV):
            px = 1 - x if k & 4 else x
            py = 1 - y if k & 2 else y
            pc = 1 - c if k & 1 else c
            cp = pltpu.make_async_remote_copy(
                src_ref=v_ref, dst_ref=all_ref.at[me], send_sem=send_sems.at[k - 1], recv_sem=recv_sems.at[k - 1],
                device_id=(px, py, pc), device_id_type=MESH)
            cp.start()
            copies.append((cp, 4 * px + 2 * py + pc))
        for k, (cp, peer) in enumerate(copies):
            pltpu.make_async_remote_copy(
                src_ref=v_ref, dst_ref=all_ref.at[peer], send_sem=send_sems.at[k], recv_sem=recv_sems.at[k],
                device_id=(x, y, c), device_id_type=MESH).wait_recv()
        for cp, _ in copies:
            cp.wait_send()
        acc = all_ref[0]
        for d in range(1, N_DEV):
            acc = acc + all_ref[d]
        sum_ref[...] = acc

    vm = pl.BlockSpec(memory_space=pltpu.VMEM)
    return pl.pallas_call(
        body, name="allreduce_small",
        in_specs=[vm], out_specs=[vm, vm],
        out_shape=[jax.ShapeDtypeStruct((R, LANES), F32), jax.ShapeDtypeStruct((N_DEV, R, LANES), F32)],
        scratch_shapes=[pltpu.SemaphoreType.DMA((N_DEV - 1,)), pltpu.SemaphoreType.DMA((N_DEV - 1,))],
    )(v)[0]


SMALL_NAMES = ("ffn1_norm", "mix_norm", "ffn2_norm", "final_norm", "conv_b", "conv_ln_g", "conv_ln_b")


def _pack_small(vecs, bias, conv_w_rows, loss_tile):
    rows = [vecs[n].reshape(-1, LANES) for n in SMALL_NAMES]
    rows.append(bias.reshape(1, LANES))
    rows.append(conv_w_rows.reshape(-1, LANES))
    rows.append(loss_tile[0:1, :])
    packed = jnp.concatenate(rows, axis=0)
    pad = (-packed.shape[0]) % 8
    return jnp.pad(packed, ((0, pad), (0, 0)))


def _unpack_small(packed, sizes, n_conv_rows):
    out, r = {}, 0
    for n in SMALL_NAMES:
        k = sizes[n] // LANES
        out[n] = packed[r:r + k].reshape(-1)
        r += k
    out["fgate_bias"] = packed[r]
    r += 1
    out["conv_w"] = packed[r:r + n_conv_rows]
    r += n_conv_rows
    out["loss"] = packed[r, 0]
    return out


def kernel(x, ffn1_norm, ffn1_w_gate, ffn1_w_up, ffn1_w_down, mix_norm, w_in, fgate_bias, conv_w, conv_b, conv_ln_g, conv_ln_b, w_out, ffn2_norm, ffn2_w_gate, ffn2_w_up, ffn2_w_down, final_norm, loss_target, m_ffn1_norm, m_ffn1_w_gate, m_ffn1_w_up, m_ffn1_w_down, m_mix_norm, m_w_in, m_fgate_bias, m_conv_w, m_conv_b, m_conv_ln_g, m_conv_ln_b, m_w_out, m_ffn2_norm, m_ffn2_w_gate, m_ffn2_w_up, m_ffn2_w_down, m_final_norm, v_ffn1_norm, v_ffn1_w_gate, v_ffn1_w_up, v_ffn1_w_down, v_mix_norm, v_w_in, v_fgate_bias, v_conv_w, v_conv_b, v_conv_ln_g, v_conv_ln_b, v_w_out, v_ffn2_norm, v_ffn2_w_gate, v_ffn2_w_up, v_ffn2_w_down, v_final_norm):
    w = dict(ffn1_norm=ffn1_norm, ffn1_w_gate=ffn1_w_gate, ffn1_w_up=ffn1_w_up, ffn1_w_down=ffn1_w_down,
             mix_norm=mix_norm, w_in=w_in, fgate_bias=fgate_bias, conv_w=conv_w, conv_b=conv_b,
             conv_ln_g=conv_ln_g, conv_ln_b=conv_ln_b, w_out=w_out, ffn2_norm=ffn2_norm,
             ffn2_w_gate=ffn2_w_gate, ffn2_w_up=ffn2_w_up, ffn2_w_down=ffn2_w_down, final_norm=final_norm)
    m = dict(ffn1_norm=m_ffn1_norm, ffn1_w_gate=m_ffn1_w_gate, ffn1_w_up=m_ffn1_w_up, ffn1_w_down=m_ffn1_w_down,
             mix_norm=m_mix_norm, w_in=m_w_in, fgate_bias=m_fgate_bias, conv_w=m_conv_w, conv_b=m_conv_b,
             conv_ln_g=m_conv_ln_g, conv_ln_b=m_conv_ln_b, w_out=m_w_out, ffn2_norm=m_ffn2_norm,
             ffn2_w_gate=m_ffn2_w_gate, ffn2_w_up=m_ffn2_w_up, ffn2_w_down=m_ffn2_w_down, final_norm=m_final_norm)
    v = dict(ffn1_norm=v_ffn1_norm, ffn1_w_gate=v_ffn1_w_gate, ffn1_w_up=v_ffn1_w_up, ffn1_w_down=v_ffn1_w_down,
             mix_norm=v_mix_norm, w_in=v_w_in, fgate_bias=v_fgate_bias, conv_w=v_conv_w, conv_b=v_conv_b,
             conv_ln_g=v_conv_ln_g, conv_ln_b=v_conv_ln_b, w_out=v_w_out, ffn2_norm=v_ffn2_norm,
             ffn2_w_gate=v_ffn2_w_gate, ffn2_w_up=v_ffn2_w_up, ffn2_w_down=v_ffn2_w_down, final_norm=v_final_norm)
    names = list(w.keys())
    big = ("ffn1_w_gate", "ffn1_w_up", "ffn1_w_down", "w_in", "w_out", "ffn2_w_gate", "ffn2_w_up", "ffn2_w_down")

    T, D = x.shape[1], x.shape[2]
    C = conv_b.shape[0]
    H = fgate_bias.shape[0]
    cs = conv_w.shape[1]
    in_cols = N_CHIP * w_in.shape[1]
    p_main = in_cols - H

    x0, tgt = x[0], loss_target[0]
    tk = _tile(T, 512, 128)
    nkv = T // tk
    row = lambda a: a.reshape(1, -1)
    grad, delta, new_m, new_v = {}, {}, {}, {}

    def update(n, parts, comm=None, halves=False):
        args = (w[n], m[n], v[n])
        if n == "w_in":
            outs = [t.T for t in adamw(*[a.T for a in args], parts, comm=comm)]
        else:
            outs = adamw(*args, parts, comm=comm, halves=halves)
        grad[n], delta[n], new_m[n], new_v[n] = outs

    rest = [n for n in big if n != "ffn1_w_gate"]
    g0 = gather_comm([w["ffn1_w_gate"].astype(BF16), jnp.pad(conv_w, ((0, HALO - CONV_K), (0, 0)))])
    wb = dict(zip(rest, cast_bf16([w[n].T if n == "w_in" else w[n] for n in rest], comm=g0)))
    wg1, conv_w4 = g0.results
    conv_w_full = conv_w4.transpose(1, 0, 2).reshape(HALO, C)
    h1, r1 = rms_fwd(x0, row(ffn1_norm))
    g1a = gather_comm([wb["ffn1_w_up"]])
    a1 = ffn_gate(h1, wg1, comm=g1a)
    wu1 = g1a.results[0]
    g1b = gather_comm([wb["ffn1_w_down"]])
    b1, mid1 = ffn_upmul(h1, wu1, a1, comm=g1b)
    wd1 = g1b.results[0]
    g2 = gather_comm([wb["w_in"]])
    x1 = mm_residual("ffn_down_g", mid1, wd1, x0, 0.5, comm=g2)[0]
    w_t = g2.results[0].reshape(in_cols, D)

    wf_t = jnp.pad(w_t[p_main:], ((0, LANES - H), (0, 0)))
    bias_pad = jnp.pad(row(fgate_bias), ((0, 0), (0, LANES - H)))
    h2, r2 = rms_fwd(x1, row(mix_norm))
    g_out = gather_comm([wb["w_out"]])
    proj = proj_main(h2, w_t, p_main, comm=g_out)
    w_out3 = g_out.results[0]
    f, cum = fgate_fwd(h2, wf_t, bias_pad, H)
    ypre, yconv = conv_fwd(proj, conv_w_full, row(conv_b), row(conv_ln_g), row(conv_ln_b))
    ck4 = cum[:, :H].T.reshape(H, nkv, 1, tk)
    g3 = gather_comm([wb["ffn2_w_gate"], wb["ffn2_w_up"]])
    o, lse = attn_fwd(proj, cum, ck4, 2 * C, comm=g3)
    wg2, wu2 = g3.results
    ycat = jnp.concatenate([yconv, o.astype(BF16)], axis=1)
    x2 = mm_residual("out_proj", ycat, w_out3.reshape(2, -1, D), x1, 1.0)[0]

    h3, r3 = rms_fwd(x2, row(ffn2_norm))
    g4 = gather_comm([wb["ffn2_w_down"]])
    a2, b2, mid2 = ffn_up(h3, wg2, wu2, comm=g4)
    wd2 = g4.results[0]
    x3 = mm_residual("ffn_down", mid2, wd2, x2, 0.5)[0]
    dx3, dx3b, loss_tile, d_final = final_loss(x3, tgt, row(final_norm))

    da2, db2 = ffn_bwd_mid(dx3b, wd2, a2, b2)
    dwd2 = dw_rowshard("ffn_dwd", mid2, dx3b, N_CHIP)[0]
    dwg2, dwu2 = dw_colshard("ffn_dwgu", h3, [da2, db2], N_CHIP)
    s1 = scatter_comm([dwd2])
    dh3 = ffn_dh(da2, db2, wg2, wu2, comm=s1)
    dx2, dx2b, d_ffn2_norm = rms_bwd(dh3, x2, r3, row(ffn2_norm), dx3, 1.0)

    dycat = mm_nt_bf16("out_proj_dy", dx2b, w_out3.reshape(-1, D))
    dw_out3 = dw_rowshard("out_proj_dw", ycat, dx2b, N_CHIP)[0]
    s3 = scatter_comm([dwg2, dwu2, dw_out3])
    dq, dk, dv, dcq, dck4 = attn_bwd(proj, o, dycat, lse, cum, ck4, 2 * C, C, comm=s3)
    dc = dcq + jnp.pad(dck4.reshape(H, T).T, ((0, 0), (0, LANES - H)))
    df, d_bias = fgate_bwd(dc, f, H)
    dag, d_conv_w, d_conv_b, d_ln_g, d_ln_b = conv_bwd(proj, ypre, dycat, conv_w_full, row(conv_ln_g),
                                                       row(conv_ln_b))
    dproj = jnp.concatenate([dag, dq.astype(BF16), dk, dv], axis=1)
    early = ("ffn2_w_down", "ffn2_w_gate", "ffn2_w_up", "w_out")
    early_sums = [sum_chips(r) for r in (s1.results[0], s3.results[0], s3.results[1], s3.results[2])]
    sw1 = swap_comm(early_sums)
    dh2 = proj_dh(dproj, w_t, df, wf_t, comm=sw1)
    dw_t, dwf_t = proj_dw(dproj, df, h2, in_cols)
    gate_rows = dwf_t[:H].astype(BF16).reshape(H, 2, D // 2).transpose(1, 0, 2)
    dw_t = lax.dynamic_update_slice(dw_t, gate_rows, (0, p_main, 0))
    dw_in_halves = [dw_t[half].reshape(N_CHIP, in_cols // N_CHIP, D // 2) for half in range(2)]
    dx1, dx1b, d_mix_norm = rms_bwd(dh2, x1, r2, row(mix_norm), dx2, 0.5)

    s4a = scatter_comm([dw_in_halves[0]])
    da1, db1 = ffn_bwd_mid(dx1b, wd1, a1, b1, comm=s4a)
    s4b = scatter_comm([dw_in_halves[1]])
    dwd1 = dw_rowshard("ffn_dwd_s", mid1, dx1b, N_CHIP, comm=s4b)[0]
    s5 = scatter_comm([dwd1])
    dwg1, dwu1 = dw_colshard("ffn_dwgu_s", h1, [da1, db1], N_CHIP, comm=s5)
    sum_in = jnp.concatenate([sum_chips(s4a.results[0]), sum_chips(s4b.results[0])], axis=1)
    mid_sums = [sum_in, sum_chips(s5.results[0])]
    s6 = join_comms(join_comms(scatter_comm([dwg1]), halfswap_comm([dwu1])), swap_comm(mid_sums))
    dh1 = ffn_dh(da1, db1, wg1, wu1, comm=s6)
    recv_g1, sibling_u1, their_in, their_d1 = s6.results
    quarters = add_sibling_half(dwu1, sibling_u1)
    sum_g1 = sum_chips(recv_g1)
    s7 = [join_comms(scatter_comm([quarters[0]]), swap_comm([sum_g1])), scatter_comm([quarters[1]])]
    grad_x, _, d_ffn1_norm = rms_bwd(dh1, x0, r1, row(ffn1_norm), dx1, 1.0, comm=s7[0])
    their_g1 = s7[0].results[1]

    for i, (n, mine, other) in enumerate(zip(early, early_sums, sw1.results)):
        update(n, [mine, other], comm=s7[1] if i == 0 else None)
    update("w_in", [mid_sums[0], their_in])
    update("ffn1_w_down", [mid_sums[1], their_d1])
    half_u1 = jnp.concatenate([sum_chips(s7[0].results[0]), sum_chips(s7[1].results[0])], axis=0)
    their_u1 = _run_comm("swap_last", swap_comm([half_u1]))[0]
    update("ffn1_w_gate", [sum_g1, their_g1])
    update("ffn1_w_up", [half_u1, their_u1], halves=True)

    gl = dict(ffn1_norm=d_ffn1_norm, mix_norm=d_mix_norm, ffn2_norm=d_ffn2_norm, final_norm=d_final,
              conv_b=d_conv_b, conv_ln_g=d_ln_g, conv_ln_b=d_ln_b)
    small_sizes = {n: w[n].shape[0] for n in SMALL_NAMES}
    packed = _pack_small(gl, d_bias, d_conv_w, loss_tile)
    red = _unpack_small(allreduce_small(packed), small_sizes, HALO * C // LANES)
    loss = red["loss"]
    my_chip = 2 * lax.axis_index("x") + lax.axis_index("y")
    g_conv_w = lax.dynamic_slice_in_dim(red["conv_w"].reshape(HALO, C)[:CONV_K], my_chip * cs, cs, axis=1)
    update("conv_w", [g_conv_w])
    vec_names = SMALL_NAMES + ("fgate_bias",)
    stack = lambda d: jnp.concatenate(
        [jnp.pad(d[n], (0, (-d[n].shape[0]) % LANES)).reshape(-1, LANES) for n in vec_names], axis=0)
    g_stack = jnp.concatenate([red[n].reshape(-1, LANES) for n in SMALL_NAMES] + [red["fgate_bias"][None, :]],
                              axis=0)
    outs = adamw(stack(w), stack(m), stack(v), [g_stack])
    r = 0
    for n in vec_names:
        size = w[n].shape[0]
        k = -(-size // LANES)
        for dst, src in zip((grad, delta, new_m, new_v), outs):
            dst[n] = src[r:r + k].reshape(-1)[:size]
        r += k

    return (loss, grad_x[None], *[grad[n] for n in names], *[delta[n] for n in names],
            *[new_m[n] for n in names], *[new_v[n] for n in names])
```
